```python
import jax, jax.numpy as jnp
from jax import lax
import numpy as np

D_MODEL = 1024
BATCH = 32
SEQ = 256
DEPTH = 1
DEC_BATCH = 2
DEC_SEQ = 1024
PAST_LEN = 512

GRID_W = 64
MLA_HEADS = 8
MLA_NOPE = 64
MLA_ROPE = 32
MLA_V = 64
Q_LORA = 768
KV_LORA = 256
ROPE_BASE = 10000.0
Q_BLOCK = 128
HG_HEADS = 4
HG_DK = 128
HG_DV = 128
HG_CHUNK = 32
N_EXPERTS = 16
EC_FACTOR = 2
D_EXPERT = 1024
ALPHA = (2 * DEPTH) ** 0.25
BETA = (8 * DEPTH) ** -0.25
EPS = 1e-6
IN_WIDTHS = (Q_LORA, KV_LORA, MLA_ROPE, HG_HEADS * HG_DK, HG_HEADS * HG_DK, HG_HEADS * HG_DK,
             HG_HEADS * HG_DV, HG_HEADS * HG_DV, D_MODEL, D_MODEL)
IN_W = sum(IN_WIDTHS)

kernel_name = 'hybrid_mla_hgrn2_ec_flow_step'


def rms_norm(x, g):
    xf = x.astype(jnp.float32)
    y = xf * lax.rsqrt(jnp.mean(xf * xf, axis=-1, keepdims=True) + EPS)
    return (y * g.astype(jnp.float32)).astype(x.dtype)


def layer_norm(x, g, b):
    xf = x.astype(jnp.float32)
    xc = xf - jnp.mean(xf, axis=-1, keepdims=True)
    var = jnp.mean(xc * xc, axis=-1, keepdims=True)
    y = xc * lax.rsqrt(var + EPS) * g.astype(jnp.float32) + b.astype(jnp.float32)
    return y.astype(x.dtype)


def adaln(cond, w_ada, b_ada):
    m = jax.nn.silu(cond) @ w_ada + b_ada
    return tuple(a[:, None, :] for a in jnp.split(m, 6, axis=-1))


def axial_rope(x, rows):
    n_freq = MLA_ROPE // 4
    inv = ROPE_BASE ** (-jnp.arange(n_freq, dtype=jnp.float32) / n_freq)
    t = jnp.arange(rows * GRID_W)
    r = (t // GRID_W).astype(jnp.float32)
    col = (t % GRID_W).astype(jnp.float32)
    ang = jnp.concatenate([r[:, None] * inv, col[:, None] * inv], axis=-1)
    cos = jnp.cos(ang)[None, :, None, :]
    sin = jnp.sin(ang)[None, :, None, :]
    xf = x.astype(jnp.float32)
    x1, x2 = xf[..., 0::2], xf[..., 1::2]
    out = jnp.stack([x1 * cos - x2 * sin, x1 * sin + x2 * cos], axis=-1).reshape(x.shape)
    return out.astype(x.dtype)


def forget_gate(z, lb):
    B, T, _ = z.shape
    f = lb + (1.0 - lb) * jax.nn.sigmoid(z.astype(jnp.float32))
    f = f.reshape(B, T, HG_HEADS, HG_DK)
    return jnp.log(f), 1.0 - f


def hgrn_chunk_scan(q, k, v, logf, s0):
    B, T, H, _ = q.shape
    n = T // HG_CHUNK

    def chunks(a):
        return a.reshape(B, n, HG_CHUNK, H, a.shape[-1]).astype(jnp.float32)

    qc, kc, vc, lf = chunks(q), chunks(k), chunks(v), chunks(logf)
    b = jnp.cumsum(lf, axis=2)
    b_last = b[:, :, -1]
    q_dec = qc * jnp.exp(b)
    k_dec = kc * jnp.exp(-b)
    causal = jnp.tril(jnp.ones((HG_CHUNK, HG_CHUNK), dtype=bool))
    a = jnp.where(causal, jnp.einsum('bnthk,bnshk->bnhts', q_dec, k_dec), 0.0)
    o_intra = jnp.einsum('bnhts,bnshv->bnthv', a, vc)
    k_end = kc * jnp.exp(b_last[:, :, None] - b)
    u = jnp.einsum('bnshk,bnshv->bnhkv', k_end, vc)

    def step(s, inp):
        decay, u_n = inp
        return jnp.exp(decay)[..., None] * s + u_n, s

    s_final, s_prev = lax.scan(step, s0.astype(jnp.float32), (b_last.swapaxes(0, 1), u.swapaxes(0, 1)))
    o_inter = jnp.einsum('bnthk,bnhkv->bnthv', q_dec, s_prev.swapaxes(0, 1))
    o = (o_intra + o_inter).reshape(B, T, H, v.shape[-1])
    return o.astype(q.dtype), s_final


def mla_up(c_kv, w_ukv):
    B, S, _ = c_kv.shape
    kv = (c_kv @ w_ukv).reshape(B, S, MLA_HEADS, MLA_NOPE + MLA_V)
    return kv[..., :MLA_NOPE], kv[..., MLA_NOPE:]


def mla_attend(q_nope, q_pe, k_nope, k_pe, v):
    B, T, H, _ = q_nope.shape
    nb = T // Q_BLOCK
    scale = (MLA_NOPE + MLA_ROPE) ** -0.5

    def blocks(a):
        return a.reshape(B, nb, Q_BLOCK, H, a.shape[-1]).swapaxes(0, 1)

    def one(qs):
        qn, qp = qs
        s = jnp.einsum('bqhd,bkhd->bhqk', qn, k_nope) + jnp.einsum('bqhr,bkr->bhqk', qp, k_pe)
        p = jax.nn.softmax(s.astype(jnp.float32) * scale, axis=-1).astype(v.dtype)
        return jnp.einsum('bhqk,bkhd->bqhd', p, v)

    o = lax.map(one, (blocks(q_nope), blocks(q_pe)))
    return o.swapaxes(0, 1).reshape(B, T, H * MLA_V)


def token_mixer(h, rows, ctx, lb, w_in, q_norm, w_uq, kv_norm, w_ukv, w_o_mla, hg_norm, w_o_hg, w_out):
    B, T, _ = h.shape
    offs = [int(o) for o in np.cumsum(IN_WIDTHS)[:-1]]
    cq, ckv_raw, kpe, hq, hf_fwd, hf_bwd, hi, hg, ga, gb = jnp.split(h @ w_in, offs, axis=-1)
    q = (rms_norm(cq, q_norm) @ w_uq).reshape(B, T, MLA_HEADS, MLA_NOPE + MLA_ROPE)
    q_nope, q_pe = q[..., :MLA_NOPE], q[..., MLA_NOPE:]
    c_kv = rms_norm(ckv_raw, kv_norm)
    k_nope, v = mla_up(c_kv, w_ukv)
    k_pe = kpe
    if rows is not None:
        q_pe = axial_rope(q_pe, rows)
        k_pe = axial_rope(kpe[:, :, None, :], rows)[:, :, 0, :]
    hg_q = jax.nn.silu(hq).reshape(B, T, HG_HEADS, HG_DK)
    hg_i = hi.reshape(B, T, HG_HEADS, HG_DV)
    logf_f, k_f = forget_gate(hf_fwd, lb[0])
    logf_b, k_b = forget_gate(hf_bwd, lb[1])
    if ctx is None:
        s_f0 = jnp.zeros((B, HG_HEADS, HG_DK, HG_DV), jnp.float32)
        s_b0 = s_f0
        keys_nope, keys_pe, vals = k_nope, k_pe, v
    else:
        ckv_c, kpe_c, s_f0, s_b0 = ctx
        kn_c, v_c = mla_up(ckv_c, w_ukv)
        keys_nope = jnp.concatenate([k_nope, kn_c], axis=1)
        keys_pe = jnp.concatenate([k_pe, kpe_c], axis=1)
        vals = jnp.concatenate([v, v_c], axis=1)
    o_mla = mla_attend(q_nope, q_pe, keys_nope, keys_pe, vals)
    o_f, s_f = hgrn_chunk_scan(hg_q, k_f, hg_i, logf_f, s_f0)
    o_b, s_b = hgrn_chunk_scan(jnp.flip(hg_q, axis=1), jnp.flip(k_b, axis=1), jnp.flip(hg_i, axis=1),
                               jnp.flip(logf_b, axis=1), s_b0)
    o_hg = rms_norm(o_f + jnp.flip(o_b, axis=1), hg_norm) * jax.nn.silu(hg.reshape(B, T, HG_HEADS, HG_DV))
    merged = (jax.nn.sigmoid(ga) * (o_mla @ w_o_mla)
              + jax.nn.sigmoid(gb) * (o_hg.reshape(B, T, HG_HEADS * HG_DV) @ w_o_hg))
    return merged @ w_out, (c_kv, k_pe, s_f.astype(h.dtype), s_b.astype(h.dtype))


def ec_ffn(h, w_router, w1, w3, w2):
    B, T, D = h.shape
    n = B * T
    cap = EC_FACTOR * n // N_EXPERTS
    xs = h.reshape(n, D)
    aff = jax.nn.softmax((xs @ w_router).astype(jnp.float32), axis=-1)
    gate, idx = lax.top_k(aff.T, cap)
    xe = xs[idx]
    hid = jax.nn.silu(jnp.einsum('ecd,edf->ecf', xe, w1)) * jnp.einsum('ecd,edf->ecf', xe, w3)
    ye = jnp.einsum('ecf,efd->ecd', hid, w2) * gate[..., None].astype(h.dtype)
    out = jnp.zeros_like(xs).at[idx.reshape(-1)].add(ye.reshape(-1, D))
    return out.reshape(B, T, D)


def trunk_layer(x, cond, rows, ctx, lb, w_ada, b_ada, w_in, q_norm, w_uq, kv_norm, w_ukv, w_o_mla,
                hg_norm, w_o_hg, w_out, ln1_g, ln1_b, w_router, w1, w3, w2, ln2_g, ln2_b):
    sh1, sc1, g1, sh2, sc2, g2 = adaln(cond, w_ada, b_ada)
    mix, ctx_out = token_mixer(x * (1.0 + sc1) + sh1, rows, ctx, lb, w_in, q_norm, w_uq, kv_norm, w_ukv,
                               w_o_mla, hg_norm, w_o_hg, w_out)
    x = layer_norm(ALPHA * x + g1 * mix, ln1_g, ln1_b)
    ffn = ec_ffn(x * (1.0 + sc2) + sh2, w_router, w1, w3, w2)
    x = layer_norm(ALPHA * x + g2 * ffn, ln2_g, ln2_b)
    return x, ctx_out


def setup_inputs(seed: int = 0) -> dict:
    key = jax.random.key(seed)
    ks = jax.random.split(key, 32)

    def nrm(k, shape, scale):
        return jax.random.normal(k, shape, jnp.float32) * scale

    hqk = MLA_HEADS * (MLA_NOPE + MLA_ROPE)
    hkv = MLA_HEADS * (MLA_NOPE + MLA_V)
    return {
        'x_prompt': nrm(ks[0], (BATCH, SEQ, D_MODEL), 1.0),
        'x_sample': nrm(ks[1], (DEC_BATCH, DEC_SEQ, D_MODEL), 1.0),
        'c': nrm(ks[2], (DEC_BATCH, D_MODEL), 1.0),
        'cache_ckv': nrm(ks[3], (DEC_BATCH, DEPTH, PAST_LEN, KV_LORA), 1.0),
        'cache_kpe': nrm(ks[4], (DEC_BATCH, DEPTH, PAST_LEN, MLA_ROPE), 1.0),
        'state_hgrn': nrm(ks[5], (DEC_BATCH, DEPTH, 2, HG_HEADS, HG_DK, HG_DV), 0.5),
        'c_ctx': nrm(ks[6], (D_MODEL,), 1.0),
        'w_ada': nrm(ks[7], (DEPTH, D_MODEL, 6 * D_MODEL), 0.5 * D_MODEL ** -0.5),
        'b_ada': nrm(ks[8], (DEPTH, 6 * D_MODEL), 0.01),
        'w_in': nrm(ks[9], (DEPTH, D_MODEL, IN_W), D_MODEL ** -0.5),
        'mla_q_norm': 1.0 + nrm(ks[10], (DEPTH, Q_LORA), 0.01),
        'mla_w_uq': nrm(ks[11], (DEPTH, Q_LORA, hqk), Q_LORA ** -0.5),
        'mla_kv_norm': 1.0 + nrm(ks[12], (DEPTH, KV_LORA), 0.01),
        'mla_w_ukv': nrm(ks[13], (DEPTH, KV_LORA, hkv), KV_LORA ** -0.5),
        'mla_w_o': nrm(ks[14], (DEPTH, MLA_HEADS * MLA_V, D_MODEL), (MLA_HEADS * MLA_V) ** -0.5),
        'hgrn_gamma': nrm(ks[15], (2, DEPTH + 1, HG_HEADS * HG_DK), 0.1),
        'hgrn_norm': 1.0 + nrm(ks[16], (DEPTH, HG_DV), 0.01),
        'hgrn_w_o': nrm(ks[17], (DEPTH, HG_HEADS * HG_DV, D_MODEL), (HG_HEADS * HG_DV) ** -0.5),
        'w_out': nrm(ks[18], (DEPTH, D_MODEL, D_MODEL), BETA * D_MODEL ** -0.5),
        'ln1_g': 1.0 + nrm(ks[19], (DEPTH, D_MODEL), 0.01),
        'ln1_b': nrm(ks[20], (DEPTH, D_MODEL), 0.01),
        'moe_w_router': nrm(ks[21], (DEPTH, D_MODEL, N_EXPERTS), D_MODEL ** -0.5),
        'moe_w1': nrm(ks[22], (DEPTH, N_EXPERTS, D_MODEL, D_EXPERT), D_MODEL ** -0.5),
        'moe_w3': nrm(ks[23], (DEPTH, N_EXPERTS, D_MODEL, D_EXPERT), D_MODEL ** -0.5),
        'moe_w2': nrm(ks[24], (DEPTH, N_EXPERTS, D_EXPERT, D_MODEL), BETA * D_EXPERT ** -0.5),
        'ln2_g': 1.0 + nrm(ks[25], (DEPTH, D_MODEL), 0.01),
        'ln2_b': nrm(ks[26], (DEPTH, D_MODEL), 0.01),
    }


def reference(x_prompt, x_sample, c, cache_ckv, cache_kpe, state_hgrn, c_ctx, w_ada, b_ada, w_in,
              mla_q_norm, mla_w_uq, mla_kv_norm, mla_w_ukv, mla_w_o, hgrn_gamma, hgrn_norm, hgrn_w_o,
              w_out, ln1_g, ln1_b, moe_w_router, moe_w1, moe_w3, moe_w2, ln2_g, ln2_b):
    lb_all = jnp.cumsum(jax.nn.softmax(hgrn_gamma.astype(jnp.float32), axis=1), axis=1)
    rows = x_sample.shape[1] // GRID_W
    y_prompt = x_prompt
    y_sample = x_sample
    ckv_list, kpe_list, st_list = [], [], []
    for l in range(DEPTH):
        weights = (w_ada[l], b_ada[l], w_in[l], mla_q_norm[l], mla_w_uq[l], mla_kv_norm[l], mla_w_ukv[l],
                   mla_w_o[l], hgrn_norm[l], hgrn_w_o[l], w_out[l], ln1_g[l], ln1_b[l], moe_w_router[l],
                   moe_w1[l], moe_w3[l], moe_w2[l], ln2_g[l], ln2_b[l])
        lb = lb_all[:, l]
        y_prompt, (ckv_p, kpe_p, sf_p, sb_p) = trunk_layer(y_prompt, c_ctx[None], None, None, lb, *weights)
        ckv_list.append(ckv_p)
        kpe_list.append(kpe_p)
        st_list.append(jnp.stack([sf_p, sb_p], axis=1))
        ctx = (cache_ckv[:, l], cache_kpe[:, l], state_hgrn[:, l, 0], state_hgrn[:, l, 1])
        y_sample, _ = trunk_layer(y_sample, c, rows, ctx, lb, *weights)
    new_ckv = jnp.stack(ckv_list, axis=1)
    new_kpe = jnp.stack(kpe_list, axis=1)
    new_state_hgrn = jnp.stack(st_list, axis=1)
    return (y_prompt, y_sample, new_ckv, new_kpe, new_state_hgrn)
```

```python
import functools

import jax
import jax.numpy as jnp
from jax import lax
from jax.experimental import pallas as pl
from jax.experimental.pallas import tpu as pltpu

F32 = jnp.float32
BF16 = jnp.bfloat16

MLA_HEADS = 8
MLA_NOPE = 64
MLA_ROPE = 32
MLA_V = 64
HG_HEADS = 4
HG_DK = 128
HG_DV = 128
HG_CHUNK = 32
GRID_W = 64
ROPE_BASE = 10000.0
EC_FACTOR = 2
EPS = 1e-6

LANES = 128
SUBLANES = 8
VMEM_LIMIT = 56 * 1024 * 1024

TOK_BLOCK = 256
WIN_ROWS = 64
UNSELECTED = -(1 << 30)

NT_DIMS = (((1,), (1,)), ((), ()))


def _dot(a, b):
    return jnp.dot(a, b, preferred_element_type=F32)


def _dot_nt(a, b):
    return lax.dot_general(a, b, NT_DIMS, preferred_element_type=F32)


def _silu(x):
    return x * jax.nn.sigmoid(x)


def _params(*sem):
    return pltpu.CompilerParams(dimension_semantics=sem, vmem_limit_bytes=VMEM_LIMIT)


def _const_spec(shape):
    zeros = (0,) * len(shape)
    return pl.BlockSpec(shape, lambda *_: zeros, pipeline_mode=pl.Buffered(1))


def _adaln_kernel(c_ref, w_ref, b_ref, o_ref):
    s = _silu(c_ref[...]).astype(BF16)
    o_ref[...] = _dot(s, w_ref[...].astype(BF16)) + b_ref[...]


def _adaln(cond, w_ada, b_ada):
    rows, d = cond.shape
    n = w_ada.shape[1]
    tn = n // 4
    return pl.pallas_call(
        _adaln_kernel,
        out_shape=jax.ShapeDtypeStruct((rows, n), F32),
        grid=(n // tn,),
        in_specs=[_const_spec((rows, d)),
                  pl.BlockSpec((d, tn), lambda j: (0, j)),
                  pl.BlockSpec((1, tn), lambda j: (0, j))],
        out_specs=pl.BlockSpec((rows, tn), lambda j: (0, j)),
        compiler_params=_params("arbitrary"),
        name="adaln",
    )(cond, w_ada, b_ada.reshape(1, n))


def _rms(x, g):
    return x * lax.rsqrt(jnp.mean(x * x, axis=-1, keepdims=True) + EPS) * g


def _rope(x, c, s):
    w = x.shape[-1]
    lane = lax.broadcasted_iota(jnp.int32, x.shape, 1)
    nxt = pltpu.roll(x, w - 1, 1)
    prv = pltpu.roll(x, 1, 1)
    return x * c + jnp.where(lane % 2 == 0, nxt, prv) * s


def _inproj_kernel(*refs, d_model, row0, per_batch, rope):
    (x_ref, mod_ref, gam_ref, wq_ref, wckv_ref, wkpe_ref, wh_ref, wg_ref, qn_ref, wuq_ref,
     kvn_ref, wk_ref, wv_ref, wpe_ref) = refs[:14]
    refs = refs[14:]
    if rope:
        cq_ref, sq_ref, ck_ref, sk_ref = refs[:4]
        refs = refs[4:]
    q_o, k_o, v_o, ckv_o, kpe_o, hgx_o, gate_o = refs
    d = d_model
    r = row0 + pl.program_id(0) if per_batch else row0
    m = mod_ref[pl.ds(r, 1), :]
    h = (x_ref[...] * (1.0 + m[:, d:2 * d]) + m[:, 0:d]).astype(BF16)

    cq = _rms(_dot(h, wq_ref[...]), qn_ref[...])
    q = _dot(cq.astype(BF16), wuq_ref[...])
    if rope:
        q = _rope(q, cq_ref[...], sq_ref[...])
    q_o[...] = q.astype(BF16)

    ckv = _rms(_dot(h, wckv_ref[...]), kvn_ref[...])
    ckv_o[...] = ckv
    kpe = _dot(h, wkpe_ref[...])
    if rope:
        kpe = _rope(kpe, ck_ref[...], sk_ref[...])
    kpe_o[...] = kpe
    cb = ckv.astype(BF16)
    k_o[...] = (_dot(cb, wk_ref[...]) + _dot(kpe.astype(BF16), wpe_ref[...])).astype(BF16)
    v_o[...] = _dot(cb, wv_ref[...]).astype(BF16)

    hw = HG_HEADS * HG_DK
    z = _dot(h, wh_ref[...])
    hgx_o[:, 0:hw] = _silu(z[:, 0:hw])
    for dr in range(2):
        g0, g1 = gam_ref[dr, 0:1, :], gam_ref[dr, 1:2, :]
        gmax = jnp.maximum(g0, g1)
        e0, e1 = jnp.exp(g0 - gmax), jnp.exp(g1 - gmax)
        lb = e0 / (e0 + e1)
        f = lb + (1.0 - lb) * jax.nn.sigmoid(z[:, (1 + dr) * hw:(2 + dr) * hw])
        hgx_o[:, (1 + 2 * dr) * hw:(2 + 2 * dr) * hw] = jnp.log(f)
        hgx_o[:, (2 + 2 * dr) * hw:(3 + 2 * dr) * hw] = 1.0 - f
    hgx_o[:, 5 * hw:6 * hw] = z[:, 3 * hw:4 * hw]
    hgx_o[:, 6 * hw:7 * hw] = z[:, 4 * hw:5 * hw]

    gate_o[...] = _dot(h, wg_ref[...])


def _inproj(x2d, batch, seq, mod, gamma, wts, row0, per_batch, rope_tabs):
    n, d = x2d.shape
    tm = TOK_BLOCK
    nblk = seq // tm
    rope = rope_tabs is not None
    hp = MLA_HEADS * LANES
    hw = HG_HEADS * HG_DK
    tok = lambda b, i: (b * nblk + i, 0)
    pos = lambda b, i: (i, 0)
    weights = [wts[k] for k in ("wq", "wckv", "wkpe", "wh", "wg", "qn", "wuq", "kvn", "wk", "wv", "wpe")]
    ins = [x2d, mod, gamma] + weights
    in_specs = ([pl.BlockSpec((tm, d), tok), _const_spec(mod.shape), _const_spec(gamma.shape)]
                + [_const_spec(w.shape) for w in weights])
    if rope:
        ins += list(rope_tabs)
        in_specs += [pl.BlockSpec((tm, t.shape[1]), pos) for t in rope_tabs]
    widths = [(hp, BF16), (hp, BF16), (hp, BF16), (wts["wckv"].shape[1], F32), (LANES, F32),
              (7 * hw, F32), (2 * d, F32)]
    return pl.pallas_call(
        functools.partial(_inproj_kernel, d_model=d, row0=row0, per_batch=per_batch, rope=rope),
        out_shape=[jax.ShapeDtypeStruct((n, w), dt) for w, dt in widths],
        grid=(batch, nblk),
        in_specs=in_specs,
        out_specs=[pl.BlockSpec((tm, w), tok) for w, _ in widths],
        compiler_params=_params("arbitrary", "arbitrary"),
        name="inproj",
    )(*ins)


def _kvup_kernel(ckv_ref, kpe_ref, wk_ref, wv_ref, wpe_ref, k_o, v_o):
    cb = ckv_ref[...].astype(BF16)
    k_o[...] = (_dot(cb, wk_ref[...]) + _dot(kpe_ref[...].astype(BF16), wpe_ref[...])).astype(BF16)
    v_o[...] = _dot(cb, wv_ref[...]).astype(BF16)


def _kvup(ckv2d, kpe2d, wts):
    n = ckv2d.shape[0]
    tm = TOK_BLOCK
    hp = MLA_HEADS * LANES
    row = lambda i: (i, 0)
    ws = [wts["wk"], wts["wv"], wts["wpe"]]
    return pl.pallas_call(
        _kvup_kernel,
        out_shape=[jax.ShapeDtypeStruct((n, hp), BF16)] * 2,
        grid=(n // tm,),
        in_specs=[pl.BlockSpec((tm, ckv2d.shape[1]), row), pl.BlockSpec((tm, LANES), row)]
                 + [_const_spec(w.shape) for w in ws],
        out_specs=[pl.BlockSpec((tm, hp), row)] * 2,
        compiler_params=_params("arbitrary"),
        name="kvup",
    )(ckv2d, kpe2d, *ws)


def _attn_kernel(*refs, scale, cached):
    if cached:
        q_ref, k_ref, v_ref, kc_ref, vc_ref, o_ref = refs
    else:
        q_ref, k_ref, v_ref, o_ref = refs
    for hd in range(MLA_HEADS):
        sl = slice(hd * LANES, (hd + 1) * LANES)
        q = q_ref[:, sl]
        s = _dot_nt(q, k_ref[:, sl]) * scale
        mx = jnp.max(s, axis=-1, keepdims=True)
        if cached:
            s2 = _dot_nt(q, kc_ref[:, sl]) * scale
            mx = jnp.maximum(mx, jnp.max(s2, axis=-1, keepdims=True))
        e = jnp.exp(s - mx)
        den = jnp.sum(e, axis=-1, keepdims=True)
        o = _dot(e.astype(BF16), v_ref[:, sl])
        if cached:
            e2 = jnp.exp(s2 - mx)
            den = den + jnp.sum(e2, axis=-1, keepdims=True)
            o = o + _dot(e2.astype(BF16), vc_ref[:, sl])
        o_ref[:, sl] = (o / den).astype(o_ref.dtype)


def _attn(q, k, v, batch, seq, cache=None):
    n, hp = q.shape
    tq = TOK_BLOCK
    nblk = seq // tq
    ins = [q, k, v]
    in_specs = [pl.BlockSpec((tq, hp), lambda b, i: (b * nblk + i, 0)),
                pl.BlockSpec((seq, hp), lambda b, i: (b, 0)),
                pl.BlockSpec((seq, hp), lambda b, i: (b, 0))]
    if cache is not None:
        past = cache[0].shape[0] // batch
        ins += list(cache)
        in_specs += [pl.BlockSpec((past, hp), lambda b, i: (b, 0))] * 2
    return pl.pallas_call(
        functools.partial(_attn_kernel, scale=(MLA_NOPE + MLA_ROPE) ** -0.5, cached=cache is not None),
        out_shape=jax.ShapeDtypeStruct((n, hp), BF16),
        grid=(batch, nblk),
        in_specs=in_specs,
        out_specs=pl.BlockSpec((tq, hp), lambda b, i: (b * nblk + i, 0)),
        compiler_params=_params("arbitrary", "arbitrary"),
        name="attn",
    )(*ins)


def _split3(x):
    h1 = x.astype(BF16)
    r1 = x - h1.astype(F32)
    h2 = r1.astype(BF16)
    h3 = (r1 - h2.astype(F32)).astype(BF16)
    return h1, h2, h3


def _hgrn_kernel(*refs, has_init):
    fwd, bwd = refs[0:4], refs[4:8]
    refs = refs[8:]
    if has_init:
        s0_ref = refs[0]
        refs = refs[1:]
    of_ref, ob_ref, sfin_ref, st_scr = refs
    i = pl.program_id(1)
    last = pl.num_programs(1) - 1
    tm = fwd[0].shape[0]
    c = HG_CHUNK
    nch = tm // c
    dk, dv = HG_DK, HG_DV

    @pl.when(i == 0)
    def _init():
        for dr in range(2):
            for hd in range(HG_HEADS):
                st_scr[dr, hd] = s0_ref[0, dr, hd].T if has_init else jnp.zeros((dv, dk), F32)

    row = lax.broadcasted_iota(jnp.int32, (tm, tm), 0)
    col = lax.broadcasted_iota(jnp.int32, (tm, tm), 1)
    same = (row // c) == (col // c)
    ones_b = jnp.where(same, 1.0, 0.0).astype(BF16)
    bd = (lax.broadcasted_iota(jnp.int32, (tm, nch * dk), 0) // c
          == lax.broadcasted_iota(jnp.int32, (tm, nch * dk), 1) // dk)

    for dr, (hq_ref, lf_ref, kk_ref, vv_ref) in enumerate((fwd, bwd)):
        o_ref = of_ref if dr == 0 else ob_ref
        tri = same & ((col <= row) if dr == 0 else (col >= row))
        lhs = jnp.concatenate([jnp.where(tri, 1.0, 0.0).astype(BF16), ones_b], axis=0)
        h1, h2, h3 = _split3(lf_ref[...])
        sums = _dot(lhs, h1) + _dot(lhs, h2) + _dot(lhs, h3)
        bcum, btot = sums[:tm], sums[tm:]
        kk = kk_ref[...]
        qd = hq_ref[...] * jnp.exp(bcum)
        kd = kk * jnp.exp(-bcum)
        ke = kk * jnp.exp(btot - bcum)
        dec = jnp.exp(btot)
        vv = vv_ref[...]
        order = range(nch) if dr == 0 else range(nch - 1, -1, -1)
        for hd in range(HG_HEADS):
            sl = slice(hd * dk, (hd + 1) * dk)
            qd_h = qd[:, sl]
            v_h = vv[:, hd * dv:(hd + 1) * dv]
            a = jnp.where(tri, _dot_nt(qd_h.astype(BF16), kd[:, sl].astype(BF16)), 0.0)
            o_intra = _dot(a.astype(BF16), v_h.astype(BF16))
            kebd = jnp.where(bd, jnp.concatenate([ke[:, sl]] * nch, axis=1), 0.0).astype(BF16)
            qbd = jnp.where(bd, jnp.concatenate([qd_h] * nch, axis=1), 0.0).astype(BF16)
            ut = _dot(v_h.T.astype(BF16), kebd)
            st = st_scr[dr, hd]
            prev = [None] * nch
            for n in order:
                prev[n] = st
                st = st * dec[n * c:n * c + 1, sl] + ut[:, n * dk:(n + 1) * dk]
            st_scr[dr, hd] = st
            o_inter = _dot_nt(qbd, jnp.concatenate(prev, axis=1).astype(BF16))
            o_ref[:, hd * dv:(hd + 1) * dv] = o_intra + o_inter

            @pl.when(i == last)
            def _final(st=st, dr=dr, hd=hd):
                sfin_ref[0, dr, hd] = st.T


def _hgrn(hgx, batch, seq, s0=None):
    n = hgx.shape[0]
    tm = TOK_BLOCK
    nblk = seq // tm
    hw = HG_HEADS * HG_DK

    def spec(lane_blk, rev):
        if rev:
            return pl.BlockSpec((tm, hw), lambda b, i: (b * nblk + nblk - 1 - i, lane_blk))
        return pl.BlockSpec((tm, hw), lambda b, i: (b * nblk + i, lane_blk))

    in_specs = [spec(0, False), spec(1, False), spec(2, False), spec(5, False),
                spec(0, True), spec(3, True), spec(4, True), spec(5, True)]
    ins = [hgx] * 8
    st_shape = (1, 2, HG_HEADS, HG_DK, HG_DV)
    st_spec = pl.BlockSpec(st_shape, lambda b, i: (b, 0, 0, 0, 0))
    if s0 is not None:
        ins.append(s0)
        in_specs.append(st_spec)
    return pl.pallas_call(
        functools.partial(_hgrn_kernel, has_init=s0 is not None),
        out_shape=[jax.ShapeDtypeStruct((n, hw), F32), jax.ShapeDtypeStruct((n, hw), F32),
                   jax.ShapeDtypeStruct((batch,) + st_shape[1:], F32)],
        grid=(batch, nblk),
        in_specs=in_specs,
        out_specs=[spec(0, False), spec(0, True), st_spec],
        scratch_shapes=[pltpu.VMEM((2, HG_HEADS, HG_DV, HG_DK), F32)],
        compiler_params=_params("arbitrary", "arbitrary"),
        name="hgrn",
    )(*ins)


def _layer_norm(x, g, b):
    xc = x - jnp.mean(x, axis=-1, keepdims=True)
    var = jnp.mean(xc * xc, axis=-1, keepdims=True)
    return xc * lax.rsqrt(var + EPS) * g + b


def _postmix_kernel(x_ref, mod_ref, of_ref, ob_ref, zg_ref, om_ref, gate_ref, hgn_ref, womla_ref,
                    wohg_ref, wout_ref, lng_ref, lnb_ref, wr_ref, x1_o, h2_o, aff_o,
                    *, d_model, alpha, n_experts, row0, per_batch):
    d = d_model
    r = row0 + pl.program_id(0) if per_batch else row0
    m = mod_ref[pl.ds(r, 1), :]
    g1, sh2, sc2 = m[:, 2 * d:3 * d], m[:, 3 * d:4 * d], m[:, 4 * d:5 * d]
    o = of_ref[...] + ob_ref[...]
    zg = zg_ref[...]
    parts = []
    for hd in range(HG_HEADS):
        sl = slice(hd * HG_DV, (hd + 1) * HG_DV)
        parts.append(_rms(o[:, sl], hgn_ref[...]) * _silu(zg[:, sl]))
    ohg = jnp.concatenate(parts, axis=1).astype(BF16)
    gates = gate_ref[...]
    merged = (jax.nn.sigmoid(gates[:, 0:d]) * _dot(om_ref[...], womla_ref[...])
              + jax.nn.sigmoid(gates[:, d:2 * d]) * _dot(ohg, wohg_ref[...]))
    mix = _dot(merged.astype(BF16), wout_ref[...])
    x1 = _layer_norm(alpha * x_ref[...] + g1 * mix, lng_ref[...], lnb_ref[...])
    x1_o[...] = x1
    h2 = (x1 * (1.0 + sc2) + sh2).astype(BF16)
    h2_o[...] = h2
    logits = _dot_nt(wr_ref[...], h2)
    e = jnp.exp(logits - jnp.max(logits, axis=0, keepdims=True))
    aff_o[...] = e / jnp.sum(e, axis=0, keepdims=True)


def _postmix(x2d, batch, seq, mod, o_f, o_b, hgx, o_mla, gates, wts, alpha, n_experts, row0, per_batch):
    n, d = x2d.shape
    tm = TOK_BLOCK
    nblk = seq // tm
    hw = HG_HEADS * HG_DV
    tok = lambda b, i: (b * nblk + i, 0)
    weights = [wts[k] for k in ("hgn", "womla", "wohg", "wout", "ln1g", "ln1b", "wr")]
    return pl.pallas_call(
        functools.partial(_postmix_kernel, d_model=d, alpha=alpha, n_experts=n_experts, row0=row0,
                          per_batch=per_batch),
        out_shape=[jax.ShapeDtypeStruct((n, d), F32), jax.ShapeDtypeStruct((n, d), BF16),
                   jax.ShapeDtypeStruct((n_experts, n), F32)],
        grid=(batch, nblk),
        in_specs=[pl.BlockSpec((tm, d), tok), _const_spec(mod.shape),
                  pl.BlockSpec((tm, hw), tok), pl.BlockSpec((tm, hw), tok),
                  pl.BlockSpec((tm, hw), lambda b, i: (b * nblk + i, 6)),
                  pl.BlockSpec((tm, o_mla.shape[1]), tok), pl.BlockSpec((tm, 2 * d), tok)]
                 + [_const_spec(w.shape) for w in weights],
        out_specs=[pl.BlockSpec((tm, d), tok), pl.BlockSpec((tm, d), tok),
                   pl.BlockSpec((n_experts, tm), lambda b, i: (0, b * nblk + i))],
        compiler_params=_params("arbitrary", "arbitrary"),
        name="postmix",
    )(x2d, mod, o_f, o_b, hgx, o_mla, gates, *weights)


def _route_kernel(aff_ref, rank_o, cnt_o, *, cap, tb):
    a = aff_ref[...]
    ne, n = a.shape
    nb = n // tb
    key = pltpu.bitcast(a, jnp.int32)

    def bit_step(it, thr):
        cand = thr | jnp.left_shift(jnp.int32(1), 30 - it)
        cnt = jnp.sum(jnp.where(key >= cand, 1.0, 0.0), axis=1, keepdims=True)
        return jnp.where(cnt >= cap, cand, thr)

    thr = lax.fori_loop(0, 31, bit_step, jnp.zeros((ne, 1), jnp.int32))
    need = cap - jnp.sum(jnp.where(key > thr, 1.0, 0.0), axis=1, keepdims=True)
    before = (lax.broadcasted_iota(jnp.int32, (tb, tb), 0)
              < lax.broadcasted_iota(jnp.int32, (tb, tb), 1))
    before = jnp.where(before, 1.0, 0.0).astype(BF16)
    off_eq = jnp.zeros((ne, 1), F32)
    off_sel = jnp.zeros((ne, 1), F32)
    cnt_o[...] = jnp.zeros_like(cnt_o)
    for blk in range(nb):
        sl = slice(blk * tb, (blk + 1) * tb)
        key_b = key[:, sl]
        eq = key_b == thr
        eq_b = jnp.where(eq, 1.0, 0.0)
        eq_rank = _dot(eq_b.astype(BF16), before) + off_eq
        sel = (key_b > thr) | (eq & (eq_rank < need))
        sel_b = jnp.where(sel, 1.0, 0.0)
        rank = _dot(sel_b.astype(BF16), before) + off_sel
        rank_o[:, sl] = jnp.where(sel, rank.astype(jnp.int32), UNSELECTED)
        cnt_o[:, blk:blk + 1] = off_sel.astype(jnp.int32)
        off_eq = off_eq + jnp.sum(eq_b, axis=1, keepdims=True)
        off_sel = off_sel + jnp.sum(sel_b, axis=1, keepdims=True)
    cnt_o[:, nb:nb + 1] = off_sel.astype(jnp.int32)


def _route(aff_t, cap):
    ne, n = aff_t.shape
    nb = n // TOK_BLOCK
    assert nb + 1 <= LANES
    rank, cnt = pl.pallas_call(
        functools.partial(_route_kernel, cap=cap, tb=TOK_BLOCK),
        out_shape=[jax.ShapeDtypeStruct((ne, n), jnp.int32), jax.ShapeDtypeStruct((ne, LANES), jnp.int32)],
        in_specs=[pl.BlockSpec(memory_space=pltpu.VMEM)],
        out_specs=[pl.BlockSpec(memory_space=pltpu.VMEM)] * 2,
        compiler_params=pltpu.CompilerParams(vmem_limit_bytes=VMEM_LIMIT),
        name="route",
    )(aff_t)
    return rank, cnt[:, :nb + 1]


def _ffn_kernel(sched_ref, *refs, groups, ts):
    ng = len(groups)
    h2_refs, rk_refs, af_refs = refs[0:ng], refs[ng:2 * ng], refs[2 * ng:3 * ng]
    w1_ref, w3_ref, w2_ref, ye_ref, xe_scr, g_scr, acc_scr = refs[3 * ng:]
    e = pl.program_id(0)
    f = pl.program_id(1)
    tb = TOK_BLOCK
    ntile_all = sum(g["ntile"] for g in groups)

    @pl.when(f == 0)
    def _gather():
        slot_iota = lax.broadcasted_iota(jnp.int32, (ts, tb), 0)
        tile0 = 0
        for gi, g in enumerate(groups):
            h2_ref, rk_ref, af_ref = h2_refs[gi], rk_refs[gi], af_refs[gi]
            for j in range(g["ntile"]):
                base = (e * ntile_all + tile0 + j) * 2
                blo = sched_ref[base]
                bhi = sched_ref[base + 1]
                acc_scr[...] = jnp.zeros_like(acc_scr)

                def pair(b, gsum, h2_ref=h2_ref, rk_ref=rk_ref, af_ref=af_ref, j=j):
                    hit = (rk_ref[0, pl.ds(b, 1), :] - j * ts) == slot_iota
                    hb = h2_ref[pl.ds(pl.multiple_of(b * tb, tb), tb), :]
                    acc_scr[...] += _dot(jnp.where(hit, 1.0, 0.0).astype(BF16), hb)
                    return gsum + jnp.sum(jnp.where(hit, af_ref[0, pl.ds(b, 1), :], 0.0),
                                          axis=1, keepdims=True)

                gsum = lax.fori_loop(blo, bhi + 1, pair, jnp.zeros((ts, 1), F32))
                r0 = (tile0 + j) * ts
                xe_scr[r0:r0 + ts, :] = acc_scr[...].astype(BF16)
                g_scr[r0:r0 + ts, :] = gsum
            tile0 += g["ntile"]

    x = xe_scr[...]
    hid = _silu(_dot(x, w1_ref[0].astype(BF16))) * _dot(x, w3_ref[0].astype(BF16))
    y = _dot(hid.astype(BF16), w2_ref[0].astype(BF16))

    @pl.when(f == 0)
    def _first():
        ye_ref[0] = y

    @pl.when(f > 0)
    def _rest():
        ye_ref[0] += y

    @pl.when(f == pl.num_programs(1) - 1)
    def _gate():
        ye_ref[0] = ye_ref[0] * g_scr[...]


def _ffn(groups, w1, w3, w2, ts, ft):
    ne, d, dff = w1.shape
    nf = dff // ft
    meta = [dict(ntile=g["sched"].shape[1]) for g in groups]
    slots = sum(m["ntile"] for m in meta) * ts
    sched = jnp.concatenate([g["sched"] for g in groups], axis=1).reshape(-1)
    blk3 = lambda a: pl.BlockSpec((1,) + a.shape[1:], lambda e, f, s: (e, 0, 0))
    in_specs = ([pl.BlockSpec(memory_space=pltpu.VMEM) for _ in groups]
                + [blk3(g["rank"]) for g in groups] + [blk3(g["aff"]) for g in groups]
                + [pl.BlockSpec((1, d, ft), lambda e, f, s: (e, 0, f)),
                   pl.BlockSpec((1, d, ft), lambda e, f, s: (e, 0, f)),
                   pl.BlockSpec((1, ft, d), lambda e, f, s: (e, f, 0))])
    return pl.pallas_call(
        functools.partial(_ffn_kernel, groups=meta, ts=ts),
        out_shape=jax.ShapeDtypeStruct((ne, slots, d), F32),
        grid_spec=pltpu.PrefetchScalarGridSpec(
            num_scalar_prefetch=1,
            grid=(ne, nf),
            in_specs=in_specs,
            out_specs=pl.BlockSpec((1, slots, d), lambda e, f, s: (e, 0, 0)),
            scratch_shapes=[pltpu.VMEM((slots, d), BF16), pltpu.VMEM((slots, 1), F32),
                            pltpu.VMEM((ts, d), F32)]),
        compiler_params=_params("arbitrary", "arbitrary"),
        name="ffn",
    )(sched, *[g["h2"] for g in groups], *[g["rank"] for g in groups], *[g["aff"] for g in groups],
      w1, w3, w2)


def _combine_kernel(lower_ref, rounds_ref, rk_ref, x1_ref, mod_ref, lng_ref, lnb_ref, ye_hbm, out_ref,
                    buf, acc_scr, sem, *, d_model, alpha, slot0, slots, row0, blocks_per_batch):
    d = d_model
    b = pl.program_id(0)
    ne, tb = rk_ref.shape[1], rk_ref.shape[2]
    win = WIN_ROWS
    acc_scr[...] = jnp.zeros_like(acc_scr)
    eye = (lax.broadcasted_iota(jnp.int32, (tb, tb), 0)
           == lax.broadcasted_iota(jnp.int32, (tb, tb), 1))
    eye = jnp.where(eye, 1.0, 0.0).astype(BF16)
    win_iota = lax.broadcasted_iota(jnp.int32, (win, tb), 0)

    def window(e, start):
        return pltpu.make_async_copy(ye_hbm.at[e, pl.ds(pl.multiple_of(start, SUBLANES), win), :],
                                     buf.at[pl.ds(e * win, win), :], sem.at[e])

    def one_round(r, carry):
        lowers = [lower_ref[b * ne + e] + r * win for e in range(ne)]
        starts = [jnp.minimum(lo, slots - win) for lo in lowers]
        for e in range(ne):
            window(e, starts[e]).start()
        hits = []
        for e in range(ne):
            slot = rk_ref[0, e:e + 1, :] + slot0
            hits.append(((slot - starts[e]) == win_iota) & (slot >= lowers[e]))
        hit = jnp.where(jnp.concatenate(hits, axis=0), 1.0, 0.0).astype(BF16)
        hit_t = _dot_nt(eye, hit).astype(BF16)
        for e in range(ne):
            window(e, starts[e]).wait()
        y = buf[...]
        y_hi = y.astype(BF16)
        y_lo = (y - y_hi.astype(F32)).astype(BF16)
        acc_scr[...] += _dot(hit_t, y_hi) + _dot(hit_t, y_lo)
        return carry

    lax.fori_loop(0, rounds_ref[b], one_round, 0)
    r = row0 + b // blocks_per_batch
    g2 = mod_ref[pl.ds(r, 1), :][:, 5 * d:6 * d]
    out_ref[...] = _layer_norm(alpha * x1_ref[...] + g2 * acc_scr[...], lng_ref[...], lnb_ref[...])


def _combine(ye, rank_bt, lower, rounds, x1, mod, ln_g, ln_b, alpha, slot0, row0, blocks_per_batch):
    n, d = x1.shape
    nb, ne, tb = rank_bt.shape
    slots = ye.shape[1]
    return pl.pallas_call(
        functools.partial(_combine_kernel, d_model=d, alpha=alpha, slot0=slot0, slots=slots, row0=row0,
                          blocks_per_batch=blocks_per_batch),
        out_shape=jax.ShapeDtypeStruct((n, d), F32),
        grid_spec=pltpu.PrefetchScalarGridSpec(
            num_scalar_prefetch=2,
            grid=(nb,),
            in_specs=[pl.BlockSpec((1, ne, tb), lambda b, *_: (b, 0, 0)),
                      pl.BlockSpec((tb, d), lambda b, *_: (b, 0)),
                      pl.BlockSpec(mod.shape, lambda b, *_: (0, 0)),
                      pl.BlockSpec((1, d), lambda b, *_: (0, 0)),
                      pl.BlockSpec((1, d), lambda b, *_: (0, 0)),
                      pl.BlockSpec(memory_space=pl.ANY)],
            out_specs=pl.BlockSpec((tb, d), lambda b, *_: (b, 0)),
            scratch_shapes=[pltpu.VMEM((ne * WIN_ROWS, d), F32), pltpu.VMEM((tb, d), F32),
                            pltpu.SemaphoreType.DMA((ne,))]),
        compiler_params=_params("arbitrary"),
        name="combine",
    )(lower, rounds, rank_bt, x1, mod, ln_g, ln_b, ye)


def _prep_weights(w_in, q_norm, w_uq, kv_norm, w_ukv, w_o_mla, hgrn_norm, w_o_hg, w_out, ln1_g, ln1_b,
                  w_router):
    d = w_in.shape[0]
    q_lora, kv_lora = q_norm.shape[0], kv_norm.shape[0]
    hw = HG_HEADS * HG_DK
    hh, hp = MLA_HEADS, MLA_HEADS * LANES
    o_q, o_kv, o_pe = 0, q_lora, q_lora + kv_lora
    o_h = o_pe + MLA_ROPE
    o_g = o_h + 5 * hw
    assert w_in.shape[1] == o_g + 2 * d
    qk = MLA_NOPE + MLA_ROPE
    kvw = MLA_NOPE + MLA_V
    pad_lanes = lambda a, w: jnp.pad(a, ((0, 0), (0, w - a.shape[1])))
    wuq = jnp.pad(w_uq.reshape(q_lora, hh, qk), ((0, 0), (0, 0), (0, LANES - qk))).reshape(q_lora, hp)
    ukv = w_ukv.reshape(kv_lora, hh, kvw)
    wk = jnp.pad(ukv[:, :, :MLA_NOPE], ((0, 0), (0, 0), (0, LANES - MLA_NOPE))).reshape(kv_lora, hp)
    wv = jnp.pad(ukv[:, :, MLA_NOPE:], ((0, 0), (0, 0), (0, LANES - MLA_V))).reshape(kv_lora, hp)
    place = jnp.pad(jnp.eye(MLA_ROPE, dtype=F32), ((0, LANES - MLA_ROPE), (MLA_NOPE, LANES - qk)))
    wpe = jnp.tile(place, (1, hh))
    womla = jnp.pad(w_o_mla.reshape(hh, MLA_V, d), ((0, 0), (0, LANES - MLA_V), (0, 0))).reshape(hp, d)
    b16 = lambda a: a.astype(BF16)
    return dict(
        wq=b16(w_in[:, o_q:o_kv]), wckv=b16(w_in[:, o_kv:o_pe]),
        wkpe=b16(pad_lanes(w_in[:, o_pe:o_h], LANES)), wh=b16(w_in[:, o_h:o_g]), wg=b16(w_in[:, o_g:]),
        qn=q_norm.reshape(1, -1), wuq=b16(wuq), kvn=kv_norm.reshape(1, -1), wk=b16(wk), wv=b16(wv),
        wpe=b16(wpe), hgn=hgrn_norm.reshape(1, -1), womla=b16(womla), wohg=b16(w_o_hg), wout=b16(w_out),
        ln1g=ln1_g.reshape(1, -1), ln1b=ln1_b.reshape(1, -1), wr=b16(w_router.T))


def _rope_tables(seq):
    n_freq = MLA_ROPE // 4
    inv = ROPE_BASE ** (-jnp.arange(n_freq, dtype=F32) / n_freq)
    t = jnp.arange(seq)
    r = (t // GRID_W).astype(F32)
    col = (t % GRID_W).astype(F32)
    ang = jnp.concatenate([r[:, None] * inv, col[:, None] * inv], axis=-1)
    cos = jnp.repeat(jnp.cos(ang), 2, axis=1)
    sin = jnp.repeat(jnp.sin(ang), 2, axis=1) * jnp.tile(jnp.array([-1.0, 1.0], F32), MLA_ROPE // 2)
    ck = jnp.pad(cos, ((0, 0), (0, LANES - MLA_ROPE)), constant_values=1.0)
    sk = jnp.pad(sin, ((0, 0), (0, LANES - MLA_ROPE)))
    cq = jnp.pad(cos, ((0, 0), (MLA_NOPE, LANES - MLA_NOPE - MLA_ROPE)), constant_values=1.0)
    sq = jnp.pad(sin, ((0, 0), (MLA_NOPE, LANES - MLA_NOPE - MLA_ROPE)))
    return jnp.tile(cq, (1, MLA_HEADS)), jnp.tile(sq, (1, MLA_HEADS)), ck, sk


def _tile_sched(cnt, ntile, ts):
    nb = cnt.shape[1] - 1
    starts = jnp.arange(ntile, dtype=jnp.int32) * ts
    blo = jnp.sum(cnt[:, 1:, None] <= starts[None, None, :], axis=1)
    bhi = jnp.sum(cnt[:, :nb, None] < (starts + ts)[None, None, :], axis=1) - 1
    return jnp.stack([blo, bhi], axis=-1).astype(jnp.int32)


def _window_sched(cnt, slot0, slots):
    first = slot0 + cnt[:, :-1]
    end = slot0 + cnt[:, 1:]
    lower = (first // SUBLANES) * SUBLANES
    rounds = jnp.max((end - lower + WIN_ROWS - 1) // WIN_ROWS, axis=0)
    return lower.T.reshape(-1).astype(jnp.int32), jnp.maximum(rounds, 1).astype(jnp.int32)


def kernel(x_prompt, x_sample, c, cache_ckv, cache_kpe, state_hgrn, c_ctx, w_ada, b_ada, w_in, mla_q_norm, mla_w_uq, mla_kv_norm, mla_w_ukv, mla_w_o, hgrn_gamma, hgrn_norm, hgrn_w_o, w_out, ln1_g, ln1_b, moe_w_router, moe_w1, moe_w3, moe_w2, ln2_g, ln2_b):
    depth = w_ada.shape[0]
    assert depth == 1, "single trunk layer"
    bp, tp, d = x_prompt.shape
    bs, tsq, _ = x_sample.shape
    ne = moe_w_router.shape[-1]
    alpha = (2 * depth) ** 0.25
    past = cache_ckv.shape[2]
    assert tp % TOK_BLOCK == 0 and tsq % TOK_BLOCK == 0 and past % TOK_BLOCK == 0 and tsq % GRID_W == 0

    wts = _prep_weights(w_in[0], mla_q_norm[0], mla_w_uq[0], mla_kv_norm[0], mla_w_ukv[0], mla_w_o[0],
                        hgrn_norm[0], hgrn_w_o[0], w_out[0], ln1_g[0], ln1_b[0], moe_w_router[0])
    cond_rows = -(-(1 + bs) // SUBLANES) * SUBLANES
    cond = jnp.zeros((cond_rows, d), F32).at[0].set(c_ctx).at[1:1 + bs].set(c)
    mod = _adaln(cond, w_ada[0], b_ada[0])

    xs = [x_prompt.reshape(bp * tp, d), x_sample.reshape(bs * tsq, d)]
    dims = [(bp, tp), (bs, tsq)]
    rows = [(0, False), (1, True)]
    ropes = [None, _rope_tables(tsq)]
    kpe_c = jnp.pad(cache_kpe[:, 0].reshape(bs * past, MLA_ROPE), ((0, 0), (0, LANES - MLA_ROPE)))
    caches = [None, _kvup(cache_ckv[:, 0].reshape(bs * past, -1), kpe_c, wts)]
    inits = [None, state_hgrn[:, 0]]

    x1s, h2s, affs, extras = [], [], [], []
    for gi in range(2):
        (bt, sq), (row0, per_batch) = dims[gi], rows[gi]
        q, k, v, ckv, kpe, hgx, gates = _inproj(xs[gi], bt, sq, mod, hgrn_gamma[:, :, :], wts, row0,
                                                per_batch, ropes[gi])
        o_mla = _attn(q, k, v, bt, sq, caches[gi])
        o_f, o_b, s_fin = _hgrn(hgx, bt, sq, inits[gi])
        x1, h2, aff = _postmix(xs[gi], bt, sq, mod, o_f, o_b, hgx, o_mla, gates, wts, alpha, ne, row0,
                               per_batch)
        x1s.append(x1)
        h2s.append(h2)
        affs.append(aff)
        extras.append((ckv, kpe, s_fin))

    caps = [EC_FACTOR * x.shape[0] // ne for x in xs]
    ts = min(TOK_BLOCK, *caps)
    assert all(cp % ts == 0 for cp in caps)
    groups, ranks, cnts = [], [], []
    for gi in range(2):
        n = xs[gi].shape[0]
        nb = n // TOK_BLOCK
        rank, cnt = _route(affs[gi], caps[gi])
        ranks.append(rank.reshape(ne, nb, TOK_BLOCK))
        cnts.append(cnt)
        groups.append(dict(h2=h2s[gi], rank=ranks[gi], aff=affs[gi].reshape(ne, nb, TOK_BLOCK),
                           sched=_tile_sched(cnt, caps[gi] // ts, ts)))
    ye = _ffn(groups, moe_w1[0], moe_w3[0], moe_w2[0], ts, ft=512)

    outs = []
    slot0 = 0
    for gi in range(2):
        lower, rounds = _window_sched(cnts[gi], slot0, ye.shape[1])
        outs.append(_combine(ye, ranks[gi].transpose(1, 0, 2), lower, rounds, x1s[gi], mod,
                             ln2_g[0].reshape(1, -1), ln2_b[0].reshape(1, -1), alpha, slot0,
                             rows[gi][0], dims[gi][1] // TOK_BLOCK if rows[gi][1] else 1 << 30))
        slot0 += caps[gi]

    ckv_p, kpe_p, st_p = extras[0]
    y_prompt = outs[0].reshape(bp, tp, d)
    y_sample = outs[1].reshape(bs, tsq, d)
    new_ckv = ckv_p.reshape(bp, 1, tp, -1)
    new_kpe = kpe_p[:, :MLA_ROPE].reshape(bp, 1, tp, MLA_ROPE)
    new_state = st_p.reshape(bp, 1, 2, HG_HEADS, HG_DK, HG_DV)
    return (y_prompt, y_sample, new_ckv, new_kpe, new_state)
```

```python
import functools

import jax
import jax.numpy as jnp
from jax import lax
from jax.experimental import pallas as pl
from jax.experimental.pallas import tpu as pltpu

F32 = jnp.float32
BF16 = jnp.bfloat16

MLA_HEADS = 8
MLA_NOPE = 64
MLA_ROPE = 32
MLA_V = 64
HG_HEADS = 4
HG_DK = 128
HG_DV = 128
HG_CHUNK = 32
GRID_W = 64
ROPE_BASE = 10000.0
EC_FACTOR = 2
EPS = 1e-6

LANES = 128
SUBLANES = 8
VMEM_LIMIT = 56 * 1024 * 1024

TOK_BLOCK = 256
WIN_ROWS = 64
UNSELECTED = -(1 << 30)

NT_DIMS = (((1,), (1,)), ((), ()))


def _dot(a, b):
    return jnp.dot(a, b, preferred_element_type=F32)


def _dot_nt(a, b):
    return lax.dot_general(a, b, NT_DIMS, preferred_element_type=F32)


def _silu(x):
    return x * jax.nn.sigmoid(x)


def _params(*sem):
    return pltpu.CompilerParams(dimension_semantics=sem, vmem_limit_bytes=VMEM_LIMIT)


def _const_spec(shape):
    zeros = (0,) * len(shape)
    return pl.BlockSpec(shape, lambda *_: zeros, pipeline_mode=pl.Buffered(1))


def _adaln_kernel(c_ref, w_ref, b_ref, o_ref):
    s = _silu(c_ref[...]).astype(BF16)
    o_ref[...] = _dot(s, w_ref[...].astype(BF16)) + b_ref[...]


def _adaln(cond, w_ada, b_ada):
    rows, d = cond.shape
    n = w_ada.shape[1]
    tn = n // 4
    return pl.pallas_call(
        _adaln_kernel,
        out_shape=jax.ShapeDtypeStruct((rows, n), F32),
        grid=(n // tn,),
        in_specs=[_const_spec((rows, d)),
                  pl.BlockSpec((d, tn), lambda j: (0, j)),
                  pl.BlockSpec((1, tn), lambda j: (0, j))],
        out_specs=pl.BlockSpec((rows, tn), lambda j: (0, j)),
        compiler_params=_params("arbitrary"),
        name="adaln",
    )(cond, w_ada, b_ada.reshape(1, n))


def _rms(x, g):
    return x * lax.rsqrt(jnp.mean(x * x, axis=-1, keepdims=True) + EPS) * g


def _rope(x, c, s):
    w = x.shape[-1]
    lane = lax.broadcasted_iota(jnp.int32, x.shape, 1)
    nxt = pltpu.roll(x, w - 1, 1)
    prv = pltpu.roll(x, 1, 1)
    return x * c + jnp.where(lane % 2 == 0, nxt, prv) * s


def _inproj_kernel(*refs, d_model, row0, per_batch, rope):
    (x_ref, mod_ref, gam_ref, wq_ref, wckv_ref, wkpe_ref, wh_ref, wg_ref, qn_ref, wuq_ref,
     kvn_ref, wk_ref, wv_ref, wpe_ref) = refs[:14]
    refs = refs[14:]
    if rope:
        cq_ref, sq_ref, ck_ref, sk_ref = refs[:4]
        refs = refs[4:]
    q_o, k_o, v_o, ckv_o, kpe_o, hgx_o, gate_o = refs
    d = d_model
    r = row0 + pl.program_id(0) if per_batch else row0
    m = mod_ref[pl.ds(r, 1), :]
    h = (x_ref[...] * (1.0 + m[:, d:2 * d]) + m[:, 0:d]).astype(BF16)

    cq = _rms(_dot(h, wq_ref[...]), qn_ref[...])
    q = _dot(cq.astype(BF16), wuq_ref[...])
    if rope:
        q = _rope(q, cq_ref[...], sq_ref[...])
    q_o[...] = q.astype(BF16)

    ckv = _rms(_dot(h, wckv_ref[...]), kvn_ref[...])
    ckv_o[...] = ckv
    kpe = _dot(h, wkpe_ref[...])
    if rope:
        kpe = _rope(kpe, ck_ref[...], sk_ref[...])
    kpe_o[...] = kpe
    cb = ckv.astype(BF16)
    k_o[...] = (_dot(cb, wk_ref[...]) + _dot(kpe.astype(BF16), wpe_ref[...])).astype(BF16)
    v_o[...] = _dot(cb, wv_ref[...]).astype(BF16)

    hw = HG_HEADS * HG_DK
    z = _dot(h, wh_ref[...])
    hgx_o[:, 0:hw] = _silu(z[:, 0:hw])
    for dr in range(2):
        g0, g1 = gam_ref[dr, 0:1, :], gam_ref[dr, 1:2, :]
        gmax = jnp.maximum(g0, g1)
        e0, e1 = jnp.exp(g0 - gmax), jnp.exp(g1 - gmax)
        lb = e0 / (e0 + e1)
        f = lb + (1.0 - lb) * jax.nn.sigmoid(z[:, (1 + dr) * hw:(2 + dr) * hw])
        hgx_o[:, (1 + 2 * dr) * hw:(2 + 2 * dr) * hw] = jnp.log(f)
        hgx_o[:, (2 + 2 * dr) * hw:(3 + 2 * dr) * hw] = 1.0 - f
    hgx_o[:, 5 * hw:6 * hw] = z[:, 3 * hw:4 * hw]
    hgx_o[:, 6 * hw:7 * hw] = z[:, 4 * hw:5 * hw]

    gate_o[...] = _dot(h, wg_ref[...])


def _inproj(x2d, batch, seq, mod, gamma, wts, row0, per_batch, rope_tabs):
    n, d = x2d.shape
    tm = TOK_BLOCK
    nblk = seq // tm
    rope = rope_tabs is not None
    hp = MLA_HEADS * LANES
    hw = HG_HEADS * HG_DK
    tok = lambda b, i: (b * nblk + i, 0)
    pos = lambda b, i: (i, 0)
    weights = [wts[k] for k in ("wq", "wckv", "wkpe", "wh", "wg", "qn", "wuq", "kvn", "wk", "wv", "wpe")]
    ins = [x2d, mod, gamma] + weights
    in_specs = ([pl.BlockSpec((tm, d), tok), _const_spec(mod.shape), _const_spec(gamma.shape)]
                + [_const_spec(w.shape) for w in weights])
    if rope:
        ins += list(rope_tabs)
        in_specs += [pl.BlockSpec((tm, t.shape[1]), pos) for t in rope_tabs]
    widths = [(hp, BF16), (hp, BF16), (hp, BF16), (wts["wckv"].shape[1], F32), (LANES, F32),
              (7 * hw, F32), (2 * d, F32)]
    return pl.pallas_call(
        functools.partial(_inproj_kernel, d_model=d, row0=row0, per_batch=per_batch, rope=rope),
        out_shape=[jax.ShapeDtypeStruct((n, w), dt) for w, dt in widths],
        grid=(batch, nblk),
        in_specs=in_specs,
        out_specs=[pl.BlockSpec((tm, w), tok) for w, _ in widths],
        compiler_params=_params("arbitrary", "arbitrary"),
        name="inproj",
    )(*ins)


def _kvup_kernel(ckv_ref, kpe_ref, wk_ref, wv_ref, wpe_ref, k_o, v_o):
    cb = ckv_ref[...].astype(BF16)
    k_o[...] = (_dot(cb, wk_ref[...]) + _dot(kpe_ref[...].astype(BF16), wpe_ref[...])).astype(BF16)
    v_o[...] = _dot(cb, wv_ref[...]).astype(BF16)


def _kvup(ckv2d, kpe2d, wts):
    n = ckv2d.shape[0]
    tm = TOK_BLOCK
    hp = MLA_HEADS * LANES
    row = lambda i: (i, 0)
    ws = [wts["wk"], wts["wv"], wts["wpe"]]
    return pl.pallas_call(
        _kvup_kernel,
        out_shape=[jax.ShapeDtypeStruct((n, hp), BF16)] * 2,
        grid=(n // tm,),
        in_specs=[pl.BlockSpec((tm, ckv2d.shape[1]), row), pl.BlockSpec((tm, LANES), row)]
                 + [_const_spec(w.shape) for w in ws],
        out_specs=[pl.BlockSpec((tm, hp), row)] * 2,
        compiler_params=_params("arbitrary"),
        name="kvup",
    )(ckv2d, kpe2d, *ws)


def _attn_kernel(*refs, scale, cached):
    if cached:
        q_ref, k_ref, v_ref, kc_ref, vc_ref, o_ref = refs
    else:
        q_ref, k_ref, v_ref, o_ref = refs
    for hd in range(MLA_HEADS):
        sl = slice(hd * LANES, (hd + 1) * LANES)
        q = q_ref[:, sl]
        s = _dot_nt(q, k_ref[:, sl]) * scale
        mx = jnp.max(s, axis=-1, keepdims=True)
        if cached:
            s2 = _dot_nt(q, kc_ref[:, sl]) * scale
            mx = jnp.maximum(mx, jnp.max(s2, axis=-1, keepdims=True))
        e = jnp.exp(s - mx)
        den = jnp.sum(e, axis=-1, keepdims=True)
        o = _dot(e.astype(BF16), v_ref[:, sl])
        if cached:
            e2 = jnp.exp(s2 - mx)
            den = den + jnp.sum(e2, axis=-1, keepdims=True)
            o = o + _dot(e2.astype(BF16), vc_ref[:, sl])
        o_ref[:, sl] = (o / den).astype(o_ref.dtype)


def _attn(q, k, v, batch, seq, cache=None):
    n, hp = q.shape
    tq = TOK_BLOCK
    nblk = seq // tq
    ins = [q, k, v]
    in_specs = [pl.BlockSpec((tq, hp), lambda b, i: (b * nblk + i, 0)),
                pl.BlockSpec((seq, hp), lambda b, i: (b, 0)),
                pl.BlockSpec((seq, hp), lambda b, i: (b, 0))]
    if cache is not None:
        past = cache[0].shape[0] // batch
        ins += list(cache)
        in_specs += [pl.BlockSpec((past, hp), lambda b, i: (b, 0))] * 2
    return pl.pallas_call(
        functools.partial(_attn_kernel, scale=(MLA_NOPE + MLA_ROPE) ** -0.5, cached=cache is not None),
        out_shape=jax.ShapeDtypeStruct((n, hp), BF16),
        grid=(batch, nblk),
        in_specs=in_specs,
        out_specs=pl.BlockSpec((tq, hp), lambda b, i: (b * nblk + i, 0)),
        compiler_params=_params("arbitrary", "arbitrary"),
        name="attn",
    )(*ins)


def _split3(x):
    h1 = x.astype(BF16)
    r1 = x - h1.astype(F32)
    h2 = r1.astype(BF16)
    h3 = (r1 - h2.astype(F32)).astype(BF16)
    return h1, h2, h3


def _hgrn_kernel(*refs, has_init):
    fwd, bwd = refs[0:4], refs[4:8]
    refs = refs[8:]
    if has_init:
        s0_ref = refs[0]
        refs = refs[1:]
    of_ref, ob_ref, sfin_ref, st_scr = refs
    i = pl.program_id(1)
    last = pl.num_programs(1) - 1
    tm = fwd[0].shape[0]
    c = HG_CHUNK
    nch = tm // c
    dk, dv = HG_DK, HG_DV

    @pl.when(i == 0)
    def _init():
        for dr in range(2):
            for hd in range(HG_HEADS):
                st_scr[dr, hd] = s0_ref[0, dr, hd].T if has_init else jnp.zeros((dv, dk), F32)

    row = lax.broadcasted_iota(jnp.int32, (tm, tm), 0)
    col = lax.broadcasted_iota(jnp.int32, (tm, tm), 1)
    same = (row // c) == (col // c)
    ones_b = jnp.where(same, 1.0, 0.0).astype(BF16)
    bd = (lax.broadcasted_iota(jnp.int32, (tm, nch * dk), 0) // c
          == lax.broadcasted_iota(jnp.int32, (tm, nch * dk), 1) // dk)

    for dr, (hq_ref, lf_ref, kk_ref, vv_ref) in enumerate((fwd, bwd)):
        o_ref = of_ref if dr == 0 else ob_ref
        tri = same & ((col <= row) if dr == 0 else (col >= row))
        lhs = jnp.concatenate([jnp.where(tri, 1.0, 0.0).astype(BF16), ones_b], axis=0)
        h1, h2, h3 = _split3(lf_ref[...])
        sums = _dot(lhs, h1) + _dot(lhs, h2) + _dot(lhs, h3)
        bcum, btot = sums[:tm], sums[tm:]
        kk = kk_ref[...]
        qd = hq_ref[...] * jnp.exp(bcum)
        kd = kk * jnp.exp(-bcum)
        ke = kk * jnp.exp(btot - bcum)
        dec = jnp.exp(btot)
        vv = vv_ref[...]
        order = range(nch) if dr == 0 else range(nch - 1, -1, -1)
        for hd in range(HG_HEADS):
            sl = slice(hd * dk, (hd + 1) * dk)
            qd_h = qd[:, sl]
            v_h = vv[:, hd * dv:(hd + 1) * dv]
            a = jnp.where(tri, _dot_nt(qd_h.astype(BF16), kd[:, sl].astype(BF16)), 0.0)
            o_intra = _dot(a.astype(BF16), v_h.astype(BF16))
            kebd = jnp.where(bd, jnp.concatenate([ke[:, sl]] * nch, axis=1), 0.0).astype(BF16)
            qbd = jnp.where(bd, jnp.concatenate([qd_h] * nch, axis=1), 0.0).astype(BF16)
            ut = _dot(v_h.T.astype(BF16), kebd)
            st = st_scr[dr, hd]
            prev = [None] * nch
            for n in order:
                prev[n] = st
                st = st * dec[n * c:n * c + 1, sl] + ut[:, n * dk:(n + 1) * dk]
            st_scr[dr, hd] = st
            o_inter = _dot_nt(qbd, jnp.concatenate(prev, axis=1).astype(BF16))
            o_ref[:, hd * dv:(hd + 1) * dv] = o_intra + o_inter

            @pl.when(i == last)
            def _final(st=st, dr=dr, hd=hd):
                sfin_ref[0, dr, hd] = st.T


def _hgrn(hgx, batch, seq, s0=None):
    n = hgx.shape[0]
    tm = TOK_BLOCK
    nblk = seq // tm
    hw = HG_HEADS * HG_DK

    def spec(lane_blk, rev):
        if rev:
            return pl.BlockSpec((tm, hw), lambda b, i: (b * nblk + nblk - 1 - i, lane_blk))
        return pl.BlockSpec((tm, hw), lambda b, i: (b * nblk + i, lane_blk))

    in_specs = [spec(0, False), spec(1, False), spec(2, False), spec(5, False),
                spec(0, True), spec(3, True), spec(4, True), spec(5, True)]
    ins = [hgx] * 8
    st_shape = (1, 2, HG_HEADS, HG_DK, HG_DV)
    st_spec = pl.BlockSpec(st_shape, lambda b, i: (b, 0, 0, 0, 0))
    if s0 is not None:
        ins.append(s0)
        in_specs.append(st_spec)
    return pl.pallas_call(
        functools.partial(_hgrn_kernel, has_init=s0 is not None),
        out_shape=[jax.ShapeDtypeStruct((n, hw), F32), jax.ShapeDtypeStruct((n, hw), F32),
                   jax.ShapeDtypeStruct((batch,) + st_shape[1:], F32)],
        grid=(batch, nblk),
        in_specs=in_specs,
        out_specs=[spec(0, False), spec(0, True), st_spec],
        scratch_shapes=[pltpu.VMEM((2, HG_HEADS, HG_DV, HG_DK), F32)],
        compiler_params=_params("arbitrary", "arbitrary"),
        name="hgrn",
    )(*ins)


def _layer_norm(x, g, b):
    xc = x - jnp.mean(x, axis=-1, keepdims=True)
    var = jnp.mean(xc * xc, axis=-1, keepdims=True)
    return xc * lax.rsqrt(var + EPS) * g + b


def _postmix_kernel(x_ref, mod_ref, of_ref, ob_ref, zg_ref, om_ref, gate_ref, hgn_ref, womla_ref,
                    wohg_ref, wout_ref, lng_ref, lnb_ref, wr_ref, x1_o, h2_o, aff_o,
                    *, d_model, alpha, n_experts, row0, per_batch):
    d = d_model
    r = row0 + pl.program_id(0) if per_batch else row0
    m = mod_ref[pl.ds(r, 1), :]
    g1, sh2, sc2 = m[:, 2 * d:3 * d], m[:, 3 * d:4 * d], m[:, 4 * d:5 * d]
    o = of_ref[...] + ob_ref[...]
    zg = zg_ref[...]
    parts = []
    for hd in range(HG_HEADS):
        sl = slice(hd * HG_DV, (hd + 1) * HG_DV)
        parts.append(_rms(o[:, sl], hgn_ref[...]) * _silu(zg[:, sl]))
    ohg = jnp.concatenate(parts, axis=1).astype(BF16)
    gates = gate_ref[...]
    merged = (jax.nn.sigmoid(gates[:, 0:d]) * _dot(om_ref[...], womla_ref[...])
              + jax.nn.sigmoid(gates[:, d:2 * d]) * _dot(ohg, wohg_ref[...]))
    mix = _dot(merged.astype(BF16), wout_ref[...])
    x1 = _layer_norm(alpha * x_ref[...] + g1 * mix, lng_ref[...], lnb_ref[...])
    x1_o[...] = x1
    h2 = (x1 * (1.0 + sc2) + sh2).astype(BF16)
    h2_o[...] = h2
    logits = _dot_nt(wr_ref[...], h2)
    e = jnp.exp(logits - jnp.max(logits, axis=0, keepdims=True))
    aff_o[0] = e / jnp.sum(e, axis=0, keepdims=True)


def _postmix(x2d, batch, seq, mod, o_f, o_b, hgx, o_mla, gates, wts, alpha, n_experts, row0, per_batch):
    n, d = x2d.shape
    tm = TOK_BLOCK
    nblk = seq // tm
    hw = HG_HEADS * HG_DV
    tok = lambda b, i: (b * nblk + i, 0)
    weights = [wts[k] for k in ("hgn", "womla", "wohg", "wout", "ln1g", "ln1b", "wr")]
    return pl.pallas_call(
        functools.partial(_postmix_kernel, d_model=d, alpha=alpha, n_experts=n_experts, row0=row0,
                          per_batch=per_batch),
        out_shape=[jax.ShapeDtypeStruct((n, d), F32), jax.ShapeDtypeStruct((n, d), BF16),
                   jax.ShapeDtypeStruct((n // tm, n_experts, tm), F32)],
        grid=(batch, nblk),
        in_specs=[pl.BlockSpec((tm, d), tok), _const_spec(mod.shape),
                  pl.BlockSpec((tm, hw), tok), pl.BlockSpec((tm, hw), tok),
                  pl.BlockSpec((tm, hw), lambda b, i: (b * nblk + i, 6)),
                  pl.BlockSpec((tm, o_mla.shape[1]), tok), pl.BlockSpec((tm, 2 * d), tok)]
                 + [_const_spec(w.shape) for w in weights],
        out_specs=[pl.BlockSpec((tm, d), tok), pl.BlockSpec((tm, d), tok),
                   pl.BlockSpec((1, n_experts, tm), lambda b, i: (b * nblk + i, 0, 0))],
        compiler_params=_params("arbitrary", "arbitrary"),
        name="postmix",
    )(x2d, mod, o_f, o_b, hgx, o_mla, gates, *weights)


def _route_kernel(aff_ref, rank_o, cnt_o, *, cap):
    nb, ne, tb = aff_ref.shape
    key = aff_ref[...]

    def count(mask):
        return jnp.sum(jnp.sum(jnp.where(mask, 1.0, 0.0), axis=0), axis=1, keepdims=True)

    def bit_step(it, bits):
        cand = bits | jnp.left_shift(jnp.int32(1), 30 - it)
        return jnp.where(count(key >= pltpu.bitcast(cand, F32)[None]) >= cap, cand, bits)

    bits = lax.fori_loop(0, 31, bit_step, jnp.zeros((ne, 1), jnp.int32))
    thr = pltpu.bitcast(bits, F32)
    need = cap - count(key > thr[None])
    before = (lax.broadcasted_iota(jnp.int32, (tb, tb), 0)
              < lax.broadcasted_iota(jnp.int32, (tb, tb), 1))
    before = jnp.where(before, 1.0, 0.0).astype(BF16)
    off_eq = jnp.zeros((ne, 1), F32)
    off_sel = jnp.zeros((ne, 1), F32)
    cnt_o[...] = jnp.zeros_like(cnt_o)
    for blk in range(nb):
        key_b = key[blk]
        eq = key_b == thr
        eq_b = jnp.where(eq, 1.0, 0.0)
        eq_rank = _dot(eq_b.astype(BF16), before) + off_eq
        sel = (key_b > thr) | (eq & (eq_rank < need))
        sel_b = jnp.where(sel, 1.0, 0.0)
        rank = _dot(sel_b.astype(BF16), before) + off_sel
        rank_o[blk] = jnp.where(sel, rank.astype(jnp.int32), UNSELECTED)
        cnt_o[:, blk:blk + 1] = off_sel.astype(jnp.int32)
        off_eq = off_eq + jnp.sum(eq_b, axis=1, keepdims=True)
        off_sel = off_sel + jnp.sum(sel_b, axis=1, keepdims=True)
    cnt_o[:, nb:nb + 1] = off_sel.astype(jnp.int32)


def _route(aff, cap):
    nb, ne, tb = aff.shape
    assert nb + 1 <= LANES
    rank, cnt = pl.pallas_call(
        functools.partial(_route_kernel, cap=cap),
        out_shape=[jax.ShapeDtypeStruct(aff.shape, jnp.int32), jax.ShapeDtypeStruct((ne, LANES), jnp.int32)],
        in_specs=[pl.BlockSpec(memory_space=pltpu.VMEM)],
        out_specs=[pl.BlockSpec(memory_space=pltpu.VMEM)] * 2,
        compiler_params=pltpu.CompilerParams(vmem_limit_bytes=VMEM_LIMIT),
        name="route",
    )(aff)
    return rank, cnt[:, :nb + 1]


def _window_hits(rk_ref, firsts, slot0, win):
    ne, tb = rk_ref.shape[1], rk_ref.shape[2]
    win_iota = lax.broadcasted_iota(jnp.int32, (win, tb), 0)
    return [(rk_ref[0, e:e + 1, :] + (slot0 - firsts[e])) == win_iota for e in range(ne)]


def _compact_kernel(first_ref, end_ref, rounds_ref, *refs, groups, slots):
    ng = len(groups)
    h2_refs, rk_refs, af_refs = refs[0:ng], refs[ng:2 * ng], refs[2 * ng:3 * ng]
    xe_hbm, stage, tail, sem, issued = refs[3 * ng:]
    b = pl.program_id(0)
    ne = rk_refs[0].shape[1]
    win = WIN_ROWS
    d = h2_refs[0].shape[1]
    sub = SUBLANES

    def copies(slot, dsts):
        return [pltpu.make_async_copy(stage.at[slot, pl.ds(e * win, win), :],
                                      xe_hbm.at[e, pl.ds(pl.multiple_of(dsts[e], sub), win), :], sem.at[e])
                for e in range(ne)]

    def wait_previous():
        @pl.when(issued[0] > 0)
        def _():
            for cp in copies(0, [0] * ne):
                cp.wait()

    @pl.when(b == 0)
    def _init():
        issued[0] = 0
        tail[...] = jnp.zeros_like(tail)
        stage[1] = jnp.zeros(stage.shape[1:], stage.dtype)
        pad = copies(1, [slots] * ne)
        for cp in pad:
            cp.start()
        for cp in pad:
            cp.wait()

    def group_body(h2_ref, rk_ref, af_ref, slot0):
        firsts = [first_ref[b * ne + e] for e in range(ne)]
        bases = [(f // sub) * sub for f in firsts]
        ends = [end_ref[b * ne + e] - bases[e] for e in range(ne)]
        sub_iota = lax.broadcasted_iota(jnp.int32, (sub, stage.shape[2]), 0)

        def one_round(r, carry):
            dsts = [bases[e] + r * win for e in range(ne)]
            hits = _window_hits(rk_ref, dsts, slot0, win)
            onehot = jnp.where(jnp.concatenate(hits, axis=0), 1.0, 0.0).astype(BF16)
            rows = _dot(onehot, h2_ref[...])
            gate = jnp.concatenate(
                [jnp.sum(jnp.where(hits[e], af_ref[0, e:e + 1, :], 0.0), axis=1, keepdims=True)
                 for e in range(ne)], axis=0)
            slot = issued[0] % 2
            stage[slot, :, 0:d] = rows
            stage[slot, :, d:] = jnp.broadcast_to(gate, (ne * win, LANES))
            for e in range(ne):
                @pl.when(r == 0)
                def _head(e=e):
                    head = stage[slot, e * win:e * win + sub, :]
                    stage[slot, e * win:e * win + sub, :] = jnp.where(
                        sub_iota < firsts[e] - bases[e], tail[e * sub:(e + 1) * sub, :], head)

                last = (ends[e] // sub) * sub
                @pl.when(r == last // win)
                def _tail(e=e, last=last):
                    tail[e * sub:(e + 1) * sub, :] = stage[
                        slot, pl.ds(pl.multiple_of(e * win + last % win, sub), sub), :]
            wait_previous()
            for cp in copies(slot, [jnp.minimum(dst, slots) for dst in dsts]):
                cp.start()
            issued[0] = issued[0] + 1
            return carry

        lax.fori_loop(0, rounds_ref[b], one_round, 0)

    blk0 = 0
    for gi, g in enumerate(groups):
        @pl.when((b >= blk0) & (b < blk0 + g["nb"]))
        def _(gi=gi, g=g):
            group_body(h2_refs[gi], rk_refs[gi], af_refs[gi], g["slot0"])
        blk0 += g["nb"]

    @pl.when(b == pl.num_programs(0) - 1)
    def _drain():
        wait_previous()


def _compact(groups, first, end, rounds, slots):
    d = groups[0]["h2"].shape[1]
    nbs = [g["rank"].shape[0] for g in groups]
    ne, tb = groups[0]["rank"].shape[1:]
    meta, specs_h2, specs_rk = [], [], []
    blk0 = 0
    for g, nb in zip(groups, nbs):
        meta.append(dict(nb=nb, slot0=g["slot0"]))
        local = lambda b, *_, blk0=blk0, nb=nb: jnp.clip(b - blk0, 0, nb - 1)
        specs_h2.append(pl.BlockSpec((tb, d), lambda b, *_, local=local: (local(b), 0)))
        specs_rk.append(pl.BlockSpec((1, ne, tb), lambda b, *_, local=local: (local(b), 0, 0)))
        blk0 += nb
    width = d + LANES
    return pl.pallas_call(
        functools.partial(_compact_kernel, groups=meta, slots=slots),
        out_shape=jax.ShapeDtypeStruct((ne, slots + WIN_ROWS, width), F32),
        grid_spec=pltpu.PrefetchScalarGridSpec(
            num_scalar_prefetch=3,
            grid=(sum(nbs),),
            in_specs=specs_h2 + specs_rk + specs_rk,
            out_specs=pl.BlockSpec(memory_space=pl.ANY),
            scratch_shapes=[pltpu.VMEM((2, ne * WIN_ROWS, width), F32),
                            pltpu.VMEM((ne * SUBLANES, width), F32),
                            pltpu.SemaphoreType.DMA((ne,)), pltpu.SMEM((1,), jnp.int32)]),
        compiler_params=_params("arbitrary"),
        name="compact",
    )(first, end, rounds, *[g["h2"] for g in groups], *[g["rank"] for g in groups],
      *[g["aff"] for g in groups])


def _ffn_kernel(xe_ref, w1_ref, w3_ref, w2_ref, ye_ref, x_scr, g_scr):
    f = pl.program_id(1)
    d = x_scr.shape[1]

    @pl.when(f == 0)
    def _unpack():
        x_scr[...] = xe_ref[0, :, 0:d].astype(BF16)
        g_scr[...] = xe_ref[0, :, d:d + 1]

    x = x_scr[...]
    hid = _silu(_dot(x, w1_ref[0].astype(BF16))) * _dot(x, w3_ref[0].astype(BF16))
    y = _dot(hid.astype(BF16), w2_ref[0].astype(BF16))

    @pl.when(f == 0)
    def _first():
        ye_ref[0] = y

    @pl.when(f > 0)
    def _rest():
        ye_ref[0] += y

    @pl.when(f == pl.num_programs(1) - 1)
    def _gate():
        ye_ref[0] = ye_ref[0] * g_scr[...]


def _ffn(xe, w1, w3, w2, slots, ft):
    ne, d, dff = w1.shape
    nf = dff // ft
    return pl.pallas_call(
        _ffn_kernel,
        out_shape=jax.ShapeDtypeStruct((ne, slots, d), F32),
        grid=(ne, nf),
        in_specs=[pl.BlockSpec((1, slots, xe.shape[2]), lambda e, f: (e, 0, 0)),
                  pl.BlockSpec((1, d, ft), lambda e, f: (e, 0, f)),
                  pl.BlockSpec((1, d, ft), lambda e, f: (e, 0, f)),
                  pl.BlockSpec((1, ft, d), lambda e, f: (e, f, 0))],
        out_specs=pl.BlockSpec((1, slots, d), lambda e, f: (e, 0, 0)),
        scratch_shapes=[pltpu.VMEM((slots, d), BF16), pltpu.VMEM((slots, 1), F32)],
        compiler_params=_params("arbitrary", "arbitrary"),
        name="ffn",
    )(xe, w1, w3, w2)


def _combine_kernel(first_ref, rounds_ref, rk_ref, x1_ref, mod_ref, lng_ref, lnb_ref, ye_hbm, out_ref,
                    buf, acc_scr, sem, *, d_model, alpha, slot0, slots, row0, blocks_per_batch):
    d = d_model
    b = pl.program_id(0)
    nblk = pl.num_programs(0)
    ne, tb = rk_ref.shape[1], rk_ref.shape[2]
    win = WIN_ROWS
    eye = (lax.broadcasted_iota(jnp.int32, (tb, tb), 0)
           == lax.broadcasted_iota(jnp.int32, (tb, tb), 1))
    eye = jnp.where(eye, 1.0, 0.0).astype(BF16)

    def starts_of(blk, r):
        firsts = [(first_ref[blk * ne + e] // SUBLANES) * SUBLANES + r * win for e in range(ne)]
        return firsts, [jnp.minimum(f, slots - win) for f in firsts]

    def windows(slot, starts):
        return [pltpu.make_async_copy(ye_hbm.at[e, pl.ds(pl.multiple_of(starts[e], SUBLANES), win), :],
                                      buf.at[slot, pl.ds(e * win, win), :], sem.at[slot, e])
                for e in range(ne)]

    def scatter(slot, firsts, starts):
        hits = _window_hits(rk_ref, starts, slot0, win)
        hits = [h & ((rk_ref[0, e:e + 1, :] + slot0) >= firsts[e]) for e, h in enumerate(hits)]
        hit = jnp.where(jnp.concatenate(hits, axis=0), 1.0, 0.0).astype(BF16)
        hit_t = _dot_nt(eye, hit).astype(BF16)
        y = buf[slot]
        y_hi = y.astype(BF16)
        y_lo = (y - y_hi.astype(F32)).astype(BF16)
        return _dot(hit_t, y_hi) + _dot(hit_t, y_lo)

    cur = b % 2

    @pl.when(b == 0)
    def _prime():
        for cp in windows(0, starts_of(0, 0)[1]):
            cp.start()

    @pl.when(b + 1 < nblk)
    def _prefetch():
        for cp in windows(1 - cur, starts_of(b + 1, 0)[1]):
            cp.start()

    firsts, starts = starts_of(b, 0)
    for cp in windows(cur, starts):
        cp.wait()
    acc_scr[...] = scatter(cur, firsts, starts)

    def extra_round(r, carry):
        firsts, starts = starts_of(b, r)
        for cp in windows(2, starts):
            cp.start()
        for cp in windows(2, starts):
            cp.wait()
        acc_scr[...] += scatter(2, firsts, starts)
        return carry

    lax.fori_loop(1, rounds_ref[b], extra_round, 0)
    r = row0 + b // blocks_per_batch
    g2 = mod_ref[pl.ds(r, 1), :][:, 5 * d:6 * d]
    out_ref[...] = _layer_norm(alpha * x1_ref[...] + g2 * acc_scr[...], lng_ref[...], lnb_ref[...])


def _combine(ye, rank, first, rounds, x1, mod, ln_g, ln_b, alpha, slot0, row0, blocks_per_batch):
    n, d = x1.shape
    nb, ne, tb = rank.shape
    slots = ye.shape[1]
    return pl.pallas_call(
        functools.partial(_combine_kernel, d_model=d, alpha=alpha, slot0=slot0, slots=slots, row0=row0,
                          blocks_per_batch=blocks_per_batch),
        out_shape=jax.ShapeDtypeStruct((n, d), F32),
        grid_spec=pltpu.PrefetchScalarGridSpec(
            num_scalar_prefetch=2,
            grid=(nb,),
            in_specs=[pl.BlockSpec((1, ne, tb), lambda b, *_: (b, 0, 0)),
                      pl.BlockSpec((tb, d), lambda b, *_: (b, 0)),
                      pl.BlockSpec(mod.shape, lambda b, *_: (0, 0)),
                      pl.BlockSpec((1, d), lambda b, *_: (0, 0)),
                      pl.BlockSpec((1, d), lambda b, *_: (0, 0)),
                      pl.BlockSpec(memory_space=pl.ANY)],
            out_specs=pl.BlockSpec((tb, d), lambda b, *_: (b, 0)),
            scratch_shapes=[pltpu.VMEM((3, ne * WIN_ROWS, d), F32), pltpu.VMEM((tb, d), F32),
                            pltpu.SemaphoreType.DMA((3, ne))]),
        compiler_params=_params("arbitrary"),
        name="combine",
    )(first, rounds, rank, x1, mod, ln_g, ln_b, ye)


def _prep_weights(w_in, q_norm, w_uq, kv_norm, w_ukv, w_o_mla, hgrn_norm, w_o_hg, w_out, ln1_g, ln1_b,
                  w_router):
    d = w_in.shape[0]
    q_lora, kv_lora = q_norm.shape[0], kv_norm.shape[0]
    hw = HG_HEADS * HG_DK
    hh, hp = MLA_HEADS, MLA_HEADS * LANES
    o_q, o_kv, o_pe = 0, q_lora, q_lora + kv_lora
    o_h = o_pe + MLA_ROPE
    o_g = o_h + 5 * hw
    assert w_in.shape[1] == o_g + 2 * d
    qk = MLA_NOPE + MLA_ROPE
    kvw = MLA_NOPE + MLA_V
    pad_lanes = lambda a, w: jnp.pad(a, ((0, 0), (0, w - a.shape[1])))
    wuq = jnp.pad(w_uq.reshape(q_lora, hh, qk), ((0, 0), (0, 0), (0, LANES - qk))).reshape(q_lora, hp)
    ukv = w_ukv.reshape(kv_lora, hh, kvw)
    wk = jnp.pad(ukv[:, :, :MLA_NOPE], ((0, 0), (0, 0), (0, LANES - MLA_NOPE))).reshape(kv_lora, hp)
    wv = jnp.pad(ukv[:, :, MLA_NOPE:], ((0, 0), (0, 0), (0, LANES - MLA_V))).reshape(kv_lora, hp)
    place = jnp.pad(jnp.eye(MLA_ROPE, dtype=F32), ((0, LANES - MLA_ROPE), (MLA_NOPE, LANES - qk)))
    wpe = jnp.tile(place, (1, hh))
    womla = jnp.pad(w_o_mla.reshape(hh, MLA_V, d), ((0, 0), (0, LANES - MLA_V), (0, 0))).reshape(hp, d)
    b16 = lambda a: a.astype(BF16)
    return dict(
        wq=b16(w_in[:, o_q:o_kv]), wckv=b16(w_in[:, o_kv:o_pe]),
        wkpe=b16(pad_lanes(w_in[:, o_pe:o_h], LANES)), wh=b16(w_in[:, o_h:o_g]), wg=b16(w_in[:, o_g:]),
        qn=q_norm.reshape(1, -1), wuq=b16(wuq), kvn=kv_norm.reshape(1, -1), wk=b16(wk), wv=b16(wv),
        wpe=b16(wpe), hgn=hgrn_norm.reshape(1, -1), womla=b16(womla), wohg=b16(w_o_hg), wout=b16(w_out),
        ln1g=ln1_g.reshape(1, -1), ln1b=ln1_b.reshape(1, -1), wr=b16(w_router.T))


def _rope_tables(seq):
    n_freq = MLA_ROPE // 4
    inv = ROPE_BASE ** (-jnp.arange(n_freq, dtype=F32) / n_freq)
    t = jnp.arange(seq)
    r = (t // GRID_W).astype(F32)
    col = (t % GRID_W).astype(F32)
    ang = jnp.concatenate([r[:, None] * inv, col[:, None] * inv], axis=-1)
    cos = jnp.repeat(jnp.cos(ang), 2, axis=1)
    sin = jnp.repeat(jnp.sin(ang), 2, axis=1) * jnp.tile(jnp.array([-1.0, 1.0], F32), MLA_ROPE // 2)
    ck = jnp.pad(cos, ((0, 0), (0, LANES - MLA_ROPE)), constant_values=1.0)
    sk = jnp.pad(sin, ((0, 0), (0, LANES - MLA_ROPE)))
    cq = jnp.pad(cos, ((0, 0), (MLA_NOPE, LANES - MLA_NOPE - MLA_ROPE)), constant_values=1.0)
    sq = jnp.pad(sin, ((0, 0), (MLA_NOPE, LANES - MLA_NOPE - MLA_ROPE)))
    return jnp.tile(cq, (1, MLA_HEADS)), jnp.tile(sq, (1, MLA_HEADS)), ck, sk


def _window_sched(cnt, slot0):
    first = slot0 + cnt[:, :-1]
    end = slot0 + cnt[:, 1:]
    rounds = jnp.max((end - (first // SUBLANES) * SUBLANES + WIN_ROWS - 1) // WIN_ROWS, axis=0)
    flat = lambda a: a.T.reshape(-1).astype(jnp.int32)
    return flat(first), flat(end), jnp.maximum(rounds, 1).astype(jnp.int32)


def kernel(x_prompt, x_sample, c, cache_ckv, cache_kpe, state_hgrn, c_ctx, w_ada, b_ada, w_in, mla_q_norm, mla_w_uq, mla_kv_norm, mla_w_ukv, mla_w_o, hgrn_gamma, hgrn_norm, hgrn_w_o, w_out, ln1_g, ln1_b, moe_w_router, moe_w1, moe_w3, moe_w2, ln2_g, ln2_b):
    depth = w_ada.shape[0]
    assert depth == 1, "single trunk layer"
    bp, tp, d = x_prompt.shape
    bs, tsq, _ = x_sample.shape
    ne = moe_w_router.shape[-1]
    alpha = (2 * depth) ** 0.25
    past = cache_ckv.shape[2]
    assert tp % TOK_BLOCK == 0 and tsq % TOK_BLOCK == 0 and past % TOK_BLOCK == 0 and tsq % GRID_W == 0

    wts = _prep_weights(w_in[0], mla_q_norm[0], mla_w_uq[0], mla_kv_norm[0], mla_w_ukv[0], mla_w_o[0],
                        hgrn_norm[0], hgrn_w_o[0], w_out[0], ln1_g[0], ln1_b[0], moe_w_router[0])
    cond_rows = -(-(1 + bs) // SUBLANES) * SUBLANES
    cond = jnp.zeros((cond_rows, d), F32).at[0].set(c_ctx).at[1:1 + bs].set(c)
    mod = _adaln(cond, w_ada[0], b_ada[0])

    xs = [x_prompt.reshape(bp * tp, d), x_sample.reshape(bs * tsq, d)]
    dims = [(bp, tp), (bs, tsq)]
    rows = [(0, False), (1, True)]
    ropes = [None, _rope_tables(tsq)]
    kpe_c = jnp.pad(cache_kpe[:, 0].reshape(bs * past, MLA_ROPE), ((0, 0), (0, LANES - MLA_ROPE)))
    caches = [None, _kvup(cache_ckv[:, 0].reshape(bs * past, -1), kpe_c, wts)]
    inits = [None, state_hgrn[:, 0]]

    x1s, h2s, affs, extras = [], [], [], []
    for gi in range(2):
        (bt, sq), (row0, per_batch) = dims[gi], rows[gi]
        q, k, v, ckv, kpe, hgx, gates = _inproj(xs[gi], bt, sq, mod, hgrn_gamma[:, :, :], wts, row0,
                                                per_batch, ropes[gi])
        o_mla = _attn(q, k, v, bt, sq, caches[gi])
        o_f, o_b, s_fin = _hgrn(hgx, bt, sq, inits[gi])
        x1, h2, aff = _postmix(xs[gi], bt, sq, mod, o_f, o_b, hgx, o_mla, gates, wts, alpha, ne, row0,
                               per_batch)
        x1s.append(x1)
        h2s.append(h2)
        affs.append(aff)
        extras.append((ckv, kpe, s_fin))

    caps = [EC_FACTOR * x.shape[0] // ne for x in xs]
    slots = sum(caps)
    assert slots % SUBLANES == 0 and slots >= WIN_ROWS
    groups, scheds = [], []
    slot0 = 0
    for gi in range(2):
        rank, cnt = _route(affs[gi], caps[gi])
        groups.append(dict(h2=h2s[gi], rank=rank, aff=affs[gi], slot0=slot0))
        scheds.append(_window_sched(cnt, slot0))
        slot0 += caps[gi]
    xe = _compact(groups, *[jnp.concatenate([s[k] for s in scheds]) for k in range(3)], slots)
    ye = _ffn(xe, moe_w1[0], moe_w3[0], moe_w2[0], slots, ft=512)

    outs = []
    for gi in range(2):
        outs.append(_combine(ye, groups[gi]["rank"], scheds[gi][0], scheds[gi][2], x1s[gi], mod,
                             ln2_g[0].reshape(1, -1), ln2_b[0].reshape(1, -1), alpha, groups[gi]["slot0"],
                             rows[gi][0], dims[gi][1] // TOK_BLOCK if rows[gi][1] else 1 << 30))

    ckv_p, kpe_p, st_p = extras[0]
    y_prompt = outs[0].reshape(bp, tp, d)
    y_sample = outs[1].reshape(bs, tsq, d)
    new_ckv = ckv_p.reshape(bp, 1, tp, -1)
    new_kpe = kpe_p[:, :MLA_ROPE].reshape(bp, 1, tp, MLA_ROPE)
    new_state = st_p.reshape(bp, 1, 2, HG_HEADS, HG_DK, HG_DV)
    return (y_prompt, y_sample, new_ckv, new_kpe, new_state)
```

```python
import functools

import jax
import jax.numpy as jnp
from jax import lax
from jax.experimental import pallas as pl
from jax.experimental.pallas import tpu as pltpu

F32 = jnp.float32
BF16 = jnp.bfloat16

MLA_HEADS = 8
MLA_NOPE = 64
MLA_ROPE = 32
MLA_V = 64
HG_HEADS = 4
HG_DK = 128
HG_DV = 128
HG_CHUNK = 32
GRID_W = 64
ROPE_BASE = 10000.0
EC_FACTOR = 2
EPS = 1e-6

LANES = 128
SUBLANES = 8
VMEM_LIMIT = 56 * 1024 * 1024

TOK_BLOCK = 256
WIN_ROWS = 64
UNSELECTED = -(1 << 30)

NT_DIMS = (((1,), (1,)), ((), ()))


def _dot(a, b):
    return jnp.dot(a, b, preferred_element_type=F32)


def _dot_nt(a, b):
    return lax.dot_general(a, b, NT_DIMS, preferred_element_type=F32)


def _silu(x):
    return x * jax.nn.sigmoid(x)


def _params(*sem):
    return pltpu.CompilerParams(dimension_semantics=sem, vmem_limit_bytes=VMEM_LIMIT)


def _const_spec(shape):
    zeros = (0,) * len(shape)
    return pl.BlockSpec(shape, lambda *_: zeros, pipeline_mode=pl.Buffered(1))


def _adaln_kernel(c_ref, w_ref, b_ref, o_ref):
    s = _silu(c_ref[...]).astype(BF16)
    o_ref[...] = _dot(s, w_ref[...].astype(BF16)) + b_ref[...]


def _adaln(cond, w_ada, b_ada):
    rows, d = cond.shape
    n = w_ada.shape[1]
    tn = n // 4
    return pl.pallas_call(
        _adaln_kernel,
        out_shape=jax.ShapeDtypeStruct((rows, n), F32),
        grid=(n // tn,),
        in_specs=[_const_spec((rows, d)),
                  pl.BlockSpec((d, tn), lambda j: (0, j)),
                  pl.BlockSpec((1, tn), lambda j: (0, j))],
        out_specs=pl.BlockSpec((rows, tn), lambda j: (0, j)),
        compiler_params=_params("arbitrary"),
        name="adaln",
    )(cond, w_ada, b_ada.reshape(1, n))


def _rms(x, g):
    return x * lax.rsqrt(jnp.mean(x * x, axis=-1, keepdims=True) + EPS) * g


def _rope(x, c, s):
    w = x.shape[-1]
    lane = lax.broadcasted_iota(jnp.int32, x.shape, 1)
    nxt = pltpu.roll(x, w - 1, 1)
    prv = pltpu.roll(x, 1, 1)
    return x * c + jnp.where(lane % 2 == 0, nxt, prv) * s


N_INPROJ_WEIGHTS = 11


def _mod_row(mod_ref, row0, per_batch):
    r = row0 + pl.program_id(0) if per_batch else row0
    return mod_ref[pl.ds(r, 1), :]


def _inproj_kernel(*refs, row0, per_batch, rope):
    x_ref, mod_ref = refs[:2]
    _inproj_body(x_ref, _mod_row(mod_ref, row0, per_batch), *refs[2:], rope=rope)


def _inproj_body(x_ref, m, *refs, rope):
    (gam_ref, wq_ref, wckv_ref, wkpe_ref, wh_ref, wg_ref, qn_ref, wuq_ref,
     kvn_ref, wk_ref, wv_ref, wpe_ref) = refs[:1 + N_INPROJ_WEIGHTS]
    refs = refs[1 + N_INPROJ_WEIGHTS:]
    if rope:
        cq_ref, sq_ref, ck_ref, sk_ref = refs[:4]
        refs = refs[4:]
    q_o, k_o, v_o, ckv_o, kpe_o, hgx_o, gate_o = refs
    d = x_ref.shape[1]
    h = (x_ref[...] * (1.0 + m[:, d:2 * d]) + m[:, 0:d]).astype(BF16)

    cq = _rms(_dot(h, wq_ref[...]), qn_ref[...])
    q = _dot(cq.astype(BF16), wuq_ref[...])
    if rope:
        q = _rope(q, cq_ref[...], sq_ref[...])
    q_o[...] = q.astype(BF16)

    ckv = _rms(_dot(h, wckv_ref[...]), kvn_ref[...])
    ckv_o[...] = ckv
    kpe = _dot(h, wkpe_ref[...])
    if rope:
        kpe = _rope(kpe, ck_ref[...], sk_ref[...])
    kpe_o[...] = kpe
    cb = ckv.astype(BF16)
    k_o[...] = (_dot(cb, wk_ref[...]) + _dot(kpe.astype(BF16), wpe_ref[...])).astype(BF16)
    v_o[...] = _dot(cb, wv_ref[...]).astype(BF16)

    hw = HG_HEADS * HG_DK
    z = _dot(h, wh_ref[...])
    hgx_o[:, 0:hw] = _silu(z[:, 0:hw])
    for dr in range(2):
        g0, g1 = gam_ref[dr, 0:1, :], gam_ref[dr, 1:2, :]
        gmax = jnp.maximum(g0, g1)
        e0, e1 = jnp.exp(g0 - gmax), jnp.exp(g1 - gmax)
        lb = e0 / (e0 + e1)
        f = lb + (1.0 - lb) * jax.nn.sigmoid(z[:, (1 + dr) * hw:(2 + dr) * hw])
        hgx_o[:, (1 + 2 * dr) * hw:(2 + 2 * dr) * hw] = jnp.log(f)
        hgx_o[:, (2 + 2 * dr) * hw:(3 + 2 * dr) * hw] = 1.0 - f
    hgx_o[:, 5 * hw:6 * hw] = z[:, 3 * hw:4 * hw]
    hgx_o[:, 6 * hw:7 * hw] = z[:, 4 * hw:5 * hw]

    gate_o[...] = _dot(h, wg_ref[...])


def _inproj(x2d, batch, seq, mod, gamma, wts, row0, per_batch, rope_tabs):
    n, d = x2d.shape
    tm = TOK_BLOCK
    nblk = seq // tm
    rope = rope_tabs is not None
    hp = MLA_HEADS * LANES
    hw = HG_HEADS * HG_DK
    tok = lambda b, i: (b * nblk + i, 0)
    pos = lambda b, i: (i, 0)
    weights = [wts[k] for k in ("wq", "wckv", "wkpe", "wh", "wg", "qn", "wuq", "kvn", "wk", "wv", "wpe")]
    ins = [x2d, mod, gamma] + weights
    in_specs = ([pl.BlockSpec((tm, d), tok), _const_spec(mod.shape), _const_spec(gamma.shape)]
                + [_const_spec(w.shape) for w in weights])
    if rope:
        ins += list(rope_tabs)
        in_specs += [pl.BlockSpec((tm, t.shape[1]), pos) for t in rope_tabs]
    widths = [(hp, BF16), (hp, BF16), (hp, BF16), (wts["wckv"].shape[1], F32), (LANES, F32),
              (7 * hw, F32), (2 * d, F32)]
    return pl.pallas_call(
        functools.partial(_inproj_kernel, row0=row0, per_batch=per_batch, rope=rope),
        out_shape=[jax.ShapeDtypeStruct((n, w), dt) for w, dt in widths],
        grid=(batch, nblk),
        in_specs=in_specs,
        out_specs=[pl.BlockSpec((tm, w), tok) for w, _ in widths],
        compiler_params=_params("arbitrary", "arbitrary"),
        name="inproj",
    )(*ins)


def _kvup_kernel(ckv_ref, kpe_ref, wk_ref, wv_ref, wpe_ref, k_o, v_o):
    cb = ckv_ref[...].astype(BF16)
    k_o[...] = (_dot(cb, wk_ref[...]) + _dot(kpe_ref[...].astype(BF16), wpe_ref[...])).astype(BF16)
    v_o[...] = _dot(cb, wv_ref[...]).astype(BF16)


def _kvup(ckv2d, kpe2d, wts):
    n = ckv2d.shape[0]
    tm = TOK_BLOCK
    hp = MLA_HEADS * LANES
    row = lambda i: (i, 0)
    ws = [wts["wk"], wts["wv"], wts["wpe"]]
    return pl.pallas_call(
        _kvup_kernel,
        out_shape=[jax.ShapeDtypeStruct((n, hp), BF16)] * 2,
        grid=(n // tm,),
        in_specs=[pl.BlockSpec((tm, ckv2d.shape[1]), row), pl.BlockSpec((tm, LANES), row)]
                 + [_const_spec(w.shape) for w in ws],
        out_specs=[pl.BlockSpec((tm, hp), row)] * 2,
        compiler_params=_params("arbitrary"),
        name="kvup",
    )(ckv2d, kpe2d, *ws)


ATTN_SCALE = (MLA_NOPE + MLA_ROPE) ** -0.5


def _attn_kernel(*refs, cached):
    if cached:
        q_ref, k_ref, v_ref, kc_ref, vc_ref, o_ref = refs
    else:
        q_ref, k_ref, v_ref, o_ref = refs
        kc_ref = vc_ref = None
    _attn_body(q_ref, k_ref, v_ref, kc_ref, vc_ref, o_ref)


def _attn_body(q_ref, k_ref, v_ref, kc_ref, vc_ref, o_ref):
    cached = kc_ref is not None
    scale = ATTN_SCALE
    for hd in range(MLA_HEADS):
        sl = slice(hd * LANES, (hd + 1) * LANES)
        q = q_ref[:, sl]
        s = _dot_nt(q, k_ref[:, sl]) * scale
        mx = jnp.max(s, axis=-1, keepdims=True)
        if cached:
            s2 = _dot_nt(q, kc_ref[:, sl]) * scale
            mx = jnp.maximum(mx, jnp.max(s2, axis=-1, keepdims=True))
        e = jnp.exp(s - mx)
        den = jnp.sum(e, axis=-1, keepdims=True)
        o = _dot(e.astype(BF16), v_ref[:, sl])
        if cached:
            e2 = jnp.exp(s2 - mx)
            den = den + jnp.sum(e2, axis=-1, keepdims=True)
            o = o + _dot(e2.astype(BF16), vc_ref[:, sl])
        o_ref[:, sl] = (o / den).astype(o_ref.dtype)


def _attn(q, k, v, batch, seq, cache=None):
    n, hp = q.shape
    tq = TOK_BLOCK
    nblk = seq // tq
    ins = [q, k, v]
    in_specs = [pl.BlockSpec((tq, hp), lambda b, i: (b * nblk + i, 0)),
                pl.BlockSpec((seq, hp), lambda b, i: (b, 0)),
                pl.BlockSpec((seq, hp), lambda b, i: (b, 0))]
    if cache is not None:
        past = cache[0].shape[0] // batch
        ins += list(cache)
        in_specs += [pl.BlockSpec((past, hp), lambda b, i: (b, 0))] * 2
    return pl.pallas_call(
        functools.partial(_attn_kernel, cached=cache is not None),
        out_shape=jax.ShapeDtypeStruct((n, hp), BF16),
        grid=(batch, nblk),
        in_specs=in_specs,
        out_specs=pl.BlockSpec((tq, hp), lambda b, i: (b * nblk + i, 0)),
        compiler_params=_params("arbitrary", "arbitrary"),
        name="attn",
    )(*ins)


def _split3(x):
    h1 = x.astype(BF16)
    r1 = x - h1.astype(F32)
    h2 = r1.astype(BF16)
    h3 = (r1 - h2.astype(F32)).astype(BF16)
    return h1, h2, h3


def _hgrn_kernel(*refs, has_init):
    fwd, bwd = refs[0:4], refs[4:8]
    refs = refs[8:]
    s0_ref = None
    if has_init:
        s0_ref = refs[0]
        refs = refs[1:]
    of_ref, ob_ref, sfin_ref, st_scr = refs
    i = pl.program_id(1)
    _hgrn_body(fwd, bwd, s0_ref, of_ref, ob_ref, sfin_ref, st_scr, i == 0, i == pl.num_programs(1) - 1)


def _hgrn_body(fwd, bwd, s0_ref, of_ref, ob_ref, sfin_ref, st_scr, first, last):
    tm = fwd[0].shape[0]
    c = HG_CHUNK
    nch = tm // c
    dk, dv = HG_DK, HG_DV
    hw = HG_HEADS * dk

    def initial(dr, hd):
        return s0_ref[0, dr, hd].T if s0_ref is not None else jnp.zeros((dv, dk), F32)

    if st_scr is not None:
        @pl.when(first)
        def _init():
            for dr in range(2):
                for hd in range(HG_HEADS):
                    st_scr[dr, hd] = initial(dr, hd)

    row = lax.broadcasted_iota(jnp.int32, (tm, tm), 0)
    col = lax.broadcasted_iota(jnp.int32, (tm, tm), 1)
    same = (row // c) == (col // c)
    bd = (lax.broadcasted_iota(jnp.int32, (tm, nch * dk), 0) // c
          == lax.broadcasted_iota(jnp.int32, (tm, nch * dk), 1) // dk)

    for dr, (hq_ref, lf_ref, kk_ref, vv_ref) in enumerate((fwd, bwd)):
        o_ref = of_ref if dr == 0 else ob_ref
        tri = same & ((col <= row) if dr == 0 else (col >= row))
        tri_b = jnp.where(tri, 1.0, 0.0).astype(BF16)
        h1, h2, h3 = _split3(lf_ref[...])
        bcum = _dot(tri_b, h1) + _dot(tri_b, h2) + _dot(tri_b, h3)
        closing = c - 1 if dr == 0 else 0
        btot3 = bcum.reshape(nch, c, hw)[:, closing:closing + 1, :]
        btot = jnp.broadcast_to(btot3, (nch, c, hw)).reshape(tm, hw)
        dec3 = jnp.exp(btot3)
        kk = kk_ref[...]
        qd = hq_ref[...] * jnp.exp(bcum)
        kd = kk * jnp.exp(-bcum)
        ke = kk * jnp.exp(btot - bcum)
        vv = vv_ref[...]
        order = range(nch) if dr == 0 else range(nch - 1, -1, -1)
        for hd in range(HG_HEADS):
            sl = slice(hd * dk, (hd + 1) * dk)
            qd_h = qd[:, sl]
            v_h = vv[:, hd * dv:(hd + 1) * dv]
            a = jnp.where(tri, _dot_nt(qd_h.astype(BF16), kd[:, sl].astype(BF16)), 0.0)
            o_intra = _dot(a.astype(BF16), v_h.astype(BF16))
            kebd = jnp.where(bd, jnp.concatenate([ke[:, sl]] * nch, axis=1), 0.0).astype(BF16)
            qbd = jnp.where(bd, jnp.concatenate([qd_h] * nch, axis=1), 0.0).astype(BF16)
            ut = _dot(v_h.T.astype(BF16), kebd)
            st = st_scr[dr, hd] if st_scr is not None else initial(dr, hd)
            prev = [None] * nch
            for n in order:
                prev[n] = st
                st = st * dec3[n][:, sl] + ut[:, n * dk:(n + 1) * dk]
            if st_scr is not None:
                st_scr[dr, hd] = st
            o_inter = _dot_nt(qbd, jnp.concatenate(prev, axis=1).astype(BF16))
            o_ref[:, hd * dv:(hd + 1) * dv] = o_intra + o_inter

            if last is True:
                sfin_ref[0, dr, hd] = st.T
            else:
                @pl.when(last)
                def _final(st=st, dr=dr, hd=hd):
                    sfin_ref[0, dr, hd] = st.T


def _hgrn(hgx, batch, seq, s0=None):
    n = hgx.shape[0]
    tm = TOK_BLOCK
    nblk = seq // tm
    hw = HG_HEADS * HG_DK

    def spec(lane_blk, rev):
        if rev:
            return pl.BlockSpec((tm, hw), lambda b, i: (b * nblk + nblk - 1 - i, lane_blk))
        return pl.BlockSpec((tm, hw), lambda b, i: (b * nblk + i, lane_blk))

    in_specs = [spec(0, False), spec(1, False), spec(2, False), spec(5, False),
                spec(0, True), spec(3, True), spec(4, True), spec(5, True)]
    ins = [hgx] * 8
    st_shape = (1, 2, HG_HEADS, HG_DK, HG_DV)
    st_spec = pl.BlockSpec(st_shape, lambda b, i: (b, 0, 0, 0, 0))
    if s0 is not None:
        ins.append(s0)
        in_specs.append(st_spec)
    return pl.pallas_call(
        functools.partial(_hgrn_kernel, has_init=s0 is not None),
        out_shape=[jax.ShapeDtypeStruct((n, hw), F32), jax.ShapeDtypeStruct((n, hw), F32),
                   jax.ShapeDtypeStruct((batch,) + st_shape[1:], F32)],
        grid=(batch, nblk),
        in_specs=in_specs,
        out_specs=[spec(0, False), spec(0, True), st_spec],
        scratch_shapes=[pltpu.VMEM((2, HG_HEADS, HG_DV, HG_DK), F32)],
        compiler_params=_params("arbitrary", "arbitrary"),
        name="hgrn",
    )(*ins)


def _layer_norm(x, g, b):
    xc = x - jnp.mean(x, axis=-1, keepdims=True)
    var = jnp.mean(xc * xc, axis=-1, keepdims=True)
    return xc * lax.rsqrt(var + EPS) * g + b


N_POSTMIX_WEIGHTS = 7


def _postmix_kernel(x_ref, mod_ref, *refs, alpha, row0, per_batch):
    _postmix_body(x_ref, _mod_row(mod_ref, row0, per_batch), *refs, alpha=alpha)


def _postmix_body(x_ref, m, of_ref, ob_ref, zg_ref, om_ref, gate_ref, hgn_ref, womla_ref,
                  wohg_ref, wout_ref, lng_ref, lnb_ref, wr_ref, x1_o, h2_o, aff_o, *, alpha):
    d = x_ref.shape[1]
    g1, sh2, sc2 = m[:, 2 * d:3 * d], m[:, 3 * d:4 * d], m[:, 4 * d:5 * d]
    o = of_ref[...] + ob_ref[...]
    zg = zg_ref[...]
    parts = []
    for hd in range(HG_HEADS):
        sl = slice(hd * HG_DV, (hd + 1) * HG_DV)
        parts.append(_rms(o[:, sl], hgn_ref[...]) * _silu(zg[:, sl]))
    ohg = jnp.concatenate(parts, axis=1).astype(BF16)
    gates = gate_ref[...]
    merged = (jax.nn.sigmoid(gates[:, 0:d]) * _dot(om_ref[...], womla_ref[...])
              + jax.nn.sigmoid(gates[:, d:2 * d]) * _dot(ohg, wohg_ref[...]))
    mix = _dot(merged.astype(BF16), wout_ref[...])
    x1 = _layer_norm(alpha * x_ref[...] + g1 * mix, lng_ref[...], lnb_ref[...])
    x1_o[...] = x1
    h2 = (x1 * (1.0 + sc2) + sh2).astype(BF16)
    h2_o[...] = h2
    logits = _dot_nt(wr_ref[...], h2)
    e = jnp.exp(logits - jnp.max(logits, axis=0, keepdims=True))
    aff_o[0] = e / jnp.sum(e, axis=0, keepdims=True)


def _postmix(x2d, batch, seq, mod, o_f, o_b, hgx, o_mla, gates, wts, alpha, n_experts, row0, per_batch):
    n, d = x2d.shape
    tm = TOK_BLOCK
    nblk = seq // tm
    hw = HG_HEADS * HG_DV
    tok = lambda b, i: (b * nblk + i, 0)
    weights = [wts[k] for k in ("hgn", "womla", "wohg", "wout", "ln1g", "ln1b", "wr")]
    return pl.pallas_call(
        functools.partial(_postmix_kernel, alpha=alpha, row0=row0, per_batch=per_batch),
        out_shape=[jax.ShapeDtypeStruct((n, d), F32), jax.ShapeDtypeStruct((n, d), BF16),
                   jax.ShapeDtypeStruct((n // tm, n_experts, tm), F32)],
        grid=(batch, nblk),
        in_specs=[pl.BlockSpec((tm, d), tok), _const_spec(mod.shape),
                  pl.BlockSpec((tm, hw), tok), pl.BlockSpec((tm, hw), tok),
                  pl.BlockSpec((tm, hw), lambda b, i: (b * nblk + i, 6)),
                  pl.BlockSpec((tm, o_mla.shape[1]), tok), pl.BlockSpec((tm, 2 * d), tok)]
                 + [_const_spec(w.shape) for w in weights],
        out_specs=[pl.BlockSpec((tm, d), tok), pl.BlockSpec((tm, d), tok),
                   pl.BlockSpec((1, n_experts, tm), lambda b, i: (b * nblk + i, 0, 0))],
        compiler_params=_params("arbitrary", "arbitrary"),
        name="postmix",
    )(x2d, mod, o_f, o_b, hgx, o_mla, gates, *weights)


INPROJ_KEYS = ("wq", "wckv", "wkpe", "wh", "wg", "qn", "wuq", "kvn", "wk", "wv", "wpe")
POSTMIX_KEYS = ("hgn", "womla", "wohg", "wout", "ln1g", "ln1b", "wr")


def _mixer_kernel(x_ref, mod_ref, *refs, alpha, row0):
    nw = 1 + N_INPROJ_WEIGHTS
    in_w, refs = refs[:nw], refs[nw:]
    pm_w, refs = refs[:N_POSTMIX_WEIGHTS], refs[N_POSTMIX_WEIGHTS:]
    x1_o, h2_o, aff_o, ckv_o, kpe_o, sfin_o, q_s, k_s, v_s, hgx_s, gate_s, om_s, of_s, ob_s = refs
    m = _mod_row(mod_ref, row0, False)
    _inproj_body(x_ref, m, *in_w, q_s, k_s, v_s, ckv_o, kpe_o, hgx_s, gate_s, rope=False)
    _attn_body(q_s, k_s, v_s, None, None, om_s)
    hw = HG_HEADS * HG_DK
    lane = lambda j: hgx_s.at[:, j * hw:(j + 1) * hw]
    _hgrn_body((lane(0), lane(1), lane(2), lane(5)), (lane(0), lane(3), lane(4), lane(5)), None,
               of_s, ob_s, sfin_o, None, True, True)
    _postmix_body(x_ref, m, of_s, ob_s, lane(6), om_s, gate_s, *pm_w, x1_o, h2_o, aff_o, alpha=alpha)


def _mixer(x2d, batch, seq, mod, gamma, wts, alpha, n_experts, row0):
    n, d = x2d.shape
    assert seq == TOK_BLOCK
    tm = seq
    hp = MLA_HEADS * LANES
    hw = HG_HEADS * HG_DK
    kvl = wts["wckv"].shape[1]
    weights = [wts[k] for k in INPROJ_KEYS + POSTMIX_KEYS]
    tok = lambda b: (b, 0)
    st_shape = (1, 2, HG_HEADS, HG_DK, HG_DV)
    return pl.pallas_call(
        functools.partial(_mixer_kernel, alpha=alpha, row0=row0),
        out_shape=[jax.ShapeDtypeStruct((n, d), F32), jax.ShapeDtypeStruct((n, d), BF16),
                   jax.ShapeDtypeStruct((n // tm, n_experts, tm), F32),
                   jax.ShapeDtypeStruct((n, kvl), F32), jax.ShapeDtypeStruct((n, LANES), F32),
                   jax.ShapeDtypeStruct((batch,) + st_shape[1:], F32)],
        grid=(batch,),
        in_specs=[pl.BlockSpec((tm, d), tok), _const_spec(mod.shape), _const_spec(gamma.shape)]
                 + [_const_spec(w.shape) for w in weights],
        out_specs=[pl.BlockSpec((tm, d), tok), pl.BlockSpec((tm, d), tok),
                   pl.BlockSpec((1, n_experts, tm), lambda b: (b, 0, 0)),
                   pl.BlockSpec((tm, kvl), tok), pl.BlockSpec((tm, LANES), tok),
                   pl.BlockSpec(st_shape, lambda b: (b, 0, 0, 0, 0))],
        scratch_shapes=[pltpu.VMEM((tm, hp), BF16)] * 3
                       + [pltpu.VMEM((tm, 7 * hw), F32), pltpu.VMEM((tm, 2 * d), F32),
                          pltpu.VMEM((tm, hp), BF16), pltpu.VMEM((tm, hw), F32), pltpu.VMEM((tm, hw), F32)],
        compiler_params=_params("arbitrary"),
        name="mixer",
    )(x2d, mod, gamma, *weights)


def _route_kernel(aff_ref, rank_o, cnt_o, *, cap):
    nb, ne, tb = aff_ref.shape
    key = aff_ref[...]

    def count(mask):
        return jnp.sum(jnp.sum(jnp.where(mask, 1.0, 0.0), axis=0), axis=1, keepdims=True)

    def bit_step(it, bits):
        cand = bits | jnp.left_shift(jnp.int32(1), 30 - it)
        return jnp.where(count(key >= pltpu.bitcast(cand, F32)[None]) >= cap, cand, bits)

    bits = lax.fori_loop(0, 31, bit_step, jnp.zeros((ne, 1), jnp.int32))
    thr = pltpu.bitcast(bits, F32)
    need = cap - count(key > thr[None])
    before = (lax.broadcasted_iota(jnp.int32, (tb, tb), 0)
              < lax.broadcasted_iota(jnp.int32, (tb, tb), 1))
    before = jnp.where(before, 1.0, 0.0).astype(BF16)
    off_eq = jnp.zeros((ne, 1), F32)
    off_sel = jnp.zeros((ne, 1), F32)
    cnt_o[...] = jnp.zeros_like(cnt_o)
    for blk in range(nb):
        key_b = key[blk]
        eq = key_b == thr
        eq_b = jnp.where(eq, 1.0, 0.0)
        eq_rank = _dot(eq_b.astype(BF16), before) + off_eq
        sel = (key_b > thr) | (eq & (eq_rank < need))
        sel_b = jnp.where(sel, 1.0, 0.0)
        rank = _dot(sel_b.astype(BF16), before) + off_sel
        rank_o[blk] = jnp.where(sel, rank.astype(jnp.int32), UNSELECTED)
        cnt_o[:, blk:blk + 1] = off_sel.astype(jnp.int32)
        off_eq = off_eq + jnp.sum(eq_b, axis=1, keepdims=True)
        off_sel = off_sel + jnp.sum(sel_b, axis=1, keepdims=True)
    cnt_o[:, nb:nb + 1] = off_sel.astype(jnp.int32)


def _route(aff, cap):
    nb, ne, tb = aff.shape
    assert nb + 1 <= LANES
    rank, cnt = pl.pallas_call(
        functools.partial(_route_kernel, cap=cap),
        out_shape=[jax.ShapeDtypeStruct(aff.shape, jnp.int32), jax.ShapeDtypeStruct((ne, LANES), jnp.int32)],
        in_specs=[pl.BlockSpec(memory_space=pltpu.VMEM)],
        out_specs=[pl.BlockSpec(memory_space=pltpu.VMEM)] * 2,
        compiler_params=pltpu.CompilerParams(vmem_limit_bytes=VMEM_LIMIT),
        name="route",
    )(aff)
    return rank, cnt[:, :nb + 1]


def _window_hits(rk_ref, firsts, slot0, win):
    ne, tb = rk_ref.shape[1], rk_ref.shape[2]
    win_iota = lax.broadcasted_iota(jnp.int32, (win, tb), 0)
    return [(rk_ref[0, e:e + 1, :] + (slot0 - firsts[e])) == win_iota for e in range(ne)]


def _compact_kernel(first_ref, end_ref, rounds_ref, *refs, groups, slots):
    ng = len(groups)
    h2_refs, rk_refs, af_refs = refs[0:ng], refs[ng:2 * ng], refs[2 * ng:3 * ng]
    xe_hbm, stage, tail, sem, issued = refs[3 * ng:]
    b = pl.program_id(0)
    ne = rk_refs[0].shape[1]
    win = WIN_ROWS
    d = h2_refs[0].shape[1]
    sub = SUBLANES

    def copies(slot, dsts):
        return [pltpu.make_async_copy(stage.at[slot, pl.ds(e * win, win), :],
                                      xe_hbm.at[e, pl.ds(pl.multiple_of(dsts[e], sub), win), :], sem.at[e])
                for e in range(ne)]

    def wait_previous():
        @pl.when(issued[0] > 0)
        def _():
            for cp in copies(0, [0] * ne):
                cp.wait()

    @pl.when(b == 0)
    def _init():
        issued[0] = 0
        tail[...] = jnp.zeros_like(tail)
        stage[1] = jnp.zeros(stage.shape[1:], stage.dtype)
        pad = copies(1, [slots] * ne)
        for cp in pad:
            cp.start()
        for cp in pad:
            cp.wait()

    def group_body(h2_ref, rk_ref, af_ref, slot0):
        firsts = [first_ref[b * ne + e] for e in range(ne)]
        bases = [(f // sub) * sub for f in firsts]
        ends = [end_ref[b * ne + e] - bases[e] for e in range(ne)]
        sub_iota = lax.broadcasted_iota(jnp.int32, (sub, stage.shape[2]), 0)

        def one_round(r, carry):
            dsts = [bases[e] + r * win for e in range(ne)]
            hits = _window_hits(rk_ref, dsts, slot0, win)
            onehot = jnp.where(jnp.concatenate(hits, axis=0), 1.0, 0.0).astype(BF16)
            rows = _dot(onehot, h2_ref[...])
            gate = jnp.concatenate(
                [jnp.sum(jnp.where(hits[e], af_ref[0, e:e + 1, :], 0.0), axis=1, keepdims=True)
                 for e in range(ne)], axis=0)
            slot = issued[0] % 2
            stage[slot, :, 0:d] = rows
            stage[slot, :, d:] = jnp.broadcast_to(gate, (ne * win, LANES))
            for e in range(ne):
                @pl.when(r == 0)
                def _head(e=e):
                    head = stage[slot, e * win:e * win + sub, :]
                    stage[slot, e * win:e * win + sub, :] = jnp.where(
                        sub_iota < firsts[e] - bases[e], tail[e * sub:(e + 1) * sub, :], head)

                last = (ends[e] // sub) * sub
                @pl.when(r == last // win)
                def _tail(e=e, last=last):
                    tail[e * sub:(e + 1) * sub, :] = stage[
                        slot, pl.ds(pl.multiple_of(e * win + last % win, sub), sub), :]
            wait_previous()
            for cp in copies(slot, [jnp.minimum(dst, slots) for dst in dsts]):
                cp.start()
            issued[0] = issued[0] + 1
            return carry

        lax.fori_loop(0, rounds_ref[b], one_round, 0)

    blk0 = 0
    for gi, g in enumerate(groups):
        @pl.when((b >= blk0) & (b < blk0 + g["nb"]))
        def _(gi=gi, g=g):
            group_body(h2_refs[gi], rk_refs[gi], af_refs[gi], g["slot0"])
        blk0 += g["nb"]

    @pl.when(b == pl.num_programs(0) - 1)
    def _drain():
        wait_previous()


def _compact(groups, first, end, rounds, slots):
    d = groups[0]["h2"].shape[1]
    nbs = [g["rank"].shape[0] for g in groups]
    ne, tb = groups[0]["rank"].shape[1:]
    meta, specs_h2, specs_rk = [], [], []
    blk0 = 0
    for g, nb in zip(groups, nbs):
        meta.append(dict(nb=nb, slot0=g["slot0"]))
        local = lambda b, *_, blk0=blk0, nb=nb: jnp.clip(b - blk0, 0, nb - 1)
        specs_h2.append(pl.BlockSpec((tb, d), lambda b, *_, local=local: (local(b), 0)))
        specs_rk.append(pl.BlockSpec((1, ne, tb), lambda b, *_, local=local: (local(b), 0, 0)))
        blk0 += nb
    width = d + LANES
    return pl.pallas_call(
        functools.partial(_compact_kernel, groups=meta, slots=slots),
        out_shape=jax.ShapeDtypeStruct((ne, slots + WIN_ROWS, width), F32),
        grid_spec=pltpu.PrefetchScalarGridSpec(
            num_scalar_prefetch=3,
            grid=(sum(nbs),),
            in_specs=specs_h2 + specs_rk + specs_rk,
            out_specs=pl.BlockSpec(memory_space=pl.ANY),
            scratch_shapes=[pltpu.VMEM((2, ne * WIN_ROWS, width), F32),
                            pltpu.VMEM((ne * SUBLANES, width), F32),
                            pltpu.SemaphoreType.DMA((ne,)), pltpu.SMEM((1,), jnp.int32)]),
        compiler_params=_params("arbitrary"),
        name="compact",
    )(first, end, rounds, *[g["h2"] for g in groups], *[g["rank"] for g in groups],
      *[g["aff"] for g in groups])


def _ffn_kernel(xe_ref, w1_ref, w3_ref, w2_ref, ye_ref, x_scr, g_scr):
    f = pl.program_id(1)
    d = x_scr.shape[1]

    @pl.when(f == 0)
    def _unpack():
        x_scr[...] = xe_ref[0, :, 0:d].astype(BF16)
        g_scr[...] = xe_ref[0, :, d:d + 1]

    x = x_scr[...]
    hid = _silu(_dot(x, w1_ref[0].astype(BF16))) * _dot(x, w3_ref[0].astype(BF16))
    y = _dot(hid.astype(BF16), w2_ref[0].astype(BF16))

    @pl.when(f == 0)
    def _first():
        ye_ref[0] = y

    @pl.when(f > 0)
    def _rest():
        ye_ref[0] += y

    @pl.when(f == pl.num_programs(1) - 1)
    def _gate():
        ye_ref[0] = ye_ref[0] * g_scr[...]


def _ffn(xe, w1, w3, w2, slots, ft):
    ne, d, dff = w1.shape
    nf = dff // ft
    return pl.pallas_call(
        _ffn_kernel,
        out_shape=jax.ShapeDtypeStruct((ne, slots, d), F32),
        grid=(ne, nf),
        in_specs=[pl.BlockSpec((1, slots, xe.shape[2]), lambda e, f: (e, 0, 0)),
                  pl.BlockSpec((1, d, ft), lambda e, f: (e, 0, f)),
                  pl.BlockSpec((1, d, ft), lambda e, f: (e, 0, f)),
                  pl.BlockSpec((1, ft, d), lambda e, f: (e, f, 0))],
        out_specs=pl.BlockSpec((1, slots, d), lambda e, f: (e, 0, 0)),
        scratch_shapes=[pltpu.VMEM((slots, d), BF16), pltpu.VMEM((slots, 1), F32)],
        compiler_params=_params("arbitrary", "arbitrary"),
        name="ffn",
    )(xe, w1, w3, w2)


def _combine_kernel(first_ref, rounds_ref, rk_ref, x1_ref, mod_ref, lng_ref, lnb_ref, ye_hbm, out_ref,
                    buf, acc_scr, sem, *, d_model, alpha, slot0, slots, row0, blocks_per_batch):
    d = d_model
    b = pl.program_id(0)
    nblk = pl.num_programs(0)
    ne, tb = rk_ref.shape[1], rk_ref.shape[2]
    win = WIN_ROWS
    eye = (lax.broadcasted_iota(jnp.int32, (tb, tb), 0)
           == lax.broadcasted_iota(jnp.int32, (tb, tb), 1))
    eye = jnp.where(eye, 1.0, 0.0).astype(BF16)

    def starts_of(blk, r):
        firsts = [(first_ref[blk * ne + e] // SUBLANES) * SUBLANES + r * win for e in range(ne)]
        return firsts, [jnp.minimum(f, slots - win) for f in firsts]

    def windows(slot, starts):
        return [pltpu.make_async_copy(ye_hbm.at[e, pl.ds(pl.multiple_of(starts[e], SUBLANES), win), :],
                                      buf.at[slot, pl.ds(e * win, win), :], sem.at[slot, e])
                for e in range(ne)]

    def scatter(slot, firsts, starts):
        hits = _window_hits(rk_ref, starts, slot0, win)
        hits = [h & ((rk_ref[0, e:e + 1, :] + slot0) >= firsts[e]) for e, h in enumerate(hits)]
        hit = jnp.where(jnp.concatenate(hits, axis=0), 1.0, 0.0).astype(BF16)
        hit_t = _dot_nt(eye, hit).astype(BF16)
        y = buf[slot]
        y_hi = y.astype(BF16)
        y_lo = (y - y_hi.astype(F32)).astype(BF16)
        return _dot(hit_t, y_hi) + _dot(hit_t, y_lo)

    cur = b % 2

    @pl.when(b == 0)
    def _prime():
        for cp in windows(0, starts_of(0, 0)[1]):
            cp.start()

    @pl.when(b + 1 < nblk)
    def _prefetch():
        for cp in windows(1 - cur, starts_of(b + 1, 0)[1]):
            cp.start()

    firsts, starts = starts_of(b, 0)
    for cp in windows(cur, starts):
        cp.wait()
    acc_scr[...] = scatter(cur, firsts, starts)

    def extra_round(r, carry):
        firsts, starts = starts_of(b, r)
        for cp in windows(2, starts):
            cp.start()
        for cp in windows(2, starts):
            cp.wait()
        acc_scr[...] += scatter(2, firsts, starts)
        return carry

    lax.fori_loop(1, rounds_ref[b], extra_round, 0)
    r = row0 + b // blocks_per_batch
    g2 = mod_ref[pl.ds(r, 1), :][:, 5 * d:6 * d]
    out_ref[...] = _layer_norm(alpha * x1_ref[...] + g2 * acc_scr[...], lng_ref[...], lnb_ref[...])


def _combine(ye, rank, first, rounds, x1, mod, ln_g, ln_b, alpha, slot0, row0, blocks_per_batch):
    n, d = x1.shape
    nb, ne, tb = rank.shape
    slots = ye.shape[1]
    return pl.pallas_call(
        functools.partial(_combine_kernel, d_model=d, alpha=alpha, slot0=slot0, slots=slots, row0=row0,
                          blocks_per_batch=blocks_per_batch),
        out_shape=jax.ShapeDtypeStruct((n, d), F32),
        grid_spec=pltpu.PrefetchScalarGridSpec(
            num_scalar_prefetch=2,
            grid=(nb,),
            in_specs=[pl.BlockSpec((1, ne, tb), lambda b, *_: (b, 0, 0)),
                      pl.BlockSpec((tb, d), lambda b, *_: (b, 0)),
                      pl.BlockSpec(mod.shape, lambda b, *_: (0, 0)),
                      pl.BlockSpec((1, d), lambda b, *_: (0, 0)),
                      pl.BlockSpec((1, d), lambda b, *_: (0, 0)),
                      pl.BlockSpec(memory_space=pl.ANY)],
            out_specs=pl.BlockSpec((tb, d), lambda b, *_: (b, 0)),
            scratch_shapes=[pltpu.VMEM((3, ne * WIN_ROWS, d), F32), pltpu.VMEM((tb, d), F32),
                            pltpu.SemaphoreType.DMA((3, ne))]),
        compiler_params=_params("arbitrary"),
        name="combine",
    )(first, rounds, rank, x1, mod, ln_g, ln_b, ye)


def _prep_weights(w_in, q_norm, w_uq, kv_norm, w_ukv, w_o_mla, hgrn_norm, w_o_hg, w_out, ln1_g, ln1_b,
                  w_router):
    d = w_in.shape[0]
    q_lora, kv_lora = q_norm.shape[0], kv_norm.shape[0]
    hw = HG_HEADS * HG_DK
    hh, hp = MLA_HEADS, MLA_HEADS * LANES
    o_q, o_kv, o_pe = 0, q_lora, q_lora + kv_lora
    o_h = o_pe + MLA_ROPE
    o_g = o_h + 5 * hw
    assert w_in.shape[1] == o_g + 2 * d
    qk = MLA_NOPE + MLA_ROPE
    kvw = MLA_NOPE + MLA_V
    pad_lanes = lambda a, w: jnp.pad(a, ((0, 0), (0, w - a.shape[1])))
    wuq = jnp.pad(w_uq.reshape(q_lora, hh, qk), ((0, 0), (0, 0), (0, LANES - qk))).reshape(q_lora, hp)
    ukv = w_ukv.reshape(kv_lora, hh, kvw)
    wk = jnp.pad(ukv[:, :, :MLA_NOPE], ((0, 0), (0, 0), (0, LANES - MLA_NOPE))).reshape(kv_lora, hp)
    wv = jnp.pad(ukv[:, :, MLA_NOPE:], ((0, 0), (0, 0), (0, LANES - MLA_V))).reshape(kv_lora, hp)
    place = jnp.pad(jnp.eye(MLA_ROPE, dtype=F32), ((0, LANES - MLA_ROPE), (MLA_NOPE, LANES - qk)))
    wpe = jnp.tile(place, (1, hh))
    womla = jnp.pad(w_o_mla.reshape(hh, MLA_V, d), ((0, 0), (0, LANES - MLA_V), (0, 0))).reshape(hp, d)
    b16 = lambda a: a.astype(BF16)
    return dict(
        wq=b16(w_in[:, o_q:o_kv]), wckv=b16(w_in[:, o_kv:o_pe]),
        wkpe=b16(pad_lanes(w_in[:, o_pe:o_h], LANES)), wh=b16(w_in[:, o_h:o_g]), wg=b16(w_in[:, o_g:]),
        qn=q_norm.reshape(1, -1), wuq=b16(wuq), kvn=kv_norm.reshape(1, -1), wk=b16(wk), wv=b16(wv),
        wpe=b16(wpe), hgn=hgrn_norm.reshape(1, -1), womla=b16(womla), wohg=b16(w_o_hg), wout=b16(w_out),
        ln1g=ln1_g.reshape(1, -1), ln1b=ln1_b.reshape(1, -1), wr=b16(w_router.T))


def _rope_tables(seq):
    n_freq = MLA_ROPE // 4
    inv = ROPE_BASE ** (-jnp.arange(n_freq, dtype=F32) / n_freq)
    t = jnp.arange(seq)
    r = (t // GRID_W).astype(F32)
    col = (t % GRID_W).astype(F32)
    ang = jnp.concatenate([r[:, None] * inv, col[:, None] * inv], axis=-1)
    cos = jnp.repeat(jnp.cos(ang), 2, axis=1)
    sin = jnp.repeat(jnp.sin(ang), 2, axis=1) * jnp.tile(jnp.array([-1.0, 1.0], F32), MLA_ROPE // 2)
    ck = jnp.pad(cos, ((0, 0), (0, LANES - MLA_ROPE)), constant_values=1.0)
    sk = jnp.pad(sin, ((0, 0), (0, LANES - MLA_ROPE)))
    cq = jnp.pad(cos, ((0, 0), (MLA_NOPE, LANES - MLA_NOPE - MLA_ROPE)), constant_values=1.0)
    sq = jnp.pad(sin, ((0, 0), (MLA_NOPE, LANES - MLA_NOPE - MLA_ROPE)))
    return jnp.tile(cq, (1, MLA_HEADS)), jnp.tile(sq, (1, MLA_HEADS)), ck, sk


def _window_sched(cnt, slot0):
    first = slot0 + cnt[:, :-1]
    end = slot0 + cnt[:, 1:]
    rounds = jnp.max((end - (first // SUBLANES) * SUBLANES + WIN_ROWS - 1) // WIN_ROWS, axis=0)
    flat = lambda a: a.T.reshape(-1).astype(jnp.int32)
    return flat(first), flat(end), jnp.maximum(rounds, 1).astype(jnp.int32)


def kernel(x_prompt, x_sample, c, cache_ckv, cache_kpe, state_hgrn, c_ctx, w_ada, b_ada, w_in, mla_q_norm, mla_w_uq, mla_kv_norm, mla_w_ukv, mla_w_o, hgrn_gamma, hgrn_norm, hgrn_w_o, w_out, ln1_g, ln1_b, moe_w_router, moe_w1, moe_w3, moe_w2, ln2_g, ln2_b):
    depth = w_ada.shape[0]
    assert depth == 1, "single trunk layer"
    bp, tp, d = x_prompt.shape
    bs, tsq, _ = x_sample.shape
    ne = moe_w_router.shape[-1]
    alpha = (2 * depth) ** 0.25
    past = cache_ckv.shape[2]
    assert tp % TOK_BLOCK == 0 and tsq % TOK_BLOCK == 0 and past % TOK_BLOCK == 0 and tsq % GRID_W == 0

    wts = _prep_weights(w_in[0], mla_q_norm[0], mla_w_uq[0], mla_kv_norm[0], mla_w_ukv[0], mla_w_o[0],
                        hgrn_norm[0], hgrn_w_o[0], w_out[0], ln1_g[0], ln1_b[0], moe_w_router[0])
    cond_rows = -(-(1 + bs) // SUBLANES) * SUBLANES
    cond = jnp.zeros((cond_rows, d), F32).at[0].set(c_ctx).at[1:1 + bs].set(c)
    mod = _adaln(cond, w_ada[0], b_ada[0])

    xs = [x_prompt.reshape(bp * tp, d), x_sample.reshape(bs * tsq, d)]
    dims = [(bp, tp), (bs, tsq)]
    rows = [(0, False), (1, True)]
    ropes = [None, _rope_tables(tsq)]
    kpe_c = jnp.pad(cache_kpe[:, 0].reshape(bs * past, MLA_ROPE), ((0, 0), (0, LANES - MLA_ROPE)))
    caches = [None, _kvup(cache_ckv[:, 0].reshape(bs * past, -1), kpe_c, wts)]
    inits = [None, state_hgrn[:, 0]]

    x1s, h2s, affs, extras = [], [], [], []
    for gi in range(2):
        (bt, sq), (row0, per_batch) = dims[gi], rows[gi]
        if sq == TOK_BLOCK and caches[gi] is None and ropes[gi] is None and not per_batch:
            x1, h2, aff, ckv, kpe, s_fin = _mixer(xs[gi], bt, sq, mod, hgrn_gamma, wts, alpha, ne, row0)
        else:
            q, k, v, ckv, kpe, hgx, gates = _inproj(xs[gi], bt, sq, mod, hgrn_gamma, wts, row0,
                                                    per_batch, ropes[gi])
            o_mla = _attn(q, k, v, bt, sq, caches[gi])
            o_f, o_b, s_fin = _hgrn(hgx, bt, sq, inits[gi])
            x1, h2, aff = _postmix(xs[gi], bt, sq, mod, o_f, o_b, hgx, o_mla, gates, wts, alpha, ne, row0,
                                   per_batch)
        x1s.append(x1)
        h2s.append(h2)
        affs.append(aff)
        extras.append((ckv, kpe, s_fin))

    caps = [EC_FACTOR * x.shape[0] // ne for x in xs]
    slots = sum(caps)
    assert slots % SUBLANES == 0 and slots >= WIN_ROWS
    groups, scheds = [], []
    slot0 = 0
    for gi in range(2):
        rank, cnt = _route(affs[gi], caps[gi])
        groups.append(dict(h2=h2s[gi], rank=rank, aff=affs[gi], slot0=slot0))
        scheds.append(_window_sched(cnt, slot0))
        slot0 += caps[gi]
    xe = _compact(groups, *[jnp.concatenate([s[k] for s in scheds]) for k in range(3)], slots)
    ye = _ffn(xe, moe_w1[0], moe_w3[0], moe_w2[0], slots, ft=512)

    outs = []
    for gi in range(2):
        outs.append(_combine(ye, groups[gi]["rank"], scheds[gi][0], scheds[gi][2], x1s[gi], mod,
                             ln2_g[0].reshape(1, -1), ln2_b[0].reshape(1, -1), alpha, groups[gi]["slot0"],
                             rows[gi][0], dims[gi][1] // TOK_BLOCK if rows[gi][1] else 1 << 30))

    ckv_p, kpe_p, st_p = extras[0]
    y_prompt = outs[0].reshape(bp, tp, d)
    y_sample = outs[1].reshape(bs, tsq, d)
    new_ckv = ckv_p.reshape(bp, 1, tp, -1)
    new_kpe = kpe_p[:, :MLA_ROPE].reshape(bp, 1, tp, MLA_ROPE)
    new_state = st_p.reshape(bp, 1, 2, HG_HEADS, HG_DK, HG_DV)
    return (y_prompt, y_sample, new_ckv, new_kpe, new_state)
```

```python
import functools

import jax
import jax.numpy as jnp
import numpy as np
from jax import lax
from jax.experimental import pallas as pl
from jax.experimental.pallas import tpu as pltpu

F32 = jnp.float32
BF16 = jnp.bfloat16

MLA_HEADS = 8
MLA_NOPE = 64
MLA_ROPE = 32
MLA_V = 64
HG_HEADS = 4
HG_DK = 128
HG_DV = 128
HG_CHUNK = 32
GRID_W = 64
ROPE_BASE = 10000.0
EC_FACTOR = 2
EPS = 1e-6

LANES = 128
SUBLANES = 8
VMEM_LIMIT = 56 * 1024 * 1024

TOK_BLOCK = 256
WIN_ROWS = 64
UNSELECTED = -(1 << 30)

NT_DIMS = (((1,), (1,)), ((), ()))


def _dot(a, b):
    return jnp.dot(a, b, preferred_element_type=F32)


def _dot_nt(a, b):
    return lax.dot_general(a, b, NT_DIMS, preferred_element_type=F32)


def _silu(x):
    return x * jax.nn.sigmoid(x)


def _params(*sem):
    return pltpu.CompilerParams(dimension_semantics=sem, vmem_limit_bytes=VMEM_LIMIT)


def _const_spec(shape):
    zeros = (0,) * len(shape)
    return pl.BlockSpec(shape, lambda *_: zeros, pipeline_mode=pl.Buffered(1))


def _adaln_kernel(c_ref, w_ref, b_ref, o_ref):
    s = _silu(c_ref[...]).astype(BF16)
    o_ref[...] = _dot(s, w_ref[...].astype(BF16)) + b_ref[...]


def _adaln(cond, w_ada, b_ada):
    rows, d = cond.shape
    n = w_ada.shape[1]
    tn = n // 4
    return pl.pallas_call(
        _adaln_kernel,
        out_shape=jax.ShapeDtypeStruct((rows, n), F32),
        grid=(n // tn,),
        in_specs=[_const_spec((rows, d)),
                  pl.BlockSpec((d, tn), lambda j: (0, j)),
                  pl.BlockSpec((1, tn), lambda j: (0, j))],
        out_specs=pl.BlockSpec((rows, tn), lambda j: (0, j)),
        compiler_params=_params("arbitrary"),
        name="adaln",
    )(cond, w_ada, b_ada.reshape(1, n))


def _rms(x, g):
    return x * lax.rsqrt(jnp.mean(x * x, axis=-1, keepdims=True) + EPS) * g


def _rope(x, c, s):
    w = x.shape[-1]
    lane = lax.broadcasted_iota(jnp.int32, x.shape, 1)
    nxt = pltpu.roll(x, w - 1, 1)
    prv = pltpu.roll(x, 1, 1)
    return x * c + jnp.where(lane % 2 == 0, nxt, prv) * s


N_INPROJ_WEIGHTS = 7


def _mod_row(mod_ref, row0, per_batch):
    r = row0 + pl.program_id(0) if per_batch else row0
    return mod_ref[pl.ds(r, 1), :]


def _modulated(x_ref, m):
    d = x_ref.shape[1]
    return (x_ref[...] * (1.0 + m[:, d:2 * d]) + m[:, 0:d]).astype(BF16)


def _inproj_kernel(*refs, row0, per_batch, rope):
    x_ref, mod_ref = refs[:2]
    _inproj_body(x_ref, _mod_row(mod_ref, row0, per_batch), *refs[2:], rope=rope)


def _inproj_body(x_ref, m, *refs, rope):
    gam_ref, win_ref, qn_ref, wuq_ref, kvn_ref, wk_ref, wv_ref, wpe_ref = refs[:1 + N_INPROJ_WEIGHTS]
    refs = refs[1 + N_INPROJ_WEIGHTS:]
    if rope:
        cq_ref, sq_ref, ck_ref, sk_ref = refs[:4]
        refs = refs[4:]
    q_o, k_o, v_o, ckv_o, kpe_o, hgx_o = refs
    h = _modulated(x_ref, m)
    hw = HG_HEADS * HG_DK
    o_q = 2 * x_ref.shape[1]
    o_kv = o_q + qn_ref.shape[1]
    o_pe = o_kv + kvn_ref.shape[1]
    o_h = o_pe + LANES

    cq = _rms(_dot(h, win_ref[:, o_q:o_kv]), qn_ref[...])
    q = _dot(cq.astype(BF16), wuq_ref[...])
    if rope:
        q = _rope(q, jnp.concatenate([cq_ref[...]] * MLA_HEADS, axis=1),
                  jnp.concatenate([sq_ref[...]] * MLA_HEADS, axis=1))
    q_o[...] = q.astype(BF16)

    ckv = _rms(_dot(h, win_ref[:, o_kv:o_pe]), kvn_ref[...])
    ckv_o[...] = ckv
    kpe = _dot(h, win_ref[:, o_pe:o_h])
    if rope:
        kpe = _rope(kpe, ck_ref[...], sk_ref[...])
    kpe_o[...] = kpe
    cb = ckv.astype(BF16)
    k_o[...] = (_dot(cb, wk_ref[...]) + _dot(kpe.astype(BF16), wpe_ref[...])).astype(BF16)
    v_o[...] = _dot(cb, wv_ref[...]).astype(BF16)

    z = _dot(h, win_ref[:, o_h:o_h + 5 * hw])
    hgx_o[:, 0:hw] = _silu(z[:, 0:hw])
    for dr in range(2):
        g0, g1 = gam_ref[dr, 0:1, :], gam_ref[dr, 1:2, :]
        gmax = jnp.maximum(g0, g1)
        e0, e1 = jnp.exp(g0 - gmax), jnp.exp(g1 - gmax)
        lb = e0 / (e0 + e1)
        f = lb + (1.0 - lb) * jax.nn.sigmoid(z[:, (1 + dr) * hw:(2 + dr) * hw])
        hgx_o[:, (1 + 2 * dr) * hw:(2 + 2 * dr) * hw] = jnp.log(f)
        hgx_o[:, (2 + 2 * dr) * hw:(3 + 2 * dr) * hw] = 1.0 - f
    hgx_o[:, 5 * hw:6 * hw] = z[:, 3 * hw:4 * hw]
    hgx_o[:, 6 * hw:7 * hw] = z[:, 4 * hw:5 * hw]


def _inproj(x2d, batch, seq, mod, gamma, wts, row0, per_batch, rope_tabs):
    n, d = x2d.shape
    tm = TOK_BLOCK
    nblk = seq // tm
    rope = rope_tabs is not None
    hp = MLA_HEADS * LANES
    hw = HG_HEADS * HG_DK
    tok = lambda b, i: (b * nblk + i, 0)
    pos = lambda b, i: (i, 0)
    weights = [wts[k] for k in INPROJ_KEYS]
    ins = [x2d, mod, gamma] + weights
    in_specs = ([pl.BlockSpec((tm, d), tok), _const_spec(mod.shape), _const_spec(gamma.shape)]
                + [_const_spec(w.shape) for w in weights])
    if rope:
        ins += list(rope_tabs)
        in_specs += [pl.BlockSpec((tm, t.shape[1]), pos) for t in rope_tabs]
    widths = [(hp, BF16), (hp, BF16), (hp, BF16), (wts["kvn"].shape[1], F32), (LANES, F32),
              (7 * hw, F32)]
    return pl.pallas_call(
        functools.partial(_inproj_kernel, row0=row0, per_batch=per_batch, rope=rope),
        out_shape=[jax.ShapeDtypeStruct((n, w), dt) for w, dt in widths],
        grid=(batch, nblk),
        in_specs=in_specs,
        out_specs=[pl.BlockSpec((tm, w), tok) for w, _ in widths],
        compiler_params=_params("arbitrary", "arbitrary"),
        name="inproj",
    )(*ins)


def _kvup_kernel(ckv_ref, kpe_ref, wk_ref, wv_ref, wpe_ref, k_o, v_o):
    cb = ckv_ref[...].astype(BF16)
    k_o[...] = (_dot(cb, wk_ref[...]) + _dot(kpe_ref[...].astype(BF16), wpe_ref[...])).astype(BF16)
    v_o[...] = _dot(cb, wv_ref[...]).astype(BF16)


def _kvup(ckv2d, kpe2d, wts):
    n = ckv2d.shape[0]
    tm = TOK_BLOCK
    hp = MLA_HEADS * LANES
    row = lambda i: (i, 0)
    ws = [wts["wk"], wts["wv"], wts["wpe"]]
    return pl.pallas_call(
        _kvup_kernel,
        out_shape=[jax.ShapeDtypeStruct((n, hp), BF16)] * 2,
        grid=(n // tm,),
        in_specs=[pl.BlockSpec((tm, ckv2d.shape[1]), row), pl.BlockSpec((tm, LANES), row)]
                 + [_const_spec(w.shape) for w in ws],
        out_specs=[pl.BlockSpec((tm, hp), row)] * 2,
        compiler_params=_params("arbitrary"),
        name="kvup",
    )(ckv2d, kpe2d, *ws)


ATTN_SCALE = (MLA_NOPE + MLA_ROPE) ** -0.5


def _attn_kernel(*refs, cached):
    if cached:
        q_ref, k_ref, v_ref, kc_ref, vc_ref, o_ref = refs
    else:
        q_ref, k_ref, v_ref, o_ref = refs
        kc_ref = vc_ref = None
    _attn_body(q_ref, k_ref, v_ref, kc_ref, vc_ref, o_ref)


def _attn_body(q_ref, k_ref, v_ref, kc_ref, vc_ref, o_ref):
    cached = kc_ref is not None
    scale = ATTN_SCALE
    for hd in range(MLA_HEADS):
        sl = slice(hd * LANES, (hd + 1) * LANES)
        q = q_ref[:, sl]
        s = _dot_nt(q, k_ref[:, sl]) * scale
        mx = jnp.max(s, axis=-1, keepdims=True)
        if cached:
            s2 = _dot_nt(q, kc_ref[:, sl]) * scale
            mx = jnp.maximum(mx, jnp.max(s2, axis=-1, keepdims=True))
        e = jnp.exp(s - mx)
        den = jnp.sum(e, axis=-1, keepdims=True)
        o = _dot(e.astype(BF16), v_ref[:, sl])
        if cached:
            e2 = jnp.exp(s2 - mx)
            den = den + jnp.sum(e2, axis=-1, keepdims=True)
            o = o + _dot(e2.astype(BF16), vc_ref[:, sl])
        o_ref[:, sl] = (o / den).astype(o_ref.dtype)


def _attn(q, k, v, batch, seq, cache=None):
    n, hp = q.shape
    tq = TOK_BLOCK
    nblk = seq // tq
    ins = [q, k, v]
    in_specs = [pl.BlockSpec((tq, hp), lambda b, i: (b * nblk + i, 0)),
                pl.BlockSpec((seq, hp), lambda b, i: (b, 0)),
                pl.BlockSpec((seq, hp), lambda b, i: (b, 0))]
    if cache is not None:
        past = cache[0].shape[0] // batch
        ins += list(cache)
        in_specs += [pl.BlockSpec((past, hp), lambda b, i: (b, 0))] * 2
    return pl.pallas_call(
        functools.partial(_attn_kernel, cached=cache is not None),
        out_shape=jax.ShapeDtypeStruct((n, hp), BF16),
        grid=(batch, nblk),
        in_specs=in_specs,
        out_specs=pl.BlockSpec((tq, hp), lambda b, i: (b * nblk + i, 0)),
        compiler_params=_params("arbitrary", "arbitrary"),
        name="attn",
    )(*ins)


def _split3(x):
    h1 = x.astype(BF16)
    r1 = x - h1.astype(F32)
    h2 = r1.astype(BF16)
    h3 = (r1 - h2.astype(F32)).astype(BF16)
    return h1, h2, h3


def _hgrn_kernel(*refs, has_init):
    fwd, bwd = refs[0:4], refs[4:8]
    refs = refs[8:]
    s0_ref = None
    if has_init:
        s0_ref = refs[0]
        refs = refs[1:]
    of_ref, ob_ref, sfin_ref, st_scr = refs
    i = pl.program_id(1)
    _hgrn_body(fwd, bwd, s0_ref, of_ref, ob_ref, sfin_ref, st_scr, i == 0, i == pl.num_programs(1) - 1)


def _hgrn_body(fwd, bwd, s0_ref, of_ref, ob_ref, sfin_ref, st_scr, first, last):
    tm = fwd[0].shape[0]
    c = HG_CHUNK
    nch = tm // c
    dk, dv = HG_DK, HG_DV
    hw = HG_HEADS * dk

    def initial(dr, hd):
        return s0_ref[0, dr, hd].T if s0_ref is not None else jnp.zeros((dv, dk), F32)

    if st_scr is not None:
        @pl.when(first)
        def _init():
            for dr in range(2):
                for hd in range(HG_HEADS):
                    st_scr[dr, hd] = initial(dr, hd)

    row = lax.broadcasted_iota(jnp.int32, (tm, tm), 0)
    col = lax.broadcasted_iota(jnp.int32, (tm, tm), 1)
    same = (row // c) == (col // c)
    bd = (lax.broadcasted_iota(jnp.int32, (tm, nch * dk), 0) // c
          == lax.broadcasted_iota(jnp.int32, (tm, nch * dk), 1) // dk)

    for dr, (hq_ref, lf_ref, kk_ref, vv_ref) in enumerate((fwd, bwd)):
        o_ref = of_ref if dr == 0 else ob_ref
        tri = same & ((col <= row) if dr == 0 else (col >= row))
        tri_b = jnp.where(tri, 1.0, 0.0).astype(BF16)
        h1, h2, h3 = _split3(lf_ref[...])
        bcum = _dot(tri_b, h1) + _dot(tri_b, h2) + _dot(tri_b, h3)
        closing = c - 1 if dr == 0 else 0
        btot3 = bcum.reshape(nch, c, hw)[:, closing:closing + 1, :]
        btot = jnp.broadcast_to(btot3, (nch, c, hw)).reshape(tm, hw)
        dec3 = jnp.exp(btot3)
        kk = kk_ref[...]
        qd = hq_ref[...] * jnp.exp(bcum)
        kd = kk * jnp.exp(-bcum)
        ke = kk * jnp.exp(btot - bcum)
        vv = vv_ref[...]
        order = range(nch) if dr == 0 else range(nch - 1, -1, -1)
        for hd in range(HG_HEADS):
            sl = slice(hd * dk, (hd + 1) * dk)
            qd_h = qd[:, sl]
            v_h = vv[:, hd * dv:(hd + 1) * dv]
            a = jnp.where(tri, _dot_nt(qd_h.astype(BF16), kd[:, sl].astype(BF16)), 0.0)
            o_intra = _dot(a.astype(BF16), v_h.astype(BF16))
            kebd = jnp.where(bd, jnp.concatenate([ke[:, sl]] * nch, axis=1), 0.0).astype(BF16)
            qbd = jnp.where(bd, jnp.concatenate([qd_h] * nch, axis=1), 0.0).astype(BF16)
            ut = _dot(v_h.T.astype(BF16), kebd)
            st = st_scr[dr, hd] if st_scr is not None else initial(dr, hd)
            prev = [None] * nch
            for n in order:
                prev[n] = st
                st = st * dec3[n][:, sl] + ut[:, n * dk:(n + 1) * dk]
            if st_scr is not None:
                st_scr[dr, hd] = st
            o_inter = _dot_nt(qbd, jnp.concatenate(prev, axis=1).astype(BF16))
            o_ref[:, hd * dv:(hd + 1) * dv] = o_intra + o_inter

            if last is True:
                sfin_ref[0, dr, hd] = st.T
            else:
                @pl.when(last)
                def _final(st=st, dr=dr, hd=hd):
                    sfin_ref[0, dr, hd] = st.T


def _hgrn(hgx, batch, seq, s0=None):
    n = hgx.shape[0]
    tm = TOK_BLOCK
    nblk = seq // tm
    hw = HG_HEADS * HG_DK

    def spec(lane_blk, rev):
        if rev:
            return pl.BlockSpec((tm, hw), lambda b, i: (b * nblk + nblk - 1 - i, lane_blk))
        return pl.BlockSpec((tm, hw), lambda b, i: (b * nblk + i, lane_blk))

    in_specs = [spec(0, False), spec(1, False), spec(2, False), spec(5, False),
                spec(0, True), spec(3, True), spec(4, True), spec(5, True)]
    ins = [hgx] * 8
    st_shape = (1, 2, HG_HEADS, HG_DK, HG_DV)
    st_spec = pl.BlockSpec(st_shape, lambda b, i: (b, 0, 0, 0, 0))
    if s0 is not None:
        ins.append(s0)
        in_specs.append(st_spec)
    return pl.pallas_call(
        functools.partial(_hgrn_kernel, has_init=s0 is not None),
        out_shape=[jax.ShapeDtypeStruct((n, hw), F32), jax.ShapeDtypeStruct((n, hw), F32),
                   jax.ShapeDtypeStruct((batch,) + st_shape[1:], F32)],
        grid=(batch, nblk),
        in_specs=in_specs,
        out_specs=[spec(0, False), spec(0, True), st_spec],
        scratch_shapes=[pltpu.VMEM((2, HG_HEADS, HG_DV, HG_DK), F32)],
        compiler_params=_params("arbitrary", "arbitrary"),
        name="hgrn",
    )(*ins)


def _layer_norm(x, g, b):
    xc = x - jnp.mean(x, axis=-1, keepdims=True)
    var = jnp.mean(xc * xc, axis=-1, keepdims=True)
    return xc * lax.rsqrt(var + EPS) * g + b


N_POSTMIX_WEIGHTS = 8
INPROJ_KEYS = ("win", "qn", "wuq", "kvn", "wk", "wv", "wpe")
POSTMIX_KEYS = ("hgn", "womla", "wohg", "wout", "ln1g", "ln1b", "wr")
MIXER_SEQS = 2


def _postmix_kernel(x_ref, mod_ref, *refs, alpha, row0, per_batch):
    _postmix_body(x_ref, _mod_row(mod_ref, row0, per_batch), *refs, alpha=alpha)


def _postmix_body(x_ref, m, of_ref, ob_ref, zg_ref, om_ref, wg_ref, hgn_ref, womla_ref,
                  wohg_ref, wout_ref, lng_ref, lnb_ref, wr_ref, x1_o, h2_o, aff_o, *, alpha):
    d = x_ref.shape[1]
    tb = aff_o.shape[2]
    g1, sh2, sc2 = m[:, 2 * d:3 * d], m[:, 3 * d:4 * d], m[:, 4 * d:5 * d]
    o = of_ref[...] + ob_ref[...]
    zg = zg_ref[...]
    parts = []
    for hd in range(HG_HEADS):
        sl = slice(hd * HG_DV, (hd + 1) * HG_DV)
        parts.append(_rms(o[:, sl], hgn_ref[...]) * _silu(zg[:, sl]))
    ohg = jnp.concatenate(parts, axis=1).astype(BF16)
    gates = _dot(_modulated(x_ref, m), wg_ref[...])
    merged = (jax.nn.sigmoid(gates[:, 0:d]) * _dot(om_ref[...], womla_ref[...])
              + jax.nn.sigmoid(gates[:, d:2 * d]) * _dot(ohg, wohg_ref[...]))
    mix = _dot(merged.astype(BF16), wout_ref[...])
    x1 = _layer_norm(alpha * x_ref[...] + g1 * mix, lng_ref[...], lnb_ref[...])
    x1_o[...] = x1
    h2 = (x1 * (1.0 + sc2) + sh2).astype(BF16)
    h2_o[...] = h2
    logits = _dot_nt(wr_ref[...], h2)
    e = jnp.exp(logits - jnp.max(logits, axis=0, keepdims=True))
    aff = e / jnp.sum(e, axis=0, keepdims=True)
    for blk in range(aff_o.shape[0]):
        aff_o[blk] = aff[:, blk * tb:(blk + 1) * tb]


def _postmix(x2d, batch, seq, mod, o_f, o_b, hgx, o_mla, wts, alpha, n_experts, row0, per_batch):
    n, d = x2d.shape
    tm = TOK_BLOCK
    nblk = seq // tm
    hw = HG_HEADS * HG_DV
    tok = lambda b, i: (b * nblk + i, 0)
    weights = [wts[k] for k in POSTMIX_KEYS]
    return pl.pallas_call(
        functools.partial(_postmix_kernel, alpha=alpha, row0=row0, per_batch=per_batch),
        out_shape=[jax.ShapeDtypeStruct((n, d), F32), jax.ShapeDtypeStruct((n, d), BF16),
                   jax.ShapeDtypeStruct((n // tm, n_experts, tm), F32)],
        grid=(batch, nblk),
        in_specs=[pl.BlockSpec((tm, d), tok), _const_spec(mod.shape),
                  pl.BlockSpec((tm, hw), tok), pl.BlockSpec((tm, hw), tok),
                  pl.BlockSpec((tm, hw), lambda b, i: (b * nblk + i, 6)),
                  pl.BlockSpec((tm, o_mla.shape[1]), tok),
                  _const_spec((d, 2 * d))]
                 + [_const_spec(w.shape) for w in weights],
        out_specs=[pl.BlockSpec((tm, d), tok), pl.BlockSpec((tm, d), tok),
                   pl.BlockSpec((1, n_experts, tm), lambda b, i: (b * nblk + i, 0, 0))],
        compiler_params=_params("arbitrary", "arbitrary"),
        name="postmix",
    )(x2d, mod, o_f, o_b, hgx, o_mla, wts["win"], *weights)


def _mixer_kernel(x_ref, mod_ref, *refs, alpha, row0, seq):
    nw = 1 + N_INPROJ_WEIGHTS
    in_w, refs = refs[:nw], refs[nw:]
    pm_w, refs = refs[:N_POSTMIX_WEIGHTS - 1], refs[N_POSTMIX_WEIGHTS - 1:]
    pm_w = (in_w[1].at[:, 0:2 * x_ref.shape[1]],) + tuple(pm_w)
    x1_o, h2_o, aff_o, ckv_o, kpe_o, sfin_o, q_s, k_s, v_s, hgx_s, om_s, of_s, ob_s = refs
    m = _mod_row(mod_ref, row0, False)
    _inproj_body(x_ref, m, *in_w, q_s, k_s, v_s, ckv_o, kpe_o, hgx_s, rope=False)
    hw = HG_HEADS * HG_DK
    for s in range(x_ref.shape[0] // seq):
        rows = slice(s * seq, (s + 1) * seq)
        _attn_body(q_s.at[rows], k_s.at[rows], v_s.at[rows], None, None, om_s.at[rows])
        lane = lambda j: hgx_s.at[rows, j * hw:(j + 1) * hw]
        _hgrn_body((lane(0), lane(1), lane(2), lane(5)), (lane(0), lane(3), lane(4), lane(5)), None,
                   of_s.at[rows], ob_s.at[rows], sfin_o.at[s:s + 1], None, True, True)
    _postmix_body(x_ref, m, of_s, ob_s, hgx_s.at[:, 6 * hw:7 * hw], om_s, *pm_w, x1_o, h2_o, aff_o,
                  alpha=alpha)


def _mixer(x2d, batch, seq, mod, gamma, wts, alpha, n_experts, row0):
    n, d = x2d.shape
    assert seq == TOK_BLOCK
    ns = MIXER_SEQS if batch % MIXER_SEQS == 0 else 1
    tm = ns * seq
    hp = MLA_HEADS * LANES
    hw = HG_HEADS * HG_DK
    kvl = wts["kvn"].shape[1]
    weights = [wts[k] for k in INPROJ_KEYS + POSTMIX_KEYS]
    tok = lambda b: (b, 0)
    st_shape = (ns, 2, HG_HEADS, HG_DK, HG_DV)
    return pl.pallas_call(
        functools.partial(_mixer_kernel, alpha=alpha, row0=row0, seq=seq),
        out_shape=[jax.ShapeDtypeStruct((n, d), F32), jax.ShapeDtypeStruct((n, d), BF16),
                   jax.ShapeDtypeStruct((n // seq, n_experts, seq), F32),
                   jax.ShapeDtypeStruct((n, kvl), F32), jax.ShapeDtypeStruct((n, LANES), F32),
                   jax.ShapeDtypeStruct((batch,) + st_shape[1:], F32)],
        grid=(batch // ns,),
        in_specs=[pl.BlockSpec((tm, d), tok), _const_spec(mod.shape), _const_spec(gamma.shape)]
                 + [_const_spec(w.shape) for w in weights],
        out_specs=[pl.BlockSpec((tm, d), tok), pl.BlockSpec((tm, d), tok),
                   pl.BlockSpec((ns, n_experts, seq), lambda b: (b, 0, 0)),
                   pl.BlockSpec((tm, kvl), tok), pl.BlockSpec((tm, LANES), tok),
                   pl.BlockSpec(st_shape, lambda b: (b, 0, 0, 0, 0))],
        scratch_shapes=[pltpu.VMEM((tm, hp), BF16)] * 3
                       + [pltpu.VMEM((tm, 7 * hw), F32), pltpu.VMEM((tm, hp), BF16),
                          pltpu.VMEM((tm, hw), F32), pltpu.VMEM((tm, hw), F32)],
        compiler_params=_params("arbitrary"),
        name="mixer",
    )(x2d, mod, gamma, *weights)


def _route_kernel(aff_ref, rank_o, cnt_o, *, cap):
    nb, ne, tb = aff_ref.shape
    key = aff_ref[...]

    def count(mask):
        return jnp.sum(jnp.sum(jnp.where(mask, 1.0, 0.0), axis=0), axis=1, keepdims=True)

    def bit_step(it, bits):
        cand = bits | jnp.left_shift(jnp.int32(1), 30 - it)
        return jnp.where(count(key >= pltpu.bitcast(cand, F32)[None]) >= cap, cand, bits)

    bits = lax.fori_loop(0, 31, bit_step, jnp.zeros((ne, 1), jnp.int32))
    thr = pltpu.bitcast(bits, F32)
    need = cap - count(key > thr[None])
    before = (lax.broadcasted_iota(jnp.int32, (tb, tb), 0)
              < lax.broadcasted_iota(jnp.int32, (tb, tb), 1))
    before = jnp.where(before, 1.0, 0.0).astype(BF16)
    off_eq = jnp.zeros((ne, 1), F32)
    off_sel = jnp.zeros((ne, 1), F32)
    cnt_o[...] = jnp.zeros_like(cnt_o)
    for blk in range(nb):
        key_b = key[blk]
        eq = key_b == thr
        eq_b = jnp.where(eq, 1.0, 0.0)
        eq_rank = _dot(eq_b.astype(BF16), before) + off_eq
        sel = (key_b > thr) | (eq & (eq_rank < need))
        sel_b = jnp.where(sel, 1.0, 0.0)
        rank = _dot(sel_b.astype(BF16), before) + off_sel
        rank_o[blk] = jnp.where(sel, rank.astype(jnp.int32), UNSELECTED)
        cnt_o[:, blk:blk + 1] = off_sel.astype(jnp.int32)
        off_eq = off_eq + jnp.sum(eq_b, axis=1, keepdims=True)
        off_sel = off_sel + jnp.sum(sel_b, axis=1, keepdims=True)
    cnt_o[:, nb:nb + 1] = off_sel.astype(jnp.int32)


def _route(aff, cap):
    nb, ne, tb = aff.shape
    assert nb + 1 <= LANES
    rank, cnt = pl.pallas_call(
        functools.partial(_route_kernel, cap=cap),
        out_shape=[jax.ShapeDtypeStruct(aff.shape, jnp.int32), jax.ShapeDtypeStruct((ne, LANES), jnp.int32)],
        in_specs=[pl.BlockSpec(memory_space=pltpu.VMEM)],
        out_specs=[pl.BlockSpec(memory_space=pltpu.VMEM)] * 2,
        compiler_params=pltpu.CompilerParams(vmem_limit_bytes=VMEM_LIMIT),
        name="route",
    )(aff)
    return rank, cnt[:, :nb + 1]


def _window_hits(rk_ref, firsts, slot0, win):
    ne, tb = rk_ref.shape[1], rk_ref.shape[2]
    win_iota = lax.broadcasted_iota(jnp.int32, (win, tb), 0)
    return [(rk_ref[0, e:e + 1, :] + (slot0 - firsts[e])) == win_iota for e in range(ne)]


def _compact_kernel(first_ref, end_ref, rounds_ref, *refs, groups, slots):
    ng = len(groups)
    h2_refs, rk_refs, af_refs = refs[0:ng], refs[ng:2 * ng], refs[2 * ng:3 * ng]
    xe_hbm, stage, tail, sem, issued = refs[3 * ng:]
    b = pl.program_id(0)
    ne = rk_refs[0].shape[1]
    win = WIN_ROWS
    d = h2_refs[0].shape[1]
    sub = SUBLANES

    def copies(slot, dsts):
        return [pltpu.make_async_copy(stage.at[slot, pl.ds(e * win, win), :],
                                      xe_hbm.at[e, pl.ds(pl.multiple_of(dsts[e], sub), win), :], sem.at[e])
                for e in range(ne)]

    def wait_previous():
        @pl.when(issued[0] > 0)
        def _():
            for cp in copies(0, [0] * ne):
                cp.wait()

    @pl.when(b == 0)
    def _init():
        issued[0] = 0
        tail[...] = jnp.zeros_like(tail)
        stage[1] = jnp.zeros(stage.shape[1:], stage.dtype)
        pad = copies(1, [slots] * ne)
        for cp in pad:
            cp.start()
        for cp in pad:
            cp.wait()

    def group_body(h2_ref, rk_ref, af_ref, slot0):
        firsts = [first_ref[b * ne + e] for e in range(ne)]
        bases = [(f // sub) * sub for f in firsts]
        ends = [end_ref[b * ne + e] - bases[e] for e in range(ne)]
        sub_iota = lax.broadcasted_iota(jnp.int32, (sub, stage.shape[2]), 0)

        def one_round(r, carry):
            dsts = [bases[e] + r * win for e in range(ne)]
            hits = _window_hits(rk_ref, dsts, slot0, win)
            onehot = jnp.where(jnp.concatenate(hits, axis=0), 1.0, 0.0).astype(BF16)
            rows = _dot(onehot, h2_ref[...])
            gate = jnp.concatenate(
                [jnp.sum(jnp.where(hits[e], af_ref[0, e:e + 1, :], 0.0), axis=1, keepdims=True)
                 for e in range(ne)], axis=0)
            slot = issued[0] % 2
            stage[slot, :, 0:d] = rows
            stage[slot, :, d:] = jnp.broadcast_to(gate, (ne * win, LANES))
            for e in range(ne):
                @pl.when(r == 0)
                def _head(e=e):
                    head = stage[slot, e * win:e * win + sub, :]
                    stage[slot, e * win:e * win + sub, :] = jnp.where(
                        sub_iota < firsts[e] - bases[e], tail[e * sub:(e + 1) * sub, :], head)

                last = (ends[e] // sub) * sub
                @pl.when(r == last // win)
                def _tail(e=e, last=last):
                    tail[e * sub:(e + 1) * sub, :] = stage[
                        slot, pl.ds(pl.multiple_of(e * win + last % win, sub), sub), :]
            wait_previous()
            for cp in copies(slot, [jnp.minimum(dst, slots) for dst in dsts]):
                cp.start()
            issued[0] = issued[0] + 1
            return carry

        lax.fori_loop(0, rounds_ref[b], one_round, 0)

    blk0 = 0
    for gi, g in enumerate(groups):
        @pl.when((b >= blk0) & (b < blk0 + g["nb"]))
        def _(gi=gi, g=g):
            group_body(h2_refs[gi], rk_refs[gi], af_refs[gi], g["slot0"])
        blk0 += g["nb"]

    @pl.when(b == pl.num_programs(0) - 1)
    def _drain():
        wait_previous()


def _compact(groups, first, end, rounds, slots):
    d = groups[0]["h2"].shape[1]
    nbs = [g["rank"].shape[0] for g in groups]
    ne, tb = groups[0]["rank"].shape[1:]
    meta, specs_h2, specs_rk = [], [], []
    blk0 = 0
    for g, nb in zip(groups, nbs):
        meta.append(dict(nb=nb, slot0=g["slot0"]))
        local = lambda b, *_, blk0=blk0, nb=nb: jnp.clip(b - blk0, 0, nb - 1)
        specs_h2.append(pl.BlockSpec((tb, d), lambda b, *_, local=local: (local(b), 0)))
        specs_rk.append(pl.BlockSpec((1, ne, tb), lambda b, *_, local=local: (local(b), 0, 0)))
        blk0 += nb
    width = d + LANES
    return pl.pallas_call(
        functools.partial(_compact_kernel, groups=meta, slots=slots),
        out_shape=jax.ShapeDtypeStruct((ne, slots + WIN_ROWS, width), F32),
        grid_spec=pltpu.PrefetchScalarGridSpec(
            num_scalar_prefetch=3,
            grid=(sum(nbs),),
            in_specs=specs_h2 + specs_rk + specs_rk,
            out_specs=pl.BlockSpec(memory_space=pl.ANY),
            scratch_shapes=[pltpu.VMEM((2, ne * WIN_ROWS, width), F32),
                            pltpu.VMEM((ne * SUBLANES, width), F32),
                            pltpu.SemaphoreType.DMA((ne,)), pltpu.SMEM((1,), jnp.int32)]),
        compiler_params=_params("arbitrary"),
        name="compact",
    )(first, end, rounds, *[g["h2"] for g in groups], *[g["rank"] for g in groups],
      *[g["aff"] for g in groups])


def _ffn_kernel(xe_ref, w1_ref, w3_ref, w2_ref, ye_ref, x_scr, g_scr):
    f = pl.program_id(1)
    d = x_scr.shape[1]

    @pl.when(f == 0)
    def _unpack():
        x_scr[...] = xe_ref[0, :, 0:d].astype(BF16)
        g_scr[...] = xe_ref[0, :, d:d + 1]

    x = x_scr[...]
    hid = _silu(_dot(x, w1_ref[0].astype(BF16))) * _dot(x, w3_ref[0].astype(BF16))
    y = _dot(hid.astype(BF16), w2_ref[0].astype(BF16))

    last = pl.num_programs(1) - 1

    @pl.when((f == 0) & (f < last))
    def _first():
        ye_ref[0] = y

    @pl.when((f > 0) & (f < last))
    def _middle():
        ye_ref[0] += y

    @pl.when((f == last) & (f > 0))
    def _last():
        ye_ref[0] = (ye_ref[0] + y) * g_scr[...]

    @pl.when((f == last) & (f == 0))
    def _only():
        ye_ref[0] = y * g_scr[...]


def _ffn(xe, w1, w3, w2, slots, ft):
    ne, d, dff = w1.shape
    nf = dff // ft
    return pl.pallas_call(
        _ffn_kernel,
        out_shape=jax.ShapeDtypeStruct((ne, slots, d), F32),
        grid=(ne, nf),
        in_specs=[pl.BlockSpec((1, slots, xe.shape[2]), lambda e, f: (e, 0, 0)),
                  pl.BlockSpec((1, d, ft), lambda e, f: (e, 0, f)),
                  pl.BlockSpec((1, d, ft), lambda e, f: (e, 0, f)),
                  pl.BlockSpec((1, ft, d), lambda e, f: (e, f, 0))],
        out_specs=pl.BlockSpec((1, slots, d), lambda e, f: (e, 0, 0)),
        scratch_shapes=[pltpu.VMEM((slots, d), BF16), pltpu.VMEM((slots, 1), F32)],
        compiler_params=_params("arbitrary", "arbitrary"),
        name="ffn",
    )(xe, w1, w3, w2)


def _combine_kernel(first_ref, rounds_ref, rk_ref, x1_ref, mod_ref, lng_ref, lnb_ref, ye_hbm, out_ref,
                    buf, acc_scr, sem, *, d_model, alpha, slot0, slots, row0, blocks_per_batch):
    d = d_model
    b = pl.program_id(0)
    nblk = pl.num_programs(0)
    ne, tb = rk_ref.shape[1], rk_ref.shape[2]
    win = WIN_ROWS
    eye = (lax.broadcasted_iota(jnp.int32, (tb, tb), 0)
           == lax.broadcasted_iota(jnp.int32, (tb, tb), 1))
    eye = jnp.where(eye, 1.0, 0.0).astype(BF16)

    def starts_of(blk, r):
        firsts = [(first_ref[blk * ne + e] // SUBLANES) * SUBLANES + r * win for e in range(ne)]
        return firsts, [jnp.minimum(f, slots - win) for f in firsts]

    def windows(slot, starts):
        return [pltpu.make_async_copy(ye_hbm.at[e, pl.ds(pl.multiple_of(starts[e], SUBLANES), win), :],
                                      buf.at[slot, pl.ds(e * win, win), :], sem.at[slot, e])
                for e in range(ne)]

    def scatter(slot, firsts, starts):
        hits = _window_hits(rk_ref, starts, slot0, win)
        hits = [h & ((rk_ref[0, e:e + 1, :] + slot0) >= firsts[e]) for e, h in enumerate(hits)]
        hit = jnp.where(jnp.concatenate(hits, axis=0), 1.0, 0.0).astype(BF16)
        hit_t = _dot_nt(eye, hit).astype(BF16)
        y = buf[slot]
        y_hi = y.astype(BF16)
        y_lo = (y - y_hi.astype(F32)).astype(BF16)
        return _dot(hit_t, y_hi) + _dot(hit_t, y_lo)

    cur = b % 2

    @pl.when(b == 0)
    def _prime():
        for cp in windows(0, starts_of(0, 0)[1]):
            cp.start()

    @pl.when(b + 1 < nblk)
    def _prefetch():
        for cp in windows(1 - cur, starts_of(b + 1, 0)[1]):
            cp.start()

    firsts, starts = starts_of(b, 0)
    for cp in windows(cur, starts):
        cp.wait()
    acc_scr[...] = scatter(cur, firsts, starts)

    def extra_round(r, carry):
        firsts, starts = starts_of(b, r)
        for cp in windows(2, starts):
            cp.start()
        for cp in windows(2, starts):
            cp.wait()
        acc_scr[...] += scatter(2, firsts, starts)
        return carry

    lax.fori_loop(1, rounds_ref[b], extra_round, 0)
    r = row0 + b // blocks_per_batch
    g2 = mod_ref[pl.ds(r, 1), :][:, 5 * d:6 * d]
    out_ref[...] = _layer_norm(alpha * x1_ref[...] + g2 * acc_scr[...], lng_ref[...], lnb_ref[...])


def _combine(ye, rank, first, rounds, x1, mod, ln_g, ln_b, alpha, slot0, row0, blocks_per_batch):
    n, d = x1.shape
    nb, ne, tb = rank.shape
    slots = ye.shape[1]
    return pl.pallas_call(
        functools.partial(_combine_kernel, d_model=d, alpha=alpha, slot0=slot0, slots=slots, row0=row0,
                          blocks_per_batch=blocks_per_batch),
        out_shape=jax.ShapeDtypeStruct((n, d), F32),
        grid_spec=pltpu.PrefetchScalarGridSpec(
            num_scalar_prefetch=2,
            grid=(nb,),
            in_specs=[pl.BlockSpec((1, ne, tb), lambda b, *_: (b, 0, 0)),
                      pl.BlockSpec((tb, d), lambda b, *_: (b, 0)),
                      pl.BlockSpec(mod.shape, lambda b, *_: (0, 0)),
                      pl.BlockSpec((1, d), lambda b, *_: (0, 0)),
                      pl.BlockSpec((1, d), lambda b, *_: (0, 0)),
                      pl.BlockSpec(memory_space=pl.ANY)],
            out_specs=pl.BlockSpec((tb, d), lambda b, *_: (b, 0)),
            scratch_shapes=[pltpu.VMEM((3, ne * WIN_ROWS, d), F32), pltpu.VMEM((tb, d), F32),
                            pltpu.SemaphoreType.DMA((3, ne))]),
        compiler_params=_params("arbitrary"),
        name="combine",
    )(first, rounds, rank, x1, mod, ln_g, ln_b, ye)


def _prep_weights(w_in, q_norm, w_uq, kv_norm, w_ukv, w_o_mla, hgrn_norm, w_o_hg, w_out, ln1_g, ln1_b,
                  w_router):
    d = w_in.shape[0]
    q_lora, kv_lora = q_norm.shape[0], kv_norm.shape[0]
    hw = HG_HEADS * HG_DK
    hh, hp = MLA_HEADS, MLA_HEADS * LANES
    o_kv, o_pe = q_lora, q_lora + kv_lora
    o_h = o_pe + MLA_ROPE
    o_g = o_h + 5 * hw
    assert w_in.shape[1] == o_g + 2 * d
    qk = MLA_NOPE + MLA_ROPE
    kvw = MLA_NOPE + MLA_V
    b16 = lambda a: a.astype(BF16)
    win = jnp.concatenate([b16(w_in[:, o_g:]), b16(w_in[:, :o_h]), jnp.zeros((d, LANES - MLA_ROPE), BF16),
                           b16(w_in[:, o_h:o_g])], axis=1)
    wuq = jnp.pad(w_uq.reshape(q_lora, hh, qk), ((0, 0), (0, 0), (0, LANES - qk))).reshape(q_lora, hp)
    ukv = w_ukv.reshape(kv_lora, hh, kvw)
    wk = jnp.pad(ukv[:, :, :MLA_NOPE], ((0, 0), (0, 0), (0, LANES - MLA_NOPE))).reshape(kv_lora, hp)
    wv = jnp.pad(ukv[:, :, MLA_NOPE:], ((0, 0), (0, 0), (0, LANES - MLA_V))).reshape(kv_lora, hp)
    place = np.pad(np.eye(MLA_ROPE, dtype=np.float32), ((0, LANES - MLA_ROPE), (MLA_NOPE, LANES - qk)))
    wpe = jnp.asarray(np.tile(place, (1, hh)), BF16)
    womla = jnp.pad(w_o_mla.reshape(hh, MLA_V, d), ((0, 0), (0, LANES - MLA_V), (0, 0))).reshape(hp, d)
    return dict(
        win=win, qn=q_norm.reshape(1, -1), wuq=b16(wuq), kvn=kv_norm.reshape(1, -1), wk=b16(wk), wv=b16(wv),
        wpe=wpe, hgn=hgrn_norm.reshape(1, -1), womla=b16(womla), wohg=b16(w_o_hg), wout=b16(w_out),
        ln1g=ln1_g.reshape(1, -1), ln1b=ln1_b.reshape(1, -1), wr=b16(w_router.T))


def _rope_tables(seq):
    n_freq = MLA_ROPE // 4
    inv = ROPE_BASE ** (-np.arange(n_freq, dtype=np.float64) / n_freq)
    t = np.arange(seq)
    ang = np.concatenate([(t // GRID_W)[:, None] * inv, (t % GRID_W)[:, None] * inv], axis=-1)
    cos = np.repeat(np.cos(ang), 2, axis=1)
    sin = np.repeat(np.sin(ang), 2, axis=1) * np.tile([-1.0, 1.0], MLA_ROPE // 2)
    ck = np.pad(cos, ((0, 0), (0, LANES - MLA_ROPE)), constant_values=1.0)
    sk = np.pad(sin, ((0, 0), (0, LANES - MLA_ROPE)))
    cq = np.pad(cos, ((0, 0), (MLA_NOPE, LANES - MLA_NOPE - MLA_ROPE)), constant_values=1.0)
    sq = np.pad(sin, ((0, 0), (MLA_NOPE, LANES - MLA_NOPE - MLA_ROPE)))
    return tuple(jnp.asarray(a, F32) for a in (cq, sq, ck, sk))


def _window_sched(cnt, slot0):
    first = slot0 + cnt[:, :-1]
    end = slot0 + cnt[:, 1:]
    rounds = jnp.max((end - (first // SUBLANES) * SUBLANES + WIN_ROWS - 1) // WIN_ROWS, axis=0)
    flat = lambda a: a.T.reshape(-1).astype(jnp.int32)
    return flat(first), flat(end), jnp.maximum(rounds, 1).astype(jnp.int32)


def kernel(x_prompt, x_sample, c, cache_ckv, cache_kpe, state_hgrn, c_ctx, w_ada, b_ada, w_in, mla_q_norm, mla_w_uq, mla_kv_norm, mla_w_ukv, mla_w_o, hgrn_gamma, hgrn_norm, hgrn_w_o, w_out, ln1_g, ln1_b, moe_w_router, moe_w1, moe_w3, moe_w2, ln2_g, ln2_b):
    depth = w_ada.shape[0]
    assert depth == 1, "single trunk layer"
    bp, tp, d = x_prompt.shape
    bs, tsq, _ = x_sample.shape
    ne = moe_w_router.shape[-1]
    alpha = (2 * depth) ** 0.25
    past = cache_ckv.shape[2]
    assert tp % TOK_BLOCK == 0 and tsq % TOK_BLOCK == 0 and past % TOK_BLOCK == 0 and tsq % GRID_W == 0

    wts = _prep_weights(w_in[0], mla_q_norm[0], mla_w_uq[0], mla_kv_norm[0], mla_w_ukv[0], mla_w_o[0],
                        hgrn_norm[0], hgrn_w_o[0], w_out[0], ln1_g[0], ln1_b[0], moe_w_router[0])
    cond_rows = -(-(1 + bs) // SUBLANES) * SUBLANES
    cond = jnp.concatenate([c_ctx[None], c, jnp.zeros((cond_rows - 1 - bs, d), F32)], axis=0)
    mod = _adaln(cond, w_ada[0], b_ada[0])

    xs = [x_prompt.reshape(bp * tp, d), x_sample.reshape(bs * tsq, d)]
    dims = [(bp, tp), (bs, tsq)]
    rows = [(0, False), (1, True)]
    ropes = [None, _rope_tables(tsq)]
    kpe_c = jnp.pad(cache_kpe[:, 0].reshape(bs * past, MLA_ROPE), ((0, 0), (0, LANES - MLA_ROPE)))
    caches = [None, _kvup(cache_ckv[:, 0].reshape(bs * past, -1), kpe_c, wts)]
    inits = [None, state_hgrn[:, 0]]

    x1s, h2s, affs, extras = [], [], [], []
    for gi in range(2):
        (bt, sq), (row0, per_batch) = dims[gi], rows[gi]
        if sq == TOK_BLOCK and caches[gi] is None and ropes[gi] is None and not per_batch:
            x1, h2, aff, ckv, kpe, s_fin = _mixer(xs[gi], bt, sq, mod, hgrn_gamma, wts, alpha, ne, row0)
        else:
            q, k, v, ckv, kpe, hgx = _inproj(xs[gi], bt, sq, mod, hgrn_gamma, wts, row0, per_batch, ropes[gi])
            o_mla = _attn(q, k, v, bt, sq, caches[gi])
            o_f, o_b, s_fin = _hgrn(hgx, bt, sq, inits[gi])
            x1, h2, aff = _postmix(xs[gi], bt, sq, mod, o_f, o_b, hgx, o_mla, wts, alpha, ne, row0, per_batch)
        x1s.append(x1)
        h2s.append(h2)
        affs.append(aff)
        extras.append((ckv, kpe, s_fin))

    caps = [EC_FACTOR * x.shape[0] // ne for x in xs]
    slots = sum(caps)
    assert slots % SUBLANES == 0 and slots >= WIN_ROWS
    groups, scheds = [], []
    slot0 = 0
    for gi in range(2):
        rank, cnt = _route(affs[gi], caps[gi])
        groups.append(dict(h2=h2s[gi], rank=rank, aff=affs[gi], slot0=slot0))
        scheds.append(_window_sched(cnt, slot0))
        slot0 += caps[gi]
    xe = _compact(groups, *[jnp.concatenate([s[k] for s in scheds]) for k in range(3)], slots)
    ye = _ffn(xe, moe_w1[0], moe_w3[0], moe_w2[0], slots, ft=512)

    outs = []
    for gi in range(2):
        outs.append(_combine(ye, groups[gi]["rank"], scheds[gi][0], scheds[gi][2], x1s[gi], mod,
                             ln2_g[0].reshape(1, -1), ln2_b[0].reshape(1, -1), alpha, groups[gi]["slot0"],
                             rows[gi][0], dims[gi][1] // TOK_BLOCK if rows[gi][1] else 1 << 30))

    ckv_p, kpe_p, st_p = extras[0]
    y_prompt = outs[0].reshape(bp, tp, d)
    y_sample = outs[1].reshape(bs, tsq, d)
    new_ckv = ckv_p.reshape(bp, 1, tp, -1)
    new_kpe = kpe_p[:, :MLA_ROPE].reshape(bp, 1, tp, MLA_ROPE)
    new_state = st_p.reshape(bp, 1, 2, HG_HEADS, HG_DK, HG_DV)
    return (y_prompt, y_sample, new_ckv, new_kpe, new_state)
```

```python
import functools

import jax
import jax.numpy as jnp
import numpy as np
from jax import lax
from jax.experimental import pallas as pl
from jax.experimental.pallas import tpu as pltpu

F32 = jnp.float32
BF16 = jnp.bfloat16

MLA_HEADS = 8
MLA_NOPE = 64
MLA_ROPE = 32
MLA_V = 64
HG_HEADS = 4
HG_DK = 128
HG_DV = 128
HG_CHUNK = 32
GRID_W = 64
ROPE_BASE = 10000.0
EC_FACTOR = 2
EPS = 1e-6

LANES = 128
SUBLANES = 8
BF16_ROWS = 16
VMEM_LIMIT = 56 * 1024 * 1024

TOK_BLOCK = 256
WIN_ROWS = 64
UNSELECTED = -(1 << 30)

NT_DIMS = (((1,), (1,)), ((), ()))


def _dot(a, b):
    return jnp.dot(a, b, preferred_element_type=F32)


def _dot_nt(a, b):
    return lax.dot_general(a, b, NT_DIMS, preferred_element_type=F32)


def _silu(x):
    return x * jax.nn.sigmoid(x)


def _params(*sem):
    return pltpu.CompilerParams(dimension_semantics=sem, vmem_limit_bytes=VMEM_LIMIT)


def _const_spec(shape):
    zeros = (0,) * len(shape)
    return pl.BlockSpec(shape, lambda *_: zeros, pipeline_mode=pl.Buffered(1))


def _adaln_kernel(c_ref, w_ref, b_ref, o_ref):
    s = _silu(c_ref[...]).astype(BF16)
    o_ref[...] = _dot(s, w_ref[...].astype(BF16)) + b_ref[...]


def _adaln(cond, w_ada, b_ada):
    rows, d = cond.shape
    n = w_ada.shape[1]
    tn = n // 4
    return pl.pallas_call(
        _adaln_kernel,
        out_shape=jax.ShapeDtypeStruct((rows, n), F32),
        grid=(n // tn,),
        in_specs=[_const_spec((rows, d)),
                  pl.BlockSpec((d, tn), lambda j: (0, j)),
                  pl.BlockSpec((1, tn), lambda j: (0, j))],
        out_specs=pl.BlockSpec((rows, tn), lambda j: (0, j)),
        compiler_params=_params("arbitrary"),
        name="adaln",
    )(cond, w_ada, b_ada.reshape(1, n))


def _rms(x, g):
    return x * lax.rsqrt(jnp.mean(x * x, axis=-1, keepdims=True) + EPS) * g


def _rope(x, c, s):
    w = x.shape[-1]
    lane = lax.broadcasted_iota(jnp.int32, x.shape, 1)
    nxt = pltpu.roll(x, w - 1, 1)
    prv = pltpu.roll(x, 1, 1)
    return x * c + jnp.where(lane % 2 == 0, nxt, prv) * s


N_INPROJ_WEIGHTS = 7


def _mod_row(mod_ref, row0, per_batch):
    r = row0 + pl.program_id(0) if per_batch else row0
    return mod_ref[pl.ds(r, 1), :]


def _modulated(x_ref, m):
    d = x_ref.shape[1]
    return (x_ref[...] * (1.0 + m[:, d:2 * d]) + m[:, 0:d]).astype(BF16)


def _inproj_kernel(*refs, row0, per_batch, rope):
    x_ref, mod_ref = refs[:2]
    _inproj_body(x_ref, _mod_row(mod_ref, row0, per_batch), *refs[2:], rope=rope)


def _inproj_body(x_ref, m, *refs, rope):
    gam_ref, win_ref, qn_ref, wuq_ref, kvn_ref, wk_ref, wv_ref, wpe_ref = refs[:1 + N_INPROJ_WEIGHTS]
    refs = refs[1 + N_INPROJ_WEIGHTS:]
    if rope:
        cq_ref, sq_ref, ck_ref, sk_ref = refs[:4]
        refs = refs[4:]
    q_o, k_o, v_o, ckv_o, kpe_o, hgx_o = refs
    h = _modulated(x_ref, m)
    hw = HG_HEADS * HG_DK
    o_kv = qn_ref.shape[1]
    o_pe = o_kv + kvn_ref.shape[1]
    o_h = o_pe + MLA_ROPE

    cq = _rms(_dot_nt(h, win_ref[0:o_kv, :]), qn_ref[...])
    q = _dot(cq.astype(BF16), wuq_ref[...])
    if rope:
        q = _rope(q, jnp.concatenate([cq_ref[...]] * MLA_HEADS, axis=1),
                  jnp.concatenate([sq_ref[...]] * MLA_HEADS, axis=1))
    q_o[...] = q.astype(BF16)

    ckv = _rms(_dot_nt(h, win_ref[o_kv:o_pe, :]), kvn_ref[...])
    ckv_o[...] = ckv
    kpe = _dot_nt(h, win_ref[o_pe:o_h, :])
    kpe = jnp.concatenate([kpe, jnp.zeros((kpe.shape[0], LANES - MLA_ROPE), F32)], axis=1)
    if rope:
        kpe = _rope(kpe, ck_ref[...], sk_ref[...])
    kpe_o[...] = kpe
    cb = ckv.astype(BF16)
    k_o[...] = (_dot(cb, wk_ref[...]) + _dot(kpe.astype(BF16), wpe_ref[...])).astype(BF16)
    v_o[...] = _dot(cb, wv_ref[...]).astype(BF16)

    z = _dot_nt(h, win_ref[o_h:o_h + 5 * hw, :])
    hgx_o[:, 0:hw] = _silu(z[:, 0:hw])
    for dr in range(2):
        g0, g1 = gam_ref[dr, 0:1, :], gam_ref[dr, 1:2, :]
        gmax = jnp.maximum(g0, g1)
        e0, e1 = jnp.exp(g0 - gmax), jnp.exp(g1 - gmax)
        lb = e0 / (e0 + e1)
        f = lb + (1.0 - lb) * jax.nn.sigmoid(z[:, (1 + dr) * hw:(2 + dr) * hw])
        hgx_o[:, (1 + 2 * dr) * hw:(2 + 2 * dr) * hw] = jnp.log(f)
        hgx_o[:, (2 + 2 * dr) * hw:(3 + 2 * dr) * hw] = 1.0 - f
    hgx_o[:, 5 * hw:6 * hw] = z[:, 3 * hw:4 * hw]
    hgx_o[:, 6 * hw:7 * hw] = z[:, 4 * hw:5 * hw]


def _inproj(x2d, batch, seq, mod, gamma, wts, row0, per_batch, rope_tabs):
    n, d = x2d.shape
    tm = TOK_BLOCK
    nblk = seq // tm
    rope = rope_tabs is not None
    hp = MLA_HEADS * LANES
    hw = HG_HEADS * HG_DK
    tok = lambda b, i: (b * nblk + i, 0)
    pos = lambda b, i: (i, 0)
    weights = [wts[k] for k in INPROJ_KEYS]
    ins = [x2d, mod, gamma] + weights
    in_specs = ([pl.BlockSpec((tm, d), tok), _const_spec(mod.shape), _const_spec(gamma.shape)]
                + [_const_spec(w.shape) for w in weights])
    if rope:
        ins += list(rope_tabs)
        in_specs += [pl.BlockSpec((tm, t.shape[1]), pos) for t in rope_tabs]
    widths = [(hp, BF16), (hp, BF16), (hp, BF16), (wts["kvn"].shape[1], F32), (LANES, F32),
              (7 * hw, F32)]
    return pl.pallas_call(
        functools.partial(_inproj_kernel, row0=row0, per_batch=per_batch, rope=rope),
        out_shape=[jax.ShapeDtypeStruct((n, w), dt) for w, dt in widths],
        grid=(batch, nblk),
        in_specs=in_specs,
        out_specs=[pl.BlockSpec((tm, w), tok) for w, _ in widths],
        compiler_params=_params("arbitrary", "arbitrary"),
        name="inproj",
    )(*ins)


def _kvup_kernel(ckv_ref, kpe_ref, wk_ref, wv_ref, wpe_ref, k_o, v_o):
    cb = ckv_ref[...].astype(BF16)
    k_o[...] = (_dot(cb, wk_ref[...]) + _dot(kpe_ref[...].astype(BF16), wpe_ref[...])).astype(BF16)
    v_o[...] = _dot(cb, wv_ref[...]).astype(BF16)


def _kvup(ckv2d, kpe2d, wts):
    n = ckv2d.shape[0]
    tm = TOK_BLOCK
    hp = MLA_HEADS * LANES
    row = lambda i: (i, 0)
    ws = [wts["wk"], wts["wv"], wts["wpe"]]
    return pl.pallas_call(
        _kvup_kernel,
        out_shape=[jax.ShapeDtypeStruct((n, hp), BF16)] * 2,
        grid=(n // tm,),
        in_specs=[pl.BlockSpec((tm, ckv2d.shape[1]), row), pl.BlockSpec((tm, LANES), row)]
                 + [_const_spec(w.shape) for w in ws],
        out_specs=[pl.BlockSpec((tm, hp), row)] * 2,
        compiler_params=_params("arbitrary"),
        name="kvup",
    )(ckv2d, kpe2d, *ws)


ATTN_SCALE = (MLA_NOPE + MLA_ROPE) ** -0.5


def _attn_kernel(*refs, cached):
    if cached:
        q_ref, k_ref, v_ref, kc_ref, vc_ref, o_ref = refs
    else:
        q_ref, k_ref, v_ref, o_ref = refs
        kc_ref = vc_ref = None
    _attn_body(q_ref, k_ref, v_ref, kc_ref, vc_ref, o_ref)


def _attn_body(q_ref, k_ref, v_ref, kc_ref, vc_ref, o_ref):
    cached = kc_ref is not None
    scale = ATTN_SCALE
    for hd in range(MLA_HEADS):
        sl = slice(hd * LANES, (hd + 1) * LANES)
        q = q_ref[:, sl]
        s = _dot_nt(q, k_ref[:, sl]) * scale
        mx = jnp.max(s, axis=-1, keepdims=True)
        if cached:
            s2 = _dot_nt(q, kc_ref[:, sl]) * scale
            mx = jnp.maximum(mx, jnp.max(s2, axis=-1, keepdims=True))
        e = jnp.exp(s - mx)
        den = jnp.sum(e, axis=-1, keepdims=True)
        o = _dot(e.astype(BF16), v_ref[:, sl])
        if cached:
            e2 = jnp.exp(s2 - mx)
            den = den + jnp.sum(e2, axis=-1, keepdims=True)
            o = o + _dot(e2.astype(BF16), vc_ref[:, sl])
        o_ref[:, sl] = (o / den).astype(o_ref.dtype)


def _attn(q, k, v, batch, seq, cache=None):
    n, hp = q.shape
    tq = TOK_BLOCK
    nblk = seq // tq
    ins = [q, k, v]
    in_specs = [pl.BlockSpec((tq, hp), lambda b, i: (b * nblk + i, 0)),
                pl.BlockSpec((seq, hp), lambda b, i: (b, 0)),
                pl.BlockSpec((seq, hp), lambda b, i: (b, 0))]
    if cache is not None:
        past = cache[0].shape[0] // batch
        ins += list(cache)
        in_specs += [pl.BlockSpec((past, hp), lambda b, i: (b, 0))] * 2
    return pl.pallas_call(
        functools.partial(_attn_kernel, cached=cache is not None),
        out_shape=jax.ShapeDtypeStruct((n, hp), BF16),
        grid=(batch, nblk),
        in_specs=in_specs,
        out_specs=pl.BlockSpec((tq, hp), lambda b, i: (b * nblk + i, 0)),
        compiler_params=_params("arbitrary", "arbitrary"),
        name="attn",
    )(*ins)


def _split3(x):
    h1 = x.astype(BF16)
    r1 = x - h1.astype(F32)
    h2 = r1.astype(BF16)
    h3 = (r1 - h2.astype(F32)).astype(BF16)
    return h1, h2, h3


def _hgrn_kernel(*refs, has_init):
    fwd, bwd = refs[0:4], refs[4:8]
    refs = refs[8:]
    s0_ref = None
    if has_init:
        s0_ref = refs[0]
        refs = refs[1:]
    of_ref, ob_ref, sfin_ref, st_scr = refs
    i = pl.program_id(1)
    _hgrn_body(fwd, bwd, s0_ref, of_ref, ob_ref, sfin_ref, st_scr, i == 0, i == pl.num_programs(1) - 1)


def _hgrn_body(fwd, bwd, s0_ref, of_ref, ob_ref, sfin_ref, st_scr, first, last):
    tm = fwd[0].shape[0]
    c = HG_CHUNK
    nch = tm // c
    dk, dv = HG_DK, HG_DV
    hw = HG_HEADS * dk

    def initial(dr, hd):
        return s0_ref[0, dr, hd].T if s0_ref is not None else jnp.zeros((dv, dk), F32)

    if st_scr is not None:
        @pl.when(first)
        def _init():
            for dr in range(2):
                for hd in range(HG_HEADS):
                    st_scr[dr, hd] = initial(dr, hd)

    row = lax.broadcasted_iota(jnp.int32, (tm, tm), 0)
    col = lax.broadcasted_iota(jnp.int32, (tm, tm), 1)
    same = (row // c) == (col // c)
    bd = (lax.broadcasted_iota(jnp.int32, (tm, nch * dk), 0) // c
          == lax.broadcasted_iota(jnp.int32, (tm, nch * dk), 1) // dk)

    for dr, (hq_ref, lf_ref, kk_ref, vv_ref) in enumerate((fwd, bwd)):
        o_ref = of_ref if dr == 0 else ob_ref
        tri = same & ((col <= row) if dr == 0 else (col >= row))
        tri_b = jnp.where(tri, 1.0, 0.0).astype(BF16)
        h1, h2, h3 = _split3(lf_ref[...])
        bcum = _dot(tri_b, h1) + _dot(tri_b, h2) + _dot(tri_b, h3)
        closing = c - 1 if dr == 0 else 0
        btot3 = bcum.reshape(nch, c, hw)[:, closing:closing + 1, :]
        btot = jnp.broadcast_to(btot3, (nch, c, hw)).reshape(tm, hw)
        dec3 = jnp.exp(btot3)
        kk = kk_ref[...]
        qd = hq_ref[...] * jnp.exp(bcum)
        kd = kk * jnp.exp(-bcum)
        ke = kk * jnp.exp(btot - bcum)
        vv = vv_ref[...]
        order = range(nch) if dr == 0 else range(nch - 1, -1, -1)
        for hd in range(HG_HEADS):
            sl = slice(hd * dk, (hd + 1) * dk)
            qd_h = qd[:, sl]
            v_h = vv[:, hd * dv:(hd + 1) * dv]
            a = jnp.where(tri, _dot_nt(qd_h.astype(BF16), kd[:, sl].astype(BF16)), 0.0)
            o_intra = _dot(a.astype(BF16), v_h.astype(BF16))
            kebd = jnp.where(bd, jnp.concatenate([ke[:, sl]] * nch, axis=1), 0.0).astype(BF16)
            qbd = jnp.where(bd, jnp.concatenate([qd_h] * nch, axis=1), 0.0).astype(BF16)
            ut = _dot(v_h.T.astype(BF16), kebd)
            st = st_scr[dr, hd] if st_scr is not None else initial(dr, hd)
            prev = [None] * nch
            for n in order:
                prev[n] = st
                st = st * dec3[n][:, sl] + ut[:, n * dk:(n + 1) * dk]
            if st_scr is not None:
                st_scr[dr, hd] = st
            o_inter = _dot_nt(qbd, jnp.concatenate(prev, axis=1).astype(BF16))
            o_ref[:, hd * dv:(hd + 1) * dv] = o_intra + o_inter

            if last is True:
                sfin_ref[0, dr, hd] = st.T
            else:
                @pl.when(last)
                def _final(st=st, dr=dr, hd=hd):
                    sfin_ref[0, dr, hd] = st.T


def _hgrn(hgx, batch, seq, s0=None):
    n = hgx.shape[0]
    tm = TOK_BLOCK
    nblk = seq // tm
    hw = HG_HEADS * HG_DK

    def spec(lane_blk, rev):
        if rev:
            return pl.BlockSpec((tm, hw), lambda b, i: (b * nblk + nblk - 1 - i, lane_blk))
        return pl.BlockSpec((tm, hw), lambda b, i: (b * nblk + i, lane_blk))

    in_specs = [spec(0, False), spec(1, False), spec(2, False), spec(5, False),
                spec(0, True), spec(3, True), spec(4, True), spec(5, True)]
    ins = [hgx] * 8
    st_shape = (1, 2, HG_HEADS, HG_DK, HG_DV)
    st_spec = pl.BlockSpec(st_shape, lambda b, i: (b, 0, 0, 0, 0))
    if s0 is not None:
        ins.append(s0)
        in_specs.append(st_spec)
    return pl.pallas_call(
        functools.partial(_hgrn_kernel, has_init=s0 is not None),
        out_shape=[jax.ShapeDtypeStruct((n, hw), F32), jax.ShapeDtypeStruct((n, hw), F32),
                   jax.ShapeDtypeStruct((batch,) + st_shape[1:], F32)],
        grid=(batch, nblk),
        in_specs=in_specs,
        out_specs=[spec(0, False), spec(0, True), st_spec],
        scratch_shapes=[pltpu.VMEM((2, HG_HEADS, HG_DV, HG_DK), F32)],
        compiler_params=_params("arbitrary", "arbitrary"),
        name="hgrn",
    )(*ins)


def _layer_norm(x, g, b):
    xc = x - jnp.mean(x, axis=-1, keepdims=True)
    var = jnp.mean(xc * xc, axis=-1, keepdims=True)
    return xc * lax.rsqrt(var + EPS) * g + b


N_POSTMIX_WEIGHTS = 8
INPROJ_KEYS = ("win", "qn", "wuq", "kvn", "wk", "wv", "wpe")
POSTMIX_KEYS = ("hgn", "womla", "wohg", "wout", "ln1g", "ln1b", "wr")
MIXER_SEQS = 2


def _postmix_kernel(x_ref, mod_ref, *refs, alpha, row0, per_batch):
    _postmix_body(x_ref, _mod_row(mod_ref, row0, per_batch), *refs, alpha=alpha)


def _postmix_body(x_ref, m, of_ref, ob_ref, zg_ref, om_ref, wg_ref, hgn_ref, womla_ref,
                  wohg_ref, wout_ref, lng_ref, lnb_ref, wr_ref, x1_o, h2_o, aff_o, *, alpha):
    d = x_ref.shape[1]
    tb = aff_o.shape[2]
    g1, sh2, sc2 = m[:, 2 * d:3 * d], m[:, 3 * d:4 * d], m[:, 4 * d:5 * d]
    o = of_ref[...] + ob_ref[...]
    zg = zg_ref[...]
    parts = []
    for hd in range(HG_HEADS):
        sl = slice(hd * HG_DV, (hd + 1) * HG_DV)
        parts.append(_rms(o[:, sl], hgn_ref[...]) * _silu(zg[:, sl]))
    ohg = jnp.concatenate(parts, axis=1).astype(BF16)
    gates = _dot_nt(_modulated(x_ref, m), wg_ref[wg_ref.shape[0] - 2 * d:, :])
    merged = (jax.nn.sigmoid(gates[:, 0:d]) * _dot(om_ref[...], womla_ref[...])
              + jax.nn.sigmoid(gates[:, d:2 * d]) * _dot(ohg, wohg_ref[...]))
    mix = _dot(merged.astype(BF16), wout_ref[...])
    x1 = _layer_norm(alpha * x_ref[...] + g1 * mix, lng_ref[...], lnb_ref[...])
    x1_o[...] = x1
    h2 = (x1 * (1.0 + sc2) + sh2).astype(BF16)
    h2_o[...] = h2
    logits = _dot_nt(wr_ref[...], h2)
    e = jnp.exp(logits - jnp.max(logits, axis=0, keepdims=True))
    aff = e / jnp.sum(e, axis=0, keepdims=True)
    for blk in range(aff_o.shape[0]):
        aff_o[blk] = aff[:, blk * tb:(blk + 1) * tb]


def _postmix(x2d, batch, seq, mod, o_f, o_b, hgx, o_mla, wts, alpha, n_experts, row0, per_batch):
    n, d = x2d.shape
    tm = TOK_BLOCK
    nblk = seq // tm
    hw = HG_HEADS * HG_DV
    tok = lambda b, i: (b * nblk + i, 0)
    weights = [wts[k] for k in POSTMIX_KEYS]
    return pl.pallas_call(
        functools.partial(_postmix_kernel, alpha=alpha, row0=row0, per_batch=per_batch),
        out_shape=[jax.ShapeDtypeStruct((n, d), F32), jax.ShapeDtypeStruct((n, d), BF16),
                   jax.ShapeDtypeStruct((n // tm, n_experts, tm), F32)],
        grid=(batch, nblk),
        in_specs=[pl.BlockSpec((tm, d), tok), _const_spec(mod.shape),
                  pl.BlockSpec((tm, hw), tok), pl.BlockSpec((tm, hw), tok),
                  pl.BlockSpec((tm, hw), lambda b, i: (b * nblk + i, 6)),
                  pl.BlockSpec((tm, o_mla.shape[1]), tok), _const_spec(wts["win"].shape)]
                 + [_const_spec(w.shape) for w in weights],
        out_specs=[pl.BlockSpec((tm, d), tok), pl.BlockSpec((tm, d), tok),
                   pl.BlockSpec((1, n_experts, tm), lambda b, i: (b * nblk + i, 0, 0))],
        compiler_params=_params("arbitrary", "arbitrary"),
        name="postmix",
    )(x2d, mod, o_f, o_b, hgx, o_mla, wts["win"], *weights)


def _mixer_kernel(x_ref, mod_ref, *refs, alpha, row0, seq):
    nw = 1 + N_INPROJ_WEIGHTS
    in_w, refs = refs[:nw], refs[nw:]
    pm_w, refs = refs[:N_POSTMIX_WEIGHTS - 1], refs[N_POSTMIX_WEIGHTS - 1:]
    pm_w = (in_w[1],) + tuple(pm_w)
    x1_o, h2_o, aff_o, ckv_o, kpe_o, sfin_o, q_s, k_s, v_s, hgx_s, om_s, of_s, ob_s = refs
    m = _mod_row(mod_ref, row0, False)
    _inproj_body(x_ref, m, *in_w, q_s, k_s, v_s, ckv_o, kpe_o, hgx_s, rope=False)
    hw = HG_HEADS * HG_DK
    for s in range(x_ref.shape[0] // seq):
        rows = slice(s * seq, (s + 1) * seq)
        _attn_body(q_s.at[rows], k_s.at[rows], v_s.at[rows], None, None, om_s.at[rows])
        lane = lambda j: hgx_s.at[rows, j * hw:(j + 1) * hw]
        _hgrn_body((lane(0), lane(1), lane(2), lane(5)), (lane(0), lane(3), lane(4), lane(5)), None,
                   of_s.at[rows], ob_s.at[rows], sfin_o.at[s:s + 1], None, True, True)
    _postmix_body(x_ref, m, of_s, ob_s, hgx_s.at[:, 6 * hw:7 * hw], om_s, *pm_w, x1_o, h2_o, aff_o,
                  alpha=alpha)


def _mixer(x2d, batch, seq, mod, gamma, wts, alpha, n_experts, row0):
    n, d = x2d.shape
    assert seq == TOK_BLOCK
    ns = MIXER_SEQS if batch % MIXER_SEQS == 0 else 1
    tm = ns * seq
    hp = MLA_HEADS * LANES
    hw = HG_HEADS * HG_DK
    kvl = wts["kvn"].shape[1]
    weights = [wts[k] for k in INPROJ_KEYS + POSTMIX_KEYS]
    tok = lambda b: (b, 0)
    st_shape = (ns, 2, HG_HEADS, HG_DK, HG_DV)
    return pl.pallas_call(
        functools.partial(_mixer_kernel, alpha=alpha, row0=row0, seq=seq),
        out_shape=[jax.ShapeDtypeStruct((n, d), F32), jax.ShapeDtypeStruct((n, d), BF16),
                   jax.ShapeDtypeStruct((n // seq, n_experts, seq), F32),
                   jax.ShapeDtypeStruct((n, kvl), F32), jax.ShapeDtypeStruct((n, LANES), F32),
                   jax.ShapeDtypeStruct((batch,) + st_shape[1:], F32)],
        grid=(batch // ns,),
        in_specs=[pl.BlockSpec((tm, d), tok), _const_spec(mod.shape), _const_spec(gamma.shape)]
                 + [_const_spec(w.shape) for w in weights],
        out_specs=[pl.BlockSpec((tm, d), tok), pl.BlockSpec((tm, d), tok),
                   pl.BlockSpec((ns, n_experts, seq), lambda b: (b, 0, 0)),
                   pl.BlockSpec((tm, kvl), tok), pl.BlockSpec((tm, LANES), tok),
                   pl.BlockSpec(st_shape, lambda b: (b, 0, 0, 0, 0))],
        scratch_shapes=[pltpu.VMEM((tm, hp), BF16)] * 3
                       + [pltpu.VMEM((tm, 7 * hw), F32), pltpu.VMEM((tm, hp), BF16),
                          pltpu.VMEM((tm, hw), F32), pltpu.VMEM((tm, hw), F32)],
        compiler_params=_params("arbitrary"),
        name="mixer",
    )(x2d, mod, gamma, *weights)


def _route_kernel(aff_ref, rank_o, cnt_o, *, cap):
    nb, ne, tb = aff_ref.shape
    key = aff_ref[...]

    def count(mask):
        return jnp.sum(jnp.sum(jnp.where(mask, 1.0, 0.0), axis=0), axis=1, keepdims=True)

    def bit_step(it, bits):
        cand = bits | jnp.left_shift(jnp.int32(1), 30 - it)
        return jnp.where(count(key >= pltpu.bitcast(cand, F32)[None]) >= cap, cand, bits)

    bits = lax.fori_loop(0, 31, bit_step, jnp.zeros((ne, 1), jnp.int32))
    thr = pltpu.bitcast(bits, F32)
    need = cap - count(key > thr[None])
    before = (lax.broadcasted_iota(jnp.int32, (tb, tb), 0)
              < lax.broadcasted_iota(jnp.int32, (tb, tb), 1))
    before = jnp.where(before, 1.0, 0.0).astype(BF16)
    off_eq = jnp.zeros((ne, 1), F32)
    off_sel = jnp.zeros((ne, 1), F32)
    cnt_o[...] = jnp.zeros_like(cnt_o)
    for blk in range(nb):
        key_b = key[blk]
        eq = key_b == thr
        eq_b = jnp.where(eq, 1.0, 0.0)
        eq_rank = _dot(eq_b.astype(BF16), before) + off_eq
        sel = (key_b > thr) | (eq & (eq_rank < need))
        sel_b = jnp.where(sel, 1.0, 0.0)
        rank = _dot(sel_b.astype(BF16), before) + off_sel
        rank_o[blk] = jnp.where(sel, rank.astype(jnp.int32), UNSELECTED)
        cnt_o[:, blk:blk + 1] = off_sel.astype(jnp.int32)
        off_eq = off_eq + jnp.sum(eq_b, axis=1, keepdims=True)
        off_sel = off_sel + jnp.sum(sel_b, axis=1, keepdims=True)
    cnt_o[:, nb:nb + 1] = off_sel.astype(jnp.int32)


def _route(aff, cap):
    nb, ne, tb = aff.shape
    assert nb + 1 <= LANES
    rank, cnt = pl.pallas_call(
        functools.partial(_route_kernel, cap=cap),
        out_shape=[jax.ShapeDtypeStruct(aff.shape, jnp.int32), jax.ShapeDtypeStruct((ne, LANES), jnp.int32)],
        in_specs=[pl.BlockSpec(memory_space=pltpu.VMEM)],
        out_specs=[pl.BlockSpec(memory_space=pltpu.VMEM)] * 2,
        compiler_params=pltpu.CompilerParams(vmem_limit_bytes=VMEM_LIMIT),
        name="route",
    )(aff)
    return rank, cnt[:, :nb + 1]


def _window_hits(rk_ref, firsts, slot0, win):
    ne, tb = rk_ref.shape[1], rk_ref.shape[2]
    win_iota = lax.broadcasted_iota(jnp.int32, (win, tb), 0)
    return [(rk_ref[0, e:e + 1, :] + (slot0 - firsts[e])) == win_iota for e in range(ne)]


def _compact_kernel(first_ref, end_ref, rounds_ref, *refs, groups, slots):
    ng = len(groups)
    h2_refs, rk_refs, af_refs = refs[0:ng], refs[ng:2 * ng], refs[2 * ng:3 * ng]
    xe_hbm, stage, tail, sem, issued = refs[3 * ng:]
    b = pl.program_id(0)
    ne = rk_refs[0].shape[1]
    win = WIN_ROWS
    d = h2_refs[0].shape[1]
    sub = SUBLANES

    def copies(slot, dsts):
        return [pltpu.make_async_copy(stage.at[slot, pl.ds(e * win, win), :],
                                      xe_hbm.at[e, pl.ds(pl.multiple_of(dsts[e], sub), win), :], sem.at[e])
                for e in range(ne)]

    def wait_previous():
        @pl.when(issued[0] > 0)
        def _():
            for cp in copies(0, [0] * ne):
                cp.wait()

    @pl.when(b == 0)
    def _init():
        issued[0] = 0
        tail[...] = jnp.zeros_like(tail)
        stage[1] = jnp.zeros(stage.shape[1:], stage.dtype)
        pad = copies(1, [slots] * ne)
        for cp in pad:
            cp.start()
        for cp in pad:
            cp.wait()

    def group_body(h2_ref, rk_ref, af_ref, slot0):
        firsts = [first_ref[b * ne + e] for e in range(ne)]
        bases = [(f // sub) * sub for f in firsts]
        ends = [end_ref[b * ne + e] - bases[e] for e in range(ne)]
        sub_iota = lax.broadcasted_iota(jnp.int32, (sub, stage.shape[2]), 0)

        def one_round(r, carry):
            dsts = [bases[e] + r * win for e in range(ne)]
            hits = _window_hits(rk_ref, dsts, slot0, win)
            onehot = jnp.where(jnp.concatenate(hits, axis=0), 1.0, 0.0).astype(BF16)
            rows = _dot(onehot, h2_ref[...])
            gate = jnp.concatenate(
                [jnp.sum(jnp.where(hits[e], af_ref[0, e:e + 1, :], 0.0), axis=1, keepdims=True)
                 for e in range(ne)], axis=0)
            slot = issued[0] % 2
            stage[slot, :, 0:d] = rows
            stage[slot, :, d:] = jnp.broadcast_to(gate, (ne * win, LANES))
            for e in range(ne):
                @pl.when(r == 0)
                def _head(e=e):
                    head = stage[slot, e * win:e * win + sub, :]
                    stage[slot, e * win:e * win + sub, :] = jnp.where(
                        sub_iota < firsts[e] - bases[e], tail[e * sub:(e + 1) * sub, :], head)

                last = (ends[e] // sub) * sub
                @pl.when(r == last // win)
                def _tail(e=e, last=last):
                    tail[e * sub:(e + 1) * sub, :] = stage[
                        slot, pl.ds(pl.multiple_of(e * win + last % win, sub), sub), :]
            wait_previous()
            for cp in copies(slot, [jnp.minimum(dst, slots) for dst in dsts]):
                cp.start()
            issued[0] = issued[0] + 1
            return carry

        lax.fori_loop(0, rounds_ref[b], one_round, 0)

    blk0 = 0
    for gi, g in enumerate(groups):
        @pl.when((b >= blk0) & (b < blk0 + g["nb"]))
        def _(gi=gi, g=g):
            group_body(h2_refs[gi], rk_refs[gi], af_refs[gi], g["slot0"])
        blk0 += g["nb"]

    @pl.when(b == pl.num_programs(0) - 1)
    def _drain():
        wait_previous()


def _compact(groups, first, end, rounds, slots):
    d = groups[0]["h2"].shape[1]
    nbs = [g["rank"].shape[0] for g in groups]
    ne, tb = groups[0]["rank"].shape[1:]
    meta, specs_h2, specs_rk = [], [], []
    blk0 = 0
    for g, nb in zip(groups, nbs):
        meta.append(dict(nb=nb, slot0=g["slot0"]))
        local = lambda b, *_, blk0=blk0, nb=nb: jnp.clip(b - blk0, 0, nb - 1)
        specs_h2.append(pl.BlockSpec((tb, d), lambda b, *_, local=local: (local(b), 0)))
        specs_rk.append(pl.BlockSpec((1, ne, tb), lambda b, *_, local=local: (local(b), 0, 0)))
        blk0 += nb
    width = d + LANES
    return pl.pallas_call(
        functools.partial(_compact_kernel, groups=meta, slots=slots),
        out_shape=jax.ShapeDtypeStruct((ne, slots + WIN_ROWS, width), F32),
        grid_spec=pltpu.PrefetchScalarGridSpec(
            num_scalar_prefetch=3,
            grid=(sum(nbs),),
            in_specs=specs_h2 + specs_rk + specs_rk,
            out_specs=pl.BlockSpec(memory_space=pl.ANY),
            scratch_shapes=[pltpu.VMEM((2, ne * WIN_ROWS, width), F32),
                            pltpu.VMEM((ne * SUBLANES, width), F32),
                            pltpu.SemaphoreType.DMA((ne,)), pltpu.SMEM((1,), jnp.int32)]),
        compiler_params=_params("arbitrary"),
        name="compact",
    )(first, end, rounds, *[g["h2"] for g in groups], *[g["rank"] for g in groups],
      *[g["aff"] for g in groups])


def _ffn_kernel(xe_ref, w1_ref, w3_ref, w2_ref, ye_ref, x_scr, g_scr):
    f = pl.program_id(1)
    d = x_scr.shape[1]

    @pl.when(f == 0)
    def _unpack():
        x_scr[...] = xe_ref[0, :, 0:d].astype(BF16)
        g_scr[...] = xe_ref[0, :, d:d + 1]

    x = x_scr[...]
    hid = _silu(_dot(x, w1_ref[0].astype(BF16))) * _dot(x, w3_ref[0].astype(BF16))
    y = _dot(hid.astype(BF16), w2_ref[0].astype(BF16))

    last = pl.num_programs(1) - 1

    @pl.when((f == 0) & (f < last))
    def _first():
        ye_ref[0] = y

    @pl.when((f > 0) & (f < last))
    def _middle():
        ye_ref[0] += y

    @pl.when((f == last) & (f > 0))
    def _last():
        ye_ref[0] = (ye_ref[0] + y) * g_scr[...]

    @pl.when((f == last) & (f == 0))
    def _only():
        ye_ref[0] = y * g_scr[...]


def _ffn(xe, w1, w3, w2, slots, ft):
    ne, d, dff = w1.shape
    nf = dff // ft
    return pl.pallas_call(
        _ffn_kernel,
        out_shape=jax.ShapeDtypeStruct((ne, slots, d), F32),
        grid=(ne, nf),
        in_specs=[pl.BlockSpec((1, slots, xe.shape[2]), lambda e, f: (e, 0, 0)),
                  pl.BlockSpec((1, d, ft), lambda e, f: (e, 0, f)),
                  pl.BlockSpec((1, d, ft), lambda e, f: (e, 0, f)),
                  pl.BlockSpec((1, ft, d), lambda e, f: (e, f, 0))],
        out_specs=pl.BlockSpec((1, slots, d), lambda e, f: (e, 0, 0)),
        scratch_shapes=[pltpu.VMEM((slots, d), BF16), pltpu.VMEM((slots, 1), F32)],
        compiler_params=_params("arbitrary", "arbitrary"),
        name="ffn",
    )(xe, w1, w3, w2)


def _combine_kernel(first_ref, rounds_ref, rk_ref, x1_ref, mod_ref, lng_ref, lnb_ref, ye_hbm, out_ref,
                    buf, acc_scr, sem, *, d_model, alpha, slot0, slots, row0, blocks_per_batch):
    d = d_model
    b = pl.program_id(0)
    nblk = pl.num_programs(0)
    ne, tb = rk_ref.shape[1], rk_ref.shape[2]
    win = WIN_ROWS
    eye = (lax.broadcasted_iota(jnp.int32, (tb, tb), 0)
           == lax.broadcasted_iota(jnp.int32, (tb, tb), 1))
    eye = jnp.where(eye, 1.0, 0.0).astype(BF16)

    def starts_of(blk, r):
        firsts = [(first_ref[blk * ne + e] // SUBLANES) * SUBLANES + r * win for e in range(ne)]
        return firsts, [jnp.minimum(f, slots - win) for f in firsts]

    def windows(slot, starts):
        return [pltpu.make_async_copy(ye_hbm.at[e, pl.ds(pl.multiple_of(starts[e], SUBLANES), win), :],
                                      buf.at[slot, pl.ds(e * win, win), :], sem.at[slot, e])
                for e in range(ne)]

    def scatter(slot, firsts, starts):
        hits = _window_hits(rk_ref, starts, slot0, win)
        hits = [h & ((rk_ref[0, e:e + 1, :] + slot0) >= firsts[e]) for e, h in enumerate(hits)]
        hit = jnp.where(jnp.concatenate(hits, axis=0), 1.0, 0.0).astype(BF16)
        hit_t = _dot_nt(eye, hit).astype(BF16)
        y = buf[slot]
        y_hi = y.astype(BF16)
        y_lo = (y - y_hi.astype(F32)).astype(BF16)
        return _dot(hit_t, y_hi) + _dot(hit_t, y_lo)

    cur = b % 2

    @pl.when(b == 0)
    def _prime():
        for cp in windows(0, starts_of(0, 0)[1]):
            cp.start()

    @pl.when(b + 1 < nblk)
    def _prefetch():
        for cp in windows(1 - cur, starts_of(b + 1, 0)[1]):
            cp.start()

    firsts, starts = starts_of(b, 0)
    for cp in windows(cur, starts):
        cp.wait()
    acc_scr[...] = scatter(cur, firsts, starts)

    def extra_round(r, carry):
        firsts, starts = starts_of(b, r)
        for cp in windows(2, starts):
            cp.start()
        for cp in windows(2, starts):
            cp.wait()
        acc_scr[...] += scatter(2, firsts, starts)
        return carry

    lax.fori_loop(1, rounds_ref[b], extra_round, 0)
    r = row0 + b // blocks_per_batch
    g2 = mod_ref[pl.ds(r, 1), :][:, 5 * d:6 * d]
    out_ref[...] = _layer_norm(alpha * x1_ref[...] + g2 * acc_scr[...], lng_ref[...], lnb_ref[...])


def _combine(ye, rank, first, rounds, x1, mod, ln_g, ln_b, alpha, slot0, row0, blocks_per_batch):
    n, d = x1.shape
    nb, ne, tb = rank.shape
    slots = ye.shape[1]
    return pl.pallas_call(
        functools.partial(_combine_kernel, d_model=d, alpha=alpha, slot0=slot0, slots=slots, row0=row0,
                          blocks_per_batch=blocks_per_batch),
        out_shape=jax.ShapeDtypeStruct((n, d), F32),
        grid_spec=pltpu.PrefetchScalarGridSpec(
            num_scalar_prefetch=2,
            grid=(nb,),
            in_specs=[pl.BlockSpec((1, ne, tb), lambda b, *_: (b, 0, 0)),
                      pl.BlockSpec((tb, d), lambda b, *_: (b, 0)),
                      pl.BlockSpec(mod.shape, lambda b, *_: (0, 0)),
                      pl.BlockSpec((1, d), lambda b, *_: (0, 0)),
                      pl.BlockSpec((1, d), lambda b, *_: (0, 0)),
                      pl.BlockSpec(memory_space=pl.ANY)],
            out_specs=pl.BlockSpec((tb, d), lambda b, *_: (b, 0)),
            scratch_shapes=[pltpu.VMEM((3, ne * WIN_ROWS, d), F32), pltpu.VMEM((tb, d), F32),
                            pltpu.SemaphoreType.DMA((3, ne))]),
        compiler_params=_params("arbitrary"),
        name="combine",
    )(first, rounds, rank, x1, mod, ln_g, ln_b, ye)


def _prep_weights(w_in, q_norm, w_uq, kv_norm, w_ukv, w_o_mla, hgrn_norm, w_o_hg, w_out, ln1_g, ln1_b,
                  w_router):
    d = w_in.shape[0]
    q_lora, kv_lora = q_norm.shape[0], kv_norm.shape[0]
    hw = HG_HEADS * HG_DK
    hh, hp = MLA_HEADS, MLA_HEADS * LANES
    o_kv, o_pe = q_lora, q_lora + kv_lora
    o_h = o_pe + MLA_ROPE
    o_g = o_h + 5 * hw
    assert w_in.shape[1] == o_g + 2 * d
    qk = MLA_NOPE + MLA_ROPE
    kvw = MLA_NOPE + MLA_V
    b16 = lambda a: a.astype(BF16)
    assert all(o % BF16_ROWS == 0 for o in (o_kv, o_pe, o_h, o_g))
    win = b16(w_in.T)
    wuq = jnp.pad(w_uq.reshape(q_lora, hh, qk), ((0, 0), (0, 0), (0, LANES - qk))).reshape(q_lora, hp)
    ukv = w_ukv.reshape(kv_lora, hh, kvw)
    wk = jnp.pad(ukv[:, :, :MLA_NOPE], ((0, 0), (0, 0), (0, LANES - MLA_NOPE))).reshape(kv_lora, hp)
    wv = jnp.pad(ukv[:, :, MLA_NOPE:], ((0, 0), (0, 0), (0, LANES - MLA_V))).reshape(kv_lora, hp)
    place = np.pad(np.eye(MLA_ROPE, dtype=np.float32), ((0, LANES - MLA_ROPE), (MLA_NOPE, LANES - qk)))
    wpe = jnp.asarray(np.tile(place, (1, hh)), BF16)
    womla = jnp.pad(w_o_mla.reshape(hh, MLA_V, d), ((0, 0), (0, LANES - MLA_V), (0, 0))).reshape(hp, d)
    return dict(
        win=win, qn=q_norm.reshape(1, -1), wuq=b16(wuq), kvn=kv_norm.reshape(1, -1), wk=b16(wk), wv=b16(wv),
        wpe=wpe, hgn=hgrn_norm.reshape(1, -1), womla=b16(womla), wohg=b16(w_o_hg), wout=b16(w_out),
        ln1g=ln1_g.reshape(1, -1), ln1b=ln1_b.reshape(1, -1), wr=b16(w_router.T))


def _rope_tables(seq):
    n_freq = MLA_ROPE // 4
    inv = ROPE_BASE ** (-np.arange(n_freq, dtype=np.float64) / n_freq)
    t = np.arange(seq)
    ang = np.concatenate([(t // GRID_W)[:, None] * inv, (t % GRID_W)[:, None] * inv], axis=-1)
    cos = np.repeat(np.cos(ang), 2, axis=1)
    sin = np.repeat(np.sin(ang), 2, axis=1) * np.tile([-1.0, 1.0], MLA_ROPE // 2)
    ck = np.pad(cos, ((0, 0), (0, LANES - MLA_ROPE)), constant_values=1.0)
    sk = np.pad(sin, ((0, 0), (0, LANES - MLA_ROPE)))
    cq = np.pad(cos, ((0, 0), (MLA_NOPE, LANES - MLA_NOPE - MLA_ROPE)), constant_values=1.0)
    sq = np.pad(sin, ((0, 0), (MLA_NOPE, LANES - MLA_NOPE - MLA_ROPE)))
    return tuple(jnp.asarray(a, F32) for a in (cq, sq, ck, sk))


def _window_sched(cnt, slot0):
    first = slot0 + cnt[:, :-1]
    end = slot0 + cnt[:, 1:]
    rounds = jnp.max((end - (first // SUBLANES) * SUBLANES + WIN_ROWS - 1) // WIN_ROWS, axis=0)
    flat = lambda a: a.T.reshape(-1).astype(jnp.int32)
    return flat(first), flat(end), jnp.maximum(rounds, 1).astype(jnp.int32)


def kernel(x_prompt, x_sample, c, cache_ckv, cache_kpe, state_hgrn, c_ctx, w_ada, b_ada, w_in, mla_q_norm, mla_w_uq, mla_kv_norm, mla_w_ukv, mla_w_o, hgrn_gamma, hgrn_norm, hgrn_w_o, w_out, ln1_g, ln1_b, moe_w_router, moe_w1, moe_w3, moe_w2, ln2_g, ln2_b):
    depth = w_ada.shape[0]
    assert depth == 1, "single trunk layer"
    bp, tp, d = x_prompt.shape
    bs, tsq, _ = x_sample.shape
    ne = moe_w_router.shape[-1]
    alpha = (2 * depth) ** 0.25
    past = cache_ckv.shape[2]
    assert tp % TOK_BLOCK == 0 and tsq % TOK_BLOCK == 0 and past % TOK_BLOCK == 0 and tsq % GRID_W == 0

    wts = _prep_weights(w_in[0], mla_q_norm[0], mla_w_uq[0], mla_kv_norm[0], mla_w_ukv[0], mla_w_o[0],
                        hgrn_norm[0], hgrn_w_o[0], w_out[0], ln1_g[0], ln1_b[0], moe_w_router[0])
    cond_rows = -(-(1 + bs) // SUBLANES) * SUBLANES
    cond = jnp.concatenate([c_ctx[None], c, jnp.zeros((cond_rows - 1 - bs, d), F32)], axis=0)
    mod = _adaln(cond, w_ada[0], b_ada[0])

    xs = [x_prompt.reshape(bp * tp, d), x_sample.reshape(bs * tsq, d)]
    dims = [(bp, tp), (bs, tsq)]
    rows = [(0, False), (1, True)]
    ropes = [None, _rope_tables(tsq)]
    kpe_c = jnp.pad(cache_kpe[:, 0].reshape(bs * past, MLA_ROPE), ((0, 0), (0, LANES - MLA_ROPE)))
    caches = [None, _kvup(cache_ckv[:, 0].reshape(bs * past, -1), kpe_c, wts)]
    inits = [None, state_hgrn[:, 0]]

    x1s, h2s, affs, extras = [], [], [], []
    for gi in range(2):
        (bt, sq), (row0, per_batch) = dims[gi], rows[gi]
        if sq == TOK_BLOCK and caches[gi] is None and ropes[gi] is None and not per_batch:
            x1, h2, aff, ckv, kpe, s_fin = _mixer(xs[gi], bt, sq, mod, hgrn_gamma, wts, alpha, ne, row0)
        else:
            q, k, v, ckv, kpe, hgx = _inproj(xs[gi], bt, sq, mod, hgrn_gamma, wts, row0, per_batch, ropes[gi])
            o_mla = _attn(q, k, v, bt, sq, caches[gi])
            o_f, o_b, s_fin = _hgrn(hgx, bt, sq, inits[gi])
            x1, h2, aff = _postmix(xs[gi], bt, sq, mod, o_f, o_b, hgx, o_mla, wts, alpha, ne, row0, per_batch)
        x1s.append(x1)
        h2s.append(h2)
        affs.append(aff)
        extras.append((ckv, kpe, s_fin))

    caps = [EC_FACTOR * x.shape[0] // ne for x in xs]
    slots = sum(caps)
    assert slots % SUBLANES == 0 and slots >= WIN_ROWS
    groups, scheds = [], []
    slot0 = 0
    for gi in range(2):
        rank, cnt = _route(affs[gi], caps[gi])
        groups.append(dict(h2=h2s[gi], rank=rank, aff=affs[gi], slot0=slot0))
        scheds.append(_window_sched(cnt, slot0))
        slot0 += caps[gi]
    xe = _compact(groups, *[jnp.concatenate([s[k] for s in scheds]) for k in range(3)], slots)
    ye = _ffn(xe, moe_w1[0], moe_w3[0], moe_w2[0], slots, ft=512)

    outs = []
    for gi in range(2):
        outs.append(_combine(ye, groups[gi]["rank"], scheds[gi][0], scheds[gi][2], x1s[gi], mod,
                             ln2_g[0].reshape(1, -1), ln2_b[0].reshape(1, -1), alpha, groups[gi]["slot0"],
                             rows[gi][0], dims[gi][1] // TOK_BLOCK if rows[gi][1] else 1 << 30))

    ckv_p, kpe_p, st_p = extras[0]
    y_prompt = outs[0].reshape(bp, tp, d)
    y_sample = outs[1].reshape(bs, tsq, d)
    new_ckv = ckv_p.reshape(bp, 1, tp, -1)
    new_kpe = kpe_p[:, :MLA_ROPE].reshape(bp, 1, tp, MLA_ROPE)
    new_state = st_p.reshape(bp, 1, 2, HG_HEADS, HG_DK, HG_DV)
    return (y_prompt, y_sample, new_ckv, new_kpe, new_state)
```

```python
import functools

import jax
import jax.numpy as jnp
import numpy as np
from jax import lax
from jax.experimental import pallas as pl
from jax.experimental.pallas import tpu as pltpu

F32 = jnp.float32
BF16 = jnp.bfloat16

MLA_HEADS = 8
MLA_NOPE = 64
MLA_ROPE = 32
MLA_V = 64
HG_HEADS = 4
HG_DK = 128
HG_DV = 128
HG_CHUNK = 32
GRID_W = 64
ROPE_BASE = 10000.0
EC_FACTOR = 2
EPS = 1e-6

LANES = 128
SUBLANES = 8
BF16_ROWS = 16
VMEM_LIMIT = 56 * 1024 * 1024

TOK_BLOCK = 256
WIN_ROWS = 64
UNSELECTED = -(1 << 30)

NT_DIMS = (((1,), (1,)), ((), ()))


def _dot(a, b):
    return jnp.dot(a, b, preferred_element_type=F32)


def _dot_nt(a, b):
    return lax.dot_general(a, b, NT_DIMS, preferred_element_type=F32)


def _silu(x):
    return x * jax.nn.sigmoid(x)


def _params(*sem):
    return pltpu.CompilerParams(dimension_semantics=sem, vmem_limit_bytes=VMEM_LIMIT)


def _const_spec(shape):
    zeros = (0,) * len(shape)
    return pl.BlockSpec(shape, lambda *_: zeros, pipeline_mode=pl.Buffered(1))


def _adaln_kernel(c_ref, w_ref, b_ref, o_ref):
    s = _silu(c_ref[...]).astype(BF16)
    o_ref[...] = _dot(s, w_ref[...].astype(BF16)) + b_ref[...]


def _adaln(cond, w_ada, b_ada):
    rows, d = cond.shape
    n = w_ada.shape[1]
    tn = n // 4
    return pl.pallas_call(
        _adaln_kernel,
        out_shape=jax.ShapeDtypeStruct((rows, n), F32),
        grid=(n // tn,),
        in_specs=[_const_spec((rows, d)),
                  pl.BlockSpec((d, tn), lambda j: (0, j)),
                  pl.BlockSpec((1, tn), lambda j: (0, j))],
        out_specs=pl.BlockSpec((rows, tn), lambda j: (0, j)),
        compiler_params=_params("arbitrary"),
        name="adaln",
    )(cond, w_ada, b_ada.reshape(1, n))


def _rms(x, g):
    return x * lax.rsqrt(jnp.mean(x * x, axis=-1, keepdims=True) + EPS) * g


def _rope(x, c, s):
    w = x.shape[-1]
    lane = lax.broadcasted_iota(jnp.int32, x.shape, 1)
    nxt = pltpu.roll(x, w - 1, 1)
    prv = pltpu.roll(x, 1, 1)
    return x * c + jnp.where(lane % 2 == 0, nxt, prv) * s


N_INPROJ_WEIGHTS = 7


def _mod_row(mod_ref, row0, per_batch):
    r = row0 + pl.program_id(0) if per_batch else row0
    return mod_ref[pl.ds(r, 1), :]


def _modulated(x_ref, m):
    d = x_ref.shape[1]
    return (x_ref[...] * (1.0 + m[:, d:2 * d]) + m[:, 0:d]).astype(BF16)


def _inproj_kernel(*refs, row0, per_batch, rope):
    x_ref, mod_ref = refs[:2]
    _inproj_body(x_ref, _mod_row(mod_ref, row0, per_batch), *refs[2:], rope=rope)


def _inproj_body(x_ref, m, *refs, rope):
    gam_ref, win_ref, qn_ref, wuq_ref, kvn_ref, wk_ref, wv_ref, wpe_ref = refs[:1 + N_INPROJ_WEIGHTS]
    refs = refs[1 + N_INPROJ_WEIGHTS:]
    if rope:
        cq_ref, sq_ref, ck_ref, sk_ref = refs[:4]
        refs = refs[4:]
    q_o, k_o, v_o, ckv_o, kpe_o, hgx_o = refs
    h = _modulated(x_ref, m)
    hw = HG_HEADS * HG_DK
    o_kv = qn_ref.shape[1]
    o_pe = o_kv + kvn_ref.shape[1]
    o_h = o_pe + MLA_ROPE

    cq = _rms(_dot_nt(h, win_ref[0:o_kv, :]), qn_ref[...])
    q = _dot(cq.astype(BF16), wuq_ref[...])
    if rope:
        q = _rope(q, jnp.concatenate([cq_ref[...]] * MLA_HEADS, axis=1),
                  jnp.concatenate([sq_ref[...]] * MLA_HEADS, axis=1))
    q_o[...] = q.astype(BF16)

    ckv = _rms(_dot_nt(h, win_ref[o_kv:o_pe, :]), kvn_ref[...])
    ckv_o[...] = ckv
    kpe = _dot_nt(h, win_ref[o_pe:o_h, :])
    kpe = jnp.concatenate([kpe, jnp.zeros((kpe.shape[0], LANES - MLA_ROPE), F32)], axis=1)
    if rope:
        kpe = _rope(kpe, ck_ref[...], sk_ref[...])
    kpe_o[...] = kpe
    cb = ckv.astype(BF16)
    k_o[...] = (_dot(cb, wk_ref[...]) + _dot(kpe.astype(BF16), wpe_ref[...])).astype(BF16)
    v_o[...] = _dot(cb, wv_ref[...]).astype(BF16)

    z = _dot_nt(h, win_ref[o_h:o_h + 5 * hw, :])
    hgx_o[:, 0:hw] = _silu(z[:, 0:hw])
    for dr in range(2):
        g0, g1 = gam_ref[dr, 0:1, :], gam_ref[dr, 1:2, :]
        gmax = jnp.maximum(g0, g1)
        e0, e1 = jnp.exp(g0 - gmax), jnp.exp(g1 - gmax)
        lb = e0 / (e0 + e1)
        f = lb + (1.0 - lb) * jax.nn.sigmoid(z[:, (1 + dr) * hw:(2 + dr) * hw])
        hgx_o[:, (1 + 2 * dr) * hw:(2 + 2 * dr) * hw] = jnp.log(f)
        hgx_o[:, (2 + 2 * dr) * hw:(3 + 2 * dr) * hw] = 1.0 - f
    hgx_o[:, 5 * hw:6 * hw] = z[:, 3 * hw:4 * hw]
    hgx_o[:, 6 * hw:7 * hw] = z[:, 4 * hw:5 * hw]


def _inproj(x2d, batch, seq, mod, gamma, wts, row0, per_batch, rope_tabs):
    n, d = x2d.shape
    tm = TOK_BLOCK
    nblk = seq // tm
    rope = rope_tabs is not None
    hp = MLA_HEADS * LANES
    hw = HG_HEADS * HG_DK
    tok = lambda b, i: (b * nblk + i, 0)
    pos = lambda b, i: (i, 0)
    weights = [wts[k] for k in INPROJ_KEYS]
    ins = [x2d, mod, gamma] + weights
    in_specs = ([pl.BlockSpec((tm, d), tok), _const_spec(mod.shape), _const_spec(gamma.shape)]
                + [_const_spec(w.shape) for w in weights])
    if rope:
        ins += list(rope_tabs)
        in_specs += [pl.BlockSpec((tm, t.shape[1]), pos) for t in rope_tabs]
    widths = [(hp, BF16), (hp, BF16), (hp, BF16), (wts["kvn"].shape[1], F32), (LANES, F32),
              (7 * hw, F32)]
    return pl.pallas_call(
        functools.partial(_inproj_kernel, row0=row0, per_batch=per_batch, rope=rope),
        out_shape=[jax.ShapeDtypeStruct((n, w), dt) for w, dt in widths],
        grid=(batch, nblk),
        in_specs=in_specs,
        out_specs=[pl.BlockSpec((tm, w), tok) for w, _ in widths],
        compiler_params=_params("arbitrary", "arbitrary"),
        name="inproj",
    )(*ins)


def _kvup_kernel(ckv_ref, kpe_ref, wk_ref, wv_ref, wpe_ref, k_o, v_o):
    cb = ckv_ref[...].astype(BF16)
    k_o[...] = (_dot(cb, wk_ref[...]) + _dot(kpe_ref[...].astype(BF16), wpe_ref[...])).astype(BF16)
    v_o[...] = _dot(cb, wv_ref[...]).astype(BF16)


def _kvup(ckv2d, kpe2d, wts):
    n = ckv2d.shape[0]
    tm = TOK_BLOCK
    hp = MLA_HEADS * LANES
    row = lambda i: (i, 0)
    ws = [wts["wk"], wts["wv"], wts["wpe"]]
    return pl.pallas_call(
        _kvup_kernel,
        out_shape=[jax.ShapeDtypeStruct((n, hp), BF16)] * 2,
        grid=(n // tm,),
        in_specs=[pl.BlockSpec((tm, ckv2d.shape[1]), row), pl.BlockSpec((tm, LANES), row)]
                 + [_const_spec(w.shape) for w in ws],
        out_specs=[pl.BlockSpec((tm, hp), row)] * 2,
        compiler_params=_params("arbitrary"),
        name="kvup",
    )(ckv2d, kpe2d, *ws)


ATTN_SCALE = (MLA_NOPE + MLA_ROPE) ** -0.5


def _attn_kernel(*refs, cached):
    if cached:
        q_ref, k_ref, v_ref, kc_ref, vc_ref, o_ref = refs
    else:
        q_ref, k_ref, v_ref, o_ref = refs
        kc_ref = vc_ref = None
    _attn_body(q_ref, k_ref, v_ref, kc_ref, vc_ref, o_ref)


def _attn_body(q_ref, k_ref, v_ref, kc_ref, vc_ref, o_ref):
    cached = kc_ref is not None
    scale = ATTN_SCALE
    for hd in range(MLA_HEADS):
        sl = slice(hd * LANES, (hd + 1) * LANES)
        q = q_ref[:, sl]
        s = _dot_nt(q, k_ref[:, sl]) * scale
        mx = jnp.max(s, axis=-1, keepdims=True)
        if cached:
            s2 = _dot_nt(q, kc_ref[:, sl]) * scale
            mx = jnp.maximum(mx, jnp.max(s2, axis=-1, keepdims=True))
        e = jnp.exp(s - mx)
        den = jnp.sum(e, axis=-1, keepdims=True)
        o = _dot(e.astype(BF16), v_ref[:, sl])
        if cached:
            e2 = jnp.exp(s2 - mx)
            den = den + jnp.sum(e2, axis=-1, keepdims=True)
            o = o + _dot(e2.astype(BF16), vc_ref[:, sl])
        o_ref[:, sl] = (o / den).astype(o_ref.dtype)


def _attn(q, k, v, batch, seq, cache=None):
    n, hp = q.shape
    tq = TOK_BLOCK
    nblk = seq // tq
    ins = [q, k, v]
    in_specs = [pl.BlockSpec((tq, hp), lambda b, i: (b * nblk + i, 0)),
                pl.BlockSpec((seq, hp), lambda b, i: (b, 0)),
                pl.BlockSpec((seq, hp), lambda b, i: (b, 0))]
    if cache is not None:
        past = cache[0].shape[0] // batch
        ins += list(cache)
        in_specs += [pl.BlockSpec((past, hp), lambda b, i: (b, 0))] * 2
    return pl.pallas_call(
        functools.partial(_attn_kernel, cached=cache is not None),
        out_shape=jax.ShapeDtypeStruct((n, hp), BF16),
        grid=(batch, nblk),
        in_specs=in_specs,
        out_specs=pl.BlockSpec((tq, hp), lambda b, i: (b * nblk + i, 0)),
        compiler_params=_params("arbitrary", "arbitrary"),
        name="attn",
    )(*ins)


def _split3(x):
    h1 = x.astype(BF16)
    r1 = x - h1.astype(F32)
    h2 = r1.astype(BF16)
    h3 = (r1 - h2.astype(F32)).astype(BF16)
    return h1, h2, h3


def _hgrn_kernel(*refs, has_init):
    fwd, bwd = refs[0:4], refs[4:8]
    refs = refs[8:]
    s0_ref = None
    if has_init:
        s0_ref = refs[0]
        refs = refs[1:]
    of_ref, ob_ref, sfin_ref, st_scr = refs
    i = pl.program_id(1)
    _hgrn_body(fwd, bwd, s0_ref, of_ref, ob_ref, sfin_ref, st_scr, i == 0, i == pl.num_programs(1) - 1)


def _hgrn_body(fwd, bwd, s0_ref, of_ref, ob_ref, sfin_ref, st_scr, first, last):
    tm = fwd[0].shape[0]
    c = HG_CHUNK
    nch = tm // c
    dk, dv = HG_DK, HG_DV
    hw = HG_HEADS * dk

    def initial(dr, hd):
        return s0_ref[0, dr, hd].T if s0_ref is not None else jnp.zeros((dv, dk), F32)

    if st_scr is not None:
        @pl.when(first)
        def _init():
            for dr in range(2):
                for hd in range(HG_HEADS):
                    st_scr[dr, hd] = initial(dr, hd)

    row = lax.broadcasted_iota(jnp.int32, (tm, tm), 0)
    col = lax.broadcasted_iota(jnp.int32, (tm, tm), 1)
    same = (row // c) == (col // c)
    bd = (lax.broadcasted_iota(jnp.int32, (tm, nch * dk), 0) // c
          == lax.broadcasted_iota(jnp.int32, (tm, nch * dk), 1) // dk)

    for dr, (hq_ref, lf_ref, kk_ref, vv_ref) in enumerate((fwd, bwd)):
        o_ref = of_ref if dr == 0 else ob_ref
        tri = same & ((col <= row) if dr == 0 else (col >= row))
        tri_b = jnp.where(tri, 1.0, 0.0).astype(BF16)
        h1, h2, h3 = _split3(lf_ref[...])
        bcum = _dot(tri_b, h1) + _dot(tri_b, h2) + _dot(tri_b, h3)
        closing = c - 1 if dr == 0 else 0
        btot3 = bcum.reshape(nch, c, hw)[:, closing:closing + 1, :]
        btot = jnp.broadcast_to(btot3, (nch, c, hw)).reshape(tm, hw)
        dec3 = jnp.exp(btot3)
        kk = kk_ref[...]
        qd = hq_ref[...] * jnp.exp(bcum)
        kd = kk * jnp.exp(-bcum)
        ke = kk * jnp.exp(btot - bcum)
        vv = vv_ref[...]
        order = range(nch) if dr == 0 else range(nch - 1, -1, -1)
        for hd in range(HG_HEADS):
            sl = slice(hd * dk, (hd + 1) * dk)
            qd_h = qd[:, sl]
            v_h = vv[:, hd * dv:(hd + 1) * dv]
            a = jnp.where(tri, _dot_nt(qd_h.astype(BF16), kd[:, sl].astype(BF16)), 0.0)
            o_intra = _dot(a.astype(BF16), v_h.astype(BF16))
            kebd = jnp.where(bd, jnp.concatenate([ke[:, sl]] * nch, axis=1), 0.0).astype(BF16)
            qbd = jnp.where(bd, jnp.concatenate([qd_h] * nch, axis=1), 0.0).astype(BF16)
            ut = _dot(v_h.T.astype(BF16), kebd)
            st = st_scr[dr, hd] if st_scr is not None else initial(dr, hd)
            prev = [None] * nch
            for n in order:
                prev[n] = st
                st = st * dec3[n][:, sl] + ut[:, n * dk:(n + 1) * dk]
            if st_scr is not None:
                st_scr[dr, hd] = st
            o_inter = _dot_nt(qbd, jnp.concatenate(prev, axis=1).astype(BF16))
            o_ref[:, hd * dv:(hd + 1) * dv] = o_intra + o_inter

            if last is True:
                sfin_ref[0, dr, hd] = st.T
            else:
                @pl.when(last)
                def _final(st=st, dr=dr, hd=hd):
                    sfin_ref[0, dr, hd] = st.T


def _hgrn(hgx, batch, seq, s0=None):
    n = hgx.shape[0]
    tm = TOK_BLOCK
    nblk = seq // tm
    hw = HG_HEADS * HG_DK

    def spec(lane_blk, rev):
        if rev:
            return pl.BlockSpec((tm, hw), lambda b, i: (b * nblk + nblk - 1 - i, lane_blk))
        return pl.BlockSpec((tm, hw), lambda b, i: (b * nblk + i, lane_blk))

    in_specs = [spec(0, False), spec(1, False), spec(2, False), spec(5, False),
                spec(0, True), spec(3, True), spec(4, True), spec(5, True)]
    ins = [hgx] * 8
    st_shape = (1, 2, HG_HEADS, HG_DK, HG_DV)
    st_spec = pl.BlockSpec(st_shape, lambda b, i: (b, 0, 0, 0, 0))
    if s0 is not None:
        ins.append(s0)
        in_specs.append(st_spec)
    return pl.pallas_call(
        functools.partial(_hgrn_kernel, has_init=s0 is not None),
        out_shape=[jax.ShapeDtypeStruct((n, hw), F32), jax.ShapeDtypeStruct((n, hw), F32),
                   jax.ShapeDtypeStruct((batch,) + st_shape[1:], F32)],
        grid=(batch, nblk),
        in_specs=in_specs,
        out_specs=[spec(0, False), spec(0, True), st_spec],
        scratch_shapes=[pltpu.VMEM((2, HG_HEADS, HG_DV, HG_DK), F32)],
        compiler_params=_params("arbitrary", "arbitrary"),
        name="hgrn",
    )(*ins)


def _layer_norm(x, g, b):
    xc = x - jnp.mean(x, axis=-1, keepdims=True)
    var = jnp.mean(xc * xc, axis=-1, keepdims=True)
    return xc * lax.rsqrt(var + EPS) * g + b


N_POSTMIX_WEIGHTS = 8
INPROJ_KEYS = ("win", "qn", "wuq", "kvn", "wk", "wv", "wpe")
POSTMIX_KEYS = ("hgn", "womla", "wohg", "wout", "ln1g", "ln1b", "wr")
MIXER_SEQS = 2


def _postmix_kernel(x_ref, mod_ref, *refs, alpha, row0, per_batch):
    _postmix_body(x_ref, _mod_row(mod_ref, row0, per_batch), *refs, alpha=alpha)


def _postmix_body(x_ref, m, of_ref, ob_ref, zg_ref, om_ref, wg_ref, hgn_ref, womla_ref,
                  wohg_ref, wout_ref, lng_ref, lnb_ref, wr_ref, x1_o, h2_o, aff_o, *, alpha):
    d = x_ref.shape[1]
    tb = aff_o.shape[2]
    g1, sh2, sc2 = m[:, 2 * d:3 * d], m[:, 3 * d:4 * d], m[:, 4 * d:5 * d]
    o = of_ref[...] + ob_ref[...]
    zg = zg_ref[...]
    parts = []
    for hd in range(HG_HEADS):
        sl = slice(hd * HG_DV, (hd + 1) * HG_DV)
        parts.append(_rms(o[:, sl], hgn_ref[...]) * _silu(zg[:, sl]))
    ohg = jnp.concatenate(parts, axis=1).astype(BF16)
    gates = _dot_nt(_modulated(x_ref, m), wg_ref[wg_ref.shape[0] - 2 * d:, :])
    merged = (jax.nn.sigmoid(gates[:, 0:d]) * _dot(om_ref[...], womla_ref[...])
              + jax.nn.sigmoid(gates[:, d:2 * d]) * _dot(ohg, wohg_ref[...]))
    mix = _dot(merged.astype(BF16), wout_ref[...])
    x1 = _layer_norm(alpha * x_ref[...] + g1 * mix, lng_ref[...], lnb_ref[...])
    x1_o[...] = x1
    h2 = (x1 * (1.0 + sc2) + sh2).astype(BF16)
    h2_o[...] = h2
    logits = _dot_nt(wr_ref[...], h2)
    e = jnp.exp(logits - jnp.max(logits, axis=0, keepdims=True))
    aff = e / jnp.sum(e, axis=0, keepdims=True)
    for blk in range(aff_o.shape[0]):
        aff_o[blk] = aff[:, blk * tb:(blk + 1) * tb]


def _postmix(x2d, batch, seq, mod, o_f, o_b, hgx, o_mla, wts, alpha, n_experts, row0, per_batch):
    n, d = x2d.shape
    tm = TOK_BLOCK
    nblk = seq // tm
    hw = HG_HEADS * HG_DV
    tok = lambda b, i: (b * nblk + i, 0)
    weights = [wts[k] for k in POSTMIX_KEYS]
    return pl.pallas_call(
        functools.partial(_postmix_kernel, alpha=alpha, row0=row0, per_batch=per_batch),
        out_shape=[jax.ShapeDtypeStruct((n, d), F32), jax.ShapeDtypeStruct((n, d), BF16),
                   jax.ShapeDtypeStruct((n // tm, n_experts, tm), F32)],
        grid=(batch, nblk),
        in_specs=[pl.BlockSpec((tm, d), tok), _const_spec(mod.shape),
                  pl.BlockSpec((tm, hw), tok), pl.BlockSpec((tm, hw), tok),
                  pl.BlockSpec((tm, hw), lambda b, i: (b * nblk + i, 6)),
                  pl.BlockSpec((tm, o_mla.shape[1]), tok), _const_spec(wts["win"].shape)]
                 + [_const_spec(w.shape) for w in weights],
        out_specs=[pl.BlockSpec((tm, d), tok), pl.BlockSpec((tm, d), tok),
                   pl.BlockSpec((1, n_experts, tm), lambda b, i: (b * nblk + i, 0, 0))],
        compiler_params=_params("arbitrary", "arbitrary"),
        name="postmix",
    )(x2d, mod, o_f, o_b, hgx, o_mla, wts["win"], *weights)


def _mixer_kernel(x_ref, mod_ref, *refs, alpha, row0, seq):
    nw = 1 + N_INPROJ_WEIGHTS
    in_w, refs = refs[:nw], refs[nw:]
    pm_w, refs = refs[:N_POSTMIX_WEIGHTS - 1], refs[N_POSTMIX_WEIGHTS - 1:]
    pm_w = (in_w[1],) + tuple(pm_w)
    x1_o, h2_o, aff_o, ckv_o, kpe_o, sfin_o, q_s, k_s, v_s, hgx_s, om_s, of_s, ob_s = refs
    m = _mod_row(mod_ref, row0, False)
    _inproj_body(x_ref, m, *in_w, q_s, k_s, v_s, ckv_o, kpe_o, hgx_s, rope=False)
    hw = HG_HEADS * HG_DK
    for s in range(x_ref.shape[0] // seq):
        rows = slice(s * seq, (s + 1) * seq)
        _attn_body(q_s.at[rows], k_s.at[rows], v_s.at[rows], None, None, om_s.at[rows])
        lane = lambda j: hgx_s.at[rows, j * hw:(j + 1) * hw]
        _hgrn_body((lane(0), lane(1), lane(2), lane(5)), (lane(0), lane(3), lane(4), lane(5)), None,
                   of_s.at[rows], ob_s.at[rows], sfin_o.at[s:s + 1], None, True, True)
    _postmix_body(x_ref, m, of_s, ob_s, hgx_s.at[:, 6 * hw:7 * hw], om_s, *pm_w, x1_o, h2_o, aff_o,
                  alpha=alpha)


def _mixer(x2d, batch, seq, mod, gamma, wts, alpha, n_experts, row0):
    n, d = x2d.shape
    assert seq == TOK_BLOCK
    ns = MIXER_SEQS if batch % MIXER_SEQS == 0 else 1
    tm = ns * seq
    hp = MLA_HEADS * LANES
    hw = HG_HEADS * HG_DK
    kvl = wts["kvn"].shape[1]
    weights = [wts[k] for k in INPROJ_KEYS + POSTMIX_KEYS]
    tok = lambda b: (b, 0)
    st_shape = (ns, 2, HG_HEADS, HG_DK, HG_DV)
    return pl.pallas_call(
        functools.partial(_mixer_kernel, alpha=alpha, row0=row0, seq=seq),
        out_shape=[jax.ShapeDtypeStruct((n, d), F32), jax.ShapeDtypeStruct((n, d), BF16),
                   jax.ShapeDtypeStruct((n // seq, n_experts, seq), F32),
                   jax.ShapeDtypeStruct((n, kvl), F32), jax.ShapeDtypeStruct((n, LANES), F32),
                   jax.ShapeDtypeStruct((batch,) + st_shape[1:], F32)],
        grid=(batch // ns,),
        in_specs=[pl.BlockSpec((tm, d), tok), _const_spec(mod.shape), _const_spec(gamma.shape)]
                 + [_const_spec(w.shape) for w in weights],
        out_specs=[pl.BlockSpec((tm, d), tok), pl.BlockSpec((tm, d), tok),
                   pl.BlockSpec((ns, n_experts, seq), lambda b: (b, 0, 0)),
                   pl.BlockSpec((tm, kvl), tok), pl.BlockSpec((tm, LANES), tok),
                   pl.BlockSpec(st_shape, lambda b: (b, 0, 0, 0, 0))],
        scratch_shapes=[pltpu.VMEM((tm, hp), BF16)] * 3
                       + [pltpu.VMEM((tm, 7 * hw), F32), pltpu.VMEM((tm, hp), BF16),
                          pltpu.VMEM((tm, hw), F32), pltpu.VMEM((tm, hw), F32)],
        compiler_params=_params("arbitrary"),
        name="mixer",
    )(x2d, mod, gamma, *weights)


def _route_kernel(aff_ref, rank_o, cnt_o, *, cap):
    nb, ne, tb = aff_ref.shape
    key = aff_ref[...]

    def count(mask):
        return jnp.sum(jnp.sum(jnp.where(mask, 1.0, 0.0), axis=0), axis=1, keepdims=True)

    def bit_step(it, bits):
        cand = bits | jnp.left_shift(jnp.int32(1), 30 - it)
        return jnp.where(count(key >= pltpu.bitcast(cand, F32)[None]) >= cap, cand, bits)

    bits = lax.fori_loop(0, 31, bit_step, jnp.zeros((ne, 1), jnp.int32))
    thr = pltpu.bitcast(bits, F32)
    need = cap - count(key > thr[None])
    before = (lax.broadcasted_iota(jnp.int32, (tb, tb), 0)
              < lax.broadcasted_iota(jnp.int32, (tb, tb), 1))
    before = jnp.where(before, 1.0, 0.0).astype(BF16)
    off_eq = jnp.zeros((ne, 1), F32)
    off_sel = jnp.zeros((ne, 1), F32)
    cnt_o[...] = jnp.zeros_like(cnt_o)
    for blk in range(nb):
        key_b = key[blk]
        eq = key_b == thr
        eq_b = jnp.where(eq, 1.0, 0.0)
        eq_rank = _dot(eq_b.astype(BF16), before) + off_eq
        sel = (key_b > thr) | (eq & (eq_rank < need))
        sel_b = jnp.where(sel, 1.0, 0.0)
        rank = _dot(sel_b.astype(BF16), before) + off_sel
        rank_o[blk] = jnp.where(sel, rank.astype(jnp.int32), UNSELECTED)
        cnt_o[:, blk:blk + 1] = off_sel.astype(jnp.int32)
        off_eq = off_eq + jnp.sum(eq_b, axis=1, keepdims=True)
        off_sel = off_sel + jnp.sum(sel_b, axis=1, keepdims=True)
    cnt_o[:, nb:nb + 1] = off_sel.astype(jnp.int32)


def _route(aff, cap):
    nb, ne, tb = aff.shape
    assert nb + 1 <= LANES
    rank, cnt = pl.pallas_call(
        functools.partial(_route_kernel, cap=cap),
        out_shape=[jax.ShapeDtypeStruct(aff.shape, jnp.int32), jax.ShapeDtypeStruct((ne, LANES), jnp.int32)],
        in_specs=[pl.BlockSpec(memory_space=pltpu.VMEM)],
        out_specs=[pl.BlockSpec(memory_space=pltpu.VMEM)] * 2,
        compiler_params=pltpu.CompilerParams(vmem_limit_bytes=VMEM_LIMIT),
        name="route",
    )(aff)
    return rank, cnt[:, :nb + 1]


def _window_hits(rk_ref, firsts, slot0, win):
    ne, tb = rk_ref.shape[1], rk_ref.shape[2]
    win_iota = lax.broadcasted_iota(jnp.int32, (win, tb), 0)
    return [(rk_ref[0, e:e + 1, :] + (slot0 - firsts[e])) == win_iota for e in range(ne)]


def _compact_kernel(first_ref, end_ref, rounds_ref, *refs, groups, slots):
    ng = len(groups)
    h2_refs, rk_refs, af_refs = refs[0:ng], refs[ng:2 * ng], refs[2 * ng:3 * ng]
    xe_hbm, stage, tail, sem, issued = refs[3 * ng:]
    b = pl.program_id(0)
    ne = rk_refs[0].shape[1]
    win = WIN_ROWS
    d = h2_refs[0].shape[1]
    sub = SUBLANES

    def copies(slot, dsts):
        return [pltpu.make_async_copy(stage.at[slot, pl.ds(e * win, win), :],
                                      xe_hbm.at[e, pl.ds(pl.multiple_of(dsts[e], sub), win), :], sem.at[e])
                for e in range(ne)]

    def wait_previous():
        @pl.when(issued[0] > 0)
        def _():
            for cp in copies(0, [0] * ne):
                cp.wait()

    @pl.when(b == 0)
    def _init():
        issued[0] = 0
        tail[...] = jnp.zeros_like(tail)
        stage[1] = jnp.zeros(stage.shape[1:], stage.dtype)
        pad = copies(1, [slots] * ne)
        for cp in pad:
            cp.start()
        for cp in pad:
            cp.wait()

    def group_body(h2_ref, rk_ref, af_ref, slot0):
        firsts = [first_ref[b * ne + e] for e in range(ne)]
        bases = [(f // sub) * sub for f in firsts]
        ends = [end_ref[b * ne + e] - bases[e] for e in range(ne)]
        sub_iota = lax.broadcasted_iota(jnp.int32, (sub, stage.shape[2]), 0)

        def one_round(r, carry):
            dsts = [bases[e] + r * win for e in range(ne)]
            hits = _window_hits(rk_ref, dsts, slot0, win)
            onehot = jnp.where(jnp.concatenate(hits, axis=0), 1.0, 0.0).astype(BF16)
            rows = _dot(onehot, h2_ref[...])
            gate = jnp.concatenate(
                [jnp.sum(jnp.where(hits[e], af_ref[0, e:e + 1, :], 0.0), axis=1, keepdims=True)
                 for e in range(ne)], axis=0)
            slot = issued[0] % 2
            stage[slot, :, 0:d] = rows
            stage[slot, :, d:] = jnp.broadcast_to(gate, (ne * win, LANES))
            for e in range(ne):
                @pl.when(r == 0)
                def _head(e=e):
                    head = stage[slot, e * win:e * win + sub, :]
                    stage[slot, e * win:e * win + sub, :] = jnp.where(
                        sub_iota < firsts[e] - bases[e], tail[e * sub:(e + 1) * sub, :], head)

                last = (ends[e] // sub) * sub
                @pl.when(r == last // win)
                def _tail(e=e, last=last):
                    tail[e * sub:(e + 1) * sub, :] = stage[
                        slot, pl.ds(pl.multiple_of(e * win + last % win, sub), sub), :]
            wait_previous()
            for cp in copies(slot, [jnp.minimum(dst, slots) for dst in dsts]):
                cp.start()
            issued[0] = issued[0] + 1
            return carry

        lax.fori_loop(0, rounds_ref[b], one_round, 0)

    blk0 = 0
    for gi, g in enumerate(groups):
        @pl.when((b >= blk0) & (b < blk0 + g["nb"]))
        def _(gi=gi, g=g):
            group_body(h2_refs[gi], rk_refs[gi], af_refs[gi], g["slot0"])
        blk0 += g["nb"]

    @pl.when(b == pl.num_programs(0) - 1)
    def _drain():
        wait_previous()


def _compact(groups, first, end, rounds, slots):
    d = groups[0]["h2"].shape[1]
    nbs = [g["rank"].shape[0] for g in groups]
    ne, tb = groups[0]["rank"].shape[1:]
    meta, specs_h2, specs_rk = [], [], []
    blk0 = 0
    for g, nb in zip(groups, nbs):
        meta.append(dict(nb=nb, slot0=g["slot0"]))
        local = lambda b, *_, blk0=blk0, nb=nb: jnp.clip(b - blk0, 0, nb - 1)
        specs_h2.append(pl.BlockSpec((tb, d), lambda b, *_, local=local: (local(b), 0)))
        specs_rk.append(pl.BlockSpec((1, ne, tb), lambda b, *_, local=local: (local(b), 0, 0)))
        blk0 += nb
    width = d + LANES
    return pl.pallas_call(
        functools.partial(_compact_kernel, groups=meta, slots=slots),
        out_shape=jax.ShapeDtypeStruct((ne, slots + WIN_ROWS, width), F32),
        grid_spec=pltpu.PrefetchScalarGridSpec(
            num_scalar_prefetch=3,
            grid=(sum(nbs),),
            in_specs=specs_h2 + specs_rk + specs_rk,
            out_specs=pl.BlockSpec(memory_space=pl.ANY),
            scratch_shapes=[pltpu.VMEM((2, ne * WIN_ROWS, width), F32),
                            pltpu.VMEM((ne * SUBLANES, width), F32),
                            pltpu.SemaphoreType.DMA((ne,)), pltpu.SMEM((1,), jnp.int32)]),
        compiler_params=_params("arbitrary"),
        name="compact",
    )(first, end, rounds, *[g["h2"] for g in groups], *[g["rank"] for g in groups],
      *[g["aff"] for g in groups])


def _ffn_kernel(xe_ref, w1_ref, w3_ref, w2_ref, ye_ref, x_scr, g_scr, acc_scr):
    f = pl.program_id(1)
    d = x_scr.shape[1]

    @pl.when(f == 0)
    def _unpack():
        x_scr[...] = xe_ref[0, :, 0:d].astype(BF16)
        g_scr[...] = xe_ref[0, :, d:d + 1]

    x = x_scr[...]
    hid = _silu(_dot(x, w1_ref[0].astype(BF16))) * _dot(x, w3_ref[0].astype(BF16))
    y = _dot(hid.astype(BF16), w2_ref[0].astype(BF16))

    last = pl.num_programs(1) - 1

    @pl.when((f == 0) & (f < last))
    def _first():
        acc_scr[...] = y

    @pl.when((f > 0) & (f < last))
    def _middle():
        acc_scr[...] += y

    @pl.when((f == last) & (f > 0))
    def _last():
        ye_ref[0] = ((acc_scr[...] + y) * g_scr[...]).astype(ye_ref.dtype)

    @pl.when((f == last) & (f == 0))
    def _only():
        ye_ref[0] = (y * g_scr[...]).astype(ye_ref.dtype)


def _ffn(xe, w1, w3, w2, slots, ft):
    ne, d, dff = w1.shape
    nf = dff // ft
    return pl.pallas_call(
        _ffn_kernel,
        out_shape=jax.ShapeDtypeStruct((ne, slots, d), BF16),
        grid=(ne, nf),
        in_specs=[pl.BlockSpec((1, slots, xe.shape[2]), lambda e, f: (e, 0, 0)),
                  pl.BlockSpec((1, d, ft), lambda e, f: (e, 0, f)),
                  pl.BlockSpec((1, d, ft), lambda e, f: (e, 0, f)),
                  pl.BlockSpec((1, ft, d), lambda e, f: (e, f, 0))],
        out_specs=pl.BlockSpec((1, slots, d), lambda e, f: (e, 0, 0)),
        scratch_shapes=[pltpu.VMEM((slots, d), BF16), pltpu.VMEM((slots, 1), F32),
                        pltpu.VMEM((slots, d), F32)],
        compiler_params=_params("arbitrary", "arbitrary"),
        name="ffn",
    )(xe, w1, w3, w2)


def _combine_kernel(first_ref, rounds_ref, rk_ref, x1_ref, mod_ref, lng_ref, lnb_ref, ye_hbm, out_ref,
                    buf, acc_scr, sem, *, d_model, alpha, slot0, slots, row0, blocks_per_batch):
    d = d_model
    b = pl.program_id(0)
    nblk = pl.num_programs(0)
    ne, tb = rk_ref.shape[1], rk_ref.shape[2]
    win = WIN_ROWS
    eye = (lax.broadcasted_iota(jnp.int32, (tb, tb), 0)
           == lax.broadcasted_iota(jnp.int32, (tb, tb), 1))
    eye = jnp.where(eye, 1.0, 0.0).astype(BF16)

    def starts_of(blk, r):
        firsts = [(first_ref[blk * ne + e] // BF16_ROWS) * BF16_ROWS + r * win for e in range(ne)]
        return firsts, [jnp.minimum(f, slots - win) for f in firsts]

    def windows(slot, starts):
        return [pltpu.make_async_copy(ye_hbm.at[e, pl.ds(pl.multiple_of(starts[e], BF16_ROWS), win), :],
                                      buf.at[slot, pl.ds(e * win, win), :], sem.at[slot, e])
                for e in range(ne)]

    def scatter(slot, firsts, starts):
        hits = _window_hits(rk_ref, starts, slot0, win)
        hits = [h & ((rk_ref[0, e:e + 1, :] + slot0) >= firsts[e]) for e, h in enumerate(hits)]
        hit = jnp.where(jnp.concatenate(hits, axis=0), 1.0, 0.0).astype(BF16)
        hit_t = _dot_nt(eye, hit).astype(BF16)
        return _dot(hit_t, buf[slot])

    cur = b % 2

    @pl.when(b == 0)
    def _prime():
        for cp in windows(0, starts_of(0, 0)[1]):
            cp.start()

    @pl.when(b + 1 < nblk)
    def _prefetch():
        for cp in windows(1 - cur, starts_of(b + 1, 0)[1]):
            cp.start()

    firsts, starts = starts_of(b, 0)
    for cp in windows(cur, starts):
        cp.wait()
    acc_scr[...] = scatter(cur, firsts, starts)

    def extra_round(r, carry):
        firsts, starts = starts_of(b, r)
        for cp in windows(2, starts):
            cp.start()
        for cp in windows(2, starts):
            cp.wait()
        acc_scr[...] += scatter(2, firsts, starts)
        return carry

    lax.fori_loop(1, rounds_ref[b], extra_round, 0)
    r = row0 + b // blocks_per_batch
    g2 = mod_ref[pl.ds(r, 1), :][:, 5 * d:6 * d]
    out_ref[...] = _layer_norm(alpha * x1_ref[...] + g2 * acc_scr[...], lng_ref[...], lnb_ref[...])


def _combine(ye, rank, first, rounds, x1, mod, ln_g, ln_b, alpha, slot0, row0, blocks_per_batch):
    n, d = x1.shape
    nb, ne, tb = rank.shape
    slots = ye.shape[1]
    return pl.pallas_call(
        functools.partial(_combine_kernel, d_model=d, alpha=alpha, slot0=slot0, slots=slots, row0=row0,
                          blocks_per_batch=blocks_per_batch),
        out_shape=jax.ShapeDtypeStruct((n, d), F32),
        grid_spec=pltpu.PrefetchScalarGridSpec(
            num_scalar_prefetch=2,
            grid=(nb,),
            in_specs=[pl.BlockSpec((1, ne, tb), lambda b, *_: (b, 0, 0)),
                      pl.BlockSpec((tb, d), lambda b, *_: (b, 0)),
                      pl.BlockSpec(mod.shape, lambda b, *_: (0, 0)),
                      pl.BlockSpec((1, d), lambda b, *_: (0, 0)),
                      pl.BlockSpec((1, d), lambda b, *_: (0, 0)),
                      pl.BlockSpec(memory_space=pl.ANY)],
            out_specs=pl.BlockSpec((tb, d), lambda b, *_: (b, 0)),
            scratch_shapes=[pltpu.VMEM((3, ne * WIN_ROWS, d), ye.dtype), pltpu.VMEM((tb, d), F32),
                            pltpu.SemaphoreType.DMA((3, ne))]),
        compiler_params=_params("arbitrary"),
        name="combine",
    )(first, rounds, rank, x1, mod, ln_g, ln_b, ye)


def _prep_weights(w_in, q_norm, w_uq, kv_norm, w_ukv, w_o_mla, hgrn_norm, w_o_hg, w_out, ln1_g, ln1_b,
                  w_router):
    d = w_in.shape[0]
    q_lora, kv_lora = q_norm.shape[0], kv_norm.shape[0]
    hw = HG_HEADS * HG_DK
    hh, hp = MLA_HEADS, MLA_HEADS * LANES
    o_kv, o_pe = q_lora, q_lora + kv_lora
    o_h = o_pe + MLA_ROPE
    o_g = o_h + 5 * hw
    assert w_in.shape[1] == o_g + 2 * d
    qk = MLA_NOPE + MLA_ROPE
    kvw = MLA_NOPE + MLA_V
    b16 = lambda a: a.astype(BF16)
    assert all(o % BF16_ROWS == 0 for o in (o_kv, o_pe, o_h, o_g))
    win = b16(w_in.T)
    wuq = jnp.pad(w_uq.reshape(q_lora, hh, qk), ((0, 0), (0, 0), (0, LANES - qk))).reshape(q_lora, hp)
    ukv = w_ukv.reshape(kv_lora, hh, kvw)
    wk = jnp.pad(ukv[:, :, :MLA_NOPE], ((0, 0), (0, 0), (0, LANES - MLA_NOPE))).reshape(kv_lora, hp)
    wv = jnp.pad(ukv[:, :, MLA_NOPE:], ((0, 0), (0, 0), (0, LANES - MLA_V))).reshape(kv_lora, hp)
    place = np.pad(np.eye(MLA_ROPE, dtype=np.float32), ((0, LANES - MLA_ROPE), (MLA_NOPE, LANES - qk)))
    wpe = jnp.asarray(np.tile(place, (1, hh)), BF16)
    womla = jnp.pad(w_o_mla.reshape(hh, MLA_V, d), ((0, 0), (0, LANES - MLA_V), (0, 0))).reshape(hp, d)
    return dict(
        win=win, qn=q_norm.reshape(1, -1), wuq=b16(wuq), kvn=kv_norm.reshape(1, -1), wk=b16(wk), wv=b16(wv),
        wpe=wpe, hgn=hgrn_norm.reshape(1, -1), womla=b16(womla), wohg=b16(w_o_hg), wout=b16(w_out),
        ln1g=ln1_g.reshape(1, -1), ln1b=ln1_b.reshape(1, -1), wr=b16(w_router.T))


def _rope_tables(seq):
    n_freq = MLA_ROPE // 4
    inv = ROPE_BASE ** (-np.arange(n_freq, dtype=np.float64) / n_freq)
    t = np.arange(seq)
    ang = np.concatenate([(t // GRID_W)[:, None] * inv, (t % GRID_W)[:, None] * inv], axis=-1)
    cos = np.repeat(np.cos(ang), 2, axis=1)
    sin = np.repeat(np.sin(ang), 2, axis=1) * np.tile([-1.0, 1.0], MLA_ROPE // 2)
    ck = np.pad(cos, ((0, 0), (0, LANES - MLA_ROPE)), constant_values=1.0)
    sk = np.pad(sin, ((0, 0), (0, LANES - MLA_ROPE)))
    cq = np.pad(cos, ((0, 0), (MLA_NOPE, LANES - MLA_NOPE - MLA_ROPE)), constant_values=1.0)
    sq = np.pad(sin, ((0, 0), (MLA_NOPE, LANES - MLA_NOPE - MLA_ROPE)))
    return tuple(jnp.asarray(a, F32) for a in (cq, sq, ck, sk))


def _window_sched(cnt, slot0):
    first = slot0 + cnt[:, :-1]
    end = slot0 + cnt[:, 1:]
    flat = lambda a: a.T.reshape(-1).astype(jnp.int32)

    def rounds(align):
        need = jnp.max((end - (first // align) * align + WIN_ROWS - 1) // WIN_ROWS, axis=0)
        return jnp.maximum(need, 1).astype(jnp.int32)

    return flat(first), flat(end), rounds(SUBLANES), rounds(BF16_ROWS)


def kernel(x_prompt, x_sample, c, cache_ckv, cache_kpe, state_hgrn, c_ctx, w_ada, b_ada, w_in, mla_q_norm, mla_w_uq, mla_kv_norm, mla_w_ukv, mla_w_o, hgrn_gamma, hgrn_norm, hgrn_w_o, w_out, ln1_g, ln1_b, moe_w_router, moe_w1, moe_w3, moe_w2, ln2_g, ln2_b):
    depth = w_ada.shape[0]
    assert depth == 1, "single trunk layer"
    bp, tp, d = x_prompt.shape
    bs, tsq, _ = x_sample.shape
    ne = moe_w_router.shape[-1]
    alpha = (2 * depth) ** 0.25
    past = cache_ckv.shape[2]
    assert tp % TOK_BLOCK == 0 and tsq % TOK_BLOCK == 0 and past % TOK_BLOCK == 0 and tsq % GRID_W == 0

    wts = _prep_weights(w_in[0], mla_q_norm[0], mla_w_uq[0], mla_kv_norm[0], mla_w_ukv[0], mla_w_o[0],
                        hgrn_norm[0], hgrn_w_o[0], w_out[0], ln1_g[0], ln1_b[0], moe_w_router[0])
    cond_rows = -(-(1 + bs) // SUBLANES) * SUBLANES
    cond = jnp.concatenate([c_ctx[None], c, jnp.zeros((cond_rows - 1 - bs, d), F32)], axis=0)
    mod = _adaln(cond, w_ada[0], b_ada[0])

    xs = [x_prompt.reshape(bp * tp, d), x_sample.reshape(bs * tsq, d)]
    dims = [(bp, tp), (bs, tsq)]
    rows = [(0, False), (1, True)]
    ropes = [None, _rope_tables(tsq)]
    kpe_c = jnp.pad(cache_kpe[:, 0].reshape(bs * past, MLA_ROPE), ((0, 0), (0, LANES - MLA_ROPE)))
    caches = [None, _kvup(cache_ckv[:, 0].reshape(bs * past, -1), kpe_c, wts)]
    inits = [None, state_hgrn[:, 0]]

    x1s, h2s, affs, extras = [], [], [], []
    for gi in range(2):
        (bt, sq), (row0, per_batch) = dims[gi], rows[gi]
        if sq == TOK_BLOCK and caches[gi] is None and ropes[gi] is None and not per_batch:
            x1, h2, aff, ckv, kpe, s_fin = _mixer(xs[gi], bt, sq, mod, hgrn_gamma, wts, alpha, ne, row0)
        else:
            q, k, v, ckv, kpe, hgx = _inproj(xs[gi], bt, sq, mod, hgrn_gamma, wts, row0, per_batch, ropes[gi])
            o_mla = _attn(q, k, v, bt, sq, caches[gi])
            o_f, o_b, s_fin = _hgrn(hgx, bt, sq, inits[gi])
            x1, h2, aff = _postmix(xs[gi], bt, sq, mod, o_f, o_b, hgx, o_mla, wts, alpha, ne, row0, per_batch)
        x1s.append(x1)
        h2s.append(h2)
        affs.append(aff)
        extras.append((ckv, kpe, s_fin))

    caps = [EC_FACTOR * x.shape[0] // ne for x in xs]
    slots = sum(caps)
    assert all(cp % BF16_ROWS == 0 for cp in caps) and slots >= WIN_ROWS
    groups, scheds = [], []
    slot0 = 0
    for gi in range(2):
        rank, cnt = _route(affs[gi], caps[gi])
        groups.append(dict(h2=h2s[gi], rank=rank, aff=affs[gi], slot0=slot0))
        scheds.append(_window_sched(cnt, slot0))
        slot0 += caps[gi]
    xe = _compact(groups, *[jnp.concatenate([s[k] for s in scheds]) for k in range(3)], slots)
    ye = _ffn(xe, moe_w1[0], moe_w3[0], moe_w2[0], slots, ft=512)

    outs = []
    for gi in range(2):
        outs.append(_combine(ye, groups[gi]["rank"], scheds[gi][0], scheds[gi][3], x1s[gi], mod,
                             ln2_g[0].reshape(1, -1), ln2_b[0].reshape(1, -1), alpha, groups[gi]["slot0"],
                             rows[gi][0], dims[gi][1] // TOK_BLOCK if rows[gi][1] else 1 << 30))

    ckv_p, kpe_p, st_p = extras[0]
    y_prompt = outs[0].reshape(bp, tp, d)
    y_sample = outs[1].reshape(bs, tsq, d)
    new_ckv = ckv_p.reshape(bp, 1, tp, -1)
    new_kpe = kpe_p[:, :MLA_ROPE].reshape(bp, 1, tp, MLA_ROPE)
    new_state = st_p.reshape(bp, 1, 2, HG_HEADS, HG_DK, HG_DV)
    return (y_prompt, y_sample, new_ckv, new_kpe, new_state)
```

```python
import functools

import jax
import jax.numpy as jnp
import numpy as np
from jax import lax
from jax.experimental import pallas as pl
from jax.experimental.pallas import tpu as pltpu

F32 = jnp.float32
BF16 = jnp.bfloat16

MLA_HEADS = 8
MLA_NOPE = 64
MLA_ROPE = 32
MLA_V = 64
HG_HEADS = 4
HG_DK = 128
HG_DV = 128
HG_CHUNK = 32
GRID_W = 64
ROPE_BASE = 10000.0
EC_FACTOR = 2
EPS = 1e-6

LANES = 128
SUBLANES = 8
BF16_ROWS = 16
VMEM_LIMIT = 56 * 1024 * 1024

TOK_BLOCK = 256
WIN_ROWS = 64
UNSELECTED = -(1 << 30)

NT_DIMS = (((1,), (1,)), ((), ()))


def _dot(a, b):
    return jnp.dot(a, b, preferred_element_type=F32)


def _dot_nt(a, b):
    return lax.dot_general(a, b, NT_DIMS, preferred_element_type=F32)


def _silu(x):
    return x * jax.nn.sigmoid(x)


def _params(*sem):
    return pltpu.CompilerParams(dimension_semantics=sem, vmem_limit_bytes=VMEM_LIMIT)


def _const_spec(shape):
    zeros = (0,) * len(shape)
    return pl.BlockSpec(shape, lambda *_: zeros, pipeline_mode=pl.Buffered(1))


def _adaln_kernel(c_ref, w_ref, b_ref, o_ref):
    s = _silu(c_ref[...]).astype(BF16)
    o_ref[...] = _dot(s, w_ref[...].astype(BF16)) + b_ref[...]


def _adaln(cond, w_ada, b_ada):
    rows, d = cond.shape
    n = w_ada.shape[1]
    tn = n // 4
    return pl.pallas_call(
        _adaln_kernel,
        out_shape=jax.ShapeDtypeStruct((rows, n), F32),
        grid=(n // tn,),
        in_specs=[_const_spec((rows, d)),
                  pl.BlockSpec((d, tn), lambda j: (0, j)),
                  pl.BlockSpec((1, tn), lambda j: (0, j))],
        out_specs=pl.BlockSpec((rows, tn), lambda j: (0, j)),
        compiler_params=_params("arbitrary"),
        name="adaln",
    )(cond, w_ada, b_ada.reshape(1, n))


def _rms(x, g):
    return x * lax.rsqrt(jnp.mean(x * x, axis=-1, keepdims=True) + EPS) * g


def _rope(x, c, s):
    w = x.shape[-1]
    lane = lax.broadcasted_iota(jnp.int32, x.shape, 1)
    nxt = pltpu.roll(x, w - 1, 1)
    prv = pltpu.roll(x, 1, 1)
    return x * c + jnp.where(lane % 2 == 0, nxt, prv) * s


N_INPROJ_WEIGHTS = 7


def _mod_row(mod_ref, row0, per_batch):
    r = row0 + pl.program_id(0) if per_batch else row0
    return mod_ref[pl.ds(r, 1), :]


def _modulated(x_ref, m):
    d = x_ref.shape[1]
    return (x_ref[...] * (1.0 + m[:, d:2 * d]) + m[:, 0:d]).astype(BF16)


def _inproj_kernel(*refs, row0, per_batch, rope):
    x_ref, mod_ref = refs[:2]
    _inproj_body(x_ref, _mod_row(mod_ref, row0, per_batch), *refs[2:], rope=rope)


def _inproj_body(x_ref, m, *refs, rope):
    gam_ref, win_ref, qn_ref, wuq_ref, kvn_ref, wk_ref, wv_ref, wpe_ref = refs[:1 + N_INPROJ_WEIGHTS]
    refs = refs[1 + N_INPROJ_WEIGHTS:]
    if rope:
        cq_ref, sq_ref, ck_ref, sk_ref = refs[:4]
        refs = refs[4:]
    q_o, k_o, v_o, ckv_o, kpe_o, hgx_o = refs
    h = _modulated(x_ref, m)
    hw = HG_HEADS * HG_DK
    o_kv = qn_ref.shape[1]
    o_pe = o_kv + kvn_ref.shape[1]
    o_h = o_pe + MLA_ROPE

    cq = _rms(_dot_nt(h, win_ref[0:o_kv, :]), qn_ref[...])
    q = _dot(cq.astype(BF16), wuq_ref[...])
    if rope:
        q = _rope(q, jnp.concatenate([cq_ref[...]] * MLA_HEADS, axis=1),
                  jnp.concatenate([sq_ref[...]] * MLA_HEADS, axis=1))
    q_o[...] = q.astype(BF16)

    ckv = _rms(_dot_nt(h, win_ref[o_kv:o_pe, :]), kvn_ref[...])
    ckv_o[...] = ckv
    kpe = _dot_nt(h, win_ref[o_pe:o_h, :])
    kpe = jnp.concatenate([kpe, jnp.zeros((kpe.shape[0], LANES - MLA_ROPE), F32)], axis=1)
    if rope:
        kpe = _rope(kpe, ck_ref[...], sk_ref[...])
    kpe_o[...] = kpe
    cb = ckv.astype(BF16)
    k_o[...] = (_dot(cb, wk_ref[...]) + _dot(kpe.astype(BF16), wpe_ref[...])).astype(BF16)
    v_o[...] = _dot(cb, wv_ref[...]).astype(BF16)

    z = _dot_nt(h, win_ref[o_h:o_h + 5 * hw, :])
    hgx_o[:, 0:hw] = _silu(z[:, 0:hw])
    for dr in range(2):
        g0, g1 = gam_ref[dr, 0:1, :], gam_ref[dr, 1:2, :]
        gmax = jnp.maximum(g0, g1)
        e0, e1 = jnp.exp(g0 - gmax), jnp.exp(g1 - gmax)
        lb = e0 / (e0 + e1)
        f = lb + (1.0 - lb) * jax.nn.sigmoid(z[:, (1 + dr) * hw:(2 + dr) * hw])
        hgx_o[:, (1 + 2 * dr) * hw:(2 + 2 * dr) * hw] = jnp.log(f)
        hgx_o[:, (2 + 2 * dr) * hw:(3 + 2 * dr) * hw] = 1.0 - f
    hgx_o[:, 5 * hw:6 * hw] = z[:, 3 * hw:4 * hw]
    hgx_o[:, 6 * hw:7 * hw] = z[:, 4 * hw:5 * hw]


def _inproj(x2d, batch, seq, mod, gamma, wts, row0, per_batch, rope_tabs):
    n, d = x2d.shape
    tm = TOK_BLOCK
    nblk = seq // tm
    rope = rope_tabs is not None
    hp = MLA_HEADS * LANES
    hw = HG_HEADS * HG_DK
    tok = lambda b, i: (b * nblk + i, 0)
    pos = lambda b, i: (i, 0)
    weights = [wts[k] for k in INPROJ_KEYS]
    ins = [x2d, mod, gamma] + weights
    in_specs = ([pl.BlockSpec((tm, d), tok), _const_spec(mod.shape), _const_spec(gamma.shape)]
                + [_const_spec(w.shape) for w in weights])
    if rope:
        ins += list(rope_tabs)
        in_specs += [pl.BlockSpec((tm, t.shape[1]), pos) for t in rope_tabs]
    widths = [(hp, BF16), (hp, BF16), (hp, BF16), (wts["kvn"].shape[1], F32), (LANES, F32),
              (7 * hw, F32)]
    return pl.pallas_call(
        functools.partial(_inproj_kernel, row0=row0, per_batch=per_batch, rope=rope),
        out_shape=[jax.ShapeDtypeStruct((n, w), dt) for w, dt in widths],
        grid=(batch, nblk),
        in_specs=in_specs,
        out_specs=[pl.BlockSpec((tm, w), tok) for w, _ in widths],
        compiler_params=_params("arbitrary", "arbitrary"),
        name="inproj",
    )(*ins)


def _kvup_kernel(ckv_ref, kpe_ref, wk_ref, wv_ref, wpe_ref, k_o, v_o):
    cb = ckv_ref[...].astype(BF16)
    k_o[...] = (_dot(cb, wk_ref[...]) + _dot(kpe_ref[...].astype(BF16), wpe_ref[...])).astype(BF16)
    v_o[...] = _dot(cb, wv_ref[...]).astype(BF16)


def _kvup(ckv2d, kpe2d, wts):
    n = ckv2d.shape[0]
    tm = TOK_BLOCK
    hp = MLA_HEADS * LANES
    row = lambda i: (i, 0)
    ws = [wts["wk"], wts["wv"], wts["wpe"]]
    return pl.pallas_call(
        _kvup_kernel,
        out_shape=[jax.ShapeDtypeStruct((n, hp), BF16)] * 2,
        grid=(n // tm,),
        in_specs=[pl.BlockSpec((tm, ckv2d.shape[1]), row), pl.BlockSpec((tm, LANES), row)]
                 + [_const_spec(w.shape) for w in ws],
        out_specs=[pl.BlockSpec((tm, hp), row)] * 2,
        compiler_params=_params("arbitrary"),
        name="kvup",
    )(ckv2d, kpe2d, *ws)


ATTN_SCALE = (MLA_NOPE + MLA_ROPE) ** -0.5


def _attn_kernel(*refs, cached):
    if cached:
        q_ref, k_ref, v_ref, kc_ref, vc_ref, o_ref = refs
    else:
        q_ref, k_ref, v_ref, o_ref = refs
        kc_ref = vc_ref = None
    _attn_body(q_ref, k_ref, v_ref, kc_ref, vc_ref, o_ref)


def _attn_body(q_ref, k_ref, v_ref, kc_ref, vc_ref, o_ref):
    cached = kc_ref is not None
    scale = ATTN_SCALE
    for hd in range(MLA_HEADS):
        sl = slice(hd * LANES, (hd + 1) * LANES)
        q = q_ref[:, sl]
        s = _dot_nt(q, k_ref[:, sl]) * scale
        mx = jnp.max(s, axis=-1, keepdims=True)
        if cached:
            s2 = _dot_nt(q, kc_ref[:, sl]) * scale
            mx = jnp.maximum(mx, jnp.max(s2, axis=-1, keepdims=True))
        e = jnp.exp(s - mx)
        den = jnp.sum(e, axis=-1, keepdims=True)
        o = _dot(e.astype(BF16), v_ref[:, sl])
        if cached:
            e2 = jnp.exp(s2 - mx)
            den = den + jnp.sum(e2, axis=-1, keepdims=True)
            o = o + _dot(e2.astype(BF16), vc_ref[:, sl])
        o_ref[:, sl] = (o / den).astype(o_ref.dtype)


def _attn(q, k, v, batch, seq, cache=None):
    n, hp = q.shape
    tq = TOK_BLOCK
    nblk = seq // tq
    ins = [q, k, v]
    in_specs = [pl.BlockSpec((tq, hp), lambda b, i: (b * nblk + i, 0)),
                pl.BlockSpec((seq, hp), lambda b, i: (b, 0)),
                pl.BlockSpec((seq, hp), lambda b, i: (b, 0))]
    if cache is not None:
        past = cache[0].shape[0] // batch
        ins += list(cache)
        in_specs += [pl.BlockSpec((past, hp), lambda b, i: (b, 0))] * 2
    return pl.pallas_call(
        functools.partial(_attn_kernel, cached=cache is not None),
        out_shape=jax.ShapeDtypeStruct((n, hp), BF16),
        grid=(batch, nblk),
        in_specs=in_specs,
        out_specs=pl.BlockSpec((tq, hp), lambda b, i: (b * nblk + i, 0)),
        compiler_params=_params("arbitrary", "arbitrary"),
        name="attn",
    )(*ins)


def _chunk_scan(x, reverse):
    tm = x.shape[0]
    rin = lax.broadcasted_iota(jnp.int32, x.shape, 0) % HG_CHUNK
    step = 1
    while step < HG_CHUNK:
        if reverse:
            x = x + jnp.where(rin < HG_CHUNK - step, pltpu.roll(x, tm - step, 0), 0.0)
        else:
            x = x + jnp.where(rin >= step, pltpu.roll(x, step, 0), 0.0)
        step *= 2
    return x


def _hgrn_kernel(*refs, has_init):
    fwd, bwd = refs[0:4], refs[4:8]
    refs = refs[8:]
    s0_ref = None
    if has_init:
        s0_ref = refs[0]
        refs = refs[1:]
    of_ref, ob_ref, sfin_ref, st_scr = refs
    i = pl.program_id(1)
    _hgrn_body(fwd, bwd, s0_ref, of_ref, ob_ref, sfin_ref, st_scr, i == 0, i == pl.num_programs(1) - 1)


def _hgrn_body(fwd, bwd, s0_ref, of_ref, ob_ref, sfin_ref, st_scr, first, last):
    tm = fwd[0].shape[0]
    c = HG_CHUNK
    nch = tm // c
    dk, dv = HG_DK, HG_DV
    hw = HG_HEADS * dk

    def initial(dr, hd):
        return s0_ref[0, dr, hd].T if s0_ref is not None else jnp.zeros((dv, dk), F32)

    if st_scr is not None:
        @pl.when(first)
        def _init():
            for dr in range(2):
                for hd in range(HG_HEADS):
                    st_scr[dr, hd] = initial(dr, hd)

    npair = nch // 2
    pair = 2 * c
    row = lax.broadcasted_iota(jnp.int32, (tm, tm), 0)
    col = lax.broadcasted_iota(jnp.int32, (tm, tm), 1)
    same = (row // c) == (col // c)
    same_pair = (row // pair) == (col // pair)
    bd = (lax.broadcasted_iota(jnp.int32, (tm, npair * dk), 0) // pair
          == lax.broadcasted_iota(jnp.int32, (tm, npair * dk), 1) // dk)
    chunk_odd = (lax.broadcasted_iota(jnp.int32, (tm, hw), 0) // c) % 2 == 1

    for dr, (hq_ref, lf_ref, kk_ref, vv_ref) in enumerate((fwd, bwd)):
        o_ref = of_ref if dr == 0 else ob_ref
        tri = same & ((col <= row) if dr == 0 else (col >= row))
        cross = same_pair & (((row // c) > (col // c)) if dr == 0 else ((row // c) < (col // c)))
        second = chunk_odd if dr == 0 else ~chunk_odd
        bcum = _chunk_scan(lf_ref[...], reverse=dr == 1)
        closing = c - 1 if dr == 0 else 0
        btot3 = bcum.reshape(nch, c, hw)[:, closing:closing + 1, :]
        btot = jnp.broadcast_to(btot3, (nch, c, hw)).reshape(tm, hw)
        bpart = jnp.where(chunk_odd, pltpu.roll(btot, c, 0), pltpu.roll(btot, tm - c, 0))
        epart = jnp.exp(bpart)
        bpair = btot + bpart
        kk = kk_ref[...]
        qd = hq_ref[...] * jnp.exp(bcum)
        kd = kk * jnp.exp(-bcum)
        ke = kk * jnp.exp(btot - bcum)
        qd2 = jnp.where(second, qd * epart, qd)
        ke2 = jnp.where(second, ke, ke * epart)
        vv = vv_ref[...]
        order = range(npair) if dr == 0 else range(npair - 1, -1, -1)
        for hd in range(HG_HEADS):
            sl = slice(hd * dk, (hd + 1) * dk)
            qd_h = qd[:, sl].astype(BF16)
            v_h = vv[:, hd * dv:(hd + 1) * dv]
            a = jnp.where(tri, _dot_nt(qd_h, kd[:, sl].astype(BF16)),
                          jnp.where(cross, _dot_nt(qd_h, ke[:, sl].astype(BF16)), 0.0))
            o_intra = _dot(a.astype(BF16), v_h.astype(BF16))
            kebd = jnp.where(bd, jnp.concatenate([ke2[:, sl]] * npair, axis=1), 0.0).astype(BF16)
            qbd = jnp.where(bd, jnp.concatenate([qd2[:, sl]] * npair, axis=1), 0.0).astype(BF16)
            ut = _dot(v_h.T.astype(BF16), kebd)
            st = st_scr[dr, hd] if st_scr is not None else initial(dr, hd)
            prev = [None] * npair
            for p in order:
                prev[p] = st
                st = st * jnp.exp(bpair[p * pair:p * pair + 1, sl]) + ut[:, p * dk:(p + 1) * dk]
            if st_scr is not None:
                st_scr[dr, hd] = st
            o_inter = _dot_nt(qbd, jnp.concatenate(prev, axis=1).astype(BF16))
            o_ref[:, hd * dv:(hd + 1) * dv] = o_intra + o_inter

            if last is True:
                sfin_ref[0, dr, hd] = st.T
            else:
                @pl.when(last)
                def _final(st=st, dr=dr, hd=hd):
                    sfin_ref[0, dr, hd] = st.T


def _hgrn(hgx, batch, seq, s0=None):
    n = hgx.shape[0]
    tm = TOK_BLOCK
    nblk = seq // tm
    hw = HG_HEADS * HG_DK

    def spec(lane_blk, rev):
        if rev:
            return pl.BlockSpec((tm, hw), lambda b, i: (b * nblk + nblk - 1 - i, lane_blk))
        return pl.BlockSpec((tm, hw), lambda b, i: (b * nblk + i, lane_blk))

    in_specs = [spec(0, False), spec(1, False), spec(2, False), spec(5, False),
                spec(0, True), spec(3, True), spec(4, True), spec(5, True)]
    ins = [hgx] * 8
    st_shape = (1, 2, HG_HEADS, HG_DK, HG_DV)
    st_spec = pl.BlockSpec(st_shape, lambda b, i: (b, 0, 0, 0, 0))
    if s0 is not None:
        ins.append(s0)
        in_specs.append(st_spec)
    return pl.pallas_call(
        functools.partial(_hgrn_kernel, has_init=s0 is not None),
        out_shape=[jax.ShapeDtypeStruct((n, hw), F32), jax.ShapeDtypeStruct((n, hw), F32),
                   jax.ShapeDtypeStruct((batch,) + st_shape[1:], F32)],
        grid=(batch, nblk),
        in_specs=in_specs,
        out_specs=[spec(0, False), spec(0, True), st_spec],
        scratch_shapes=[pltpu.VMEM((2, HG_HEADS, HG_DV, HG_DK), F32)],
        compiler_params=_params("arbitrary", "arbitrary"),
        name="hgrn",
    )(*ins)


def _layer_norm(x, g, b):
    xc = x - jnp.mean(x, axis=-1, keepdims=True)
    var = jnp.mean(xc * xc, axis=-1, keepdims=True)
    return xc * lax.rsqrt(var + EPS) * g + b


N_POSTMIX_WEIGHTS = 8
INPROJ_KEYS = ("win", "qn", "wuq", "kvn", "wk", "wv", "wpe")
POSTMIX_KEYS = ("hgn", "womla", "wohg", "wout", "ln1g", "ln1b", "wr")
MIXER_SEQS = 2


def _postmix_kernel(x_ref, mod_ref, *refs, alpha, row0, per_batch):
    _postmix_body(x_ref, _mod_row(mod_ref, row0, per_batch), *refs, alpha=alpha)


def _postmix_body(x_ref, m, of_ref, ob_ref, zg_ref, om_ref, wg_ref, hgn_ref, womla_ref,
                  wohg_ref, wout_ref, lng_ref, lnb_ref, wr_ref, x1_o, h2_o, aff_o, *, alpha):
    d = x_ref.shape[1]
    tb = aff_o.shape[2]
    g1, sh2, sc2 = m[:, 2 * d:3 * d], m[:, 3 * d:4 * d], m[:, 4 * d:5 * d]
    o = of_ref[...] + ob_ref[...]
    zg = zg_ref[...]
    parts = []
    for hd in range(HG_HEADS):
        sl = slice(hd * HG_DV, (hd + 1) * HG_DV)
        parts.append(_rms(o[:, sl], hgn_ref[...]) * _silu(zg[:, sl]))
    ohg = jnp.concatenate(parts, axis=1).astype(BF16)
    gates = _dot_nt(_modulated(x_ref, m), wg_ref[wg_ref.shape[0] - 2 * d:, :])
    merged = (jax.nn.sigmoid(gates[:, 0:d]) * _dot(om_ref[...], womla_ref[...])
              + jax.nn.sigmoid(gates[:, d:2 * d]) * _dot(ohg, wohg_ref[...]))
    mix = _dot(merged.astype(BF16), wout_ref[...])
    x1 = _layer_norm(alpha * x_ref[...] + g1 * mix, lng_ref[...], lnb_ref[...])
    x1_o[...] = x1
    h2 = (x1 * (1.0 + sc2) + sh2).astype(BF16)
    h2_o[...] = h2
    logits = _dot_nt(wr_ref[...], h2)
    e = jnp.exp(logits - jnp.max(logits, axis=0, keepdims=True))
    aff = e / jnp.sum(e, axis=0, keepdims=True)
    for blk in range(aff_o.shape[0]):
        aff_o[blk] = aff[:, blk * tb:(blk + 1) * tb]


def _postmix(x2d, batch, seq, mod, o_f, o_b, hgx, o_mla, wts, alpha, n_experts, row0, per_batch):
    n, d = x2d.shape
    tm = TOK_BLOCK
    nblk = seq // tm
    hw = HG_HEADS * HG_DV
    tok = lambda b, i: (b * nblk + i, 0)
    weights = [wts[k] for k in POSTMIX_KEYS]
    return pl.pallas_call(
        functools.partial(_postmix_kernel, alpha=alpha, row0=row0, per_batch=per_batch),
        out_shape=[jax.ShapeDtypeStruct((n, d), F32), jax.ShapeDtypeStruct((n, d), BF16),
                   jax.ShapeDtypeStruct((n // tm, n_experts, tm), F32)],
        grid=(batch, nblk),
        in_specs=[pl.BlockSpec((tm, d), tok), _const_spec(mod.shape),
                  pl.BlockSpec((tm, hw), tok), pl.BlockSpec((tm, hw), tok),
                  pl.BlockSpec((tm, hw), lambda b, i: (b * nblk + i, 6)),
                  pl.BlockSpec((tm, o_mla.shape[1]), tok), _const_spec(wts["win"].shape)]
                 + [_const_spec(w.shape) for w in weights],
        out_specs=[pl.BlockSpec((tm, d), tok), pl.BlockSpec((tm, d), tok),
                   pl.BlockSpec((1, n_experts, tm), lambda b, i: (b * nblk + i, 0, 0))],
        compiler_params=_params("arbitrary", "arbitrary"),
        name="postmix",
    )(x2d, mod, o_f, o_b, hgx, o_mla, wts["win"], *weights)


def _mixer_kernel(x_ref, mod_ref, *refs, alpha, row0, seq):
    nw = 1 + N_INPROJ_WEIGHTS
    in_w, refs = refs[:nw], refs[nw:]
    pm_w, refs = refs[:N_POSTMIX_WEIGHTS - 1], refs[N_POSTMIX_WEIGHTS - 1:]
    pm_w = (in_w[1],) + tuple(pm_w)
    x1_o, h2_o, aff_o, ckv_o, kpe_o, sfin_o, q_s, k_s, v_s, hgx_s, om_s, of_s, ob_s = refs
    m = _mod_row(mod_ref, row0, False)
    _inproj_body(x_ref, m, *in_w, q_s, k_s, v_s, ckv_o, kpe_o, hgx_s, rope=False)
    hw = HG_HEADS * HG_DK
    for s in range(x_ref.shape[0] // seq):
        rows = slice(s * seq, (s + 1) * seq)
        _attn_body(q_s.at[rows], k_s.at[rows], v_s.at[rows], None, None, om_s.at[rows])
        lane = lambda j: hgx_s.at[rows, j * hw:(j + 1) * hw]
        _hgrn_body((lane(0), lane(1), lane(2), lane(5)), (lane(0), lane(3), lane(4), lane(5)), None,
                   of_s.at[rows], ob_s.at[rows], sfin_o.at[s:s + 1], None, True, True)
    _postmix_body(x_ref, m, of_s, ob_s, hgx_s.at[:, 6 * hw:7 * hw], om_s, *pm_w, x1_o, h2_o, aff_o,
                  alpha=alpha)


def _mixer(x2d, batch, seq, mod, gamma, wts, alpha, n_experts, row0):
    n, d = x2d.shape
    assert seq == TOK_BLOCK
    ns = MIXER_SEQS if batch % MIXER_SEQS == 0 else 1
    tm = ns * seq
    hp = MLA_HEADS * LANES
    hw = HG_HEADS * HG_DK
    kvl = wts["kvn"].shape[1]
    weights = [wts[k] for k in INPROJ_KEYS + POSTMIX_KEYS]
    tok = lambda b: (b, 0)
    st_shape = (ns, 2, HG_HEADS, HG_DK, HG_DV)
    return pl.pallas_call(
        functools.partial(_mixer_kernel, alpha=alpha, row0=row0, seq=seq),
        out_shape=[jax.ShapeDtypeStruct((n, d), F32), jax.ShapeDtypeStruct((n, d), BF16),
                   jax.ShapeDtypeStruct((n // seq, n_experts, seq), F32),
                   jax.ShapeDtypeStruct((n, kvl), F32), jax.ShapeDtypeStruct((n, LANES), F32),
                   jax.ShapeDtypeStruct((batch,) + st_shape[1:], F32)],
        grid=(batch // ns,),
        in_specs=[pl.BlockSpec((tm, d), tok), _const_spec(mod.shape), _const_spec(gamma.shape)]
                 + [_const_spec(w.shape) for w in weights],
        out_specs=[pl.BlockSpec((tm, d), tok), pl.BlockSpec((tm, d), tok),
                   pl.BlockSpec((ns, n_experts, seq), lambda b: (b, 0, 0)),
                   pl.BlockSpec((tm, kvl), tok), pl.BlockSpec((tm, LANES), tok),
                   pl.BlockSpec(st_shape, lambda b: (b, 0, 0, 0, 0))],
        scratch_shapes=[pltpu.VMEM((tm, hp), BF16)] * 3
                       + [pltpu.VMEM((tm, 7 * hw), F32), pltpu.VMEM((tm, hp), BF16),
                          pltpu.VMEM((tm, hw), F32), pltpu.VMEM((tm, hw), F32)],
        compiler_params=_params("arbitrary"),
        name="mixer",
    )(x2d, mod, gamma, *weights)


def _route_kernel(aff_ref, rank_o, cnt_o, *, cap):
    nb, ne, tb = aff_ref.shape
    key = aff_ref[...]

    def count(mask):
        return jnp.sum(jnp.sum(jnp.where(mask, 1.0, 0.0), axis=0), axis=1, keepdims=True)

    def bit_step(it, bits):
        cand = bits | jnp.left_shift(jnp.int32(1), 30 - it)
        return jnp.where(count(key >= pltpu.bitcast(cand, F32)[None]) >= cap, cand, bits)

    bits = lax.fori_loop(0, 31, bit_step, jnp.zeros((ne, 1), jnp.int32))
    thr = pltpu.bitcast(bits, F32)
    need = cap - count(key > thr[None])
    before = (lax.broadcasted_iota(jnp.int32, (tb, tb), 0)
              < lax.broadcasted_iota(jnp.int32, (tb, tb), 1))
    before = jnp.where(before, 1.0, 0.0).astype(BF16)
    off_eq = jnp.zeros((ne, 1), F32)
    off_sel = jnp.zeros((ne, 1), F32)
    cnt_o[...] = jnp.zeros_like(cnt_o)
    for blk in range(nb):
        key_b = key[blk]
        eq = key_b == thr
        eq_b = jnp.where(eq, 1.0, 0.0)
        eq_rank = _dot(eq_b.astype(BF16), before) + off_eq
        sel = (key_b > thr) | (eq & (eq_rank < need))
        sel_b = jnp.where(sel, 1.0, 0.0)
        rank = _dot(sel_b.astype(BF16), before) + off_sel
        rank_o[blk] = jnp.where(sel, rank.astype(jnp.int32), UNSELECTED)
        cnt_o[:, blk:blk + 1] = off_sel.astype(jnp.int32)
        off_eq = off_eq + jnp.sum(eq_b, axis=1, keepdims=True)
        off_sel = off_sel + jnp.sum(sel_b, axis=1, keepdims=True)
    cnt_o[:, nb:nb + 1] = off_sel.astype(jnp.int32)


def _route(aff, cap):
    nb, ne, tb = aff.shape
    assert nb + 1 <= LANES
    rank, cnt = pl.pallas_call(
        functools.partial(_route_kernel, cap=cap),
        out_shape=[jax.ShapeDtypeStruct(aff.shape, jnp.int32), jax.ShapeDtypeStruct((ne, LANES), jnp.int32)],
        in_specs=[pl.BlockSpec(memory_space=pltpu.VMEM)],
        out_specs=[pl.BlockSpec(memory_space=pltpu.VMEM)] * 2,
        compiler_params=pltpu.CompilerParams(vmem_limit_bytes=VMEM_LIMIT),
        name="route",
    )(aff)
    return rank, cnt[:, :nb + 1]


def _window_hits(rk_ref, firsts, slot0, win):
    ne, tb = rk_ref.shape[1], rk_ref.shape[2]
    win_iota = lax.broadcasted_iota(jnp.int32, (win, tb), 0)
    return [(rk_ref[0, e:e + 1, :] + (slot0 - firsts[e])) == win_iota for e in range(ne)]


def _compact_kernel(first_ref, end_ref, rounds_ref, *refs, groups, slots):
    ng = len(groups)
    h2_refs, rk_refs, af_refs = refs[0:ng], refs[ng:2 * ng], refs[2 * ng:3 * ng]
    xe_hbm, stage, tail, sem, issued = refs[3 * ng:]
    b = pl.program_id(0)
    ne = rk_refs[0].shape[1]
    win = WIN_ROWS
    d = h2_refs[0].shape[1]
    sub = SUBLANES

    def copies(slot, dsts):
        return [pltpu.make_async_copy(stage.at[slot, pl.ds(e * win, win), :],
                                      xe_hbm.at[e, pl.ds(pl.multiple_of(dsts[e], sub), win), :], sem.at[e])
                for e in range(ne)]

    def wait_previous():
        @pl.when(issued[0] > 0)
        def _():
            for cp in copies(0, [0] * ne):
                cp.wait()

    @pl.when(b == 0)
    def _init():
        issued[0] = 0
        tail[...] = jnp.zeros_like(tail)
        stage[1] = jnp.zeros(stage.shape[1:], stage.dtype)
        pad = copies(1, [slots] * ne)
        for cp in pad:
            cp.start()
        for cp in pad:
            cp.wait()

    def group_body(h2_ref, rk_ref, af_ref, slot0):
        firsts = [first_ref[b * ne + e] for e in range(ne)]
        bases = [(f // sub) * sub for f in firsts]
        ends = [end_ref[b * ne + e] - bases[e] for e in range(ne)]
        sub_iota = lax.broadcasted_iota(jnp.int32, (sub, stage.shape[2]), 0)

        def one_round(r, carry):
            dsts = [bases[e] + r * win for e in range(ne)]
            hits = _window_hits(rk_ref, dsts, slot0, win)
            onehot = jnp.where(jnp.concatenate(hits, axis=0), 1.0, 0.0).astype(BF16)
            rows = _dot(onehot, h2_ref[...])
            gate = jnp.concatenate(
                [jnp.sum(jnp.where(hits[e], af_ref[0, e:e + 1, :], 0.0), axis=1, keepdims=True)
                 for e in range(ne)], axis=0)
            slot = issued[0] % 2
            stage[slot, :, 0:d] = rows
            stage[slot, :, d:] = jnp.broadcast_to(gate, (ne * win, LANES))
            for e in range(ne):
                @pl.when(r == 0)
                def _head(e=e):
                    head = stage[slot, e * win:e * win + sub, :]
                    stage[slot, e * win:e * win + sub, :] = jnp.where(
                        sub_iota < firsts[e] - bases[e], tail[e * sub:(e + 1) * sub, :], head)

                last = (ends[e] // sub) * sub
                @pl.when(r == last // win)
                def _tail(e=e, last=last):
                    tail[e * sub:(e + 1) * sub, :] = stage[
                        slot, pl.ds(pl.multiple_of(e * win + last % win, sub), sub), :]
            wait_previous()
            for cp in copies(slot, [jnp.minimum(dst, slots) for dst in dsts]):
                cp.start()
            issued[0] = issued[0] + 1
            return carry

        lax.fori_loop(0, rounds_ref[b], one_round, 0)

    blk0 = 0
    for gi, g in enumerate(groups):
        @pl.when((b >= blk0) & (b < blk0 + g["nb"]))
        def _(gi=gi, g=g):
            group_body(h2_refs[gi], rk_refs[gi], af_refs[gi], g["slot0"])
        blk0 += g["nb"]

    @pl.when(b == pl.num_programs(0) - 1)
    def _drain():
        wait_previous()


def _compact(groups, first, end, rounds, slots):
    d = groups[0]["h2"].shape[1]
    nbs = [g["rank"].shape[0] for g in groups]
    ne, tb = groups[0]["rank"].shape[1:]
    meta, specs_h2, specs_rk = [], [], []
    blk0 = 0
    for g, nb in zip(groups, nbs):
        meta.append(dict(nb=nb, slot0=g["slot0"]))
        local = lambda b, *_, blk0=blk0, nb=nb: jnp.clip(b - blk0, 0, nb - 1)
        specs_h2.append(pl.BlockSpec((tb, d), lambda b, *_, local=local: (local(b), 0)))
        specs_rk.append(pl.BlockSpec((1, ne, tb), lambda b, *_, local=local: (local(b), 0, 0)))
        blk0 += nb
    width = d + LANES
    return pl.pallas_call(
        functools.partial(_compact_kernel, groups=meta, slots=slots),
        out_shape=jax.ShapeDtypeStruct((ne, slots + WIN_ROWS, width), F32),
        grid_spec=pltpu.PrefetchScalarGridSpec(
            num_scalar_prefetch=3,
            grid=(sum(nbs),),
            in_specs=specs_h2 + specs_rk + specs_rk,
            out_specs=pl.BlockSpec(memory_space=pl.ANY),
            scratch_shapes=[pltpu.VMEM((2, ne * WIN_ROWS, width), F32),
                            pltpu.VMEM((ne * SUBLANES, width), F32),
                            pltpu.SemaphoreType.DMA((ne,)), pltpu.SMEM((1,), jnp.int32)]),
        compiler_params=_params("arbitrary"),
        name="compact",
    )(first, end, rounds, *[g["h2"] for g in groups], *[g["rank"] for g in groups],
      *[g["aff"] for g in groups])


def _ffn_kernel(xe_ref, w1_ref, w3_ref, w2_ref, ye_ref, x_scr, g_scr, acc_scr):
    f = pl.program_id(1)
    d = x_scr.shape[1]

    @pl.when(f == 0)
    def _unpack():
        x_scr[...] = xe_ref[0, :, 0:d].astype(BF16)
        g_scr[...] = xe_ref[0, :, d:d + 1]

    x = x_scr[...]
    hid = _silu(_dot(x, w1_ref[0].astype(BF16))) * _dot(x, w3_ref[0].astype(BF16))
    y = _dot(hid.astype(BF16), w2_ref[0].astype(BF16))

    last = pl.num_programs(1) - 1

    @pl.when((f == 0) & (f < last))
    def _first():
        acc_scr[...] = y

    @pl.when((f > 0) & (f < last))
    def _middle():
        acc_scr[...] += y

    @pl.when((f == last) & (f > 0))
    def _last():
        ye_ref[0] = ((acc_scr[...] + y) * g_scr[...]).astype(ye_ref.dtype)

    @pl.when((f == last) & (f == 0))
    def _only():
        ye_ref[0] = (y * g_scr[...]).astype(ye_ref.dtype)


def _ffn(xe, w1, w3, w2, slots, ft):
    ne, d, dff = w1.shape
    nf = dff // ft
    return pl.pallas_call(
        _ffn_kernel,
        out_shape=jax.ShapeDtypeStruct((ne, slots, d), BF16),
        grid=(ne, nf),
        in_specs=[pl.BlockSpec((1, slots, xe.shape[2]), lambda e, f: (e, 0, 0)),
                  pl.BlockSpec((1, d, ft), lambda e, f: (e, 0, f)),
                  pl.BlockSpec((1, d, ft), lambda e, f: (e, 0, f)),
                  pl.BlockSpec((1, ft, d), lambda e, f: (e, f, 0))],
        out_specs=pl.BlockSpec((1, slots, d), lambda e, f: (e, 0, 0)),
        scratch_shapes=[pltpu.VMEM((slots, d), BF16), pltpu.VMEM((slots, 1), F32),
                        pltpu.VMEM((slots, d), F32)],
        compiler_params=_params("arbitrary", "arbitrary"),
        name="ffn",
    )(xe, w1, w3, w2)


def _combine_kernel(first_ref, rounds_ref, rk_ref, x1_ref, mod_ref, lng_ref, lnb_ref, ye_hbm, out_ref,
                    buf, acc_scr, sem, *, d_model, alpha, slot0, slots, row0, blocks_per_batch):
    d = d_model
    b = pl.program_id(0)
    nblk = pl.num_programs(0)
    ne, tb = rk_ref.shape[1], rk_ref.shape[2]
    win = WIN_ROWS
    eye = (lax.broadcasted_iota(jnp.int32, (tb, tb), 0)
           == lax.broadcasted_iota(jnp.int32, (tb, tb), 1))
    eye = jnp.where(eye, 1.0, 0.0).astype(BF16)

    def starts_of(blk, r):
        firsts = [(first_ref[blk * ne + e] // BF16_ROWS) * BF16_ROWS + r * win for e in range(ne)]
        return firsts, [jnp.minimum(f, slots - win) for f in firsts]

    def windows(slot, starts):
        return [pltpu.make_async_copy(ye_hbm.at[e, pl.ds(pl.multiple_of(starts[e], BF16_ROWS), win), :],
                                      buf.at[slot, pl.ds(e * win, win), :], sem.at[slot, e])
                for e in range(ne)]

    def scatter(slot, firsts, starts):
        hits = _window_hits(rk_ref, starts, slot0, win)
        hits = [h & ((rk_ref[0, e:e + 1, :] + slot0) >= firsts[e]) for e, h in enumerate(hits)]
        hit = jnp.where(jnp.concatenate(hits, axis=0), 1.0, 0.0).astype(BF16)
        hit_t = _dot_nt(eye, hit).astype(BF16)
        return _dot(hit_t, buf[slot])

    cur = b % 2

    @pl.when(b == 0)
    def _prime():
        for cp in windows(0, starts_of(0, 0)[1]):
            cp.start()

    @pl.when(b + 1 < nblk)
    def _prefetch():
        for cp in windows(1 - cur, starts_of(b + 1, 0)[1]):
            cp.start()

    firsts, starts = starts_of(b, 0)
    for cp in windows(cur, starts):
        cp.wait()
    acc_scr[...] = scatter(cur, firsts, starts)

    def extra_round(r, carry):
        firsts, starts = starts_of(b, r)
        for cp in windows(2, starts):
            cp.start()
        for cp in windows(2, starts):
            cp.wait()
        acc_scr[...] += scatter(2, firsts, starts)
        return carry

    lax.fori_loop(1, rounds_ref[b], extra_round, 0)
    r = row0 + b // blocks_per_batch
    g2 = mod_ref[pl.ds(r, 1), :][:, 5 * d:6 * d]
    out_ref[...] = _layer_norm(alpha * x1_ref[...] + g2 * acc_scr[...], lng_ref[...], lnb_ref[...])


def _combine(ye, rank, first, rounds, x1, mod, ln_g, ln_b, alpha, slot0, row0, blocks_per_batch):
    n, d = x1.shape
    nb, ne, tb = rank.shape
    slots = ye.shape[1]
    return pl.pallas_call(
        functools.partial(_combine_kernel, d_model=d, alpha=alpha, slot0=slot0, slots=slots, row0=row0,
                          blocks_per_batch=blocks_per_batch),
        out_shape=jax.ShapeDtypeStruct((n, d), F32),
        grid_spec=pltpu.PrefetchScalarGridSpec(
            num_scalar_prefetch=2,
            grid=(nb,),
            in_specs=[pl.BlockSpec((1, ne, tb), lambda b, *_: (b, 0, 0)),
                      pl.BlockSpec((tb, d), lambda b, *_: (b, 0)),
                      pl.BlockSpec(mod.shape, lambda b, *_: (0, 0)),
                      pl.BlockSpec((1, d), lambda b, *_: (0, 0)),
                      pl.BlockSpec((1, d), lambda b, *_: (0, 0)),
                      pl.BlockSpec(memory_space=pl.ANY)],
            out_specs=pl.BlockSpec((tb, d), lambda b, *_: (b, 0)),
            scratch_shapes=[pltpu.VMEM((3, ne * WIN_ROWS, d), ye.dtype), pltpu.VMEM((tb, d), F32),
                            pltpu.SemaphoreType.DMA((3, ne))]),
        compiler_params=_params("arbitrary"),
        name="combine",
    )(first, rounds, rank, x1, mod, ln_g, ln_b, ye)


def _prep_weights(w_in, q_norm, w_uq, kv_norm, w_ukv, w_o_mla, hgrn_norm, w_o_hg, w_out, ln1_g, ln1_b,
                  w_router):
    d = w_in.shape[0]
    q_lora, kv_lora = q_norm.shape[0], kv_norm.shape[0]
    hw = HG_HEADS * HG_DK
    hh, hp = MLA_HEADS, MLA_HEADS * LANES
    o_kv, o_pe = q_lora, q_lora + kv_lora
    o_h = o_pe + MLA_ROPE
    o_g = o_h + 5 * hw
    assert w_in.shape[1] == o_g + 2 * d
    qk = MLA_NOPE + MLA_ROPE
    kvw = MLA_NOPE + MLA_V
    b16 = lambda a: a.astype(BF16)
    assert all(o % BF16_ROWS == 0 for o in (o_kv, o_pe, o_h, o_g))
    win = b16(w_in.T)
    wuq = jnp.pad(w_uq.reshape(q_lora, hh, qk), ((0, 0), (0, 0), (0, LANES - qk))).reshape(q_lora, hp)
    ukv = w_ukv.reshape(kv_lora, hh, kvw)
    wk = jnp.pad(ukv[:, :, :MLA_NOPE], ((0, 0), (0, 0), (0, LANES - MLA_NOPE))).reshape(kv_lora, hp)
    wv = jnp.pad(ukv[:, :, MLA_NOPE:], ((0, 0), (0, 0), (0, LANES - MLA_V))).reshape(kv_lora, hp)
    place = np.pad(np.eye(MLA_ROPE, dtype=np.float32), ((0, LANES - MLA_ROPE), (MLA_NOPE, LANES - qk)))
    wpe = jnp.asarray(np.tile(place, (1, hh)), BF16)
    womla = jnp.pad(w_o_mla.reshape(hh, MLA_V, d), ((0, 0), (0, LANES - MLA_V), (0, 0))).reshape(hp, d)
    return dict(
        win=win, qn=q_norm.reshape(1, -1), wuq=b16(wuq), kvn=kv_norm.reshape(1, -1), wk=b16(wk), wv=b16(wv),
        wpe=wpe, hgn=hgrn_norm.reshape(1, -1), womla=b16(womla), wohg=b16(w_o_hg), wout=b16(w_out),
        ln1g=ln1_g.reshape(1, -1), ln1b=ln1_b.reshape(1, -1), wr=b16(w_router.T))


def _rope_tables(seq):
    n_freq = MLA_ROPE // 4
    inv = ROPE_BASE ** (-np.arange(n_freq, dtype=np.float64) / n_freq)
    t = np.arange(seq)
    ang = np.concatenate([(t // GRID_W)[:, None] * inv, (t % GRID_W)[:, None] * inv], axis=-1)
    cos = np.repeat(np.cos(ang), 2, axis=1)
    sin = np.repeat(np.sin(ang), 2, axis=1) * np.tile([-1.0, 1.0], MLA_ROPE // 2)
    ck = np.pad(cos, ((0, 0), (0, LANES - MLA_ROPE)), constant_values=1.0)
    sk = np.pad(sin, ((0, 0), (0, LANES - MLA_ROPE)))
    cq = np.pad(cos, ((0, 0), (MLA_NOPE, LANES - MLA_NOPE - MLA_ROPE)), constant_values=1.0)
    sq = np.pad(sin, ((0, 0), (MLA_NOPE, LANES - MLA_NOPE - MLA_ROPE)))
    return tuple(jnp.asarray(a, F32) for a in (cq, sq, ck, sk))


def _window_sched(cnt, slot0):
    first = slot0 + cnt[:, :-1]
    end = slot0 + cnt[:, 1:]
    flat = lambda a: a.T.reshape(-1).astype(jnp.int32)

    def rounds(align):
        need = jnp.max((end - (first // align) * align + WIN_ROWS - 1) // WIN_ROWS, axis=0)
        return jnp.maximum(need, 1).astype(jnp.int32)

    return flat(first), flat(end), rounds(SUBLANES), rounds(BF16_ROWS)


def kernel(x_prompt, x_sample, c, cache_ckv, cache_kpe, state_hgrn, c_ctx, w_ada, b_ada, w_in, mla_q_norm, mla_w_uq, mla_kv_norm, mla_w_ukv, mla_w_o, hgrn_gamma, hgrn_norm, hgrn_w_o, w_out, ln1_g, ln1_b, moe_w_router, moe_w1, moe_w3, moe_w2, ln2_g, ln2_b):
    depth = w_ada.shape[0]
    assert depth == 1, "single trunk layer"
    bp, tp, d = x_prompt.shape
    bs, tsq, _ = x_sample.shape
    ne = moe_w_router.shape[-1]
    alpha = (2 * depth) ** 0.25
    past = cache_ckv.shape[2]
    assert tp % TOK_BLOCK == 0 and tsq % TOK_BLOCK == 0 and past % TOK_BLOCK == 0 and tsq % GRID_W == 0

    wts = _prep_weights(w_in[0], mla_q_norm[0], mla_w_uq[0], mla_kv_norm[0], mla_w_ukv[0], mla_w_o[0],
                        hgrn_norm[0], hgrn_w_o[0], w_out[0], ln1_g[0], ln1_b[0], moe_w_router[0])
    cond_rows = -(-(1 + bs) // SUBLANES) * SUBLANES
    cond = jnp.concatenate([c_ctx[None], c, jnp.zeros((cond_rows - 1 - bs, d), F32)], axis=0)
    mod = _adaln(cond, w_ada[0], b_ada[0])

    xs = [x_prompt.reshape(bp * tp, d), x_sample.reshape(bs * tsq, d)]
    dims = [(bp, tp), (bs, tsq)]
    rows = [(0, False), (1, True)]
    ropes = [None, _rope_tables(tsq)]
    kpe_c = jnp.pad(cache_kpe[:, 0].reshape(bs * past, MLA_ROPE), ((0, 0), (0, LANES - MLA_ROPE)))
    caches = [None, _kvup(cache_ckv[:, 0].reshape(bs * past, -1), kpe_c, wts)]
    inits = [None, state_hgrn[:, 0]]

    x1s, h2s, affs, extras = [], [], [], []
    for gi in range(2):
        (bt, sq), (row0, per_batch) = dims[gi], rows[gi]
        if sq == TOK_BLOCK and caches[gi] is None and ropes[gi] is None and not per_batch:
            x1, h2, aff, ckv, kpe, s_fin = _mixer(xs[gi], bt, sq, mod, hgrn_gamma, wts, alpha, ne, row0)
        else:
            q, k, v, ckv, kpe, hgx = _inproj(xs[gi], bt, sq, mod, hgrn_gamma, wts, row0, per_batch, ropes[gi])
            o_mla = _attn(q, k, v, bt, sq, caches[gi])
            o_f, o_b, s_fin = _hgrn(hgx, bt, sq, inits[gi])
            x1, h2, aff = _postmix(xs[gi], bt, sq, mod, o_f, o_b, hgx, o_mla, wts, alpha, ne, row0, per_batch)
        x1s.append(x1)
        h2s.append(h2)
        affs.append(aff)
        extras.append((ckv, kpe, s_fin))

    caps = [EC_FACTOR * x.shape[0] // ne for x in xs]
    slots = sum(caps)
    assert all(cp % BF16_ROWS == 0 for cp in caps) and slots >= WIN_ROWS
    groups, scheds = [], []
    slot0 = 0
    for gi in range(2):
        rank, cnt = _route(affs[gi], caps[gi])
        groups.append(dict(h2=h2s[gi], rank=rank, aff=affs[gi], slot0=slot0))
        scheds.append(_window_sched(cnt, slot0))
        slot0 += caps[gi]
    xe = _compact(groups, *[jnp.concatenate([s[k] for s in scheds]) for k in range(3)], slots)
    ye = _ffn(xe, moe_w1[0], moe_w3[0], moe_w2[0], slots, ft=512)

    outs = []
    for gi in range(2):
        outs.append(_combine(ye, groups[gi]["rank"], scheds[gi][0], scheds[gi][3], x1s[gi], mod,
                             ln2_g[0].reshape(1, -1), ln2_b[0].reshape(1, -1), alpha, groups[gi]["slot0"],
                             rows[gi][0], dims[gi][1] // TOK_BLOCK if rows[gi][1] else 1 << 30))

    ckv_p, kpe_p, st_p = extras[0]
    y_prompt = outs[0].reshape(bp, tp, d)
    y_sample = outs[1].reshape(bs, tsq, d)
    new_ckv = ckv_p.reshape(bp, 1, tp, -1)
    new_kpe = kpe_p[:, :MLA_ROPE].reshape(bp, 1, tp, MLA_ROPE)
    new_state = st_p.reshape(bp, 1, 2, HG_HEADS, HG_DK, HG_DV)
    return (y_prompt, y_sample, new_ckv, new_kpe, new_state)
```

```python
import functools

import jax
import jax.numpy as jnp
import numpy as np
from jax import lax
from jax.experimental import pallas as pl
from jax.experimental.pallas import tpu as pltpu

F32 = jnp.float32
BF16 = jnp.bfloat16

MLA_HEADS = 8
MLA_NOPE = 64
MLA_ROPE = 32
MLA_V = 64
HG_HEADS = 4
HG_DK = 128
HG_DV = 128
HG_CHUNK = 32
GRID_W = 64
ROPE_BASE = 10000.0
EC_FACTOR = 2
EPS = 1e-6

LANES = 128
SUBLANES = 8
BF16_ROWS = 16
VMEM_LIMIT = 56 * 1024 * 1024

TOK_BLOCK = 256
WIN_ROWS = 64
UNSELECTED = -(1 << 30)

NT_DIMS = (((1,), (1,)), ((), ()))


def _dot(a, b):
    return jnp.dot(a, b, preferred_element_type=F32)


def _dot_nt(a, b):
    return lax.dot_general(a, b, NT_DIMS, preferred_element_type=F32)


def _silu(x):
    return x * jax.nn.sigmoid(x)


def _params(*sem):
    return pltpu.CompilerParams(dimension_semantics=sem, vmem_limit_bytes=VMEM_LIMIT)


def _const_spec(shape):
    zeros = (0,) * len(shape)
    return pl.BlockSpec(shape, lambda *_: zeros, pipeline_mode=pl.Buffered(1))


def _adaln_kernel(c_ref, w_ref, b_ref, o_ref):
    s = _silu(c_ref[...]).astype(BF16)
    o_ref[...] = _dot(s, w_ref[...].astype(BF16)) + b_ref[...]


def _adaln(cond, w_ada, b_ada):
    rows, d = cond.shape
    n = w_ada.shape[1]
    tn = n // 4
    return pl.pallas_call(
        _adaln_kernel,
        out_shape=jax.ShapeDtypeStruct((rows, n), F32),
        grid=(n // tn,),
        in_specs=[_const_spec((rows, d)),
                  pl.BlockSpec((d, tn), lambda j: (0, j)),
                  pl.BlockSpec((1, tn), lambda j: (0, j))],
        out_specs=pl.BlockSpec((rows, tn), lambda j: (0, j)),
        compiler_params=_params("arbitrary"),
        name="adaln",
    )(cond, w_ada, b_ada.reshape(1, n))


def _rms(x, g):
    return x * lax.rsqrt(jnp.mean(x * x, axis=-1, keepdims=True) + EPS) * g


def _rope(x, c, s):
    w = x.shape[-1]
    lane = lax.broadcasted_iota(jnp.int32, x.shape, 1)
    nxt = pltpu.roll(x, w - 1, 1)
    prv = pltpu.roll(x, 1, 1)
    return x * c + jnp.where(lane % 2 == 0, nxt, prv) * s


N_INPROJ_WEIGHTS = 6


def _mod_row(mod_ref, row0, per_batch):
    r = row0 + pl.program_id(0) if per_batch else row0
    return mod_ref[pl.ds(r, 1), :]


def _modulated(x_ref, m):
    d = x_ref.shape[1]
    return (x_ref[...] * (1.0 + m[:, d:2 * d]) + m[:, 0:d]).astype(BF16)


def _keys(k_nope, kpe):
    shared = pltpu.roll(kpe, MLA_NOPE, 1)
    return (k_nope + jnp.concatenate([shared] * MLA_HEADS, axis=1)).astype(BF16)


def _inproj_kernel(*refs, row0, per_batch, rope):
    x_ref, mod_ref = refs[:2]
    _inproj_body(x_ref, _mod_row(mod_ref, row0, per_batch), *refs[2:], rope=rope)


def _inproj_body(x_ref, m, *refs, rope):
    gam_ref, win_ref, qn_ref, wuq_ref, kvn_ref, wk_ref, wv_ref = refs[:1 + N_INPROJ_WEIGHTS]
    refs = refs[1 + N_INPROJ_WEIGHTS:]
    if rope:
        cq_ref, sq_ref, ck_ref, sk_ref = refs[:4]
        refs = refs[4:]
    q_o, k_o, v_o, ckv_o, kpe_o, hgx_o = refs
    h = _modulated(x_ref, m)
    hw = HG_HEADS * HG_DK
    o_kv = qn_ref.shape[1]
    o_pe = o_kv + kvn_ref.shape[1]
    o_h = o_pe + MLA_ROPE

    cq = _rms(_dot_nt(h, win_ref[0:o_kv, :]), qn_ref[...])
    q = _dot(cq.astype(BF16), wuq_ref[...])
    if rope:
        q = _rope(q, jnp.concatenate([cq_ref[...]] * MLA_HEADS, axis=1),
                  jnp.concatenate([sq_ref[...]] * MLA_HEADS, axis=1))
    q_o[...] = q.astype(BF16)

    ckv = _rms(_dot_nt(h, win_ref[o_kv:o_pe, :]), kvn_ref[...])
    ckv_o[...] = ckv
    kpe = _dot_nt(h, win_ref[o_pe:o_h, :])
    kpe = jnp.concatenate([kpe, jnp.zeros((kpe.shape[0], LANES - MLA_ROPE), F32)], axis=1)
    if rope:
        kpe = _rope(kpe, ck_ref[...], sk_ref[...])
    kpe_o[...] = kpe
    cb = ckv.astype(BF16)
    k_o[...] = _keys(_dot(cb, wk_ref[...]), kpe)
    v_o[...] = _dot(cb, wv_ref[...]).astype(BF16)

    z = _dot_nt(h, win_ref[o_h:o_h + 5 * hw, :])
    hgx_o[:, 0:hw] = _silu(z[:, 0:hw])
    for dr in range(2):
        g0, g1 = gam_ref[dr, 0:1, :], gam_ref[dr, 1:2, :]
        gmax = jnp.maximum(g0, g1)
        e0, e1 = jnp.exp(g0 - gmax), jnp.exp(g1 - gmax)
        lb = e0 / (e0 + e1)
        f = lb + (1.0 - lb) * jax.nn.sigmoid(z[:, (1 + dr) * hw:(2 + dr) * hw])
        hgx_o[:, (1 + 2 * dr) * hw:(2 + 2 * dr) * hw] = jnp.log(f)
        hgx_o[:, (2 + 2 * dr) * hw:(3 + 2 * dr) * hw] = 1.0 - f
    hgx_o[:, 5 * hw:6 * hw] = z[:, 3 * hw:4 * hw]
    hgx_o[:, 6 * hw:7 * hw] = z[:, 4 * hw:5 * hw]


def _inproj(x2d, batch, seq, mod, gamma, wts, row0, per_batch, rope_tabs):
    n, d = x2d.shape
    tm = TOK_BLOCK
    nblk = seq // tm
    rope = rope_tabs is not None
    hp = MLA_HEADS * LANES
    hw = HG_HEADS * HG_DK
    tok = lambda b, i: (b * nblk + i, 0)
    pos = lambda b, i: (i, 0)
    weights = [wts[k] for k in INPROJ_KEYS]
    ins = [x2d, mod, gamma] + weights
    in_specs = ([pl.BlockSpec((tm, d), tok), _const_spec(mod.shape), _const_spec(gamma.shape)]
                + [_const_spec(w.shape) for w in weights])
    if rope:
        ins += list(rope_tabs)
        in_specs += [pl.BlockSpec((tm, t.shape[1]), pos) for t in rope_tabs]
    widths = [(hp, BF16), (hp, BF16), (MLA_HEADS * MLA_V, BF16), (wts["kvn"].shape[1], F32), (LANES, F32),
              (7 * hw, F32)]
    return pl.pallas_call(
        functools.partial(_inproj_kernel, row0=row0, per_batch=per_batch, rope=rope),
        out_shape=[jax.ShapeDtypeStruct((n, w), dt) for w, dt in widths],
        grid=(batch, nblk),
        in_specs=in_specs,
        out_specs=[pl.BlockSpec((tm, w), tok) for w, _ in widths],
        compiler_params=_params("arbitrary", "arbitrary"),
        name="inproj",
    )(*ins)


def _kvup_kernel(ckv_ref, kpe_ref, wk_ref, wv_ref, k_o, v_o):
    cb = ckv_ref[...].astype(BF16)
    k_o[...] = _keys(_dot(cb, wk_ref[...]), kpe_ref[...])
    v_o[...] = _dot(cb, wv_ref[...]).astype(BF16)


def _kvup(ckv2d, kpe2d, wts):
    n = ckv2d.shape[0]
    tm = TOK_BLOCK
    widths = [MLA_HEADS * LANES, MLA_HEADS * MLA_V]
    row = lambda i: (i, 0)
    ws = [wts["wk"], wts["wv"]]
    return pl.pallas_call(
        _kvup_kernel,
        out_shape=[jax.ShapeDtypeStruct((n, w), BF16) for w in widths],
        grid=(n // tm,),
        in_specs=[pl.BlockSpec((tm, ckv2d.shape[1]), row), pl.BlockSpec((tm, LANES), row)]
                 + [_const_spec(w.shape) for w in ws],
        out_specs=[pl.BlockSpec((tm, w), row) for w in widths],
        compiler_params=_params("arbitrary"),
        name="kvup",
    )(ckv2d, kpe2d, *ws)


ATTN_SCALE = (MLA_NOPE + MLA_ROPE) ** -0.5


def _attn_kernel(*refs, cached):
    if cached:
        q_ref, k_ref, v_ref, kc_ref, vc_ref, o_ref = refs
    else:
        q_ref, k_ref, v_ref, o_ref = refs
        kc_ref = vc_ref = None
    _attn_body(q_ref, k_ref, v_ref, kc_ref, vc_ref, o_ref)


def _attn_body(q_ref, k_ref, v_ref, kc_ref, vc_ref, o_ref):
    cached = kc_ref is not None
    scale = ATTN_SCALE
    per_slab = LANES // MLA_V
    own = lax.broadcasted_iota(jnp.int32, (q_ref.shape[0], LANES), 1) // MLA_V
    for slab in range(MLA_HEADS // per_slab):
        vsl = slice(slab * LANES, (slab + 1) * LANES)
        out = None
        for sub in range(per_slab):
            hd = slab * per_slab + sub
            sl = slice(hd * LANES, (hd + 1) * LANES)
            q = q_ref[:, sl]
            s = _dot_nt(q, k_ref[:, sl]) * scale
            mx = jnp.max(s, axis=-1, keepdims=True)
            if cached:
                s2 = _dot_nt(q, kc_ref[:, sl]) * scale
                mx = jnp.maximum(mx, jnp.max(s2, axis=-1, keepdims=True))
            e = jnp.exp(s - mx)
            den = jnp.sum(e, axis=-1, keepdims=True)
            o = _dot(e.astype(BF16), v_ref[:, vsl])
            if cached:
                e2 = jnp.exp(s2 - mx)
                den = den + jnp.sum(e2, axis=-1, keepdims=True)
                o = o + _dot(e2.astype(BF16), vc_ref[:, vsl])
            o = o / den
            out = o if out is None else jnp.where(own == sub, o, out)
        o_ref[:, vsl] = out.astype(o_ref.dtype)


def _attn(q, k, v, batch, seq, cache=None):
    n, hp = q.shape
    hv = v.shape[1]
    tq = TOK_BLOCK
    nblk = seq // tq
    ins = [q, k, v]
    in_specs = [pl.BlockSpec((tq, hp), lambda b, i: (b * nblk + i, 0)),
                pl.BlockSpec((seq, hp), lambda b, i: (b, 0)),
                pl.BlockSpec((seq, hv), lambda b, i: (b, 0))]
    if cache is not None:
        past = cache[0].shape[0] // batch
        ins += list(cache)
        in_specs += [pl.BlockSpec((past, hp), lambda b, i: (b, 0)),
                     pl.BlockSpec((past, hv), lambda b, i: (b, 0))]
    return pl.pallas_call(
        functools.partial(_attn_kernel, cached=cache is not None),
        out_shape=jax.ShapeDtypeStruct((n, hv), BF16),
        grid=(batch, nblk),
        in_specs=in_specs,
        out_specs=pl.BlockSpec((tq, hv), lambda b, i: (b * nblk + i, 0)),
        compiler_params=_params("arbitrary", "arbitrary"),
        name="attn",
    )(*ins)


def _chunk_scan(x, reverse):
    tm = x.shape[0]
    rin = lax.broadcasted_iota(jnp.int32, x.shape, 0) % HG_CHUNK
    step = 1
    while step < HG_CHUNK:
        if reverse:
            x = x + jnp.where(rin < HG_CHUNK - step, pltpu.roll(x, tm - step, 0), 0.0)
        else:
            x = x + jnp.where(rin >= step, pltpu.roll(x, step, 0), 0.0)
        step *= 2
    return x


def _hgrn_kernel(*refs, has_init):
    fwd, bwd = refs[0:4], refs[4:8]
    refs = refs[8:]
    s0_ref = None
    if has_init:
        s0_ref = refs[0]
        refs = refs[1:]
    of_ref, ob_ref, sfin_ref, st_scr = refs
    i = pl.program_id(1)
    _hgrn_body(fwd, bwd, s0_ref, of_ref, ob_ref, sfin_ref, st_scr, i == 0, i == pl.num_programs(1) - 1)


def _hgrn_body(fwd, bwd, s0_ref, of_ref, ob_ref, sfin_ref, st_scr, first, last):
    tm = fwd[0].shape[0]
    c = HG_CHUNK
    nch = tm // c
    dk, dv = HG_DK, HG_DV
    hw = HG_HEADS * dk

    def initial(dr, hd):
        return s0_ref[0, dr, hd].T if s0_ref is not None else jnp.zeros((dv, dk), F32)

    if st_scr is not None:
        @pl.when(first)
        def _init():
            for dr in range(2):
                for hd in range(HG_HEADS):
                    st_scr[dr, hd] = initial(dr, hd)

    npair = nch // 2
    pair = 2 * c
    row = lax.broadcasted_iota(jnp.int32, (tm, tm), 0)
    col = lax.broadcasted_iota(jnp.int32, (tm, tm), 1)
    same = (row // c) == (col // c)
    same_pair = (row // pair) == (col // pair)
    bd = (lax.broadcasted_iota(jnp.int32, (tm, npair * dk), 0) // pair
          == lax.broadcasted_iota(jnp.int32, (tm, npair * dk), 1) // dk)
    chunk_odd = (lax.broadcasted_iota(jnp.int32, (tm, hw), 0) // c) % 2 == 1

    for dr, (hq_ref, lf_ref, kk_ref, vv_ref) in enumerate((fwd, bwd)):
        o_ref = of_ref if dr == 0 else ob_ref
        tri = same & ((col <= row) if dr == 0 else (col >= row))
        cross = same_pair & (((row // c) > (col // c)) if dr == 0 else ((row // c) < (col // c)))
        second = chunk_odd if dr == 0 else ~chunk_odd
        bcum = _chunk_scan(lf_ref[...], reverse=dr == 1)
        closing = c - 1 if dr == 0 else 0
        btot3 = bcum.reshape(nch, c, hw)[:, closing:closing + 1, :]
        btot = jnp.broadcast_to(btot3, (nch, c, hw)).reshape(tm, hw)
        bpart = jnp.where(chunk_odd, pltpu.roll(btot, c, 0), pltpu.roll(btot, tm - c, 0))
        epart = jnp.exp(bpart)
        bpair = btot + bpart
        kk = kk_ref[...]
        qd = hq_ref[...] * jnp.exp(bcum)
        kd = kk * jnp.exp(-bcum)
        ke = kk * jnp.exp(btot - bcum)
        qd2 = jnp.where(second, qd * epart, qd)
        ke2 = jnp.where(second, ke, ke * epart)
        vv = vv_ref[...]
        order = range(npair) if dr == 0 else range(npair - 1, -1, -1)
        for hd in range(HG_HEADS):
            sl = slice(hd * dk, (hd + 1) * dk)
            qd_h = qd[:, sl].astype(BF16)
            v_h = vv[:, hd * dv:(hd + 1) * dv]
            a = jnp.where(tri, _dot_nt(qd_h, kd[:, sl].astype(BF16)),
                          jnp.where(cross, _dot_nt(qd_h, ke[:, sl].astype(BF16)), 0.0))
            o_intra = _dot(a.astype(BF16), v_h.astype(BF16))
            kebd = jnp.where(bd, jnp.concatenate([ke2[:, sl]] * npair, axis=1), 0.0).astype(BF16)
            qbd = jnp.where(bd, jnp.concatenate([qd2[:, sl]] * npair, axis=1), 0.0).astype(BF16)
            ut = _dot(v_h.T.astype(BF16), kebd)
            st = st_scr[dr, hd] if st_scr is not None else initial(dr, hd)
            prev = [None] * npair
            for p in order:
                prev[p] = st
                st = st * jnp.exp(bpair[p * pair:p * pair + 1, sl]) + ut[:, p * dk:(p + 1) * dk]
            if st_scr is not None:
                st_scr[dr, hd] = st
            o_inter = _dot_nt(qbd, jnp.concatenate(prev, axis=1).astype(BF16))
            o_ref[:, hd * dv:(hd + 1) * dv] = o_intra + o_inter

            if last is True:
                sfin_ref[0, dr, hd] = st.T
            else:
                @pl.when(last)
                def _final(st=st, dr=dr, hd=hd):
                    sfin_ref[0, dr, hd] = st.T


def _hgrn(hgx, batch, seq, s0=None):
    n = hgx.shape[0]
    tm = TOK_BLOCK
    nblk = seq // tm
    hw = HG_HEADS * HG_DK

    def spec(lane_blk, rev):
        if rev:
            return pl.BlockSpec((tm, hw), lambda b, i: (b * nblk + nblk - 1 - i, lane_blk))
        return pl.BlockSpec((tm, hw), lambda b, i: (b * nblk + i, lane_blk))

    in_specs = [spec(0, False), spec(1, False), spec(2, False), spec(5, False),
                spec(0, True), spec(3, True), spec(4, True), spec(5, True)]
    ins = [hgx] * 8
    st_shape = (1, 2, HG_HEADS, HG_DK, HG_DV)
    st_spec = pl.BlockSpec(st_shape, lambda b, i: (b, 0, 0, 0, 0))
    if s0 is not None:
        ins.append(s0)
        in_specs.append(st_spec)
    return pl.pallas_call(
        functools.partial(_hgrn_kernel, has_init=s0 is not None),
        out_shape=[jax.ShapeDtypeStruct((n, hw), F32), jax.ShapeDtypeStruct((n, hw), F32),
                   jax.ShapeDtypeStruct((batch,) + st_shape[1:], F32)],
        grid=(batch, nblk),
        in_specs=in_specs,
        out_specs=[spec(0, False), spec(0, True), st_spec],
        scratch_shapes=[pltpu.VMEM((2, HG_HEADS, HG_DV, HG_DK), F32)],
        compiler_params=_params("arbitrary", "arbitrary"),
        name="hgrn",
    )(*ins)


def _layer_norm(x, g, b):
    xc = x - jnp.mean(x, axis=-1, keepdims=True)
    var = jnp.mean(xc * xc, axis=-1, keepdims=True)
    return xc * lax.rsqrt(var + EPS) * g + b


N_POSTMIX_WEIGHTS = 8
INPROJ_KEYS = ("win", "qn", "wuq", "kvn", "wk", "wv")
POSTMIX_KEYS = ("hgn", "womla", "wohg", "wout", "ln1g", "ln1b", "wr")
MIXER_SEQS = 2


def _postmix_kernel(x_ref, mod_ref, *refs, alpha, row0, per_batch):
    _postmix_body(x_ref, _mod_row(mod_ref, row0, per_batch), *refs, alpha=alpha)


def _postmix_body(x_ref, m, of_ref, ob_ref, zg_ref, om_ref, wg_ref, hgn_ref, womla_ref,
                  wohg_ref, wout_ref, lng_ref, lnb_ref, wr_ref, x1_o, h2_o, aff_o, *, alpha):
    d = x_ref.shape[1]
    tb = aff_o.shape[2]
    g1, sh2, sc2 = m[:, 2 * d:3 * d], m[:, 3 * d:4 * d], m[:, 4 * d:5 * d]
    o = of_ref[...] + ob_ref[...]
    zg = zg_ref[...]
    parts = []
    for hd in range(HG_HEADS):
        sl = slice(hd * HG_DV, (hd + 1) * HG_DV)
        parts.append(_rms(o[:, sl], hgn_ref[...]) * _silu(zg[:, sl]))
    ohg = jnp.concatenate(parts, axis=1).astype(BF16)
    gates = _dot_nt(_modulated(x_ref, m), wg_ref[wg_ref.shape[0] - 2 * d:, :])
    merged = (jax.nn.sigmoid(gates[:, 0:d]) * _dot(om_ref[...], womla_ref[...])
              + jax.nn.sigmoid(gates[:, d:2 * d]) * _dot(ohg, wohg_ref[...]))
    mix = _dot(merged.astype(BF16), wout_ref[...])
    x1 = _layer_norm(alpha * x_ref[...] + g1 * mix, lng_ref[...], lnb_ref[...])
    x1_o[...] = x1
    h2 = (x1 * (1.0 + sc2) + sh2).astype(BF16)
    h2_o[...] = h2
    logits = _dot_nt(wr_ref[...], h2)
    e = jnp.exp(logits - jnp.max(logits, axis=0, keepdims=True))
    aff = e / jnp.sum(e, axis=0, keepdims=True)
    for blk in range(aff_o.shape[0]):
        aff_o[blk] = aff[:, blk * tb:(blk + 1) * tb]


def _postmix(x2d, batch, seq, mod, o_f, o_b, hgx, o_mla, wts, alpha, n_experts, row0, per_batch):
    n, d = x2d.shape
    tm = TOK_BLOCK
    nblk = seq // tm
    hw = HG_HEADS * HG_DV
    tok = lambda b, i: (b * nblk + i, 0)
    weights = [wts[k] for k in POSTMIX_KEYS]
    return pl.pallas_call(
        functools.partial(_postmix_kernel, alpha=alpha, row0=row0, per_batch=per_batch),
        out_shape=[jax.ShapeDtypeStruct((n, d), F32), jax.ShapeDtypeStruct((n, d), BF16),
                   jax.ShapeDtypeStruct((n // tm, n_experts, tm), F32)],
        grid=(batch, nblk),
        in_specs=[pl.BlockSpec((tm, d), tok), _const_spec(mod.shape),
                  pl.BlockSpec((tm, hw), tok), pl.BlockSpec((tm, hw), tok),
                  pl.BlockSpec((tm, hw), lambda b, i: (b * nblk + i, 6)),
                  pl.BlockSpec((tm, o_mla.shape[1]), tok), _const_spec(wts["win"].shape)]
                 + [_const_spec(w.shape) for w in weights],
        out_specs=[pl.BlockSpec((tm, d), tok), pl.BlockSpec((tm, d), tok),
                   pl.BlockSpec((1, n_experts, tm), lambda b, i: (b * nblk + i, 0, 0))],
        compiler_params=_params("arbitrary", "arbitrary"),
        name="postmix",
    )(x2d, mod, o_f, o_b, hgx, o_mla, wts["win"], *weights)


def _mixer_kernel(x_ref, mod_ref, *refs, alpha, row0, seq):
    nw = 1 + N_INPROJ_WEIGHTS
    in_w, refs = refs[:nw], refs[nw:]
    pm_w, refs = refs[:N_POSTMIX_WEIGHTS - 1], refs[N_POSTMIX_WEIGHTS - 1:]
    pm_w = (in_w[1],) + tuple(pm_w)
    x1_o, h2_o, aff_o, ckv_o, kpe_o, sfin_o, q_s, k_s, v_s, hgx_s, om_s, of_s, ob_s = refs
    m = _mod_row(mod_ref, row0, False)
    _inproj_body(x_ref, m, *in_w, q_s, k_s, v_s, ckv_o, kpe_o, hgx_s, rope=False)
    hw = HG_HEADS * HG_DK
    for s in range(x_ref.shape[0] // seq):
        rows = slice(s * seq, (s + 1) * seq)
        _attn_body(q_s.at[rows], k_s.at[rows], v_s.at[rows], None, None, om_s.at[rows])
        lane = lambda j: hgx_s.at[rows, j * hw:(j + 1) * hw]
        _hgrn_body((lane(0), lane(1), lane(2), lane(5)), (lane(0), lane(3), lane(4), lane(5)), None,
                   of_s.at[rows], ob_s.at[rows], sfin_o.at[s:s + 1], None, True, True)
    _postmix_body(x_ref, m, of_s, ob_s, hgx_s.at[:, 6 * hw:7 * hw], om_s, *pm_w, x1_o, h2_o, aff_o,
                  alpha=alpha)


def _mixer(x2d, batch, seq, mod, gamma, wts, alpha, n_experts, row0):
    n, d = x2d.shape
    assert seq == TOK_BLOCK
    ns = MIXER_SEQS if batch % MIXER_SEQS == 0 else 1
    tm = ns * seq
    hp = MLA_HEADS * LANES
    hv = MLA_HEADS * MLA_V
    hw = HG_HEADS * HG_DK
    kvl = wts["kvn"].shape[1]
    weights = [wts[k] for k in INPROJ_KEYS + POSTMIX_KEYS]
    tok = lambda b: (b, 0)
    st_shape = (ns, 2, HG_HEADS, HG_DK, HG_DV)
    return pl.pallas_call(
        functools.partial(_mixer_kernel, alpha=alpha, row0=row0, seq=seq),
        out_shape=[jax.ShapeDtypeStruct((n, d), F32), jax.ShapeDtypeStruct((n, d), BF16),
                   jax.ShapeDtypeStruct((n // seq, n_experts, seq), F32),
                   jax.ShapeDtypeStruct((n, kvl), F32), jax.ShapeDtypeStruct((n, LANES), F32),
                   jax.ShapeDtypeStruct((batch,) + st_shape[1:], F32)],
        grid=(batch // ns,),
        in_specs=[pl.BlockSpec((tm, d), tok), _const_spec(mod.shape), _const_spec(gamma.shape)]
                 + [_const_spec(w.shape) for w in weights],
        out_specs=[pl.BlockSpec((tm, d), tok), pl.BlockSpec((tm, d), tok),
                   pl.BlockSpec((ns, n_experts, seq), lambda b: (b, 0, 0)),
                   pl.BlockSpec((tm, kvl), tok), pl.BlockSpec((tm, LANES), tok),
                   pl.BlockSpec(st_shape, lambda b: (b, 0, 0, 0, 0))],
        scratch_shapes=[pltpu.VMEM((tm, hp), BF16), pltpu.VMEM((tm, hp), BF16), pltpu.VMEM((tm, hv), BF16),
                        pltpu.VMEM((tm, 7 * hw), F32), pltpu.VMEM((tm, hv), BF16),
                        pltpu.VMEM((tm, hw), F32), pltpu.VMEM((tm, hw), F32)],
        compiler_params=_params("arbitrary"),
        name="mixer",
    )(x2d, mod, gamma, *weights)


def _route_kernel(aff_ref, rank_o, cnt_o, *, cap):
    nb, ne, tb = aff_ref.shape
    key = aff_ref[...]

    def count(mask):
        return jnp.sum(jnp.sum(jnp.where(mask, 1.0, 0.0), axis=0), axis=1, keepdims=True)

    def bit_step(it, bits):
        cand = bits | jnp.left_shift(jnp.int32(1), 30 - it)
        return jnp.where(count(key >= pltpu.bitcast(cand, F32)[None]) >= cap, cand, bits)

    bits = lax.fori_loop(0, 31, bit_step, jnp.zeros((ne, 1), jnp.int32))
    thr = pltpu.bitcast(bits, F32)
    need = cap - count(key > thr[None])
    before = (lax.broadcasted_iota(jnp.int32, (tb, tb), 0)
              < lax.broadcasted_iota(jnp.int32, (tb, tb), 1))
    before = jnp.where(before, 1.0, 0.0).astype(BF16)
    off_eq = jnp.zeros((ne, 1), F32)
    off_sel = jnp.zeros((ne, 1), F32)
    cnt_o[...] = jnp.zeros_like(cnt_o)
    for blk in range(nb):
        key_b = key[blk]
        eq = key_b == thr
        eq_b = jnp.where(eq, 1.0, 0.0)
        eq_rank = _dot(eq_b.astype(BF16), before) + off_eq
        sel = (key_b > thr) | (eq & (eq_rank < need))
        sel_b = jnp.where(sel, 1.0, 0.0)
        rank = _dot(sel_b.astype(BF16), before) + off_sel
        rank_o[blk] = jnp.where(sel, rank.astype(jnp.int32), UNSELECTED)
        cnt_o[:, blk:blk + 1] = off_sel.astype(jnp.int32)
        off_eq = off_eq + jnp.sum(eq_b, axis=1, keepdims=True)
        off_sel = off_sel + jnp.sum(sel_b, axis=1, keepdims=True)
    cnt_o[:, nb:nb + 1] = off_sel.astype(jnp.int32)


def _route(aff, cap):
    nb, ne, tb = aff.shape
    assert nb + 1 <= LANES
    rank, cnt = pl.pallas_call(
        functools.partial(_route_kernel, cap=cap),
        out_shape=[jax.ShapeDtypeStruct(aff.shape, jnp.int32), jax.ShapeDtypeStruct((ne, LANES), jnp.int32)],
        in_specs=[pl.BlockSpec(memory_space=pltpu.VMEM)],
        out_specs=[pl.BlockSpec(memory_space=pltpu.VMEM)] * 2,
        compiler_params=pltpu.CompilerParams(vmem_limit_bytes=VMEM_LIMIT),
        name="route",
    )(aff)
    return rank, cnt[:, :nb + 1]


def _window_hits(rk_ref, firsts, slot0, win):
    ne, tb = rk_ref.shape[1], rk_ref.shape[2]
    win_iota = lax.broadcasted_iota(jnp.int32, (win, tb), 0)
    return [(rk_ref[0, e:e + 1, :] + (slot0 - firsts[e])) == win_iota for e in range(ne)]


def _compact_kernel(first_ref, end_ref, rounds_ref, *refs, groups, slots):
    ng = len(groups)
    h2_refs, rk_refs, af_refs = refs[0:ng], refs[ng:2 * ng], refs[2 * ng:3 * ng]
    xe_hbm, stage, tail, sem, issued = refs[3 * ng:]
    b = pl.program_id(0)
    ne = rk_refs[0].shape[1]
    win = WIN_ROWS
    d = h2_refs[0].shape[1]
    sub = SUBLANES

    def copies(slot, dsts):
        return [pltpu.make_async_copy(stage.at[slot, pl.ds(e * win, win), :],
                                      xe_hbm.at[e, pl.ds(pl.multiple_of(dsts[e], sub), win), :], sem.at[e])
                for e in range(ne)]

    def wait_previous():
        @pl.when(issued[0] > 0)
        def _():
            for cp in copies(0, [0] * ne):
                cp.wait()

    @pl.when(b == 0)
    def _init():
        issued[0] = 0
        tail[...] = jnp.zeros_like(tail)
        stage[1] = jnp.zeros(stage.shape[1:], stage.dtype)
        pad = copies(1, [slots] * ne)
        for cp in pad:
            cp.start()
        for cp in pad:
            cp.wait()

    def group_body(h2_ref, rk_ref, af_ref, slot0):
        firsts = [first_ref[b * ne + e] for e in range(ne)]
        bases = [(f // sub) * sub for f in firsts]
        ends = [end_ref[b * ne + e] - bases[e] for e in range(ne)]
        sub_iota = lax.broadcasted_iota(jnp.int32, (sub, stage.shape[2]), 0)

        def one_round(r, carry):
            dsts = [bases[e] + r * win for e in range(ne)]
            hits = _window_hits(rk_ref, dsts, slot0, win)
            onehot = jnp.where(jnp.concatenate(hits, axis=0), 1.0, 0.0).astype(BF16)
            rows = _dot(onehot, h2_ref[...])
            gate = jnp.concatenate(
                [jnp.sum(jnp.where(hits[e], af_ref[0, e:e + 1, :], 0.0), axis=1, keepdims=True)
                 for e in range(ne)], axis=0)
            slot = issued[0] % 2
            stage[slot, :, 0:d] = rows
            stage[slot, :, d:] = jnp.broadcast_to(gate, (ne * win, LANES))
            for e in range(ne):
                @pl.when(r == 0)
                def _head(e=e):
                    head = stage[slot, e * win:e * win + sub, :]
                    stage[slot, e * win:e * win + sub, :] = jnp.where(
                        sub_iota < firsts[e] - bases[e], tail[e * sub:(e + 1) * sub, :], head)

                last = (ends[e] // sub) * sub
                @pl.when(r == last // win)
                def _tail(e=e, last=last):
                    tail[e * sub:(e + 1) * sub, :] = stage[
                        slot, pl.ds(pl.multiple_of(e * win + last % win, sub), sub), :]
            wait_previous()
            for cp in copies(slot, [jnp.minimum(dst, slots) for dst in dsts]):
                cp.start()
            issued[0] = issued[0] + 1
            return carry

        lax.fori_loop(0, rounds_ref[b], one_round, 0)

    blk0 = 0
    for gi, g in enumerate(groups):
        @pl.when((b >= blk0) & (b < blk0 + g["nb"]))
        def _(gi=gi, g=g):
            group_body(h2_refs[gi], rk_refs[gi], af_refs[gi], g["slot0"])
        blk0 += g["nb"]

    @pl.when(b == pl.num_programs(0) - 1)
    def _drain():
        wait_previous()


def _compact(groups, first, end, rounds, slots):
    d = groups[0]["h2"].shape[1]
    nbs = [g["rank"].shape[0] for g in groups]
    ne, tb = groups[0]["rank"].shape[1:]
    meta, specs_h2, specs_rk = [], [], []
    blk0 = 0
    for g, nb in zip(groups, nbs):
        meta.append(dict(nb=nb, slot0=g["slot0"]))
        local = lambda b, *_, blk0=blk0, nb=nb: jnp.clip(b - blk0, 0, nb - 1)
        specs_h2.append(pl.BlockSpec((tb, d), lambda b, *_, local=local: (local(b), 0)))
        specs_rk.append(pl.BlockSpec((1, ne, tb), lambda b, *_, local=local: (local(b), 0, 0)))
        blk0 += nb
    width = d + LANES
    return pl.pallas_call(
        functools.partial(_compact_kernel, groups=meta, slots=slots),
        out_shape=jax.ShapeDtypeStruct((ne, slots + WIN_ROWS, width), F32),
        grid_spec=pltpu.PrefetchScalarGridSpec(
            num_scalar_prefetch=3,
            grid=(sum(nbs),),
            in_specs=specs_h2 + specs_rk + specs_rk,
            out_specs=pl.BlockSpec(memory_space=pl.ANY),
            scratch_shapes=[pltpu.VMEM((2, ne * WIN_ROWS, width), F32),
                            pltpu.VMEM((ne * SUBLANES, width), F32),
                            pltpu.SemaphoreType.DMA((ne,)), pltpu.SMEM((1,), jnp.int32)]),
        compiler_params=_params("arbitrary"),
        name="compact",
    )(first, end, rounds, *[g["h2"] for g in groups], *[g["rank"] for g in groups],
      *[g["aff"] for g in groups])


def _ffn_kernel(xe_ref, w1_ref, w3_ref, w2_ref, ye_ref, x_scr, g_scr, acc_scr):
    f = pl.program_id(1)
    d = x_scr.shape[1]

    @pl.when(f == 0)
    def _unpack():
        x_scr[...] = xe_ref[0, :, 0:d].astype(BF16)
        g_scr[...] = xe_ref[0, :, d:d + 1]

    x = x_scr[...]
    hid = _silu(_dot(x, w1_ref[0].astype(BF16))) * _dot(x, w3_ref[0].astype(BF16))
    y = _dot(hid.astype(BF16), w2_ref[0].astype(BF16))

    last = pl.num_programs(1) - 1

    @pl.when((f == 0) & (f < last))
    def _first():
        acc_scr[...] = y

    @pl.when((f > 0) & (f < last))
    def _middle():
        acc_scr[...] += y

    @pl.when((f == last) & (f > 0))
    def _last():
        ye_ref[0] = ((acc_scr[...] + y) * g_scr[...]).astype(ye_ref.dtype)

    @pl.when((f == last) & (f == 0))
    def _only():
        ye_ref[0] = (y * g_scr[...]).astype(ye_ref.dtype)


def _ffn(xe, w1, w3, w2, slots, ft):
    ne, d, dff = w1.shape
    nf = dff // ft
    return pl.pallas_call(
        _ffn_kernel,
        out_shape=jax.ShapeDtypeStruct((ne, slots, d), BF16),
        grid=(ne, nf),
        in_specs=[pl.BlockSpec((1, slots, xe.shape[2]), lambda e, f: (e, 0, 0)),
                  pl.BlockSpec((1, d, ft), lambda e, f: (e, 0, f)),
                  pl.BlockSpec((1, d, ft), lambda e, f: (e, 0, f)),
                  pl.BlockSpec((1, ft, d), lambda e, f: (e, f, 0))],
        out_specs=pl.BlockSpec((1, slots, d), lambda e, f: (e, 0, 0)),
        scratch_shapes=[pltpu.VMEM((slots, d), BF16), pltpu.VMEM((slots, 1), F32),
                        pltpu.VMEM((slots, d), F32)],
        compiler_params=_params("arbitrary", "arbitrary"),
        name="ffn",
    )(xe, w1, w3, w2)


def _combine_kernel(first_ref, rounds_ref, rk_ref, x1_ref, mod_ref, lng_ref, lnb_ref, ye_hbm, out_ref,
                    buf, acc_scr, sem, *, d_model, alpha, slot0, slots, row0, blocks_per_batch):
    d = d_model
    b = pl.program_id(0)
    nblk = pl.num_programs(0)
    ne, tb = rk_ref.shape[1], rk_ref.shape[2]
    win = WIN_ROWS
    eye = (lax.broadcasted_iota(jnp.int32, (tb, tb), 0)
           == lax.broadcasted_iota(jnp.int32, (tb, tb), 1))
    eye = jnp.where(eye, 1.0, 0.0).astype(BF16)

    def starts_of(blk, r):
        firsts = [(first_ref[blk * ne + e] // BF16_ROWS) * BF16_ROWS + r * win for e in range(ne)]
        return firsts, [jnp.minimum(f, slots - win) for f in firsts]

    def windows(slot, starts):
        return [pltpu.make_async_copy(ye_hbm.at[e, pl.ds(pl.multiple_of(starts[e], BF16_ROWS), win), :],
                                      buf.at[slot, pl.ds(e * win, win), :], sem.at[slot, e])
                for e in range(ne)]

    def scatter(slot, firsts, starts):
        hits = _window_hits(rk_ref, starts, slot0, win)
        hits = [h & ((rk_ref[0, e:e + 1, :] + slot0) >= firsts[e]) for e, h in enumerate(hits)]
        hit = jnp.where(jnp.concatenate(hits, axis=0), 1.0, 0.0).astype(BF16)
        hit_t = _dot_nt(eye, hit).astype(BF16)
        return _dot(hit_t, buf[slot])

    cur = b % 2

    @pl.when(b == 0)
    def _prime():
        for cp in windows(0, starts_of(0, 0)[1]):
            cp.start()

    @pl.when(b + 1 < nblk)
    def _prefetch():
        for cp in windows(1 - cur, starts_of(b + 1, 0)[1]):
            cp.start()

    firsts, starts = starts_of(b, 0)
    for cp in windows(cur, starts):
        cp.wait()
    acc_scr[...] = scatter(cur, firsts, starts)

    def extra_round(r, carry):
        firsts, starts = starts_of(b, r)
        for cp in windows(2, starts):
            cp.start()
        for cp in windows(2, starts):
            cp.wait()
        acc_scr[...] += scatter(2, firsts, starts)
        return carry

    lax.fori_loop(1, rounds_ref[b], extra_round, 0)
    r = row0 + b // blocks_per_batch
    g2 = mod_ref[pl.ds(r, 1), :][:, 5 * d:6 * d]
    out_ref[...] = _layer_norm(alpha * x1_ref[...] + g2 * acc_scr[...], lng_ref[...], lnb_ref[...])


def _combine(ye, rank, first, rounds, x1, mod, ln_g, ln_b, alpha, slot0, row0, blocks_per_batch):
    n, d = x1.shape
    nb, ne, tb = rank.shape
    slots = ye.shape[1]
    return pl.pallas_call(
        functools.partial(_combine_kernel, d_model=d, alpha=alpha, slot0=slot0, slots=slots, row0=row0,
                          blocks_per_batch=blocks_per_batch),
        out_shape=jax.ShapeDtypeStruct((n, d), F32),
        grid_spec=pltpu.PrefetchScalarGridSpec(
            num_scalar_prefetch=2,
            grid=(nb,),
            in_specs=[pl.BlockSpec((1, ne, tb), lambda b, *_: (b, 0, 0)),
                      pl.BlockSpec((tb, d), lambda b, *_: (b, 0)),
                      pl.BlockSpec(mod.shape, lambda b, *_: (0, 0)),
                      pl.BlockSpec((1, d), lambda b, *_: (0, 0)),
                      pl.BlockSpec((1, d), lambda b, *_: (0, 0)),
                      pl.BlockSpec(memory_space=pl.ANY)],
            out_specs=pl.BlockSpec((tb, d), lambda b, *_: (b, 0)),
            scratch_shapes=[pltpu.VMEM((3, ne * WIN_ROWS, d), ye.dtype), pltpu.VMEM((tb, d), F32),
                            pltpu.SemaphoreType.DMA((3, ne))]),
        compiler_params=_params("arbitrary"),
        name="combine",
    )(first, rounds, rank, x1, mod, ln_g, ln_b, ye)


def _prep_weights(w_in, q_norm, w_uq, kv_norm, w_ukv, w_o_mla, hgrn_norm, w_o_hg, w_out, ln1_g, ln1_b,
                  w_router):
    d = w_in.shape[0]
    q_lora, kv_lora = q_norm.shape[0], kv_norm.shape[0]
    hw = HG_HEADS * HG_DK
    hh, hp = MLA_HEADS, MLA_HEADS * LANES
    o_kv, o_pe = q_lora, q_lora + kv_lora
    o_h = o_pe + MLA_ROPE
    o_g = o_h + 5 * hw
    assert w_in.shape[1] == o_g + 2 * d
    qk = MLA_NOPE + MLA_ROPE
    kvw = MLA_NOPE + MLA_V
    b16 = lambda a: a.astype(BF16)
    assert all(o % BF16_ROWS == 0 for o in (o_kv, o_pe, o_h, o_g))
    win = b16(w_in.T)
    wuq = jnp.pad(w_uq.reshape(q_lora, hh, qk), ((0, 0), (0, 0), (0, LANES - qk))).reshape(q_lora, hp)
    ukv = w_ukv.reshape(kv_lora, hh, kvw)
    wk = jnp.pad(ukv[:, :, :MLA_NOPE], ((0, 0), (0, 0), (0, LANES - MLA_NOPE))).reshape(kv_lora, hp)
    wv = ukv[:, :, MLA_NOPE:].reshape(kv_lora, hh * MLA_V)
    return dict(
        win=win, qn=q_norm.reshape(1, -1), wuq=b16(wuq), kvn=kv_norm.reshape(1, -1), wk=b16(wk), wv=b16(wv),
        hgn=hgrn_norm.reshape(1, -1), womla=b16(w_o_mla), wohg=b16(w_o_hg), wout=b16(w_out),
        ln1g=ln1_g.reshape(1, -1), ln1b=ln1_b.reshape(1, -1), wr=b16(w_router.T))


def _rope_tables(seq):
    n_freq = MLA_ROPE // 4
    inv = ROPE_BASE ** (-np.arange(n_freq, dtype=np.float64) / n_freq)
    t = np.arange(seq)
    ang = np.concatenate([(t // GRID_W)[:, None] * inv, (t % GRID_W)[:, None] * inv], axis=-1)
    cos = np.repeat(np.cos(ang), 2, axis=1)
    sin = np.repeat(np.sin(ang), 2, axis=1) * np.tile([-1.0, 1.0], MLA_ROPE // 2)
    ck = np.pad(cos, ((0, 0), (0, LANES - MLA_ROPE)), constant_values=1.0)
    sk = np.pad(sin, ((0, 0), (0, LANES - MLA_ROPE)))
    cq = np.pad(cos, ((0, 0), (MLA_NOPE, LANES - MLA_NOPE - MLA_ROPE)), constant_values=1.0)
    sq = np.pad(sin, ((0, 0), (MLA_NOPE, LANES - MLA_NOPE - MLA_ROPE)))
    return tuple(jnp.asarray(a, F32) for a in (cq, sq, ck, sk))


def _window_sched(cnt, slot0):
    first = slot0 + cnt[:, :-1]
    end = slot0 + cnt[:, 1:]
    flat = lambda a: a.T.reshape(-1).astype(jnp.int32)

    def rounds(align):
        need = jnp.max((end - (first // align) * align + WIN_ROWS - 1) // WIN_ROWS, axis=0)
        return jnp.maximum(need, 1).astype(jnp.int32)

    return flat(first), flat(end), rounds(SUBLANES), rounds(BF16_ROWS)


def kernel(x_prompt, x_sample, c, cache_ckv, cache_kpe, state_hgrn, c_ctx, w_ada, b_ada, w_in, mla_q_norm, mla_w_uq, mla_kv_norm, mla_w_ukv, mla_w_o, hgrn_gamma, hgrn_norm, hgrn_w_o, w_out, ln1_g, ln1_b, moe_w_router, moe_w1, moe_w3, moe_w2, ln2_g, ln2_b):
    depth = w_ada.shape[0]
    assert depth == 1, "single trunk layer"
    bp, tp, d = x_prompt.shape
    bs, tsq, _ = x_sample.shape
    ne = moe_w_router.shape[-1]
    alpha = (2 * depth) ** 0.25
    past = cache_ckv.shape[2]
    assert tp % TOK_BLOCK == 0 and tsq % TOK_BLOCK == 0 and past % TOK_BLOCK == 0 and tsq % GRID_W == 0

    wts = _prep_weights(w_in[0], mla_q_norm[0], mla_w_uq[0], mla_kv_norm[0], mla_w_ukv[0], mla_w_o[0],
                        hgrn_norm[0], hgrn_w_o[0], w_out[0], ln1_g[0], ln1_b[0], moe_w_router[0])
    cond_rows = -(-(1 + bs) // SUBLANES) * SUBLANES
    cond = jnp.concatenate([c_ctx[None], c, jnp.zeros((cond_rows - 1 - bs, d), F32)], axis=0)
    mod = _adaln(cond, w_ada[0], b_ada[0])

    xs = [x_prompt.reshape(bp * tp, d), x_sample.reshape(bs * tsq, d)]
    dims = [(bp, tp), (bs, tsq)]
    rows = [(0, False), (1, True)]
    ropes = [None, _rope_tables(tsq)]
    kpe_c = jnp.pad(cache_kpe[:, 0].reshape(bs * past, MLA_ROPE), ((0, 0), (0, LANES - MLA_ROPE)))
    caches = [None, _kvup(cache_ckv[:, 0].reshape(bs * past, -1), kpe_c, wts)]
    inits = [None, state_hgrn[:, 0]]

    x1s, h2s, affs, extras = [], [], [], []
    for gi in range(2):
        (bt, sq), (row0, per_batch) = dims[gi], rows[gi]
        if sq == TOK_BLOCK and caches[gi] is None and ropes[gi] is None and not per_batch:
            x1, h2, aff, ckv, kpe, s_fin = _mixer(xs[gi], bt, sq, mod, hgrn_gamma, wts, alpha, ne, row0)
        else:
            q, k, v, ckv, kpe, hgx = _inproj(xs[gi], bt, sq, mod, hgrn_gamma, wts, row0, per_batch, ropes[gi])
            o_mla = _attn(q, k, v, bt, sq, caches[gi])
            o_f, o_b, s_fin = _hgrn(hgx, bt, sq, inits[gi])
            x1, h2, aff = _postmix(xs[gi], bt, sq, mod, o_f, o_b, hgx, o_mla, wts, alpha, ne, row0, per_batch)
        x1s.append(x1)
        h2s.append(h2)
        affs.append(aff)
        extras.append((ckv, kpe, s_fin))

    caps = [EC_FACTOR * x.shape[0] // ne for x in xs]
    slots = sum(caps)
    assert all(cp % BF16_ROWS == 0 for cp in caps) and slots >= WIN_ROWS
    groups, scheds = [], []
    slot0 = 0
    for gi in range(2):
        rank, cnt = _route(affs[gi], caps[gi])
        groups.append(dict(h2=h2s[gi], rank=rank, aff=affs[gi], slot0=slot0))
        scheds.append(_window_sched(cnt, slot0))
        slot0 += caps[gi]
    xe = _compact(groups, *[jnp.concatenate([s[k] for s in scheds]) for k in range(3)], slots)
    ye = _ffn(xe, moe_w1[0], moe_w3[0], moe_w2[0], slots, ft=512)

    outs = []
    for gi in range(2):
        outs.append(_combine(ye, groups[gi]["rank"], scheds[gi][0], scheds[gi][3], x1s[gi], mod,
                             ln2_g[0].reshape(1, -1), ln2_b[0].reshape(1, -1), alpha, groups[gi]["slot0"],
                             rows[gi][0], dims[gi][1] // TOK_BLOCK if rows[gi][1] else 1 << 30))

    ckv_p, kpe_p, st_p = extras[0]
    y_prompt = outs[0].reshape(bp, tp, d)
    y_sample = outs[1].reshape(bs, tsq, d)
    new_ckv = ckv_p.reshape(bp, 1, tp, -1)
    new_kpe = kpe_p[:, :MLA_ROPE].reshape(bp, 1, tp, MLA_ROPE)
    new_state = st_p.reshape(bp, 1, 2, HG_HEADS, HG_DK, HG_DV)
    return (y_prompt, y_sample, new_ckv, new_kpe, new_state)
```

```python
import functools

import jax
import jax.numpy as jnp
import numpy as np
from jax import lax
from jax.experimental import pallas as pl
from jax.experimental.pallas import tpu as pltpu

F32 = jnp.float32
BF16 = jnp.bfloat16

MLA_HEADS = 8
MLA_NOPE = 64
MLA_ROPE = 32
MLA_V = 64
HG_HEADS = 4
HG_DK = 128
HG_DV = 128
HG_CHUNK = 32
GRID_W = 64
ROPE_BASE = 10000.0
EC_FACTOR = 2
EPS = 1e-6

LANES = 128
SUBLANES = 8
BF16_ROWS = 16
VMEM_LIMIT = 56 * 1024 * 1024

TOK_BLOCK = 256
WIN_ROWS = 64
UNSELECTED = -(1 << 30)

NT_DIMS = (((1,), (1,)), ((), ()))


def _dot(a, b):
    return jnp.dot(a, b, preferred_element_type=F32)


def _dot_nt(a, b):
    return lax.dot_general(a, b, NT_DIMS, preferred_element_type=F32)


def _silu(x):
    return x * jax.nn.sigmoid(x)


def _params(*sem):
    return pltpu.CompilerParams(dimension_semantics=sem, vmem_limit_bytes=VMEM_LIMIT)


def _const_spec(shape):
    zeros = (0,) * len(shape)
    return pl.BlockSpec(shape, lambda *_: zeros, pipeline_mode=pl.Buffered(1))


def _adaln_kernel(c_ref, w_ref, b_ref, o_ref):
    s = _silu(c_ref[...]).astype(BF16)
    o_ref[...] = _dot(s, w_ref[...].astype(BF16)) + b_ref[...]


def _adaln(cond, w_ada, b_ada):
    rows, d = cond.shape
    n = w_ada.shape[1]
    tn = n // 4
    return pl.pallas_call(
        _adaln_kernel,
        out_shape=jax.ShapeDtypeStruct((rows, n), F32),
        grid=(n // tn,),
        in_specs=[_const_spec((rows, d)),
                  pl.BlockSpec((d, tn), lambda j: (0, j)),
                  pl.BlockSpec((1, tn), lambda j: (0, j))],
        out_specs=pl.BlockSpec((rows, tn), lambda j: (0, j)),
        compiler_params=_params("arbitrary"),
        name="adaln",
    )(cond, w_ada, b_ada.reshape(1, n))


def _rms(x, g):
    return x * lax.rsqrt(jnp.mean(x * x, axis=-1, keepdims=True) + EPS) * g


def _rope(x, c, s):
    w = x.shape[-1]
    lane = lax.broadcasted_iota(jnp.int32, x.shape, 1)
    nxt = pltpu.roll(x, w - 1, 1)
    prv = pltpu.roll(x, 1, 1)
    return x * c + jnp.where(lane % 2 == 0, nxt, prv) * s


N_INPROJ_WEIGHTS = 6


def _mod_row(mod_ref, row0, per_batch):
    r = row0 + pl.program_id(0) if per_batch else row0
    return mod_ref[pl.ds(r, 1), :]


def _modulated(x_ref, m):
    d = x_ref.shape[1]
    return (x_ref[...] * (1.0 + m[:, d:2 * d]) + m[:, 0:d]).astype(BF16)


def _keys(k_nope, kpe):
    shared = pltpu.roll(kpe, MLA_NOPE, 1)
    return (k_nope + jnp.concatenate([shared] * MLA_HEADS, axis=1)).astype(BF16)


def _inproj_kernel(*refs, row0, per_batch, rope):
    x_ref, mod_ref = refs[:2]
    _inproj_body(x_ref, _mod_row(mod_ref, row0, per_batch), *refs[2:], rope=rope)


def _inproj_body(x_ref, m, *refs, rope):
    gam_ref, win_ref, qn_ref, wuq_ref, kvn_ref, wk_ref, wv_ref = refs[:1 + N_INPROJ_WEIGHTS]
    refs = refs[1 + N_INPROJ_WEIGHTS:]
    if rope:
        cq_ref, sq_ref, ck_ref, sk_ref = refs[:4]
        refs = refs[4:]
    q_o, k_o, v_o, ckv_o, kpe_o, hgx_o = refs
    h = _modulated(x_ref, m)
    hw = HG_HEADS * HG_DK
    o_kv = qn_ref.shape[1]
    o_pe = o_kv + kvn_ref.shape[1]
    o_h = o_pe + MLA_ROPE

    cq = _rms(_dot_nt(h, win_ref[0:o_kv, :]), qn_ref[...])
    q = _dot(cq.astype(BF16), wuq_ref[...])
    if rope:
        q = _rope(q, jnp.concatenate([cq_ref[...]] * MLA_HEADS, axis=1),
                  jnp.concatenate([sq_ref[...]] * MLA_HEADS, axis=1))
    q_o[...] = q.astype(BF16)

    ckv = _rms(_dot_nt(h, win_ref[o_kv:o_pe, :]), kvn_ref[...])
    ckv_o[...] = ckv
    kpe = _dot_nt(h, win_ref[o_pe:o_h, :])
    kpe = jnp.concatenate([kpe, jnp.zeros((kpe.shape[0], LANES - MLA_ROPE), F32)], axis=1)
    if rope:
        kpe = _rope(kpe, ck_ref[...], sk_ref[...])
    kpe_o[...] = kpe
    cb = ckv.astype(BF16)
    k_o[...] = _keys(_dot(cb, wk_ref[...]), kpe)
    v_o[...] = _dot(cb, wv_ref[...]).astype(BF16)

    z = _dot_nt(h, win_ref[o_h:o_h + 5 * hw, :])
    hgx_o[:, 0:hw] = _silu(z[:, 0:hw])
    for dr in range(2):
        g0, g1 = gam_ref[dr, 0:1, :], gam_ref[dr, 1:2, :]
        gmax = jnp.maximum(g0, g1)
        e0, e1 = jnp.exp(g0 - gmax), jnp.exp(g1 - gmax)
        lb = e0 / (e0 + e1)
        f = lb + (1.0 - lb) * jax.nn.sigmoid(z[:, (1 + dr) * hw:(2 + dr) * hw])
        hgx_o[:, (1 + 2 * dr) * hw:(2 + 2 * dr) * hw] = jnp.log(f)
        hgx_o[:, (2 + 2 * dr) * hw:(3 + 2 * dr) * hw] = 1.0 - f
    hgx_o[:, 5 * hw:6 * hw] = z[:, 3 * hw:4 * hw]
    hgx_o[:, 6 * hw:7 * hw] = z[:, 4 * hw:5 * hw]


def _inproj(x2d, batch, seq, mod, gamma, wts, row0, per_batch, rope_tabs):
    n, d = x2d.shape
    tm = TOK_BLOCK
    nblk = seq // tm
    rope = rope_tabs is not None
    hp = MLA_HEADS * LANES
    hw = HG_HEADS * HG_DK
    tok = lambda b, i: (b * nblk + i, 0)
    pos = lambda b, i: (i, 0)
    weights = [wts[k] for k in INPROJ_KEYS]
    ins = [x2d, mod, gamma] + weights
    in_specs = ([pl.BlockSpec((tm, d), tok), _const_spec(mod.shape), _const_spec(gamma.shape)]
                + [_const_spec(w.shape) for w in weights])
    if rope:
        ins += list(rope_tabs)
        in_specs += [pl.BlockSpec((tm, t.shape[1]), pos) for t in rope_tabs]
    widths = [(hp, BF16), (hp, BF16), (MLA_HEADS * MLA_V, BF16), (wts["kvn"].shape[1], F32), (LANES, F32),
              (7 * hw, F32)]
    return pl.pallas_call(
        functools.partial(_inproj_kernel, row0=row0, per_batch=per_batch, rope=rope),
        out_shape=[jax.ShapeDtypeStruct((n, w), dt) for w, dt in widths],
        grid=(batch, nblk),
        in_specs=in_specs,
        out_specs=[pl.BlockSpec((tm, w), tok) for w, _ in widths],
        compiler_params=_params("arbitrary", "arbitrary"),
        name="inproj",
    )(*ins)


def _kvup_kernel(ckv_ref, kpe_ref, wk_ref, wv_ref, k_o, v_o):
    cb = ckv_ref[...].astype(BF16)
    k_o[...] = _keys(_dot(cb, wk_ref[...]), kpe_ref[...])
    v_o[...] = _dot(cb, wv_ref[...]).astype(BF16)


def _kvup(ckv2d, kpe2d, wts):
    n = ckv2d.shape[0]
    tm = TOK_BLOCK
    widths = [MLA_HEADS * LANES, MLA_HEADS * MLA_V]
    row = lambda i: (i, 0)
    ws = [wts["wk"], wts["wv"]]
    return pl.pallas_call(
        _kvup_kernel,
        out_shape=[jax.ShapeDtypeStruct((n, w), BF16) for w in widths],
        grid=(n // tm,),
        in_specs=[pl.BlockSpec((tm, ckv2d.shape[1]), row), pl.BlockSpec((tm, LANES), row)]
                 + [_const_spec(w.shape) for w in ws],
        out_specs=[pl.BlockSpec((tm, w), row) for w in widths],
        compiler_params=_params("arbitrary"),
        name="kvup",
    )(ckv2d, kpe2d, *ws)


ATTN_SCALE = (MLA_NOPE + MLA_ROPE) ** -0.5


def _attn_kernel(*refs, cached):
    if cached:
        q_ref, k_ref, v_ref, kc_ref, vc_ref, o_ref = refs
    else:
        q_ref, k_ref, v_ref, o_ref = refs
        kc_ref = vc_ref = None
    _attn_body(q_ref, k_ref, v_ref, kc_ref, vc_ref, o_ref)


def _attn_body(q_ref, k_ref, v_ref, kc_ref, vc_ref, o_ref):
    cached = kc_ref is not None
    scale = ATTN_SCALE
    per_slab = LANES // MLA_V
    own = lax.broadcasted_iota(jnp.int32, (q_ref.shape[0], LANES), 1) // MLA_V
    for slab in range(MLA_HEADS // per_slab):
        vsl = slice(slab * LANES, (slab + 1) * LANES)
        out = None
        for sub in range(per_slab):
            hd = slab * per_slab + sub
            sl = slice(hd * LANES, (hd + 1) * LANES)
            q = q_ref[:, sl]
            s = _dot_nt(q, k_ref[:, sl]) * scale
            mx = jnp.max(s, axis=-1, keepdims=True)
            if cached:
                s2 = _dot_nt(q, kc_ref[:, sl]) * scale
                mx = jnp.maximum(mx, jnp.max(s2, axis=-1, keepdims=True))
            e = jnp.exp(s - mx)
            den = jnp.sum(e, axis=-1, keepdims=True)
            o = _dot(e.astype(BF16), v_ref[:, vsl])
            if cached:
                e2 = jnp.exp(s2 - mx)
                den = den + jnp.sum(e2, axis=-1, keepdims=True)
                o = o + _dot(e2.astype(BF16), vc_ref[:, vsl])
            o = o / den
            out = o if out is None else jnp.where(own == sub, o, out)
        o_ref[:, vsl] = out.astype(o_ref.dtype)


def _attn(q, k, v, batch, seq, cache=None):
    n, hp = q.shape
    hv = v.shape[1]
    tq = TOK_BLOCK
    nblk = seq // tq
    ins = [q, k, v]
    in_specs = [pl.BlockSpec((tq, hp), lambda b, i: (b * nblk + i, 0)),
                pl.BlockSpec((seq, hp), lambda b, i: (b, 0)),
                pl.BlockSpec((seq, hv), lambda b, i: (b, 0))]
    if cache is not None:
        past = cache[0].shape[0] // batch
        ins += list(cache)
        in_specs += [pl.BlockSpec((past, hp), lambda b, i: (b, 0)),
                     pl.BlockSpec((past, hv), lambda b, i: (b, 0))]
    return pl.pallas_call(
        functools.partial(_attn_kernel, cached=cache is not None),
        out_shape=jax.ShapeDtypeStruct((n, hv), BF16),
        grid=(batch, nblk),
        in_specs=in_specs,
        out_specs=pl.BlockSpec((tq, hv), lambda b, i: (b * nblk + i, 0)),
        compiler_params=_params("arbitrary", "arbitrary"),
        name="attn",
    )(*ins)


def _chunk_scan(x, reverse):
    tm = x.shape[0]
    rin = lax.broadcasted_iota(jnp.int32, x.shape, 0) % HG_CHUNK
    step = 1
    while step < HG_CHUNK:
        if reverse:
            x = x + jnp.where(rin < HG_CHUNK - step, pltpu.roll(x, tm - step, 0), 0.0)
        else:
            x = x + jnp.where(rin >= step, pltpu.roll(x, step, 0), 0.0)
        step *= 2
    return x


def _hgrn_kernel(*refs, has_init):
    fwd, bwd = refs[0:4], refs[4:8]
    refs = refs[8:]
    s0_ref = None
    if has_init:
        s0_ref = refs[0]
        refs = refs[1:]
    of_ref, ob_ref, sfin_ref, st_scr = refs
    i = pl.program_id(1)
    _hgrn_body(fwd, bwd, s0_ref, of_ref, ob_ref, sfin_ref, st_scr, i == 0, i == pl.num_programs(1) - 1)


def _hgrn_body(fwd, bwd, s0_ref, of_ref, ob_ref, sfin_ref, st_scr, first, last):
    tm = fwd[0].shape[0]
    c = HG_CHUNK
    nch = tm // c
    dk, dv = HG_DK, HG_DV
    hw = HG_HEADS * dk

    def initial(dr, hd):
        return s0_ref[0, dr, hd].T if s0_ref is not None else jnp.zeros((dv, dk), F32)

    if st_scr is not None:
        @pl.when(first)
        def _init():
            for dr in range(2):
                for hd in range(HG_HEADS):
                    st_scr[dr, hd] = initial(dr, hd)

    npair = nch // 2
    pair = 2 * c
    row = lax.broadcasted_iota(jnp.int32, (tm, tm), 0)
    col = lax.broadcasted_iota(jnp.int32, (tm, tm), 1)
    same = (row // c) == (col // c)
    same_pair = (row // pair) == (col // pair)
    bd = (lax.broadcasted_iota(jnp.int32, (tm, npair * dk), 0) // pair
          == lax.broadcasted_iota(jnp.int32, (tm, npair * dk), 1) // dk)
    chunk_odd = (lax.broadcasted_iota(jnp.int32, (tm, hw), 0) // c) % 2 == 1

    for dr, (hq_ref, lf_ref, kk_ref, vv_ref) in enumerate((fwd, bwd)):
        o_ref = of_ref if dr == 0 else ob_ref
        tri = same & ((col <= row) if dr == 0 else (col >= row))
        cross = same_pair & (((row // c) > (col // c)) if dr == 0 else ((row // c) < (col // c)))
        second = chunk_odd if dr == 0 else ~chunk_odd
        bcum = _chunk_scan(lf_ref[...], reverse=dr == 1)
        closing = c - 1 if dr == 0 else 0
        btot3 = bcum.reshape(nch, c, hw)[:, closing:closing + 1, :]
        btot = jnp.broadcast_to(btot3, (nch, c, hw)).reshape(tm, hw)
        bpart = jnp.where(chunk_odd, pltpu.roll(btot, c, 0), pltpu.roll(btot, tm - c, 0))
        epart = jnp.exp(bpart)
        bpair = btot + bpart
        kk = kk_ref[...]
        qd = hq_ref[...] * jnp.exp(bcum)
        kd = kk * jnp.exp(-bcum)
        ke = kk * jnp.exp(btot - bcum)
        qd2 = jnp.where(second, qd * epart, qd)
        ke2 = jnp.where(second, ke, ke * epart)
        vv = vv_ref[...]
        order = range(npair) if dr == 0 else range(npair - 1, -1, -1)
        for hd in range(HG_HEADS):
            sl = slice(hd * dk, (hd + 1) * dk)
            qd_h = qd[:, sl].astype(BF16)
            v_h = vv[:, hd * dv:(hd + 1) * dv]
            a = jnp.where(tri, _dot_nt(qd_h, kd[:, sl].astype(BF16)),
                          jnp.where(cross, _dot_nt(qd_h, ke[:, sl].astype(BF16)), 0.0))
            o_intra = _dot(a.astype(BF16), v_h.astype(BF16))
            kebd = jnp.where(bd, jnp.concatenate([ke2[:, sl]] * npair, axis=1), 0.0).astype(BF16)
            qbd = jnp.where(bd, jnp.concatenate([qd2[:, sl]] * npair, axis=1), 0.0).astype(BF16)
            ut = _dot(v_h.T.astype(BF16), kebd)
            st = st_scr[dr, hd] if st_scr is not None else initial(dr, hd)
            prev = [None] * npair
            for p in order:
                prev[p] = st
                st = st * jnp.exp(bpair[p * pair:p * pair + 1, sl]) + ut[:, p * dk:(p + 1) * dk]
            if st_scr is not None:
                st_scr[dr, hd] = st
            o_inter = _dot_nt(qbd, jnp.concatenate(prev, axis=1).astype(BF16))
            o_ref[:, hd * dv:(hd + 1) * dv] = o_intra + o_inter

            if last is True:
                sfin_ref[0, dr, hd] = st.T
            else:
                @pl.when(last)
                def _final(st=st, dr=dr, hd=hd):
                    sfin_ref[0, dr, hd] = st.T


def _hgrn(hgx, batch, seq, s0=None):
    n = hgx.shape[0]
    tm = TOK_BLOCK
    nblk = seq // tm
    hw = HG_HEADS * HG_DK

    def spec(lane_blk, rev):
        if rev:
            return pl.BlockSpec((tm, hw), lambda b, i: (b * nblk + nblk - 1 - i, lane_blk))
        return pl.BlockSpec((tm, hw), lambda b, i: (b * nblk + i, lane_blk))

    in_specs = [spec(0, False), spec(1, False), spec(2, False), spec(5, False),
                spec(0, True), spec(3, True), spec(4, True), spec(5, True)]
    ins = [hgx] * 8
    st_shape = (1, 2, HG_HEADS, HG_DK, HG_DV)
    st_spec = pl.BlockSpec(st_shape, lambda b, i: (b, 0, 0, 0, 0))
    if s0 is not None:
        ins.append(s0)
        in_specs.append(st_spec)
    return pl.pallas_call(
        functools.partial(_hgrn_kernel, has_init=s0 is not None),
        out_shape=[jax.ShapeDtypeStruct((n, hw), F32), jax.ShapeDtypeStruct((n, hw), F32),
                   jax.ShapeDtypeStruct((batch,) + st_shape[1:], F32)],
        grid=(batch, nblk),
        in_specs=in_specs,
        out_specs=[spec(0, False), spec(0, True), st_spec],
        scratch_shapes=[pltpu.VMEM((2, HG_HEADS, HG_DV, HG_DK), F32)],
        compiler_params=_params("arbitrary", "arbitrary"),
        name="hgrn",
    )(*ins)


def _layer_norm(x, g, b):
    xc = x - jnp.mean(x, axis=-1, keepdims=True)
    var = jnp.mean(xc * xc, axis=-1, keepdims=True)
    return xc * lax.rsqrt(var + EPS) * g + b


N_POSTMIX_WEIGHTS = 8
INPROJ_KEYS = ("win", "qn", "wuq", "kvn", "wk", "wv")
POSTMIX_KEYS = ("hgn", "womla", "wohg", "wout", "ln1g", "ln1b", "wr")
MIXER_SEQS = 2


def _postmix_kernel(x_ref, mod_ref, *refs, alpha, row0, per_batch):
    _postmix_body(x_ref, _mod_row(mod_ref, row0, per_batch), *refs, alpha=alpha)


def _postmix_body(x_ref, m, of_ref, ob_ref, zg_ref, om_ref, wg_ref, hgn_ref, womla_ref,
                  wohg_ref, wout_ref, lng_ref, lnb_ref, wr_ref, x1_o, h2_o, aff_o, *, alpha):
    d = x_ref.shape[1]
    tb = aff_o.shape[2]
    g1, sh2, sc2 = m[:, 2 * d:3 * d], m[:, 3 * d:4 * d], m[:, 4 * d:5 * d]
    o = of_ref[...] + ob_ref[...]
    zg = zg_ref[...]
    parts = []
    for hd in range(HG_HEADS):
        sl = slice(hd * HG_DV, (hd + 1) * HG_DV)
        parts.append(_rms(o[:, sl], hgn_ref[...]) * _silu(zg[:, sl]))
    ohg = jnp.concatenate(parts, axis=1).astype(BF16)
    gates = _dot_nt(_modulated(x_ref, m), wg_ref[wg_ref.shape[0] - 2 * d:, :])
    merged = (jax.nn.sigmoid(gates[:, 0:d]) * _dot(om_ref[...], womla_ref[...])
              + jax.nn.sigmoid(gates[:, d:2 * d]) * _dot(ohg, wohg_ref[...]))
    mix = _dot(merged.astype(BF16), wout_ref[...])
    x1 = _layer_norm(alpha * x_ref[...] + g1 * mix, lng_ref[...], lnb_ref[...])
    x1_o[...] = x1
    h2 = (x1 * (1.0 + sc2) + sh2).astype(BF16)
    h2_o[...] = h2
    logits = _dot_nt(wr_ref[...], h2)
    e = jnp.exp(logits - jnp.max(logits, axis=0, keepdims=True))
    aff = e / jnp.sum(e, axis=0, keepdims=True)
    for blk in range(aff_o.shape[0]):
        aff_o[blk] = aff[:, blk * tb:(blk + 1) * tb]


def _postmix(x2d, batch, seq, mod, o_f, o_b, hgx, o_mla, wts, alpha, n_experts, row0, per_batch):
    n, d = x2d.shape
    tm = TOK_BLOCK
    nblk = seq // tm
    hw = HG_HEADS * HG_DV
    tok = lambda b, i: (b * nblk + i, 0)
    weights = [wts[k] for k in POSTMIX_KEYS]
    return pl.pallas_call(
        functools.partial(_postmix_kernel, alpha=alpha, row0=row0, per_batch=per_batch),
        out_shape=[jax.ShapeDtypeStruct((n, d), F32), jax.ShapeDtypeStruct((n, d), BF16),
                   jax.ShapeDtypeStruct((n // tm, n_experts, tm), F32)],
        grid=(batch, nblk),
        in_specs=[pl.BlockSpec((tm, d), tok), _const_spec(mod.shape),
                  pl.BlockSpec((tm, hw), tok), pl.BlockSpec((tm, hw), tok),
                  pl.BlockSpec((tm, hw), lambda b, i: (b * nblk + i, 6)),
                  pl.BlockSpec((tm, o_mla.shape[1]), tok), _const_spec(wts["win"].shape)]
                 + [_const_spec(w.shape) for w in weights],
        out_specs=[pl.BlockSpec((tm, d), tok), pl.BlockSpec((tm, d), tok),
                   pl.BlockSpec((1, n_experts, tm), lambda b, i: (b * nblk + i, 0, 0))],
        compiler_params=_params("arbitrary", "arbitrary"),
        name="postmix",
    )(x2d, mod, o_f, o_b, hgx, o_mla, wts["win"], *weights)


def _mixer_kernel(x_ref, mod_ref, *refs, alpha, row0, seq):
    nw = 1 + N_INPROJ_WEIGHTS
    in_w, refs = refs[:nw], refs[nw:]
    pm_w, refs = refs[:N_POSTMIX_WEIGHTS - 1], refs[N_POSTMIX_WEIGHTS - 1:]
    pm_w = (in_w[1],) + tuple(pm_w)
    x1_o, h2_o, aff_o, ckv_o, kpe_o, sfin_o, q_s, k_s, v_s, hgx_s, om_s, of_s, ob_s = refs
    m = _mod_row(mod_ref, row0, False)
    _inproj_body(x_ref, m, *in_w, q_s, k_s, v_s, ckv_o, kpe_o, hgx_s, rope=False)
    hw = HG_HEADS * HG_DK
    for s in range(x_ref.shape[0] // seq):
        rows = slice(s * seq, (s + 1) * seq)
        _attn_body(q_s.at[rows], k_s.at[rows], v_s.at[rows], None, None, om_s.at[rows])
        lane = lambda j: hgx_s.at[rows, j * hw:(j + 1) * hw]
        _hgrn_body((lane(0), lane(1), lane(2), lane(5)), (lane(0), lane(3), lane(4), lane(5)), None,
                   of_s.at[rows], ob_s.at[rows], sfin_o.at[s:s + 1], None, True, True)
    _postmix_body(x_ref, m, of_s, ob_s, hgx_s.at[:, 6 * hw:7 * hw], om_s, *pm_w, x1_o, h2_o, aff_o,
                  alpha=alpha)


def _mixer(x2d, batch, seq, mod, gamma, wts, alpha, n_experts, row0):
    n, d = x2d.shape
    assert seq == TOK_BLOCK
    ns = MIXER_SEQS if batch % MIXER_SEQS == 0 else 1
    tm = ns * seq
    hp = MLA_HEADS * LANES
    hv = MLA_HEADS * MLA_V
    hw = HG_HEADS * HG_DK
    kvl = wts["kvn"].shape[1]
    weights = [wts[k] for k in INPROJ_KEYS + POSTMIX_KEYS]
    tok = lambda b: (b, 0)
    st_shape = (ns, 2, HG_HEADS, HG_DK, HG_DV)
    return pl.pallas_call(
        functools.partial(_mixer_kernel, alpha=alpha, row0=row0, seq=seq),
        out_shape=[jax.ShapeDtypeStruct((n, d), F32), jax.ShapeDtypeStruct((n, d), BF16),
                   jax.ShapeDtypeStruct((n // seq, n_experts, seq), F32),
                   jax.ShapeDtypeStruct((n, kvl), F32), jax.ShapeDtypeStruct((n, LANES), F32),
                   jax.ShapeDtypeStruct((batch,) + st_shape[1:], F32)],
        grid=(batch // ns,),
        in_specs=[pl.BlockSpec((tm, d), tok), _const_spec(mod.shape), _const_spec(gamma.shape)]
                 + [_const_spec(w.shape) for w in weights],
        out_specs=[pl.BlockSpec((tm, d), tok), pl.BlockSpec((tm, d), tok),
                   pl.BlockSpec((ns, n_experts, seq), lambda b: (b, 0, 0)),
                   pl.BlockSpec((tm, kvl), tok), pl.BlockSpec((tm, LANES), tok),
                   pl.BlockSpec(st_shape, lambda b: (b, 0, 0, 0, 0))],
        scratch_shapes=[pltpu.VMEM((tm, hp), BF16), pltpu.VMEM((tm, hp), BF16), pltpu.VMEM((tm, hv), BF16),
                        pltpu.VMEM((tm, 7 * hw), F32), pltpu.VMEM((tm, hv), BF16),
                        pltpu.VMEM((tm, hw), F32), pltpu.VMEM((tm, hw), F32)],
        compiler_params=_params("arbitrary"),
        name="mixer",
    )(x2d, mod, gamma, *weights)


def _route_kernel(aff_ref, rank_o, cnt_o, *, cap):
    nb, ne, tb = aff_ref.shape
    key = aff_ref[...]

    def count(mask):
        return jnp.sum(jnp.sum(jnp.where(mask, 1.0, 0.0), axis=0), axis=1, keepdims=True)

    def bit_step(it, bits):
        cand = bits | jnp.left_shift(jnp.int32(1), 30 - it)
        return jnp.where(count(key >= pltpu.bitcast(cand, F32)[None]) >= cap, cand, bits)

    bits = lax.fori_loop(0, 31, bit_step, jnp.zeros((ne, 1), jnp.int32))
    thr = pltpu.bitcast(bits, F32)
    need = cap - count(key > thr[None])
    before = (lax.broadcasted_iota(jnp.int32, (tb, tb), 0)
              < lax.broadcasted_iota(jnp.int32, (tb, tb), 1))
    before = jnp.where(before, 1.0, 0.0).astype(BF16)
    off_eq = jnp.zeros((ne, 1), F32)
    off_sel = jnp.zeros((ne, 1), F32)
    cnt_o[...] = jnp.zeros_like(cnt_o)
    for blk in range(nb):
        key_b = key[blk]
        eq = key_b == thr
        eq_b = jnp.where(eq, 1.0, 0.0)
        eq_rank = _dot(eq_b.astype(BF16), before) + off_eq
        sel = (key_b > thr) | (eq & (eq_rank < need))
        sel_b = jnp.where(sel, 1.0, 0.0)
        rank = _dot(sel_b.astype(BF16), before) + off_sel
        rank_o[blk] = jnp.where(sel, rank.astype(jnp.int32), UNSELECTED)
        cnt_o[:, blk:blk + 1] = off_sel.astype(jnp.int32)
        off_eq = off_eq + jnp.sum(eq_b, axis=1, keepdims=True)
        off_sel = off_sel + jnp.sum(sel_b, axis=1, keepdims=True)
    cnt_o[:, nb:nb + 1] = off_sel.astype(jnp.int32)


def _route(aff, cap):
    nb, ne, tb = aff.shape
    assert nb + 1 <= LANES
    rank, cnt = pl.pallas_call(
        functools.partial(_route_kernel, cap=cap),
        out_shape=[jax.ShapeDtypeStruct(aff.shape, jnp.int32), jax.ShapeDtypeStruct((ne, LANES), jnp.int32)],
        in_specs=[pl.BlockSpec(memory_space=pltpu.VMEM)],
        out_specs=[pl.BlockSpec(memory_space=pltpu.VMEM)] * 2,
        compiler_params=pltpu.CompilerParams(vmem_limit_bytes=VMEM_LIMIT),
        name="route",
    )(aff)
    return rank, cnt[:, :nb + 1]


def _window_hits(rk_ref, firsts, slot0, win):
    ne, tb = rk_ref.shape[1], rk_ref.shape[2]
    win_iota = lax.broadcasted_iota(jnp.int32, (win, tb), 0)
    return [(rk_ref[0, e:e + 1, :] + (slot0 - firsts[e])) == win_iota for e in range(ne)]


def _compact_kernel(first_ref, end_ref, rounds_ref, *refs, groups, slots):
    ng = len(groups)
    h2_refs, rk_refs, af_refs = refs[0:ng], refs[ng:2 * ng], refs[2 * ng:3 * ng]
    hbms, refs = refs[3 * ng:3 * ng + 2], refs[3 * ng + 2:]
    stages, tails = refs[0:2], refs[2:4]
    sem, issued = refs[4:]
    b = pl.program_id(0)
    ne = rk_refs[0].shape[1]
    win = WIN_ROWS
    sub = BF16_ROWS

    def copies(slot, dsts):
        return [pltpu.make_async_copy(stage.at[slot, pl.ds(e * win, win), :],
                                      hbm.at[e, pl.ds(pl.multiple_of(dsts[e], sub), win), :], sem.at[c, e])
                for c, (stage, hbm) in enumerate(zip(stages, hbms)) for e in range(ne)]

    def wait_previous():
        @pl.when(issued[0] > 0)
        def _():
            for cp in copies(0, [0] * ne):
                cp.wait()

    @pl.when(b == 0)
    def _init():
        issued[0] = 0
        for stage, tail in zip(stages, tails):
            tail[...] = jnp.zeros_like(tail)
            stage[1] = jnp.zeros(stage.shape[1:], stage.dtype)
        pad = copies(1, [slots] * ne)
        for cp in pad:
            cp.start()
        for cp in pad:
            cp.wait()

    def group_body(h2_ref, rk_ref, af_ref, slot0):
        firsts = [first_ref[b * ne + e] for e in range(ne)]
        bases = [(f // sub) * sub for f in firsts]
        ends = [end_ref[b * ne + e] - bases[e] for e in range(ne)]

        def one_round(r, carry):
            dsts = [bases[e] + r * win for e in range(ne)]
            hits = _window_hits(rk_ref, dsts, slot0, win)
            onehot = jnp.where(jnp.concatenate(hits, axis=0), 1.0, 0.0).astype(BF16)
            gate = jnp.concatenate(
                [jnp.sum(jnp.where(hits[e], af_ref[0, e:e + 1, :], 0.0), axis=1, keepdims=True)
                 for e in range(ne)], axis=0)
            slot = issued[0] % 2
            stages[0][slot] = _dot(onehot, h2_ref[...]).astype(BF16)
            stages[1][slot] = jnp.broadcast_to(gate, (ne * win, LANES))
            for stage, tail in zip(stages, tails):
                sub_iota = lax.broadcasted_iota(jnp.int32, (sub, stage.shape[2]), 0)
                for e in range(ne):
                    @pl.when(r == 0)
                    def _head(e=e, stage=stage, tail=tail, sub_iota=sub_iota):
                        head = stage[slot, e * win:e * win + sub, :]
                        stage[slot, e * win:e * win + sub, :] = jnp.where(
                            sub_iota < firsts[e] - bases[e], tail[e * sub:(e + 1) * sub, :], head)

                    last = (ends[e] // sub) * sub
                    @pl.when(r == last // win)
                    def _tail(e=e, last=last, stage=stage, tail=tail):
                        tail[e * sub:(e + 1) * sub, :] = stage[
                            slot, pl.ds(pl.multiple_of(e * win + last % win, sub), sub), :]
            wait_previous()
            for cp in copies(slot, [jnp.minimum(dst, slots) for dst in dsts]):
                cp.start()
            issued[0] = issued[0] + 1
            return carry

        lax.fori_loop(0, rounds_ref[b], one_round, 0)

    blk0 = 0
    for gi, g in enumerate(groups):
        @pl.when((b >= blk0) & (b < blk0 + g["nb"]))
        def _(gi=gi, g=g):
            group_body(h2_refs[gi], rk_refs[gi], af_refs[gi], g["slot0"])
        blk0 += g["nb"]

    @pl.when(b == pl.num_programs(0) - 1)
    def _drain():
        wait_previous()


def _compact(groups, first, end, rounds, slots):
    d = groups[0]["h2"].shape[1]
    nbs = [g["rank"].shape[0] for g in groups]
    ne, tb = groups[0]["rank"].shape[1:]
    meta, specs_h2, specs_rk = [], [], []
    blk0 = 0
    for g, nb in zip(groups, nbs):
        meta.append(dict(nb=nb, slot0=g["slot0"]))
        local = lambda b, *_, blk0=blk0, nb=nb: jnp.clip(b - blk0, 0, nb - 1)
        specs_h2.append(pl.BlockSpec((tb, d), lambda b, *_, local=local: (local(b), 0)))
        specs_rk.append(pl.BlockSpec((1, ne, tb), lambda b, *_, local=local: (local(b), 0, 0)))
        blk0 += nb
    streams = [(d, BF16), (LANES, F32)]
    return pl.pallas_call(
        functools.partial(_compact_kernel, groups=meta, slots=slots),
        out_shape=[jax.ShapeDtypeStruct((ne, slots + WIN_ROWS, w), dt) for w, dt in streams],
        grid_spec=pltpu.PrefetchScalarGridSpec(
            num_scalar_prefetch=3,
            grid=(sum(nbs),),
            in_specs=specs_h2 + specs_rk + specs_rk,
            out_specs=[pl.BlockSpec(memory_space=pl.ANY)] * 2,
            scratch_shapes=[pltpu.VMEM((2, ne * WIN_ROWS, w), dt) for w, dt in streams]
                           + [pltpu.VMEM((ne * BF16_ROWS, w), dt) for w, dt in streams]
                           + [pltpu.SemaphoreType.DMA((2, ne)), pltpu.SMEM((1,), jnp.int32)]),
        compiler_params=_params("arbitrary"),
        name="compact",
    )(first, end, rounds, *[g["h2"] for g in groups], *[g["rank"] for g in groups],
      *[g["aff"] for g in groups])


def _ffn_kernel(xe_ref, ge_ref, w1_ref, w3_ref, w2_ref, ye_ref, acc_scr):
    f = pl.program_id(1)
    x = xe_ref[0]
    hid = _silu(_dot(x, w1_ref[0].astype(BF16))) * _dot(x, w3_ref[0].astype(BF16))
    y = _dot(hid.astype(BF16), w2_ref[0].astype(BF16))

    last = pl.num_programs(1) - 1

    @pl.when((f == 0) & (f < last))
    def _first():
        acc_scr[...] = y

    @pl.when((f > 0) & (f < last))
    def _middle():
        acc_scr[...] += y

    @pl.when((f == last) & (f > 0))
    def _last():
        ye_ref[0] = ((acc_scr[...] + y) * ge_ref[0, :, 0:1]).astype(ye_ref.dtype)

    @pl.when((f == last) & (f == 0))
    def _only():
        ye_ref[0] = (y * ge_ref[0, :, 0:1]).astype(ye_ref.dtype)


def _ffn(xe, ge, w1, w3, w2, slots, ft):
    ne, d, dff = w1.shape
    nf = dff // ft
    return pl.pallas_call(
        _ffn_kernel,
        out_shape=jax.ShapeDtypeStruct((ne, slots, d), BF16),
        grid=(ne, nf),
        in_specs=[pl.BlockSpec((1, slots, d), lambda e, f: (e, 0, 0)),
                  pl.BlockSpec((1, slots, ge.shape[2]), lambda e, f: (e, 0, 0)),
                  pl.BlockSpec((1, d, ft), lambda e, f: (e, 0, f)),
                  pl.BlockSpec((1, d, ft), lambda e, f: (e, 0, f)),
                  pl.BlockSpec((1, ft, d), lambda e, f: (e, f, 0))],
        out_specs=pl.BlockSpec((1, slots, d), lambda e, f: (e, 0, 0)),
        scratch_shapes=[pltpu.VMEM((slots, d), F32)],
        compiler_params=_params("arbitrary", "arbitrary"),
        name="ffn",
    )(xe, ge, w1, w3, w2)


def _combine_kernel(first_ref, rounds_ref, rk_ref, x1_ref, mod_ref, lng_ref, lnb_ref, ye_hbm, out_ref,
                    buf, acc_scr, sem, *, d_model, alpha, slot0, slots, row0, blocks_per_batch):
    d = d_model
    b = pl.program_id(0)
    nblk = pl.num_programs(0)
    ne, tb = rk_ref.shape[1], rk_ref.shape[2]
    win = WIN_ROWS
    eye = (lax.broadcasted_iota(jnp.int32, (tb, tb), 0)
           == lax.broadcasted_iota(jnp.int32, (tb, tb), 1))
    eye = jnp.where(eye, 1.0, 0.0).astype(BF16)

    def starts_of(blk, r):
        firsts = [(first_ref[blk * ne + e] // BF16_ROWS) * BF16_ROWS + r * win for e in range(ne)]
        return firsts, [jnp.minimum(f, slots - win) for f in firsts]

    def windows(slot, starts):
        return [pltpu.make_async_copy(ye_hbm.at[e, pl.ds(pl.multiple_of(starts[e], BF16_ROWS), win), :],
                                      buf.at[slot, pl.ds(e * win, win), :], sem.at[slot, e])
                for e in range(ne)]

    def scatter(slot, firsts, starts):
        hits = _window_hits(rk_ref, starts, slot0, win)
        hits = [h & ((rk_ref[0, e:e + 1, :] + slot0) >= firsts[e]) for e, h in enumerate(hits)]
        hit = jnp.where(jnp.concatenate(hits, axis=0), 1.0, 0.0).astype(BF16)
        hit_t = _dot_nt(eye, hit).astype(BF16)
        return _dot(hit_t, buf[slot])

    cur = b % 2

    @pl.when(b == 0)
    def _prime():
        for cp in windows(0, starts_of(0, 0)[1]):
            cp.start()

    @pl.when(b + 1 < nblk)
    def _prefetch():
        for cp in windows(1 - cur, starts_of(b + 1, 0)[1]):
            cp.start()

    firsts, starts = starts_of(b, 0)
    for cp in windows(cur, starts):
        cp.wait()
    acc_scr[...] = scatter(cur, firsts, starts)

    def extra_round(r, carry):
        firsts, starts = starts_of(b, r)
        for cp in windows(2, starts):
            cp.start()
        for cp in windows(2, starts):
            cp.wait()
        acc_scr[...] += scatter(2, firsts, starts)
        return carry

    lax.fori_loop(1, rounds_ref[b], extra_round, 0)
    r = row0 + b // blocks_per_batch
    g2 = mod_ref[pl.ds(r, 1), :][:, 5 * d:6 * d]
    out_ref[...] = _layer_norm(alpha * x1_ref[...] + g2 * acc_scr[...], lng_ref[...], lnb_ref[...])


def _combine(ye, rank, first, rounds, x1, mod, ln_g, ln_b, alpha, slot0, row0, blocks_per_batch):
    n, d = x1.shape
    nb, ne, tb = rank.shape
    slots = ye.shape[1]
    return pl.pallas_call(
        functools.partial(_combine_kernel, d_model=d, alpha=alpha, slot0=slot0, slots=slots, row0=row0,
                          blocks_per_batch=blocks_per_batch),
        out_shape=jax.ShapeDtypeStruct((n, d), F32),
        grid_spec=pltpu.PrefetchScalarGridSpec(
            num_scalar_prefetch=2,
            grid=(nb,),
            in_specs=[pl.BlockSpec((1, ne, tb), lambda b, *_: (b, 0, 0)),
                      pl.BlockSpec((tb, d), lambda b, *_: (b, 0)),
                      pl.BlockSpec(mod.shape, lambda b, *_: (0, 0)),
                      pl.BlockSpec((1, d), lambda b, *_: (0, 0)),
                      pl.BlockSpec((1, d), lambda b, *_: (0, 0)),
                      pl.BlockSpec(memory_space=pl.ANY)],
            out_specs=pl.BlockSpec((tb, d), lambda b, *_: (b, 0)),
            scratch_shapes=[pltpu.VMEM((3, ne * WIN_ROWS, d), ye.dtype), pltpu.VMEM((tb, d), F32),
                            pltpu.SemaphoreType.DMA((3, ne))]),
        compiler_params=_params("arbitrary"),
        name="combine",
    )(first, rounds, rank, x1, mod, ln_g, ln_b, ye)


def _prep_weights(w_in, q_norm, w_uq, kv_norm, w_ukv, w_o_mla, hgrn_norm, w_o_hg, w_out, ln1_g, ln1_b,
                  w_router):
    d = w_in.shape[0]
    q_lora, kv_lora = q_norm.shape[0], kv_norm.shape[0]
    hw = HG_HEADS * HG_DK
    hh, hp = MLA_HEADS, MLA_HEADS * LANES
    o_kv, o_pe = q_lora, q_lora + kv_lora
    o_h = o_pe + MLA_ROPE
    o_g = o_h + 5 * hw
    assert w_in.shape[1] == o_g + 2 * d
    qk = MLA_NOPE + MLA_ROPE
    kvw = MLA_NOPE + MLA_V
    b16 = lambda a: a.astype(BF16)
    assert all(o % BF16_ROWS == 0 for o in (o_kv, o_pe, o_h, o_g))
    win = b16(w_in.T)
    wuq = jnp.pad(w_uq.reshape(q_lora, hh, qk), ((0, 0), (0, 0), (0, LANES - qk))).reshape(q_lora, hp)
    ukv = w_ukv.reshape(kv_lora, hh, kvw)
    wk = jnp.pad(ukv[:, :, :MLA_NOPE], ((0, 0), (0, 0), (0, LANES - MLA_NOPE))).reshape(kv_lora, hp)
    wv = ukv[:, :, MLA_NOPE:].reshape(kv_lora, hh * MLA_V)
    return dict(
        win=win, qn=q_norm.reshape(1, -1), wuq=b16(wuq), kvn=kv_norm.reshape(1, -1), wk=b16(wk), wv=b16(wv),
        hgn=hgrn_norm.reshape(1, -1), womla=b16(w_o_mla), wohg=b16(w_o_hg), wout=b16(w_out),
        ln1g=ln1_g.reshape(1, -1), ln1b=ln1_b.reshape(1, -1), wr=b16(w_router.T))


def _rope_tables(seq):
    n_freq = MLA_ROPE // 4
    inv = ROPE_BASE ** (-np.arange(n_freq, dtype=np.float64) / n_freq)
    t = np.arange(seq)
    ang = np.concatenate([(t // GRID_W)[:, None] * inv, (t % GRID_W)[:, None] * inv], axis=-1)
    cos = np.repeat(np.cos(ang), 2, axis=1)
    sin = np.repeat(np.sin(ang), 2, axis=1) * np.tile([-1.0, 1.0], MLA_ROPE // 2)
    ck = np.pad(cos, ((0, 0), (0, LANES - MLA_ROPE)), constant_values=1.0)
    sk = np.pad(sin, ((0, 0), (0, LANES - MLA_ROPE)))
    cq = np.pad(cos, ((0, 0), (MLA_NOPE, LANES - MLA_NOPE - MLA_ROPE)), constant_values=1.0)
    sq = np.pad(sin, ((0, 0), (MLA_NOPE, LANES - MLA_NOPE - MLA_ROPE)))
    return tuple(jnp.asarray(a, F32) for a in (cq, sq, ck, sk))


def _window_sched(cnt, slot0):
    first = slot0 + cnt[:, :-1]
    end = slot0 + cnt[:, 1:]
    flat = lambda a: a.T.reshape(-1).astype(jnp.int32)
    rounds = jnp.max((end - (first // BF16_ROWS) * BF16_ROWS + WIN_ROWS - 1) // WIN_ROWS, axis=0)
    return flat(first), flat(end), jnp.maximum(rounds, 1).astype(jnp.int32)


def kernel(x_prompt, x_sample, c, cache_ckv, cache_kpe, state_hgrn, c_ctx, w_ada, b_ada, w_in, mla_q_norm, mla_w_uq, mla_kv_norm, mla_w_ukv, mla_w_o, hgrn_gamma, hgrn_norm, hgrn_w_o, w_out, ln1_g, ln1_b, moe_w_router, moe_w1, moe_w3, moe_w2, ln2_g, ln2_b):
    depth = w_ada.shape[0]
    assert depth == 1, "single trunk layer"
    bp, tp, d = x_prompt.shape
    bs, tsq, _ = x_sample.shape
    ne = moe_w_router.shape[-1]
    alpha = (2 * depth) ** 0.25
    past = cache_ckv.shape[2]
    assert tp % TOK_BLOCK == 0 and tsq % TOK_BLOCK == 0 and past % TOK_BLOCK == 0 and tsq % GRID_W == 0

    wts = _prep_weights(w_in[0], mla_q_norm[0], mla_w_uq[0], mla_kv_norm[0], mla_w_ukv[0], mla_w_o[0],
                        hgrn_norm[0], hgrn_w_o[0], w_out[0], ln1_g[0], ln1_b[0], moe_w_router[0])
    cond_rows = -(-(1 + bs) // SUBLANES) * SUBLANES
    cond = jnp.concatenate([c_ctx[None], c, jnp.zeros((cond_rows - 1 - bs, d), F32)], axis=0)
    mod = _adaln(cond, w_ada[0], b_ada[0])

    xs = [x_prompt.reshape(bp * tp, d), x_sample.reshape(bs * tsq, d)]
    dims = [(bp, tp), (bs, tsq)]
    rows = [(0, False), (1, True)]
    ropes = [None, _rope_tables(tsq)]
    kpe_c = jnp.pad(cache_kpe[:, 0].reshape(bs * past, MLA_ROPE), ((0, 0), (0, LANES - MLA_ROPE)))
    caches = [None, _kvup(cache_ckv[:, 0].reshape(bs * past, -1), kpe_c, wts)]
    inits = [None, state_hgrn[:, 0]]

    x1s, h2s, affs, extras = [], [], [], []
    for gi in range(2):
        (bt, sq), (row0, per_batch) = dims[gi], rows[gi]
        if sq == TOK_BLOCK and caches[gi] is None and ropes[gi] is None and not per_batch:
            x1, h2, aff, ckv, kpe, s_fin = _mixer(xs[gi], bt, sq, mod, hgrn_gamma, wts, alpha, ne, row0)
        else:
            q, k, v, ckv, kpe, hgx = _inproj(xs[gi], bt, sq, mod, hgrn_gamma, wts, row0, per_batch, ropes[gi])
            o_mla = _attn(q, k, v, bt, sq, caches[gi])
            o_f, o_b, s_fin = _hgrn(hgx, bt, sq, inits[gi])
            x1, h2, aff = _postmix(xs[gi], bt, sq, mod, o_f, o_b, hgx, o_mla, wts, alpha, ne, row0, per_batch)
        x1s.append(x1)
        h2s.append(h2)
        affs.append(aff)
        extras.append((ckv, kpe, s_fin))

    caps = [EC_FACTOR * x.shape[0] // ne for x in xs]
    slots = sum(caps)
    assert all(cp % BF16_ROWS == 0 for cp in caps) and slots >= WIN_ROWS
    groups, scheds = [], []
    slot0 = 0
    for gi in range(2):
        rank, cnt = _route(affs[gi], caps[gi])
        groups.append(dict(h2=h2s[gi], rank=rank, aff=affs[gi], slot0=slot0))
        scheds.append(_window_sched(cnt, slot0))
        slot0 += caps[gi]
    xe, ge = _compact(groups, *[jnp.concatenate([s[k] for s in scheds]) for k in range(3)], slots)
    ye = _ffn(xe, ge, moe_w1[0], moe_w3[0], moe_w2[0], slots, ft=512)

    outs = []
    for gi in range(2):
        outs.append(_combine(ye, groups[gi]["rank"], scheds[gi][0], scheds[gi][2], x1s[gi], mod,
                             ln2_g[0].reshape(1, -1), ln2_b[0].reshape(1, -1), alpha, groups[gi]["slot0"],
                             rows[gi][0], dims[gi][1] // TOK_BLOCK if rows[gi][1] else 1 << 30))

    ckv_p, kpe_p, st_p = extras[0]
    y_prompt = outs[0].reshape(bp, tp, d)
    y_sample = outs[1].reshape(bs, tsq, d)
    new_ckv = ckv_p.reshape(bp, 1, tp, -1)
    new_kpe = kpe_p[:, :MLA_ROPE].reshape(bp, 1, tp, MLA_ROPE)
    new_state = st_p.reshape(bp, 1, 2, HG_HEADS, HG_DK, HG_DV)
    return (y_prompt, y_sample, new_ckv, new_kpe, new_state)
```

```python
import functools

import jax
import jax.numpy as jnp
import numpy as np
from jax import lax
from jax.experimental import pallas as pl
from jax.experimental.pallas import tpu as pltpu

F32 = jnp.float32
BF16 = jnp.bfloat16

MLA_HEADS = 8
MLA_NOPE = 64
MLA_ROPE = 32
MLA_V = 64
HG_HEADS = 4
HG_DK = 128
HG_DV = 128
HG_CHUNK = 32
GRID_W = 64
ROPE_BASE = 10000.0
EC_FACTOR = 2
EPS = 1e-6

LANES = 128
SUBLANES = 8
BF16_ROWS = 16
VMEM_LIMIT = 56 * 1024 * 1024

TOK_BLOCK = 256
WIN_ROWS = 64
UNSELECTED = -(1 << 30)

NT_DIMS = (((1,), (1,)), ((), ()))


def _dot(a, b):
    return jnp.dot(a, b, preferred_element_type=F32)


def _dot_nt(a, b):
    return lax.dot_general(a, b, NT_DIMS, preferred_element_type=F32)


def _silu(x):
    return x * jax.nn.sigmoid(x)


def _params(*sem):
    return pltpu.CompilerParams(dimension_semantics=sem, vmem_limit_bytes=VMEM_LIMIT)


def _const_spec(shape):
    zeros = (0,) * len(shape)
    return pl.BlockSpec(shape, lambda *_: zeros, pipeline_mode=pl.Buffered(1))


def _adaln_kernel(c_ref, w_ref, b_ref, o_ref):
    s = _silu(c_ref[...]).astype(BF16)
    o_ref[...] = _dot(s, w_ref[...].astype(BF16)) + b_ref[...]


def _adaln(cond, w_ada, b_ada):
    rows, d = cond.shape
    n = w_ada.shape[1]
    tn = n // 4
    return pl.pallas_call(
        _adaln_kernel,
        out_shape=jax.ShapeDtypeStruct((rows, n), F32),
        grid=(n // tn,),
        in_specs=[_const_spec((rows, d)),
                  pl.BlockSpec((d, tn), lambda j: (0, j)),
                  pl.BlockSpec((1, tn), lambda j: (0, j))],
        out_specs=pl.BlockSpec((rows, tn), lambda j: (0, j)),
        compiler_params=_params("arbitrary"),
        name="adaln",
    )(cond, w_ada, b_ada.reshape(1, n))


def _rms(x, g):
    return x * lax.rsqrt(jnp.mean(x * x, axis=-1, keepdims=True) + EPS) * g


def _rope(x, c, s):
    w = x.shape[-1]
    lane = lax.broadcasted_iota(jnp.int32, x.shape, 1)
    nxt = pltpu.roll(x, w - 1, 1)
    prv = pltpu.roll(x, 1, 1)
    return x * c + jnp.where(lane % 2 == 0, nxt, prv) * s


N_INPROJ_WEIGHTS = 6


def _mod_row(mod_ref, row0, per_batch):
    r = row0 + pl.program_id(0) if per_batch else row0
    return mod_ref[pl.ds(r, 1), :]


def _modulated(x_ref, m):
    d = x_ref.shape[1]
    return (x_ref[...] * (1.0 + m[:, d:2 * d]) + m[:, 0:d]).astype(BF16)


def _keys(k_nope, kpe):
    shared = pltpu.roll(kpe, MLA_NOPE, 1)
    return (k_nope + jnp.concatenate([shared] * MLA_HEADS, axis=1)).astype(BF16)


def _inproj_kernel(*refs, row0, per_batch, rope):
    x_ref, mod_ref = refs[:2]
    _inproj_body(x_ref, _mod_row(mod_ref, row0, per_batch), *refs[2:], rope=rope)


def _inproj_body(x_ref, m, *refs, rope):
    gam_ref, win_ref, qn_ref, wuq_ref, kvn_ref, wk_ref, wv_ref = refs[:1 + N_INPROJ_WEIGHTS]
    refs = refs[1 + N_INPROJ_WEIGHTS:]
    if rope:
        cq_ref, sq_ref, ck_ref, sk_ref = refs[:4]
        refs = refs[4:]
    q_o, k_o, v_o, ckv_o, kpe_o, hgx_o = refs
    h = _modulated(x_ref, m)
    hw = HG_HEADS * HG_DK
    o_kv = qn_ref.shape[1]
    o_pe = o_kv + kvn_ref.shape[1]
    o_h = o_pe + MLA_ROPE

    cq = _rms(_dot_nt(h, win_ref[0:o_kv, :]), qn_ref[...])
    q = _dot(cq.astype(BF16), wuq_ref[...])
    if rope:
        q = _rope(q, jnp.concatenate([cq_ref[...]] * MLA_HEADS, axis=1),
                  jnp.concatenate([sq_ref[...]] * MLA_HEADS, axis=1))
    q_o[...] = q.astype(BF16)

    ckv = _rms(_dot_nt(h, win_ref[o_kv:o_pe, :]), kvn_ref[...])
    ckv_o[...] = ckv
    kpe = _dot_nt(h, win_ref[o_pe:o_h, :])
    kpe = jnp.concatenate([kpe, jnp.zeros((kpe.shape[0], LANES - MLA_ROPE), F32)], axis=1)
    if rope:
        kpe = _rope(kpe, ck_ref[...], sk_ref[...])
    kpe_o[...] = kpe
    cb = ckv.astype(BF16)
    k_o[...] = _keys(_dot(cb, wk_ref[...]), kpe)
    v_o[...] = _dot(cb, wv_ref[...]).astype(BF16)

    z = _dot_nt(h, win_ref[o_h:o_h + 5 * hw, :])
    hgx_o[:, 0:hw] = _silu(z[:, 0:hw])
    for dr in range(2):
        g0, g1 = gam_ref[dr, 0:1, :], gam_ref[dr, 1:2, :]
        gmax = jnp.maximum(g0, g1)
        e0, e1 = jnp.exp(g0 - gmax), jnp.exp(g1 - gmax)
        lb = e0 / (e0 + e1)
        f = lb + (1.0 - lb) * jax.nn.sigmoid(z[:, (1 + dr) * hw:(2 + dr) * hw])
        hgx_o[:, (1 + 2 * dr) * hw:(2 + 2 * dr) * hw] = jnp.log(f)
        hgx_o[:, (2 + 2 * dr) * hw:(3 + 2 * dr) * hw] = 1.0 - f
    hgx_o[:, 5 * hw:6 * hw] = z[:, 3 * hw:4 * hw]
    hgx_o[:, 6 * hw:7 * hw] = z[:, 4 * hw:5 * hw]


def _inproj(x2d, batch, seq, mod, gamma, wts, row0, per_batch, rope_tabs):
    n, d = x2d.shape
    tm = TOK_BLOCK
    nblk = seq // tm
    rope = rope_tabs is not None
    hp = MLA_HEADS * LANES
    hw = HG_HEADS * HG_DK
    tok = lambda b, i: (b * nblk + i, 0)
    pos = lambda b, i: (i, 0)
    weights = [wts[k] for k in INPROJ_KEYS]
    ins = [x2d, mod, gamma] + weights
    in_specs = ([pl.BlockSpec((tm, d), tok), _const_spec(mod.shape), _const_spec(gamma.shape)]
                + [_const_spec(w.shape) for w in weights])
    if rope:
        ins += list(rope_tabs)
        in_specs += [pl.BlockSpec((tm, t.shape[1]), pos) for t in rope_tabs]
    widths = [(hp, BF16), (hp, BF16), (MLA_HEADS * MLA_V, BF16), (wts["kvn"].shape[1], F32), (LANES, F32),
              (7 * hw, F32)]
    return pl.pallas_call(
        functools.partial(_inproj_kernel, row0=row0, per_batch=per_batch, rope=rope),
        out_shape=[jax.ShapeDtypeStruct((n, w), dt) for w, dt in widths],
        grid=(batch, nblk),
        in_specs=in_specs,
        out_specs=[pl.BlockSpec((tm, w), tok) for w, _ in widths],
        compiler_params=_params("arbitrary", "arbitrary"),
        name="inproj",
    )(*ins)


def _kvup_kernel(ckv_ref, kpe_ref, wk_ref, wv_ref, k_o, v_o):
    cb = ckv_ref[...].astype(BF16)
    k_o[...] = _keys(_dot(cb, wk_ref[...]), kpe_ref[...])
    v_o[...] = _dot(cb, wv_ref[...]).astype(BF16)


def _kvup(ckv2d, kpe2d, wts):
    n = ckv2d.shape[0]
    tm = TOK_BLOCK
    widths = [MLA_HEADS * LANES, MLA_HEADS * MLA_V]
    row = lambda i: (i, 0)
    ws = [wts["wk"], wts["wv"]]
    return pl.pallas_call(
        _kvup_kernel,
        out_shape=[jax.ShapeDtypeStruct((n, w), BF16) for w in widths],
        grid=(n // tm,),
        in_specs=[pl.BlockSpec((tm, ckv2d.shape[1]), row), pl.BlockSpec((tm, LANES), row)]
                 + [_const_spec(w.shape) for w in ws],
        out_specs=[pl.BlockSpec((tm, w), row) for w in widths],
        compiler_params=_params("arbitrary"),
        name="kvup",
    )(ckv2d, kpe2d, *ws)


ATTN_SCALE = (MLA_NOPE + MLA_ROPE) ** -0.5


def _attn_kernel(*refs, cached):
    if cached:
        q_ref, k_ref, v_ref, kc_ref, vc_ref, o_ref = refs
    else:
        q_ref, k_ref, v_ref, o_ref = refs
        kc_ref = vc_ref = None
    _attn_body(q_ref, k_ref, v_ref, kc_ref, vc_ref, o_ref)


def _attn_body(q_ref, k_ref, v_ref, kc_ref, vc_ref, o_ref):
    cached = kc_ref is not None
    scale = ATTN_SCALE
    per_slab = LANES // MLA_V
    own = lax.broadcasted_iota(jnp.int32, (q_ref.shape[0], LANES), 1) // MLA_V
    for slab in range(MLA_HEADS // per_slab):
        vsl = slice(slab * LANES, (slab + 1) * LANES)
        out = None
        for sub in range(per_slab):
            hd = slab * per_slab + sub
            sl = slice(hd * LANES, (hd + 1) * LANES)
            q = q_ref[:, sl]
            s = _dot_nt(q, k_ref[:, sl]) * scale
            mx = jnp.max(s, axis=-1, keepdims=True)
            if cached:
                s2 = _dot_nt(q, kc_ref[:, sl]) * scale
                mx = jnp.maximum(mx, jnp.max(s2, axis=-1, keepdims=True))
            e = jnp.exp(s - mx)
            den = jnp.sum(e, axis=-1, keepdims=True)
            o = _dot(e.astype(BF16), v_ref[:, vsl])
            if cached:
                e2 = jnp.exp(s2 - mx)
                den = den + jnp.sum(e2, axis=-1, keepdims=True)
                o = o + _dot(e2.astype(BF16), vc_ref[:, vsl])
            o = o / den
            out = o if out is None else jnp.where(own == sub, o, out)
        o_ref[:, vsl] = out.astype(o_ref.dtype)


def _attn(q, k, v, batch, seq, cache=None):
    n, hp = q.shape
    hv = v.shape[1]
    tq = TOK_BLOCK
    nblk = seq // tq
    ins = [q, k, v]
    in_specs = [pl.BlockSpec((tq, hp), lambda b, i: (b * nblk + i, 0)),
                pl.BlockSpec((seq, hp), lambda b, i: (b, 0)),
                pl.BlockSpec((seq, hv), lambda b, i: (b, 0))]
    if cache is not None:
        past = cache[0].shape[0] // batch
        ins += list(cache)
        in_specs += [pl.BlockSpec((past, hp), lambda b, i: (b, 0)),
                     pl.BlockSpec((past, hv), lambda b, i: (b, 0))]
    return pl.pallas_call(
        functools.partial(_attn_kernel, cached=cache is not None),
        out_shape=jax.ShapeDtypeStruct((n, hv), BF16),
        grid=(batch, nblk),
        in_specs=in_specs,
        out_specs=pl.BlockSpec((tq, hv), lambda b, i: (b * nblk + i, 0)),
        compiler_params=_params("arbitrary", "arbitrary"),
        name="attn",
    )(*ins)


def _chunk_scan(x, reverse):
    tm = x.shape[0]
    rin = lax.broadcasted_iota(jnp.int32, x.shape, 0) % HG_CHUNK
    step = 1
    while step < HG_CHUNK:
        if reverse:
            x = x + jnp.where(rin < HG_CHUNK - step, pltpu.roll(x, tm - step, 0), 0.0)
        else:
            x = x + jnp.where(rin >= step, pltpu.roll(x, step, 0), 0.0)
        step *= 2
    return x


def _hgrn_kernel(*refs, has_init):
    fwd, bwd = refs[0:4], refs[4:8]
    refs = refs[8:]
    s0_ref = None
    if has_init:
        s0_ref = refs[0]
        refs = refs[1:]
    of_ref, ob_ref, sfin_ref, st_scr = refs
    i = pl.program_id(1)
    _hgrn_body(fwd, bwd, s0_ref, of_ref, ob_ref, sfin_ref, st_scr, i == 0, i == pl.num_programs(1) - 1)


def _hgrn_body(fwd, bwd, s0_ref, of_ref, ob_ref, sfin_ref, st_scr, first, last):
    tm = fwd[0].shape[0]
    c = HG_CHUNK
    nch = tm // c
    dk, dv = HG_DK, HG_DV
    hw = HG_HEADS * dk

    def initial(dr, hd):
        return s0_ref[0, dr, hd].T if s0_ref is not None else jnp.zeros((dv, dk), F32)

    if st_scr is not None:
        @pl.when(first)
        def _init():
            for dr in range(2):
                for hd in range(HG_HEADS):
                    st_scr[dr, hd] = initial(dr, hd)

    npair = nch // 2
    pair = 2 * c
    row = lax.broadcasted_iota(jnp.int32, (tm, tm), 0)
    col = lax.broadcasted_iota(jnp.int32, (tm, tm), 1)
    same = (row // c) == (col // c)
    same_pair = (row // pair) == (col // pair)
    bd = (lax.broadcasted_iota(jnp.int32, (tm, npair * dk), 0) // pair
          == lax.broadcasted_iota(jnp.int32, (tm, npair * dk), 1) // dk)
    chunk_odd = (lax.broadcasted_iota(jnp.int32, (tm, hw), 0) // c) % 2 == 1

    for dr, (hq_ref, lf_ref, kk_ref, vv_ref) in enumerate((fwd, bwd)):
        o_ref = of_ref if dr == 0 else ob_ref
        tri = same & ((col <= row) if dr == 0 else (col >= row))
        cross = same_pair & (((row // c) > (col // c)) if dr == 0 else ((row // c) < (col // c)))
        second = chunk_odd if dr == 0 else ~chunk_odd
        bcum = _chunk_scan(lf_ref[...], reverse=dr == 1)
        closing = c - 1 if dr == 0 else 0
        btot3 = bcum.reshape(nch, c, hw)[:, closing:closing + 1, :]
        btot = jnp.broadcast_to(btot3, (nch, c, hw)).reshape(tm, hw)
        bpart = jnp.where(chunk_odd, pltpu.roll(btot, c, 0), pltpu.roll(btot, tm - c, 0))
        epart = jnp.exp(bpart)
        bpair = btot + bpart
        kk = kk_ref[...]
        qd = hq_ref[...] * jnp.exp(bcum)
        kd = kk * jnp.exp(-bcum)
        ke = kk * jnp.exp(btot - bcum)
        qd2 = jnp.where(second, qd * epart, qd)
        ke2 = jnp.where(second, ke, ke * epart)
        vv = vv_ref[...]
        order = range(npair) if dr == 0 else range(npair - 1, -1, -1)
        for hd in range(HG_HEADS):
            sl = slice(hd * dk, (hd + 1) * dk)
            qd_h = qd[:, sl].astype(BF16)
            v_h = vv[:, hd * dv:(hd + 1) * dv]
            a = jnp.where(tri, _dot_nt(qd_h, kd[:, sl].astype(BF16)),
                          jnp.where(cross, _dot_nt(qd_h, ke[:, sl].astype(BF16)), 0.0))
            o_intra = _dot(a.astype(BF16), v_h.astype(BF16))
            kebd = jnp.where(bd, jnp.concatenate([ke2[:, sl]] * npair, axis=1), 0.0).astype(BF16)
            qbd = jnp.where(bd, jnp.concatenate([qd2[:, sl]] * npair, axis=1), 0.0).astype(BF16)
            ut = _dot(v_h.T.astype(BF16), kebd)
            st = st_scr[dr, hd] if st_scr is not None else initial(dr, hd)
            prev = [None] * npair
            for p in order:
                prev[p] = st
                st = st * jnp.exp(bpair[p * pair:p * pair + 1, sl]) + ut[:, p * dk:(p + 1) * dk]
            if st_scr is not None:
                st_scr[dr, hd] = st
            o_inter = _dot_nt(qbd, jnp.concatenate(prev, axis=1).astype(BF16))
            o_ref[:, hd * dv:(hd + 1) * dv] = o_intra + o_inter

            if last is True:
                sfin_ref[0, dr, hd] = st.T
            else:
                @pl.when(last)
                def _final(st=st, dr=dr, hd=hd):
                    sfin_ref[0, dr, hd] = st.T


def _hgrn(hgx, batch, seq, s0=None):
    n = hgx.shape[0]
    tm = TOK_BLOCK
    nblk = seq // tm
    hw = HG_HEADS * HG_DK

    def spec(lane_blk, rev):
        if rev:
            return pl.BlockSpec((tm, hw), lambda b, i: (b * nblk + nblk - 1 - i, lane_blk))
        return pl.BlockSpec((tm, hw), lambda b, i: (b * nblk + i, lane_blk))

    in_specs = [spec(0, False), spec(1, False), spec(2, False), spec(5, False),
                spec(0, True), spec(3, True), spec(4, True), spec(5, True)]
    ins = [hgx] * 8
    st_shape = (1, 2, HG_HEADS, HG_DK, HG_DV)
    st_spec = pl.BlockSpec(st_shape, lambda b, i: (b, 0, 0, 0, 0))
    if s0 is not None:
        ins.append(s0)
        in_specs.append(st_spec)
    return pl.pallas_call(
        functools.partial(_hgrn_kernel, has_init=s0 is not None),
        out_shape=[jax.ShapeDtypeStruct((n, hw), F32), jax.ShapeDtypeStruct((n, hw), F32),
                   jax.ShapeDtypeStruct((batch,) + st_shape[1:], F32)],
        grid=(batch, nblk),
        in_specs=in_specs,
        out_specs=[spec(0, False), spec(0, True), st_spec],
        scratch_shapes=[pltpu.VMEM((2, HG_HEADS, HG_DV, HG_DK), F32)],
        compiler_params=_params("arbitrary", "arbitrary"),
        name="hgrn",
    )(*ins)


def _layer_norm(x, g, b):
    xc = x - jnp.mean(x, axis=-1, keepdims=True)
    var = jnp.mean(xc * xc, axis=-1, keepdims=True)
    return xc * lax.rsqrt(var + EPS) * g + b


N_POSTMIX_WEIGHTS = 8
INPROJ_KEYS = ("win", "qn", "wuq", "kvn", "wk", "wv")
POSTMIX_KEYS = ("hgn", "womla", "wohg", "wout", "ln1g", "ln1b", "wr")
MIXER_SEQS = 2


def _postmix_kernel(x_ref, mod_ref, *refs, alpha, row0, per_batch):
    _postmix_body(x_ref, _mod_row(mod_ref, row0, per_batch), *refs, alpha=alpha)


def _postmix_body(x_ref, m, of_ref, ob_ref, zg_ref, om_ref, wg_ref, hgn_ref, womla_ref,
                  wohg_ref, wout_ref, lng_ref, lnb_ref, wr_ref, x1_o, h2_o, aff_o, *, alpha):
    d = x_ref.shape[1]
    tb = aff_o.shape[2]
    g1, sh2, sc2 = m[:, 2 * d:3 * d], m[:, 3 * d:4 * d], m[:, 4 * d:5 * d]
    o = of_ref[...] + ob_ref[...]
    zg = zg_ref[...]
    parts = []
    for hd in range(HG_HEADS):
        sl = slice(hd * HG_DV, (hd + 1) * HG_DV)
        parts.append(_rms(o[:, sl], hgn_ref[...]) * _silu(zg[:, sl]))
    ohg = jnp.concatenate(parts, axis=1).astype(BF16)
    gates = _dot_nt(_modulated(x_ref, m), wg_ref[wg_ref.shape[0] - 2 * d:, :])
    merged = (jax.nn.sigmoid(gates[:, 0:d]) * _dot(om_ref[...], womla_ref[...])
              + jax.nn.sigmoid(gates[:, d:2 * d]) * _dot(ohg, wohg_ref[...]))
    mix = _dot(merged.astype(BF16), wout_ref[...])
    x1 = _layer_norm(alpha * x_ref[...] + g1 * mix, lng_ref[...], lnb_ref[...])
    x1_o[...] = x1
    h2 = (x1 * (1.0 + sc2) + sh2).astype(BF16)
    h2_o[...] = h2
    logits = _dot_nt(wr_ref[...], h2)
    e = jnp.exp(logits - jnp.max(logits, axis=0, keepdims=True))
    aff = e / jnp.sum(e, axis=0, keepdims=True)
    for blk in range(aff_o.shape[0]):
        aff_o[blk] = aff[:, blk * tb:(blk + 1) * tb]


def _postmix(x2d, batch, seq, mod, o_f, o_b, hgx, o_mla, wts, alpha, n_experts, row0, per_batch):
    n, d = x2d.shape
    tm = TOK_BLOCK
    nblk = seq // tm
    hw = HG_HEADS * HG_DV
    tok = lambda b, i: (b * nblk + i, 0)
    weights = [wts[k] for k in POSTMIX_KEYS]
    return pl.pallas_call(
        functools.partial(_postmix_kernel, alpha=alpha, row0=row0, per_batch=per_batch),
        out_shape=[jax.ShapeDtypeStruct((n, d), F32), jax.ShapeDtypeStruct((n, d), BF16),
                   jax.ShapeDtypeStruct((n // tm, n_experts, tm), F32)],
        grid=(batch, nblk),
        in_specs=[pl.BlockSpec((tm, d), tok), _const_spec(mod.shape),
                  pl.BlockSpec((tm, hw), tok), pl.BlockSpec((tm, hw), tok),
                  pl.BlockSpec((tm, hw), lambda b, i: (b * nblk + i, 6)),
                  pl.BlockSpec((tm, o_mla.shape[1]), tok), _const_spec(wts["win"].shape)]
                 + [_const_spec(w.shape) for w in weights],
        out_specs=[pl.BlockSpec((tm, d), tok), pl.BlockSpec((tm, d), tok),
                   pl.BlockSpec((1, n_experts, tm), lambda b, i: (b * nblk + i, 0, 0))],
        compiler_params=_params("arbitrary", "arbitrary"),
        name="postmix",
    )(x2d, mod, o_f, o_b, hgx, o_mla, wts["win"], *weights)


def _mixer_kernel(x_ref, mod_ref, *refs, alpha, row0, seq):
    nw = 1 + N_INPROJ_WEIGHTS
    in_w, refs = refs[:nw], refs[nw:]
    pm_w, refs = refs[:N_POSTMIX_WEIGHTS - 1], refs[N_POSTMIX_WEIGHTS - 1:]
    pm_w = (in_w[1],) + tuple(pm_w)
    x1_o, h2_o, aff_o, ckv_o, kpe_o, sfin_o, q_s, k_s, v_s, hgx_s, om_s, of_s, ob_s = refs
    m = _mod_row(mod_ref, row0, False)
    _inproj_body(x_ref, m, *in_w, q_s, k_s, v_s, ckv_o, kpe_o, hgx_s, rope=False)
    hw = HG_HEADS * HG_DK
    for s in range(x_ref.shape[0] // seq):
        rows = slice(s * seq, (s + 1) * seq)
        _attn_body(q_s.at[rows], k_s.at[rows], v_s.at[rows], None, None, om_s.at[rows])
        lane = lambda j: hgx_s.at[rows, j * hw:(j + 1) * hw]
        _hgrn_body((lane(0), lane(1), lane(2), lane(5)), (lane(0), lane(3), lane(4), lane(5)), None,
                   of_s.at[rows], ob_s.at[rows], sfin_o.at[s:s + 1], None, True, True)
    _postmix_body(x_ref, m, of_s, ob_s, hgx_s.at[:, 6 * hw:7 * hw], om_s, *pm_w, x1_o, h2_o, aff_o,
                  alpha=alpha)


def _mixer(x2d, batch, seq, mod, gamma, wts, alpha, n_experts, row0):
    n, d = x2d.shape
    assert seq == TOK_BLOCK
    ns = MIXER_SEQS if batch % MIXER_SEQS == 0 else 1
    tm = ns * seq
    hp = MLA_HEADS * LANES
    hv = MLA_HEADS * MLA_V
    hw = HG_HEADS * HG_DK
    kvl = wts["kvn"].shape[1]
    weights = [wts[k] for k in INPROJ_KEYS + POSTMIX_KEYS]
    tok = lambda b: (b, 0)
    st_shape = (ns, 2, HG_HEADS, HG_DK, HG_DV)
    return pl.pallas_call(
        functools.partial(_mixer_kernel, alpha=alpha, row0=row0, seq=seq),
        out_shape=[jax.ShapeDtypeStruct((n, d), F32), jax.ShapeDtypeStruct((n, d), BF16),
                   jax.ShapeDtypeStruct((n // seq, n_experts, seq), F32),
                   jax.ShapeDtypeStruct((n, kvl), F32), jax.ShapeDtypeStruct((n, LANES), F32),
                   jax.ShapeDtypeStruct((batch,) + st_shape[1:], F32)],
        grid=(batch // ns,),
        in_specs=[pl.BlockSpec((tm, d), tok), _const_spec(mod.shape), _const_spec(gamma.shape)]
                 + [_const_spec(w.shape) for w in weights],
        out_specs=[pl.BlockSpec((tm, d), tok), pl.BlockSpec((tm, d), tok),
                   pl.BlockSpec((ns, n_experts, seq), lambda b: (b, 0, 0)),
                   pl.BlockSpec((tm, kvl), tok), pl.BlockSpec((tm, LANES), tok),
                   pl.BlockSpec(st_shape, lambda b: (b, 0, 0, 0, 0))],
        scratch_shapes=[pltpu.VMEM((tm, hp), BF16), pltpu.VMEM((tm, hp), BF16), pltpu.VMEM((tm, hv), BF16),
                        pltpu.VMEM((tm, 7 * hw), F32), pltpu.VMEM((tm, hv), BF16),
                        pltpu.VMEM((tm, hw), F32), pltpu.VMEM((tm, hw), F32)],
        compiler_params=_params("arbitrary"),
        name="mixer",
    )(x2d, mod, gamma, *weights)


def _route_kernel(aff_ref, rank_o, cnt_o, *, cap):
    nb, ne, tb = aff_ref.shape
    key = aff_ref[...]

    def count(mask):
        return jnp.sum(jnp.sum(jnp.where(mask, 1.0, 0.0), axis=0), axis=1, keepdims=True)

    def bit_step(it, bits):
        cand = bits | jnp.left_shift(jnp.int32(1), 30 - it)
        return jnp.where(count(key >= pltpu.bitcast(cand, F32)[None]) >= cap, cand, bits)

    bits = lax.fori_loop(0, 31, bit_step, jnp.zeros((ne, 1), jnp.int32))
    thr = pltpu.bitcast(bits, F32)
    need = cap - count(key > thr[None])
    before = (lax.broadcasted_iota(jnp.int32, (tb, tb), 0)
              < lax.broadcasted_iota(jnp.int32, (tb, tb), 1))
    before = jnp.where(before, 1.0, 0.0).astype(BF16)
    off_eq = jnp.zeros((ne, 1), F32)
    off_sel = jnp.zeros((ne, 1), F32)
    cnt_o[...] = jnp.zeros_like(cnt_o)
    for blk in range(nb):
        key_b = key[blk]
        eq = key_b == thr
        eq_b = jnp.where(eq, 1.0, 0.0)
        eq_rank = _dot(eq_b.astype(BF16), before) + off_eq
        sel = (key_b > thr) | (eq & (eq_rank < need))
        sel_b = jnp.where(sel, 1.0, 0.0)
        rank = _dot(sel_b.astype(BF16), before) + off_sel
        rank_o[blk] = jnp.where(sel, rank.astype(jnp.int32), UNSELECTED)
        cnt_o[:, blk:blk + 1] = off_sel.astype(jnp.int32)
        off_eq = off_eq + jnp.sum(eq_b, axis=1, keepdims=True)
        off_sel = off_sel + jnp.sum(sel_b, axis=1, keepdims=True)
    cnt_o[:, nb:nb + 1] = off_sel.astype(jnp.int32)


def _route(aff, cap):
    nb, ne, tb = aff.shape
    assert nb + 1 <= LANES
    rank, cnt = pl.pallas_call(
        functools.partial(_route_kernel, cap=cap),
        out_shape=[jax.ShapeDtypeStruct(aff.shape, jnp.int32), jax.ShapeDtypeStruct((ne, LANES), jnp.int32)],
        in_specs=[pl.BlockSpec(memory_space=pltpu.VMEM)],
        out_specs=[pl.BlockSpec(memory_space=pltpu.VMEM)] * 2,
        compiler_params=pltpu.CompilerParams(vmem_limit_bytes=VMEM_LIMIT),
        name="route",
    )(aff)
    return rank, cnt[:, :nb + 1]


def _window_hits(rk_ref, firsts, slot0, win):
    ne, tb = rk_ref.shape[1], rk_ref.shape[2]
    win_iota = lax.broadcasted_iota(jnp.int32, (win, tb), 0)
    return [(rk_ref[0, e:e + 1, :] + (slot0 - firsts[e])) == win_iota for e in range(ne)]


def _compact_kernel(first_ref, end_ref, rounds_ref, *refs, groups, slots):
    ng = len(groups)
    h2_refs, rk_refs, af_refs = refs[0:ng], refs[ng:2 * ng], refs[2 * ng:3 * ng]
    hbms, refs = refs[3 * ng:3 * ng + 2], refs[3 * ng + 2:]
    stages, tails = refs[0:2], refs[2:4]
    sem, issued = refs[4:]
    b = pl.program_id(0)
    ne = rk_refs[0].shape[1]
    win = WIN_ROWS
    sub = BF16_ROWS

    def copies(slot, dsts):
        return [pltpu.make_async_copy(stage.at[slot, pl.ds(e * win, win), :],
                                      hbm.at[e, pl.ds(pl.multiple_of(dsts[e], sub), win), :], sem.at[c, e])
                for c, (stage, hbm) in enumerate(zip(stages, hbms)) for e in range(ne)]

    def wait_previous():
        @pl.when(issued[0] > 0)
        def _():
            for cp in copies(0, [0] * ne):
                cp.wait()

    @pl.when(b == 0)
    def _init():
        issued[0] = 0
        for stage, tail in zip(stages, tails):
            tail[...] = jnp.zeros_like(tail)
            stage[1] = jnp.zeros(stage.shape[1:], stage.dtype)
        pad = copies(1, [slots] * ne)
        for cp in pad:
            cp.start()
        for cp in pad:
            cp.wait()

    def group_body(h2_ref, rk_ref, af_ref, slot0):
        firsts = [first_ref[b * ne + e] for e in range(ne)]
        bases = [(f // sub) * sub for f in firsts]
        ends = [end_ref[b * ne + e] - bases[e] for e in range(ne)]

        def one_round(r, carry):
            dsts = [bases[e] + r * win for e in range(ne)]
            hits = _window_hits(rk_ref, dsts, slot0, win)
            onehot = jnp.where(jnp.concatenate(hits, axis=0), 1.0, 0.0).astype(BF16)
            gate = jnp.concatenate(
                [jnp.sum(jnp.where(hits[e], af_ref[0, e:e + 1, :], 0.0), axis=1, keepdims=True)
                 for e in range(ne)], axis=0)
            slot = issued[0] % 2
            stages[0][slot] = _dot(onehot, h2_ref[...]).astype(BF16)
            stages[1][slot] = jnp.broadcast_to(gate, (ne * win, LANES))
            @pl.when(r == 0)
            def _heads():
                for stage, tail in zip(stages, tails):
                    sub_iota = lax.broadcasted_iota(jnp.int32, (sub, stage.shape[2]), 0)
                    for e in range(ne):
                        head = stage[slot, e * win:e * win + sub, :]
                        stage[slot, e * win:e * win + sub, :] = jnp.where(
                            sub_iota < firsts[e] - bases[e], tail[e * sub:(e + 1) * sub, :], head)

            for stage, tail in zip(stages, tails):
                for e in range(ne):
                    last = (ends[e] // sub) * sub
                    group = stage[slot, pl.ds(pl.multiple_of(e * win + last % win, sub), sub), :]
                    tail[e * sub:(e + 1) * sub, :] = jnp.where(r == last // win, group,
                                                               tail[e * sub:(e + 1) * sub, :])
            wait_previous()
            for cp in copies(slot, [jnp.minimum(dst, slots) for dst in dsts]):
                cp.start()
            issued[0] = issued[0] + 1
            return carry

        lax.fori_loop(0, rounds_ref[b], one_round, 0)

    blk0 = 0
    for gi, g in enumerate(groups):
        @pl.when((b >= blk0) & (b < blk0 + g["nb"]))
        def _(gi=gi, g=g):
            group_body(h2_refs[gi], rk_refs[gi], af_refs[gi], g["slot0"])
        blk0 += g["nb"]

    @pl.when(b == pl.num_programs(0) - 1)
    def _drain():
        wait_previous()


def _compact(groups, first, end, rounds, slots):
    d = groups[0]["h2"].shape[1]
    nbs = [g["rank"].shape[0] for g in groups]
    ne, tb = groups[0]["rank"].shape[1:]
    meta, specs_h2, specs_rk = [], [], []
    blk0 = 0
    for g, nb in zip(groups, nbs):
        meta.append(dict(nb=nb, slot0=g["slot0"]))
        local = lambda b, *_, blk0=blk0, nb=nb: jnp.clip(b - blk0, 0, nb - 1)
        specs_h2.append(pl.BlockSpec((tb, d), lambda b, *_, local=local: (local(b), 0)))
        specs_rk.append(pl.BlockSpec((1, ne, tb), lambda b, *_, local=local: (local(b), 0, 0)))
        blk0 += nb
    streams = [(d, BF16), (LANES, F32)]
    return pl.pallas_call(
        functools.partial(_compact_kernel, groups=meta, slots=slots),
        out_shape=[jax.ShapeDtypeStruct((ne, slots + WIN_ROWS, w), dt) for w, dt in streams],
        grid_spec=pltpu.PrefetchScalarGridSpec(
            num_scalar_prefetch=3,
            grid=(sum(nbs),),
            in_specs=specs_h2 + specs_rk + specs_rk,
            out_specs=[pl.BlockSpec(memory_space=pl.ANY)] * 2,
            scratch_shapes=[pltpu.VMEM((2, ne * WIN_ROWS, w), dt) for w, dt in streams]
                           + [pltpu.VMEM((ne * BF16_ROWS, w), dt) for w, dt in streams]
                           + [pltpu.SemaphoreType.DMA((2, ne)), pltpu.SMEM((1,), jnp.int32)]),
        compiler_params=_params("arbitrary"),
        name="compact",
    )(first, end, rounds, *[g["h2"] for g in groups], *[g["rank"] for g in groups],
      *[g["aff"] for g in groups])


def _ffn_kernel(xe_ref, ge_ref, w1_ref, w3_ref, w2_ref, ye_ref, acc_scr):
    f = pl.program_id(1)
    x = xe_ref[0]
    hid = _silu(_dot(x, w1_ref[0].astype(BF16))) * _dot(x, w3_ref[0].astype(BF16))
    y = _dot(hid.astype(BF16), w2_ref[0].astype(BF16))

    last = pl.num_programs(1) - 1

    @pl.when((f == 0) & (f < last))
    def _first():
        acc_scr[...] = y

    @pl.when((f > 0) & (f < last))
    def _middle():
        acc_scr[...] += y

    @pl.when((f == last) & (f > 0))
    def _last():
        ye_ref[0] = ((acc_scr[...] + y) * ge_ref[0, :, 0:1]).astype(ye_ref.dtype)

    @pl.when((f == last) & (f == 0))
    def _only():
        ye_ref[0] = (y * ge_ref[0, :, 0:1]).astype(ye_ref.dtype)


def _ffn(xe, ge, w1, w3, w2, slots, ft):
    ne, d, dff = w1.shape
    nf = dff // ft
    return pl.pallas_call(
        _ffn_kernel,
        out_shape=jax.ShapeDtypeStruct((ne, slots, d), BF16),
        grid=(ne, nf),
        in_specs=[pl.BlockSpec((1, slots, d), lambda e, f: (e, 0, 0)),
                  pl.BlockSpec((1, slots, ge.shape[2]), lambda e, f: (e, 0, 0)),
                  pl.BlockSpec((1, d, ft), lambda e, f: (e, 0, f)),
                  pl.BlockSpec((1, d, ft), lambda e, f: (e, 0, f)),
                  pl.BlockSpec((1, ft, d), lambda e, f: (e, f, 0))],
        out_specs=pl.BlockSpec((1, slots, d), lambda e, f: (e, 0, 0)),
        scratch_shapes=[pltpu.VMEM((slots, d), F32)],
        compiler_params=_params("arbitrary", "arbitrary"),
        name="ffn",
    )(xe, ge, w1, w3, w2)


def _combine_kernel(first_ref, rounds_ref, rk_ref, x1_ref, mod_ref, lng_ref, lnb_ref, ye_hbm, out_ref,
                    buf, acc_scr, sem, *, d_model, alpha, slot0, slots, row0, blocks_per_batch):
    d = d_model
    b = pl.program_id(0)
    nblk = pl.num_programs(0)
    ne, tb = rk_ref.shape[1], rk_ref.shape[2]
    win = WIN_ROWS
    eye = (lax.broadcasted_iota(jnp.int32, (tb, tb), 0)
           == lax.broadcasted_iota(jnp.int32, (tb, tb), 1))
    eye = jnp.where(eye, 1.0, 0.0).astype(BF16)

    def starts_of(blk, r):
        firsts = [(first_ref[blk * ne + e] // BF16_ROWS) * BF16_ROWS + r * win for e in range(ne)]
        return firsts, [jnp.minimum(f, slots - win) for f in firsts]

    def windows(slot, starts):
        return [pltpu.make_async_copy(ye_hbm.at[e, pl.ds(pl.multiple_of(starts[e], BF16_ROWS), win), :],
                                      buf.at[slot, pl.ds(e * win, win), :], sem.at[slot, e])
                for e in range(ne)]

    def scatter(slot, firsts, starts):
        hits = _window_hits(rk_ref, starts, slot0, win)
        hits = [h & ((rk_ref[0, e:e + 1, :] + slot0) >= firsts[e]) for e, h in enumerate(hits)]
        hit = jnp.where(jnp.concatenate(hits, axis=0), 1.0, 0.0).astype(BF16)
        hit_t = _dot_nt(eye, hit).astype(BF16)
        return _dot(hit_t, buf[slot])

    cur = b % 2

    @pl.when(b == 0)
    def _prime():
        for cp in windows(0, starts_of(0, 0)[1]):
            cp.start()

    @pl.when(b + 1 < nblk)
    def _prefetch():
        for cp in windows(1 - cur, starts_of(b + 1, 0)[1]):
            cp.start()

    firsts, starts = starts_of(b, 0)
    for cp in windows(cur, starts):
        cp.wait()
    acc_scr[...] = scatter(cur, firsts, starts)

    def extra_round(r, carry):
        firsts, starts = starts_of(b, r)
        for cp in windows(2, starts):
            cp.start()
        for cp in windows(2, starts):
            cp.wait()
        acc_scr[...] += scatter(2, firsts, starts)
        return carry

    lax.fori_loop(1, rounds_ref[b], extra_round, 0)
    r = row0 + b // blocks_per_batch
    g2 = mod_ref[pl.ds(r, 1), :][:, 5 * d:6 * d]
    out_ref[...] = _layer_norm(alpha * x1_ref[...] + g2 * acc_scr[...], lng_ref[...], lnb_ref[...])


def _combine(ye, rank, first, rounds, x1, mod, ln_g, ln_b, alpha, slot0, row0, blocks_per_batch):
    n, d = x1.shape
    nb, ne, tb = rank.shape
    slots = ye.shape[1]
    return pl.pallas_call(
        functools.partial(_combine_kernel, d_model=d, alpha=alpha, slot0=slot0, slots=slots, row0=row0,
                          blocks_per_batch=blocks_per_batch),
        out_shape=jax.ShapeDtypeStruct((n, d), F32),
        grid_spec=pltpu.PrefetchScalarGridSpec(
            num_scalar_prefetch=2,
            grid=(nb,),
            in_specs=[pl.BlockSpec((1, ne, tb), lambda b, *_: (b, 0, 0)),
                      pl.BlockSpec((tb, d), lambda b, *_: (b, 0)),
                      pl.BlockSpec(mod.shape, lambda b, *_: (0, 0)),
                      pl.BlockSpec((1, d), lambda b, *_: (0, 0)),
                      pl.BlockSpec((1, d), lambda b, *_: (0, 0)),
                      pl.BlockSpec(memory_space=pl.ANY)],
            out_specs=pl.BlockSpec((tb, d), lambda b, *_: (b, 0)),
            scratch_shapes=[pltpu.VMEM((3, ne * WIN_ROWS, d), ye.dtype), pltpu.VMEM((tb, d), F32),
                            pltpu.SemaphoreType.DMA((3, ne))]),
        compiler_params=_params("arbitrary"),
        name="combine",
    )(first, rounds, rank, x1, mod, ln_g, ln_b, ye)


def _prep_weights(w_in, q_norm, w_uq, kv_norm, w_ukv, w_o_mla, hgrn_norm, w_o_hg, w_out, ln1_g, ln1_b,
                  w_router):
    d = w_in.shape[0]
    q_lora, kv_lora = q_norm.shape[0], kv_norm.shape[0]
    hw = HG_HEADS * HG_DK
    hh, hp = MLA_HEADS, MLA_HEADS * LANES
    o_kv, o_pe = q_lora, q_lora + kv_lora
    o_h = o_pe + MLA_ROPE
    o_g = o_h + 5 * hw
    assert w_in.shape[1] == o_g + 2 * d
    qk = MLA_NOPE + MLA_ROPE
    kvw = MLA_NOPE + MLA_V
    b16 = lambda a: a.astype(BF16)
    assert all(o % BF16_ROWS == 0 for o in (o_kv, o_pe, o_h, o_g))
    win = b16(w_in.T)
    wuq = jnp.pad(w_uq.reshape(q_lora, hh, qk), ((0, 0), (0, 0), (0, LANES - qk))).reshape(q_lora, hp)
    ukv = w_ukv.reshape(kv_lora, hh, kvw)
    wk = jnp.pad(ukv[:, :, :MLA_NOPE], ((0, 0), (0, 0), (0, LANES - MLA_NOPE))).reshape(kv_lora, hp)
    wv = ukv[:, :, MLA_NOPE:].reshape(kv_lora, hh * MLA_V)
    return dict(
        win=win, qn=q_norm.reshape(1, -1), wuq=b16(wuq), kvn=kv_norm.reshape(1, -1), wk=b16(wk), wv=b16(wv),
        hgn=hgrn_norm.reshape(1, -1), womla=b16(w_o_mla), wohg=b16(w_o_hg), wout=b16(w_out),
        ln1g=ln1_g.reshape(1, -1), ln1b=ln1_b.reshape(1, -1), wr=b16(w_router.T))


def _rope_tables(seq):
    n_freq = MLA_ROPE // 4
    inv = ROPE_BASE ** (-np.arange(n_freq, dtype=np.float64) / n_freq)
    t = np.arange(seq)
    ang = np.concatenate([(t // GRID_W)[:, None] * inv, (t % GRID_W)[:, None] * inv], axis=-1)
    cos = np.repeat(np.cos(ang), 2, axis=1)
    sin = np.repeat(np.sin(ang), 2, axis=1) * np.tile([-1.0, 1.0], MLA_ROPE // 2)
    ck = np.pad(cos, ((0, 0), (0, LANES - MLA_ROPE)), constant_values=1.0)
    sk = np.pad(sin, ((0, 0), (0, LANES - MLA_ROPE)))
    cq = np.pad(cos, ((0, 0), (MLA_NOPE, LANES - MLA_NOPE - MLA_ROPE)), constant_values=1.0)
    sq = np.pad(sin, ((0, 0), (MLA_NOPE, LANES - MLA_NOPE - MLA_ROPE)))
    return tuple(jnp.asarray(a, F32) for a in (cq, sq, ck, sk))


def _window_sched(cnt, slot0):
    first = slot0 + cnt[:, :-1]
    end = slot0 + cnt[:, 1:]
    flat = lambda a: a.T.reshape(-1).astype(jnp.int32)
    rounds = jnp.max((end - (first // BF16_ROWS) * BF16_ROWS + WIN_ROWS - 1) // WIN_ROWS, axis=0)
    return flat(first), flat(end), jnp.maximum(rounds, 1).astype(jnp.int32)


def kernel(x_prompt, x_sample, c, cache_ckv, cache_kpe, state_hgrn, c_ctx, w_ada, b_ada, w_in, mla_q_norm, mla_w_uq, mla_kv_norm, mla_w_ukv, mla_w_o, hgrn_gamma, hgrn_norm, hgrn_w_o, w_out, ln1_g, ln1_b, moe_w_router, moe_w1, moe_w3, moe_w2, ln2_g, ln2_b):
    depth = w_ada.shape[0]
    assert depth == 1, "single trunk layer"
    bp, tp, d = x_prompt.shape
    bs, tsq, _ = x_sample.shape
    ne = moe_w_router.shape[-1]
    alpha = (2 * depth) ** 0.25
    past = cache_ckv.shape[2]
    assert tp % TOK_BLOCK == 0 and tsq % TOK_BLOCK == 0 and past % TOK_BLOCK == 0 and tsq % GRID_W == 0

    wts = _prep_weights(w_in[0], mla_q_norm[0], mla_w_uq[0], mla_kv_norm[0], mla_w_ukv[0], mla_w_o[0],
                        hgrn_norm[0], hgrn_w_o[0], w_out[0], ln1_g[0], ln1_b[0], moe_w_router[0])
    cond_rows = -(-(1 + bs) // SUBLANES) * SUBLANES
    cond = jnp.concatenate([c_ctx[None], c, jnp.zeros((cond_rows - 1 - bs, d), F32)], axis=0)
    mod = _adaln(cond, w_ada[0], b_ada[0])

    xs = [x_prompt.reshape(bp * tp, d), x_sample.reshape(bs * tsq, d)]
    dims = [(bp, tp), (bs, tsq)]
    rows = [(0, False), (1, True)]
    ropes = [None, _rope_tables(tsq)]
    kpe_c = jnp.pad(cache_kpe[:, 0].reshape(bs * past, MLA_ROPE), ((0, 0), (0, LANES - MLA_ROPE)))
    caches = [None, _kvup(cache_ckv[:, 0].reshape(bs * past, -1), kpe_c, wts)]
    inits = [None, state_hgrn[:, 0]]

    x1s, h2s, affs, extras = [], [], [], []
    for gi in range(2):
        (bt, sq), (row0, per_batch) = dims[gi], rows[gi]
        if sq == TOK_BLOCK and caches[gi] is None and ropes[gi] is None and not per_batch:
            x1, h2, aff, ckv, kpe, s_fin = _mixer(xs[gi], bt, sq, mod, hgrn_gamma, wts, alpha, ne, row0)
        else:
            q, k, v, ckv, kpe, hgx = _inproj(xs[gi], bt, sq, mod, hgrn_gamma, wts, row0, per_batch, ropes[gi])
            o_mla = _attn(q, k, v, bt, sq, caches[gi])
            o_f, o_b, s_fin = _hgrn(hgx, bt, sq, inits[gi])
            x1, h2, aff = _postmix(xs[gi], bt, sq, mod, o_f, o_b, hgx, o_mla, wts, alpha, ne, row0, per_batch)
        x1s.append(x1)
        h2s.append(h2)
        affs.append(aff)
        extras.append((ckv, kpe, s_fin))

    caps = [EC_FACTOR * x.shape[0] // ne for x in xs]
    slots = sum(caps)
    assert all(cp % BF16_ROWS == 0 for cp in caps) and slots >= WIN_ROWS
    groups, scheds = [], []
    slot0 = 0
    for gi in range(2):
        rank, cnt = _route(affs[gi], caps[gi])
        groups.append(dict(h2=h2s[gi], rank=rank, aff=affs[gi], slot0=slot0))
        scheds.append(_window_sched(cnt, slot0))
        slot0 += caps[gi]
    xe, ge = _compact(groups, *[jnp.concatenate([s[k] for s in scheds]) for k in range(3)], slots)
    ye = _ffn(xe, ge, moe_w1[0], moe_w3[0], moe_w2[0], slots, ft=512)

    outs = []
    for gi in range(2):
        outs.append(_combine(ye, groups[gi]["rank"], scheds[gi][0], scheds[gi][2], x1s[gi], mod,
                             ln2_g[0].reshape(1, -1), ln2_b[0].reshape(1, -1), alpha, groups[gi]["slot0"],
                             rows[gi][0], dims[gi][1] // TOK_BLOCK if rows[gi][1] else 1 << 30))

    ckv_p, kpe_p, st_p = extras[0]
    y_prompt = outs[0].reshape(bp, tp, d)
    y_sample = outs[1].reshape(bs, tsq, d)
    new_ckv = ckv_p.reshape(bp, 1, tp, -1)
    new_kpe = kpe_p[:, :MLA_ROPE].reshape(bp, 1, tp, MLA_ROPE)
    new_state = st_p.reshape(bp, 1, 2, HG_HEADS, HG_DK, HG_DV)
    return (y_prompt, y_sample, new_ckv, new_kpe, new_state)
```

```python
import functools

import jax
import jax.numpy as jnp
import numpy as np
from jax import lax
from jax.experimental import pallas as pl
from jax.experimental.pallas import tpu as pltpu

F32 = jnp.float32
BF16 = jnp.bfloat16

MLA_HEADS = 8
MLA_NOPE = 64
MLA_ROPE = 32
MLA_V = 64
HG_HEADS = 4
HG_DK = 128
HG_DV = 128
HG_CHUNK = 32
GRID_W = 64
ROPE_BASE = 10000.0
EC_FACTOR = 2
EPS = 1e-6

LANES = 128
SUBLANES = 8
BF16_ROWS = 16
VMEM_LIMIT = 56 * 1024 * 1024

TOK_BLOCK = 256
WIN_ROWS = 64
UNSELECTED = -(1 << 30)

NT_DIMS = (((1,), (1,)), ((), ()))


def _dot(a, b):
    return jnp.dot(a, b, preferred_element_type=F32)


def _dot_nt(a, b):
    return lax.dot_general(a, b, NT_DIMS, preferred_element_type=F32)


def _silu(x):
    return x * jax.nn.sigmoid(x)


def _params(*sem):
    return pltpu.CompilerParams(dimension_semantics=sem, vmem_limit_bytes=VMEM_LIMIT)


def _const_spec(shape):
    zeros = (0,) * len(shape)
    return pl.BlockSpec(shape, lambda *_: zeros, pipeline_mode=pl.Buffered(1))


def _adaln_kernel(c_ref, w_ref, b_ref, o_ref):
    s = _silu(c_ref[...]).astype(BF16)
    o_ref[...] = _dot(s, w_ref[...].astype(BF16)) + b_ref[...]


def _adaln(cond, w_ada, b_ada):
    rows, d = cond.shape
    n = w_ada.shape[1]
    tn = n // 4
    return pl.pallas_call(
        _adaln_kernel,
        out_shape=jax.ShapeDtypeStruct((rows, n), F32),
        grid=(n // tn,),
        in_specs=[_const_spec((rows, d)),
                  pl.BlockSpec((d, tn), lambda j: (0, j)),
                  pl.BlockSpec((1, tn), lambda j: (0, j))],
        out_specs=pl.BlockSpec((rows, tn), lambda j: (0, j)),
        compiler_params=_params("arbitrary"),
        name="adaln",
    )(cond, w_ada, b_ada.reshape(1, n))


def _rms(x, g):
    return x * lax.rsqrt(jnp.mean(x * x, axis=-1, keepdims=True) + EPS) * g


def _rope(x, c, s):
    w = x.shape[-1]
    lane = lax.broadcasted_iota(jnp.int32, x.shape, 1)
    nxt = pltpu.roll(x, w - 1, 1)
    prv = pltpu.roll(x, 1, 1)
    return x * c + jnp.where(lane % 2 == 0, nxt, prv) * s


N_INPROJ_WEIGHTS = 6


def _mod_row(mod_ref, row0, per_batch):
    r = row0 + pl.program_id(0) if per_batch else row0
    return mod_ref[pl.ds(r, 1), :]


def _modulated(x_ref, m):
    d = x_ref.shape[1]
    return (x_ref[...] * (1.0 + m[:, d:2 * d]) + m[:, 0:d]).astype(BF16)


def _keys(k_nope, kpe):
    shared = pltpu.roll(kpe, MLA_NOPE, 1)
    return (k_nope + jnp.concatenate([shared] * MLA_HEADS, axis=1)).astype(BF16)


def _inproj_kernel(*refs, row0, per_batch, rope):
    x_ref, mod_ref = refs[:2]
    _inproj_body(x_ref, _mod_row(mod_ref, row0, per_batch), *refs[2:], rope=rope)


def _inproj_body(x_ref, m, *refs, rope):
    gam_ref, win_ref, qn_ref, wuq_ref, kvn_ref, wk_ref, wv_ref = refs[:1 + N_INPROJ_WEIGHTS]
    refs = refs[1 + N_INPROJ_WEIGHTS:]
    if rope:
        cq_ref, sq_ref, ck_ref, sk_ref = refs[:4]
        refs = refs[4:]
    q_o, k_o, v_o, ckv_o, kpe_o, hgx_o = refs
    h = _modulated(x_ref, m)
    hw = HG_HEADS * HG_DK
    o_kv = qn_ref.shape[1]
    o_pe = o_kv + kvn_ref.shape[1]
    o_h = o_pe + MLA_ROPE

    cq = _rms(_dot_nt(h, win_ref[0:o_kv, :]), qn_ref[...])
    q = _dot(cq.astype(BF16), wuq_ref[...])
    if rope:
        q = _rope(q, jnp.concatenate([cq_ref[...]] * MLA_HEADS, axis=1),
                  jnp.concatenate([sq_ref[...]] * MLA_HEADS, axis=1))
    q_o[...] = q.astype(BF16)

    ckv = _rms(_dot_nt(h, win_ref[o_kv:o_pe, :]), kvn_ref[...])
    ckv_o[...] = ckv
    kpe = _dot_nt(h, win_ref[o_pe:o_h, :])
    kpe = jnp.concatenate([kpe, jnp.zeros((kpe.shape[0], LANES - MLA_ROPE), F32)], axis=1)
    if rope:
        kpe = _rope(kpe, ck_ref[...], sk_ref[...])
    kpe_o[...] = kpe
    cb = ckv.astype(BF16)
    k_o[...] = _keys(_dot(cb, wk_ref[...]), kpe)
    v_o[...] = _dot(cb, wv_ref[...]).astype(BF16)

    z = _dot_nt(h, win_ref[o_h:o_h + 5 * hw, :])
    hgx_o[:, 0:hw] = _silu(z[:, 0:hw])
    for dr in range(2):
        g0, g1 = gam_ref[dr, 0:1, :], gam_ref[dr, 1:2, :]
        gmax = jnp.maximum(g0, g1)
        e0, e1 = jnp.exp(g0 - gmax), jnp.exp(g1 - gmax)
        lb = e0 / (e0 + e1)
        f = lb + (1.0 - lb) * jax.nn.sigmoid(z[:, (1 + dr) * hw:(2 + dr) * hw])
        hgx_o[:, (1 + 2 * dr) * hw:(2 + 2 * dr) * hw] = jnp.log(f)
        hgx_o[:, (2 + 2 * dr) * hw:(3 + 2 * dr) * hw] = 1.0 - f
    hgx_o[:, 5 * hw:6 * hw] = z[:, 3 * hw:4 * hw]
    hgx_o[:, 6 * hw:7 * hw] = z[:, 4 * hw:5 * hw]


def _inproj(x2d, batch, seq, mod, gamma, wts, row0, per_batch, rope_tabs):
    n, d = x2d.shape
    tm = TOK_BLOCK
    nblk = seq // tm
    rope = rope_tabs is not None
    hp = MLA_HEADS * LANES
    hw = HG_HEADS * HG_DK
    tok = lambda b, i: (b * nblk + i, 0)
    pos = lambda b, i: (i, 0)
    weights = [wts[k] for k in INPROJ_KEYS]
    ins = [x2d, mod, gamma] + weights
    in_specs = ([pl.BlockSpec((tm, d), tok), _const_spec(mod.shape), _const_spec(gamma.shape)]
                + [_const_spec(w.shape) for w in weights])
    if rope:
        ins += list(rope_tabs)
        in_specs += [pl.BlockSpec((tm, t.shape[1]), pos) for t in rope_tabs]
    widths = [(hp, BF16), (hp, BF16), (MLA_HEADS * MLA_V, BF16), (wts["kvn"].shape[1], F32), (LANES, F32),
              (7 * hw, F32)]
    return pl.pallas_call(
        functools.partial(_inproj_kernel, row0=row0, per_batch=per_batch, rope=rope),
        out_shape=[jax.ShapeDtypeStruct((n, w), dt) for w, dt in widths],
        grid=(batch, nblk),
        in_specs=in_specs,
        out_specs=[pl.BlockSpec((tm, w), tok) for w, _ in widths],
        compiler_params=_params("arbitrary", "arbitrary"),
        name="inproj",
    )(*ins)


def _kvup_kernel(ckv_ref, kpe_ref, wk_ref, wv_ref, k_o, v_o):
    cb = ckv_ref[...].astype(BF16)
    k_o[...] = _keys(_dot(cb, wk_ref[...]), kpe_ref[...])
    v_o[...] = _dot(cb, wv_ref[...]).astype(BF16)


def _kvup(ckv2d, kpe2d, wts):
    n = ckv2d.shape[0]
    tm = TOK_BLOCK
    widths = [MLA_HEADS * LANES, MLA_HEADS * MLA_V]
    row = lambda i: (i, 0)
    ws = [wts["wk"], wts["wv"]]
    return pl.pallas_call(
        _kvup_kernel,
        out_shape=[jax.ShapeDtypeStruct((n, w), BF16) for w in widths],
        grid=(n // tm,),
        in_specs=[pl.BlockSpec((tm, ckv2d.shape[1]), row), pl.BlockSpec((tm, LANES), row)]
                 + [_const_spec(w.shape) for w in ws],
        out_specs=[pl.BlockSpec((tm, w), row) for w in widths],
        compiler_params=_params("arbitrary"),
        name="kvup",
    )(ckv2d, kpe2d, *ws)


ATTN_SCALE = (MLA_NOPE + MLA_ROPE) ** -0.5


def _attn_kernel(*refs, cached):
    if cached:
        q_ref, k_ref, v_ref, kc_ref, vc_ref, o_ref = refs
    else:
        q_ref, k_ref, v_ref, o_ref = refs
        kc_ref = vc_ref = None
    _attn_body(q_ref, k_ref, v_ref, kc_ref, vc_ref, o_ref)


def _attn_body(q_ref, k_ref, v_ref, kc_ref, vc_ref, o_ref):
    cached = kc_ref is not None
    scale = ATTN_SCALE
    per_slab = LANES // MLA_V
    own = lax.broadcasted_iota(jnp.int32, (q_ref.shape[0], LANES), 1) // MLA_V
    for slab in range(MLA_HEADS // per_slab):
        vsl = slice(slab * LANES, (slab + 1) * LANES)
        out = None
        for sub in range(per_slab):
            hd = slab * per_slab + sub
            sl = slice(hd * LANES, (hd + 1) * LANES)
            q = q_ref[:, sl]
            s = _dot_nt(q, k_ref[:, sl]) * scale
            mx = jnp.max(s, axis=-1, keepdims=True)
            if cached:
                s2 = _dot_nt(q, kc_ref[:, sl]) * scale
                mx = jnp.maximum(mx, jnp.max(s2, axis=-1, keepdims=True))
            e = jnp.exp(s - mx)
            den = jnp.sum(e, axis=-1, keepdims=True)
            o = _dot(e.astype(BF16), v_ref[:, vsl])
            if cached:
                e2 = jnp.exp(s2 - mx)
                den = den + jnp.sum(e2, axis=-1, keepdims=True)
                o = o + _dot(e2.astype(BF16), vc_ref[:, vsl])
            o = o / den
            out = o if out is None else jnp.where(own == sub, o, out)
        o_ref[:, vsl] = out.astype(o_ref.dtype)


def _attn(q, k, v, batch, seq, cache=None):
    n, hp = q.shape
    hv = v.shape[1]
    tq = TOK_BLOCK
    nblk = seq // tq
    ins = [q, k, v]
    in_specs = [pl.BlockSpec((tq, hp), lambda b, i: (b * nblk + i, 0)),
                pl.BlockSpec((seq, hp), lambda b, i: (b, 0)),
                pl.BlockSpec((seq, hv), lambda b, i: (b, 0))]
    if cache is not None:
        past = cache[0].shape[0] // batch
        ins += list(cache)
        in_specs += [pl.BlockSpec((past, hp), lambda b, i: (b, 0)),
                     pl.BlockSpec((past, hv), lambda b, i: (b, 0))]
    return pl.pallas_call(
        functools.partial(_attn_kernel, cached=cache is not None),
        out_shape=jax.ShapeDtypeStruct((n, hv), BF16),
        grid=(batch, nblk),
        in_specs=in_specs,
        out_specs=pl.BlockSpec((tq, hv), lambda b, i: (b * nblk + i, 0)),
        compiler_params=_params("arbitrary", "arbitrary"),
        name="attn",
    )(*ins)


def _chunk_scan(x, reverse):
    tm = x.shape[0]
    rin = lax.broadcasted_iota(jnp.int32, x.shape, 0) % HG_CHUNK
    step = 1
    while step < HG_CHUNK:
        if reverse:
            x = x + jnp.where(rin < HG_CHUNK - step, pltpu.roll(x, tm - step, 0), 0.0)
        else:
            x = x + jnp.where(rin >= step, pltpu.roll(x, step, 0), 0.0)
        step *= 2
    return x


def _hgrn_kernel(*refs, has_init):
    fwd, bwd = refs[0:4], refs[4:8]
    refs = refs[8:]
    s0_ref = None
    if has_init:
        s0_ref = refs[0]
        refs = refs[1:]
    of_ref, ob_ref, sfin_ref, st_scr = refs
    i = pl.program_id(1)
    _hgrn_body(fwd, bwd, s0_ref, of_ref, ob_ref, sfin_ref, st_scr, i == 0, i == pl.num_programs(1) - 1)


def _hgrn_body(fwd, bwd, s0_ref, of_ref, ob_ref, sfin_ref, st_scr, first, last):
    tm = fwd[0].shape[0]
    c = HG_CHUNK
    nch = tm // c
    dk, dv = HG_DK, HG_DV
    hw = HG_HEADS * dk

    def initial(dr, hd):
        return s0_ref[0, dr, hd].T if s0_ref is not None else jnp.zeros((dv, dk), F32)

    if st_scr is not None:
        @pl.when(first)
        def _init():
            for dr in range(2):
                for hd in range(HG_HEADS):
                    st_scr[dr, hd] = initial(dr, hd)

    npair = nch // 2
    pair = 2 * c
    row = lax.broadcasted_iota(jnp.int32, (tm, tm), 0)
    col = lax.broadcasted_iota(jnp.int32, (tm, tm), 1)
    same = (row // c) == (col // c)
    same_pair = (row // pair) == (col // pair)
    bd = (lax.broadcasted_iota(jnp.int32, (tm, npair * dk), 0) // pair
          == lax.broadcasted_iota(jnp.int32, (tm, npair * dk), 1) // dk)
    chunk_odd = (lax.broadcasted_iota(jnp.int32, (tm, hw), 0) // c) % 2 == 1

    for dr, (hq_ref, lf_ref, kk_ref, vv_ref) in enumerate((fwd, bwd)):
        o_ref = of_ref if dr == 0 else ob_ref
        tri = same & ((col <= row) if dr == 0 else (col >= row))
        cross = same_pair & (((row // c) > (col // c)) if dr == 0 else ((row // c) < (col // c)))
        second = chunk_odd if dr == 0 else ~chunk_odd
        bcum = _chunk_scan(lf_ref[...], reverse=dr == 1)
        closing = c - 1 if dr == 0 else 0
        btot3 = bcum.reshape(nch, c, hw)[:, closing:closing + 1, :]
        btot = jnp.broadcast_to(btot3, (nch, c, hw)).reshape(tm, hw)
        bpart = jnp.where(chunk_odd, pltpu.roll(btot, c, 0), pltpu.roll(btot, tm - c, 0))
        epart = jnp.exp(bpart)
        bpair = btot + bpart
        kk = kk_ref[...]
        qd = hq_ref[...] * jnp.exp(bcum)
        kd = kk * jnp.exp(-bcum)
        ke = kk * jnp.exp(btot - bcum)
        qd2 = jnp.where(second, qd * epart, qd)
        ke2 = jnp.where(second, ke, ke * epart)
        vv = vv_ref[...]
        order = range(npair) if dr == 0 else range(npair - 1, -1, -1)
        for hd in range(HG_HEADS):
            sl = slice(hd * dk, (hd + 1) * dk)
            qd_h = qd[:, sl].astype(BF16)
            v_h = vv[:, hd * dv:(hd + 1) * dv]
            a = jnp.where(tri, _dot_nt(qd_h, kd[:, sl].astype(BF16)),
                          jnp.where(cross, _dot_nt(qd_h, ke[:, sl].astype(BF16)), 0.0))
            o_intra = _dot(a.astype(BF16), v_h.astype(BF16))
            kebd = jnp.where(bd, jnp.concatenate([ke2[:, sl]] * npair, axis=1), 0.0).astype(BF16)
            qbd = jnp.where(bd, jnp.concatenate([qd2[:, sl]] * npair, axis=1), 0.0).astype(BF16)
            ut = _dot(v_h.T.astype(BF16), kebd)
            st = st_scr[dr, hd] if st_scr is not None else initial(dr, hd)
            prev = [None] * npair
            for p in order:
                prev[p] = st
                st = st * jnp.exp(bpair[p * pair:p * pair + 1, sl]) + ut[:, p * dk:(p + 1) * dk]
            if st_scr is not None:
                st_scr[dr, hd] = st
            o_inter = _dot_nt(qbd, jnp.concatenate(prev, axis=1).astype(BF16))
            o_ref[:, hd * dv:(hd + 1) * dv] = o_intra + o_inter

            if last is True:
                sfin_ref[0, dr, hd] = st.T
            else:
                @pl.when(last)
                def _final(st=st, dr=dr, hd=hd):
                    sfin_ref[0, dr, hd] = st.T


def _hgrn(hgx, batch, seq, s0=None):
    n = hgx.shape[0]
    tm = TOK_BLOCK
    nblk = seq // tm
    hw = HG_HEADS * HG_DK

    def spec(lane_blk, rev):
        if rev:
            return pl.BlockSpec((tm, hw), lambda b, i: (b * nblk + nblk - 1 - i, lane_blk))
        return pl.BlockSpec((tm, hw), lambda b, i: (b * nblk + i, lane_blk))

    in_specs = [spec(0, False), spec(1, False), spec(2, False), spec(5, False),
                spec(0, True), spec(3, True), spec(4, True), spec(5, True)]
    ins = [hgx] * 8
    st_shape = (1, 2, HG_HEADS, HG_DK, HG_DV)
    st_spec = pl.BlockSpec(st_shape, lambda b, i: (b, 0, 0, 0, 0))
    if s0 is not None:
        ins.append(s0)
        in_specs.append(st_spec)
    return pl.pallas_call(
        functools.partial(_hgrn_kernel, has_init=s0 is not None),
        out_shape=[jax.ShapeDtypeStruct((n, hw), F32), jax.ShapeDtypeStruct((n, hw), F32),
                   jax.ShapeDtypeStruct((batch,) + st_shape[1:], F32)],
        grid=(batch, nblk),
        in_specs=in_specs,
        out_specs=[spec(0, False), spec(0, True), st_spec],
        scratch_shapes=[pltpu.VMEM((2, HG_HEADS, HG_DV, HG_DK), F32)],
        compiler_params=_params("arbitrary", "arbitrary"),
        name="hgrn",
    )(*ins)


def _layer_norm(x, g, b):
    xc = x - jnp.mean(x, axis=-1, keepdims=True)
    var = jnp.mean(xc * xc, axis=-1, keepdims=True)
    return xc * lax.rsqrt(var + EPS) * g + b


N_POSTMIX_WEIGHTS = 8
INPROJ_KEYS = ("win", "qn", "wuq", "kvn", "wk", "wv")
POSTMIX_KEYS = ("hgn", "womla", "wohg", "wout", "ln1g", "ln1b", "wr")
MIXER_SEQS = 2


def _postmix_kernel(x_ref, mod_ref, *refs, alpha, row0, per_batch):
    _postmix_body(x_ref, _mod_row(mod_ref, row0, per_batch), *refs, alpha=alpha)


def _postmix_body(x_ref, m, of_ref, ob_ref, zg_ref, om_ref, wg_ref, hgn_ref, womla_ref,
                  wohg_ref, wout_ref, lng_ref, lnb_ref, wr_ref, x1_o, h2_o, aff_o, *, alpha):
    d = x_ref.shape[1]
    tb = aff_o.shape[2]
    g1, sh2, sc2 = m[:, 2 * d:3 * d], m[:, 3 * d:4 * d], m[:, 4 * d:5 * d]
    o = of_ref[...] + ob_ref[...]
    zg = zg_ref[...]
    parts = []
    for hd in range(HG_HEADS):
        sl = slice(hd * HG_DV, (hd + 1) * HG_DV)
        parts.append(_rms(o[:, sl], hgn_ref[...]) * _silu(zg[:, sl]))
    ohg = jnp.concatenate(parts, axis=1).astype(BF16)
    gates = _dot_nt(_modulated(x_ref, m), wg_ref[wg_ref.shape[0] - 2 * d:, :])
    merged = (jax.nn.sigmoid(gates[:, 0:d]) * _dot(om_ref[...], womla_ref[...])
              + jax.nn.sigmoid(gates[:, d:2 * d]) * _dot(ohg, wohg_ref[...]))
    mix = _dot(merged.astype(BF16), wout_ref[...])
    x1 = _layer_norm(alpha * x_ref[...] + g1 * mix, lng_ref[...], lnb_ref[...])
    x1_o[...] = x1
    h2 = (x1 * (1.0 + sc2) + sh2).astype(BF16)
    h2_o[...] = h2
    logits = _dot_nt(wr_ref[...], h2)
    e = jnp.exp(logits - jnp.max(logits, axis=0, keepdims=True))
    aff = e / jnp.sum(e, axis=0, keepdims=True)
    for blk in range(aff_o.shape[0]):
        aff_o[blk] = aff[:, blk * tb:(blk + 1) * tb]


def _postmix(x2d, batch, seq, mod, o_f, o_b, hgx, o_mla, wts, alpha, n_experts, row0, per_batch):
    n, d = x2d.shape
    tm = TOK_BLOCK
    nblk = seq // tm
    hw = HG_HEADS * HG_DV
    tok = lambda b, i: (b * nblk + i, 0)
    weights = [wts[k] for k in POSTMIX_KEYS]
    return pl.pallas_call(
        functools.partial(_postmix_kernel, alpha=alpha, row0=row0, per_batch=per_batch),
        out_shape=[jax.ShapeDtypeStruct((n, d), F32), jax.ShapeDtypeStruct((n, d), BF16),
                   jax.ShapeDtypeStruct((n // tm, n_experts, tm), F32)],
        grid=(batch, nblk),
        in_specs=[pl.BlockSpec((tm, d), tok), _const_spec(mod.shape),
                  pl.BlockSpec((tm, hw), tok), pl.BlockSpec((tm, hw), tok),
                  pl.BlockSpec((tm, hw), lambda b, i: (b * nblk + i, 6)),
                  pl.BlockSpec((tm, o_mla.shape[1]), tok), _const_spec(wts["win"].shape)]
                 + [_const_spec(w.shape) for w in weights],
        out_specs=[pl.BlockSpec((tm, d), tok), pl.BlockSpec((tm, d), tok),
                   pl.BlockSpec((1, n_experts, tm), lambda b, i: (b * nblk + i, 0, 0))],
        compiler_params=_params("arbitrary", "arbitrary"),
        name="postmix",
    )(x2d, mod, o_f, o_b, hgx, o_mla, wts["win"], *weights)


def _mixer_kernel(x_ref, mod_ref, *refs, alpha, row0, seq):
    nw = 1 + N_INPROJ_WEIGHTS
    in_w, refs = refs[:nw], refs[nw:]
    pm_w, refs = refs[:N_POSTMIX_WEIGHTS - 1], refs[N_POSTMIX_WEIGHTS - 1:]
    pm_w = (in_w[1],) + tuple(pm_w)
    x1_o, h2_o, aff_o, ckv_o, kpe_o, sfin_o, q_s, k_s, v_s, hgx_s, om_s, of_s, ob_s = refs
    m = _mod_row(mod_ref, row0, False)
    _inproj_body(x_ref, m, *in_w, q_s, k_s, v_s, ckv_o, kpe_o, hgx_s, rope=False)
    hw = HG_HEADS * HG_DK
    for s in range(x_ref.shape[0] // seq):
        rows = slice(s * seq, (s + 1) * seq)
        _attn_body(q_s.at[rows], k_s.at[rows], v_s.at[rows], None, None, om_s.at[rows])
        lane = lambda j: hgx_s.at[rows, j * hw:(j + 1) * hw]
        _hgrn_body((lane(0), lane(1), lane(2), lane(5)), (lane(0), lane(3), lane(4), lane(5)), None,
                   of_s.at[rows], ob_s.at[rows], sfin_o.at[s:s + 1], None, True, True)
    _postmix_body(x_ref, m, of_s, ob_s, hgx_s.at[:, 6 * hw:7 * hw], om_s, *pm_w, x1_o, h2_o, aff_o,
                  alpha=alpha)


def _mixer(x2d, batch, seq, mod, gamma, wts, alpha, n_experts, row0):
    n, d = x2d.shape
    assert seq == TOK_BLOCK
    ns = MIXER_SEQS if batch % MIXER_SEQS == 0 else 1
    tm = ns * seq
    hp = MLA_HEADS * LANES
    hv = MLA_HEADS * MLA_V
    hw = HG_HEADS * HG_DK
    kvl = wts["kvn"].shape[1]
    weights = [wts[k] for k in INPROJ_KEYS + POSTMIX_KEYS]
    tok = lambda b: (b, 0)
    st_shape = (ns, 2, HG_HEADS, HG_DK, HG_DV)
    return pl.pallas_call(
        functools.partial(_mixer_kernel, alpha=alpha, row0=row0, seq=seq),
        out_shape=[jax.ShapeDtypeStruct((n, d), F32), jax.ShapeDtypeStruct((n, d), BF16),
                   jax.ShapeDtypeStruct((n // seq, n_experts, seq), F32),
                   jax.ShapeDtypeStruct((n, kvl), F32), jax.ShapeDtypeStruct((n, LANES), F32),
                   jax.ShapeDtypeStruct((batch,) + st_shape[1:], F32)],
        grid=(batch // ns,),
        in_specs=[pl.BlockSpec((tm, d), tok), _const_spec(mod.shape), _const_spec(gamma.shape)]
                 + [_const_spec(w.shape) for w in weights],
        out_specs=[pl.BlockSpec((tm, d), tok), pl.BlockSpec((tm, d), tok),
                   pl.BlockSpec((ns, n_experts, seq), lambda b: (b, 0, 0)),
                   pl.BlockSpec((tm, kvl), tok), pl.BlockSpec((tm, LANES), tok),
                   pl.BlockSpec(st_shape, lambda b: (b, 0, 0, 0, 0))],
        scratch_shapes=[pltpu.VMEM((tm, hp), BF16), pltpu.VMEM((tm, hp), BF16), pltpu.VMEM((tm, hv), BF16),
                        pltpu.VMEM((tm, 7 * hw), F32), pltpu.VMEM((tm, hv), BF16),
                        pltpu.VMEM((tm, hw), F32), pltpu.VMEM((tm, hw), F32)],
        compiler_params=_params("arbitrary"),
        name="mixer",
    )(x2d, mod, gamma, *weights)


def _route_kernel(aff_ref, rank_o, cnt_o, *, cap):
    nb, ne, tb = aff_ref.shape
    key = aff_ref[...]

    def count(mask):
        return jnp.sum(jnp.sum(jnp.where(mask, 1.0, 0.0), axis=0), axis=1, keepdims=True)

    def bit_step(it, bits):
        cand = bits | jnp.left_shift(jnp.int32(1), 30 - it)
        return jnp.where(count(key >= pltpu.bitcast(cand, F32)[None]) >= cap, cand, bits)

    bits = lax.fori_loop(0, 31, bit_step, jnp.zeros((ne, 1), jnp.int32))
    thr = pltpu.bitcast(bits, F32)
    need = cap - count(key > thr[None])
    before = (lax.broadcasted_iota(jnp.int32, (tb, tb), 0)
              < lax.broadcasted_iota(jnp.int32, (tb, tb), 1))
    before = jnp.where(before, 1.0, 0.0).astype(BF16)
    off_eq = jnp.zeros((ne, 1), F32)
    off_sel = jnp.zeros((ne, 1), F32)
    cnt_o[...] = jnp.zeros_like(cnt_o)
    for blk in range(nb):
        key_b = key[blk]
        eq = key_b == thr
        eq_b = jnp.where(eq, 1.0, 0.0)
        eq_rank = _dot(eq_b.astype(BF16), before) + off_eq
        sel = (key_b > thr) | (eq & (eq_rank < need))
        sel_b = jnp.where(sel, 1.0, 0.0)
        rank = _dot(sel_b.astype(BF16), before) + off_sel
        rank_o[blk] = jnp.where(sel, rank.astype(jnp.int32), UNSELECTED)
        cnt_o[:, blk:blk + 1] = off_sel.astype(jnp.int32)
        off_eq = off_eq + jnp.sum(eq_b, axis=1, keepdims=True)
        off_sel = off_sel + jnp.sum(sel_b, axis=1, keepdims=True)
    cnt_o[:, nb:nb + 1] = off_sel.astype(jnp.int32)


def _route(aff, cap):
    nb, ne, tb = aff.shape
    assert nb + 1 <= LANES
    rank, cnt = pl.pallas_call(
        functools.partial(_route_kernel, cap=cap),
        out_shape=[jax.ShapeDtypeStruct(aff.shape, jnp.int32), jax.ShapeDtypeStruct((ne, LANES), jnp.int32)],
        in_specs=[pl.BlockSpec(memory_space=pltpu.VMEM)],
        out_specs=[pl.BlockSpec(memory_space=pltpu.VMEM)] * 2,
        compiler_params=pltpu.CompilerParams(vmem_limit_bytes=VMEM_LIMIT),
        name="route",
    )(aff)
    return rank, cnt[:, :nb + 1]


def _window_hits(rk_ref, firsts, slot0, win):
    ne, tb = rk_ref.shape[1], rk_ref.shape[2]
    win_iota = lax.broadcasted_iota(jnp.int32, (win, tb), 0)
    return [(rk_ref[0, e:e + 1, :] + (slot0 - firsts[e])) == win_iota for e in range(ne)]


def _compact_kernel(first_ref, end_ref, rounds_ref, *refs, groups, slots):
    ng = len(groups)
    h2_refs, rk_refs, af_refs = refs[0:ng], refs[ng:2 * ng], refs[2 * ng:3 * ng]
    hbms, refs = refs[3 * ng:3 * ng + 2], refs[3 * ng + 2:]
    stages, tails = refs[0:2], refs[2:4]
    sem, issued = refs[4:]
    b = pl.program_id(0)
    ne = rk_refs[0].shape[1]
    win = WIN_ROWS
    sub = BF16_ROWS

    def copies(slot, dsts):
        return [pltpu.make_async_copy(stage.at[slot, pl.ds(e * win, win), :],
                                      hbm.at[e, pl.ds(pl.multiple_of(dsts[e], sub), win), :], sem.at[c, e])
                for c, (stage, hbm) in enumerate(zip(stages, hbms)) for e in range(ne)]

    def wait_previous():
        @pl.when(issued[0] > 0)
        def _():
            for cp in copies(0, [0] * ne):
                cp.wait()

    @pl.when(b == 0)
    def _init():
        issued[0] = 0
        for stage, tail in zip(stages, tails):
            tail[...] = jnp.zeros_like(tail)
            stage[1] = jnp.zeros(stage.shape[1:], stage.dtype)
        pad = copies(1, [slots] * ne)
        for cp in pad:
            cp.start()
        for cp in pad:
            cp.wait()

    def group_body(h2_ref, rk_ref, af_ref, slot0):
        firsts = [first_ref[b * ne + e] for e in range(ne)]
        bases = [(f // sub) * sub for f in firsts]
        ends = [end_ref[b * ne + e] - bases[e] for e in range(ne)]

        def one_round(r, carry):
            dsts = [bases[e] + r * win for e in range(ne)]
            hits = _window_hits(rk_ref, dsts, slot0, win)
            onehot = jnp.where(jnp.concatenate(hits, axis=0), 1.0, 0.0).astype(BF16)
            gate = jnp.concatenate(
                [jnp.sum(jnp.where(hits[e], af_ref[0, e:e + 1, :], 0.0), axis=1, keepdims=True)
                 for e in range(ne)], axis=0)
            slot = issued[0] % 2
            stages[0][slot] = _dot(onehot, h2_ref[...]).astype(BF16)
            stages[1][slot] = jnp.broadcast_to(gate, (ne * win, LANES))
            @pl.when(r == 0)
            def _heads():
                for stage, tail in zip(stages, tails):
                    sub_iota = lax.broadcasted_iota(jnp.int32, (sub, stage.shape[2]), 0)
                    for e in range(ne):
                        head = stage[slot, e * win:e * win + sub, :]
                        stage[slot, e * win:e * win + sub, :] = jnp.where(
                            sub_iota < firsts[e] - bases[e], tail[e * sub:(e + 1) * sub, :], head)

            for stage, tail in zip(stages, tails):
                for e in range(ne):
                    last = (ends[e] // sub) * sub
                    group = stage[slot, pl.ds(pl.multiple_of(e * win + last % win, sub), sub), :]
                    tail[e * sub:(e + 1) * sub, :] = jnp.where(r == last // win, group,
                                                               tail[e * sub:(e + 1) * sub, :])
            wait_previous()
            for cp in copies(slot, [jnp.minimum(dst, slots) for dst in dsts]):
                cp.start()
            issued[0] = issued[0] + 1
            return carry

        lax.fori_loop(0, rounds_ref[b], one_round, 0)

    blk0 = 0
    for gi, g in enumerate(groups):
        @pl.when((b >= blk0) & (b < blk0 + g["nb"]))
        def _(gi=gi, g=g):
            group_body(h2_refs[gi], rk_refs[gi], af_refs[gi], g["slot0"])
        blk0 += g["nb"]

    @pl.when(b == pl.num_programs(0) - 1)
    def _drain():
        wait_previous()


def _compact(groups, first, end, rounds, slots):
    d = groups[0]["h2"].shape[1]
    nbs = [g["rank"].shape[0] for g in groups]
    ne, tb = groups[0]["rank"].shape[1:]
    meta, specs_h2, specs_rk = [], [], []
    blk0 = 0
    for g, nb in zip(groups, nbs):
        meta.append(dict(nb=nb, slot0=g["slot0"]))
        local = lambda b, *_, blk0=blk0, nb=nb: jnp.clip(b - blk0, 0, nb - 1)
        specs_h2.append(pl.BlockSpec((tb, d), lambda b, *_, local=local: (local(b), 0)))
        specs_rk.append(pl.BlockSpec((1, ne, tb), lambda b, *_, local=local: (local(b), 0, 0)))
        blk0 += nb
    streams = [(d, BF16), (LANES, F32)]
    return pl.pallas_call(
        functools.partial(_compact_kernel, groups=meta, slots=slots),
        out_shape=[jax.ShapeDtypeStruct((ne, slots + WIN_ROWS, w), dt) for w, dt in streams],
        grid_spec=pltpu.PrefetchScalarGridSpec(
            num_scalar_prefetch=3,
            grid=(sum(nbs),),
            in_specs=specs_h2 + specs_rk + specs_rk,
            out_specs=[pl.BlockSpec(memory_space=pl.ANY)] * 2,
            scratch_shapes=[pltpu.VMEM((2, ne * WIN_ROWS, w), dt) for w, dt in streams]
                           + [pltpu.VMEM((ne * BF16_ROWS, w), dt) for w, dt in streams]
                           + [pltpu.SemaphoreType.DMA((2, ne)), pltpu.SMEM((1,), jnp.int32)]),
        compiler_params=_params("arbitrary"),
        name="compact",
    )(first, end, rounds, *[g["h2"] for g in groups], *[g["rank"] for g in groups],
      *[g["aff"] for g in groups])


def _ffn_kernel(xe_ref, ge_ref, w1_ref, w3_ref, w2_ref, ye_ref, *scratch):
    f = pl.program_id(1)
    x = xe_ref[0]
    hid = _silu(_dot(x, w1_ref[0].astype(BF16))) * _dot(x, w3_ref[0].astype(BF16))
    y = _dot(hid.astype(BF16), w2_ref[0].astype(BF16))
    if not scratch:
        ye_ref[0] = (y * ge_ref[0, :, 0:1]).astype(ye_ref.dtype)
        return
    acc_scr, = scratch
    last = pl.num_programs(1) - 1

    @pl.when(f == 0)
    def _first():
        acc_scr[...] = y

    @pl.when((f > 0) & (f < last))
    def _middle():
        acc_scr[...] += y

    @pl.when(f == last)
    def _last():
        ye_ref[0] = ((acc_scr[...] + y) * ge_ref[0, :, 0:1]).astype(ye_ref.dtype)


def _ffn(xe, ge, w1, w3, w2, slots, ft):
    ne, d, dff = w1.shape
    nf = dff // ft
    return pl.pallas_call(
        _ffn_kernel,
        out_shape=jax.ShapeDtypeStruct((ne, slots, d), BF16),
        grid=(ne, nf),
        in_specs=[pl.BlockSpec((1, slots, d), lambda e, f: (e, 0, 0)),
                  pl.BlockSpec((1, slots, ge.shape[2]), lambda e, f: (e, 0, 0)),
                  pl.BlockSpec((1, d, ft), lambda e, f: (e, 0, f)),
                  pl.BlockSpec((1, d, ft), lambda e, f: (e, 0, f)),
                  pl.BlockSpec((1, ft, d), lambda e, f: (e, f, 0))],
        out_specs=pl.BlockSpec((1, slots, d), lambda e, f: (e, 0, 0)),
        scratch_shapes=[pltpu.VMEM((slots, d), F32)] if nf > 1 else [],
        compiler_params=_params("arbitrary", "arbitrary"),
        name="ffn",
    )(xe, ge, w1, w3, w2)


def _combine_kernel(first_ref, rounds_ref, rk_ref, x1_ref, mod_ref, lng_ref, lnb_ref, ye_hbm, out_ref,
                    buf, acc_scr, sem, *, d_model, alpha, slot0, slots, row0, blocks_per_batch):
    d = d_model
    b = pl.program_id(0)
    nblk = pl.num_programs(0)
    ne, tb = rk_ref.shape[1], rk_ref.shape[2]
    win = WIN_ROWS
    eye = (lax.broadcasted_iota(jnp.int32, (tb, tb), 0)
           == lax.broadcasted_iota(jnp.int32, (tb, tb), 1))
    eye = jnp.where(eye, 1.0, 0.0).astype(BF16)

    def starts_of(blk, r):
        firsts = [(first_ref[blk * ne + e] // BF16_ROWS) * BF16_ROWS + r * win for e in range(ne)]
        return firsts, [jnp.minimum(f, slots - win) for f in firsts]

    def windows(slot, starts):
        return [pltpu.make_async_copy(ye_hbm.at[e, pl.ds(pl.multiple_of(starts[e], BF16_ROWS), win), :],
                                      buf.at[slot, pl.ds(e * win, win), :], sem.at[slot, e])
                for e in range(ne)]

    def scatter(slot, firsts, starts):
        hits = _window_hits(rk_ref, starts, slot0, win)
        hits = [h & ((rk_ref[0, e:e + 1, :] + slot0) >= firsts[e]) for e, h in enumerate(hits)]
        hit = jnp.where(jnp.concatenate(hits, axis=0), 1.0, 0.0).astype(BF16)
        hit_t = _dot_nt(eye, hit).astype(BF16)
        return _dot(hit_t, buf[slot])

    cur = b % 2

    @pl.when(b == 0)
    def _prime():
        for cp in windows(0, starts_of(0, 0)[1]):
            cp.start()

    @pl.when(b + 1 < nblk)
    def _prefetch():
        for cp in windows(1 - cur, starts_of(b + 1, 0)[1]):
            cp.start()

    firsts, starts = starts_of(b, 0)
    for cp in windows(cur, starts):
        cp.wait()
    acc_scr[...] = scatter(cur, firsts, starts)

    def extra_round(r, carry):
        firsts, starts = starts_of(b, r)
        for cp in windows(2, starts):
            cp.start()
        for cp in windows(2, starts):
            cp.wait()
        acc_scr[...] += scatter(2, firsts, starts)
        return carry

    lax.fori_loop(1, rounds_ref[b], extra_round, 0)
    r = row0 + b // blocks_per_batch
    g2 = mod_ref[pl.ds(r, 1), :][:, 5 * d:6 * d]
    out_ref[...] = _layer_norm(alpha * x1_ref[...] + g2 * acc_scr[...], lng_ref[...], lnb_ref[...])


def _combine(ye, rank, first, rounds, x1, mod, ln_g, ln_b, alpha, slot0, row0, blocks_per_batch):
    n, d = x1.shape
    nb, ne, tb = rank.shape
    slots = ye.shape[1]
    return pl.pallas_call(
        functools.partial(_combine_kernel, d_model=d, alpha=alpha, slot0=slot0, slots=slots, row0=row0,
                          blocks_per_batch=blocks_per_batch),
        out_shape=jax.ShapeDtypeStruct((n, d), F32),
        grid_spec=pltpu.PrefetchScalarGridSpec(
            num_scalar_prefetch=2,
            grid=(nb,),
            in_specs=[pl.BlockSpec((1, ne, tb), lambda b, *_: (b, 0, 0)),
                      pl.BlockSpec((tb, d), lambda b, *_: (b, 0)),
                      pl.BlockSpec(mod.shape, lambda b, *_: (0, 0)),
                      pl.BlockSpec((1, d), lambda b, *_: (0, 0)),
                      pl.BlockSpec((1, d), lambda b, *_: (0, 0)),
                      pl.BlockSpec(memory_space=pl.ANY)],
            out_specs=pl.BlockSpec((tb, d), lambda b, *_: (b, 0)),
            scratch_shapes=[pltpu.VMEM((3, ne * WIN_ROWS, d), ye.dtype), pltpu.VMEM((tb, d), F32),
                            pltpu.SemaphoreType.DMA((3, ne))]),
        compiler_params=_params("arbitrary"),
        name="combine",
    )(first, rounds, rank, x1, mod, ln_g, ln_b, ye)


def _prep_weights(w_in, q_norm, w_uq, kv_norm, w_ukv, w_o_mla, hgrn_norm, w_o_hg, w_out, ln1_g, ln1_b,
                  w_router):
    d = w_in.shape[0]
    q_lora, kv_lora = q_norm.shape[0], kv_norm.shape[0]
    hw = HG_HEADS * HG_DK
    hh, hp = MLA_HEADS, MLA_HEADS * LANES
    o_kv, o_pe = q_lora, q_lora + kv_lora
    o_h = o_pe + MLA_ROPE
    o_g = o_h + 5 * hw
    assert w_in.shape[1] == o_g + 2 * d
    qk = MLA_NOPE + MLA_ROPE
    kvw = MLA_NOPE + MLA_V
    b16 = lambda a: a.astype(BF16)
    assert all(o % BF16_ROWS == 0 for o in (o_kv, o_pe, o_h, o_g))
    win = b16(w_in.T)
    wuq = jnp.pad(w_uq.reshape(q_lora, hh, qk), ((0, 0), (0, 0), (0, LANES - qk))).reshape(q_lora, hp)
    ukv = w_ukv.reshape(kv_lora, hh, kvw)
    wk = jnp.pad(ukv[:, :, :MLA_NOPE], ((0, 0), (0, 0), (0, LANES - MLA_NOPE))).reshape(kv_lora, hp)
    wv = ukv[:, :, MLA_NOPE:].reshape(kv_lora, hh * MLA_V)
    return dict(
        win=win, qn=q_norm.reshape(1, -1), wuq=b16(wuq), kvn=kv_norm.reshape(1, -1), wk=b16(wk), wv=b16(wv),
        hgn=hgrn_norm.reshape(1, -1), womla=b16(w_o_mla), wohg=b16(w_o_hg), wout=b16(w_out),
        ln1g=ln1_g.reshape(1, -1), ln1b=ln1_b.reshape(1, -1), wr=b16(w_router.T))


def _rope_tables(seq):
    n_freq = MLA_ROPE // 4
    inv = ROPE_BASE ** (-np.arange(n_freq, dtype=np.float64) / n_freq)
    t = np.arange(seq)
    ang = np.concatenate([(t // GRID_W)[:, None] * inv, (t % GRID_W)[:, None] * inv], axis=-1)
    cos = np.repeat(np.cos(ang), 2, axis=1)
    sin = np.repeat(np.sin(ang), 2, axis=1) * np.tile([-1.0, 1.0], MLA_ROPE // 2)
    ck = np.pad(cos, ((0, 0), (0, LANES - MLA_ROPE)), constant_values=1.0)
    sk = np.pad(sin, ((0, 0), (0, LANES - MLA_ROPE)))
    cq = np.pad(cos, ((0, 0), (MLA_NOPE, LANES - MLA_NOPE - MLA_ROPE)), constant_values=1.0)
    sq = np.pad(sin, ((0, 0), (MLA_NOPE, LANES - MLA_NOPE - MLA_ROPE)))
    return tuple(jnp.asarray(a, F32) for a in (cq, sq, ck, sk))


def _window_sched(cnt, slot0):
    first = slot0 + cnt[:, :-1]
    end = slot0 + cnt[:, 1:]
    flat = lambda a: a.T.reshape(-1).astype(jnp.int32)
    rounds = jnp.max((end - (first // BF16_ROWS) * BF16_ROWS + WIN_ROWS - 1) // WIN_ROWS, axis=0)
    return flat(first), flat(end), jnp.maximum(rounds, 1).astype(jnp.int32)


def kernel(x_prompt, x_sample, c, cache_ckv, cache_kpe, state_hgrn, c_ctx, w_ada, b_ada, w_in, mla_q_norm, mla_w_uq, mla_kv_norm, mla_w_ukv, mla_w_o, hgrn_gamma, hgrn_norm, hgrn_w_o, w_out, ln1_g, ln1_b, moe_w_router, moe_w1, moe_w3, moe_w2, ln2_g, ln2_b):
    depth = w_ada.shape[0]
    assert depth == 1, "single trunk layer"
    bp, tp, d = x_prompt.shape
    bs, tsq, _ = x_sample.shape
    ne = moe_w_router.shape[-1]
    alpha = (2 * depth) ** 0.25
    past = cache_ckv.shape[2]
    assert tp % TOK_BLOCK == 0 and tsq % TOK_BLOCK == 0 and past % TOK_BLOCK == 0 and tsq % GRID_W == 0

    wts = _prep_weights(w_in[0], mla_q_norm[0], mla_w_uq[0], mla_kv_norm[0], mla_w_ukv[0], mla_w_o[0],
                        hgrn_norm[0], hgrn_w_o[0], w_out[0], ln1_g[0], ln1_b[0], moe_w_router[0])
    cond_rows = -(-(1 + bs) // SUBLANES) * SUBLANES
    cond = jnp.concatenate([c_ctx[None], c, jnp.zeros((cond_rows - 1 - bs, d), F32)], axis=0)
    mod = _adaln(cond, w_ada[0], b_ada[0])

    xs = [x_prompt.reshape(bp * tp, d), x_sample.reshape(bs * tsq, d)]
    dims = [(bp, tp), (bs, tsq)]
    rows = [(0, False), (1, True)]
    ropes = [None, _rope_tables(tsq)]
    kpe_c = jnp.pad(cache_kpe[:, 0].reshape(bs * past, MLA_ROPE), ((0, 0), (0, LANES - MLA_ROPE)))
    caches = [None, _kvup(cache_ckv[:, 0].reshape(bs * past, -1), kpe_c, wts)]
    inits = [None, state_hgrn[:, 0]]

    x1s, h2s, affs, extras = [], [], [], []
    for gi in range(2):
        (bt, sq), (row0, per_batch) = dims[gi], rows[gi]
        if sq == TOK_BLOCK and caches[gi] is None and ropes[gi] is None and not per_batch:
            x1, h2, aff, ckv, kpe, s_fin = _mixer(xs[gi], bt, sq, mod, hgrn_gamma, wts, alpha, ne, row0)
        else:
            q, k, v, ckv, kpe, hgx = _inproj(xs[gi], bt, sq, mod, hgrn_gamma, wts, row0, per_batch, ropes[gi])
            o_mla = _attn(q, k, v, bt, sq, caches[gi])
            o_f, o_b, s_fin = _hgrn(hgx, bt, sq, inits[gi])
            x1, h2, aff = _postmix(xs[gi], bt, sq, mod, o_f, o_b, hgx, o_mla, wts, alpha, ne, row0, per_batch)
        x1s.append(x1)
        h2s.append(h2)
        affs.append(aff)
        extras.append((ckv, kpe, s_fin))

    caps = [EC_FACTOR * x.shape[0] // ne for x in xs]
    slots = sum(caps)
    assert all(cp % BF16_ROWS == 0 for cp in caps) and slots >= WIN_ROWS
    groups, scheds = [], []
    slot0 = 0
    for gi in range(2):
        rank, cnt = _route(affs[gi], caps[gi])
        groups.append(dict(h2=h2s[gi], rank=rank, aff=affs[gi], slot0=slot0))
        scheds.append(_window_sched(cnt, slot0))
        slot0 += caps[gi]
    xe, ge = _compact(groups, *[jnp.concatenate([s[k] for s in scheds]) for k in range(3)], slots)
    ye = _ffn(xe, ge, moe_w1[0], moe_w3[0], moe_w2[0], slots, ft=moe_w1.shape[-1])

    outs = []
    for gi in range(2):
        outs.append(_combine(ye, groups[gi]["rank"], scheds[gi][0], scheds[gi][2], x1s[gi], mod,
                             ln2_g[0].reshape(1, -1), ln2_b[0].reshape(1, -1), alpha, groups[gi]["slot0"],
                             rows[gi][0], dims[gi][1] // TOK_BLOCK if rows[gi][1] else 1 << 30))

    ckv_p, kpe_p, st_p = extras[0]
    y_prompt = outs[0].reshape(bp, tp, d)
    y_sample = outs[1].reshape(bs, tsq, d)
    new_ckv = ckv_p.reshape(bp, 1, tp, -1)
    new_kpe = kpe_p[:, :MLA_ROPE].reshape(bp, 1, tp, MLA_ROPE)
    new_state = st_p.reshape(bp, 1, 2, HG_HEADS, HG_DK, HG_DV)
    return (y_prompt, y_sample, new_ckv, new_kpe, new_state)
```

```python
import functools

import jax
import jax.numpy as jnp
import numpy as np
from jax import lax
from jax.experimental import pallas as pl
from jax.experimental.pallas import tpu as pltpu

F32 = jnp.float32
BF16 = jnp.bfloat16

MLA_HEADS = 8
MLA_NOPE = 64
MLA_ROPE = 32
MLA_V = 64
HG_HEADS = 4
HG_DK = 128
HG_DV = 128
HG_CHUNK = 32
GRID_W = 64
ROPE_BASE = 10000.0
EC_FACTOR = 2
EPS = 1e-6

LANES = 128
SUBLANES = 8
BF16_ROWS = 16
VMEM_LIMIT = 56 * 1024 * 1024

TOK_BLOCK = 256
WIN_ROWS = 64
UNSELECTED = -(1 << 30)

NT_DIMS = (((1,), (1,)), ((), ()))


def _dot(a, b):
    return jnp.dot(a, b, preferred_element_type=F32)


def _dot_nt(a, b):
    return lax.dot_general(a, b, NT_DIMS, preferred_element_type=F32)


def _silu(x):
    return x * jax.nn.sigmoid(x)


def _params(*sem, flags=None):
    return pltpu.CompilerParams(dimension_semantics=sem, vmem_limit_bytes=VMEM_LIMIT, flags=flags)


def _const_spec(shape):
    zeros = (0,) * len(shape)
    return pl.BlockSpec(shape, lambda *_: zeros, pipeline_mode=pl.Buffered(1))


def _adaln_kernel(c_ref, w_ref, b_ref, o_ref):
    s = _silu(c_ref[...]).astype(BF16)
    o_ref[...] = _dot(s, w_ref[...].astype(BF16)) + b_ref[...]


def _adaln(cond, w_ada, b_ada):
    rows, d = cond.shape
    n = w_ada.shape[1]
    tn = n // 4
    return pl.pallas_call(
        _adaln_kernel,
        out_shape=jax.ShapeDtypeStruct((rows, n), F32),
        grid=(n // tn,),
        in_specs=[_const_spec((rows, d)),
                  pl.BlockSpec((d, tn), lambda j: (0, j)),
                  pl.BlockSpec((1, tn), lambda j: (0, j))],
        out_specs=pl.BlockSpec((rows, tn), lambda j: (0, j)),
        compiler_params=_params("arbitrary"),
        name="adaln",
    )(cond, w_ada, b_ada.reshape(1, n))


def _rms(x, g):
    return x * lax.rsqrt(jnp.mean(x * x, axis=-1, keepdims=True) + EPS) * g


def _rope(x, c, s):
    w = x.shape[-1]
    lane = lax.broadcasted_iota(jnp.int32, x.shape, 1)
    nxt = pltpu.roll(x, w - 1, 1)
    prv = pltpu.roll(x, 1, 1)
    return x * c + jnp.where(lane % 2 == 0, nxt, prv) * s


N_INPROJ_WEIGHTS = 6


def _mod_row(mod_ref, row0, per_batch):
    r = row0 + pl.program_id(0) if per_batch else row0
    return mod_ref[pl.ds(r, 1), :]


def _modulated(x_ref, m):
    d = x_ref.shape[1]
    return (x_ref[...] * (1.0 + m[:, d:2 * d]) + m[:, 0:d]).astype(BF16)


def _keys(k_nope, kpe):
    shared = pltpu.roll(kpe, MLA_NOPE, 1)
    return (k_nope + jnp.concatenate([shared] * MLA_HEADS, axis=1)).astype(BF16)


def _inproj_kernel(*refs, row0, per_batch, rope):
    x_ref, mod_ref = refs[:2]
    _inproj_body(x_ref, _mod_row(mod_ref, row0, per_batch), *refs[2:], rope=rope)


def _inproj_body(x_ref, m, *refs, rope):
    gam_ref, win_ref, qn_ref, wuq_ref, kvn_ref, wk_ref, wv_ref = refs[:1 + N_INPROJ_WEIGHTS]
    refs = refs[1 + N_INPROJ_WEIGHTS:]
    if rope:
        cq_ref, sq_ref, ck_ref, sk_ref = refs[:4]
        refs = refs[4:]
    q_o, k_o, v_o, ckv_o, kpe_o, hgx_o = refs
    h = _modulated(x_ref, m)
    hw = HG_HEADS * HG_DK
    o_kv = qn_ref.shape[1]
    o_pe = o_kv + kvn_ref.shape[1]
    o_h = o_pe + MLA_ROPE

    cq = _rms(_dot_nt(h, win_ref[0:o_kv, :]), qn_ref[...])
    q = _dot(cq.astype(BF16), wuq_ref[...])
    if rope:
        q = _rope(q, jnp.concatenate([cq_ref[...]] * MLA_HEADS, axis=1),
                  jnp.concatenate([sq_ref[...]] * MLA_HEADS, axis=1))
    q_o[...] = q.astype(BF16)

    ckv = _rms(_dot_nt(h, win_ref[o_kv:o_pe, :]), kvn_ref[...])
    ckv_o[...] = ckv
    kpe = _dot_nt(h, win_ref[o_pe:o_h, :])
    kpe = jnp.concatenate([kpe, jnp.zeros((kpe.shape[0], LANES - MLA_ROPE), F32)], axis=1)
    if rope:
        kpe = _rope(kpe, ck_ref[...], sk_ref[...])
    kpe_o[...] = kpe
    cb = ckv.astype(BF16)
    k_o[...] = _keys(_dot(cb, wk_ref[...]), kpe)
    v_o[...] = _dot(cb, wv_ref[...]).astype(BF16)

    z = _dot_nt(h, win_ref[o_h:o_h + 5 * hw, :])
    hgx_o[:, 0:hw] = _silu(z[:, 0:hw])
    for dr in range(2):
        g0, g1 = gam_ref[dr, 0:1, :], gam_ref[dr, 1:2, :]
        gmax = jnp.maximum(g0, g1)
        e0, e1 = jnp.exp(g0 - gmax), jnp.exp(g1 - gmax)
        lb = e0 / (e0 + e1)
        f = lb + (1.0 - lb) * jax.nn.sigmoid(z[:, (1 + dr) * hw:(2 + dr) * hw])
        hgx_o[:, (1 + 2 * dr) * hw:(2 + 2 * dr) * hw] = jnp.log(f)
        hgx_o[:, (2 + 2 * dr) * hw:(3 + 2 * dr) * hw] = 1.0 - f
    hgx_o[:, 5 * hw:6 * hw] = z[:, 3 * hw:4 * hw]
    hgx_o[:, 6 * hw:7 * hw] = z[:, 4 * hw:5 * hw]


def _inproj(x2d, batch, seq, mod, gamma, wts, row0, per_batch, rope_tabs):
    n, d = x2d.shape
    tm = TOK_BLOCK
    nblk = seq // tm
    rope = rope_tabs is not None
    hp = MLA_HEADS * LANES
    hw = HG_HEADS * HG_DK
    tok = lambda b, i: (b * nblk + i, 0)
    pos = lambda b, i: (i, 0)
    weights = [wts[k] for k in INPROJ_KEYS]
    ins = [x2d, mod, gamma] + weights
    in_specs = ([pl.BlockSpec((tm, d), tok), _const_spec(mod.shape), _const_spec(gamma.shape)]
                + [_const_spec(w.shape) for w in weights])
    if rope:
        ins += list(rope_tabs)
        in_specs += [pl.BlockSpec((tm, t.shape[1]), pos) for t in rope_tabs]
    widths = [(hp, BF16), (hp, BF16), (MLA_HEADS * MLA_V, BF16), (wts["kvn"].shape[1], F32), (LANES, F32),
              (7 * hw, F32)]
    return pl.pallas_call(
        functools.partial(_inproj_kernel, row0=row0, per_batch=per_batch, rope=rope),
        out_shape=[jax.ShapeDtypeStruct((n, w), dt) for w, dt in widths],
        grid=(batch, nblk),
        in_specs=in_specs,
        out_specs=[pl.BlockSpec((tm, w), tok) for w, _ in widths],
        compiler_params=_params("arbitrary", "arbitrary"),
        name="inproj",
    )(*ins)


def _kvup_kernel(ckv_ref, kpe_ref, wk_ref, wv_ref, k_o, v_o):
    cb = ckv_ref[...].astype(BF16)
    k_o[...] = _keys(_dot(cb, wk_ref[...]), kpe_ref[...])
    v_o[...] = _dot(cb, wv_ref[...]).astype(BF16)


def _kvup(ckv2d, kpe2d, wts):
    n = ckv2d.shape[0]
    tm = TOK_BLOCK
    widths = [MLA_HEADS * LANES, MLA_HEADS * MLA_V]
    row = lambda i: (i, 0)
    ws = [wts["wk"], wts["wv"]]
    return pl.pallas_call(
        _kvup_kernel,
        out_shape=[jax.ShapeDtypeStruct((n, w), BF16) for w in widths],
        grid=(n // tm,),
        in_specs=[pl.BlockSpec((tm, ckv2d.shape[1]), row), pl.BlockSpec((tm, LANES), row)]
                 + [_const_spec(w.shape) for w in ws],
        out_specs=[pl.BlockSpec((tm, w), row) for w in widths],
        compiler_params=_params("arbitrary"),
        name="kvup",
    )(ckv2d, kpe2d, *ws)


ATTN_SCALE = (MLA_NOPE + MLA_ROPE) ** -0.5


def _attn_body(q_ref, k_ref, v_ref, kc_ref, vc_ref, o_ref):
    cached = kc_ref is not None
    scale = ATTN_SCALE
    per_slab = LANES // MLA_V
    own = lax.broadcasted_iota(jnp.int32, (q_ref.shape[0], LANES), 1) // MLA_V
    for slab in range(MLA_HEADS // per_slab):
        vsl = slice(slab * LANES, (slab + 1) * LANES)
        out = None
        for sub in range(per_slab):
            hd = slab * per_slab + sub
            sl = slice(hd * LANES, (hd + 1) * LANES)
            q = q_ref[:, sl]
            s = _dot_nt(q, k_ref[:, sl]) * scale
            mx = jnp.max(s, axis=-1, keepdims=True)
            if cached:
                s2 = _dot_nt(q, kc_ref[:, sl]) * scale
                mx = jnp.maximum(mx, jnp.max(s2, axis=-1, keepdims=True))
            e = jnp.exp(s - mx)
            den = jnp.sum(e, axis=-1, keepdims=True)
            o = _dot(e.astype(BF16), v_ref[:, vsl])
            if cached:
                e2 = jnp.exp(s2 - mx)
                den = den + jnp.sum(e2, axis=-1, keepdims=True)
                o = o + _dot(e2.astype(BF16), vc_ref[:, vsl])
            o = o / den
            out = o if out is None else jnp.where(own == sub, o, out)
        o_ref[:, vsl] = out.astype(o_ref.dtype)


def _chunk_scan(x, reverse):
    tm = x.shape[0]
    rin = lax.broadcasted_iota(jnp.int32, x.shape, 0) % HG_CHUNK
    step = 1
    while step < HG_CHUNK:
        if reverse:
            x = x + jnp.where(rin < HG_CHUNK - step, pltpu.roll(x, tm - step, 0), 0.0)
        else:
            x = x + jnp.where(rin >= step, pltpu.roll(x, step, 0), 0.0)
        step *= 2
    return x


def _hgrn_kernel(*refs, has_init):
    fwd, bwd = refs[0:4], refs[4:8]
    refs = refs[8:]
    s0_ref = None
    if has_init:
        s0_ref = refs[0]
        refs = refs[1:]
    of_ref, ob_ref, sfin_ref, st_scr = refs
    i = pl.program_id(1)
    _hgrn_body(fwd, bwd, s0_ref, of_ref, ob_ref, sfin_ref, st_scr, i == 0, i == pl.num_programs(1) - 1)


def _hgrn_body(fwd, bwd, s0_ref, of_ref, ob_ref, sfin_ref, st_scr, first, last):
    tm = fwd[0].shape[0]
    c = HG_CHUNK
    nch = tm // c
    dk, dv = HG_DK, HG_DV
    hw = HG_HEADS * dk

    def initial(dr, hd):
        return s0_ref[0, dr, hd].T if s0_ref is not None else jnp.zeros((dv, dk), F32)

    if st_scr is not None:
        @pl.when(first)
        def _init():
            for dr in range(2):
                for hd in range(HG_HEADS):
                    st_scr[dr, hd] = initial(dr, hd)

    npair = nch // 2
    pair = 2 * c
    row = lax.broadcasted_iota(jnp.int32, (tm, tm), 0)
    col = lax.broadcasted_iota(jnp.int32, (tm, tm), 1)
    same = (row // c) == (col // c)
    same_pair = (row // pair) == (col // pair)
    bd = (lax.broadcasted_iota(jnp.int32, (tm, npair * dk), 0) // pair
          == lax.broadcasted_iota(jnp.int32, (tm, npair * dk), 1) // dk)
    chunk_odd = (lax.broadcasted_iota(jnp.int32, (tm, hw), 0) // c) % 2 == 1

    for dr, (hq_ref, lf_ref, kk_ref, vv_ref) in enumerate((fwd, bwd)):
        o_ref = of_ref if dr == 0 else ob_ref
        tri = same & ((col <= row) if dr == 0 else (col >= row))
        cross = same_pair & (((row // c) > (col // c)) if dr == 0 else ((row // c) < (col // c)))
        second = chunk_odd if dr == 0 else ~chunk_odd
        bcum = _chunk_scan(lf_ref[...], reverse=dr == 1)
        closing = c - 1 if dr == 0 else 0
        btot3 = bcum.reshape(nch, c, hw)[:, closing:closing + 1, :]
        btot = jnp.broadcast_to(btot3, (nch, c, hw)).reshape(tm, hw)
        bpart = jnp.where(chunk_odd, pltpu.roll(btot, c, 0), pltpu.roll(btot, tm - c, 0))
        epart = jnp.exp(bpart)
        bpair = btot + bpart
        kk = kk_ref[...]
        qd = hq_ref[...] * jnp.exp(bcum)
        kd = kk * jnp.exp(-bcum)
        ke = kk * jnp.exp(btot - bcum)
        qd2 = jnp.where(second, qd * epart, qd)
        ke2 = jnp.where(second, ke, ke * epart)
        vv = vv_ref[...]
        order = range(npair) if dr == 0 else range(npair - 1, -1, -1)
        for hd in range(HG_HEADS):
            sl = slice(hd * dk, (hd + 1) * dk)
            qd_h = qd[:, sl].astype(BF16)
            v_h = vv[:, hd * dv:(hd + 1) * dv]
            a = jnp.where(tri, _dot_nt(qd_h, kd[:, sl].astype(BF16)),
                          jnp.where(cross, _dot_nt(qd_h, ke[:, sl].astype(BF16)), 0.0))
            o_intra = _dot(a.astype(BF16), v_h.astype(BF16))
            kebd = jnp.where(bd, jnp.concatenate([ke2[:, sl]] * npair, axis=1), 0.0).astype(BF16)
            qbd = jnp.where(bd, jnp.concatenate([qd2[:, sl]] * npair, axis=1), 0.0).astype(BF16)
            ut = _dot(v_h.T.astype(BF16), kebd)
            st = st_scr[dr, hd] if st_scr is not None else initial(dr, hd)
            prev = [None] * npair
            for p in order:
                prev[p] = st
                st = st * jnp.exp(bpair[p * pair:p * pair + 1, sl]) + ut[:, p * dk:(p + 1) * dk]
            if st_scr is not None:
                st_scr[dr, hd] = st
            o_inter = _dot_nt(qbd, jnp.concatenate(prev, axis=1).astype(BF16))
            o_ref[:, hd * dv:(hd + 1) * dv] = o_intra + o_inter

            if last is True:
                sfin_ref[0, dr, hd] = st.T
            else:
                @pl.when(last)
                def _final(st=st, dr=dr, hd=hd):
                    sfin_ref[0, dr, hd] = st.T


def _hgrn(hgx, batch, seq, s0=None):
    n = hgx.shape[0]
    tm = TOK_BLOCK
    nblk = seq // tm
    hw = HG_HEADS * HG_DK

    def spec(lane_blk, rev):
        if rev:
            return pl.BlockSpec((tm, hw), lambda b, i: (b * nblk + nblk - 1 - i, lane_blk))
        return pl.BlockSpec((tm, hw), lambda b, i: (b * nblk + i, lane_blk))

    in_specs = [spec(0, False), spec(1, False), spec(2, False), spec(5, False),
                spec(0, True), spec(3, True), spec(4, True), spec(5, True)]
    ins = [hgx] * 8
    st_shape = (1, 2, HG_HEADS, HG_DK, HG_DV)
    st_spec = pl.BlockSpec(st_shape, lambda b, i: (b, 0, 0, 0, 0))
    if s0 is not None:
        ins.append(s0)
        in_specs.append(st_spec)
    return pl.pallas_call(
        functools.partial(_hgrn_kernel, has_init=s0 is not None),
        out_shape=[jax.ShapeDtypeStruct((n, hw), F32), jax.ShapeDtypeStruct((n, hw), F32),
                   jax.ShapeDtypeStruct((batch,) + st_shape[1:], F32)],
        grid=(batch, nblk),
        in_specs=in_specs,
        out_specs=[spec(0, False), spec(0, True), st_spec],
        scratch_shapes=[pltpu.VMEM((2, HG_HEADS, HG_DV, HG_DK), F32)],
        compiler_params=_params("arbitrary", "arbitrary"),
        name="hgrn",
    )(*ins)


def _layer_norm(x, g, b):
    xc = x - jnp.mean(x, axis=-1, keepdims=True)
    var = jnp.mean(xc * xc, axis=-1, keepdims=True)
    return xc * lax.rsqrt(var + EPS) * g + b


N_POSTMIX_WEIGHTS = 8
INPROJ_KEYS = ("win", "qn", "wuq", "kvn", "wk", "wv")
POSTMIX_KEYS = ("hgn", "womla", "wohg", "wout", "ln1g", "ln1b", "wr")
MIXER_SEQS = 2


def _postmix_kernel(x_ref, mod_ref, *refs, alpha, row0, per_batch, cached):
    n_attn = 5 if cached else 3
    q_ref, k_ref, v_ref = refs[:3]
    kc_ref, vc_ref = refs[3:5] if cached else (None, None)
    of_ref, ob_ref, zg_ref = refs[n_attn:n_attn + 3]
    om_s = refs[-1]
    _attn_body(q_ref, k_ref, v_ref, kc_ref, vc_ref, om_s)
    _postmix_body(x_ref, _mod_row(mod_ref, row0, per_batch), of_ref, ob_ref, zg_ref, om_s,
                  *refs[n_attn + 3:-1], alpha=alpha)


def _postmix_body(x_ref, m, of_ref, ob_ref, zg_ref, om_ref, wg_ref, hgn_ref, womla_ref,
                  wohg_ref, wout_ref, lng_ref, lnb_ref, wr_ref, x1_o, h2_o, aff_o, *, alpha):
    d = x_ref.shape[1]
    tb = aff_o.shape[2]
    g1, sh2, sc2 = m[:, 2 * d:3 * d], m[:, 3 * d:4 * d], m[:, 4 * d:5 * d]
    o = of_ref[...] + ob_ref[...]
    zg = zg_ref[...]
    parts = []
    for hd in range(HG_HEADS):
        sl = slice(hd * HG_DV, (hd + 1) * HG_DV)
        parts.append(_rms(o[:, sl], hgn_ref[...]) * _silu(zg[:, sl]))
    ohg = jnp.concatenate(parts, axis=1).astype(BF16)
    gates = _dot_nt(_modulated(x_ref, m), wg_ref[wg_ref.shape[0] - 2 * d:, :])
    merged = (jax.nn.sigmoid(gates[:, 0:d]) * _dot(om_ref[...], womla_ref[...])
              + jax.nn.sigmoid(gates[:, d:2 * d]) * _dot(ohg, wohg_ref[...]))
    mix = _dot(merged.astype(BF16), wout_ref[...])
    x1 = _layer_norm(alpha * x_ref[...] + g1 * mix, lng_ref[...], lnb_ref[...])
    x1_o[...] = x1
    h2 = (x1 * (1.0 + sc2) + sh2).astype(BF16)
    h2_o[...] = h2
    logits = _dot_nt(wr_ref[...], h2)
    e = jnp.exp(logits - jnp.max(logits, axis=0, keepdims=True))
    aff = e / jnp.sum(e, axis=0, keepdims=True)
    for blk in range(aff_o.shape[0]):
        aff_o[blk] = aff[:, blk * tb:(blk + 1) * tb]


def _postmix(x2d, batch, seq, mod, q, k, v, cache, o_f, o_b, hgx, wts, alpha, n_experts, row0, per_batch):
    n, d = x2d.shape
    tm = TOK_BLOCK
    nblk = seq // tm
    hw = HG_HEADS * HG_DV
    hp, hv = q.shape[1], v.shape[1]
    tok = lambda b, i: (b * nblk + i, 0)
    per_seq = lambda b, i: (b, 0)
    weights = [wts[k] for k in POSTMIX_KEYS]
    attn_ins = [q, k, v]
    attn_specs = [pl.BlockSpec((tm, hp), tok), pl.BlockSpec((seq, hp), per_seq), pl.BlockSpec((seq, hv), per_seq)]
    if cache is not None:
        past = cache[0].shape[0] // batch
        attn_ins += list(cache)
        attn_specs += [pl.BlockSpec((past, hp), per_seq), pl.BlockSpec((past, hv), per_seq)]
    return pl.pallas_call(
        functools.partial(_postmix_kernel, alpha=alpha, row0=row0, per_batch=per_batch,
                          cached=cache is not None),
        out_shape=[jax.ShapeDtypeStruct((n, d), F32), jax.ShapeDtypeStruct((n, d), BF16),
                   jax.ShapeDtypeStruct((n // tm, n_experts, tm), F32)],
        grid=(batch, nblk),
        in_specs=[pl.BlockSpec((tm, d), tok), _const_spec(mod.shape)] + attn_specs
                 + [pl.BlockSpec((tm, hw), tok), pl.BlockSpec((tm, hw), tok),
                    pl.BlockSpec((tm, hw), lambda b, i: (b * nblk + i, 6)), _const_spec(wts["win"].shape)]
                 + [_const_spec(w.shape) for w in weights],
        out_specs=[pl.BlockSpec((tm, d), tok), pl.BlockSpec((tm, d), tok),
                   pl.BlockSpec((1, n_experts, tm), lambda b, i: (b * nblk + i, 0, 0))],
        scratch_shapes=[pltpu.VMEM((tm, hv), BF16)],
        compiler_params=_params("arbitrary", "arbitrary"),
        name="postmix",
    )(x2d, mod, *attn_ins, o_f, o_b, hgx, wts["win"], *weights)


def _mixer_kernel(x_ref, mod_ref, *refs, alpha, row0, seq):
    nw = 1 + N_INPROJ_WEIGHTS
    in_w, refs = refs[:nw], refs[nw:]
    pm_w, refs = refs[:N_POSTMIX_WEIGHTS - 1], refs[N_POSTMIX_WEIGHTS - 1:]
    pm_w = (in_w[1],) + tuple(pm_w)
    x1_o, h2_o, aff_o, ckv_o, kpe_o, sfin_o, q_s, k_s, v_s, hgx_s, om_s, of_s, ob_s = refs
    m = _mod_row(mod_ref, row0, False)
    _inproj_body(x_ref, m, *in_w, q_s, k_s, v_s, ckv_o, kpe_o, hgx_s, rope=False)
    hw = HG_HEADS * HG_DK
    for s in range(x_ref.shape[0] // seq):
        rows = slice(s * seq, (s + 1) * seq)
        _attn_body(q_s.at[rows], k_s.at[rows], v_s.at[rows], None, None, om_s.at[rows])
        lane = lambda j: hgx_s.at[rows, j * hw:(j + 1) * hw]
        _hgrn_body((lane(0), lane(1), lane(2), lane(5)), (lane(0), lane(3), lane(4), lane(5)), None,
                   of_s.at[rows], ob_s.at[rows], sfin_o.at[s:s + 1], None, True, True)
    _postmix_body(x_ref, m, of_s, ob_s, hgx_s.at[:, 6 * hw:7 * hw], om_s, *pm_w, x1_o, h2_o, aff_o,
                  alpha=alpha)


def _mixer(x2d, batch, seq, mod, gamma, wts, alpha, n_experts, row0):
    n, d = x2d.shape
    assert seq == TOK_BLOCK
    ns = MIXER_SEQS if batch % MIXER_SEQS == 0 else 1
    tm = ns * seq
    hp = MLA_HEADS * LANES
    hv = MLA_HEADS * MLA_V
    hw = HG_HEADS * HG_DK
    kvl = wts["kvn"].shape[1]
    weights = [wts[k] for k in INPROJ_KEYS + POSTMIX_KEYS]
    tok = lambda b: (b, 0)
    st_shape = (ns, 2, HG_HEADS, HG_DK, HG_DV)
    return pl.pallas_call(
        functools.partial(_mixer_kernel, alpha=alpha, row0=row0, seq=seq),
        out_shape=[jax.ShapeDtypeStruct((n, d), F32), jax.ShapeDtypeStruct((n, d), BF16),
                   jax.ShapeDtypeStruct((n // seq, n_experts, seq), F32),
                   jax.ShapeDtypeStruct((n, kvl), F32), jax.ShapeDtypeStruct((n, LANES), F32),
                   jax.ShapeDtypeStruct((batch,) + st_shape[1:], F32)],
        grid=(batch // ns,),
        in_specs=[pl.BlockSpec((tm, d), tok), _const_spec(mod.shape), _const_spec(gamma.shape)]
                 + [_const_spec(w.shape) for w in weights],
        out_specs=[pl.BlockSpec((tm, d), tok), pl.BlockSpec((tm, d), tok),
                   pl.BlockSpec((ns, n_experts, seq), lambda b: (b, 0, 0)),
                   pl.BlockSpec((tm, kvl), tok), pl.BlockSpec((tm, LANES), tok),
                   pl.BlockSpec(st_shape, lambda b: (b, 0, 0, 0, 0))],
        scratch_shapes=[pltpu.VMEM((tm, hp), BF16), pltpu.VMEM((tm, hp), BF16), pltpu.VMEM((tm, hv), BF16),
                        pltpu.VMEM((tm, 7 * hw), F32), pltpu.VMEM((tm, hv), BF16),
                        pltpu.VMEM((tm, hw), F32), pltpu.VMEM((tm, hw), F32)],
        compiler_params=_params("arbitrary"),
        name="mixer",
    )(x2d, mod, gamma, *weights)


def _route_kernel(aff_ref, rank_o, cnt_o, *, cap):
    nb, ne, tb = aff_ref.shape
    key = aff_ref[...]

    def count(mask):
        return jnp.sum(jnp.sum(jnp.where(mask, 1.0, 0.0), axis=0), axis=1, keepdims=True)

    def bit_step(it, bits):
        cand = bits | jnp.left_shift(jnp.int32(1), 30 - it)
        return jnp.where(count(key >= pltpu.bitcast(cand, F32)[None]) >= cap, cand, bits)

    bits = lax.fori_loop(0, 31, bit_step, jnp.zeros((ne, 1), jnp.int32))
    thr = pltpu.bitcast(bits, F32)
    need = cap - count(key > thr[None])
    before = (lax.broadcasted_iota(jnp.int32, (tb, tb), 0)
              < lax.broadcasted_iota(jnp.int32, (tb, tb), 1))
    before = jnp.where(before, 1.0, 0.0).astype(BF16)
    off_eq = jnp.zeros((ne, 1), F32)
    off_sel = jnp.zeros((ne, 1), F32)
    cnt_o[...] = jnp.zeros_like(cnt_o)
    for blk in range(nb):
        key_b = key[blk]
        eq = key_b == thr
        eq_b = jnp.where(eq, 1.0, 0.0)
        eq_rank = _dot(eq_b.astype(BF16), before) + off_eq
        sel = (key_b > thr) | (eq & (eq_rank < need))
        sel_b = jnp.where(sel, 1.0, 0.0)
        rank = _dot(sel_b.astype(BF16), before) + off_sel
        rank_o[blk] = jnp.where(sel, rank.astype(jnp.int32), UNSELECTED)
        cnt_o[:, blk:blk + 1] = off_sel.astype(jnp.int32)
        off_eq = off_eq + jnp.sum(eq_b, axis=1, keepdims=True)
        off_sel = off_sel + jnp.sum(sel_b, axis=1, keepdims=True)
    cnt_o[:, nb:nb + 1] = off_sel.astype(jnp.int32)


def _route(aff, cap):
    nb, ne, tb = aff.shape
    assert nb + 1 <= LANES
    rank, cnt = pl.pallas_call(
        functools.partial(_route_kernel, cap=cap),
        out_shape=[jax.ShapeDtypeStruct(aff.shape, jnp.int32), jax.ShapeDtypeStruct((ne, LANES), jnp.int32)],
        in_specs=[pl.BlockSpec(memory_space=pltpu.VMEM)],
        out_specs=[pl.BlockSpec(memory_space=pltpu.VMEM)] * 2,
        compiler_params=pltpu.CompilerParams(vmem_limit_bytes=VMEM_LIMIT),
        name="route",
    )(aff)
    return rank, cnt[:, :nb + 1]


def _window_hits(rk_ref, firsts, slot0, win):
    ne, tb = rk_ref.shape[1], rk_ref.shape[2]
    win_iota = lax.broadcasted_iota(jnp.int32, (win, tb), 0)
    return [(rk_ref[0, e:e + 1, :] + (slot0 - firsts[e])) == win_iota for e in range(ne)]


def _compact_kernel(first_ref, end_ref, rounds_ref, *refs, groups, slots):
    ng = len(groups)
    h2_refs, rk_refs, af_refs = refs[0:ng], refs[ng:2 * ng], refs[2 * ng:3 * ng]
    hbms, refs = refs[3 * ng:3 * ng + 2], refs[3 * ng + 2:]
    stages, tails = refs[0:2], refs[2:4]
    sem, issued = refs[4:]
    b = pl.program_id(0)
    ne = rk_refs[0].shape[1]
    win = WIN_ROWS
    sub = BF16_ROWS

    def copies(slot, dsts):
        return [pltpu.make_async_copy(stage.at[slot, pl.ds(e * win, win), :],
                                      hbm.at[e, pl.ds(pl.multiple_of(dsts[e], sub), win), :], sem.at[c, e])
                for c, (stage, hbm) in enumerate(zip(stages, hbms)) for e in range(ne)]

    def wait_previous():
        @pl.when(issued[0] > 0)
        def _():
            for cp in copies(0, [0] * ne):
                cp.wait()

    @pl.when(b == 0)
    def _init():
        issued[0] = 0
        for stage, tail in zip(stages, tails):
            tail[...] = jnp.zeros_like(tail)
            stage[1] = jnp.zeros(stage.shape[1:], stage.dtype)
        pad = copies(1, [slots] * ne)
        for cp in pad:
            cp.start()
        for cp in pad:
            cp.wait()

    def group_body(h2_ref, rk_ref, af_ref, slot0):
        firsts = [first_ref[b * ne + e] for e in range(ne)]
        bases = [(f // sub) * sub for f in firsts]
        ends = [end_ref[b * ne + e] - bases[e] for e in range(ne)]

        def one_round(r, carry):
            dsts = [bases[e] + r * win for e in range(ne)]
            hits = _window_hits(rk_ref, dsts, slot0, win)
            onehot = jnp.where(jnp.concatenate(hits, axis=0), 1.0, 0.0).astype(BF16)
            gate = jnp.concatenate(
                [jnp.sum(jnp.where(hits[e], af_ref[0, e:e + 1, :], 0.0), axis=1, keepdims=True)
                 for e in range(ne)], axis=0)
            slot = issued[0] % 2
            stages[0][slot] = _dot(onehot, h2_ref[...]).astype(BF16)
            stages[1][slot] = jnp.broadcast_to(gate, (ne * win, LANES))
            @pl.when(r == 0)
            def _heads():
                for stage, tail in zip(stages, tails):
                    sub_iota = lax.broadcasted_iota(jnp.int32, (sub, stage.shape[2]), 0)
                    for e in range(ne):
                        head = stage[slot, e * win:e * win + sub, :]
                        stage[slot, e * win:e * win + sub, :] = jnp.where(
                            sub_iota < firsts[e] - bases[e], tail[e * sub:(e + 1) * sub, :], head)

            for stage, tail in zip(stages, tails):
                for e in range(ne):
                    last = (ends[e] // sub) * sub
                    group = stage[slot, pl.ds(pl.multiple_of(e * win + last % win, sub), sub), :]
                    tail[e * sub:(e + 1) * sub, :] = jnp.where(r == last // win, group,
                                                               tail[e * sub:(e + 1) * sub, :])
            wait_previous()
            for cp in copies(slot, [jnp.minimum(dst, slots) for dst in dsts]):
                cp.start()
            issued[0] = issued[0] + 1
            return carry

        lax.fori_loop(0, rounds_ref[b], one_round, 0)

    blk0 = 0
    for gi, g in enumerate(groups):
        @pl.when((b >= blk0) & (b < blk0 + g["nb"]))
        def _(gi=gi, g=g):
            group_body(h2_refs[gi], rk_refs[gi], af_refs[gi], g["slot0"])
        blk0 += g["nb"]

    @pl.when(b == pl.num_programs(0) - 1)
    def _drain():
        wait_previous()


def _compact(groups, first, end, rounds, slots):
    d = groups[0]["h2"].shape[1]
    nbs = [g["rank"].shape[0] for g in groups]
    ne, tb = groups[0]["rank"].shape[1:]
    meta, specs_h2, specs_rk = [], [], []
    blk0 = 0
    for g, nb in zip(groups, nbs):
        meta.append(dict(nb=nb, slot0=g["slot0"]))
        local = lambda b, *_, blk0=blk0, nb=nb: jnp.clip(b - blk0, 0, nb - 1)
        specs_h2.append(pl.BlockSpec((tb, d), lambda b, *_, local=local: (local(b), 0)))
        specs_rk.append(pl.BlockSpec((1, ne, tb), lambda b, *_, local=local: (local(b), 0, 0)))
        blk0 += nb
    streams = [(d, BF16), (LANES, F32)]
    return pl.pallas_call(
        functools.partial(_compact_kernel, groups=meta, slots=slots),
        out_shape=[jax.ShapeDtypeStruct((ne, slots + WIN_ROWS, w), dt) for w, dt in streams],
        grid_spec=pltpu.PrefetchScalarGridSpec(
            num_scalar_prefetch=3,
            grid=(sum(nbs),),
            in_specs=specs_h2 + specs_rk + specs_rk,
            out_specs=[pl.BlockSpec(memory_space=pl.ANY)] * 2,
            scratch_shapes=[pltpu.VMEM((2, ne * WIN_ROWS, w), dt) for w, dt in streams]
                           + [pltpu.VMEM((ne * BF16_ROWS, w), dt) for w, dt in streams]
                           + [pltpu.SemaphoreType.DMA((2, ne)), pltpu.SMEM((1,), jnp.int32)]),
        compiler_params=_params("arbitrary"),
        name="compact",
    )(first, end, rounds, *[g["h2"] for g in groups], *[g["rank"] for g in groups],
      *[g["aff"] for g in groups])


def _ffn_kernel(xe_ref, ge_ref, w1_ref, w3_ref, w2_ref, ye_ref, *scratch):
    f = pl.program_id(1)
    x = xe_ref[0]
    hid = _silu(_dot(x, w1_ref[0].astype(BF16))) * _dot(x, w3_ref[0].astype(BF16))
    y = _dot(hid.astype(BF16), w2_ref[0].astype(BF16))
    if not scratch:
        ye_ref[0] = (y * ge_ref[0, :, 0:1]).astype(ye_ref.dtype)
        return
    acc_scr, = scratch
    last = pl.num_programs(1) - 1

    @pl.when(f == 0)
    def _first():
        acc_scr[...] = y

    @pl.when((f > 0) & (f < last))
    def _middle():
        acc_scr[...] += y

    @pl.when(f == last)
    def _last():
        ye_ref[0] = ((acc_scr[...] + y) * ge_ref[0, :, 0:1]).astype(ye_ref.dtype)


def _ffn(xe, ge, w1, w3, w2, slots, ft):
    ne, d, dff = w1.shape
    nf = dff // ft
    return pl.pallas_call(
        _ffn_kernel,
        out_shape=jax.ShapeDtypeStruct((ne, slots, d), BF16),
        grid=(ne, nf),
        in_specs=[pl.BlockSpec((1, slots, d), lambda e, f: (e, 0, 0)),
                  pl.BlockSpec((1, slots, ge.shape[2]), lambda e, f: (e, 0, 0)),
                  pl.BlockSpec((1, d, ft), lambda e, f: (e, 0, f)),
                  pl.BlockSpec((1, d, ft), lambda e, f: (e, 0, f)),
                  pl.BlockSpec((1, ft, d), lambda e, f: (e, f, 0))],
        out_specs=pl.BlockSpec((1, slots, d), lambda e, f: (e, 0, 0)),
        scratch_shapes=[pltpu.VMEM((slots, d), F32)] if nf > 1 else [],
        compiler_params=_params("arbitrary", "arbitrary"),
        name="ffn",
    )(xe, ge, w1, w3, w2)


def _combine_kernel(first_ref, rounds_ref, rk_ref, x1_ref, mod_ref, lng_ref, lnb_ref, ye_hbm, out_ref,
                    buf, acc_scr, sem, *, d_model, alpha, slot0, slots, row0, blocks_per_batch):
    d = d_model
    b = pl.program_id(0)
    nblk = pl.num_programs(0)
    ne, tb = rk_ref.shape[1], rk_ref.shape[2]
    win = WIN_ROWS
    eye = (lax.broadcasted_iota(jnp.int32, (tb, tb), 0)
           == lax.broadcasted_iota(jnp.int32, (tb, tb), 1))
    eye = jnp.where(eye, 1.0, 0.0).astype(BF16)

    def starts_of(blk, r):
        firsts = [(first_ref[blk * ne + e] // BF16_ROWS) * BF16_ROWS + r * win for e in range(ne)]
        return firsts, [jnp.minimum(f, slots - win) for f in firsts]

    def windows(slot, starts):
        return [pltpu.make_async_copy(ye_hbm.at[e, pl.ds(pl.multiple_of(starts[e], BF16_ROWS), win), :],
                                      buf.at[slot, pl.ds(e * win, win), :], sem.at[slot, e])
                for e in range(ne)]

    def scatter(slot, firsts, starts):
        hits = _window_hits(rk_ref, starts, slot0, win)
        hits = [h & ((rk_ref[0, e:e + 1, :] + slot0) >= firsts[e]) for e, h in enumerate(hits)]
        hit = jnp.where(jnp.concatenate(hits, axis=0), 1.0, 0.0).astype(BF16)
        hit_t = _dot_nt(eye, hit).astype(BF16)
        return _dot(hit_t, buf[slot])

    cur = b % 2

    @pl.when(b == 0)
    def _prime():
        for cp in windows(0, starts_of(0, 0)[1]):
            cp.start()

    @pl.when(b + 1 < nblk)
    def _prefetch():
        for cp in windows(1 - cur, starts_of(b + 1, 0)[1]):
            cp.start()

    firsts, starts = starts_of(b, 0)
    for cp in windows(cur, starts):
        cp.wait()
    acc_scr[...] = scatter(cur, firsts, starts)

    def extra_round(r, carry):
        firsts, starts = starts_of(b, r)
        for cp in windows(2, starts):
            cp.start()
        for cp in windows(2, starts):
            cp.wait()
        acc_scr[...] += scatter(2, firsts, starts)
        return carry

    lax.fori_loop(1, rounds_ref[b], extra_round, 0)
    r = row0 + b // blocks_per_batch
    g2 = mod_ref[pl.ds(r, 1), :][:, 5 * d:6 * d]
    out_ref[...] = _layer_norm(alpha * x1_ref[...] + g2 * acc_scr[...], lng_ref[...], lnb_ref[...])


def _combine(ye, rank, first, rounds, x1, mod, ln_g, ln_b, alpha, slot0, row0, blocks_per_batch):
    n, d = x1.shape
    nb, ne, tb = rank.shape
    slots = ye.shape[1]
    return pl.pallas_call(
        functools.partial(_combine_kernel, d_model=d, alpha=alpha, slot0=slot0, slots=slots, row0=row0,
                          blocks_per_batch=blocks_per_batch),
        out_shape=jax.ShapeDtypeStruct((n, d), F32),
        grid_spec=pltpu.PrefetchScalarGridSpec(
            num_scalar_prefetch=2,
            grid=(nb,),
            in_specs=[pl.BlockSpec((1, ne, tb), lambda b, *_: (b, 0, 0)),
                      pl.BlockSpec((tb, d), lambda b, *_: (b, 0)),
                      pl.BlockSpec(mod.shape, lambda b, *_: (0, 0)),
                      pl.BlockSpec((1, d), lambda b, *_: (0, 0)),
                      pl.BlockSpec((1, d), lambda b, *_: (0, 0)),
                      pl.BlockSpec(memory_space=pl.ANY)],
            out_specs=pl.BlockSpec((tb, d), lambda b, *_: (b, 0)),
            scratch_shapes=[pltpu.VMEM((3, ne * WIN_ROWS, d), ye.dtype), pltpu.VMEM((tb, d), F32),
                            pltpu.SemaphoreType.DMA((3, ne))]),
        compiler_params=_params("arbitrary"),
        name="combine",
    )(first, rounds, rank, x1, mod, ln_g, ln_b, ye)


def _prep_weights(w_in, q_norm, w_uq, kv_norm, w_ukv, w_o_mla, hgrn_norm, w_o_hg, w_out, ln1_g, ln1_b,
                  w_router):
    d = w_in.shape[0]
    q_lora, kv_lora = q_norm.shape[0], kv_norm.shape[0]
    hw = HG_HEADS * HG_DK
    hh, hp = MLA_HEADS, MLA_HEADS * LANES
    o_kv, o_pe = q_lora, q_lora + kv_lora
    o_h = o_pe + MLA_ROPE
    o_g = o_h + 5 * hw
    assert w_in.shape[1] == o_g + 2 * d
    qk = MLA_NOPE + MLA_ROPE
    kvw = MLA_NOPE + MLA_V
    b16 = lambda a: a.astype(BF16)
    assert all(o % BF16_ROWS == 0 for o in (o_kv, o_pe, o_h, o_g))
    win = b16(w_in.T)
    wuq = jnp.pad(w_uq.reshape(q_lora, hh, qk), ((0, 0), (0, 0), (0, LANES - qk))).reshape(q_lora, hp)
    ukv = w_ukv.reshape(kv_lora, hh, kvw)
    wk = jnp.pad(ukv[:, :, :MLA_NOPE], ((0, 0), (0, 0), (0, LANES - MLA_NOPE))).reshape(kv_lora, hp)
    wv = ukv[:, :, MLA_NOPE:].reshape(kv_lora, hh * MLA_V)
    return dict(
        win=win, qn=q_norm.reshape(1, -1), wuq=b16(wuq), kvn=kv_norm.reshape(1, -1), wk=b16(wk), wv=b16(wv),
        hgn=hgrn_norm.reshape(1, -1), womla=b16(w_o_mla), wohg=b16(w_o_hg), wout=b16(w_out),
        ln1g=ln1_g.reshape(1, -1), ln1b=ln1_b.reshape(1, -1), wr=b16(w_router.T))


def _rope_tables(seq):
    n_freq = MLA_ROPE // 4
    inv = ROPE_BASE ** (-np.arange(n_freq, dtype=np.float64) / n_freq)
    t = np.arange(seq)
    ang = np.concatenate([(t // GRID_W)[:, None] * inv, (t % GRID_W)[:, None] * inv], axis=-1)
    cos = np.repeat(np.cos(ang), 2, axis=1)
    sin = np.repeat(np.sin(ang), 2, axis=1) * np.tile([-1.0, 1.0], MLA_ROPE // 2)
    ck = np.pad(cos, ((0, 0), (0, LANES - MLA_ROPE)), constant_values=1.0)
    sk = np.pad(sin, ((0, 0), (0, LANES - MLA_ROPE)))
    cq = np.pad(cos, ((0, 0), (MLA_NOPE, LANES - MLA_NOPE - MLA_ROPE)), constant_values=1.0)
    sq = np.pad(sin, ((0, 0), (MLA_NOPE, LANES - MLA_NOPE - MLA_ROPE)))
    return tuple(jnp.asarray(a, F32) for a in (cq, sq, ck, sk))


def _window_sched(cnt, slot0):
    first = slot0 + cnt[:, :-1]
    end = slot0 + cnt[:, 1:]
    flat = lambda a: a.T.reshape(-1).astype(jnp.int32)
    rounds = jnp.max((end - (first // BF16_ROWS) * BF16_ROWS + WIN_ROWS - 1) // WIN_ROWS, axis=0)
    return flat(first), flat(end), jnp.maximum(rounds, 1).astype(jnp.int32)


def kernel(x_prompt, x_sample, c, cache_ckv, cache_kpe, state_hgrn, c_ctx, w_ada, b_ada, w_in, mla_q_norm, mla_w_uq, mla_kv_norm, mla_w_ukv, mla_w_o, hgrn_gamma, hgrn_norm, hgrn_w_o, w_out, ln1_g, ln1_b, moe_w_router, moe_w1, moe_w3, moe_w2, ln2_g, ln2_b):
    depth = w_ada.shape[0]
    assert depth == 1, "single trunk layer"
    bp, tp, d = x_prompt.shape
    bs, tsq, _ = x_sample.shape
    ne = moe_w_router.shape[-1]
    alpha = (2 * depth) ** 0.25
    past = cache_ckv.shape[2]
    assert tp % TOK_BLOCK == 0 and tsq % TOK_BLOCK == 0 and past % TOK_BLOCK == 0 and tsq % GRID_W == 0

    wts = _prep_weights(w_in[0], mla_q_norm[0], mla_w_uq[0], mla_kv_norm[0], mla_w_ukv[0], mla_w_o[0],
                        hgrn_norm[0], hgrn_w_o[0], w_out[0], ln1_g[0], ln1_b[0], moe_w_router[0])
    cond_rows = -(-(1 + bs) // SUBLANES) * SUBLANES
    cond = jnp.concatenate([c_ctx[None], c, jnp.zeros((cond_rows - 1 - bs, d), F32)], axis=0)
    mod = _adaln(cond, w_ada[0], b_ada[0])

    xs = [x_prompt.reshape(bp * tp, d), x_sample.reshape(bs * tsq, d)]
    dims = [(bp, tp), (bs, tsq)]
    rows = [(0, False), (1, True)]
    ropes = [None, _rope_tables(tsq)]
    kpe_c = jnp.pad(cache_kpe[:, 0].reshape(bs * past, MLA_ROPE), ((0, 0), (0, LANES - MLA_ROPE)))
    caches = [None, _kvup(cache_ckv[:, 0].reshape(bs * past, -1), kpe_c, wts)]
    inits = [None, state_hgrn[:, 0]]

    x1s, h2s, affs, extras = [], [], [], []
    for gi in range(2):
        (bt, sq), (row0, per_batch) = dims[gi], rows[gi]
        if sq == TOK_BLOCK and caches[gi] is None and ropes[gi] is None and not per_batch:
            x1, h2, aff, ckv, kpe, s_fin = _mixer(xs[gi], bt, sq, mod, hgrn_gamma, wts, alpha, ne, row0)
        else:
            q, k, v, ckv, kpe, hgx = _inproj(xs[gi], bt, sq, mod, hgrn_gamma, wts, row0, per_batch, ropes[gi])
            o_f, o_b, s_fin = _hgrn(hgx, bt, sq, inits[gi])
            x1, h2, aff = _postmix(xs[gi], bt, sq, mod, q, k, v, caches[gi], o_f, o_b, hgx, wts, alpha, ne,
                                   row0, per_batch)
        x1s.append(x1)
        h2s.append(h2)
        affs.append(aff)
        extras.append((ckv, kpe, s_fin))

    caps = [EC_FACTOR * x.shape[0] // ne for x in xs]
    slots = sum(caps)
    assert all(cp % BF16_ROWS == 0 for cp in caps) and slots >= WIN_ROWS
    groups, scheds = [], []
    slot0 = 0
    for gi in range(2):
        rank, cnt = _route(affs[gi], caps[gi])
        groups.append(dict(h2=h2s[gi], rank=rank, aff=affs[gi], slot0=slot0))
        scheds.append(_window_sched(cnt, slot0))
        slot0 += caps[gi]
    xe, ge = _compact(groups, *[jnp.concatenate([s[k] for s in scheds]) for k in range(3)], slots)
    ye = _ffn(xe, ge, moe_w1[0], moe_w3[0], moe_w2[0], slots, ft=moe_w1.shape[-1])

    outs = []
    for gi in range(2):
        outs.append(_combine(ye, groups[gi]["rank"], scheds[gi][0], scheds[gi][2], x1s[gi], mod,
                             ln2_g[0].reshape(1, -1), ln2_b[0].reshape(1, -1), alpha, groups[gi]["slot0"],
                             rows[gi][0], dims[gi][1] // TOK_BLOCK if rows[gi][1] else 1 << 30))

    ckv_p, kpe_p, st_p = extras[0]
    y_prompt = outs[0].reshape(bp, tp, d)
    y_sample = outs[1].reshape(bs, tsq, d)
    new_ckv = ckv_p.reshape(bp, 1, tp, -1)
    new_kpe = kpe_p[:, :MLA_ROPE].reshape(bp, 1, tp, MLA_ROPE)
    new_state = st_p.reshape(bp, 1, 2, HG_HEADS, HG_DK, HG_DV)
    return (y_prompt, y_sample, new_ckv, new_kpe, new_state)
```

```python
import functools

import jax
import jax.numpy as jnp
import numpy as np
from jax import lax
from jax.experimental import pallas as pl
from jax.experimental.pallas import tpu as pltpu

F32 = jnp.float32
BF16 = jnp.bfloat16

MLA_HEADS = 8
MLA_NOPE = 64
MLA_ROPE = 32
MLA_V = 64
HG_HEADS = 4
HG_DK = 128
HG_DV = 128
HG_CHUNK = 32
GRID_W = 64
ROPE_BASE = 10000.0
EC_FACTOR = 2
EPS = 1e-6

LANES = 128
SUBLANES = 8
BF16_ROWS = 16
VMEM_LIMIT = 56 * 1024 * 1024

TOK_BLOCK = 256
WIN_ROWS = 64
UNSELECTED = -(1 << 30)

NT_DIMS = (((1,), (1,)), ((), ()))


def _dot(a, b):
    return jnp.dot(a, b, preferred_element_type=F32)


def _dot_nt(a, b):
    return lax.dot_general(a, b, NT_DIMS, preferred_element_type=F32)


def _silu(x):
    return x * jax.nn.sigmoid(x)


def _params(*sem):
    return pltpu.CompilerParams(dimension_semantics=sem, vmem_limit_bytes=VMEM_LIMIT)


def _const_spec(shape):
    zeros = (0,) * len(shape)
    return pl.BlockSpec(shape, lambda *_: zeros, pipeline_mode=pl.Buffered(1))


def _adaln_kernel(c_ref, w_ref, b_ref, o_ref):
    s = _silu(c_ref[...]).astype(BF16)
    o_ref[...] = _dot(s, w_ref[...].astype(BF16)) + b_ref[...]


def _adaln(cond, w_ada, b_ada):
    rows, d = cond.shape
    n = w_ada.shape[1]
    tn = n // 4
    return pl.pallas_call(
        _adaln_kernel,
        out_shape=jax.ShapeDtypeStruct((rows, n), F32),
        grid=(n // tn,),
        in_specs=[_const_spec((rows, d)),
                  pl.BlockSpec((d, tn), lambda j: (0, j)),
                  pl.BlockSpec((1, tn), lambda j: (0, j))],
        out_specs=pl.BlockSpec((rows, tn), lambda j: (0, j)),
        compiler_params=_params("arbitrary"),
        name="adaln",
    )(cond, w_ada, b_ada.reshape(1, n))


def _rms(x, g):
    return x * lax.rsqrt(jnp.mean(x * x, axis=-1, keepdims=True) + EPS) * g


def _rope(x, c, s):
    w = x.shape[-1]
    lane = lax.broadcasted_iota(jnp.int32, x.shape, 1)
    nxt = pltpu.roll(x, w - 1, 1)
    prv = pltpu.roll(x, 1, 1)
    return x * c + jnp.where(lane % 2 == 0, nxt, prv) * s


N_INPROJ_WEIGHTS = 6


def _mod_row(mod_ref, row0, per_batch):
    r = row0 + pl.program_id(0) if per_batch else row0
    return mod_ref[pl.ds(r, 1), :]


def _modulated(x_ref, m):
    d = x_ref.shape[1]
    return (x_ref[...] * (1.0 + m[:, d:2 * d]) + m[:, 0:d]).astype(BF16)


def _keys(k_nope, kpe):
    shared = pltpu.roll(kpe, MLA_NOPE, 1)
    return (k_nope + jnp.concatenate([shared] * MLA_HEADS, axis=1)).astype(BF16)


def _inproj_kernel(*refs, row0, per_batch, rope):
    x_ref, mod_ref = refs[:2]
    _inproj_body(x_ref, _mod_row(mod_ref, row0, per_batch), *refs[2:], rope=rope)


def _inproj_body(x_ref, m, *refs, rope):
    gam_ref, win_ref, qn_ref, wuq_ref, kvn_ref, wk_ref, wv_ref = refs[:1 + N_INPROJ_WEIGHTS]
    refs = refs[1 + N_INPROJ_WEIGHTS:]
    if rope:
        cq_ref, sq_ref, ck_ref, sk_ref = refs[:4]
        refs = refs[4:]
    q_o, k_o, v_o, ckv_o, kpe_o, hgx_o = refs
    h = _modulated(x_ref, m)
    hw = HG_HEADS * HG_DK
    o_kv = qn_ref.shape[1]
    o_pe = o_kv + kvn_ref.shape[1]
    o_h = o_pe + MLA_ROPE

    cq = _rms(_dot_nt(h, win_ref[0:o_kv, :]), qn_ref[...])
    q = _dot(cq.astype(BF16), wuq_ref[...])
    if rope:
        q = _rope(q, jnp.concatenate([cq_ref[...]] * MLA_HEADS, axis=1),
                  jnp.concatenate([sq_ref[...]] * MLA_HEADS, axis=1))
    q_o[...] = q.astype(BF16)

    ckv = _rms(_dot_nt(h, win_ref[o_kv:o_pe, :]), kvn_ref[...])
    ckv_o[...] = ckv
    kpe = _dot_nt(h, win_ref[o_pe:o_h, :])
    kpe = jnp.concatenate([kpe, jnp.zeros((kpe.shape[0], LANES - MLA_ROPE), F32)], axis=1)
    if rope:
        kpe = _rope(kpe, ck_ref[...], sk_ref[...])
    kpe_o[...] = kpe
    cb = ckv.astype(BF16)
    k_o[...] = _keys(_dot(cb, wk_ref[...]), kpe)
    v_o[...] = _dot(cb, wv_ref[...]).astype(BF16)

    z = _dot_nt(h, win_ref[o_h:o_h + 5 * hw, :])
    hgx_o[:, 0:hw] = _silu(z[:, 0:hw])
    for dr in range(2):
        g0, g1 = gam_ref[dr, 0:1, :], gam_ref[dr, 1:2, :]
        gmax = jnp.maximum(g0, g1)
        e0, e1 = jnp.exp(g0 - gmax), jnp.exp(g1 - gmax)
        lb = e0 / (e0 + e1)
        f = lb + (1.0 - lb) * jax.nn.sigmoid(z[:, (1 + dr) * hw:(2 + dr) * hw])
        hgx_o[:, (1 + 2 * dr) * hw:(2 + 2 * dr) * hw] = jnp.log(f)
        hgx_o[:, (2 + 2 * dr) * hw:(3 + 2 * dr) * hw] = 1.0 - f
    hgx_o[:, 5 * hw:6 * hw] = z[:, 3 * hw:4 * hw]
    hgx_o[:, 6 * hw:7 * hw] = z[:, 4 * hw:5 * hw]


def _inproj(x2d, batch, seq, mod, gamma, wts, row0, per_batch, rope_tabs):
    n, d = x2d.shape
    tm = MIXER_SEQS * TOK_BLOCK if seq % (MIXER_SEQS * TOK_BLOCK) == 0 else TOK_BLOCK
    nblk = seq // tm
    rope = rope_tabs is not None
    hp = MLA_HEADS * LANES
    hw = HG_HEADS * HG_DK
    tok = lambda b, i: (b * nblk + i, 0)
    pos = lambda b, i: (i, 0)
    weights = [wts[k] for k in INPROJ_KEYS]
    ins = [x2d, mod, gamma] + weights
    in_specs = ([pl.BlockSpec((tm, d), tok), _const_spec(mod.shape), _const_spec(gamma.shape)]
                + [_const_spec(w.shape) for w in weights])
    if rope:
        ins += list(rope_tabs)
        in_specs += [pl.BlockSpec((tm, t.shape[1]), pos) for t in rope_tabs]
    widths = [(hp, BF16), (hp, BF16), (MLA_HEADS * MLA_V, BF16), (wts["kvn"].shape[1], F32), (LANES, F32),
              (7 * hw, F32)]
    return pl.pallas_call(
        functools.partial(_inproj_kernel, row0=row0, per_batch=per_batch, rope=rope),
        out_shape=[jax.ShapeDtypeStruct((n, w), dt) for w, dt in widths],
        grid=(batch, nblk),
        in_specs=in_specs,
        out_specs=[pl.BlockSpec((tm, w), tok) for w, _ in widths],
        compiler_params=_params("arbitrary", "arbitrary"),
        name="inproj",
    )(*ins)


def _kvup_kernel(ckv_ref, kpe_ref, wk_ref, wv_ref, k_o, v_o):
    cb = ckv_ref[...].astype(BF16)
    k_o[...] = _keys(_dot(cb, wk_ref[...]), kpe_ref[...])
    v_o[...] = _dot(cb, wv_ref[...]).astype(BF16)


def _kvup(ckv2d, kpe2d, wts):
    n = ckv2d.shape[0]
    tm = TOK_BLOCK
    widths = [MLA_HEADS * LANES, MLA_HEADS * MLA_V]
    row = lambda i: (i, 0)
    ws = [wts["wk"], wts["wv"]]
    return pl.pallas_call(
        _kvup_kernel,
        out_shape=[jax.ShapeDtypeStruct((n, w), BF16) for w in widths],
        grid=(n // tm,),
        in_specs=[pl.BlockSpec((tm, ckv2d.shape[1]), row), pl.BlockSpec((tm, LANES), row)]
                 + [_const_spec(w.shape) for w in ws],
        out_specs=[pl.BlockSpec((tm, w), row) for w in widths],
        compiler_params=_params("arbitrary"),
        name="kvup",
    )(ckv2d, kpe2d, *ws)


ATTN_SCALE = (MLA_NOPE + MLA_ROPE) ** -0.5


def _attn_body(q_ref, k_ref, v_ref, kc_ref, vc_ref, o_ref):
    cached = kc_ref is not None
    scale = ATTN_SCALE
    per_slab = LANES // MLA_V
    own = lax.broadcasted_iota(jnp.int32, (q_ref.shape[0], LANES), 1) // MLA_V
    for slab in range(MLA_HEADS // per_slab):
        vsl = slice(slab * LANES, (slab + 1) * LANES)
        out = None
        for sub in range(per_slab):
            hd = slab * per_slab + sub
            sl = slice(hd * LANES, (hd + 1) * LANES)
            q = q_ref[:, sl]
            s = _dot_nt(q, k_ref[:, sl]) * scale
            mx = jnp.max(s, axis=-1, keepdims=True)
            if cached:
                s2 = _dot_nt(q, kc_ref[:, sl]) * scale
                mx = jnp.maximum(mx, jnp.max(s2, axis=-1, keepdims=True))
            e = jnp.exp(s - mx)
            den = jnp.sum(e, axis=-1, keepdims=True)
            o = _dot(e.astype(BF16), v_ref[:, vsl])
            if cached:
                e2 = jnp.exp(s2 - mx)
                den = den + jnp.sum(e2, axis=-1, keepdims=True)
                o = o + _dot(e2.astype(BF16), vc_ref[:, vsl])
            o = o / den
            out = o if out is None else jnp.where(own == sub, o, out)
        o_ref[:, vsl] = out.astype(o_ref.dtype)


def _chunk_scan(x, reverse):
    tm = x.shape[0]
    rin = lax.broadcasted_iota(jnp.int32, x.shape, 0) % HG_CHUNK
    step = 1
    while step < HG_CHUNK:
        if reverse:
            x = x + jnp.where(rin < HG_CHUNK - step, pltpu.roll(x, tm - step, 0), 0.0)
        else:
            x = x + jnp.where(rin >= step, pltpu.roll(x, step, 0), 0.0)
        step *= 2
    return x


def _hgrn_kernel(*refs, has_init):
    fwd, bwd = refs[0:4], refs[4:8]
    refs = refs[8:]
    s0_ref = None
    if has_init:
        s0_ref = refs[0]
        refs = refs[1:]
    of_ref, ob_ref, sfin_ref, st_scr = refs
    i = pl.program_id(1)
    _hgrn_body(fwd, bwd, s0_ref, of_ref, ob_ref, sfin_ref, st_scr, i == 0, i == pl.num_programs(1) - 1)


def _hgrn_body(fwd, bwd, s0_ref, of_ref, ob_ref, sfin_ref, st_scr, first, last):
    tm = fwd[0].shape[0]
    c = HG_CHUNK
    nch = tm // c
    dk, dv = HG_DK, HG_DV
    hw = HG_HEADS * dk

    def initial(dr, hd):
        return s0_ref[0, dr, hd].T if s0_ref is not None else jnp.zeros((dv, dk), F32)

    if st_scr is not None:
        @pl.when(first)
        def _init():
            for dr in range(2):
                for hd in range(HG_HEADS):
                    st_scr[dr, hd] = initial(dr, hd)

    npair = nch // 2
    pair = 2 * c
    row = lax.broadcasted_iota(jnp.int32, (tm, tm), 0)
    col = lax.broadcasted_iota(jnp.int32, (tm, tm), 1)
    same = (row // c) == (col // c)
    same_pair = (row // pair) == (col // pair)
    bd = (lax.broadcasted_iota(jnp.int32, (tm, npair * dk), 0) // pair
          == lax.broadcasted_iota(jnp.int32, (tm, npair * dk), 1) // dk)
    chunk_odd = (lax.broadcasted_iota(jnp.int32, (tm, hw), 0) // c) % 2 == 1

    for dr, (hq_ref, lf_ref, kk_ref, vv_ref) in enumerate((fwd, bwd)):
        o_ref = of_ref if dr == 0 else ob_ref
        tri = same & ((col <= row) if dr == 0 else (col >= row))
        cross = same_pair & (((row // c) > (col // c)) if dr == 0 else ((row // c) < (col // c)))
        second = chunk_odd if dr == 0 else ~chunk_odd
        bcum = _chunk_scan(lf_ref[...], reverse=dr == 1)
        closing = c - 1 if dr == 0 else 0
        btot3 = bcum.reshape(nch, c, hw)[:, closing:closing + 1, :]
        btot = jnp.broadcast_to(btot3, (nch, c, hw)).reshape(tm, hw)
        bpart = jnp.where(chunk_odd, pltpu.roll(btot, c, 0), pltpu.roll(btot, tm - c, 0))
        epart = jnp.exp(bpart)
        bpair = btot + bpart
        kk = kk_ref[...]
        qd = hq_ref[...] * jnp.exp(bcum)
        kd = kk * jnp.exp(-bcum)
        ke = kk * jnp.exp(btot - bcum)
        qd2 = jnp.where(second, qd * epart, qd)
        ke2 = jnp.where(second, ke, ke * epart)
        vv = vv_ref[...]
        order = range(npair) if dr == 0 else range(npair - 1, -1, -1)
        for hd in range(HG_HEADS):
            sl = slice(hd * dk, (hd + 1) * dk)
            qd_h = qd[:, sl].astype(BF16)
            v_h = vv[:, hd * dv:(hd + 1) * dv]
            a = jnp.where(tri, _dot_nt(qd_h, kd[:, sl].astype(BF16)),
                          jnp.where(cross, _dot_nt(qd_h, ke[:, sl].astype(BF16)), 0.0))
            o_intra = _dot(a.astype(BF16), v_h.astype(BF16))
            kebd = jnp.where(bd, jnp.concatenate([ke2[:, sl]] * npair, axis=1), 0.0).astype(BF16)
            qbd = jnp.where(bd, jnp.concatenate([qd2[:, sl]] * npair, axis=1), 0.0).astype(BF16)
            ut = _dot(v_h.T.astype(BF16), kebd)
            st = st_scr[dr, hd] if st_scr is not None else initial(dr, hd)
            prev = [None] * npair
            for p in order:
                prev[p] = st
                st = st * jnp.exp(bpair[p * pair:p * pair + 1, sl]) + ut[:, p * dk:(p + 1) * dk]
            if st_scr is not None:
                st_scr[dr, hd] = st
            o_inter = _dot_nt(qbd, jnp.concatenate(prev, axis=1).astype(BF16))
            o_ref[:, hd * dv:(hd + 1) * dv] = o_intra + o_inter

            if last is True:
                sfin_ref[0, dr, hd] = st.T
            else:
                @pl.when(last)
                def _final(st=st, dr=dr, hd=hd):
                    sfin_ref[0, dr, hd] = st.T


def _hgrn(hgx, batch, seq, s0=None):
    n = hgx.shape[0]
    tm = TOK_BLOCK
    nblk = seq // tm
    hw = HG_HEADS * HG_DK

    def spec(lane_blk, rev):
        if rev:
            return pl.BlockSpec((tm, hw), lambda b, i: (b * nblk + nblk - 1 - i, lane_blk))
        return pl.BlockSpec((tm, hw), lambda b, i: (b * nblk + i, lane_blk))

    in_specs = [spec(0, False), spec(1, False), spec(2, False), spec(5, False),
                spec(0, True), spec(3, True), spec(4, True), spec(5, True)]
    ins = [hgx] * 8
    st_shape = (1, 2, HG_HEADS, HG_DK, HG_DV)
    st_spec = pl.BlockSpec(st_shape, lambda b, i: (b, 0, 0, 0, 0))
    if s0 is not None:
        ins.append(s0)
        in_specs.append(st_spec)
    return pl.pallas_call(
        functools.partial(_hgrn_kernel, has_init=s0 is not None),
        out_shape=[jax.ShapeDtypeStruct((n, hw), F32), jax.ShapeDtypeStruct((n, hw), F32),
                   jax.ShapeDtypeStruct((batch,) + st_shape[1:], F32)],
        grid=(batch, nblk),
        in_specs=in_specs,
        out_specs=[spec(0, False), spec(0, True), st_spec],
        scratch_shapes=[pltpu.VMEM((2, HG_HEADS, HG_DV, HG_DK), F32)],
        compiler_params=_params("arbitrary", "arbitrary"),
        name="hgrn",
    )(*ins)


def _layer_norm(x, g, b):
    xc = x - jnp.mean(x, axis=-1, keepdims=True)
    var = jnp.mean(xc * xc, axis=-1, keepdims=True)
    return xc * lax.rsqrt(var + EPS) * g + b


N_POSTMIX_WEIGHTS = 8
INPROJ_KEYS = ("win", "qn", "wuq", "kvn", "wk", "wv")
POSTMIX_KEYS = ("hgn", "womla", "wohg", "wout", "ln1g", "ln1b", "wr")
MIXER_SEQS = 2


def _postmix_kernel(x_ref, mod_ref, *refs, alpha, row0, per_batch, cached):
    n_attn = 5 if cached else 3
    q_ref, k_ref, v_ref = refs[:3]
    kc_ref, vc_ref = refs[3:5] if cached else (None, None)
    of_ref, ob_ref, zg_ref = refs[n_attn:n_attn + 3]
    om_s = refs[-1]
    _attn_body(q_ref, k_ref, v_ref, kc_ref, vc_ref, om_s)
    _postmix_body(x_ref, _mod_row(mod_ref, row0, per_batch), of_ref, ob_ref, zg_ref, om_s,
                  *refs[n_attn + 3:-1], alpha=alpha)


def _postmix_body(x_ref, m, of_ref, ob_ref, zg_ref, om_ref, wg_ref, hgn_ref, womla_ref,
                  wohg_ref, wout_ref, lng_ref, lnb_ref, wr_ref, x1_o, h2_o, aff_o, *, alpha):
    d = x_ref.shape[1]
    tb = aff_o.shape[2]
    g1, sh2, sc2 = m[:, 2 * d:3 * d], m[:, 3 * d:4 * d], m[:, 4 * d:5 * d]
    o = of_ref[...] + ob_ref[...]
    zg = zg_ref[...]
    parts = []
    for hd in range(HG_HEADS):
        sl = slice(hd * HG_DV, (hd + 1) * HG_DV)
        parts.append(_rms(o[:, sl], hgn_ref[...]) * _silu(zg[:, sl]))
    ohg = jnp.concatenate(parts, axis=1).astype(BF16)
    gates = _dot_nt(_modulated(x_ref, m), wg_ref[wg_ref.shape[0] - 2 * d:, :])
    merged = (jax.nn.sigmoid(gates[:, 0:d]) * _dot(om_ref[...], womla_ref[...])
              + jax.nn.sigmoid(gates[:, d:2 * d]) * _dot(ohg, wohg_ref[...]))
    mix = _dot(merged.astype(BF16), wout_ref[...])
    x1 = _layer_norm(alpha * x_ref[...] + g1 * mix, lng_ref[...], lnb_ref[...])
    x1_o[...] = x1
    h2 = (x1 * (1.0 + sc2) + sh2).astype(BF16)
    h2_o[...] = h2
    logits = _dot_nt(wr_ref[...], h2)
    e = jnp.exp(logits - jnp.max(logits, axis=0, keepdims=True))
    aff = e / jnp.sum(e, axis=0, keepdims=True)
    for blk in range(aff_o.shape[0]):
        aff_o[blk] = aff[:, blk * tb:(blk + 1) * tb]


def _postmix(x2d, batch, seq, mod, q, k, v, cache, o_f, o_b, hgx, wts, alpha, n_experts, row0, per_batch):
    n, d = x2d.shape
    tm = TOK_BLOCK
    nblk = seq // tm
    hw = HG_HEADS * HG_DV
    hp, hv = q.shape[1], v.shape[1]
    tok = lambda b, i: (b * nblk + i, 0)
    per_seq = lambda b, i: (b, 0)
    weights = [wts[k] for k in POSTMIX_KEYS]
    attn_ins = [q, k, v]
    attn_specs = [pl.BlockSpec((tm, hp), tok), pl.BlockSpec((seq, hp), per_seq), pl.BlockSpec((seq, hv), per_seq)]
    if cache is not None:
        past = cache[0].shape[0] // batch
        attn_ins += list(cache)
        attn_specs += [pl.BlockSpec((past, hp), per_seq), pl.BlockSpec((past, hv), per_seq)]
    return pl.pallas_call(
        functools.partial(_postmix_kernel, alpha=alpha, row0=row0, per_batch=per_batch,
                          cached=cache is not None),
        out_shape=[jax.ShapeDtypeStruct((n, d), F32), jax.ShapeDtypeStruct((n, d), BF16),
                   jax.ShapeDtypeStruct((n // tm, n_experts, tm), F32)],
        grid=(batch, nblk),
        in_specs=[pl.BlockSpec((tm, d), tok), _const_spec(mod.shape)] + attn_specs
                 + [pl.BlockSpec((tm, hw), tok), pl.BlockSpec((tm, hw), tok),
                    pl.BlockSpec((tm, hw), lambda b, i: (b * nblk + i, 6)), _const_spec(wts["win"].shape)]
                 + [_const_spec(w.shape) for w in weights],
        out_specs=[pl.BlockSpec((tm, d), tok), pl.BlockSpec((tm, d), tok),
                   pl.BlockSpec((1, n_experts, tm), lambda b, i: (b * nblk + i, 0, 0))],
        scratch_shapes=[pltpu.VMEM((tm, hv), BF16)],
        compiler_params=_params("arbitrary", "arbitrary"),
        name="postmix",
    )(x2d, mod, *attn_ins, o_f, o_b, hgx, wts["win"], *weights)


def _mixer_kernel(x_ref, mod_ref, *refs, alpha, row0, seq):
    nw = 1 + N_INPROJ_WEIGHTS
    in_w, refs = refs[:nw], refs[nw:]
    pm_w, refs = refs[:N_POSTMIX_WEIGHTS - 1], refs[N_POSTMIX_WEIGHTS - 1:]
    pm_w = (in_w[1],) + tuple(pm_w)
    x1_o, h2_o, aff_o, ckv_o, kpe_o, sfin_o, q_s, k_s, v_s, hgx_s, om_s, of_s, ob_s = refs
    m = _mod_row(mod_ref, row0, False)
    _inproj_body(x_ref, m, *in_w, q_s, k_s, v_s, ckv_o, kpe_o, hgx_s, rope=False)
    hw = HG_HEADS * HG_DK
    for s in range(x_ref.shape[0] // seq):
        rows = slice(s * seq, (s + 1) * seq)
        _attn_body(q_s.at[rows], k_s.at[rows], v_s.at[rows], None, None, om_s.at[rows])
        lane = lambda j: hgx_s.at[rows, j * hw:(j + 1) * hw]
        _hgrn_body((lane(0), lane(1), lane(2), lane(5)), (lane(0), lane(3), lane(4), lane(5)), None,
                   of_s.at[rows], ob_s.at[rows], sfin_o.at[s:s + 1], None, True, True)
    _postmix_body(x_ref, m, of_s, ob_s, hgx_s.at[:, 6 * hw:7 * hw], om_s, *pm_w, x1_o, h2_o, aff_o,
                  alpha=alpha)


def _mixer(x2d, batch, seq, mod, gamma, wts, alpha, n_experts, row0):
    n, d = x2d.shape
    assert seq == TOK_BLOCK
    ns = MIXER_SEQS if batch % MIXER_SEQS == 0 else 1
    tm = ns * seq
    hp = MLA_HEADS * LANES
    hv = MLA_HEADS * MLA_V
    hw = HG_HEADS * HG_DK
    kvl = wts["kvn"].shape[1]
    weights = [wts[k] for k in INPROJ_KEYS + POSTMIX_KEYS]
    tok = lambda b: (b, 0)
    st_shape = (ns, 2, HG_HEADS, HG_DK, HG_DV)
    return pl.pallas_call(
        functools.partial(_mixer_kernel, alpha=alpha, row0=row0, seq=seq),
        out_shape=[jax.ShapeDtypeStruct((n, d), F32), jax.ShapeDtypeStruct((n, d), BF16),
                   jax.ShapeDtypeStruct((n // seq, n_experts, seq), F32),
                   jax.ShapeDtypeStruct((n, kvl), F32), jax.ShapeDtypeStruct((n, LANES), F32),
                   jax.ShapeDtypeStruct((batch,) + st_shape[1:], F32)],
        grid=(batch // ns,),
        in_specs=[pl.BlockSpec((tm, d), tok), _const_spec(mod.shape), _const_spec(gamma.shape)]
                 + [_const_spec(w.shape) for w in weights],
        out_specs=[pl.BlockSpec((tm, d), tok), pl.BlockSpec((tm, d), tok),
                   pl.BlockSpec((ns, n_experts, seq), lambda b: (b, 0, 0)),
                   pl.BlockSpec((tm, kvl), tok), pl.BlockSpec((tm, LANES), tok),
                   pl.BlockSpec(st_shape, lambda b: (b, 0, 0, 0, 0))],
        scratch_shapes=[pltpu.VMEM((tm, hp), BF16), pltpu.VMEM((tm, hp), BF16), pltpu.VMEM((tm, hv), BF16),
                        pltpu.VMEM((tm, 7 * hw), F32), pltpu.VMEM((tm, hv), BF16),
                        pltpu.VMEM((tm, hw), F32), pltpu.VMEM((tm, hw), F32)],
        compiler_params=_params("arbitrary"),
        name="mixer",
    )(x2d, mod, gamma, *weights)


def _route_kernel(aff_ref, rank_o, cnt_o, *, cap):
    nb, ne, tb = aff_ref.shape
    key = aff_ref[...]

    def count(mask):
        return jnp.sum(jnp.sum(jnp.where(mask, 1.0, 0.0), axis=0), axis=1, keepdims=True)

    def bit_step(it, bits):
        cand = bits | jnp.left_shift(jnp.int32(1), 30 - it)
        return jnp.where(count(key >= pltpu.bitcast(cand, F32)[None]) >= cap, cand, bits)

    bits = lax.fori_loop(0, 31, bit_step, jnp.zeros((ne, 1), jnp.int32))
    thr = pltpu.bitcast(bits, F32)
    need = cap - count(key > thr[None])
    before = (lax.broadcasted_iota(jnp.int32, (tb, tb), 0)
              < lax.broadcasted_iota(jnp.int32, (tb, tb), 1))
    before = jnp.where(before, 1.0, 0.0).astype(BF16)
    off_eq = jnp.zeros((ne, 1), F32)
    off_sel = jnp.zeros((ne, 1), F32)
    cnt_o[...] = jnp.zeros_like(cnt_o)
    for blk in range(nb):
        key_b = key[blk]
        eq = key_b == thr
        eq_b = jnp.where(eq, 1.0, 0.0)
        eq_rank = _dot(eq_b.astype(BF16), before) + off_eq
        sel = (key_b > thr) | (eq & (eq_rank < need))
        sel_b = jnp.where(sel, 1.0, 0.0)
        rank = _dot(sel_b.astype(BF16), before) + off_sel
        rank_o[blk] = jnp.where(sel, rank.astype(jnp.int32), UNSELECTED)
        cnt_o[:, blk:blk + 1] = off_sel.astype(jnp.int32)
        off_eq = off_eq + jnp.sum(eq_b, axis=1, keepdims=True)
        off_sel = off_sel + jnp.sum(sel_b, axis=1, keepdims=True)
    cnt_o[:, nb:nb + 1] = off_sel.astype(jnp.int32)


def _route(aff, cap):
    nb, ne, tb = aff.shape
    assert nb + 1 <= LANES
    rank, cnt = pl.pallas_call(
        functools.partial(_route_kernel, cap=cap),
        out_shape=[jax.ShapeDtypeStruct(aff.shape, jnp.int32), jax.ShapeDtypeStruct((ne, LANES), jnp.int32)],
        in_specs=[pl.BlockSpec(memory_space=pltpu.VMEM)],
        out_specs=[pl.BlockSpec(memory_space=pltpu.VMEM)] * 2,
        compiler_params=pltpu.CompilerParams(vmem_limit_bytes=VMEM_LIMIT),
        name="route",
    )(aff)
    return rank, cnt[:, :nb + 1]


def _window_hits(rk_ref, firsts, slot0, win):
    ne, tb = rk_ref.shape[1], rk_ref.shape[2]
    win_iota = lax.broadcasted_iota(jnp.int32, (win, tb), 0)
    return [(rk_ref[0, e:e + 1, :] + (slot0 - firsts[e])) == win_iota for e in range(ne)]


def _compact_kernel(first_ref, end_ref, rounds_ref, *refs, groups, slots):
    ng = len(groups)
    h2_refs, rk_refs, af_refs = refs[0:ng], refs[ng:2 * ng], refs[2 * ng:3 * ng]
    hbms, refs = refs[3 * ng:3 * ng + 2], refs[3 * ng + 2:]
    stages, tails = refs[0:2], refs[2:4]
    sem, issued = refs[4:]
    b = pl.program_id(0)
    ne = rk_refs[0].shape[1]
    win = WIN_ROWS
    sub = BF16_ROWS

    def copies(slot, dsts):
        return [pltpu.make_async_copy(stage.at[slot, pl.ds(e * win, win), :],
                                      hbm.at[e, pl.ds(pl.multiple_of(dsts[e], sub), win), :], sem.at[c, e])
                for c, (stage, hbm) in enumerate(zip(stages, hbms)) for e in range(ne)]

    def wait_previous():
        @pl.when(issued[0] > 0)
        def _():
            for cp in copies(0, [0] * ne):
                cp.wait()

    @pl.when(b == 0)
    def _init():
        issued[0] = 0
        for stage, tail in zip(stages, tails):
            tail[...] = jnp.zeros_like(tail)
            stage[1] = jnp.zeros(stage.shape[1:], stage.dtype)
        pad = copies(1, [slots] * ne)
        for cp in pad:
            cp.start()
        for cp in pad:
            cp.wait()

    def group_body(h2_ref, rk_ref, af_ref, slot0):
        firsts = [first_ref[b * ne + e] for e in range(ne)]
        bases = [(f // sub) * sub for f in firsts]
        ends = [end_ref[b * ne + e] - bases[e] for e in range(ne)]

        def one_round(r, carry):
            dsts = [bases[e] + r * win for e in range(ne)]
            hits = _window_hits(rk_ref, dsts, slot0, win)
            onehot = jnp.where(jnp.concatenate(hits, axis=0), 1.0, 0.0).astype(BF16)
            gate = jnp.concatenate(
                [jnp.sum(jnp.where(hits[e], af_ref[0, e:e + 1, :], 0.0), axis=1, keepdims=True)
                 for e in range(ne)], axis=0)
            slot = issued[0] % 2
            stages[0][slot] = _dot(onehot, h2_ref[...]).astype(BF16)
            stages[1][slot] = jnp.broadcast_to(gate, (ne * win, LANES))
            @pl.when(r == 0)
            def _heads():
                for stage, tail in zip(stages, tails):
                    sub_iota = lax.broadcasted_iota(jnp.int32, (sub, stage.shape[2]), 0)
                    for e in range(ne):
                        head = stage[slot, e * win:e * win + sub, :]
                        stage[slot, e * win:e * win + sub, :] = jnp.where(
                            sub_iota < firsts[e] - bases[e], tail[e * sub:(e + 1) * sub, :], head)

            for stage, tail in zip(stages, tails):
                for e in range(ne):
                    last = (ends[e] // sub) * sub
                    group = stage[slot, pl.ds(pl.multiple_of(e * win + last % win, sub), sub), :]
                    tail[e * sub:(e + 1) * sub, :] = jnp.where(r == last // win, group,
                                                               tail[e * sub:(e + 1) * sub, :])
            wait_previous()
            for cp in copies(slot, [jnp.minimum(dst, slots) for dst in dsts]):
                cp.start()
            issued[0] = issued[0] + 1
            return carry

        lax.fori_loop(0, rounds_ref[b], one_round, 0)

    blk0 = 0
    for gi, g in enumerate(groups):
        @pl.when((b >= blk0) & (b < blk0 + g["nb"]))
        def _(gi=gi, g=g):
            group_body(h2_refs[gi], rk_refs[gi], af_refs[gi], g["slot0"])
        blk0 += g["nb"]

    @pl.when(b == pl.num_programs(0) - 1)
    def _drain():
        wait_previous()


def _compact(groups, first, end, rounds, slots):
    d = groups[0]["h2"].shape[1]
    nbs = [g["rank"].shape[0] for g in groups]
    ne, tb = groups[0]["rank"].shape[1:]
    meta, specs_h2, specs_rk = [], [], []
    blk0 = 0
    for g, nb in zip(groups, nbs):
        meta.append(dict(nb=nb, slot0=g["slot0"]))
        local = lambda b, *_, blk0=blk0, nb=nb: jnp.clip(b - blk0, 0, nb - 1)
        specs_h2.append(pl.BlockSpec((tb, d), lambda b, *_, local=local: (local(b), 0)))
        specs_rk.append(pl.BlockSpec((1, ne, tb), lambda b, *_, local=local: (local(b), 0, 0)))
        blk0 += nb
    streams = [(d, BF16), (LANES, F32)]
    return pl.pallas_call(
        functools.partial(_compact_kernel, groups=meta, slots=slots),
        out_shape=[jax.ShapeDtypeStruct((ne, slots + WIN_ROWS, w), dt) for w, dt in streams],
        grid_spec=pltpu.PrefetchScalarGridSpec(
            num_scalar_prefetch=3,
            grid=(sum(nbs),),
            in_specs=specs_h2 + specs_rk + specs_rk,
            out_specs=[pl.BlockSpec(memory_space=pl.ANY)] * 2,
            scratch_shapes=[pltpu.VMEM((2, ne * WIN_ROWS, w), dt) for w, dt in streams]
                           + [pltpu.VMEM((ne * BF16_ROWS, w), dt) for w, dt in streams]
                           + [pltpu.SemaphoreType.DMA((2, ne)), pltpu.SMEM((1,), jnp.int32)]),
        compiler_params=_params("arbitrary"),
        name="compact",
    )(first, end, rounds, *[g["h2"] for g in groups], *[g["rank"] for g in groups],
      *[g["aff"] for g in groups])


def _ffn_kernel(xe_ref, ge_ref, w1_ref, w3_ref, w2_ref, ye_ref, *scratch):
    f = pl.program_id(1)
    x = xe_ref[0]
    hid = _silu(_dot(x, w1_ref[0].astype(BF16))) * _dot(x, w3_ref[0].astype(BF16))
    y = _dot(hid.astype(BF16), w2_ref[0].astype(BF16))
    if not scratch:
        ye_ref[0] = (y * ge_ref[0, :, 0:1]).astype(ye_ref.dtype)
        return
    acc_scr, = scratch
    last = pl.num_programs(1) - 1

    @pl.when(f == 0)
    def _first():
        acc_scr[...] = y

    @pl.when((f > 0) & (f < last))
    def _middle():
        acc_scr[...] += y

    @pl.when(f == last)
    def _last():
        ye_ref[0] = ((acc_scr[...] + y) * ge_ref[0, :, 0:1]).astype(ye_ref.dtype)


def _ffn(xe, ge, w1, w3, w2, slots, ft):
    ne, d, dff = w1.shape
    nf = dff // ft
    return pl.pallas_call(
        _ffn_kernel,
        out_shape=jax.ShapeDtypeStruct((ne, slots, d), BF16),
        grid=(ne, nf),
        in_specs=[pl.BlockSpec((1, slots, d), lambda e, f: (e, 0, 0)),
                  pl.BlockSpec((1, slots, ge.shape[2]), lambda e, f: (e, 0, 0)),
                  pl.BlockSpec((1, d, ft), lambda e, f: (e, 0, f)),
                  pl.BlockSpec((1, d, ft), lambda e, f: (e, 0, f)),
                  pl.BlockSpec((1, ft, d), lambda e, f: (e, f, 0))],
        out_specs=pl.BlockSpec((1, slots, d), lambda e, f: (e, 0, 0)),
        scratch_shapes=[pltpu.VMEM((slots, d), F32)] if nf > 1 else [],
        compiler_params=_params("arbitrary", "arbitrary"),
        name="ffn",
    )(xe, ge, w1, w3, w2)


def _combine_kernel(first_ref, rounds_ref, *refs, groups, alpha, slots):
    ng = len(groups)
    rk_refs, x1_refs = refs[0:ng], refs[ng:2 * ng]
    mod_ref, lng_ref, lnb_ref, ye_hbm = refs[2 * ng:2 * ng + 4]
    out_refs = refs[2 * ng + 4:3 * ng + 4]
    buf, acc_scr, sem = refs[3 * ng + 4:]
    d = x1_refs[0].shape[1]
    b = pl.program_id(0)
    nblk = pl.num_programs(0)
    ne, tb = rk_refs[0].shape[1], rk_refs[0].shape[2]
    win = WIN_ROWS
    eye = (lax.broadcasted_iota(jnp.int32, (tb, tb), 0)
           == lax.broadcasted_iota(jnp.int32, (tb, tb), 1))
    eye = jnp.where(eye, 1.0, 0.0).astype(BF16)

    def starts_of(blk, r):
        firsts = [(first_ref[blk * ne + e] // BF16_ROWS) * BF16_ROWS + r * win for e in range(ne)]
        return firsts, [jnp.minimum(f, slots - win) for f in firsts]

    def windows(slot, starts):
        return [pltpu.make_async_copy(ye_hbm.at[e, pl.ds(pl.multiple_of(starts[e], BF16_ROWS), win), :],
                                      buf.at[slot, pl.ds(e * win, win), :], sem.at[slot, e])
                for e in range(ne)]

    def scatter(rk_ref, slot0, slot, firsts, starts):
        hits = _window_hits(rk_ref, starts, slot0, win)
        hits = [h & ((rk_ref[0, e:e + 1, :] + slot0) >= firsts[e]) for e, h in enumerate(hits)]
        hit = jnp.where(jnp.concatenate(hits, axis=0), 1.0, 0.0).astype(BF16)
        hit_t = _dot_nt(eye, hit).astype(BF16)
        return _dot(hit_t, buf[slot])

    cur = b % 2

    @pl.when(b == 0)
    def _prime():
        for cp in windows(0, starts_of(0, 0)[1]):
            cp.start()

    @pl.when(b + 1 < nblk)
    def _prefetch():
        for cp in windows(1 - cur, starts_of(b + 1, 0)[1]):
            cp.start()

    firsts0, starts0 = starts_of(b, 0)
    for cp in windows(cur, starts0):
        cp.wait()

    def group_body(rk_ref, x1_ref, out_ref, g, local):
        acc_scr[...] = scatter(rk_ref, g["slot0"], cur, firsts0, starts0)

        def extra_round(r, carry):
            firsts, starts = starts_of(b, r)
            for cp in windows(2, starts):
                cp.start()
            for cp in windows(2, starts):
                cp.wait()
            acc_scr[...] += scatter(rk_ref, g["slot0"], 2, firsts, starts)
            return carry

        lax.fori_loop(1, rounds_ref[b], extra_round, 0)
        r = g["row0"] + local // g["blocks_per_batch"]
        g2 = mod_ref[pl.ds(r, 1), :][:, 5 * d:6 * d]
        out_ref[...] = _layer_norm(alpha * x1_ref[...] + g2 * acc_scr[...], lng_ref[...], lnb_ref[...])

    blk0 = 0
    for gi, g in enumerate(groups):
        @pl.when((b >= blk0) & (b < blk0 + g["nb"]))
        def _(gi=gi, g=g, blk0=blk0):
            group_body(rk_refs[gi], x1_refs[gi], out_refs[gi], g, b - blk0)
        blk0 += g["nb"]


def _combine(ye, groups, first, rounds, mod, ln_g, ln_b, alpha):
    d = groups[0]["x1"].shape[1]
    ne, tb = groups[0]["rank"].shape[1:]
    slots = ye.shape[1]
    meta, specs_rk, specs_x1 = [], [], []
    blk0 = 0
    for g in groups:
        nb = g["rank"].shape[0]
        meta.append(dict(nb=nb, slot0=g["slot0"], row0=g["row0"], blocks_per_batch=g["blocks_per_batch"]))
        local = lambda b, *_, blk0=blk0, nb=nb: jnp.clip(b - blk0, 0, nb - 1)
        specs_rk.append(pl.BlockSpec((1, ne, tb), lambda b, *_, local=local: (local(b), 0, 0)))
        specs_x1.append(pl.BlockSpec((tb, d), lambda b, *_, local=local: (local(b), 0)))
        blk0 += nb
    const = lambda shape: pl.BlockSpec(shape, lambda b, *_: (0,) * len(shape))
    return pl.pallas_call(
        functools.partial(_combine_kernel, groups=meta, alpha=alpha, slots=slots),
        out_shape=[jax.ShapeDtypeStruct(g["x1"].shape, F32) for g in groups],
        grid_spec=pltpu.PrefetchScalarGridSpec(
            num_scalar_prefetch=2,
            grid=(blk0,),
            in_specs=specs_rk + specs_x1 + [const(mod.shape), const((1, d)), const((1, d)),
                                            pl.BlockSpec(memory_space=pl.ANY)],
            out_specs=specs_x1,
            scratch_shapes=[pltpu.VMEM((3, ne * WIN_ROWS, d), ye.dtype), pltpu.VMEM((tb, d), F32),
                            pltpu.SemaphoreType.DMA((3, ne))]),
        compiler_params=_params("arbitrary"),
        name="combine",
    )(first, rounds, *[g["rank"] for g in groups], *[g["x1"] for g in groups], mod, ln_g, ln_b, ye)


def _prep_weights(w_in, q_norm, w_uq, kv_norm, w_ukv, w_o_mla, hgrn_norm, w_o_hg, w_out, ln1_g, ln1_b,
                  w_router):
    d = w_in.shape[0]
    q_lora, kv_lora = q_norm.shape[0], kv_norm.shape[0]
    hw = HG_HEADS * HG_DK
    hh, hp = MLA_HEADS, MLA_HEADS * LANES
    o_kv, o_pe = q_lora, q_lora + kv_lora
    o_h = o_pe + MLA_ROPE
    o_g = o_h + 5 * hw
    assert w_in.shape[1] == o_g + 2 * d
    qk = MLA_NOPE + MLA_ROPE
    kvw = MLA_NOPE + MLA_V
    b16 = lambda a: a.astype(BF16)
    assert all(o % BF16_ROWS == 0 for o in (o_kv, o_pe, o_h, o_g))
    win = b16(w_in.T)
    wuq = jnp.pad(w_uq.reshape(q_lora, hh, qk), ((0, 0), (0, 0), (0, LANES - qk))).reshape(q_lora, hp)
    ukv = w_ukv.reshape(kv_lora, hh, kvw)
    wk = jnp.pad(ukv[:, :, :MLA_NOPE], ((0, 0), (0, 0), (0, LANES - MLA_NOPE))).reshape(kv_lora, hp)
    wv = ukv[:, :, MLA_NOPE:].reshape(kv_lora, hh * MLA_V)
    return dict(
        win=win, qn=q_norm.reshape(1, -1), wuq=b16(wuq), kvn=kv_norm.reshape(1, -1), wk=b16(wk), wv=b16(wv),
        hgn=hgrn_norm.reshape(1, -1), womla=b16(w_o_mla), wohg=b16(w_o_hg), wout=b16(w_out),
        ln1g=ln1_g.reshape(1, -1), ln1b=ln1_b.reshape(1, -1), wr=b16(w_router.T))


def _rope_tables(seq):
    n_freq = MLA_ROPE // 4
    inv = ROPE_BASE ** (-np.arange(n_freq, dtype=np.float64) / n_freq)
    t = np.arange(seq)
    ang = np.concatenate([(t // GRID_W)[:, None] * inv, (t % GRID_W)[:, None] * inv], axis=-1)
    cos = np.repeat(np.cos(ang), 2, axis=1)
    sin = np.repeat(np.sin(ang), 2, axis=1) * np.tile([-1.0, 1.0], MLA_ROPE // 2)
    ck = np.pad(cos, ((0, 0), (0, LANES - MLA_ROPE)), constant_values=1.0)
    sk = np.pad(sin, ((0, 0), (0, LANES - MLA_ROPE)))
    cq = np.pad(cos, ((0, 0), (MLA_NOPE, LANES - MLA_NOPE - MLA_ROPE)), constant_values=1.0)
    sq = np.pad(sin, ((0, 0), (MLA_NOPE, LANES - MLA_NOPE - MLA_ROPE)))
    return tuple(jnp.asarray(a, F32) for a in (cq, sq, ck, sk))


def _window_sched(cnt, slot0):
    first = slot0 + cnt[:, :-1]
    end = slot0 + cnt[:, 1:]
    flat = lambda a: a.T.reshape(-1).astype(jnp.int32)
    rounds = jnp.max((end - (first // BF16_ROWS) * BF16_ROWS + WIN_ROWS - 1) // WIN_ROWS, axis=0)
    return flat(first), flat(end), jnp.maximum(rounds, 1).astype(jnp.int32)


def kernel(x_prompt, x_sample, c, cache_ckv, cache_kpe, state_hgrn, c_ctx, w_ada, b_ada, w_in, mla_q_norm, mla_w_uq, mla_kv_norm, mla_w_ukv, mla_w_o, hgrn_gamma, hgrn_norm, hgrn_w_o, w_out, ln1_g, ln1_b, moe_w_router, moe_w1, moe_w3, moe_w2, ln2_g, ln2_b):
    depth = w_ada.shape[0]
    assert depth == 1, "single trunk layer"
    bp, tp, d = x_prompt.shape
    bs, tsq, _ = x_sample.shape
    ne = moe_w_router.shape[-1]
    alpha = (2 * depth) ** 0.25
    past = cache_ckv.shape[2]
    assert tp % TOK_BLOCK == 0 and tsq % TOK_BLOCK == 0 and past % TOK_BLOCK == 0 and tsq % GRID_W == 0

    wts = _prep_weights(w_in[0], mla_q_norm[0], mla_w_uq[0], mla_kv_norm[0], mla_w_ukv[0], mla_w_o[0],
                        hgrn_norm[0], hgrn_w_o[0], w_out[0], ln1_g[0], ln1_b[0], moe_w_router[0])
    cond_rows = -(-(1 + bs) // SUBLANES) * SUBLANES
    cond = jnp.concatenate([c_ctx[None], c, jnp.zeros((cond_rows - 1 - bs, d), F32)], axis=0)
    mod = _adaln(cond, w_ada[0], b_ada[0])

    xs = [x_prompt.reshape(bp * tp, d), x_sample.reshape(bs * tsq, d)]
    dims = [(bp, tp), (bs, tsq)]
    rows = [(0, False), (1, True)]
    ropes = [None, _rope_tables(tsq)]
    kpe_c = jnp.pad(cache_kpe[:, 0].reshape(bs * past, MLA_ROPE), ((0, 0), (0, LANES - MLA_ROPE)))
    caches = [None, _kvup(cache_ckv[:, 0].reshape(bs * past, -1), kpe_c, wts)]
    inits = [None, state_hgrn[:, 0]]

    x1s, h2s, affs, extras = [], [], [], []
    for gi in range(2):
        (bt, sq), (row0, per_batch) = dims[gi], rows[gi]
        if sq == TOK_BLOCK and caches[gi] is None and ropes[gi] is None and not per_batch:
            x1, h2, aff, ckv, kpe, s_fin = _mixer(xs[gi], bt, sq, mod, hgrn_gamma, wts, alpha, ne, row0)
        else:
            q, k, v, ckv, kpe, hgx = _inproj(xs[gi], bt, sq, mod, hgrn_gamma, wts, row0, per_batch, ropes[gi])
            o_f, o_b, s_fin = _hgrn(hgx, bt, sq, inits[gi])
            x1, h2, aff = _postmix(xs[gi], bt, sq, mod, q, k, v, caches[gi], o_f, o_b, hgx, wts, alpha, ne,
                                   row0, per_batch)
        x1s.append(x1)
        h2s.append(h2)
        affs.append(aff)
        extras.append((ckv, kpe, s_fin))

    caps = [EC_FACTOR * x.shape[0] // ne for x in xs]
    slots = sum(caps)
    assert all(cp % BF16_ROWS == 0 for cp in caps) and slots >= WIN_ROWS
    groups, scheds = [], []
    slot0 = 0
    for gi in range(2):
        rank, cnt = _route(affs[gi], caps[gi])
        groups.append(dict(h2=h2s[gi], rank=rank, aff=affs[gi], slot0=slot0, x1=x1s[gi], row0=rows[gi][0],
                           blocks_per_batch=dims[gi][1] // TOK_BLOCK if rows[gi][1] else 1 << 30))
        scheds.append(_window_sched(cnt, slot0))
        slot0 += caps[gi]
    first, end, rounds = [jnp.concatenate([s[k] for s in scheds]) for k in range(3)]
    xe, ge = _compact(groups, first, end, rounds, slots)
    ye = _ffn(xe, ge, moe_w1[0], moe_w3[0], moe_w2[0], slots, ft=moe_w1.shape[-1])
    outs = _combine(ye, groups, first, rounds, mod, ln2_g[0].reshape(1, -1), ln2_b[0].reshape(1, -1), alpha)

    ckv_p, kpe_p, st_p = extras[0]
    y_prompt = outs[0].reshape(bp, tp, d)
    y_sample = outs[1].reshape(bs, tsq, d)
    new_ckv = ckv_p.reshape(bp, 1, tp, -1)
    new_kpe = kpe_p[:, :MLA_ROPE].reshape(bp, 1, tp, MLA_ROPE)
    new_state = st_p.reshape(bp, 1, 2, HG_HEADS, HG_DK, HG_DV)
    return (y_prompt, y_sample, new_ckv, new_kpe, new_state)
```

```python
import functools

import jax
import jax.numpy as jnp
import numpy as np
from jax import lax
from jax.experimental import pallas as pl
from jax.experimental.pallas import tpu as pltpu

F32 = jnp.float32
BF16 = jnp.bfloat16

MLA_HEADS = 8
MLA_NOPE = 64
MLA_ROPE = 32
MLA_V = 64
HG_HEADS = 4
HG_DK = 128
HG_DV = 128
HG_CHUNK = 32
GRID_W = 64
ROPE_BASE = 10000.0
EC_FACTOR = 2
EPS = 1e-6

LANES = 128
SUBLANES = 8
BF16_ROWS = 16
VMEM_LIMIT = 56 * 1024 * 1024

TOK_BLOCK = 256
WIN_ROWS = 64
UNSELECTED = -(1 << 30)

NT_DIMS = (((1,), (1,)), ((), ()))


def _dot(a, b):
    return jnp.dot(a, b, preferred_element_type=F32)


def _dot_nt(a, b):
    return lax.dot_general(a, b, NT_DIMS, preferred_element_type=F32)


def _silu(x):
    return x * jax.nn.sigmoid(x)


def _params(*sem):
    return pltpu.CompilerParams(dimension_semantics=sem, vmem_limit_bytes=VMEM_LIMIT)


def _const_spec(shape):
    zeros = (0,) * len(shape)
    return pl.BlockSpec(shape, lambda *_: zeros, pipeline_mode=pl.Buffered(1))


def _adaln_kernel(c_ref, w_ref, b_ref, o_ref):
    s = _silu(c_ref[...]).astype(BF16)
    o_ref[...] = _dot(s, w_ref[...].astype(BF16)) + b_ref[...]


def _adaln(cond, w_ada, b_ada):
    rows, d = cond.shape
    n = w_ada.shape[1]
    tn = n // 4
    return pl.pallas_call(
        _adaln_kernel,
        out_shape=jax.ShapeDtypeStruct((rows, n), F32),
        grid=(n // tn,),
        in_specs=[_const_spec((rows, d)),
                  pl.BlockSpec((d, tn), lambda j: (0, j)),
                  pl.BlockSpec((1, tn), lambda j: (0, j))],
        out_specs=pl.BlockSpec((rows, tn), lambda j: (0, j)),
        compiler_params=_params("arbitrary"),
        name="adaln",
    )(cond, w_ada, b_ada.reshape(1, n))


def _rms(x, g):
    return x * lax.rsqrt(jnp.mean(x * x, axis=-1, keepdims=True) + EPS) * g


def _rope(x, c, s):
    w = x.shape[-1]
    lane = lax.broadcasted_iota(jnp.int32, x.shape, 1)
    nxt = pltpu.roll(x, w - 1, 1)
    prv = pltpu.roll(x, 1, 1)
    return x * c + jnp.where(lane % 2 == 0, nxt, prv) * s


N_INPROJ_WEIGHTS = 6


def _mod_row(mod_ref, row0, per_batch):
    r = row0 + pl.program_id(0) if per_batch else row0
    return mod_ref[pl.ds(r, 1), :]


def _modulated(x_ref, m):
    d = x_ref.shape[1]
    return (x_ref[...] * (1.0 + m[:, d:2 * d]) + m[:, 0:d]).astype(BF16)


def _keys(k_nope, kpe):
    shared = pltpu.roll(kpe, MLA_NOPE, 1)
    return (k_nope + jnp.concatenate([shared] * MLA_HEADS, axis=1)).astype(BF16)


def _inproj_kernel(*refs, row0, per_batch, rope):
    x_ref, mod_ref = refs[:2]
    _inproj_body(x_ref, _mod_row(mod_ref, row0, per_batch), *refs[2:], rope=rope)


def _inproj_body(x_ref, m, *refs, rope):
    gam_ref, win_ref, qn_ref, wuq_ref, kvn_ref, wk_ref, wv_ref = refs[:1 + N_INPROJ_WEIGHTS]
    refs = refs[1 + N_INPROJ_WEIGHTS:]
    if rope:
        cq_ref, sq_ref, ck_ref, sk_ref = refs[:4]
        refs = refs[4:]
    q_o, k_o, v_o, ckv_o, kpe_o, hgx_o = refs
    h = _modulated(x_ref, m)
    hw = HG_HEADS * HG_DK
    o_kv = qn_ref.shape[1]
    o_pe = o_kv + kvn_ref.shape[1]
    o_h = o_pe + MLA_ROPE

    cq = _rms(_dot_nt(h, win_ref[0:o_kv, :]), qn_ref[...])
    q = _dot(cq.astype(BF16), wuq_ref[...])
    if rope:
        q = _rope(q, jnp.concatenate([cq_ref[...]] * MLA_HEADS, axis=1),
                  jnp.concatenate([sq_ref[...]] * MLA_HEADS, axis=1))
    q_o[...] = q.astype(BF16)

    ckv = _rms(_dot_nt(h, win_ref[o_kv:o_pe, :]), kvn_ref[...])
    ckv_o[...] = ckv
    kpe = _dot_nt(h, win_ref[o_pe:o_h, :])
    kpe = jnp.concatenate([kpe, jnp.zeros((kpe.shape[0], LANES - MLA_ROPE), F32)], axis=1)
    if rope:
        kpe = _rope(kpe, ck_ref[...], sk_ref[...])
    kpe_o[...] = kpe
    cb = ckv.astype(BF16)
    k_o[...] = _keys(_dot(cb, wk_ref[...]), kpe)
    v_o[...] = _dot(cb, wv_ref[...]).astype(BF16)

    z = _dot_nt(h, win_ref[o_h:o_h + 5 * hw, :])
    hgx_o[:, 0:hw] = _silu(z[:, 0:hw])
    for dr in range(2):
        g0, g1 = gam_ref[dr, 0:1, :], gam_ref[dr, 1:2, :]
        gmax = jnp.maximum(g0, g1)
        e0, e1 = jnp.exp(g0 - gmax), jnp.exp(g1 - gmax)
        lb = e0 / (e0 + e1)
        f = lb + (1.0 - lb) * jax.nn.sigmoid(z[:, (1 + dr) * hw:(2 + dr) * hw])
        hgx_o[:, (1 + 2 * dr) * hw:(2 + 2 * dr) * hw] = jnp.log(f)
        hgx_o[:, (2 + 2 * dr) * hw:(3 + 2 * dr) * hw] = 1.0 - f
    hgx_o[:, 5 * hw:6 * hw] = z[:, 3 * hw:4 * hw]
    hgx_o[:, 6 * hw:7 * hw] = z[:, 4 * hw:5 * hw]


def _inproj(x2d, batch, seq, mod, gamma, wts, row0, per_batch, rope_tabs):
    n, d = x2d.shape
    tm = MIXER_SEQS * TOK_BLOCK if seq % (MIXER_SEQS * TOK_BLOCK) == 0 else TOK_BLOCK
    nblk = seq // tm
    rope = rope_tabs is not None
    hp = MLA_HEADS * LANES
    hw = HG_HEADS * HG_DK
    tok = lambda b, i: (b * nblk + i, 0)
    pos = lambda b, i: (i, 0)
    weights = [wts[k] for k in INPROJ_KEYS]
    ins = [x2d, mod, gamma] + weights
    in_specs = ([pl.BlockSpec((tm, d), tok), _const_spec(mod.shape), _const_spec(gamma.shape)]
                + [_const_spec(w.shape) for w in weights])
    if rope:
        ins += list(rope_tabs)
        in_specs += [pl.BlockSpec((tm, t.shape[1]), pos) for t in rope_tabs]
    widths = [(hp, BF16), (hp, BF16), (MLA_HEADS * MLA_V, BF16), (wts["kvn"].shape[1], F32), (LANES, F32),
              (7 * hw, F32)]
    return pl.pallas_call(
        functools.partial(_inproj_kernel, row0=row0, per_batch=per_batch, rope=rope),
        out_shape=[jax.ShapeDtypeStruct((n, w), dt) for w, dt in widths],
        grid=(batch, nblk),
        in_specs=in_specs,
        out_specs=[pl.BlockSpec((tm, w), tok) for w, _ in widths],
        compiler_params=_params("arbitrary", "arbitrary"),
        name="inproj",
    )(*ins)


def _kvup_kernel(ckv_ref, kpe_ref, wk_ref, wv_ref, k_o, v_o):
    cb = ckv_ref[...].astype(BF16)
    k_o[...] = _keys(_dot(cb, wk_ref[...]), kpe_ref[...])
    v_o[...] = _dot(cb, wv_ref[...]).astype(BF16)


def _kvup(ckv2d, kpe2d, wts):
    n = ckv2d.shape[0]
    tm = TOK_BLOCK
    widths = [MLA_HEADS * LANES, MLA_HEADS * MLA_V]
    row = lambda i: (i, 0)
    ws = [wts["wk"], wts["wv"]]
    return pl.pallas_call(
        _kvup_kernel,
        out_shape=[jax.ShapeDtypeStruct((n, w), BF16) for w in widths],
        grid=(n // tm,),
        in_specs=[pl.BlockSpec((tm, ckv2d.shape[1]), row), pl.BlockSpec((tm, LANES), row)]
                 + [_const_spec(w.shape) for w in ws],
        out_specs=[pl.BlockSpec((tm, w), row) for w in widths],
        compiler_params=_params("arbitrary"),
        name="kvup",
    )(ckv2d, kpe2d, *ws)


ATTN_SCALE = (MLA_NOPE + MLA_ROPE) ** -0.5


def _attn_body(q_ref, k_ref, v_ref, kc_ref, vc_ref, o_ref):
    cached = kc_ref is not None
    scale = ATTN_SCALE
    per_slab = LANES // MLA_V
    own = lax.broadcasted_iota(jnp.int32, (q_ref.shape[0], LANES), 1) // MLA_V
    for slab in range(MLA_HEADS // per_slab):
        vsl = slice(slab * LANES, (slab + 1) * LANES)
        out = None
        for sub in range(per_slab):
            hd = slab * per_slab + sub
            sl = slice(hd * LANES, (hd + 1) * LANES)
            q = q_ref[:, sl]
            s = _dot_nt(q, k_ref[:, sl]) * scale
            mx = jnp.max(s, axis=-1, keepdims=True)
            if cached:
                s2 = _dot_nt(q, kc_ref[:, sl]) * scale
                mx = jnp.maximum(mx, jnp.max(s2, axis=-1, keepdims=True))
            e = jnp.exp(s - mx)
            den = jnp.sum(e, axis=-1, keepdims=True)
            o = _dot(e.astype(BF16), v_ref[:, vsl])
            if cached:
                e2 = jnp.exp(s2 - mx)
                den = den + jnp.sum(e2, axis=-1, keepdims=True)
                o = o + _dot(e2.astype(BF16), vc_ref[:, vsl])
            o = o / den
            out = o if out is None else jnp.where(own == sub, o, out)
        o_ref[:, vsl] = out.astype(o_ref.dtype)


def _chunk_scan(x, reverse):
    tm = x.shape[0]
    rin = lax.broadcasted_iota(jnp.int32, x.shape, 0) % HG_CHUNK
    step = 1
    while step < HG_CHUNK:
        if reverse:
            x = x + jnp.where(rin < HG_CHUNK - step, pltpu.roll(x, tm - step, 0), 0.0)
        else:
            x = x + jnp.where(rin >= step, pltpu.roll(x, step, 0), 0.0)
        step *= 2
    return x


def _hgrn_kernel(*refs, has_init):
    fwd, bwd = refs[0:4], refs[4:8]
    refs = refs[8:]
    s0_ref = None
    if has_init:
        s0_ref = refs[0]
        refs = refs[1:]
    of_ref, ob_ref, sfin_ref, st_scr = refs
    i = pl.program_id(1)
    _hgrn_body(fwd, bwd, s0_ref, of_ref, ob_ref, sfin_ref, st_scr, i == 0, i == pl.num_programs(1) - 1)


def _hgrn_body(fwd, bwd, s0_ref, of_ref, ob_ref, sfin_ref, st_scr, first, last):
    tm = fwd[0].shape[0]
    c = HG_CHUNK
    nch = tm // c
    dk, dv = HG_DK, HG_DV
    hw = HG_HEADS * dk

    def initial(dr, hd):
        return s0_ref[0, dr, hd].T if s0_ref is not None else jnp.zeros((dv, dk), F32)

    if st_scr is not None:
        @pl.when(first)
        def _init():
            for dr in range(2):
                for hd in range(HG_HEADS):
                    st_scr[dr, hd] = initial(dr, hd)

    npair = nch // 2
    pair = 2 * c
    row = lax.broadcasted_iota(jnp.int32, (tm, tm), 0)
    col = lax.broadcasted_iota(jnp.int32, (tm, tm), 1)
    same = (row // c) == (col // c)
    same_pair = (row // pair) == (col // pair)
    bd = (lax.broadcasted_iota(jnp.int32, (tm, npair * dk), 0) // pair
          == lax.broadcasted_iota(jnp.int32, (tm, npair * dk), 1) // dk)
    chunk_odd = (lax.broadcasted_iota(jnp.int32, (tm, hw), 0) // c) % 2 == 1

    for dr, (hq_ref, lf_ref, kk_ref, vv_ref) in enumerate((fwd, bwd)):
        o_ref = of_ref if dr == 0 else ob_ref
        tri = same & ((col <= row) if dr == 0 else (col >= row))
        cross = same_pair & (((row // c) > (col // c)) if dr == 0 else ((row // c) < (col // c)))
        second = chunk_odd if dr == 0 else ~chunk_odd
        bcum = _chunk_scan(lf_ref[...], reverse=dr == 1)
        closing = c - 1 if dr == 0 else 0
        btot3 = bcum.reshape(nch, c, hw)[:, closing:closing + 1, :]
        btot = jnp.broadcast_to(btot3, (nch, c, hw)).reshape(tm, hw)
        bpart = jnp.where(chunk_odd, pltpu.roll(btot, c, 0), pltpu.roll(btot, tm - c, 0))
        epart = jnp.exp(bpart)
        bpair = btot + bpart
        kk = kk_ref[...]
        qd = hq_ref[...] * jnp.exp(bcum)
        kd = kk * jnp.exp(-bcum)
        ke = kk * jnp.exp(btot - bcum)
        qd2 = jnp.where(second, qd * epart, qd)
        ke2 = jnp.where(second, ke, ke * epart)
        vv = vv_ref[...]
        order = range(npair) if dr == 0 else range(npair - 1, -1, -1)
        for hd in range(HG_HEADS):
            sl = slice(hd * dk, (hd + 1) * dk)
            qd_h = qd[:, sl].astype(BF16)
            v_h = vv[:, hd * dv:(hd + 1) * dv]
            a = jnp.where(tri, _dot_nt(qd_h, kd[:, sl].astype(BF16)),
                          jnp.where(cross, _dot_nt(qd_h, ke[:, sl].astype(BF16)), 0.0))
            o_intra = _dot(a.astype(BF16), v_h.astype(BF16))
            kebd = jnp.where(bd, jnp.concatenate([ke2[:, sl]] * npair, axis=1), 0.0).astype(BF16)
            qbd = jnp.where(bd, jnp.concatenate([qd2[:, sl]] * npair, axis=1), 0.0).astype(BF16)
            ut = _dot(v_h.T.astype(BF16), kebd)
            st = st_scr[dr, hd] if st_scr is not None else initial(dr, hd)
            prev = [None] * npair
            for p in order:
                prev[p] = st
                st = st * jnp.exp(bpair[p * pair:p * pair + 1, sl]) + ut[:, p * dk:(p + 1) * dk]
            if st_scr is not None:
                st_scr[dr, hd] = st
            o_inter = _dot_nt(qbd, jnp.concatenate(prev, axis=1).astype(BF16))
            o_ref[:, hd * dv:(hd + 1) * dv] = o_intra + o_inter

            if last is True:
                sfin_ref[0, dr, hd] = st.T
            else:
                @pl.when(last)
                def _final(st=st, dr=dr, hd=hd):
                    sfin_ref[0, dr, hd] = st.T


def _hgrn(hgx, batch, seq, s0=None):
    n = hgx.shape[0]
    tm = TOK_BLOCK
    nblk = seq // tm
    hw = HG_HEADS * HG_DK

    def spec(lane_blk, rev):
        if rev:
            return pl.BlockSpec((tm, hw), lambda b, i: (b * nblk + nblk - 1 - i, lane_blk))
        return pl.BlockSpec((tm, hw), lambda b, i: (b * nblk + i, lane_blk))

    in_specs = [spec(0, False), spec(1, False), spec(2, False), spec(5, False),
                spec(0, True), spec(3, True), spec(4, True), spec(5, True)]
    ins = [hgx] * 8
    st_shape = (1, 2, HG_HEADS, HG_DK, HG_DV)
    st_spec = pl.BlockSpec(st_shape, lambda b, i: (b, 0, 0, 0, 0))
    if s0 is not None:
        ins.append(s0)
        in_specs.append(st_spec)
    return pl.pallas_call(
        functools.partial(_hgrn_kernel, has_init=s0 is not None),
        out_shape=[jax.ShapeDtypeStruct((n, hw), F32), jax.ShapeDtypeStruct((n, hw), F32),
                   jax.ShapeDtypeStruct((batch,) + st_shape[1:], F32)],
        grid=(batch, nblk),
        in_specs=in_specs,
        out_specs=[spec(0, False), spec(0, True), st_spec],
        scratch_shapes=[pltpu.VMEM((2, HG_HEADS, HG_DV, HG_DK), F32)],
        compiler_params=_params("arbitrary", "arbitrary"),
        name="hgrn",
    )(*ins)


def _layer_norm(x, g, b):
    xc = x - jnp.mean(x, axis=-1, keepdims=True)
    var = jnp.mean(xc * xc, axis=-1, keepdims=True)
    return xc * lax.rsqrt(var + EPS) * g + b


N_POSTMIX_WEIGHTS = 8
INPROJ_KEYS = ("win", "qn", "wuq", "kvn", "wk", "wv")
POSTMIX_KEYS = ("hgn", "womla", "wohg", "wout", "ln1g", "ln1b", "wr")
MIXER_SEQS = 2


def _postmix_kernel(x_ref, mod_ref, *refs, alpha, row0, per_batch, cached):
    n_attn = 5 if cached else 3
    q_ref, k_ref, v_ref = refs[:3]
    kc_ref, vc_ref = refs[3:5] if cached else (None, None)
    of_ref, ob_ref, zg_ref = refs[n_attn:n_attn + 3]
    om_s = refs[-1]
    _attn_body(q_ref, k_ref, v_ref, kc_ref, vc_ref, om_s)
    _postmix_body(x_ref, _mod_row(mod_ref, row0, per_batch), of_ref, ob_ref, zg_ref, om_s,
                  *refs[n_attn + 3:-1], alpha=alpha)


def _postmix_body(x_ref, m, of_ref, ob_ref, zg_ref, om_ref, wg_ref, hgn_ref, womla_ref,
                  wohg_ref, wout_ref, lng_ref, lnb_ref, wr_ref, x1_o, h2_o, aff_o, *, alpha):
    d = x_ref.shape[1]
    tb = aff_o.shape[2]
    g1, sh2, sc2 = m[:, 2 * d:3 * d], m[:, 3 * d:4 * d], m[:, 4 * d:5 * d]
    o = of_ref[...] + ob_ref[...]
    zg = zg_ref[...]
    parts = []
    for hd in range(HG_HEADS):
        sl = slice(hd * HG_DV, (hd + 1) * HG_DV)
        parts.append(_rms(o[:, sl], hgn_ref[...]) * _silu(zg[:, sl]))
    ohg = jnp.concatenate(parts, axis=1).astype(BF16)
    gates = _dot_nt(_modulated(x_ref, m), wg_ref[wg_ref.shape[0] - 2 * d:, :])
    merged = (jax.nn.sigmoid(gates[:, 0:d]) * _dot(om_ref[...], womla_ref[...])
              + jax.nn.sigmoid(gates[:, d:2 * d]) * _dot(ohg, wohg_ref[...]))
    mix = _dot(merged.astype(BF16), wout_ref[...])
    x1 = _layer_norm(alpha * x_ref[...] + g1 * mix, lng_ref[...], lnb_ref[...])
    x1_o[...] = x1
    h2 = (x1 * (1.0 + sc2) + sh2).astype(BF16)
    h2_o[...] = h2
    logits = _dot_nt(wr_ref[...], h2)
    e = jnp.exp(logits - jnp.max(logits, axis=0, keepdims=True))
    aff = e / jnp.sum(e, axis=0, keepdims=True)
    for blk in range(aff_o.shape[0]):
        aff_o[blk] = aff[:, blk * tb:(blk + 1) * tb]


def _postmix(x2d, batch, seq, mod, q, k, v, cache, o_f, o_b, hgx, wts, alpha, n_experts, row0, per_batch):
    n, d = x2d.shape
    tm = TOK_BLOCK
    nblk = seq // tm
    hw = HG_HEADS * HG_DV
    hp, hv = q.shape[1], v.shape[1]
    tok = lambda b, i: (b * nblk + i, 0)
    per_seq = lambda b, i: (b, 0)
    weights = [wts[k] for k in POSTMIX_KEYS]
    attn_ins = [q, k, v]
    attn_specs = [pl.BlockSpec((tm, hp), tok), pl.BlockSpec((seq, hp), per_seq), pl.BlockSpec((seq, hv), per_seq)]
    if cache is not None:
        past = cache[0].shape[0] // batch
        attn_ins += list(cache)
        attn_specs += [pl.BlockSpec((past, hp), per_seq), pl.BlockSpec((past, hv), per_seq)]
    return pl.pallas_call(
        functools.partial(_postmix_kernel, alpha=alpha, row0=row0, per_batch=per_batch,
                          cached=cache is not None),
        out_shape=[jax.ShapeDtypeStruct((n, d), F32), jax.ShapeDtypeStruct((n, d), BF16),
                   jax.ShapeDtypeStruct((n // tm, n_experts, tm), F32)],
        grid=(batch, nblk),
        in_specs=[pl.BlockSpec((tm, d), tok), _const_spec(mod.shape)] + attn_specs
                 + [pl.BlockSpec((tm, hw), tok), pl.BlockSpec((tm, hw), tok),
                    pl.BlockSpec((tm, hw), lambda b, i: (b * nblk + i, 6)), _const_spec(wts["win"].shape)]
                 + [_const_spec(w.shape) for w in weights],
        out_specs=[pl.BlockSpec((tm, d), tok), pl.BlockSpec((tm, d), tok),
                   pl.BlockSpec((1, n_experts, tm), lambda b, i: (b * nblk + i, 0, 0))],
        scratch_shapes=[pltpu.VMEM((tm, hv), BF16)],
        compiler_params=_params("arbitrary", "arbitrary"),
        name="postmix",
    )(x2d, mod, *attn_ins, o_f, o_b, hgx, wts["win"], *weights)


def _mixer_kernel(x_ref, mod_ref, *refs, alpha, row0, seq):
    nw = 1 + N_INPROJ_WEIGHTS
    in_w, refs = refs[:nw], refs[nw:]
    pm_w, refs = refs[:N_POSTMIX_WEIGHTS - 1], refs[N_POSTMIX_WEIGHTS - 1:]
    pm_w = (in_w[1],) + tuple(pm_w)
    x1_o, h2_o, aff_o, ckv_o, kpe_o, sfin_o, q_s, k_s, v_s, hgx_s, om_s, of_s, ob_s = refs
    m = _mod_row(mod_ref, row0, False)
    _inproj_body(x_ref, m, *in_w, q_s, k_s, v_s, ckv_o, kpe_o, hgx_s, rope=False)
    hw = HG_HEADS * HG_DK
    for s in range(x_ref.shape[0] // seq):
        rows = slice(s * seq, (s + 1) * seq)
        _attn_body(q_s.at[rows], k_s.at[rows], v_s.at[rows], None, None, om_s.at[rows])
        lane = lambda j: hgx_s.at[rows, j * hw:(j + 1) * hw]
        _hgrn_body((lane(0), lane(1), lane(2), lane(5)), (lane(0), lane(3), lane(4), lane(5)), None,
                   of_s.at[rows], ob_s.at[rows], sfin_o.at[s:s + 1], None, True, True)
    _postmix_body(x_ref, m, of_s, ob_s, hgx_s.at[:, 6 * hw:7 * hw], om_s, *pm_w, x1_o, h2_o, aff_o,
                  alpha=alpha)


def _mixer(x2d, batch, seq, mod, gamma, wts, alpha, n_experts, row0):
    n, d = x2d.shape
    assert seq == TOK_BLOCK
    ns = MIXER_SEQS if batch % MIXER_SEQS == 0 else 1
    tm = ns * seq
    hp = MLA_HEADS * LANES
    hv = MLA_HEADS * MLA_V
    hw = HG_HEADS * HG_DK
    kvl = wts["kvn"].shape[1]
    weights = [wts[k] for k in INPROJ_KEYS + POSTMIX_KEYS]
    tok = lambda b: (b, 0)
    st_shape = (ns, 2, HG_HEADS, HG_DK, HG_DV)
    return pl.pallas_call(
        functools.partial(_mixer_kernel, alpha=alpha, row0=row0, seq=seq),
        out_shape=[jax.ShapeDtypeStruct((n, d), F32), jax.ShapeDtypeStruct((n, d), BF16),
                   jax.ShapeDtypeStruct((n // seq, n_experts, seq), F32),
                   jax.ShapeDtypeStruct((n, kvl), F32), jax.ShapeDtypeStruct((n, LANES), F32),
                   jax.ShapeDtypeStruct((batch,) + st_shape[1:], F32)],
        grid=(batch // ns,),
        in_specs=[pl.BlockSpec((tm, d), tok), _const_spec(mod.shape), _const_spec(gamma.shape)]
                 + [_const_spec(w.shape) for w in weights],
        out_specs=[pl.BlockSpec((tm, d), tok), pl.BlockSpec((tm, d), tok),
                   pl.BlockSpec((ns, n_experts, seq), lambda b: (b, 0, 0)),
                   pl.BlockSpec((tm, kvl), tok), pl.BlockSpec((tm, LANES), tok),
                   pl.BlockSpec(st_shape, lambda b: (b, 0, 0, 0, 0))],
        scratch_shapes=[pltpu.VMEM((tm, hp), BF16), pltpu.VMEM((tm, hp), BF16), pltpu.VMEM((tm, hv), BF16),
                        pltpu.VMEM((tm, 7 * hw), F32), pltpu.VMEM((tm, hv), BF16),
                        pltpu.VMEM((tm, hw), F32), pltpu.VMEM((tm, hw), F32)],
        compiler_params=_params("arbitrary"),
        name="mixer",
    )(x2d, mod, gamma, *weights)


def _route_kernel(*refs, caps):
    ng = len(caps)
    for aff_ref, rank_o, cnt_o, cap in zip(refs[0:ng], refs[ng:2 * ng], refs[2 * ng:3 * ng], caps):
        _route_group(aff_ref, rank_o, cnt_o, cap)


def _route_group(aff_ref, rank_o, cnt_o, cap):
    nb, ne, tb = aff_ref.shape
    key = aff_ref[...]

    def count(mask):
        return jnp.sum(jnp.sum(jnp.where(mask, 1.0, 0.0), axis=0), axis=1, keepdims=True)

    def bit_step(it, bits):
        cand = bits | jnp.left_shift(jnp.int32(1), 30 - it)
        return jnp.where(count(key >= pltpu.bitcast(cand, F32)[None]) >= cap, cand, bits)

    bits = lax.fori_loop(0, 31, bit_step, jnp.zeros((ne, 1), jnp.int32))
    thr = pltpu.bitcast(bits, F32)
    need = cap - count(key > thr[None])
    before = (lax.broadcasted_iota(jnp.int32, (tb, tb), 0)
              < lax.broadcasted_iota(jnp.int32, (tb, tb), 1))
    before = jnp.where(before, 1.0, 0.0).astype(BF16)
    off_eq = jnp.zeros((ne, 1), F32)
    off_sel = jnp.zeros((ne, 1), F32)
    cnt_o[...] = jnp.zeros_like(cnt_o)
    for blk in range(nb):
        key_b = key[blk]
        eq = key_b == thr
        eq_b = jnp.where(eq, 1.0, 0.0)
        eq_rank = _dot(eq_b.astype(BF16), before) + off_eq
        sel = (key_b > thr) | (eq & (eq_rank < need))
        sel_b = jnp.where(sel, 1.0, 0.0)
        rank = _dot(sel_b.astype(BF16), before) + off_sel
        rank_o[blk] = jnp.where(sel, rank.astype(jnp.int32), UNSELECTED)
        cnt_o[:, blk:blk + 1] = off_sel.astype(jnp.int32)
        off_eq = off_eq + jnp.sum(eq_b, axis=1, keepdims=True)
        off_sel = off_sel + jnp.sum(sel_b, axis=1, keepdims=True)
    cnt_o[:, nb:nb + 1] = off_sel.astype(jnp.int32)


def _route(affs, caps):
    ng = len(affs)
    ne = affs[0].shape[1]
    assert all(a.shape[0] + 1 <= LANES for a in affs)
    outs = pl.pallas_call(
        functools.partial(_route_kernel, caps=tuple(caps)),
        out_shape=[jax.ShapeDtypeStruct(a.shape, jnp.int32) for a in affs]
                  + [jax.ShapeDtypeStruct((ne, LANES), jnp.int32)] * ng,
        in_specs=[pl.BlockSpec(memory_space=pltpu.VMEM)] * ng,
        out_specs=[pl.BlockSpec(memory_space=pltpu.VMEM)] * (2 * ng),
        compiler_params=pltpu.CompilerParams(vmem_limit_bytes=VMEM_LIMIT),
        name="route",
    )(*affs)
    return [(outs[gi], outs[ng + gi][:, :affs[gi].shape[0] + 1]) for gi in range(ng)]


def _window_hits(rk_ref, firsts, slot0, win):
    ne, tb = rk_ref.shape[1], rk_ref.shape[2]
    win_iota = lax.broadcasted_iota(jnp.int32, (win, tb), 0)
    return [(rk_ref[0, e:e + 1, :] + (slot0 - firsts[e])) == win_iota for e in range(ne)]


def _compact_kernel(first_ref, end_ref, rounds_ref, *refs, groups, slots):
    ng = len(groups)
    h2_refs, rk_refs, af_refs = refs[0:ng], refs[ng:2 * ng], refs[2 * ng:3 * ng]
    hbms, refs = refs[3 * ng:3 * ng + 2], refs[3 * ng + 2:]
    stages, tails = refs[0:2], refs[2:4]
    sem, issued = refs[4:]
    b = pl.program_id(0)
    ne = rk_refs[0].shape[1]
    win = WIN_ROWS
    sub = BF16_ROWS

    def copies(slot, dsts):
        return [pltpu.make_async_copy(stage.at[slot, pl.ds(e * win, win), :],
                                      hbm.at[e, pl.ds(pl.multiple_of(dsts[e], sub), win), :], sem.at[c, e])
                for c, (stage, hbm) in enumerate(zip(stages, hbms)) for e in range(ne)]

    def wait_previous():
        @pl.when(issued[0] > 0)
        def _():
            for cp in copies(0, [0] * ne):
                cp.wait()

    @pl.when(b == 0)
    def _init():
        issued[0] = 0
        for stage, tail in zip(stages, tails):
            tail[...] = jnp.zeros_like(tail)
            stage[1] = jnp.zeros(stage.shape[1:], stage.dtype)
        pad = copies(1, [slots] * ne)
        for cp in pad:
            cp.start()
        for cp in pad:
            cp.wait()

    def group_body(h2_ref, rk_ref, af_ref, slot0):
        firsts = [first_ref[b * ne + e] for e in range(ne)]
        bases = [(f // sub) * sub for f in firsts]
        ends = [end_ref[b * ne + e] - bases[e] for e in range(ne)]

        def one_round(r, carry):
            dsts = [bases[e] + r * win for e in range(ne)]
            hits = _window_hits(rk_ref, dsts, slot0, win)
            onehot = jnp.where(jnp.concatenate(hits, axis=0), 1.0, 0.0).astype(BF16)
            gate = jnp.concatenate(
                [jnp.sum(jnp.where(hits[e], af_ref[0, e:e + 1, :], 0.0), axis=1, keepdims=True)
                 for e in range(ne)], axis=0)
            slot = issued[0] % 2
            stages[0][slot] = _dot(onehot, h2_ref[...]).astype(BF16)
            stages[1][slot] = jnp.broadcast_to(gate, (ne * win, LANES))
            @pl.when(r == 0)
            def _heads():
                for stage, tail in zip(stages, tails):
                    sub_iota = lax.broadcasted_iota(jnp.int32, (sub, stage.shape[2]), 0)
                    for e in range(ne):
                        head = stage[slot, e * win:e * win + sub, :]
                        stage[slot, e * win:e * win + sub, :] = jnp.where(
                            sub_iota < firsts[e] - bases[e], tail[e * sub:(e + 1) * sub, :], head)

            for stage, tail in zip(stages, tails):
                for e in range(ne):
                    last = (ends[e] // sub) * sub
                    group = stage[slot, pl.ds(pl.multiple_of(e * win + last % win, sub), sub), :]
                    tail[e * sub:(e + 1) * sub, :] = jnp.where(r == last // win, group,
                                                               tail[e * sub:(e + 1) * sub, :])
            wait_previous()
            for cp in copies(slot, [jnp.minimum(dst, slots) for dst in dsts]):
                cp.start()
            issued[0] = issued[0] + 1
            return carry

        lax.fori_loop(0, rounds_ref[b], one_round, 0)

    blk0 = 0
    for gi, g in enumerate(groups):
        @pl.when((b >= blk0) & (b < blk0 + g["nb"]))
        def _(gi=gi, g=g):
            group_body(h2_refs[gi], rk_refs[gi], af_refs[gi], g["slot0"])
        blk0 += g["nb"]

    @pl.when(b == pl.num_programs(0) - 1)
    def _drain():
        wait_previous()


def _compact(groups, first, end, rounds, slots):
    d = groups[0]["h2"].shape[1]
    nbs = [g["rank"].shape[0] for g in groups]
    ne, tb = groups[0]["rank"].shape[1:]
    meta, specs_h2, specs_rk = [], [], []
    blk0 = 0
    for g, nb in zip(groups, nbs):
        meta.append(dict(nb=nb, slot0=g["slot0"]))
        local = lambda b, *_, blk0=blk0, nb=nb: jnp.clip(b - blk0, 0, nb - 1)
        specs_h2.append(pl.BlockSpec((tb, d), lambda b, *_, local=local: (local(b), 0)))
        specs_rk.append(pl.BlockSpec((1, ne, tb), lambda b, *_, local=local: (local(b), 0, 0)))
        blk0 += nb
    streams = [(d, BF16), (LANES, F32)]
    return pl.pallas_call(
        functools.partial(_compact_kernel, groups=meta, slots=slots),
        out_shape=[jax.ShapeDtypeStruct((ne, slots + WIN_ROWS, w), dt) for w, dt in streams],
        grid_spec=pltpu.PrefetchScalarGridSpec(
            num_scalar_prefetch=3,
            grid=(sum(nbs),),
            in_specs=specs_h2 + specs_rk + specs_rk,
            out_specs=[pl.BlockSpec(memory_space=pl.ANY)] * 2,
            scratch_shapes=[pltpu.VMEM((2, ne * WIN_ROWS, w), dt) for w, dt in streams]
                           + [pltpu.VMEM((ne * BF16_ROWS, w), dt) for w, dt in streams]
                           + [pltpu.SemaphoreType.DMA((2, ne)), pltpu.SMEM((1,), jnp.int32)]),
        compiler_params=_params("arbitrary"),
        name="compact",
    )(first, end, rounds, *[g["h2"] for g in groups], *[g["rank"] for g in groups],
      *[g["aff"] for g in groups])


def _ffn_kernel(xe_ref, ge_ref, w1_ref, w3_ref, w2_ref, ye_ref, *scratch):
    f = pl.program_id(1)
    x = xe_ref[0]
    hid = _silu(_dot(x, w1_ref[0].astype(BF16))) * _dot(x, w3_ref[0].astype(BF16))
    y = _dot(hid.astype(BF16), w2_ref[0].astype(BF16))
    if not scratch:
        ye_ref[0] = (y * ge_ref[0, :, 0:1]).astype(ye_ref.dtype)
        return
    acc_scr, = scratch
    last = pl.num_programs(1) - 1

    @pl.when(f == 0)
    def _first():
        acc_scr[...] = y

    @pl.when((f > 0) & (f < last))
    def _middle():
        acc_scr[...] += y

    @pl.when(f == last)
    def _last():
        ye_ref[0] = ((acc_scr[...] + y) * ge_ref[0, :, 0:1]).astype(ye_ref.dtype)


def _ffn(xe, ge, w1, w3, w2, slots, ft):
    ne, d, dff = w1.shape
    nf = dff // ft
    return pl.pallas_call(
        _ffn_kernel,
        out_shape=jax.ShapeDtypeStruct((ne, slots, d), BF16),
        grid=(ne, nf),
        in_specs=[pl.BlockSpec((1, slots, d), lambda e, f: (e, 0, 0)),
                  pl.BlockSpec((1, slots, ge.shape[2]), lambda e, f: (e, 0, 0)),
                  pl.BlockSpec((1, d, ft), lambda e, f: (e, 0, f)),
                  pl.BlockSpec((1, d, ft), lambda e, f: (e, 0, f)),
                  pl.BlockSpec((1, ft, d), lambda e, f: (e, f, 0))],
        out_specs=pl.BlockSpec((1, slots, d), lambda e, f: (e, 0, 0)),
        scratch_shapes=[pltpu.VMEM((slots, d), F32)] if nf > 1 else [],
        compiler_params=_params("arbitrary", "arbitrary"),
        name="ffn",
    )(xe, ge, w1, w3, w2)


def _combine_kernel(first_ref, rounds_ref, *refs, groups, alpha, slots):
    ng = len(groups)
    rk_refs, x1_refs = refs[0:ng], refs[ng:2 * ng]
    mod_ref, lng_ref, lnb_ref, ye_hbm = refs[2 * ng:2 * ng + 4]
    out_refs = refs[2 * ng + 4:3 * ng + 4]
    buf, acc_scr, sem = refs[3 * ng + 4:]
    d = x1_refs[0].shape[1]
    b = pl.program_id(0)
    nblk = pl.num_programs(0)
    ne, tb = rk_refs[0].shape[1], rk_refs[0].shape[2]
    win = WIN_ROWS
    eye = (lax.broadcasted_iota(jnp.int32, (tb, tb), 0)
           == lax.broadcasted_iota(jnp.int32, (tb, tb), 1))
    eye = jnp.where(eye, 1.0, 0.0).astype(BF16)

    def starts_of(blk, r):
        firsts = [(first_ref[blk * ne + e] // BF16_ROWS) * BF16_ROWS + r * win for e in range(ne)]
        return firsts, [jnp.minimum(f, slots - win) for f in firsts]

    def windows(slot, starts):
        return [pltpu.make_async_copy(ye_hbm.at[e, pl.ds(pl.multiple_of(starts[e], BF16_ROWS), win), :],
                                      buf.at[slot, pl.ds(e * win, win), :], sem.at[slot, e])
                for e in range(ne)]

    def scatter(rk_ref, slot0, slot, firsts, starts):
        hits = _window_hits(rk_ref, starts, slot0, win)
        hits = [h & ((rk_ref[0, e:e + 1, :] + slot0) >= firsts[e]) for e, h in enumerate(hits)]
        hit = jnp.where(jnp.concatenate(hits, axis=0), 1.0, 0.0).astype(BF16)
        hit_t = _dot_nt(eye, hit).astype(BF16)
        return _dot(hit_t, buf[slot])

    cur = b % 2

    @pl.when(b == 0)
    def _prime():
        for cp in windows(0, starts_of(0, 0)[1]):
            cp.start()

    @pl.when(b + 1 < nblk)
    def _prefetch():
        for cp in windows(1 - cur, starts_of(b + 1, 0)[1]):
            cp.start()

    firsts0, starts0 = starts_of(b, 0)
    for cp in windows(cur, starts0):
        cp.wait()

    def group_body(rk_ref, x1_ref, out_ref, g, local):
        acc_scr[...] = scatter(rk_ref, g["slot0"], cur, firsts0, starts0)

        def extra_round(r, carry):
            firsts, starts = starts_of(b, r)
            for cp in windows(2, starts):
                cp.start()
            for cp in windows(2, starts):
                cp.wait()
            acc_scr[...] += scatter(rk_ref, g["slot0"], 2, firsts, starts)
            return carry

        lax.fori_loop(1, rounds_ref[b], extra_round, 0)
        r = g["row0"] + local // g["blocks_per_batch"]
        g2 = mod_ref[pl.ds(r, 1), :][:, 5 * d:6 * d]
        out_ref[...] = _layer_norm(alpha * x1_ref[...] + g2 * acc_scr[...], lng_ref[...], lnb_ref[...])

    blk0 = 0
    for gi, g in enumerate(groups):
        @pl.when((b >= blk0) & (b < blk0 + g["nb"]))
        def _(gi=gi, g=g, blk0=blk0):
            group_body(rk_refs[gi], x1_refs[gi], out_refs[gi], g, b - blk0)
        blk0 += g["nb"]


def _combine(ye, groups, first, rounds, mod, ln_g, ln_b, alpha):
    d = groups[0]["x1"].shape[1]
    ne, tb = groups[0]["rank"].shape[1:]
    slots = ye.shape[1]
    meta, specs_rk, specs_x1 = [], [], []
    blk0 = 0
    for g in groups:
        nb = g["rank"].shape[0]
        meta.append(dict(nb=nb, slot0=g["slot0"], row0=g["row0"], blocks_per_batch=g["blocks_per_batch"]))
        local = lambda b, *_, blk0=blk0, nb=nb: jnp.clip(b - blk0, 0, nb - 1)
        specs_rk.append(pl.BlockSpec((1, ne, tb), lambda b, *_, local=local: (local(b), 0, 0)))
        specs_x1.append(pl.BlockSpec((tb, d), lambda b, *_, local=local: (local(b), 0)))
        blk0 += nb
    const = lambda shape: pl.BlockSpec(shape, lambda b, *_: (0,) * len(shape))
    return pl.pallas_call(
        functools.partial(_combine_kernel, groups=meta, alpha=alpha, slots=slots),
        out_shape=[jax.ShapeDtypeStruct(g["x1"].shape, F32) for g in groups],
        grid_spec=pltpu.PrefetchScalarGridSpec(
            num_scalar_prefetch=2,
            grid=(blk0,),
            in_specs=specs_rk + specs_x1 + [const(mod.shape), const((1, d)), const((1, d)),
                                            pl.BlockSpec(memory_space=pl.ANY)],
            out_specs=specs_x1,
            scratch_shapes=[pltpu.VMEM((3, ne * WIN_ROWS, d), ye.dtype), pltpu.VMEM((tb, d), F32),
                            pltpu.SemaphoreType.DMA((3, ne))]),
        compiler_params=_params("arbitrary"),
        name="combine",
    )(first, rounds, *[g["rank"] for g in groups], *[g["x1"] for g in groups], mod, ln_g, ln_b, ye)


def _prep_weights(w_in, q_norm, w_uq, kv_norm, w_ukv, w_o_mla, hgrn_norm, w_o_hg, w_out, ln1_g, ln1_b,
                  w_router):
    d = w_in.shape[0]
    q_lora, kv_lora = q_norm.shape[0], kv_norm.shape[0]
    hw = HG_HEADS * HG_DK
    hh, hp = MLA_HEADS, MLA_HEADS * LANES
    o_kv, o_pe = q_lora, q_lora + kv_lora
    o_h = o_pe + MLA_ROPE
    o_g = o_h + 5 * hw
    assert w_in.shape[1] == o_g + 2 * d
    qk = MLA_NOPE + MLA_ROPE
    kvw = MLA_NOPE + MLA_V
    b16 = lambda a: a.astype(BF16)
    assert all(o % BF16_ROWS == 0 for o in (o_kv, o_pe, o_h, o_g))
    win = b16(w_in.T)
    wuq = jnp.pad(w_uq.reshape(q_lora, hh, qk), ((0, 0), (0, 0), (0, LANES - qk))).reshape(q_lora, hp)
    ukv = w_ukv.reshape(kv_lora, hh, kvw)
    wk = jnp.pad(ukv[:, :, :MLA_NOPE], ((0, 0), (0, 0), (0, LANES - MLA_NOPE))).reshape(kv_lora, hp)
    wv = ukv[:, :, MLA_NOPE:].reshape(kv_lora, hh * MLA_V)
    return dict(
        win=win, qn=q_norm.reshape(1, -1), wuq=b16(wuq), kvn=kv_norm.reshape(1, -1), wk=b16(wk), wv=b16(wv),
        hgn=hgrn_norm.reshape(1, -1), womla=b16(w_o_mla), wohg=b16(w_o_hg), wout=b16(w_out),
        ln1g=ln1_g.reshape(1, -1), ln1b=ln1_b.reshape(1, -1), wr=b16(w_router.T))


def _rope_tables(seq):
    n_freq = MLA_ROPE // 4
    inv = ROPE_BASE ** (-np.arange(n_freq, dtype=np.float64) / n_freq)
    t = np.arange(seq)
    ang = np.concatenate([(t // GRID_W)[:, None] * inv, (t % GRID_W)[:, None] * inv], axis=-1)
    cos = np.repeat(np.cos(ang), 2, axis=1)
    sin = np.repeat(np.sin(ang), 2, axis=1) * np.tile([-1.0, 1.0], MLA_ROPE // 2)
    ck = np.pad(cos, ((0, 0), (0, LANES - MLA_ROPE)), constant_values=1.0)
    sk = np.pad(sin, ((0, 0), (0, LANES - MLA_ROPE)))
    cq = np.pad(cos, ((0, 0), (MLA_NOPE, LANES - MLA_NOPE - MLA_ROPE)), constant_values=1.0)
    sq = np.pad(sin, ((0, 0), (MLA_NOPE, LANES - MLA_NOPE - MLA_ROPE)))
    return tuple(jnp.asarray(a, F32) for a in (cq, sq, ck, sk))


def _window_sched(cnt, slot0):
    first = slot0 + cnt[:, :-1]
    end = slot0 + cnt[:, 1:]
    flat = lambda a: a.T.reshape(-1).astype(jnp.int32)
    rounds = jnp.max((end - (first // BF16_ROWS) * BF16_ROWS + WIN_ROWS - 1) // WIN_ROWS, axis=0)
    return flat(first), flat(end), jnp.maximum(rounds, 1).astype(jnp.int32)


def kernel(x_prompt, x_sample, c, cache_ckv, cache_kpe, state_hgrn, c_ctx, w_ada, b_ada, w_in, mla_q_norm, mla_w_uq, mla_kv_norm, mla_w_ukv, mla_w_o, hgrn_gamma, hgrn_norm, hgrn_w_o, w_out, ln1_g, ln1_b, moe_w_router, moe_w1, moe_w3, moe_w2, ln2_g, ln2_b):
    depth = w_ada.shape[0]
    assert depth == 1, "single trunk layer"
    bp, tp, d = x_prompt.shape
    bs, tsq, _ = x_sample.shape
    ne = moe_w_router.shape[-1]
    alpha = (2 * depth) ** 0.25
    past = cache_ckv.shape[2]
    assert tp % TOK_BLOCK == 0 and tsq % TOK_BLOCK == 0 and past % TOK_BLOCK == 0 and tsq % GRID_W == 0

    wts = _prep_weights(w_in[0], mla_q_norm[0], mla_w_uq[0], mla_kv_norm[0], mla_w_ukv[0], mla_w_o[0],
                        hgrn_norm[0], hgrn_w_o[0], w_out[0], ln1_g[0], ln1_b[0], moe_w_router[0])
    cond_rows = -(-(1 + bs) // SUBLANES) * SUBLANES
    cond = jnp.concatenate([c_ctx[None], c, jnp.zeros((cond_rows - 1 - bs, d), F32)], axis=0)
    mod = _adaln(cond, w_ada[0], b_ada[0])

    xs = [x_prompt.reshape(bp * tp, d), x_sample.reshape(bs * tsq, d)]
    dims = [(bp, tp), (bs, tsq)]
    rows = [(0, False), (1, True)]
    ropes = [None, _rope_tables(tsq)]
    kpe_c = jnp.pad(cache_kpe[:, 0].reshape(bs * past, MLA_ROPE), ((0, 0), (0, LANES - MLA_ROPE)))
    caches = [None, _kvup(cache_ckv[:, 0].reshape(bs * past, -1), kpe_c, wts)]
    inits = [None, state_hgrn[:, 0]]

    x1s, h2s, affs, extras = [], [], [], []
    for gi in range(2):
        (bt, sq), (row0, per_batch) = dims[gi], rows[gi]
        if sq == TOK_BLOCK and caches[gi] is None and ropes[gi] is None and not per_batch:
            x1, h2, aff, ckv, kpe, s_fin = _mixer(xs[gi], bt, sq, mod, hgrn_gamma, wts, alpha, ne, row0)
        else:
            q, k, v, ckv, kpe, hgx = _inproj(xs[gi], bt, sq, mod, hgrn_gamma, wts, row0, per_batch, ropes[gi])
            o_f, o_b, s_fin = _hgrn(hgx, bt, sq, inits[gi])
            x1, h2, aff = _postmix(xs[gi], bt, sq, mod, q, k, v, caches[gi], o_f, o_b, hgx, wts, alpha, ne,
                                   row0, per_batch)
        x1s.append(x1)
        h2s.append(h2)
        affs.append(aff)
        extras.append((ckv, kpe, s_fin))

    caps = [EC_FACTOR * x.shape[0] // ne for x in xs]
    slots = sum(caps)
    assert all(cp % BF16_ROWS == 0 for cp in caps) and slots >= WIN_ROWS
    groups, scheds = [], []
    slot0 = 0
    routed = _route(affs, caps)
    for gi in range(2):
        rank, cnt = routed[gi]
        groups.append(dict(h2=h2s[gi], rank=rank, aff=affs[gi], slot0=slot0, x1=x1s[gi], row0=rows[gi][0],
                           blocks_per_batch=dims[gi][1] // TOK_BLOCK if rows[gi][1] else 1 << 30))
        scheds.append(_window_sched(cnt, slot0))
        slot0 += caps[gi]
    first, end, rounds = [jnp.concatenate([s[k] for s in scheds]) for k in range(3)]
    xe, ge = _compact(groups, first, end, rounds, slots)
    ye = _ffn(xe, ge, moe_w1[0], moe_w3[0], moe_w2[0], slots, ft=moe_w1.shape[-1])
    outs = _combine(ye, groups, first, rounds, mod, ln2_g[0].reshape(1, -1), ln2_b[0].reshape(1, -1), alpha)

    ckv_p, kpe_p, st_p = extras[0]
    y_prompt = outs[0].reshape(bp, tp, d)
    y_sample = outs[1].reshape(bs, tsq, d)
    new_ckv = ckv_p.reshape(bp, 1, tp, -1)
    new_kpe = kpe_p[:, :MLA_ROPE].reshape(bp, 1, tp, MLA_ROPE)
    new_state = st_p.reshape(bp, 1, 2, HG_HEADS, HG_DK, HG_DV)
    return (y_prompt, y_sample, new_ckv, new_kpe, new_state)
```

```python
import functools

import jax
import jax.numpy as jnp
import numpy as np
from jax import lax
from jax.experimental import pallas as pl
from jax.experimental.pallas import tpu as pltpu

F32 = jnp.float32
BF16 = jnp.bfloat16

MLA_HEADS = 8
MLA_NOPE = 64
MLA_ROPE = 32
MLA_V = 64
HG_HEADS = 4
HG_DK = 128
HG_DV = 128
HG_CHUNK = 32
GRID_W = 64
ROPE_BASE = 10000.0
EC_FACTOR = 2
EPS = 1e-6

LANES = 128
SUBLANES = 8
BF16_ROWS = 16
VMEM_LIMIT = 56 * 1024 * 1024

TOK_BLOCK = 256
WIN_ROWS = 64
UNSELECTED = -(1 << 30)

NT_DIMS = (((1,), (1,)), ((), ()))


def _dot(a, b):
    return jnp.dot(a, b, preferred_element_type=F32)


def _dot_nt(a, b):
    return lax.dot_general(a, b, NT_DIMS, preferred_element_type=F32)


def _silu(x):
    return x * jax.nn.sigmoid(x)


def _params(*sem):
    return pltpu.CompilerParams(dimension_semantics=sem, vmem_limit_bytes=VMEM_LIMIT)


def _const_spec(shape):
    zeros = (0,) * len(shape)
    return pl.BlockSpec(shape, lambda *_: zeros, pipeline_mode=pl.Buffered(1))


def _adaln_kernel(c_ref, w_ref, b_ref, o_ref):
    s = _silu(c_ref[...]).astype(BF16)
    o_ref[...] = _dot(s, w_ref[...].astype(BF16)) + b_ref[...]


def _adaln(cond, w_ada, b_ada):
    rows, d = cond.shape
    n = w_ada.shape[1]
    tn = n // 4
    return pl.pallas_call(
        _adaln_kernel,
        out_shape=jax.ShapeDtypeStruct((rows, n), F32),
        grid=(n // tn,),
        in_specs=[_const_spec((rows, d)),
                  pl.BlockSpec((d, tn), lambda j: (0, j)),
                  pl.BlockSpec((1, tn), lambda j: (0, j))],
        out_specs=pl.BlockSpec((rows, tn), lambda j: (0, j)),
        compiler_params=_params("arbitrary"),
        name="adaln",
    )(cond, w_ada, b_ada.reshape(1, n))


def _rms(x, g):
    return x * lax.rsqrt(jnp.mean(x * x, axis=-1, keepdims=True) + EPS) * g


def _rope(x, c, s):
    w = x.shape[-1]
    lane = lax.broadcasted_iota(jnp.int32, x.shape, 1)
    nxt = pltpu.roll(x, w - 1, 1)
    prv = pltpu.roll(x, 1, 1)
    return x * c + jnp.where(lane % 2 == 0, nxt, prv) * s


N_INPROJ_WEIGHTS = 6


def _mod_row(mod_ref, row0, per_batch):
    r = row0 + pl.program_id(0) if per_batch else row0
    return mod_ref[pl.ds(r, 1), :]


def _modulated(x_ref, m):
    d = x_ref.shape[1]
    return (x_ref[...] * (1.0 + m[:, d:2 * d]) + m[:, 0:d]).astype(BF16)


def _keys(k_nope, kpe):
    shared = pltpu.roll(kpe, MLA_NOPE, 1)
    return (k_nope + jnp.concatenate([shared] * MLA_HEADS, axis=1)).astype(BF16)


def _inproj_kernel(*refs, row0, per_batch, rope):
    x_ref, mod_ref = refs[:2]
    _inproj_body(x_ref, _mod_row(mod_ref, row0, per_batch), *refs[2:], rope=rope)


def _inproj_body(x_ref, m, *refs, rope):
    gam_ref, win_ref, qn_ref, wuq_ref, kvn_ref, wk_ref, wv_ref = refs[:1 + N_INPROJ_WEIGHTS]
    refs = refs[1 + N_INPROJ_WEIGHTS:]
    if rope:
        cq_ref, sq_ref, ck_ref, sk_ref = refs[:4]
        refs = refs[4:]
    q_o, k_o, v_o, ckv_o, kpe_o, hgx_o = refs
    h = _modulated(x_ref, m)
    hw = HG_HEADS * HG_DK
    o_kv = qn_ref.shape[1]
    o_pe = o_kv + kvn_ref.shape[1]
    o_h = o_pe + MLA_ROPE

    cq = _rms(_dot_nt(h, win_ref[0:o_kv, :]), qn_ref[...])
    q = _dot(cq.astype(BF16), wuq_ref[...])
    if rope:
        q = _rope(q, jnp.concatenate([cq_ref[...]] * MLA_HEADS, axis=1),
                  jnp.concatenate([sq_ref[...]] * MLA_HEADS, axis=1))
    q_o[...] = q.astype(BF16)

    ckv = _rms(_dot_nt(h, win_ref[o_kv:o_pe, :]), kvn_ref[...])
    ckv_o[...] = ckv
    kpe = _dot_nt(h, win_ref[o_pe:o_h, :])
    kpe = jnp.concatenate([kpe, jnp.zeros((kpe.shape[0], LANES - MLA_ROPE), F32)], axis=1)
    if rope:
        kpe = _rope(kpe, ck_ref[...], sk_ref[...])
    kpe_o[...] = kpe
    cb = ckv.astype(BF16)
    k_o[...] = _keys(_dot(cb, wk_ref[...]), kpe)
    v_o[...] = _dot(cb, wv_ref[...]).astype(BF16)

    z = _dot_nt(h, win_ref[o_h:o_h + 5 * hw, :])
    hgx_o[:, 0:hw] = _silu(z[:, 0:hw])
    for dr in range(2):
        g0, g1 = gam_ref[dr, 0:1, :], gam_ref[dr, 1:2, :]
        gmax = jnp.maximum(g0, g1)
        e0, e1 = jnp.exp(g0 - gmax), jnp.exp(g1 - gmax)
        lb = e0 / (e0 + e1)
        f = lb + (1.0 - lb) * jax.nn.sigmoid(z[:, (1 + dr) * hw:(2 + dr) * hw])
        hgx_o[:, (1 + 2 * dr) * hw:(2 + 2 * dr) * hw] = jnp.log(f)
        hgx_o[:, (2 + 2 * dr) * hw:(3 + 2 * dr) * hw] = 1.0 - f
    hgx_o[:, 5 * hw:6 * hw] = z[:, 3 * hw:4 * hw]
    hgx_o[:, 6 * hw:7 * hw] = z[:, 4 * hw:5 * hw]


def _inproj(x2d, batch, seq, mod, gamma, wts, row0, per_batch, rope_tabs):
    n, d = x2d.shape
    tm = MIXER_SEQS * TOK_BLOCK if seq % (MIXER_SEQS * TOK_BLOCK) == 0 else TOK_BLOCK
    nblk = seq // tm
    rope = rope_tabs is not None
    hp = MLA_HEADS * LANES
    hw = HG_HEADS * HG_DK
    tok = lambda b, i: (b * nblk + i, 0)
    pos = lambda b, i: (i, 0)
    weights = [wts[k] for k in INPROJ_KEYS]
    ins = [x2d, mod, gamma] + weights
    in_specs = ([pl.BlockSpec((tm, d), tok), _const_spec(mod.shape), _const_spec(gamma.shape)]
                + [_const_spec(w.shape) for w in weights])
    if rope:
        ins += list(rope_tabs)
        in_specs += [pl.BlockSpec((tm, t.shape[1]), pos) for t in rope_tabs]
    widths = [(hp, BF16), (hp, BF16), (MLA_HEADS * MLA_V, BF16), (wts["kvn"].shape[1], F32), (LANES, F32),
              (7 * hw, F32)]
    return pl.pallas_call(
        functools.partial(_inproj_kernel, row0=row0, per_batch=per_batch, rope=rope),
        out_shape=[jax.ShapeDtypeStruct((n, w), dt) for w, dt in widths],
        grid=(batch, nblk),
        in_specs=in_specs,
        out_specs=[pl.BlockSpec((tm, w), tok) for w, _ in widths],
        compiler_params=_params("arbitrary", "arbitrary"),
        name="inproj",
    )(*ins)


def _kvup_kernel(ckv_ref, kpe_ref, wk_ref, wv_ref, k_o, v_o):
    cb = ckv_ref[...].astype(BF16)
    k_o[...] = _keys(_dot(cb, wk_ref[...]), kpe_ref[...])
    v_o[...] = _dot(cb, wv_ref[...]).astype(BF16)


def _kvup(ckv2d, kpe2d, wts):
    n = ckv2d.shape[0]
    tm = TOK_BLOCK
    widths = [MLA_HEADS * LANES, MLA_HEADS * MLA_V]
    row = lambda i: (i, 0)
    ws = [wts["wk"], wts["wv"]]
    return pl.pallas_call(
        _kvup_kernel,
        out_shape=[jax.ShapeDtypeStruct((n, w), BF16) for w in widths],
        grid=(n // tm,),
        in_specs=[pl.BlockSpec((tm, ckv2d.shape[1]), row), pl.BlockSpec((tm, LANES), row)]
                 + [_const_spec(w.shape) for w in ws],
        out_specs=[pl.BlockSpec((tm, w), row) for w in widths],
        compiler_params=_params("arbitrary"),
        name="kvup",
    )(ckv2d, kpe2d, *ws)


ATTN_SCALE = (MLA_NOPE + MLA_ROPE) ** -0.5


def _attn_body(q_ref, k_ref, v_ref, kc_ref, vc_ref, o_ref):
    cached = kc_ref is not None
    scale = ATTN_SCALE
    per_slab = LANES // MLA_V
    own = lax.broadcasted_iota(jnp.int32, (q_ref.shape[0], LANES), 1) // MLA_V
    for slab in range(MLA_HEADS // per_slab):
        vsl = slice(slab * LANES, (slab + 1) * LANES)
        out = None
        for sub in range(per_slab):
            hd = slab * per_slab + sub
            sl = slice(hd * LANES, (hd + 1) * LANES)
            q = q_ref[:, sl]
            s = _dot_nt(q, k_ref[:, sl]) * scale
            mx = jnp.max(s, axis=-1, keepdims=True)
            if cached:
                s2 = _dot_nt(q, kc_ref[:, sl]) * scale
                mx = jnp.maximum(mx, jnp.max(s2, axis=-1, keepdims=True))
            e = jnp.exp(s - mx)
            den = jnp.sum(e, axis=-1, keepdims=True)
            o = _dot(e.astype(BF16), v_ref[:, vsl])
            if cached:
                e2 = jnp.exp(s2 - mx)
                den = den + jnp.sum(e2, axis=-1, keepdims=True)
                o = o + _dot(e2.astype(BF16), vc_ref[:, vsl])
            o = o / den
            out = o if out is None else jnp.where(own == sub, o, out)
        o_ref[:, vsl] = out.astype(o_ref.dtype)


def _chunk_scan(x, reverse):
    tm = x.shape[0]
    rin = lax.broadcasted_iota(jnp.int32, x.shape, 0) % HG_CHUNK
    step = 1
    while step < HG_CHUNK:
        if reverse:
            x = x + jnp.where(rin < HG_CHUNK - step, pltpu.roll(x, tm - step, 0), 0.0)
        else:
            x = x + jnp.where(rin >= step, pltpu.roll(x, step, 0), 0.0)
        step *= 2
    return x


def _hgrn_kernel(*refs, has_init):
    fwd, bwd = refs[0:4], refs[4:8]
    refs = refs[8:]
    s0_ref = None
    if has_init:
        s0_ref = refs[0]
        refs = refs[1:]
    of_ref, ob_ref, sfin_ref, st_scr = refs
    i = pl.program_id(1)
    _hgrn_body([(fwd, bwd, of_ref, ob_ref, sfin_ref)], s0_ref, st_scr, i == 0, i == pl.num_programs(1) - 1)


def _hgrn_body(jobs, s0_ref, st_scr, first, last):
    tm = jobs[0][0][0].shape[0]
    c = HG_CHUNK
    nch = tm // c
    dk, dv = HG_DK, HG_DV
    hw = HG_HEADS * dk

    def initial(dr, hd):
        return s0_ref[0, dr, hd].T if s0_ref is not None else jnp.zeros((dv, dk), F32)

    if st_scr is not None:
        @pl.when(first)
        def _init():
            for dr in range(2):
                for hd in range(HG_HEADS):
                    st_scr[dr, hd] = initial(dr, hd)

    npair = nch // 2
    pair = 2 * c
    row = lax.broadcasted_iota(jnp.int32, (tm, tm), 0)
    col = lax.broadcasted_iota(jnp.int32, (tm, tm), 1)
    same = (row // c) == (col // c)
    same_pair = (row // pair) == (col // pair)
    bd = (lax.broadcasted_iota(jnp.int32, (tm, npair * dk), 0) // pair
          == lax.broadcasted_iota(jnp.int32, (tm, npair * dk), 1) // dk)
    chunk_odd = (lax.broadcasted_iota(jnp.int32, (tm, hw), 0) // c) % 2 == 1

    for dr in range(2):
        tri = same & ((col <= row) if dr == 0 else (col >= row))
        cross = same_pair & (((row // c) > (col // c)) if dr == 0 else ((row // c) < (col // c)))
        second = chunk_odd if dr == 0 else ~chunk_odd
        order = range(npair) if dr == 0 else range(npair - 1, -1, -1)

        def decayed(job):
            hq_ref, lf_ref, kk_ref, vv_ref = job[dr]
            bcum = _chunk_scan(lf_ref[...], reverse=dr == 1)
            closing = c - 1 if dr == 0 else 0
            btot3 = bcum.reshape(nch, c, hw)[:, closing:closing + 1, :]
            btot = jnp.broadcast_to(btot3, (nch, c, hw)).reshape(tm, hw)
            bpart = jnp.where(chunk_odd, pltpu.roll(btot, c, 0), pltpu.roll(btot, tm - c, 0))
            epart = jnp.exp(bpart)
            kk = kk_ref[...]
            qd = hq_ref[...] * jnp.exp(bcum)
            kd = kk * jnp.exp(-bcum)
            ke = kk * jnp.exp(btot - bcum)
            qd2 = jnp.where(second, qd * epart, qd)
            ke2 = jnp.where(second, ke, ke * epart)
            return qd, kd, ke, qd2, ke2, btot + bpart, vv_ref[...]

        def head(job, hd, qd, kd, ke, qd2, ke2, bpair, vv):
            o_ref, sfin_ref = job[2 + dr], job[4]
            sl = slice(hd * dk, (hd + 1) * dk)
            qd_h = qd[:, sl].astype(BF16)
            v_h = vv[:, hd * dv:(hd + 1) * dv]
            a = jnp.where(tri, _dot_nt(qd_h, kd[:, sl].astype(BF16)),
                          jnp.where(cross, _dot_nt(qd_h, ke[:, sl].astype(BF16)), 0.0))
            o_intra = _dot(a.astype(BF16), v_h.astype(BF16))
            kebd = jnp.where(bd, jnp.concatenate([ke2[:, sl]] * npair, axis=1), 0.0).astype(BF16)
            qbd = jnp.where(bd, jnp.concatenate([qd2[:, sl]] * npair, axis=1), 0.0).astype(BF16)
            ut = _dot(v_h.T.astype(BF16), kebd)
            st = st_scr[dr, hd] if st_scr is not None else initial(dr, hd)
            prev = [None] * npair
            for p in order:
                prev[p] = st
                st = st * jnp.exp(bpair[p * pair:p * pair + 1, sl]) + ut[:, p * dk:(p + 1) * dk]
            if st_scr is not None:
                st_scr[dr, hd] = st
            o_inter = _dot_nt(qbd, jnp.concatenate(prev, axis=1).astype(BF16))
            o_ref[:, hd * dv:(hd + 1) * dv] = o_intra + o_inter

            if last is True:
                sfin_ref[0, dr, hd] = st.T
            else:
                @pl.when(last)
                def _final():
                    sfin_ref[0, dr, hd] = st.T

        prepared = [decayed(job) for job in jobs]
        for hd in range(HG_HEADS):
            for job, arrays in zip(jobs, prepared):
                head(job, hd, *arrays)


def _hgrn(hgx, batch, seq, s0=None):
    n = hgx.shape[0]
    tm = TOK_BLOCK
    nblk = seq // tm
    hw = HG_HEADS * HG_DK

    def spec(lane_blk, rev):
        if rev:
            return pl.BlockSpec((tm, hw), lambda b, i: (b * nblk + nblk - 1 - i, lane_blk))
        return pl.BlockSpec((tm, hw), lambda b, i: (b * nblk + i, lane_blk))

    in_specs = [spec(0, False), spec(1, False), spec(2, False), spec(5, False),
                spec(0, True), spec(3, True), spec(4, True), spec(5, True)]
    ins = [hgx] * 8
    st_shape = (1, 2, HG_HEADS, HG_DK, HG_DV)
    st_spec = pl.BlockSpec(st_shape, lambda b, i: (b, 0, 0, 0, 0))
    if s0 is not None:
        ins.append(s0)
        in_specs.append(st_spec)
    return pl.pallas_call(
        functools.partial(_hgrn_kernel, has_init=s0 is not None),
        out_shape=[jax.ShapeDtypeStruct((n, hw), F32), jax.ShapeDtypeStruct((n, hw), F32),
                   jax.ShapeDtypeStruct((batch,) + st_shape[1:], F32)],
        grid=(batch, nblk),
        in_specs=in_specs,
        out_specs=[spec(0, False), spec(0, True), st_spec],
        scratch_shapes=[pltpu.VMEM((2, HG_HEADS, HG_DV, HG_DK), F32)],
        compiler_params=_params("arbitrary", "arbitrary"),
        name="hgrn",
    )(*ins)


def _layer_norm(x, g, b):
    xc = x - jnp.mean(x, axis=-1, keepdims=True)
    var = jnp.mean(xc * xc, axis=-1, keepdims=True)
    return xc * lax.rsqrt(var + EPS) * g + b


N_POSTMIX_WEIGHTS = 8
INPROJ_KEYS = ("win", "qn", "wuq", "kvn", "wk", "wv")
POSTMIX_KEYS = ("hgn", "womla", "wohg", "wout", "ln1g", "ln1b", "wr")
MIXER_SEQS = 2


def _postmix_kernel(x_ref, mod_ref, *refs, alpha, row0, per_batch, cached):
    n_attn = 5 if cached else 3
    q_ref, k_ref, v_ref = refs[:3]
    kc_ref, vc_ref = refs[3:5] if cached else (None, None)
    of_ref, ob_ref, zg_ref = refs[n_attn:n_attn + 3]
    om_s = refs[-1]
    _attn_body(q_ref, k_ref, v_ref, kc_ref, vc_ref, om_s)
    _postmix_body(x_ref, _mod_row(mod_ref, row0, per_batch), of_ref, ob_ref, zg_ref, om_s,
                  *refs[n_attn + 3:-1], alpha=alpha)


def _postmix_body(x_ref, m, of_ref, ob_ref, zg_ref, om_ref, wg_ref, hgn_ref, womla_ref,
                  wohg_ref, wout_ref, lng_ref, lnb_ref, wr_ref, x1_o, h2_o, aff_o, *, alpha):
    d = x_ref.shape[1]
    tb = aff_o.shape[2]
    g1, sh2, sc2 = m[:, 2 * d:3 * d], m[:, 3 * d:4 * d], m[:, 4 * d:5 * d]
    o = of_ref[...] + ob_ref[...]
    zg = zg_ref[...]
    parts = []
    for hd in range(HG_HEADS):
        sl = slice(hd * HG_DV, (hd + 1) * HG_DV)
        parts.append(_rms(o[:, sl], hgn_ref[...]) * _silu(zg[:, sl]))
    ohg = jnp.concatenate(parts, axis=1).astype(BF16)
    gates = _dot_nt(_modulated(x_ref, m), wg_ref[wg_ref.shape[0] - 2 * d:, :])
    merged = (jax.nn.sigmoid(gates[:, 0:d]) * _dot(om_ref[...], womla_ref[...])
              + jax.nn.sigmoid(gates[:, d:2 * d]) * _dot(ohg, wohg_ref[...]))
    mix = _dot(merged.astype(BF16), wout_ref[...])
    x1 = _layer_norm(alpha * x_ref[...] + g1 * mix, lng_ref[...], lnb_ref[...])
    x1_o[...] = x1
    h2 = (x1 * (1.0 + sc2) + sh2).astype(BF16)
    h2_o[...] = h2
    logits = _dot_nt(wr_ref[...], h2)
    e = jnp.exp(logits - jnp.max(logits, axis=0, keepdims=True))
    aff = e / jnp.sum(e, axis=0, keepdims=True)
    for blk in range(aff_o.shape[0]):
        aff_o[blk] = aff[:, blk * tb:(blk + 1) * tb]


def _postmix(x2d, batch, seq, mod, q, k, v, cache, o_f, o_b, hgx, wts, alpha, n_experts, row0, per_batch):
    n, d = x2d.shape
    tm = TOK_BLOCK
    nblk = seq // tm
    hw = HG_HEADS * HG_DV
    hp, hv = q.shape[1], v.shape[1]
    tok = lambda b, i: (b * nblk + i, 0)
    per_seq = lambda b, i: (b, 0)
    weights = [wts[k] for k in POSTMIX_KEYS]
    attn_ins = [q, k, v]
    attn_specs = [pl.BlockSpec((tm, hp), tok), pl.BlockSpec((seq, hp), per_seq), pl.BlockSpec((seq, hv), per_seq)]
    if cache is not None:
        past = cache[0].shape[0] // batch
        attn_ins += list(cache)
        attn_specs += [pl.BlockSpec((past, hp), per_seq), pl.BlockSpec((past, hv), per_seq)]
    return pl.pallas_call(
        functools.partial(_postmix_kernel, alpha=alpha, row0=row0, per_batch=per_batch,
                          cached=cache is not None),
        out_shape=[jax.ShapeDtypeStruct((n, d), F32), jax.ShapeDtypeStruct((n, d), BF16),
                   jax.ShapeDtypeStruct((n // tm, n_experts, tm), F32)],
        grid=(batch, nblk),
        in_specs=[pl.BlockSpec((tm, d), tok), _const_spec(mod.shape)] + attn_specs
                 + [pl.BlockSpec((tm, hw), tok), pl.BlockSpec((tm, hw), tok),
                    pl.BlockSpec((tm, hw), lambda b, i: (b * nblk + i, 6)), _const_spec(wts["win"].shape)]
                 + [_const_spec(w.shape) for w in weights],
        out_specs=[pl.BlockSpec((tm, d), tok), pl.BlockSpec((tm, d), tok),
                   pl.BlockSpec((1, n_experts, tm), lambda b, i: (b * nblk + i, 0, 0))],
        scratch_shapes=[pltpu.VMEM((tm, hv), BF16)],
        compiler_params=_params("arbitrary", "arbitrary"),
        name="postmix",
    )(x2d, mod, *attn_ins, o_f, o_b, hgx, wts["win"], *weights)


def _mixer_kernel(x_ref, mod_ref, *refs, alpha, row0, seq):
    nw = 1 + N_INPROJ_WEIGHTS
    in_w, refs = refs[:nw], refs[nw:]
    pm_w, refs = refs[:N_POSTMIX_WEIGHTS - 1], refs[N_POSTMIX_WEIGHTS - 1:]
    pm_w = (in_w[1],) + tuple(pm_w)
    x1_o, h2_o, aff_o, ckv_o, kpe_o, sfin_o, q_s, k_s, v_s, hgx_s, om_s, of_s, ob_s = refs
    m = _mod_row(mod_ref, row0, False)
    _inproj_body(x_ref, m, *in_w, q_s, k_s, v_s, ckv_o, kpe_o, hgx_s, rope=False)
    hw = HG_HEADS * HG_DK
    jobs = []
    for s in range(x_ref.shape[0] // seq):
        rows = slice(s * seq, (s + 1) * seq)
        _attn_body(q_s.at[rows], k_s.at[rows], v_s.at[rows], None, None, om_s.at[rows])
        lane = lambda j, rows=rows: hgx_s.at[rows, j * hw:(j + 1) * hw]
        jobs.append(((lane(0), lane(1), lane(2), lane(5)), (lane(0), lane(3), lane(4), lane(5)),
                     of_s.at[rows], ob_s.at[rows], sfin_o.at[s:s + 1]))
    _hgrn_body(jobs, None, None, True, True)
    _postmix_body(x_ref, m, of_s, ob_s, hgx_s.at[:, 6 * hw:7 * hw], om_s, *pm_w, x1_o, h2_o, aff_o,
                  alpha=alpha)


def _mixer(x2d, batch, seq, mod, gamma, wts, alpha, n_experts, row0):
    n, d = x2d.shape
    assert seq == TOK_BLOCK
    ns = MIXER_SEQS if batch % MIXER_SEQS == 0 else 1
    tm = ns * seq
    hp = MLA_HEADS * LANES
    hv = MLA_HEADS * MLA_V
    hw = HG_HEADS * HG_DK
    kvl = wts["kvn"].shape[1]
    weights = [wts[k] for k in INPROJ_KEYS + POSTMIX_KEYS]
    tok = lambda b: (b, 0)
    st_shape = (ns, 2, HG_HEADS, HG_DK, HG_DV)
    return pl.pallas_call(
        functools.partial(_mixer_kernel, alpha=alpha, row0=row0, seq=seq),
        out_shape=[jax.ShapeDtypeStruct((n, d), F32), jax.ShapeDtypeStruct((n, d), BF16),
                   jax.ShapeDtypeStruct((n // seq, n_experts, seq), F32),
                   jax.ShapeDtypeStruct((n, kvl), F32), jax.ShapeDtypeStruct((n, LANES), F32),
                   jax.ShapeDtypeStruct((batch,) + st_shape[1:], F32)],
        grid=(batch // ns,),
        in_specs=[pl.BlockSpec((tm, d), tok), _const_spec(mod.shape), _const_spec(gamma.shape)]
                 + [_const_spec(w.shape) for w in weights],
        out_specs=[pl.BlockSpec((tm, d), tok), pl.BlockSpec((tm, d), tok),
                   pl.BlockSpec((ns, n_experts, seq), lambda b: (b, 0, 0)),
                   pl.BlockSpec((tm, kvl), tok), pl.BlockSpec((tm, LANES), tok),
                   pl.BlockSpec(st_shape, lambda b: (b, 0, 0, 0, 0))],
        scratch_shapes=[pltpu.VMEM((tm, hp), BF16), pltpu.VMEM((tm, hp), BF16), pltpu.VMEM((tm, hv), BF16),
                        pltpu.VMEM((tm, 7 * hw), F32), pltpu.VMEM((tm, hv), BF16),
                        pltpu.VMEM((tm, hw), F32), pltpu.VMEM((tm, hw), F32)],
        compiler_params=_params("arbitrary"),
        name="mixer",
    )(x2d, mod, gamma, *weights)


def _route_kernel(*refs, caps):
    ng = len(caps)
    for aff_ref, rank_o, cnt_o, cap in zip(refs[0:ng], refs[ng:2 * ng], refs[2 * ng:3 * ng], caps):
        _route_group(aff_ref, rank_o, cnt_o, cap)


def _route_group(aff_ref, rank_o, cnt_o, cap):
    nb, ne, tb = aff_ref.shape
    key = aff_ref[...]

    def count(mask):
        return jnp.sum(jnp.sum(jnp.where(mask, 1.0, 0.0), axis=0), axis=1, keepdims=True)

    def bit_step(it, bits):
        cand = bits | jnp.left_shift(jnp.int32(1), 30 - it)
        return jnp.where(count(key >= pltpu.bitcast(cand, F32)[None]) >= cap, cand, bits)

    bits = lax.fori_loop(0, 31, bit_step, jnp.zeros((ne, 1), jnp.int32))
    thr = pltpu.bitcast(bits, F32)
    need = cap - count(key > thr[None])
    before = (lax.broadcasted_iota(jnp.int32, (tb, tb), 0)
              < lax.broadcasted_iota(jnp.int32, (tb, tb), 1))
    before = jnp.where(before, 1.0, 0.0).astype(BF16)
    off_eq = jnp.zeros((ne, 1), F32)
    off_sel = jnp.zeros((ne, 1), F32)
    cnt_o[...] = jnp.zeros_like(cnt_o)
    for blk in range(nb):
        key_b = key[blk]
        eq = key_b == thr
        eq_b = jnp.where(eq, 1.0, 0.0)
        eq_rank = _dot(eq_b.astype(BF16), before) + off_eq
        sel = (key_b > thr) | (eq & (eq_rank < need))
        sel_b = jnp.where(sel, 1.0, 0.0)
        rank = _dot(sel_b.astype(BF16), before) + off_sel
        rank_o[blk] = jnp.where(sel, rank.astype(jnp.int32), UNSELECTED)
        cnt_o[:, blk:blk + 1] = off_sel.astype(jnp.int32)
        off_eq = off_eq + jnp.sum(eq_b, axis=1, keepdims=True)
        off_sel = off_sel + jnp.sum(sel_b, axis=1, keepdims=True)
    cnt_o[:, nb:nb + 1] = off_sel.astype(jnp.int32)


def _route(affs, caps):
    ng = len(affs)
    ne = affs[0].shape[1]
    assert all(a.shape[0] + 1 <= LANES for a in affs)
    outs = pl.pallas_call(
        functools.partial(_route_kernel, caps=tuple(caps)),
        out_shape=[jax.ShapeDtypeStruct(a.shape, jnp.int32) for a in affs]
                  + [jax.ShapeDtypeStruct((ne, LANES), jnp.int32)] * ng,
        in_specs=[pl.BlockSpec(memory_space=pltpu.VMEM)] * ng,
        out_specs=[pl.BlockSpec(memory_space=pltpu.VMEM)] * (2 * ng),
        compiler_params=pltpu.CompilerParams(vmem_limit_bytes=VMEM_LIMIT),
        name="route",
    )(*affs)
    return [(outs[gi], outs[ng + gi][:, :affs[gi].shape[0] + 1]) for gi in range(ng)]


def _window_hits(rk_ref, firsts, slot0, win):
    ne, tb = rk_ref.shape[1], rk_ref.shape[2]
    win_iota = lax.broadcasted_iota(jnp.int32, (win, tb), 0)
    return [(rk_ref[0, e:e + 1, :] + (slot0 - firsts[e])) == win_iota for e in range(ne)]


def _compact_kernel(first_ref, end_ref, rounds_ref, *refs, groups, slots):
    ng = len(groups)
    h2_refs, rk_refs, af_refs = refs[0:ng], refs[ng:2 * ng], refs[2 * ng:3 * ng]
    hbms, refs = refs[3 * ng:3 * ng + 2], refs[3 * ng + 2:]
    stages, tails = refs[0:2], refs[2:4]
    sem, issued = refs[4:]
    b = pl.program_id(0)
    ne = rk_refs[0].shape[1]
    win = WIN_ROWS
    sub = BF16_ROWS

    def copies(slot, dsts):
        return [pltpu.make_async_copy(stage.at[slot, pl.ds(e * win, win), :],
                                      hbm.at[e, pl.ds(pl.multiple_of(dsts[e], sub), win), :], sem.at[c, e])
                for c, (stage, hbm) in enumerate(zip(stages, hbms)) for e in range(ne)]

    def wait_previous():
        @pl.when(issued[0] > 0)
        def _():
            for cp in copies(0, [0] * ne):
                cp.wait()

    @pl.when(b == 0)
    def _init():
        issued[0] = 0
        for stage, tail in zip(stages, tails):
            tail[...] = jnp.zeros_like(tail)
            stage[1] = jnp.zeros(stage.shape[1:], stage.dtype)
        pad = copies(1, [slots] * ne)
        for cp in pad:
            cp.start()
        for cp in pad:
            cp.wait()

    def group_body(h2_ref, rk_ref, af_ref, slot0):
        firsts = [first_ref[b * ne + e] for e in range(ne)]
        bases = [(f // sub) * sub for f in firsts]
        ends = [end_ref[b * ne + e] - bases[e] for e in range(ne)]

        def one_round(r, carry):
            dsts = [bases[e] + r * win for e in range(ne)]
            hits = _window_hits(rk_ref, dsts, slot0, win)
            onehot = jnp.where(jnp.concatenate(hits, axis=0), 1.0, 0.0).astype(BF16)
            gate = jnp.concatenate(
                [jnp.sum(jnp.where(hits[e], af_ref[0, e:e + 1, :], 0.0), axis=1, keepdims=True)
                 for e in range(ne)], axis=0)
            slot = issued[0] % 2
            streams = (_dot(onehot, h2_ref[...]).astype(BF16),
                       jnp.broadcast_to(gate, (ne * win, LANES)))
            for val, stage, tail in zip(streams, stages, tails):
                sub_iota = lax.broadcasted_iota(jnp.int32, (sub, val.shape[1]), 0)
                pieces = []
                for e in range(ne):
                    old = tail[e * sub:(e + 1) * sub, :]
                    shared = jnp.where(r == 0, firsts[e] - bases[e], 0)
                    groups_e = [val[e * win + g * sub:e * win + (g + 1) * sub, :] for g in range(win // sub)]
                    groups_e[0] = jnp.where(sub_iota < shared, old, groups_e[0])
                    pieces += groups_e
                    last = (ends[e] // sub) * sub
                    new = old
                    for g, grp in enumerate(groups_e):
                        new = jnp.where((r == last // win) & (last % win == g * sub), grp, new)
                    tail[e * sub:(e + 1) * sub, :] = new
                stage[slot] = jnp.concatenate(pieces, axis=0)
            wait_previous()
            for cp in copies(slot, [jnp.minimum(dst, slots) for dst in dsts]):
                cp.start()
            issued[0] = issued[0] + 1
            return carry

        lax.fori_loop(0, rounds_ref[b], one_round, 0)

    blk0 = 0
    for gi, g in enumerate(groups):
        @pl.when((b >= blk0) & (b < blk0 + g["nb"]))
        def _(gi=gi, g=g):
            group_body(h2_refs[gi], rk_refs[gi], af_refs[gi], g["slot0"])
        blk0 += g["nb"]

    @pl.when(b == pl.num_programs(0) - 1)
    def _drain():
        wait_previous()


def _compact(groups, first, end, rounds, slots):
    d = groups[0]["h2"].shape[1]
    nbs = [g["rank"].shape[0] for g in groups]
    ne, tb = groups[0]["rank"].shape[1:]
    meta, specs_h2, specs_rk = [], [], []
    blk0 = 0
    for g, nb in zip(groups, nbs):
        meta.append(dict(nb=nb, slot0=g["slot0"]))
        local = lambda b, *_, blk0=blk0, nb=nb: jnp.clip(b - blk0, 0, nb - 1)
        specs_h2.append(pl.BlockSpec((tb, d), lambda b, *_, local=local: (local(b), 0)))
        specs_rk.append(pl.BlockSpec((1, ne, tb), lambda b, *_, local=local: (local(b), 0, 0)))
        blk0 += nb
    streams = [(d, BF16), (LANES, F32)]
    return pl.pallas_call(
        functools.partial(_compact_kernel, groups=meta, slots=slots),
        out_shape=[jax.ShapeDtypeStruct((ne, slots + WIN_ROWS, w), dt) for w, dt in streams],
        grid_spec=pltpu.PrefetchScalarGridSpec(
            num_scalar_prefetch=3,
            grid=(sum(nbs),),
            in_specs=specs_h2 + specs_rk + specs_rk,
            out_specs=[pl.BlockSpec(memory_space=pl.ANY)] * 2,
            scratch_shapes=[pltpu.VMEM((2, ne * WIN_ROWS, w), dt) for w, dt in streams]
                           + [pltpu.VMEM((ne * BF16_ROWS, w), dt) for w, dt in streams]
                           + [pltpu.SemaphoreType.DMA((2, ne)), pltpu.SMEM((1,), jnp.int32)]),
        compiler_params=_params("arbitrary"),
        name="compact",
    )(first, end, rounds, *[g["h2"] for g in groups], *[g["rank"] for g in groups],
      *[g["aff"] for g in groups])


def _ffn_kernel(xe_ref, ge_ref, w1_ref, w3_ref, w2_ref, ye_ref, *scratch):
    f = pl.program_id(1)
    x = xe_ref[0]
    hid = _silu(_dot(x, w1_ref[0].astype(BF16))) * _dot(x, w3_ref[0].astype(BF16))
    y = _dot(hid.astype(BF16), w2_ref[0].astype(BF16))
    if not scratch:
        ye_ref[0] = (y * ge_ref[0, :, 0:1]).astype(ye_ref.dtype)
        return
    acc_scr, = scratch
    last = pl.num_programs(1) - 1

    @pl.when(f == 0)
    def _first():
        acc_scr[...] = y

    @pl.when((f > 0) & (f < last))
    def _middle():
        acc_scr[...] += y

    @pl.when(f == last)
    def _last():
        ye_ref[0] = ((acc_scr[...] + y) * ge_ref[0, :, 0:1]).astype(ye_ref.dtype)


def _ffn(xe, ge, w1, w3, w2, slots, ft):
    ne, d, dff = w1.shape
    nf = dff // ft
    return pl.pallas_call(
        _ffn_kernel,
        out_shape=jax.ShapeDtypeStruct((ne, slots, d), BF16),
        grid=(ne, nf),
        in_specs=[pl.BlockSpec((1, slots, d), lambda e, f: (e, 0, 0)),
                  pl.BlockSpec((1, slots, ge.shape[2]), lambda e, f: (e, 0, 0)),
                  pl.BlockSpec((1, d, ft), lambda e, f: (e, 0, f)),
                  pl.BlockSpec((1, d, ft), lambda e, f: (e, 0, f)),
                  pl.BlockSpec((1, ft, d), lambda e, f: (e, f, 0))],
        out_specs=pl.BlockSpec((1, slots, d), lambda e, f: (e, 0, 0)),
        scratch_shapes=[pltpu.VMEM((slots, d), F32)] if nf > 1 else [],
        compiler_params=_params("arbitrary", "arbitrary"),
        name="ffn",
    )(xe, ge, w1, w3, w2)


def _combine_kernel(first_ref, rounds_ref, *refs, groups, alpha, slots):
    ng = len(groups)
    rk_refs, x1_refs = refs[0:ng], refs[ng:2 * ng]
    mod_ref, lng_ref, lnb_ref, ye_hbm = refs[2 * ng:2 * ng + 4]
    out_refs = refs[2 * ng + 4:3 * ng + 4]
    buf, acc_scr, sem = refs[3 * ng + 4:]
    d = x1_refs[0].shape[1]
    b = pl.program_id(0)
    nblk = pl.num_programs(0)
    ne, tb = rk_refs[0].shape[1], rk_refs[0].shape[2]
    win = WIN_ROWS
    eye = (lax.broadcasted_iota(jnp.int32, (tb, tb), 0)
           == lax.broadcasted_iota(jnp.int32, (tb, tb), 1))
    eye = jnp.where(eye, 1.0, 0.0).astype(BF16)

    def starts_of(blk, r):
        firsts = [(first_ref[blk * ne + e] // BF16_ROWS) * BF16_ROWS + r * win for e in range(ne)]
        return firsts, [jnp.minimum(f, slots - win) for f in firsts]

    def windows(slot, starts):
        return [pltpu.make_async_copy(ye_hbm.at[e, pl.ds(pl.multiple_of(starts[e], BF16_ROWS), win), :],
                                      buf.at[slot, pl.ds(e * win, win), :], sem.at[slot, e])
                for e in range(ne)]

    def scatter(rk_ref, slot0, slot, firsts, starts):
        hits = _window_hits(rk_ref, starts, slot0, win)
        hits = [h & ((rk_ref[0, e:e + 1, :] + slot0) >= firsts[e]) for e, h in enumerate(hits)]
        hit = jnp.where(jnp.concatenate(hits, axis=0), 1.0, 0.0).astype(BF16)
        hit_t = _dot_nt(eye, hit).astype(BF16)
        return _dot(hit_t, buf[slot])

    cur = b % 2

    @pl.when(b == 0)
    def _prime():
        for cp in windows(0, starts_of(0, 0)[1]):
            cp.start()

    @pl.when(b + 1 < nblk)
    def _prefetch():
        for cp in windows(1 - cur, starts_of(b + 1, 0)[1]):
            cp.start()

    firsts0, starts0 = starts_of(b, 0)
    for cp in windows(cur, starts0):
        cp.wait()

    def group_body(rk_ref, x1_ref, out_ref, g, local):
        acc_scr[...] = scatter(rk_ref, g["slot0"], cur, firsts0, starts0)

        def extra_round(r, carry):
            firsts, starts = starts_of(b, r)
            for cp in windows(2, starts):
                cp.start()
            for cp in windows(2, starts):
                cp.wait()
            acc_scr[...] += scatter(rk_ref, g["slot0"], 2, firsts, starts)
            return carry

        lax.fori_loop(1, rounds_ref[b], extra_round, 0)
        r = g["row0"] + local // g["blocks_per_batch"]
        g2 = mod_ref[pl.ds(r, 1), :][:, 5 * d:6 * d]
        out_ref[...] = _layer_norm(alpha * x1_ref[...] + g2 * acc_scr[...], lng_ref[...], lnb_ref[...])

    blk0 = 0
    for gi, g in enumerate(groups):
        @pl.when((b >= blk0) & (b < blk0 + g["nb"]))
        def _(gi=gi, g=g, blk0=blk0):
            group_body(rk_refs[gi], x1_refs[gi], out_refs[gi], g, b - blk0)
        blk0 += g["nb"]


def _combine(ye, groups, first, rounds, mod, ln_g, ln_b, alpha):
    d = groups[0]["x1"].shape[1]
    ne, tb = groups[0]["rank"].shape[1:]
    slots = ye.shape[1]
    meta, specs_rk, specs_x1 = [], [], []
    blk0 = 0
    for g in groups:
        nb = g["rank"].shape[0]
        meta.append(dict(nb=nb, slot0=g["slot0"], row0=g["row0"], blocks_per_batch=g["blocks_per_batch"]))
        local = lambda b, *_, blk0=blk0, nb=nb: jnp.clip(b - blk0, 0, nb - 1)
        specs_rk.append(pl.BlockSpec((1, ne, tb), lambda b, *_, local=local: (local(b), 0, 0)))
        specs_x1.append(pl.BlockSpec((tb, d), lambda b, *_, local=local: (local(b), 0)))
        blk0 += nb
    const = lambda shape: pl.BlockSpec(shape, lambda b, *_: (0,) * len(shape))
    return pl.pallas_call(
        functools.partial(_combine_kernel, groups=meta, alpha=alpha, slots=slots),
        out_shape=[jax.ShapeDtypeStruct(g["x1"].shape, F32) for g in groups],
        grid_spec=pltpu.PrefetchScalarGridSpec(
            num_scalar_prefetch=2,
            grid=(blk0,),
            in_specs=specs_rk + specs_x1 + [const(mod.shape), const((1, d)), const((1, d)),
                                            pl.BlockSpec(memory_space=pl.ANY)],
            out_specs=specs_x1,
            scratch_shapes=[pltpu.VMEM((3, ne * WIN_ROWS, d), ye.dtype), pltpu.VMEM((tb, d), F32),
                            pltpu.SemaphoreType.DMA((3, ne))]),
        compiler_params=_params("arbitrary"),
        name="combine",
    )(first, rounds, *[g["rank"] for g in groups], *[g["x1"] for g in groups], mod, ln_g, ln_b, ye)


def _prep_weights(w_in, q_norm, w_uq, kv_norm, w_ukv, w_o_mla, hgrn_norm, w_o_hg, w_out, ln1_g, ln1_b,
                  w_router):
    d = w_in.shape[0]
    q_lora, kv_lora = q_norm.shape[0], kv_norm.shape[0]
    hw = HG_HEADS * HG_DK
    hh, hp = MLA_HEADS, MLA_HEADS * LANES
    o_kv, o_pe = q_lora, q_lora + kv_lora
    o_h = o_pe + MLA_ROPE
    o_g = o_h + 5 * hw
    assert w_in.shape[1] == o_g + 2 * d
    qk = MLA_NOPE + MLA_ROPE
    kvw = MLA_NOPE + MLA_V
    b16 = lambda a: a.astype(BF16)
    assert all(o % BF16_ROWS == 0 for o in (o_kv, o_pe, o_h, o_g))
    win = b16(w_in.T)
    wuq = jnp.pad(w_uq.reshape(q_lora, hh, qk), ((0, 0), (0, 0), (0, LANES - qk))).reshape(q_lora, hp)
    ukv = w_ukv.reshape(kv_lora, hh, kvw)
    wk = jnp.pad(ukv[:, :, :MLA_NOPE], ((0, 0), (0, 0), (0, LANES - MLA_NOPE))).reshape(kv_lora, hp)
    wv = ukv[:, :, MLA_NOPE:].reshape(kv_lora, hh * MLA_V)
    return dict(
        win=win, qn=q_norm.reshape(1, -1), wuq=b16(wuq), kvn=kv_norm.reshape(1, -1), wk=b16(wk), wv=b16(wv),
        hgn=hgrn_norm.reshape(1, -1), womla=b16(w_o_mla), wohg=b16(w_o_hg), wout=b16(w_out),
        ln1g=ln1_g.reshape(1, -1), ln1b=ln1_b.reshape(1, -1), wr=b16(w_router.T))


def _rope_tables(seq):
    n_freq = MLA_ROPE // 4
    inv = ROPE_BASE ** (-np.arange(n_freq, dtype=np.float64) / n_freq)
    t = np.arange(seq)
    ang = np.concatenate([(t // GRID_W)[:, None] * inv, (t % GRID_W)[:, None] * inv], axis=-1)
    cos = np.repeat(np.cos(ang), 2, axis=1)
    sin = np.repeat(np.sin(ang), 2, axis=1) * np.tile([-1.0, 1.0], MLA_ROPE // 2)
    ck = np.pad(cos, ((0, 0), (0, LANES - MLA_ROPE)), constant_values=1.0)
    sk = np.pad(sin, ((0, 0), (0, LANES - MLA_ROPE)))
    cq = np.pad(cos, ((0, 0), (MLA_NOPE, LANES - MLA_NOPE - MLA_ROPE)), constant_values=1.0)
    sq = np.pad(sin, ((0, 0), (MLA_NOPE, LANES - MLA_NOPE - MLA_ROPE)))
    return tuple(jnp.asarray(a, F32) for a in (cq, sq, ck, sk))


def _window_sched(cnt, slot0):
    first = slot0 + cnt[:, :-1]
    end = slot0 + cnt[:, 1:]
    flat = lambda a: a.T.reshape(-1).astype(jnp.int32)
    rounds = jnp.max((end - (first // BF16_ROWS) * BF16_ROWS + WIN_ROWS - 1) // WIN_ROWS, axis=0)
    return flat(first), flat(end), jnp.maximum(rounds, 1).astype(jnp.int32)


def kernel(x_prompt, x_sample, c, cache_ckv, cache_kpe, state_hgrn, c_ctx, w_ada, b_ada, w_in, mla_q_norm, mla_w_uq, mla_kv_norm, mla_w_ukv, mla_w_o, hgrn_gamma, hgrn_norm, hgrn_w_o, w_out, ln1_g, ln1_b, moe_w_router, moe_w1, moe_w3, moe_w2, ln2_g, ln2_b):
    depth = w_ada.shape[0]
    assert depth == 1, "single trunk layer"
    bp, tp, d = x_prompt.shape
    bs, tsq, _ = x_sample.shape
    ne = moe_w_router.shape[-1]
    alpha = (2 * depth) ** 0.25
    past = cache_ckv.shape[2]
    assert tp % TOK_BLOCK == 0 and tsq % TOK_BLOCK == 0 and past % TOK_BLOCK == 0 and tsq % GRID_W == 0

    wts = _prep_weights(w_in[0], mla_q_norm[0], mla_w_uq[0], mla_kv_norm[0], mla_w_ukv[0], mla_w_o[0],
                        hgrn_norm[0], hgrn_w_o[0], w_out[0], ln1_g[0], ln1_b[0], moe_w_router[0])
    cond_rows = -(-(1 + bs) // SUBLANES) * SUBLANES
    cond = jnp.concatenate([c_ctx[None], c, jnp.zeros((cond_rows - 1 - bs, d), F32)], axis=0)
    mod = _adaln(cond, w_ada[0], b_ada[0])

    xs = [x_prompt.reshape(bp * tp, d), x_sample.reshape(bs * tsq, d)]
    dims = [(bp, tp), (bs, tsq)]
    rows = [(0, False), (1, True)]
    ropes = [None, _rope_tables(tsq)]
    kpe_c = jnp.pad(cache_kpe[:, 0].reshape(bs * past, MLA_ROPE), ((0, 0), (0, LANES - MLA_ROPE)))
    caches = [None, _kvup(cache_ckv[:, 0].reshape(bs * past, -1), kpe_c, wts)]
    inits = [None, state_hgrn[:, 0]]

    x1s, h2s, affs, extras = [], [], [], []
    for gi in range(2):
        (bt, sq), (row0, per_batch) = dims[gi], rows[gi]
        if sq == TOK_BLOCK and caches[gi] is None and ropes[gi] is None and not per_batch:
            x1, h2, aff, ckv, kpe, s_fin = _mixer(xs[gi], bt, sq, mod, hgrn_gamma, wts, alpha, ne, row0)
        else:
            q, k, v, ckv, kpe, hgx = _inproj(xs[gi], bt, sq, mod, hgrn_gamma, wts, row0, per_batch, ropes[gi])
            o_f, o_b, s_fin = _hgrn(hgx, bt, sq, inits[gi])
            x1, h2, aff = _postmix(xs[gi], bt, sq, mod, q, k, v, caches[gi], o_f, o_b, hgx, wts, alpha, ne,
                                   row0, per_batch)
        x1s.append(x1)
        h2s.append(h2)
        affs.append(aff)
        extras.append((ckv, kpe, s_fin))

    caps = [EC_FACTOR * x.shape[0] // ne for x in xs]
    slots = sum(caps)
    assert all(cp % BF16_ROWS == 0 for cp in caps) and slots >= WIN_ROWS
    groups, scheds = [], []
    slot0 = 0
    routed = _route(affs, caps)
    for gi in range(2):
        rank, cnt = routed[gi]
        groups.append(dict(h2=h2s[gi], rank=rank, aff=affs[gi], slot0=slot0, x1=x1s[gi], row0=rows[gi][0],
                           blocks_per_batch=dims[gi][1] // TOK_BLOCK if rows[gi][1] else 1 << 30))
        scheds.append(_window_sched(cnt, slot0))
        slot0 += caps[gi]
    first, end, rounds = [jnp.concatenate([s[k] for s in scheds]) for k in range(3)]
    xe, ge = _compact(groups, first, end, rounds, slots)
    ye = _ffn(xe, ge, moe_w1[0], moe_w3[0], moe_w2[0], slots, ft=moe_w1.shape[-1])
    outs = _combine(ye, groups, first, rounds, mod, ln2_g[0].reshape(1, -1), ln2_b[0].reshape(1, -1), alpha)

    ckv_p, kpe_p, st_p = extras[0]
    y_prompt = outs[0].reshape(bp, tp, d)
    y_sample = outs[1].reshape(bs, tsq, d)
    new_ckv = ckv_p.reshape(bp, 1, tp, -1)
    new_kpe = kpe_p[:, :MLA_ROPE].reshape(bp, 1, tp, MLA_ROPE)
    new_state = st_p.reshape(bp, 1, 2, HG_HEADS, HG_DK, HG_DV)
    return (y_prompt, y_sample, new_ckv, new_kpe, new_state)
```

```python
import functools

import jax
import jax.numpy as jnp
import numpy as np
from jax import lax
from jax.experimental import pallas as pl
from jax.experimental.pallas import tpu as pltpu

F32 = jnp.float32
BF16 = jnp.bfloat16

MLA_HEADS = 8
MLA_NOPE = 64
MLA_ROPE = 32
MLA_V = 64
HG_HEADS = 4
HG_DK = 128
HG_DV = 128
HG_CHUNK = 32
GRID_W = 64
ROPE_BASE = 10000.0
EC_FACTOR = 2
EPS = 1e-6

LANES = 128
SUBLANES = 8
BF16_ROWS = 16
VMEM_LIMIT = 56 * 1024 * 1024

TOK_BLOCK = 256
WIN_ROWS = 64
UNSELECTED = -(1 << 30)

NT_DIMS = (((1,), (1,)), ((), ()))


def _dot(a, b):
    return jnp.dot(a, b, preferred_element_type=F32)


def _dot_nt(a, b):
    return lax.dot_general(a, b, NT_DIMS, preferred_element_type=F32)


def _silu(x):
    return x * jax.nn.sigmoid(x)


def _params(*sem):
    return pltpu.CompilerParams(dimension_semantics=sem, vmem_limit_bytes=VMEM_LIMIT)


def _const_spec(shape):
    zeros = (0,) * len(shape)
    return pl.BlockSpec(shape, lambda *_: zeros, pipeline_mode=pl.Buffered(1))


def _adaln_kernel(c_ref, w_ref, b_ref, o_ref):
    s = _silu(c_ref[...]).astype(BF16)
    o_ref[...] = _dot(s, w_ref[...].astype(BF16)) + b_ref[...]


def _adaln(cond, w_ada, b_ada):
    rows, d = cond.shape
    n = w_ada.shape[1]
    tn = n // 4
    return pl.pallas_call(
        _adaln_kernel,
        out_shape=jax.ShapeDtypeStruct((rows, n), F32),
        grid=(n // tn,),
        in_specs=[_const_spec((rows, d)),
                  pl.BlockSpec((d, tn), lambda j: (0, j)),
                  pl.BlockSpec((1, tn), lambda j: (0, j))],
        out_specs=pl.BlockSpec((rows, tn), lambda j: (0, j)),
        compiler_params=_params("arbitrary"),
        name="adaln",
    )(cond, w_ada, b_ada.reshape(1, n))


def _rms(x, g):
    return x * lax.rsqrt(jnp.mean(x * x, axis=-1, keepdims=True) + EPS) * g


def _rope(x, c, s):
    w = x.shape[-1]
    lane = lax.broadcasted_iota(jnp.int32, x.shape, 1)
    nxt = pltpu.roll(x, w - 1, 1)
    prv = pltpu.roll(x, 1, 1)
    return x * c + jnp.where(lane % 2 == 0, nxt, prv) * s


N_INPROJ_WEIGHTS = 6


def _mod_row(mod_ref, row0, per_batch):
    r = row0 + pl.program_id(0) if per_batch else row0
    return mod_ref[pl.ds(r, 1), :]


def _modulated(x_ref, m):
    d = x_ref.shape[1]
    return (x_ref[...] * (1.0 + m[:, d:2 * d]) + m[:, 0:d]).astype(BF16)


def _keys(k_nope, kpe):
    shared = pltpu.roll(kpe, MLA_NOPE, 1)
    return (k_nope + jnp.concatenate([shared] * MLA_HEADS, axis=1)).astype(BF16)


def _inproj_kernel(*refs, row0, per_batch, rope):
    x_ref, mod_ref = refs[:2]
    _inproj_body(x_ref, _mod_row(mod_ref, row0, per_batch), *refs[2:], rope=rope)


def _inproj_body(x_ref, m, *refs, rope):
    gam_ref, win_ref, qn_ref, wuq_ref, kvn_ref, wk_ref, wv_ref = refs[:1 + N_INPROJ_WEIGHTS]
    refs = refs[1 + N_INPROJ_WEIGHTS:]
    if rope:
        cq_ref, sq_ref, ck_ref, sk_ref = refs[:4]
        refs = refs[4:]
    q_o, k_o, v_o, ckv_o, kpe_o, hgx_o = refs
    h = _modulated(x_ref, m)
    hw = HG_HEADS * HG_DK
    o_kv = qn_ref.shape[1]
    o_pe = o_kv + kvn_ref.shape[1]
    o_h = o_pe + MLA_ROPE

    cq = _rms(_dot_nt(h, win_ref[0:o_kv, :]), qn_ref[...])
    q = _dot(cq.astype(BF16), wuq_ref[...])
    if rope:
        q = _rope(q, jnp.concatenate([cq_ref[...]] * MLA_HEADS, axis=1),
                  jnp.concatenate([sq_ref[...]] * MLA_HEADS, axis=1))
    q_o[...] = q.astype(BF16)

    ckv = _rms(_dot_nt(h, win_ref[o_kv:o_pe, :]), kvn_ref[...])
    ckv_o[...] = ckv
    kpe = _dot_nt(h, win_ref[o_pe:o_h, :])
    kpe = jnp.concatenate([kpe, jnp.zeros((kpe.shape[0], LANES - MLA_ROPE), F32)], axis=1)
    if rope:
        kpe = _rope(kpe, ck_ref[...], sk_ref[...])
    kpe_o[...] = kpe
    cb = ckv.astype(BF16)
    k_o[...] = _keys(_dot(cb, wk_ref[...]), kpe)
    v_o[...] = _dot(cb, wv_ref[...]).astype(BF16)

    z = _dot_nt(h, win_ref[o_h:o_h + 5 * hw, :])
    hgx_o[:, 0:hw] = _silu(z[:, 0:hw])
    for dr in range(2):
        g0, g1 = gam_ref[dr, 0:1, :], gam_ref[dr, 1:2, :]
        gmax = jnp.maximum(g0, g1)
        e0, e1 = jnp.exp(g0 - gmax), jnp.exp(g1 - gmax)
        lb = e0 / (e0 + e1)
        f = lb + (1.0 - lb) * jax.nn.sigmoid(z[:, (1 + dr) * hw:(2 + dr) * hw])
        hgx_o[:, (1 + 2 * dr) * hw:(2 + 2 * dr) * hw] = jnp.log(f)
        hgx_o[:, (2 + 2 * dr) * hw:(3 + 2 * dr) * hw] = 1.0 - f
    hgx_o[:, 5 * hw:6 * hw] = z[:, 3 * hw:4 * hw]
    hgx_o[:, 6 * hw:7 * hw] = z[:, 4 * hw:5 * hw]


def _inproj(x2d, batch, seq, mod, gamma, wts, row0, per_batch, rope_tabs):
    n, d = x2d.shape
    tm = MIXER_SEQS * TOK_BLOCK if seq % (MIXER_SEQS * TOK_BLOCK) == 0 else TOK_BLOCK
    nblk = seq // tm
    rope = rope_tabs is not None
    hp = MLA_HEADS * LANES
    hw = HG_HEADS * HG_DK
    tok = lambda b, i: (b * nblk + i, 0)
    pos = lambda b, i: (i, 0)
    weights = [wts[k] for k in INPROJ_KEYS]
    ins = [x2d, mod, gamma] + weights
    in_specs = ([pl.BlockSpec((tm, d), tok), _const_spec(mod.shape), _const_spec(gamma.shape)]
                + [_const_spec(w.shape) for w in weights])
    if rope:
        ins += list(rope_tabs)
        in_specs += [pl.BlockSpec((tm, t.shape[1]), pos) for t in rope_tabs]
    widths = [(hp, BF16), (hp, BF16), (MLA_HEADS * MLA_V, BF16), (wts["kvn"].shape[1], F32), (LANES, F32),
              (7 * hw, F32)]
    return pl.pallas_call(
        functools.partial(_inproj_kernel, row0=row0, per_batch=per_batch, rope=rope),
        out_shape=[jax.ShapeDtypeStruct((n, w), dt) for w, dt in widths],
        grid=(batch, nblk),
        in_specs=in_specs,
        out_specs=[pl.BlockSpec((tm, w), tok) for w, _ in widths],
        compiler_params=_params("arbitrary", "arbitrary"),
        name="inproj",
    )(*ins)


def _kvup_kernel(ckv_ref, kpe_ref, wk_ref, wv_ref, k_o, v_o):
    cb = ckv_ref[...].astype(BF16)
    k_o[...] = _keys(_dot(cb, wk_ref[...]), kpe_ref[...])
    v_o[...] = _dot(cb, wv_ref[...]).astype(BF16)


def _kvup(ckv2d, kpe2d, wts):
    n = ckv2d.shape[0]
    tm = TOK_BLOCK
    widths = [MLA_HEADS * LANES, MLA_HEADS * MLA_V]
    row = lambda i: (i, 0)
    ws = [wts["wk"], wts["wv"]]
    return pl.pallas_call(
        _kvup_kernel,
        out_shape=[jax.ShapeDtypeStruct((n, w), BF16) for w in widths],
        grid=(n // tm,),
        in_specs=[pl.BlockSpec((tm, ckv2d.shape[1]), row), pl.BlockSpec((tm, LANES), row)]
                 + [_const_spec(w.shape) for w in ws],
        out_specs=[pl.BlockSpec((tm, w), row) for w in widths],
        compiler_params=_params("arbitrary"),
        name="kvup",
    )(ckv2d, kpe2d, *ws)


ATTN_SCALE = (MLA_NOPE + MLA_ROPE) ** -0.5


def _attn_body(q_ref, k_ref, v_ref, kc_ref, vc_ref, o_ref):
    cached = kc_ref is not None
    scale = ATTN_SCALE
    per_slab = LANES // MLA_V
    own = lax.broadcasted_iota(jnp.int32, (q_ref.shape[0], LANES), 1) // MLA_V
    for slab in range(MLA_HEADS // per_slab):
        vsl = slice(slab * LANES, (slab + 1) * LANES)
        out = None
        for sub in range(per_slab):
            hd = slab * per_slab + sub
            sl = slice(hd * LANES, (hd + 1) * LANES)
            q = q_ref[:, sl]
            s = _dot_nt(q, k_ref[:, sl]) * scale
            mx = jnp.max(s, axis=-1, keepdims=True)
            if cached:
                s2 = _dot_nt(q, kc_ref[:, sl]) * scale
                mx = jnp.maximum(mx, jnp.max(s2, axis=-1, keepdims=True))
            e = jnp.exp(s - mx)
            den = jnp.sum(e, axis=-1, keepdims=True)
            o = _dot(e.astype(BF16), v_ref[:, vsl])
            if cached:
                e2 = jnp.exp(s2 - mx)
                den = den + jnp.sum(e2, axis=-1, keepdims=True)
                o = o + _dot(e2.astype(BF16), vc_ref[:, vsl])
            o = o / den
            out = o if out is None else jnp.where(own == sub, o, out)
        o_ref[:, vsl] = out.astype(o_ref.dtype)


def _chunk_scan(x, reverse):
    tm = x.shape[0]
    rin = lax.broadcasted_iota(jnp.int32, x.shape, 0) % HG_CHUNK
    step = 1
    while step < HG_CHUNK:
        if reverse:
            x = x + jnp.where(rin < HG_CHUNK - step, pltpu.roll(x, tm - step, 0), 0.0)
        else:
            x = x + jnp.where(rin >= step, pltpu.roll(x, step, 0), 0.0)
        step *= 2
    return x


def _hgrn_kernel(*refs, has_init):
    fwd, bwd = refs[0:4], refs[4:8]
    refs = refs[8:]
    s0_ref = None
    if has_init:
        s0_ref = refs[0]
        refs = refs[1:]
    of_ref, ob_ref, sfin_ref, st_scr = refs
    i = pl.program_id(1)
    _hgrn_body([(fwd, bwd, of_ref, ob_ref, sfin_ref)], s0_ref, st_scr, i == 0, i == pl.num_programs(1) - 1)


def _hgrn_body(jobs, s0_ref, st_scr, first, last):
    tm = jobs[0][0][0].shape[0]
    c = HG_CHUNK
    nch = tm // c
    dk, dv = HG_DK, HG_DV
    hw = HG_HEADS * dk

    def initial(dr, hd):
        return s0_ref[0, dr, hd].T if s0_ref is not None else jnp.zeros((dv, dk), F32)

    if st_scr is not None:
        @pl.when(first)
        def _init():
            for dr in range(2):
                for hd in range(HG_HEADS):
                    st_scr[dr, hd] = initial(dr, hd)

    npair = nch // 2
    pair = 2 * c
    row = lax.broadcasted_iota(jnp.int32, (tm, tm), 0)
    col = lax.broadcasted_iota(jnp.int32, (tm, tm), 1)
    same = (row // c) == (col // c)
    same_pair = (row // pair) == (col // pair)
    bd = (lax.broadcasted_iota(jnp.int32, (tm, npair * dk), 0) // pair
          == lax.broadcasted_iota(jnp.int32, (tm, npair * dk), 1) // dk)
    chunk_odd = (lax.broadcasted_iota(jnp.int32, (tm, hw), 0) // c) % 2 == 1

    for dr in range(2):
        tri = same & ((col <= row) if dr == 0 else (col >= row))
        cross = same_pair & (((row // c) > (col // c)) if dr == 0 else ((row // c) < (col // c)))
        second = chunk_odd if dr == 0 else ~chunk_odd
        order = range(npair) if dr == 0 else range(npair - 1, -1, -1)

        def decayed(job):
            hq_ref, lf_ref, kk_ref, vv_ref = job[dr]
            bcum = _chunk_scan(lf_ref[...], reverse=dr == 1)
            closing = c - 1 if dr == 0 else 0
            btot3 = bcum.reshape(nch, c, hw)[:, closing:closing + 1, :]
            btot = jnp.broadcast_to(btot3, (nch, c, hw)).reshape(tm, hw)
            bpart = jnp.where(chunk_odd, pltpu.roll(btot, c, 0), pltpu.roll(btot, tm - c, 0))
            epart = jnp.exp(bpart)
            kk = kk_ref[...]
            qd = hq_ref[...] * jnp.exp(bcum)
            kd = kk * jnp.exp(-bcum)
            ke = kk * jnp.exp(btot - bcum)
            qd2 = jnp.where(second, qd * epart, qd)
            ke2 = jnp.where(second, ke, ke * epart)
            return qd, kd, ke, qd2, ke2, btot + bpart, vv_ref[...]

        def head(job, hd, qd, kd, ke, qd2, ke2, bpair, vv):
            o_ref, sfin_ref = job[2 + dr], job[4]
            sl = slice(hd * dk, (hd + 1) * dk)
            qd_h = qd[:, sl].astype(BF16)
            v_h = vv[:, hd * dv:(hd + 1) * dv]
            a = jnp.where(tri, _dot_nt(qd_h, kd[:, sl].astype(BF16)),
                          jnp.where(cross, _dot_nt(qd_h, ke[:, sl].astype(BF16)), 0.0))
            o_intra = _dot(a.astype(BF16), v_h.astype(BF16))
            kebd = jnp.where(bd, jnp.concatenate([ke2[:, sl]] * npair, axis=1), 0.0).astype(BF16)
            qbd = jnp.where(bd, jnp.concatenate([qd2[:, sl]] * npair, axis=1), 0.0).astype(BF16)
            ut = _dot(v_h.T.astype(BF16), kebd)
            st = st_scr[dr, hd] if st_scr is not None else initial(dr, hd)
            prev = [None] * npair
            for p in order:
                prev[p] = st
                st = st * jnp.exp(bpair[p * pair:p * pair + 1, sl]) + ut[:, p * dk:(p + 1) * dk]
            if st_scr is not None:
                st_scr[dr, hd] = st
            o_inter = _dot_nt(qbd, jnp.concatenate(prev, axis=1).astype(BF16))
            o_ref[:, hd * dv:(hd + 1) * dv] = o_intra + o_inter

            if last is True:
                sfin_ref[0, dr, hd] = st.T
            else:
                @pl.when(last)
                def _final():
                    sfin_ref[0, dr, hd] = st.T

        prepared = [decayed(job) for job in jobs]
        for hd in range(HG_HEADS):
            for job, arrays in zip(jobs, prepared):
                head(job, hd, *arrays)


def _hgrn(hgx, batch, seq, s0=None):
    n = hgx.shape[0]
    tm = TOK_BLOCK
    nblk = seq // tm
    hw = HG_HEADS * HG_DK

    def spec(lane_blk, rev):
        if rev:
            return pl.BlockSpec((tm, hw), lambda b, i: (b * nblk + nblk - 1 - i, lane_blk))
        return pl.BlockSpec((tm, hw), lambda b, i: (b * nblk + i, lane_blk))

    in_specs = [spec(0, False), spec(1, False), spec(2, False), spec(5, False),
                spec(0, True), spec(3, True), spec(4, True), spec(5, True)]
    ins = [hgx] * 8
    st_shape = (1, 2, HG_HEADS, HG_DK, HG_DV)
    st_spec = pl.BlockSpec(st_shape, lambda b, i: (b, 0, 0, 0, 0))
    if s0 is not None:
        ins.append(s0)
        in_specs.append(st_spec)
    return pl.pallas_call(
        functools.partial(_hgrn_kernel, has_init=s0 is not None),
        out_shape=[jax.ShapeDtypeStruct((n, hw), F32), jax.ShapeDtypeStruct((n, hw), F32),
                   jax.ShapeDtypeStruct((batch,) + st_shape[1:], F32)],
        grid=(batch, nblk),
        in_specs=in_specs,
        out_specs=[spec(0, False), spec(0, True), st_spec],
        scratch_shapes=[pltpu.VMEM((2, HG_HEADS, HG_DV, HG_DK), F32)],
        compiler_params=_params("arbitrary", "arbitrary"),
        name="hgrn",
    )(*ins)


def _layer_norm(x, g, b):
    xc = x - jnp.mean(x, axis=-1, keepdims=True)
    var = jnp.mean(xc * xc, axis=-1, keepdims=True)
    return xc * lax.rsqrt(var + EPS) * g + b


N_POSTMIX_WEIGHTS = 8
INPROJ_KEYS = ("win", "qn", "wuq", "kvn", "wk", "wv")
POSTMIX_KEYS = ("hgn", "womla", "wohg", "wout", "ln1g", "ln1b", "wr")
MIXER_SEQS = 2


def _postmix_kernel(x_ref, mod_ref, *refs, alpha, row0, per_batch, cached):
    n_attn = 5 if cached else 3
    q_ref, k_ref, v_ref = refs[:3]
    kc_ref, vc_ref = refs[3:5] if cached else (None, None)
    of_ref, ob_ref, zg_ref = refs[n_attn:n_attn + 3]
    om_s = refs[-1]
    _attn_body(q_ref, k_ref, v_ref, kc_ref, vc_ref, om_s)
    _postmix_body(x_ref, _mod_row(mod_ref, row0, per_batch), of_ref, ob_ref, zg_ref, om_s,
                  *refs[n_attn + 3:-1], alpha=alpha)


def _postmix_body(x_ref, m, of_ref, ob_ref, zg_ref, om_ref, wg_ref, hgn_ref, womla_ref,
                  wohg_ref, wout_ref, lng_ref, lnb_ref, wr_ref, x1_o, h2_o, aff_o, *, alpha):
    d = x_ref.shape[1]
    tb = aff_o.shape[2]
    g1, sh2, sc2 = m[:, 2 * d:3 * d], m[:, 3 * d:4 * d], m[:, 4 * d:5 * d]
    o = of_ref[...] + ob_ref[...]
    zg = zg_ref[...]
    parts = []
    for hd in range(HG_HEADS):
        sl = slice(hd * HG_DV, (hd + 1) * HG_DV)
        parts.append(_rms(o[:, sl], hgn_ref[...]) * _silu(zg[:, sl]))
    ohg = jnp.concatenate(parts, axis=1).astype(BF16)
    gates = _dot_nt(_modulated(x_ref, m), wg_ref[wg_ref.shape[0] - 2 * d:, :])
    merged = (jax.nn.sigmoid(gates[:, 0:d]) * _dot(om_ref[...], womla_ref[...])
              + jax.nn.sigmoid(gates[:, d:2 * d]) * _dot(ohg, wohg_ref[...]))
    mix = _dot(merged.astype(BF16), wout_ref[...])
    x1 = _layer_norm(alpha * x_ref[...] + g1 * mix, lng_ref[...], lnb_ref[...])
    x1_o[...] = x1
    h2 = (x1 * (1.0 + sc2) + sh2).astype(BF16)
    h2_o[...] = h2
    logits = _dot_nt(wr_ref[...], h2)
    e = jnp.exp(logits - jnp.max(logits, axis=0, keepdims=True))
    aff = e / jnp.sum(e, axis=0, keepdims=True)
    for blk in range(aff_o.shape[0]):
        aff_o[blk] = aff[:, blk * tb:(blk + 1) * tb]


def _postmix(x2d, batch, seq, mod, q, k, v, cache, o_f, o_b, hgx, wts, alpha, n_experts, row0, per_batch):
    n, d = x2d.shape
    tm = TOK_BLOCK
    nblk = seq // tm
    hw = HG_HEADS * HG_DV
    hp, hv = q.shape[1], v.shape[1]
    tok = lambda b, i: (b * nblk + i, 0)
    per_seq = lambda b, i: (b, 0)
    weights = [wts[k] for k in POSTMIX_KEYS]
    attn_ins = [q, k, v]
    attn_specs = [pl.BlockSpec((tm, hp), tok), pl.BlockSpec((seq, hp), per_seq), pl.BlockSpec((seq, hv), per_seq)]
    if cache is not None:
        past = cache[0].shape[0] // batch
        attn_ins += list(cache)
        attn_specs += [pl.BlockSpec((past, hp), per_seq), pl.BlockSpec((past, hv), per_seq)]
    return pl.pallas_call(
        functools.partial(_postmix_kernel, alpha=alpha, row0=row0, per_batch=per_batch,
                          cached=cache is not None),
        out_shape=[jax.ShapeDtypeStruct((n, d), F32), jax.ShapeDtypeStruct((n, d), BF16),
                   jax.ShapeDtypeStruct((n // tm, n_experts, tm), F32)],
        grid=(batch, nblk),
        in_specs=[pl.BlockSpec((tm, d), tok), _const_spec(mod.shape)] + attn_specs
                 + [pl.BlockSpec((tm, hw), tok), pl.BlockSpec((tm, hw), tok),
                    pl.BlockSpec((tm, hw), lambda b, i: (b * nblk + i, 6)), _const_spec(wts["win"].shape)]
                 + [_const_spec(w.shape) for w in weights],
        out_specs=[pl.BlockSpec((tm, d), tok), pl.BlockSpec((tm, d), tok),
                   pl.BlockSpec((1, n_experts, tm), lambda b, i: (b * nblk + i, 0, 0))],
        scratch_shapes=[pltpu.VMEM((tm, hv), BF16)],
        compiler_params=_params("arbitrary", "arbitrary"),
        name="postmix",
    )(x2d, mod, *attn_ins, o_f, o_b, hgx, wts["win"], *weights)


def _mixer_kernel(x_ref, mod_ref, *refs, alpha, row0, seq):
    nw = 1 + N_INPROJ_WEIGHTS
    in_w, refs = refs[:nw], refs[nw:]
    pm_w, refs = refs[:N_POSTMIX_WEIGHTS - 1], refs[N_POSTMIX_WEIGHTS - 1:]
    pm_w = (in_w[1],) + tuple(pm_w)
    x1_o, h2_o, aff_o, ckv_o, kpe_o, sfin_o, q_s, k_s, v_s, hgx_s, om_s, of_s, ob_s = refs
    m = _mod_row(mod_ref, row0, False)
    _inproj_body(x_ref, m, *in_w, q_s, k_s, v_s, ckv_o, kpe_o, hgx_s, rope=False)
    hw = HG_HEADS * HG_DK
    jobs = []
    for s in range(x_ref.shape[0] // seq):
        rows = slice(s * seq, (s + 1) * seq)
        _attn_body(q_s.at[rows], k_s.at[rows], v_s.at[rows], None, None, om_s.at[rows])
        lane = lambda j, rows=rows: hgx_s.at[rows, j * hw:(j + 1) * hw]
        jobs.append(((lane(0), lane(1), lane(2), lane(5)), (lane(0), lane(3), lane(4), lane(5)),
                     of_s.at[rows], ob_s.at[rows], sfin_o.at[s:s + 1]))
    _hgrn_body(jobs, None, None, True, True)
    _postmix_body(x_ref, m, of_s, ob_s, hgx_s.at[:, 6 * hw:7 * hw], om_s, *pm_w, x1_o, h2_o, aff_o,
                  alpha=alpha)


def _mixer(x2d, batch, seq, mod, gamma, wts, alpha, n_experts, row0):
    n, d = x2d.shape
    assert seq == TOK_BLOCK
    ns = MIXER_SEQS if batch % MIXER_SEQS == 0 else 1
    tm = ns * seq
    hp = MLA_HEADS * LANES
    hv = MLA_HEADS * MLA_V
    hw = HG_HEADS * HG_DK
    kvl = wts["kvn"].shape[1]
    weights = [wts[k] for k in INPROJ_KEYS + POSTMIX_KEYS]
    tok = lambda b: (b, 0)
    st_shape = (ns, 2, HG_HEADS, HG_DK, HG_DV)
    return pl.pallas_call(
        functools.partial(_mixer_kernel, alpha=alpha, row0=row0, seq=seq),
        out_shape=[jax.ShapeDtypeStruct((n, d), F32), jax.ShapeDtypeStruct((n, d), BF16),
                   jax.ShapeDtypeStruct((n // seq, n_experts, seq), F32),
                   jax.ShapeDtypeStruct((n, kvl), F32), jax.ShapeDtypeStruct((n, LANES), F32),
                   jax.ShapeDtypeStruct((batch,) + st_shape[1:], F32)],
        grid=(batch // ns,),
        in_specs=[pl.BlockSpec((tm, d), tok), _const_spec(mod.shape), _const_spec(gamma.shape)]
                 + [_const_spec(w.shape) for w in weights],
        out_specs=[pl.BlockSpec((tm, d), tok), pl.BlockSpec((tm, d), tok),
                   pl.BlockSpec((ns, n_experts, seq), lambda b: (b, 0, 0)),
                   pl.BlockSpec((tm, kvl), tok), pl.BlockSpec((tm, LANES), tok),
                   pl.BlockSpec(st_shape, lambda b: (b, 0, 0, 0, 0))],
        scratch_shapes=[pltpu.VMEM((tm, hp), BF16), pltpu.VMEM((tm, hp), BF16), pltpu.VMEM((tm, hv), BF16),
                        pltpu.VMEM((tm, 7 * hw), F32), pltpu.VMEM((tm, hv), BF16),
                        pltpu.VMEM((tm, hw), F32), pltpu.VMEM((tm, hw), F32)],
        compiler_params=_params("arbitrary"),
        name="mixer",
    )(x2d, mod, gamma, *weights)


def _route_kernel(*refs, caps):
    ng = len(caps)
    for aff_ref, rank_o, cnt_o, cap in zip(refs[0:ng], refs[ng:2 * ng], refs[2 * ng:3 * ng], caps):
        _route_group(aff_ref, rank_o, cnt_o, cap)


def _route_group(aff_ref, rank_o, cnt_o, cap):
    nb, ne, tb = aff_ref.shape
    key = aff_ref[...]

    def count(mask):
        return jnp.sum(jnp.sum(jnp.where(mask, 1.0, 0.0), axis=0), axis=1, keepdims=True)

    def bit_step(it, bits):
        cand = bits | jnp.left_shift(jnp.int32(1), 30 - it)
        return jnp.where(count(key >= pltpu.bitcast(cand, F32)[None]) >= cap, cand, bits)

    bits = lax.fori_loop(0, 31, bit_step, jnp.zeros((ne, 1), jnp.int32))
    thr = pltpu.bitcast(bits, F32)
    need = cap - count(key > thr[None])
    before = (lax.broadcasted_iota(jnp.int32, (tb, tb), 0)
              < lax.broadcasted_iota(jnp.int32, (tb, tb), 1))
    before = jnp.where(before, 1.0, 0.0).astype(BF16)
    off_eq = jnp.zeros((ne, 1), F32)
    off_sel = jnp.zeros((ne, 1), F32)
    cnt_o[...] = jnp.zeros_like(cnt_o)
    for blk in range(nb):
        key_b = key[blk]
        eq = key_b == thr
        eq_b = jnp.where(eq, 1.0, 0.0)
        eq_rank = _dot(eq_b.astype(BF16), before) + off_eq
        sel = (key_b > thr) | (eq & (eq_rank < need))
        sel_b = jnp.where(sel, 1.0, 0.0)
        rank = _dot(sel_b.astype(BF16), before) + off_sel
        rank_o[blk] = jnp.where(sel, rank.astype(jnp.int32), UNSELECTED)
        cnt_o[:, blk:blk + 1] = off_sel.astype(jnp.int32)
        off_eq = off_eq + jnp.sum(eq_b, axis=1, keepdims=True)
        off_sel = off_sel + jnp.sum(sel_b, axis=1, keepdims=True)
    cnt_o[:, nb:nb + 1] = off_sel.astype(jnp.int32)


def _route(affs, caps):
    ng = len(affs)
    ne = affs[0].shape[1]
    assert all(a.shape[0] + 1 <= LANES for a in affs)
    outs = pl.pallas_call(
        functools.partial(_route_kernel, caps=tuple(caps)),
        out_shape=[jax.ShapeDtypeStruct(a.shape, jnp.int32) for a in affs]
                  + [jax.ShapeDtypeStruct((ne, LANES), jnp.int32)] * ng,
        in_specs=[pl.BlockSpec(memory_space=pltpu.VMEM)] * ng,
        out_specs=[pl.BlockSpec(memory_space=pltpu.VMEM)] * (2 * ng),
        compiler_params=pltpu.CompilerParams(vmem_limit_bytes=VMEM_LIMIT),
        name="route",
    )(*affs)
    return [(outs[gi], outs[ng + gi][:, :affs[gi].shape[0] + 1]) for gi in range(ng)]


def _window_hits(rk_ref, firsts, slot0, win):
    ne, tb = rk_ref.shape[1], rk_ref.shape[2]
    win_iota = lax.broadcasted_iota(jnp.int32, (win, tb), 0)
    return [(rk_ref[0, e:e + 1, :] + (slot0 - firsts[e])) == win_iota for e in range(ne)]


def _compact_kernel(first_ref, end_ref, rounds_ref, *refs, groups, slots):
    ng = len(groups)
    h2_refs, rk_refs, af_refs = refs[0:ng], refs[ng:2 * ng], refs[2 * ng:3 * ng]
    hbms, refs = refs[3 * ng:3 * ng + 2], refs[3 * ng + 2:]
    stages, tails = refs[0:2], refs[2:4]
    sem, issued = refs[4:]
    b = pl.program_id(0)
    ne = rk_refs[0].shape[1]
    win = WIN_ROWS
    sub = BF16_ROWS

    def copies(slot, dsts):
        return [pltpu.make_async_copy(stage.at[slot, pl.ds(e * win, win), :],
                                      hbm.at[e, pl.ds(pl.multiple_of(dsts[e], sub), win), :], sem.at[c, e])
                for c, (stage, hbm) in enumerate(zip(stages, hbms)) for e in range(ne)]

    def wait_previous():
        @pl.when(issued[0] > 0)
        def _():
            for cp in copies(0, [0] * ne):
                cp.wait()

    @pl.when(b == 0)
    def _init():
        issued[0] = 0
        for stage, tail in zip(stages, tails):
            tail[...] = jnp.zeros_like(tail)
            stage[1] = jnp.zeros(stage.shape[1:], stage.dtype)
        pad = copies(1, [slots] * ne)
        for cp in pad:
            cp.start()
        for cp in pad:
            cp.wait()

    def group_body(h2_ref, rk_ref, af_ref, slot0):
        firsts = [first_ref[b * ne + e] for e in range(ne)]
        bases = [(f // sub) * sub for f in firsts]
        ends = [end_ref[b * ne + e] - bases[e] for e in range(ne)]

        def one_round(r, carry):
            dsts = [bases[e] + r * win for e in range(ne)]
            hits = _window_hits(rk_ref, dsts, slot0, win)
            onehot = jnp.where(jnp.concatenate(hits, axis=0), 1.0, 0.0).astype(BF16)
            gate = jnp.concatenate(
                [jnp.sum(jnp.where(hits[e], af_ref[0, e:e + 1, :], 0.0), axis=1, keepdims=True)
                 for e in range(ne)], axis=0)
            slot = issued[0] % 2
            streams = (_dot(onehot, h2_ref[...]).astype(BF16),
                       jnp.broadcast_to(gate, (ne * win, LANES)))
            for val, stage, tail in zip(streams, stages, tails):
                sub_iota = lax.broadcasted_iota(jnp.int32, (sub, val.shape[1]), 0)
                pieces = []
                for e in range(ne):
                    old = tail[e * sub:(e + 1) * sub, :]
                    shared = jnp.where(r == 0, firsts[e] - bases[e], 0)
                    groups_e = [val[e * win + g * sub:e * win + (g + 1) * sub, :] for g in range(win // sub)]
                    groups_e[0] = jnp.where(sub_iota < shared, old, groups_e[0])
                    pieces += groups_e
                    last = (ends[e] // sub) * sub
                    new = old
                    for g, grp in enumerate(groups_e):
                        new = jnp.where((r == last // win) & (last % win == g * sub), grp, new)
                    tail[e * sub:(e + 1) * sub, :] = new
                stage[slot] = jnp.concatenate(pieces, axis=0)
            wait_previous()
            for cp in copies(slot, [jnp.minimum(dst, slots) for dst in dsts]):
                cp.start()
            issued[0] = issued[0] + 1
            return carry

        lax.fori_loop(0, rounds_ref[b], one_round, 0)

    blk0 = 0
    for gi, g in enumerate(groups):
        @pl.when((b >= blk0) & (b < blk0 + g["nb"]))
        def _(gi=gi, g=g):
            group_body(h2_refs[gi], rk_refs[gi], af_refs[gi], g["slot0"])
        blk0 += g["nb"]

    @pl.when(b == pl.num_programs(0) - 1)
    def _drain():
        wait_previous()


def _compact(groups, first, end, rounds, slots):
    d = groups[0]["h2"].shape[1]
    nbs = [g["rank"].shape[0] for g in groups]
    ne, tb = groups[0]["rank"].shape[1:]
    meta, specs_h2, specs_rk = [], [], []
    blk0 = 0
    for g, nb in zip(groups, nbs):
        meta.append(dict(nb=nb, slot0=g["slot0"]))
        local = lambda b, *_, blk0=blk0, nb=nb: jnp.clip(b - blk0, 0, nb - 1)
        specs_h2.append(pl.BlockSpec((tb, d), lambda b, *_, local=local: (local(b), 0)))
        specs_rk.append(pl.BlockSpec((1, ne, tb), lambda b, *_, local=local: (local(b), 0, 0)))
        blk0 += nb
    streams = [(d, BF16), (LANES, F32)]
    return pl.pallas_call(
        functools.partial(_compact_kernel, groups=meta, slots=slots),
        out_shape=[jax.ShapeDtypeStruct((ne, slots + WIN_ROWS, w), dt) for w, dt in streams],
        grid_spec=pltpu.PrefetchScalarGridSpec(
            num_scalar_prefetch=3,
            grid=(sum(nbs),),
            in_specs=specs_h2 + specs_rk + specs_rk,
            out_specs=[pl.BlockSpec(memory_space=pl.ANY)] * 2,
            scratch_shapes=[pltpu.VMEM((2, ne * WIN_ROWS, w), dt) for w, dt in streams]
                           + [pltpu.VMEM((ne * BF16_ROWS, w), dt) for w, dt in streams]
                           + [pltpu.SemaphoreType.DMA((2, ne)), pltpu.SMEM((1,), jnp.int32)]),
        compiler_params=_params("arbitrary"),
        name="compact",
    )(first, end, rounds, *[g["h2"] for g in groups], *[g["rank"] for g in groups],
      *[g["aff"] for g in groups])


def _ffn_kernel(xe_ref, ge_ref, w1_ref, w3_ref, w2_ref, ye_ref, *scratch):
    f = pl.program_id(1)
    x = xe_ref[0]
    hid = _silu(_dot(x, w1_ref[0].astype(BF16))) * _dot(x, w3_ref[0].astype(BF16))
    y = _dot(hid.astype(BF16), w2_ref[0].astype(BF16))
    if not scratch:
        ye_ref[0] = (y * ge_ref[0, :, 0:1]).astype(ye_ref.dtype)
        return
    acc_scr, = scratch
    last = pl.num_programs(1) - 1

    @pl.when(f == 0)
    def _first():
        acc_scr[...] = y

    @pl.when((f > 0) & (f < last))
    def _middle():
        acc_scr[...] += y

    @pl.when(f == last)
    def _last():
        ye_ref[0] = ((acc_scr[...] + y) * ge_ref[0, :, 0:1]).astype(ye_ref.dtype)


def _ffn(xe, ge, w1, w3, w2, slots, ft):
    ne, d, dff = w1.shape
    nf = dff // ft
    return pl.pallas_call(
        _ffn_kernel,
        out_shape=jax.ShapeDtypeStruct((ne, slots, d), BF16),
        grid=(ne, nf),
        in_specs=[pl.BlockSpec((1, slots, d), lambda e, f: (e, 0, 0)),
                  pl.BlockSpec((1, slots, ge.shape[2]), lambda e, f: (e, 0, 0)),
                  pl.BlockSpec((1, d, ft), lambda e, f: (e, 0, f)),
                  pl.BlockSpec((1, d, ft), lambda e, f: (e, 0, f)),
                  pl.BlockSpec((1, ft, d), lambda e, f: (e, f, 0))],
        out_specs=pl.BlockSpec((1, slots, d), lambda e, f: (e, 0, 0)),
        scratch_shapes=[pltpu.VMEM((slots, d), F32)] if nf > 1 else [],
        compiler_params=_params("arbitrary", "arbitrary"),
        name="ffn",
    )(xe, ge, w1, w3, w2)


def _combine_kernel(first_ref, rounds_ref, *refs, groups, alpha, slots):
    ng = len(groups)
    rk_refs, x1_refs = refs[0:ng], refs[ng:2 * ng]
    mod_ref, lng_ref, lnb_ref, ye_hbm = refs[2 * ng:2 * ng + 4]
    out_refs = refs[2 * ng + 4:3 * ng + 4]
    buf, acc_scr, sem = refs[3 * ng + 4:]
    d = x1_refs[0].shape[1]
    b = pl.program_id(0)
    nblk = pl.num_programs(0)
    ne, tb = rk_refs[0].shape[1], rk_refs[0].shape[2]
    win = WIN_ROWS
    eye = (lax.broadcasted_iota(jnp.int32, (tb, tb), 0)
           == lax.broadcasted_iota(jnp.int32, (tb, tb), 1))
    eye = jnp.where(eye, 1.0, 0.0).astype(BF16)

    def starts_of(blk, r):
        firsts = [(first_ref[blk * ne + e] // BF16_ROWS) * BF16_ROWS + r * win for e in range(ne)]
        return firsts, [jnp.minimum(f, slots - win) for f in firsts]

    def windows(slot, starts):
        return [pltpu.make_async_copy(ye_hbm.at[e, pl.ds(pl.multiple_of(starts[e], BF16_ROWS), win), :],
                                      buf.at[slot, pl.ds(e * win, win), :], sem.at[slot, e])
                for e in range(ne)]

    def scatter(rk_ref, slot0, slot, firsts, starts, later_round):
        hits = _window_hits(rk_ref, starts, slot0, win)
        if later_round:
            hits = [h & ((rk_ref[0, e:e + 1, :] + slot0) >= firsts[e]) for e, h in enumerate(hits)]
        hit = jnp.where(jnp.concatenate(hits, axis=0), 1.0, 0.0).astype(BF16)
        hit_t = _dot_nt(eye, hit).astype(BF16)
        return _dot(hit_t, buf[slot])

    cur = b % 2

    @pl.when(b == 0)
    def _prime():
        for cp in windows(0, starts_of(0, 0)[1]):
            cp.start()

    @pl.when(b + 1 < nblk)
    def _prefetch():
        for cp in windows(1 - cur, starts_of(b + 1, 0)[1]):
            cp.start()

    firsts0, starts0 = starts_of(b, 0)
    for cp in windows(cur, starts0):
        cp.wait()

    def group_body(rk_ref, x1_ref, out_ref, g, local):
        acc_scr[...] = scatter(rk_ref, g["slot0"], cur, firsts0, starts0, False)

        def extra_round(r, carry):
            firsts, starts = starts_of(b, r)
            for cp in windows(2, starts):
                cp.start()
            for cp in windows(2, starts):
                cp.wait()
            acc_scr[...] += scatter(rk_ref, g["slot0"], 2, firsts, starts, True)
            return carry

        lax.fori_loop(1, rounds_ref[b], extra_round, 0)
        r = g["row0"] + local // g["blocks_per_batch"]
        g2 = mod_ref[pl.ds(r, 1), :][:, 5 * d:6 * d]
        out_ref[...] = _layer_norm(alpha * x1_ref[...] + g2 * acc_scr[...], lng_ref[...], lnb_ref[...])

    blk0 = 0
    for gi, g in enumerate(groups):
        @pl.when((b >= blk0) & (b < blk0 + g["nb"]))
        def _(gi=gi, g=g, blk0=blk0):
            group_body(rk_refs[gi], x1_refs[gi], out_refs[gi], g, b - blk0)
        blk0 += g["nb"]


def _combine(ye, groups, first, rounds, mod, ln_g, ln_b, alpha):
    d = groups[0]["x1"].shape[1]
    ne, tb = groups[0]["rank"].shape[1:]
    slots = ye.shape[1]
    meta, specs_rk, specs_x1 = [], [], []
    blk0 = 0
    for g in groups:
        nb = g["rank"].shape[0]
        meta.append(dict(nb=nb, slot0=g["slot0"], row0=g["row0"], blocks_per_batch=g["blocks_per_batch"]))
        local = lambda b, *_, blk0=blk0, nb=nb: jnp.clip(b - blk0, 0, nb - 1)
        specs_rk.append(pl.BlockSpec((1, ne, tb), lambda b, *_, local=local: (local(b), 0, 0)))
        specs_x1.append(pl.BlockSpec((tb, d), lambda b, *_, local=local: (local(b), 0)))
        blk0 += nb
    const = lambda shape: pl.BlockSpec(shape, lambda b, *_: (0,) * len(shape))
    return pl.pallas_call(
        functools.partial(_combine_kernel, groups=meta, alpha=alpha, slots=slots),
        out_shape=[jax.ShapeDtypeStruct(g["x1"].shape, F32) for g in groups],
        grid_spec=pltpu.PrefetchScalarGridSpec(
            num_scalar_prefetch=2,
            grid=(blk0,),
            in_specs=specs_rk + specs_x1 + [const(mod.shape), const((1, d)), const((1, d)),
                                            pl.BlockSpec(memory_space=pl.ANY)],
            out_specs=specs_x1,
            scratch_shapes=[pltpu.VMEM((3, ne * WIN_ROWS, d), ye.dtype), pltpu.VMEM((tb, d), F32),
                            pltpu.SemaphoreType.DMA((3, ne))]),
        compiler_params=_params("arbitrary"),
        name="combine",
    )(first, rounds, *[g["rank"] for g in groups], *[g["x1"] for g in groups], mod, ln_g, ln_b, ye)


def _prep_weights(w_in, q_norm, w_uq, kv_norm, w_ukv, w_o_mla, hgrn_norm, w_o_hg, w_out, ln1_g, ln1_b,
                  w_router):
    d = w_in.shape[0]
    q_lora, kv_lora = q_norm.shape[0], kv_norm.shape[0]
    hw = HG_HEADS * HG_DK
    hh, hp = MLA_HEADS, MLA_HEADS * LANES
    o_kv, o_pe = q_lora, q_lora + kv_lora
    o_h = o_pe + MLA_ROPE
    o_g = o_h + 5 * hw
    assert w_in.shape[1] == o_g + 2 * d
    qk = MLA_NOPE + MLA_ROPE
    kvw = MLA_NOPE + MLA_V
    b16 = lambda a: a.astype(BF16)
    assert all(o % BF16_ROWS == 0 for o in (o_kv, o_pe, o_h, o_g))
    win = b16(w_in.T)
    wuq = jnp.pad(w_uq.reshape(q_lora, hh, qk), ((0, 0), (0, 0), (0, LANES - qk))).reshape(q_lora, hp)
    ukv = w_ukv.reshape(kv_lora, hh, kvw)
    wk = jnp.pad(ukv[:, :, :MLA_NOPE], ((0, 0), (0, 0), (0, LANES - MLA_NOPE))).reshape(kv_lora, hp)
    wv = ukv[:, :, MLA_NOPE:].reshape(kv_lora, hh * MLA_V)
    return dict(
        win=win, qn=q_norm.reshape(1, -1), wuq=b16(wuq), kvn=kv_norm.reshape(1, -1), wk=b16(wk), wv=b16(wv),
        hgn=hgrn_norm.reshape(1, -1), womla=b16(w_o_mla), wohg=b16(w_o_hg), wout=b16(w_out),
        ln1g=ln1_g.reshape(1, -1), ln1b=ln1_b.reshape(1, -1), wr=b16(w_router.T))


def _rope_tables(seq):
    n_freq = MLA_ROPE // 4
    inv = ROPE_BASE ** (-np.arange(n_freq, dtype=np.float64) / n_freq)
    t = np.arange(seq)
    ang = np.concatenate([(t // GRID_W)[:, None] * inv, (t % GRID_W)[:, None] * inv], axis=-1)
    cos = np.repeat(np.cos(ang), 2, axis=1)
    sin = np.repeat(np.sin(ang), 2, axis=1) * np.tile([-1.0, 1.0], MLA_ROPE // 2)
    ck = np.pad(cos, ((0, 0), (0, LANES - MLA_ROPE)), constant_values=1.0)
    sk = np.pad(sin, ((0, 0), (0, LANES - MLA_ROPE)))
    cq = np.pad(cos, ((0, 0), (MLA_NOPE, LANES - MLA_NOPE - MLA_ROPE)), constant_values=1.0)
    sq = np.pad(sin, ((0, 0), (MLA_NOPE, LANES - MLA_NOPE - MLA_ROPE)))
    return tuple(jnp.asarray(a, F32) for a in (cq, sq, ck, sk))


def _window_sched(cnt, slot0):
    first = slot0 + cnt[:, :-1]
    end = slot0 + cnt[:, 1:]
    flat = lambda a: a.T.reshape(-1).astype(jnp.int32)
    rounds = jnp.max((end - (first // BF16_ROWS) * BF16_ROWS + WIN_ROWS - 1) // WIN_ROWS, axis=0)
    return flat(first), flat(end), jnp.maximum(rounds, 1).astype(jnp.int32)


def kernel(x_prompt, x_sample, c, cache_ckv, cache_kpe, state_hgrn, c_ctx, w_ada, b_ada, w_in, mla_q_norm, mla_w_uq, mla_kv_norm, mla_w_ukv, mla_w_o, hgrn_gamma, hgrn_norm, hgrn_w_o, w_out, ln1_g, ln1_b, moe_w_router, moe_w1, moe_w3, moe_w2, ln2_g, ln2_b):
    depth = w_ada.shape[0]
    assert depth == 1, "single trunk layer"
    bp, tp, d = x_prompt.shape
    bs, tsq, _ = x_sample.shape
    ne = moe_w_router.shape[-1]
    alpha = (2 * depth) ** 0.25
    past = cache_ckv.shape[2]
    assert tp % TOK_BLOCK == 0 and tsq % TOK_BLOCK == 0 and past % TOK_BLOCK == 0 and tsq % GRID_W == 0

    wts = _prep_weights(w_in[0], mla_q_norm[0], mla_w_uq[0], mla_kv_norm[0], mla_w_ukv[0], mla_w_o[0],
                        hgrn_norm[0], hgrn_w_o[0], w_out[0], ln1_g[0], ln1_b[0], moe_w_router[0])
    cond_rows = -(-(1 + bs) // SUBLANES) * SUBLANES
    cond = jnp.concatenate([c_ctx[None], c, jnp.zeros((cond_rows - 1 - bs, d), F32)], axis=0)
    mod = _adaln(cond, w_ada[0], b_ada[0])

    xs = [x_prompt.reshape(bp * tp, d), x_sample.reshape(bs * tsq, d)]
    dims = [(bp, tp), (bs, tsq)]
    rows = [(0, False), (1, True)]
    ropes = [None, _rope_tables(tsq)]
    kpe_c = jnp.pad(cache_kpe[:, 0].reshape(bs * past, MLA_ROPE), ((0, 0), (0, LANES - MLA_ROPE)))
    caches = [None, _kvup(cache_ckv[:, 0].reshape(bs * past, -1), kpe_c, wts)]
    inits = [None, state_hgrn[:, 0]]

    x1s, h2s, affs, extras = [], [], [], []
    for gi in range(2):
        (bt, sq), (row0, per_batch) = dims[gi], rows[gi]
        if sq == TOK_BLOCK and caches[gi] is None and ropes[gi] is None and not per_batch:
            x1, h2, aff, ckv, kpe, s_fin = _mixer(xs[gi], bt, sq, mod, hgrn_gamma, wts, alpha, ne, row0)
        else:
            q, k, v, ckv, kpe, hgx = _inproj(xs[gi], bt, sq, mod, hgrn_gamma, wts, row0, per_batch, ropes[gi])
            o_f, o_b, s_fin = _hgrn(hgx, bt, sq, inits[gi])
            x1, h2, aff = _postmix(xs[gi], bt, sq, mod, q, k, v, caches[gi], o_f, o_b, hgx, wts, alpha, ne,
                                   row0, per_batch)
        x1s.append(x1)
        h2s.append(h2)
        affs.append(aff)
        extras.append((ckv, kpe, s_fin))

    caps = [EC_FACTOR * x.shape[0] // ne for x in xs]
    slots = sum(caps)
    assert all(cp % BF16_ROWS == 0 for cp in caps) and slots >= WIN_ROWS
    groups, scheds = [], []
    slot0 = 0
    routed = _route(affs, caps)
    for gi in range(2):
        rank, cnt = routed[gi]
        groups.append(dict(h2=h2s[gi], rank=rank, aff=affs[gi], slot0=slot0, x1=x1s[gi], row0=rows[gi][0],
                           blocks_per_batch=dims[gi][1] // TOK_BLOCK if rows[gi][1] else 1 << 30))
        scheds.append(_window_sched(cnt, slot0))
        slot0 += caps[gi]
    first, end, rounds = [jnp.concatenate([s[k] for s in scheds]) for k in range(3)]
    xe, ge = _compact(groups, first, end, rounds, slots)
    ye = _ffn(xe, ge, moe_w1[0], moe_w3[0], moe_w2[0], slots, ft=moe_w1.shape[-1])
    outs = _combine(ye, groups, first, rounds, mod, ln2_g[0].reshape(1, -1), ln2_b[0].reshape(1, -1), alpha)

    ckv_p, kpe_p, st_p = extras[0]
    y_prompt = outs[0].reshape(bp, tp, d)
    y_sample = outs[1].reshape(bs, tsq, d)
    new_ckv = ckv_p.reshape(bp, 1, tp, -1)
    new_kpe = kpe_p[:, :MLA_ROPE].reshape(bp, 1, tp, MLA_ROPE)
    new_state = st_p.reshape(bp, 1, 2, HG_HEADS, HG_DK, HG_DV)
    return (y_prompt, y_sample, new_ckv, new_kpe, new_state)
```

```python
import functools
import math

import jax
import jax.numpy as jnp
import numpy as np
from jax import lax
from jax.experimental import pallas as pl
from jax.experimental.pallas import tpu as pltpu

F32 = jnp.float32
BF16 = jnp.bfloat16

MLA_HEADS = 8
MLA_NOPE = 64
MLA_ROPE = 32
MLA_V = 64
HG_HEADS = 4
HG_DK = 128
HG_DV = 128
HG_CHUNK = 32
GRID_W = 64
ROPE_BASE = 10000.0
EC_FACTOR = 2
EPS = 1e-6

LANES = 128
SUBLANES = 8
BF16_ROWS = 16
VMEM_LIMIT = 56 * 1024 * 1024

TOK_BLOCK = 256
WIN_ROWS = 64
UNSELECTED = -(1 << 30)

NT_DIMS = (((1,), (1,)), ((), ()))


def _dot(a, b):
    return jnp.dot(a, b, preferred_element_type=F32)


def _dot_nt(a, b):
    return lax.dot_general(a, b, NT_DIMS, preferred_element_type=F32)


def _silu(x):
    return x * jax.nn.sigmoid(x)


def _params(*sem):
    return pltpu.CompilerParams(dimension_semantics=sem, vmem_limit_bytes=VMEM_LIMIT)


def _const_spec(shape):
    zeros = (0,) * len(shape)
    return pl.BlockSpec(shape, lambda *_: zeros, pipeline_mode=pl.Buffered(1))


def _adaln_kernel(c_ref, w_ref, b_ref, o_ref):
    s = _silu(c_ref[...]).astype(BF16)
    o_ref[...] = _dot(s, w_ref[...].astype(BF16)) + b_ref[...]


def _adaln(cond, w_ada, b_ada):
    rows, d = cond.shape
    n = w_ada.shape[1]
    tn = n // 4
    return pl.pallas_call(
        _adaln_kernel,
        out_shape=jax.ShapeDtypeStruct((rows, n), F32),
        grid=(n // tn,),
        in_specs=[_const_spec((rows, d)),
                  pl.BlockSpec((d, tn), lambda j: (0, j)),
                  pl.BlockSpec((1, tn), lambda j: (0, j))],
        out_specs=pl.BlockSpec((rows, tn), lambda j: (0, j)),
        compiler_params=_params("arbitrary"),
        name="adaln",
    )(cond, w_ada, b_ada.reshape(1, n))


def _rms(x, g):
    return x * lax.rsqrt(jnp.mean(x * x, axis=-1, keepdims=True) + EPS) * g


def _rope(x, c, s):
    w = x.shape[-1]
    lane = lax.broadcasted_iota(jnp.int32, x.shape, 1)
    nxt = pltpu.roll(x, w - 1, 1)
    prv = pltpu.roll(x, 1, 1)
    return x * c + jnp.where(lane % 2 == 0, nxt, prv) * s


N_INPROJ_WEIGHTS = 6


def _mod_row(mod_ref, row0, per_batch):
    r = row0 + pl.program_id(0) if per_batch else row0
    return mod_ref[pl.ds(r, 1), :]


def _modulated(x_ref, m):
    d = x_ref.shape[1]
    return (x_ref[...] * (1.0 + m[:, d:2 * d]) + m[:, 0:d]).astype(BF16)


def _keys(k_nope, kpe):
    shared = pltpu.roll(kpe, MLA_NOPE, 1)
    return (k_nope + jnp.concatenate([shared] * MLA_HEADS, axis=1)).astype(BF16)


def _inproj_kernel(*refs, row0, per_batch, rope):
    x_ref, mod_ref = refs[:2]
    _inproj_body(x_ref, _mod_row(mod_ref, row0, per_batch), *refs[2:], rope=rope)


def _inproj_body(x_ref, m, *refs, rope):
    gam_ref, win_ref, qn_ref, wuq_ref, kvn_ref, wk_ref, wv_ref = refs[:1 + N_INPROJ_WEIGHTS]
    refs = refs[1 + N_INPROJ_WEIGHTS:]
    if rope:
        cq_ref, sq_ref, ck_ref, sk_ref = refs[:4]
        refs = refs[4:]
    q_o, k_o, v_o, ckv_o, kpe_o, hgx_o = refs
    h = _modulated(x_ref, m)
    hw = HG_HEADS * HG_DK
    o_kv = qn_ref.shape[1]
    o_pe = o_kv + kvn_ref.shape[1]
    o_h = o_pe + MLA_ROPE

    cq = _rms(_dot_nt(h, win_ref[0:o_kv, :]), qn_ref[...])
    q = _dot(cq.astype(BF16), wuq_ref[...])
    if rope:
        q = _rope(q, jnp.concatenate([cq_ref[...]] * MLA_HEADS, axis=1),
                  jnp.concatenate([sq_ref[...]] * MLA_HEADS, axis=1))
    q_o[...] = q.astype(BF16)

    ckv = _rms(_dot_nt(h, win_ref[o_kv:o_pe, :]), kvn_ref[...])
    ckv_o[...] = ckv
    kpe = _dot_nt(h, win_ref[o_pe:o_h, :])
    kpe = jnp.concatenate([kpe, jnp.zeros((kpe.shape[0], LANES - MLA_ROPE), F32)], axis=1)
    if rope:
        kpe = _rope(kpe, ck_ref[...], sk_ref[...])
    kpe_o[...] = kpe
    cb = ckv.astype(BF16)
    k_o[...] = _keys(_dot(cb, wk_ref[...]), kpe)
    v_o[...] = _dot(cb, wv_ref[...]).astype(BF16)

    z = _dot_nt(h, win_ref[o_h:o_h + 5 * hw, :])
    hgx_o[:, 0:hw] = _silu(z[:, 0:hw])
    for dr in range(2):
        g0, g1 = gam_ref[dr, 0:1, :], gam_ref[dr, 1:2, :]
        gmax = jnp.maximum(g0, g1)
        e0, e1 = jnp.exp(g0 - gmax), jnp.exp(g1 - gmax)
        lb = e0 / (e0 + e1)
        f = lb + (1.0 - lb) * jax.nn.sigmoid(z[:, (1 + dr) * hw:(2 + dr) * hw])
        hgx_o[:, (1 + 2 * dr) * hw:(2 + 2 * dr) * hw] = jnp.log(f)
        hgx_o[:, (2 + 2 * dr) * hw:(3 + 2 * dr) * hw] = 1.0 - f
    hgx_o[:, 5 * hw:6 * hw] = z[:, 3 * hw:4 * hw]
    hgx_o[:, 6 * hw:7 * hw] = z[:, 4 * hw:5 * hw]


def _inproj(x2d, batch, seq, mod, gamma, wts, row0, per_batch, rope_tabs):
    n, d = x2d.shape
    tm = MIXER_SEQS * TOK_BLOCK if seq % (MIXER_SEQS * TOK_BLOCK) == 0 else TOK_BLOCK
    nblk = seq // tm
    rope = rope_tabs is not None
    hp = MLA_HEADS * LANES
    hw = HG_HEADS * HG_DK
    tok = lambda b, i: (b * nblk + i, 0)
    pos = lambda b, i: (i, 0)
    weights = [wts[k] for k in INPROJ_KEYS]
    ins = [x2d, mod, gamma] + weights
    in_specs = ([pl.BlockSpec((tm, d), tok), _const_spec(mod.shape), _const_spec(gamma.shape)]
                + [_const_spec(w.shape) for w in weights])
    if rope:
        ins += list(rope_tabs)
        in_specs += [pl.BlockSpec((tm, t.shape[1]), pos) for t in rope_tabs]
    widths = [(hp, BF16), (hp, BF16), (MLA_HEADS * MLA_V, BF16), (wts["kvn"].shape[1], F32), (LANES, F32),
              (7 * hw, F32)]
    return pl.pallas_call(
        functools.partial(_inproj_kernel, row0=row0, per_batch=per_batch, rope=rope),
        out_shape=[jax.ShapeDtypeStruct((n, w), dt) for w, dt in widths],
        grid=(batch, nblk),
        in_specs=in_specs,
        out_specs=[pl.BlockSpec((tm, w), tok) for w, _ in widths],
        compiler_params=_params("arbitrary", "arbitrary"),
        name="inproj",
    )(*ins)


def _kvup_kernel(ckv_ref, kpe_ref, wk_ref, wv_ref, k_o, v_o):
    cb = ckv_ref[...].astype(BF16)
    k_o[...] = _keys(_dot(cb, wk_ref[...]), kpe_ref[...])
    v_o[...] = _dot(cb, wv_ref[...]).astype(BF16)


def _kvup(ckv2d, kpe2d, wts):
    n = ckv2d.shape[0]
    tm = TOK_BLOCK
    widths = [MLA_HEADS * LANES, MLA_HEADS * MLA_V]
    row = lambda i: (i, 0)
    ws = [wts["wk"], wts["wv"]]
    return pl.pallas_call(
        _kvup_kernel,
        out_shape=[jax.ShapeDtypeStruct((n, w), BF16) for w in widths],
        grid=(n // tm,),
        in_specs=[pl.BlockSpec((tm, ckv2d.shape[1]), row), pl.BlockSpec((tm, LANES), row)]
                 + [_const_spec(w.shape) for w in ws],
        out_specs=[pl.BlockSpec((tm, w), row) for w in widths],
        compiler_params=_params("arbitrary"),
        name="kvup",
    )(ckv2d, kpe2d, *ws)


ATTN_SCALE = (MLA_NOPE + MLA_ROPE) ** -0.5


def _attn_body(q_ref, k_ref, v_ref, kc_ref, vc_ref, o_ref):
    cached = kc_ref is not None
    scale = ATTN_SCALE * math.log2(math.e)
    per_slab = LANES // MLA_V
    own = lax.broadcasted_iota(jnp.int32, (q_ref.shape[0], LANES), 1) // MLA_V
    for slab in range(MLA_HEADS // per_slab):
        vsl = slice(slab * LANES, (slab + 1) * LANES)
        out = None
        for sub in range(per_slab):
            hd = slab * per_slab + sub
            sl = slice(hd * LANES, (hd + 1) * LANES)
            q = q_ref[:, sl]
            s = _dot_nt(q, k_ref[:, sl])
            mx = jnp.max(s, axis=-1, keepdims=True)
            if cached:
                s2 = _dot_nt(q, kc_ref[:, sl])
                mx = jnp.maximum(mx, jnp.max(s2, axis=-1, keepdims=True))
            e = jnp.exp2((s - mx) * scale)
            den = jnp.sum(e, axis=-1, keepdims=True)
            o = _dot(e.astype(BF16), v_ref[:, vsl])
            if cached:
                e2 = jnp.exp2((s2 - mx) * scale)
                den = den + jnp.sum(e2, axis=-1, keepdims=True)
                o = o + _dot(e2.astype(BF16), vc_ref[:, vsl])
            o = o / den
            out = o if out is None else jnp.where(own == sub, o, out)
        o_ref[:, vsl] = out.astype(o_ref.dtype)


def _chunk_scan(x, reverse):
    tm = x.shape[0]
    rin = lax.broadcasted_iota(jnp.int32, x.shape, 0) % HG_CHUNK
    step = 1
    while step < HG_CHUNK:
        if reverse:
            x = x + jnp.where(rin < HG_CHUNK - step, pltpu.roll(x, tm - step, 0), 0.0)
        else:
            x = x + jnp.where(rin >= step, pltpu.roll(x, step, 0), 0.0)
        step *= 2
    return x


def _hgrn_kernel(*refs, has_init):
    fwd, bwd = refs[0:4], refs[4:8]
    refs = refs[8:]
    s0_ref = None
    if has_init:
        s0_ref = refs[0]
        refs = refs[1:]
    of_ref, ob_ref, sfin_ref, st_scr = refs
    i = pl.program_id(1)
    _hgrn_body([(fwd, bwd, of_ref, ob_ref, sfin_ref)], s0_ref, st_scr, i == 0, i == pl.num_programs(1) - 1)


def _hgrn_body(jobs, s0_ref, st_scr, first, last):
    tm = jobs[0][0][0].shape[0]
    c = HG_CHUNK
    nch = tm // c
    dk, dv = HG_DK, HG_DV
    hw = HG_HEADS * dk

    def initial(dr, hd):
        return s0_ref[0, dr, hd].T if s0_ref is not None else jnp.zeros((dv, dk), F32)

    if st_scr is not None:
        @pl.when(first)
        def _init():
            for dr in range(2):
                for hd in range(HG_HEADS):
                    st_scr[dr, hd] = initial(dr, hd)

    npair = nch // 2
    pair = 2 * c
    row = lax.broadcasted_iota(jnp.int32, (tm, tm), 0)
    col = lax.broadcasted_iota(jnp.int32, (tm, tm), 1)
    same = (row // c) == (col // c)
    same_pair = (row // pair) == (col // pair)
    bd = (lax.broadcasted_iota(jnp.int32, (tm, npair * dk), 0) // pair
          == lax.broadcasted_iota(jnp.int32, (tm, npair * dk), 1) // dk)
    chunk_odd = (lax.broadcasted_iota(jnp.int32, (tm, hw), 0) // c) % 2 == 1

    for dr in range(2):
        tri = same & ((col <= row) if dr == 0 else (col >= row))
        cross = same_pair & (((row // c) > (col // c)) if dr == 0 else ((row // c) < (col // c)))
        second = chunk_odd if dr == 0 else ~chunk_odd
        order = range(npair) if dr == 0 else range(npair - 1, -1, -1)

        def decayed(job):
            hq_ref, lf_ref, kk_ref, vv_ref = job[dr]
            bcum = _chunk_scan(lf_ref[...], reverse=dr == 1)
            closing = c - 1 if dr == 0 else 0
            btot3 = bcum.reshape(nch, c, hw)[:, closing:closing + 1, :]
            btot = jnp.broadcast_to(btot3, (nch, c, hw)).reshape(tm, hw)
            bpart = jnp.where(chunk_odd, pltpu.roll(btot, c, 0), pltpu.roll(btot, tm - c, 0))
            epart = jnp.exp(bpart)
            kk = kk_ref[...]
            qd = hq_ref[...] * jnp.exp(bcum)
            kd = kk * jnp.exp(-bcum)
            ke = kk * jnp.exp(btot - bcum)
            qd2 = jnp.where(second, qd * epart, qd)
            ke2 = jnp.where(second, ke, ke * epart)
            return qd, kd, ke, qd2, ke2, btot + bpart, vv_ref[...]

        def head(job, hd, qd, kd, ke, qd2, ke2, bpair, vv):
            o_ref, sfin_ref = job[2 + dr], job[4]
            sl = slice(hd * dk, (hd + 1) * dk)
            qd_h = qd[:, sl].astype(BF16)
            v_h = vv[:, hd * dv:(hd + 1) * dv]
            a = jnp.where(tri, _dot_nt(qd_h, kd[:, sl].astype(BF16)),
                          jnp.where(cross, _dot_nt(qd_h, ke[:, sl].astype(BF16)), 0.0))
            o_intra = _dot(a.astype(BF16), v_h.astype(BF16))
            kebd = jnp.where(bd, jnp.concatenate([ke2[:, sl]] * npair, axis=1), 0.0).astype(BF16)
            qbd = jnp.where(bd, jnp.concatenate([qd2[:, sl]] * npair, axis=1), 0.0).astype(BF16)
            ut = _dot(v_h.T.astype(BF16), kebd)
            st = st_scr[dr, hd] if st_scr is not None else initial(dr, hd)
            prev = [None] * npair
            for p in order:
                prev[p] = st
                st = st * jnp.exp(bpair[p * pair:p * pair + 1, sl]) + ut[:, p * dk:(p + 1) * dk]
            if st_scr is not None:
                st_scr[dr, hd] = st
            o_inter = _dot_nt(qbd, jnp.concatenate(prev, axis=1).astype(BF16))
            o_ref[:, hd * dv:(hd + 1) * dv] = o_intra + o_inter

            if last is True:
                sfin_ref[0, dr, hd] = st.T
            else:
                @pl.when(last)
                def _final():
                    sfin_ref[0, dr, hd] = st.T

        prepared = [decayed(job) for job in jobs]
        for hd in range(HG_HEADS):
            for job, arrays in zip(jobs, prepared):
                head(job, hd, *arrays)


def _hgrn(hgx, batch, seq, s0=None):
    n = hgx.shape[0]
    tm = TOK_BLOCK
    nblk = seq // tm
    hw = HG_HEADS * HG_DK

    def spec(lane_blk, rev):
        if rev:
            return pl.BlockSpec((tm, hw), lambda b, i: (b * nblk + nblk - 1 - i, lane_blk))
        return pl.BlockSpec((tm, hw), lambda b, i: (b * nblk + i, lane_blk))

    in_specs = [spec(0, False), spec(1, False), spec(2, False), spec(5, False),
                spec(0, True), spec(3, True), spec(4, True), spec(5, True)]
    ins = [hgx] * 8
    st_shape = (1, 2, HG_HEADS, HG_DK, HG_DV)
    st_spec = pl.BlockSpec(st_shape, lambda b, i: (b, 0, 0, 0, 0))
    if s0 is not None:
        ins.append(s0)
        in_specs.append(st_spec)
    return pl.pallas_call(
        functools.partial(_hgrn_kernel, has_init=s0 is not None),
        out_shape=[jax.ShapeDtypeStruct((n, hw), F32), jax.ShapeDtypeStruct((n, hw), F32),
                   jax.ShapeDtypeStruct((batch,) + st_shape[1:], F32)],
        grid=(batch, nblk),
        in_specs=in_specs,
        out_specs=[spec(0, False), spec(0, True), st_spec],
        scratch_shapes=[pltpu.VMEM((2, HG_HEADS, HG_DV, HG_DK), F32)],
        compiler_params=_params("arbitrary", "arbitrary"),
        name="hgrn",
    )(*ins)


def _layer_norm(x, g, b):
    xc = x - jnp.mean(x, axis=-1, keepdims=True)
    var = jnp.mean(xc * xc, axis=-1, keepdims=True)
    return xc * lax.rsqrt(var + EPS) * g + b


N_POSTMIX_WEIGHTS = 8
INPROJ_KEYS = ("win", "qn", "wuq", "kvn", "wk", "wv")
POSTMIX_KEYS = ("hgn", "womla", "wohg", "wout", "ln1g", "ln1b", "wr")
MIXER_SEQS = 2


def _postmix_kernel(x_ref, mod_ref, *refs, alpha, row0, per_batch, cached):
    n_attn = 5 if cached else 3
    q_ref, k_ref, v_ref = refs[:3]
    kc_ref, vc_ref = refs[3:5] if cached else (None, None)
    of_ref, ob_ref, zg_ref = refs[n_attn:n_attn + 3]
    om_s = refs[-1]
    _attn_body(q_ref, k_ref, v_ref, kc_ref, vc_ref, om_s)
    _postmix_body(x_ref, _mod_row(mod_ref, row0, per_batch), of_ref, ob_ref, zg_ref, om_s,
                  *refs[n_attn + 3:-1], alpha=alpha)


def _postmix_body(x_ref, m, of_ref, ob_ref, zg_ref, om_ref, wg_ref, hgn_ref, womla_ref,
                  wohg_ref, wout_ref, lng_ref, lnb_ref, wr_ref, x1_o, h2_o, aff_o, *, alpha):
    d = x_ref.shape[1]
    tb = aff_o.shape[2]
    g1, sh2, sc2 = m[:, 2 * d:3 * d], m[:, 3 * d:4 * d], m[:, 4 * d:5 * d]
    o = of_ref[...] + ob_ref[...]
    zg = zg_ref[...]
    parts = []
    for hd in range(HG_HEADS):
        sl = slice(hd * HG_DV, (hd + 1) * HG_DV)
        parts.append(_rms(o[:, sl], hgn_ref[...]) * _silu(zg[:, sl]))
    ohg = jnp.concatenate(parts, axis=1).astype(BF16)
    gates = _dot_nt(_modulated(x_ref, m), wg_ref[wg_ref.shape[0] - 2 * d:, :])
    merged = (jax.nn.sigmoid(gates[:, 0:d]) * _dot(om_ref[...], womla_ref[...])
              + jax.nn.sigmoid(gates[:, d:2 * d]) * _dot(ohg, wohg_ref[...]))
    mix = _dot(merged.astype(BF16), wout_ref[...])
    x1 = _layer_norm(alpha * x_ref[...] + g1 * mix, lng_ref[...], lnb_ref[...])
    x1_o[...] = x1
    h2 = (x1 * (1.0 + sc2) + sh2).astype(BF16)
    h2_o[...] = h2
    logits = _dot_nt(wr_ref[...], h2)
    e = jnp.exp(logits - jnp.max(logits, axis=0, keepdims=True))
    aff = e / jnp.sum(e, axis=0, keepdims=True)
    for blk in range(aff_o.shape[0]):
        aff_o[blk] = aff[:, blk * tb:(blk + 1) * tb]


def _postmix(x2d, batch, seq, mod, q, k, v, cache, o_f, o_b, hgx, wts, alpha, n_experts, row0, per_batch):
    n, d = x2d.shape
    tm = TOK_BLOCK
    nblk = seq // tm
    hw = HG_HEADS * HG_DV
    hp, hv = q.shape[1], v.shape[1]
    tok = lambda b, i: (b * nblk + i, 0)
    per_seq = lambda b, i: (b, 0)
    weights = [wts[k] for k in POSTMIX_KEYS]
    attn_ins = [q, k, v]
    attn_specs = [pl.BlockSpec((tm, hp), tok), pl.BlockSpec((seq, hp), per_seq), pl.BlockSpec((seq, hv), per_seq)]
    if cache is not None:
        past = cache[0].shape[0] // batch
        attn_ins += list(cache)
        attn_specs += [pl.BlockSpec((past, hp), per_seq), pl.BlockSpec((past, hv), per_seq)]
    return pl.pallas_call(
        functools.partial(_postmix_kernel, alpha=alpha, row0=row0, per_batch=per_batch,
                          cached=cache is not None),
        out_shape=[jax.ShapeDtypeStruct((n, d), F32), jax.ShapeDtypeStruct((n, d), BF16),
                   jax.ShapeDtypeStruct((n // tm, n_experts, tm), F32)],
        grid=(batch, nblk),
        in_specs=[pl.BlockSpec((tm, d), tok), _const_spec(mod.shape)] + attn_specs
                 + [pl.BlockSpec((tm, hw), tok), pl.BlockSpec((tm, hw), tok),
                    pl.BlockSpec((tm, hw), lambda b, i: (b * nblk + i, 6)), _const_spec(wts["win"].shape)]
                 + [_const_spec(w.shape) for w in weights],
        out_specs=[pl.BlockSpec((tm, d), tok), pl.BlockSpec((tm, d), tok),
                   pl.BlockSpec((1, n_experts, tm), lambda b, i: (b * nblk + i, 0, 0))],
        scratch_shapes=[pltpu.VMEM((tm, hv), BF16)],
        compiler_params=_params("arbitrary", "arbitrary"),
        name="postmix",
    )(x2d, mod, *attn_ins, o_f, o_b, hgx, wts["win"], *weights)


def _mixer_kernel(x_ref, mod_ref, *refs, alpha, row0, seq):
    nw = 1 + N_INPROJ_WEIGHTS
    in_w, refs = refs[:nw], refs[nw:]
    pm_w, refs = refs[:N_POSTMIX_WEIGHTS - 1], refs[N_POSTMIX_WEIGHTS - 1:]
    pm_w = (in_w[1],) + tuple(pm_w)
    x1_o, h2_o, aff_o, ckv_o, kpe_o, sfin_o, q_s, k_s, v_s, hgx_s, om_s, of_s, ob_s = refs
    m = _mod_row(mod_ref, row0, False)
    _inproj_body(x_ref, m, *in_w, q_s, k_s, v_s, ckv_o, kpe_o, hgx_s, rope=False)
    hw = HG_HEADS * HG_DK
    jobs = []
    for s in range(x_ref.shape[0] // seq):
        rows = slice(s * seq, (s + 1) * seq)
        _attn_body(q_s.at[rows], k_s.at[rows], v_s.at[rows], None, None, om_s.at[rows])
        lane = lambda j, rows=rows: hgx_s.at[rows, j * hw:(j + 1) * hw]
        jobs.append(((lane(0), lane(1), lane(2), lane(5)), (lane(0), lane(3), lane(4), lane(5)),
                     of_s.at[rows], ob_s.at[rows], sfin_o.at[s:s + 1]))
    _hgrn_body(jobs, None, None, True, True)
    _postmix_body(x_ref, m, of_s, ob_s, hgx_s.at[:, 6 * hw:7 * hw], om_s, *pm_w, x1_o, h2_o, aff_o,
                  alpha=alpha)


def _mixer(x2d, batch, seq, mod, gamma, wts, alpha, n_experts, row0):
    n, d = x2d.shape
    assert seq == TOK_BLOCK
    ns = MIXER_SEQS if batch % MIXER_SEQS == 0 else 1
    tm = ns * seq
    hp = MLA_HEADS * LANES
    hv = MLA_HEADS * MLA_V
    hw = HG_HEADS * HG_DK
    kvl = wts["kvn"].shape[1]
    weights = [wts[k] for k in INPROJ_KEYS + POSTMIX_KEYS]
    tok = lambda b: (b, 0)
    st_shape = (ns, 2, HG_HEADS, HG_DK, HG_DV)
    return pl.pallas_call(
        functools.partial(_mixer_kernel, alpha=alpha, row0=row0, seq=seq),
        out_shape=[jax.ShapeDtypeStruct((n, d), F32), jax.ShapeDtypeStruct((n, d), BF16),
                   jax.ShapeDtypeStruct((n // seq, n_experts, seq), F32),
                   jax.ShapeDtypeStruct((n, kvl), F32), jax.ShapeDtypeStruct((n, LANES), F32),
                   jax.ShapeDtypeStruct((batch,) + st_shape[1:], F32)],
        grid=(batch // ns,),
        in_specs=[pl.BlockSpec((tm, d), tok), _const_spec(mod.shape), _const_spec(gamma.shape)]
                 + [_const_spec(w.shape) for w in weights],
        out_specs=[pl.BlockSpec((tm, d), tok), pl.BlockSpec((tm, d), tok),
                   pl.BlockSpec((ns, n_experts, seq), lambda b: (b, 0, 0)),
                   pl.BlockSpec((tm, kvl), tok), pl.BlockSpec((tm, LANES), tok),
                   pl.BlockSpec(st_shape, lambda b: (b, 0, 0, 0, 0))],
        scratch_shapes=[pltpu.VMEM((tm, hp), BF16), pltpu.VMEM((tm, hp), BF16), pltpu.VMEM((tm, hv), BF16),
                        pltpu.VMEM((tm, 7 * hw), F32), pltpu.VMEM((tm, hv), BF16),
                        pltpu.VMEM((tm, hw), F32), pltpu.VMEM((tm, hw), F32)],
        compiler_params=_params("arbitrary"),
        name="mixer",
    )(x2d, mod, gamma, *weights)


def _route_kernel(*refs, caps):
    ng = len(caps)
    for aff_ref, rank_o, cnt_o, cap in zip(refs[0:ng], refs[ng:2 * ng], refs[2 * ng:3 * ng], caps):
        _route_group(aff_ref, rank_o, cnt_o, cap)


def _route_group(aff_ref, rank_o, cnt_o, cap):
    nb, ne, tb = aff_ref.shape
    key = aff_ref[...]

    def count(mask):
        return jnp.sum(jnp.sum(jnp.where(mask, 1.0, 0.0), axis=0), axis=1, keepdims=True)

    def bit_step(it, bits):
        cand = bits | jnp.left_shift(jnp.int32(1), 30 - it)
        return jnp.where(count(key >= pltpu.bitcast(cand, F32)[None]) >= cap, cand, bits)

    bits = lax.fori_loop(0, 31, bit_step, jnp.zeros((ne, 1), jnp.int32))
    thr = pltpu.bitcast(bits, F32)
    need = cap - count(key > thr[None])
    before = (lax.broadcasted_iota(jnp.int32, (tb, tb), 0)
              < lax.broadcasted_iota(jnp.int32, (tb, tb), 1))
    before = jnp.where(before, 1.0, 0.0).astype(BF16)
    off_eq = jnp.zeros((ne, 1), F32)
    off_sel = jnp.zeros((ne, 1), F32)
    cnt_o[...] = jnp.zeros_like(cnt_o)
    for blk in range(nb):
        key_b = key[blk]
        eq = key_b == thr
        eq_b = jnp.where(eq, 1.0, 0.0)
        eq_rank = _dot(eq_b.astype(BF16), before) + off_eq
        sel = (key_b > thr) | (eq & (eq_rank < need))
        sel_b = jnp.where(sel, 1.0, 0.0)
        rank = _dot(sel_b.astype(BF16), before) + off_sel
        rank_o[blk] = jnp.where(sel, rank.astype(jnp.int32), UNSELECTED)
        cnt_o[:, blk:blk + 1] = off_sel.astype(jnp.int32)
        off_eq = off_eq + jnp.sum(eq_b, axis=1, keepdims=True)
        off_sel = off_sel + jnp.sum(sel_b, axis=1, keepdims=True)
    cnt_o[:, nb:nb + 1] = off_sel.astype(jnp.int32)


def _route(affs, caps):
    ng = len(affs)
    ne = affs[0].shape[1]
    assert all(a.shape[0] + 1 <= LANES for a in affs)
    outs = pl.pallas_call(
        functools.partial(_route_kernel, caps=tuple(caps)),
        out_shape=[jax.ShapeDtypeStruct(a.shape, jnp.int32) for a in affs]
                  + [jax.ShapeDtypeStruct((ne, LANES), jnp.int32)] * ng,
        in_specs=[pl.BlockSpec(memory_space=pltpu.VMEM)] * ng,
        out_specs=[pl.BlockSpec(memory_space=pltpu.VMEM)] * (2 * ng),
        compiler_params=pltpu.CompilerParams(vmem_limit_bytes=VMEM_LIMIT),
        name="route",
    )(*affs)
    return [(outs[gi], outs[ng + gi][:, :affs[gi].shape[0] + 1]) for gi in range(ng)]


def _window_hits(rk_ref, firsts, slot0, win):
    ne, tb = rk_ref.shape[1], rk_ref.shape[2]
    win_iota = lax.broadcasted_iota(jnp.int32, (win, tb), 0)
    return [(rk_ref[0, e:e + 1, :] + (slot0 - firsts[e])) == win_iota for e in range(ne)]


def _compact_kernel(first_ref, end_ref, rounds_ref, *refs, groups, slots):
    ng = len(groups)
    h2_refs, rk_refs, af_refs = refs[0:ng], refs[ng:2 * ng], refs[2 * ng:3 * ng]
    hbms, refs = refs[3 * ng:3 * ng + 2], refs[3 * ng + 2:]
    stages, tails = refs[0:2], refs[2:4]
    sem, issued = refs[4:]
    b = pl.program_id(0)
    ne = rk_refs[0].shape[1]
    win = WIN_ROWS
    sub = BF16_ROWS

    def copies(slot, dsts):
        return [pltpu.make_async_copy(stage.at[slot, pl.ds(e * win, win), :],
                                      hbm.at[e, pl.ds(pl.multiple_of(dsts[e], sub), win), :], sem.at[c, e])
                for c, (stage, hbm) in enumerate(zip(stages, hbms)) for e in range(ne)]

    def wait_previous():
        @pl.when(issued[0] > 0)
        def _():
            for cp in copies(0, [0] * ne):
                cp.wait()

    @pl.when(b == 0)
    def _init():
        issued[0] = 0
        for stage, tail in zip(stages, tails):
            tail[...] = jnp.zeros_like(tail)
            stage[1] = jnp.zeros(stage.shape[1:], stage.dtype)
        pad = copies(1, [slots] * ne)
        for cp in pad:
            cp.start()
        for cp in pad:
            cp.wait()

    def group_body(h2_ref, rk_ref, af_ref, slot0):
        firsts = [first_ref[b * ne + e] for e in range(ne)]
        bases = [(f // sub) * sub for f in firsts]
        ends = [end_ref[b * ne + e] - bases[e] for e in range(ne)]

        def one_round(r, carry):
            dsts = [bases[e] + r * win for e in range(ne)]
            hits = _window_hits(rk_ref, dsts, slot0, win)
            onehot = jnp.where(jnp.concatenate(hits, axis=0), 1.0, 0.0).astype(BF16)
            gate = jnp.concatenate(
                [jnp.sum(jnp.where(hits[e], af_ref[0, e:e + 1, :], 0.0), axis=1, keepdims=True)
                 for e in range(ne)], axis=0)
            slot = issued[0] % 2
            streams = (_dot(onehot, h2_ref[...]).astype(BF16),
                       jnp.broadcast_to(gate, (ne * win, LANES)))
            for val, stage, tail in zip(streams, stages, tails):
                sub_iota = lax.broadcasted_iota(jnp.int32, (sub, val.shape[1]), 0)
                pieces = []
                for e in range(ne):
                    old = tail[e * sub:(e + 1) * sub, :]
                    shared = jnp.where(r == 0, firsts[e] - bases[e], 0)
                    groups_e = [val[e * win + g * sub:e * win + (g + 1) * sub, :] for g in range(win // sub)]
                    groups_e[0] = jnp.where(sub_iota < shared, old, groups_e[0])
                    pieces += groups_e
                    last = (ends[e] // sub) * sub
                    new = old
                    for g, grp in enumerate(groups_e):
                        new = jnp.where((r == last // win) & (last % win == g * sub), grp, new)
                    tail[e * sub:(e + 1) * sub, :] = new
                stage[slot] = jnp.concatenate(pieces, axis=0)
            wait_previous()
            for cp in copies(slot, [jnp.minimum(dst, slots) for dst in dsts]):
                cp.start()
            issued[0] = issued[0] + 1
            return carry

        lax.fori_loop(0, rounds_ref[b], one_round, 0)

    blk0 = 0
    for gi, g in enumerate(groups):
        @pl.when((b >= blk0) & (b < blk0 + g["nb"]))
        def _(gi=gi, g=g):
            group_body(h2_refs[gi], rk_refs[gi], af_refs[gi], g["slot0"])
        blk0 += g["nb"]

    @pl.when(b == pl.num_programs(0) - 1)
    def _drain():
        wait_previous()


def _compact(groups, first, end, rounds, slots):
    d = groups[0]["h2"].shape[1]
    nbs = [g["rank"].shape[0] for g in groups]
    ne, tb = groups[0]["rank"].shape[1:]
    meta, specs_h2, specs_rk = [], [], []
    blk0 = 0
    for g, nb in zip(groups, nbs):
        meta.append(dict(nb=nb, slot0=g["slot0"]))
        local = lambda b, *_, blk0=blk0, nb=nb: jnp.clip(b - blk0, 0, nb - 1)
        specs_h2.append(pl.BlockSpec((tb, d), lambda b, *_, local=local: (local(b), 0)))
        specs_rk.append(pl.BlockSpec((1, ne, tb), lambda b, *_, local=local: (local(b), 0, 0)))
        blk0 += nb
    streams = [(d, BF16), (LANES, F32)]
    return pl.pallas_call(
        functools.partial(_compact_kernel, groups=meta, slots=slots),
        out_shape=[jax.ShapeDtypeStruct((ne, slots + WIN_ROWS, w), dt) for w, dt in streams],
        grid_spec=pltpu.PrefetchScalarGridSpec(
            num_scalar_prefetch=3,
            grid=(sum(nbs),),
            in_specs=specs_h2 + specs_rk + specs_rk,
            out_specs=[pl.BlockSpec(memory_space=pl.ANY)] * 2,
            scratch_shapes=[pltpu.VMEM((2, ne * WIN_ROWS, w), dt) for w, dt in streams]
                           + [pltpu.VMEM((ne * BF16_ROWS, w), dt) for w, dt in streams]
                           + [pltpu.SemaphoreType.DMA((2, ne)), pltpu.SMEM((1,), jnp.int32)]),
        compiler_params=_params("arbitrary"),
        name="compact",
    )(first, end, rounds, *[g["h2"] for g in groups], *[g["rank"] for g in groups],
      *[g["aff"] for g in groups])


def _ffn_kernel(xe_ref, ge_ref, w1_ref, w3_ref, w2_ref, ye_ref, *scratch):
    f = pl.program_id(1)
    x = xe_ref[0]
    hid = _silu(_dot(x, w1_ref[0].astype(BF16))) * _dot(x, w3_ref[0].astype(BF16))
    y = _dot(hid.astype(BF16), w2_ref[0].astype(BF16))
    if not scratch:
        ye_ref[0] = (y * ge_ref[0, :, 0:1]).astype(ye_ref.dtype)
        return
    acc_scr, = scratch
    last = pl.num_programs(1) - 1

    @pl.when(f == 0)
    def _first():
        acc_scr[...] = y

    @pl.when((f > 0) & (f < last))
    def _middle():
        acc_scr[...] += y

    @pl.when(f == last)
    def _last():
        ye_ref[0] = ((acc_scr[...] + y) * ge_ref[0, :, 0:1]).astype(ye_ref.dtype)


def _ffn(xe, ge, w1, w3, w2, slots, ft):
    ne, d, dff = w1.shape
    nf = dff // ft
    return pl.pallas_call(
        _ffn_kernel,
        out_shape=jax.ShapeDtypeStruct((ne, slots, d), BF16),
        grid=(ne, nf),
        in_specs=[pl.BlockSpec((1, slots, d), lambda e, f: (e, 0, 0)),
                  pl.BlockSpec((1, slots, ge.shape[2]), lambda e, f: (e, 0, 0)),
                  pl.BlockSpec((1, d, ft), lambda e, f: (e, 0, f)),
                  pl.BlockSpec((1, d, ft), lambda e, f: (e, 0, f)),
                  pl.BlockSpec((1, ft, d), lambda e, f: (e, f, 0))],
        out_specs=pl.BlockSpec((1, slots, d), lambda e, f: (e, 0, 0)),
        scratch_shapes=[pltpu.VMEM((slots, d), F32)] if nf > 1 else [],
        compiler_params=_params("arbitrary", "arbitrary"),
        name="ffn",
    )(xe, ge, w1, w3, w2)


def _combine_kernel(first_ref, rounds_ref, *refs, groups, alpha, slots):
    ng = len(groups)
    rk_refs, x1_refs = refs[0:ng], refs[ng:2 * ng]
    mod_ref, lng_ref, lnb_ref, ye_hbm = refs[2 * ng:2 * ng + 4]
    out_refs = refs[2 * ng + 4:3 * ng + 4]
    buf, acc_scr, sem = refs[3 * ng + 4:]
    d = x1_refs[0].shape[1]
    b = pl.program_id(0)
    nblk = pl.num_programs(0)
    ne, tb = rk_refs[0].shape[1], rk_refs[0].shape[2]
    win = WIN_ROWS
    eye = (lax.broadcasted_iota(jnp.int32, (tb, tb), 0)
           == lax.broadcasted_iota(jnp.int32, (tb, tb), 1))
    eye = jnp.where(eye, 1.0, 0.0).astype(BF16)

    def starts_of(blk, r):
        firsts = [(first_ref[blk * ne + e] // BF16_ROWS) * BF16_ROWS + r * win for e in range(ne)]
        return firsts, [jnp.minimum(f, slots - win) for f in firsts]

    def windows(slot, starts):
        return [pltpu.make_async_copy(ye_hbm.at[e, pl.ds(pl.multiple_of(starts[e], BF16_ROWS), win), :],
                                      buf.at[slot, pl.ds(e * win, win), :], sem.at[slot, e])
                for e in range(ne)]

    def scatter(rk_ref, slot0, slot, firsts, starts, later_round):
        hits = _window_hits(rk_ref, starts, slot0, win)
        if later_round:
            hits = [h & ((rk_ref[0, e:e + 1, :] + slot0) >= firsts[e]) for e, h in enumerate(hits)]
        hit = jnp.where(jnp.concatenate(hits, axis=0), 1.0, 0.0).astype(BF16)
        hit_t = _dot_nt(eye, hit).astype(BF16)
        return _dot(hit_t, buf[slot])

    cur = b % 2

    @pl.when(b == 0)
    def _prime():
        for cp in windows(0, starts_of(0, 0)[1]):
            cp.start()

    @pl.when(b + 1 < nblk)
    def _prefetch():
        for cp in windows(1 - cur, starts_of(b + 1, 0)[1]):
            cp.start()

    firsts0, starts0 = starts_of(b, 0)
    for cp in windows(cur, starts0):
        cp.wait()

    def group_body(rk_ref, x1_ref, out_ref, g, local):
        acc_scr[...] = scatter(rk_ref, g["slot0"], cur, firsts0, starts0, False)

        def extra_round(r, carry):
            firsts, starts = starts_of(b, r)
            for cp in windows(2, starts):
                cp.start()
            for cp in windows(2, starts):
                cp.wait()
            acc_scr[...] += scatter(rk_ref, g["slot0"], 2, firsts, starts, True)
            return carry

        lax.fori_loop(1, rounds_ref[b], extra_round, 0)
        r = g["row0"] + local // g["blocks_per_batch"]
        g2 = mod_ref[pl.ds(r, 1), :][:, 5 * d:6 * d]
        out_ref[...] = _layer_norm(alpha * x1_ref[...] + g2 * acc_scr[...], lng_ref[...], lnb_ref[...])

    blk0 = 0
    for gi, g in enumerate(groups):
        @pl.when((b >= blk0) & (b < blk0 + g["nb"]))
        def _(gi=gi, g=g, blk0=blk0):
            group_body(rk_refs[gi], x1_refs[gi], out_refs[gi], g, b - blk0)
        blk0 += g["nb"]


def _combine(ye, groups, first, rounds, mod, ln_g, ln_b, alpha):
    d = groups[0]["x1"].shape[1]
    ne, tb = groups[0]["rank"].shape[1:]
    slots = ye.shape[1]
    meta, specs_rk, specs_x1 = [], [], []
    blk0 = 0
    for g in groups:
        nb = g["rank"].shape[0]
        meta.append(dict(nb=nb, slot0=g["slot0"], row0=g["row0"], blocks_per_batch=g["blocks_per_batch"]))
        local = lambda b, *_, blk0=blk0, nb=nb: jnp.clip(b - blk0, 0, nb - 1)
        specs_rk.append(pl.BlockSpec((1, ne, tb), lambda b, *_, local=local: (local(b), 0, 0)))
        specs_x1.append(pl.BlockSpec((tb, d), lambda b, *_, local=local: (local(b), 0)))
        blk0 += nb
    const = lambda shape: pl.BlockSpec(shape, lambda b, *_: (0,) * len(shape))
    return pl.pallas_call(
        functools.partial(_combine_kernel, groups=meta, alpha=alpha, slots=slots),
        out_shape=[jax.ShapeDtypeStruct(g["x1"].shape, F32) for g in groups],
        grid_spec=pltpu.PrefetchScalarGridSpec(
            num_scalar_prefetch=2,
            grid=(blk0,),
            in_specs=specs_rk + specs_x1 + [const(mod.shape), const((1, d)), const((1, d)),
                                            pl.BlockSpec(memory_space=pl.ANY)],
            out_specs=specs_x1,
            scratch_shapes=[pltpu.VMEM((3, ne * WIN_ROWS, d), ye.dtype), pltpu.VMEM((tb, d), F32),
                            pltpu.SemaphoreType.DMA((3, ne))]),
        compiler_params=_params("arbitrary"),
        name="combine",
    )(first, rounds, *[g["rank"] for g in groups], *[g["x1"] for g in groups], mod, ln_g, ln_b, ye)


def _prep_weights(w_in, q_norm, w_uq, kv_norm, w_ukv, w_o_mla, hgrn_norm, w_o_hg, w_out, ln1_g, ln1_b,
                  w_router):
    d = w_in.shape[0]
    q_lora, kv_lora = q_norm.shape[0], kv_norm.shape[0]
    hw = HG_HEADS * HG_DK
    hh, hp = MLA_HEADS, MLA_HEADS * LANES
    o_kv, o_pe = q_lora, q_lora + kv_lora
    o_h = o_pe + MLA_ROPE
    o_g = o_h + 5 * hw
    assert w_in.shape[1] == o_g + 2 * d
    qk = MLA_NOPE + MLA_ROPE
    kvw = MLA_NOPE + MLA_V
    b16 = lambda a: a.astype(BF16)
    assert all(o % BF16_ROWS == 0 for o in (o_kv, o_pe, o_h, o_g))
    win = b16(w_in.T)
    wuq = jnp.pad(w_uq.reshape(q_lora, hh, qk), ((0, 0), (0, 0), (0, LANES - qk))).reshape(q_lora, hp)
    ukv = w_ukv.reshape(kv_lora, hh, kvw)
    wk = jnp.pad(ukv[:, :, :MLA_NOPE], ((0, 0), (0, 0), (0, LANES - MLA_NOPE))).reshape(kv_lora, hp)
    wv = ukv[:, :, MLA_NOPE:].reshape(kv_lora, hh * MLA_V)
    return dict(
        win=win, qn=q_norm.reshape(1, -1), wuq=b16(wuq), kvn=kv_norm.reshape(1, -1), wk=b16(wk), wv=b16(wv),
        hgn=hgrn_norm.reshape(1, -1), womla=b16(w_o_mla), wohg=b16(w_o_hg), wout=b16(w_out),
        ln1g=ln1_g.reshape(1, -1), ln1b=ln1_b.reshape(1, -1), wr=b16(w_router.T))


def _rope_tables(seq):
    n_freq = MLA_ROPE // 4
    inv = ROPE_BASE ** (-np.arange(n_freq, dtype=np.float64) / n_freq)
    t = np.arange(seq)
    ang = np.concatenate([(t // GRID_W)[:, None] * inv, (t % GRID_W)[:, None] * inv], axis=-1)
    cos = np.repeat(np.cos(ang), 2, axis=1)
    sin = np.repeat(np.sin(ang), 2, axis=1) * np.tile([-1.0, 1.0], MLA_ROPE // 2)
    ck = np.pad(cos, ((0, 0), (0, LANES - MLA_ROPE)), constant_values=1.0)
    sk = np.pad(sin, ((0, 0), (0, LANES - MLA_ROPE)))
    cq = np.pad(cos, ((0, 0), (MLA_NOPE, LANES - MLA_NOPE - MLA_ROPE)), constant_values=1.0)
    sq = np.pad(sin, ((0, 0), (MLA_NOPE, LANES - MLA_NOPE - MLA_ROPE)))
    return tuple(jnp.asarray(a, F32) for a in (cq, sq, ck, sk))


def _window_sched(cnt, slot0):
    first = slot0 + cnt[:, :-1]
    end = slot0 + cnt[:, 1:]
    flat = lambda a: a.T.reshape(-1).astype(jnp.int32)
    rounds = jnp.max((end - (first // BF16_ROWS) * BF16_ROWS + WIN_ROWS - 1) // WIN_ROWS, axis=0)
    return flat(first), flat(end), jnp.maximum(rounds, 1).astype(jnp.int32)


def kernel(x_prompt, x_sample, c, cache_ckv, cache_kpe, state_hgrn, c_ctx, w_ada, b_ada, w_in, mla_q_norm, mla_w_uq, mla_kv_norm, mla_w_ukv, mla_w_o, hgrn_gamma, hgrn_norm, hgrn_w_o, w_out, ln1_g, ln1_b, moe_w_router, moe_w1, moe_w3, moe_w2, ln2_g, ln2_b):
    depth = w_ada.shape[0]
    assert depth == 1, "single trunk layer"
    bp, tp, d = x_prompt.shape
    bs, tsq, _ = x_sample.shape
    ne = moe_w_router.shape[-1]
    alpha = (2 * depth) ** 0.25
    past = cache_ckv.shape[2]
    assert tp % TOK_BLOCK == 0 and tsq % TOK_BLOCK == 0 and past % TOK_BLOCK == 0 and tsq % GRID_W == 0

    wts = _prep_weights(w_in[0], mla_q_norm[0], mla_w_uq[0], mla_kv_norm[0], mla_w_ukv[0], mla_w_o[0],
                        hgrn_norm[0], hgrn_w_o[0], w_out[0], ln1_g[0], ln1_b[0], moe_w_router[0])
    cond_rows = -(-(1 + bs) // SUBLANES) * SUBLANES
    cond = jnp.concatenate([c_ctx[None], c, jnp.zeros((cond_rows - 1 - bs, d), F32)], axis=0)
    mod = _adaln(cond, w_ada[0], b_ada[0])

    xs = [x_prompt.reshape(bp * tp, d), x_sample.reshape(bs * tsq, d)]
    dims = [(bp, tp), (bs, tsq)]
    rows = [(0, False), (1, True)]
    ropes = [None, _rope_tables(tsq)]
    kpe_c = jnp.pad(cache_kpe[:, 0].reshape(bs * past, MLA_ROPE), ((0, 0), (0, LANES - MLA_ROPE)))
    caches = [None, _kvup(cache_ckv[:, 0].reshape(bs * past, -1), kpe_c, wts)]
    inits = [None, state_hgrn[:, 0]]

    x1s, h2s, affs, extras = [], [], [], []
    for gi in range(2):
        (bt, sq), (row0, per_batch) = dims[gi], rows[gi]
        if sq == TOK_BLOCK and caches[gi] is None and ropes[gi] is None and not per_batch:
            x1, h2, aff, ckv, kpe, s_fin = _mixer(xs[gi], bt, sq, mod, hgrn_gamma, wts, alpha, ne, row0)
        else:
            q, k, v, ckv, kpe, hgx = _inproj(xs[gi], bt, sq, mod, hgrn_gamma, wts, row0, per_batch, ropes[gi])
            o_f, o_b, s_fin = _hgrn(hgx, bt, sq, inits[gi])
            x1, h2, aff = _postmix(xs[gi], bt, sq, mod, q, k, v, caches[gi], o_f, o_b, hgx, wts, alpha, ne,
                                   row0, per_batch)
        x1s.append(x1)
        h2s.append(h2)
        affs.append(aff)
        extras.append((ckv, kpe, s_fin))

    caps = [EC_FACTOR * x.shape[0] // ne for x in xs]
    slots = sum(caps)
    assert all(cp % BF16_ROWS == 0 for cp in caps) and slots >= WIN_ROWS
    groups, scheds = [], []
    slot0 = 0
    routed = _route(affs, caps)
    for gi in range(2):
        rank, cnt = routed[gi]
        groups.append(dict(h2=h2s[gi], rank=rank, aff=affs[gi], slot0=slot0, x1=x1s[gi], row0=rows[gi][0],
                           blocks_per_batch=dims[gi][1] // TOK_BLOCK if rows[gi][1] else 1 << 30))
        scheds.append(_window_sched(cnt, slot0))
        slot0 += caps[gi]
    first, end, rounds = [jnp.concatenate([s[k] for s in scheds]) for k in range(3)]
    xe, ge = _compact(groups, first, end, rounds, slots)
    ye = _ffn(xe, ge, moe_w1[0], moe_w3[0], moe_w2[0], slots, ft=moe_w1.shape[-1])
    outs = _combine(ye, groups, first, rounds, mod, ln2_g[0].reshape(1, -1), ln2_b[0].reshape(1, -1), alpha)

    ckv_p, kpe_p, st_p = extras[0]
    y_prompt = outs[0].reshape(bp, tp, d)
    y_sample = outs[1].reshape(bs, tsq, d)
    new_ckv = ckv_p.reshape(bp, 1, tp, -1)
    new_kpe = kpe_p[:, :MLA_ROPE].reshape(bp, 1, tp, MLA_ROPE)
    new_state = st_p.reshape(bp, 1, 2, HG_HEADS, HG_DK, HG_DV)
    return (y_prompt, y_sample, new_ckv, new_kpe, new_state)
```

```python
import functools
import math

import jax
import jax.numpy as jnp
import numpy as np
from jax import lax
from jax.experimental import pallas as pl
from jax.experimental.pallas import tpu as pltpu

F32 = jnp.float32
BF16 = jnp.bfloat16

MLA_HEADS = 8
MLA_NOPE = 64
MLA_ROPE = 32
MLA_V = 64
HG_HEADS = 4
HG_DK = 128
HG_DV = 128
HG_CHUNK = 32
GRID_W = 64
ROPE_BASE = 10000.0
EC_FACTOR = 2
EPS = 1e-6

LANES = 128
SUBLANES = 8
BF16_ROWS = 16
VMEM_LIMIT = 56 * 1024 * 1024

TOK_BLOCK = 256
WIN_ROWS = 64
UNSELECTED = -(1 << 30)

NT_DIMS = (((1,), (1,)), ((), ()))


def _dot(a, b):
    return jnp.dot(a, b, preferred_element_type=F32)


def _dot_nt(a, b):
    return lax.dot_general(a, b, NT_DIMS, preferred_element_type=F32)


def _silu(x):
    return x * jax.nn.sigmoid(x)


def _params(*sem):
    return pltpu.CompilerParams(dimension_semantics=sem, vmem_limit_bytes=VMEM_LIMIT)


def _const_spec(shape):
    zeros = (0,) * len(shape)
    return pl.BlockSpec(shape, lambda *_: zeros, pipeline_mode=pl.Buffered(1))


def _adaln_kernel(c_ref, w_ref, b_ref, o_ref):
    s = _silu(c_ref[...]).astype(BF16)
    o_ref[...] = _dot(s, w_ref[...].astype(BF16)) + b_ref[...]


def _adaln(cond, w_ada, b_ada):
    rows, d = cond.shape
    n = w_ada.shape[1]
    tn = n // 4
    return pl.pallas_call(
        _adaln_kernel,
        out_shape=jax.ShapeDtypeStruct((rows, n), F32),
        grid=(n // tn,),
        in_specs=[_const_spec((rows, d)),
                  pl.BlockSpec((d, tn), lambda j: (0, j)),
                  pl.BlockSpec((1, tn), lambda j: (0, j))],
        out_specs=pl.BlockSpec((rows, tn), lambda j: (0, j)),
        compiler_params=_params("arbitrary"),
        name="adaln",
    )(cond, w_ada, b_ada.reshape(1, n))


def _rms(x, g):
    return x * lax.rsqrt(jnp.mean(x * x, axis=-1, keepdims=True) + EPS) * g


def _rope(x, c, s):
    w = x.shape[-1]
    lane = lax.broadcasted_iota(jnp.int32, x.shape, 1)
    nxt = pltpu.roll(x, w - 1, 1)
    prv = pltpu.roll(x, 1, 1)
    return x * c + jnp.where(lane % 2 == 0, nxt, prv) * s


N_INPROJ_WEIGHTS = 6


def _mod_row(mod_ref, row0, per_batch):
    r = row0 + pl.program_id(0) if per_batch else row0
    return mod_ref[pl.ds(r, 1), :]


def _modulated(x_ref, m):
    d = x_ref.shape[1]
    return (x_ref[...] * (1.0 + m[:, d:2 * d]) + m[:, 0:d]).astype(BF16)


def _keys(k_nope, kpe):
    shared = pltpu.roll(kpe, MLA_NOPE, 1)
    return (k_nope + jnp.concatenate([shared] * MLA_HEADS, axis=1)).astype(BF16)


def _inproj_kernel(*refs, row0, per_batch, rope):
    x_ref, mod_ref = refs[:2]
    _inproj_body(x_ref, _mod_row(mod_ref, row0, per_batch), *refs[2:], rope=rope)


def _inproj_body(x_ref, m, *refs, rope):
    gam_ref, win_ref, qn_ref, wuq_ref, kvn_ref, wk_ref, wv_ref = refs[:1 + N_INPROJ_WEIGHTS]
    refs = refs[1 + N_INPROJ_WEIGHTS:]
    if rope:
        cq_ref, sq_ref, ck_ref, sk_ref = refs[:4]
        refs = refs[4:]
    q_o, k_o, v_o, ckv_o, kpe_o, hgx_o = refs
    h = _modulated(x_ref, m)
    hw = HG_HEADS * HG_DK
    o_kv = qn_ref.shape[1]
    o_pe = o_kv + kvn_ref.shape[1]
    o_h = o_pe + MLA_ROPE

    cq = _rms(_dot_nt(h, win_ref[0:o_kv, :]), qn_ref[...])
    q = _dot(cq.astype(BF16), wuq_ref[...])
    if rope:
        q = _rope(q, jnp.concatenate([cq_ref[...]] * MLA_HEADS, axis=1),
                  jnp.concatenate([sq_ref[...]] * MLA_HEADS, axis=1))
    q_o[...] = q.astype(BF16)

    ckv = _rms(_dot_nt(h, win_ref[o_kv:o_pe, :]), kvn_ref[...])
    ckv_o[...] = ckv
    kpe = _dot_nt(h, win_ref[o_pe:o_h, :])
    kpe = jnp.concatenate([kpe, jnp.zeros((kpe.shape[0], LANES - MLA_ROPE), F32)], axis=1)
    if rope:
        kpe = _rope(kpe, ck_ref[...], sk_ref[...])
    kpe_o[...] = kpe
    cb = ckv.astype(BF16)
    k_o[...] = _keys(_dot(cb, wk_ref[...]), kpe)
    v_o[...] = _dot(cb, wv_ref[...]).astype(BF16)

    z = _dot_nt(h, win_ref[o_h:o_h + 5 * hw, :])
    hgx_o[:, 0:hw] = _silu(z[:, 0:hw])
    for dr in range(2):
        g0, g1 = gam_ref[dr, 0:1, :], gam_ref[dr, 1:2, :]
        gmax = jnp.maximum(g0, g1)
        e0, e1 = jnp.exp(g0 - gmax), jnp.exp(g1 - gmax)
        lb = e0 / (e0 + e1)
        f = lb + (1.0 - lb) * jax.nn.sigmoid(z[:, (1 + dr) * hw:(2 + dr) * hw])
        hgx_o[:, (1 + 2 * dr) * hw:(2 + 2 * dr) * hw] = jnp.log(f)
        hgx_o[:, (2 + 2 * dr) * hw:(3 + 2 * dr) * hw] = 1.0 - f
    hgx_o[:, 5 * hw:6 * hw] = z[:, 3 * hw:4 * hw]
    hgx_o[:, 6 * hw:7 * hw] = z[:, 4 * hw:5 * hw]


def _inproj(x2d, batch, seq, mod, gamma, wts, row0, per_batch, rope_tabs):
    n, d = x2d.shape
    tm = MIXER_SEQS * TOK_BLOCK if seq % (MIXER_SEQS * TOK_BLOCK) == 0 else TOK_BLOCK
    nblk = seq // tm
    rope = rope_tabs is not None
    hp = MLA_HEADS * LANES
    hw = HG_HEADS * HG_DK
    tok = lambda b, i: (b * nblk + i, 0)
    pos = lambda b, i: (i, 0)
    weights = [wts[k] for k in INPROJ_KEYS]
    ins = [x2d, mod, gamma] + weights
    in_specs = ([pl.BlockSpec((tm, d), tok), _const_spec(mod.shape), _const_spec(gamma.shape)]
                + [_const_spec(w.shape) for w in weights])
    if rope:
        ins += list(rope_tabs)
        in_specs += [pl.BlockSpec((tm, t.shape[1]), pos) for t in rope_tabs]
    widths = [(hp, BF16), (hp, BF16), (MLA_HEADS * MLA_V, BF16), (wts["kvn"].shape[1], F32), (LANES, F32),
              (7 * hw, F32)]
    return pl.pallas_call(
        functools.partial(_inproj_kernel, row0=row0, per_batch=per_batch, rope=rope),
        out_shape=[jax.ShapeDtypeStruct((n, w), dt) for w, dt in widths],
        grid=(batch, nblk),
        in_specs=in_specs,
        out_specs=[pl.BlockSpec((tm, w), tok) for w, _ in widths],
        compiler_params=_params("arbitrary", "arbitrary"),
        name="inproj",
    )(*ins)


def _kvup_kernel(ckv_ref, kpe_ref, wk_ref, wv_ref, k_o, v_o):
    cb = ckv_ref[...].astype(BF16)
    k_o[...] = _keys(_dot(cb, wk_ref[...]), kpe_ref[...])
    v_o[...] = _dot(cb, wv_ref[...]).astype(BF16)


def _kvup(ckv2d, kpe2d, wts):
    n = ckv2d.shape[0]
    tm = TOK_BLOCK
    widths = [MLA_HEADS * LANES, MLA_HEADS * MLA_V]
    row = lambda i: (i, 0)
    ws = [wts["wk"], wts["wv"]]
    return pl.pallas_call(
        _kvup_kernel,
        out_shape=[jax.ShapeDtypeStruct((n, w), BF16) for w in widths],
        grid=(n // tm,),
        in_specs=[pl.BlockSpec((tm, ckv2d.shape[1]), row), pl.BlockSpec((tm, LANES), row)]
                 + [_const_spec(w.shape) for w in ws],
        out_specs=[pl.BlockSpec((tm, w), row) for w in widths],
        compiler_params=_params("arbitrary"),
        name="kvup",
    )(ckv2d, kpe2d, *ws)


ATTN_SCALE = (MLA_NOPE + MLA_ROPE) ** -0.5


def _attn_body(q_ref, k_ref, v_ref, kc_ref, vc_ref, o_ref):
    cached = kc_ref is not None
    scale = ATTN_SCALE * math.log2(math.e)
    per_slab = LANES // MLA_V
    own = lax.broadcasted_iota(jnp.int32, (q_ref.shape[0], LANES), 1) // MLA_V
    for slab in range(MLA_HEADS // per_slab):
        vsl = slice(slab * LANES, (slab + 1) * LANES)
        out = None
        for sub in range(per_slab):
            hd = slab * per_slab + sub
            sl = slice(hd * LANES, (hd + 1) * LANES)
            q = q_ref[:, sl]
            s = _dot_nt(q, k_ref[:, sl])
            mx = jnp.max(s, axis=-1, keepdims=True)
            if cached:
                s2 = _dot_nt(q, kc_ref[:, sl])
                mx = jnp.maximum(mx, jnp.max(s2, axis=-1, keepdims=True))
            e = jnp.exp2((s - mx) * scale)
            den = jnp.sum(e, axis=-1, keepdims=True)
            o = _dot(e.astype(BF16), v_ref[:, vsl])
            if cached:
                e2 = jnp.exp2((s2 - mx) * scale)
                den = den + jnp.sum(e2, axis=-1, keepdims=True)
                o = o + _dot(e2.astype(BF16), vc_ref[:, vsl])
            o = o / den
            out = o if out is None else jnp.where(own == sub, o, out)
        o_ref[:, vsl] = out.astype(o_ref.dtype)


def _chunk_scan(x, reverse):
    tm = x.shape[0]
    rin = lax.broadcasted_iota(jnp.int32, x.shape, 0) % HG_CHUNK
    step = 1
    while step < HG_CHUNK:
        if reverse:
            x = x + jnp.where(rin < HG_CHUNK - step, pltpu.roll(x, tm - step, 0), 0.0)
        else:
            x = x + jnp.where(rin >= step, pltpu.roll(x, step, 0), 0.0)
        step *= 2
    return x


def _hgrn_kernel(*refs, has_init):
    fwd, bwd = refs[0:4], refs[4:8]
    refs = refs[8:]
    s0_ref = None
    if has_init:
        s0_ref = refs[0]
        refs = refs[1:]
    of_ref, ob_ref, sfin_ref, st_scr = refs
    i = pl.program_id(1)
    _hgrn_body([(fwd, bwd, of_ref, ob_ref, sfin_ref)], s0_ref, st_scr, i == 0, i == pl.num_programs(1) - 1)


def _hgrn_body(jobs, s0_ref, st_scr, first, last):
    tm = jobs[0][0][0].shape[0]
    c = HG_CHUNK
    nch = tm // c
    dk, dv = HG_DK, HG_DV
    hw = HG_HEADS * dk

    def initial(dr, hd):
        return s0_ref[0, dr, hd].T if s0_ref is not None else jnp.zeros((dv, dk), F32)

    if st_scr is not None:
        @pl.when(first)
        def _init():
            for dr in range(2):
                for hd in range(HG_HEADS):
                    st_scr[dr, hd] = initial(dr, hd)

    npair = nch // 2
    pair = 2 * c
    row = lax.broadcasted_iota(jnp.int32, (tm, tm), 0)
    col = lax.broadcasted_iota(jnp.int32, (tm, tm), 1)
    same = (row // c) == (col // c)
    same_pair = (row // pair) == (col // pair)
    bd = (lax.broadcasted_iota(jnp.int32, (tm, npair * dk), 0) // pair
          == lax.broadcasted_iota(jnp.int32, (tm, npair * dk), 1) // dk)
    chunk_odd = (lax.broadcasted_iota(jnp.int32, (tm, hw), 0) // c) % 2 == 1

    for dr in range(2):
        tri = same & ((col <= row) if dr == 0 else (col >= row))
        cross = same_pair & (((row // c) > (col // c)) if dr == 0 else ((row // c) < (col // c)))
        second = chunk_odd if dr == 0 else ~chunk_odd
        order = range(npair) if dr == 0 else range(npair - 1, -1, -1)

        def decayed(job):
            hq_ref, lf_ref, kk_ref, vv_ref = job[dr]
            bcum = _chunk_scan(lf_ref[...], reverse=dr == 1)
            closing = c - 1 if dr == 0 else 0
            btot3 = bcum.reshape(nch, c, hw)[:, closing:closing + 1, :]
            btot = jnp.broadcast_to(btot3, (nch, c, hw)).reshape(tm, hw)
            bpart = jnp.where(chunk_odd, pltpu.roll(btot, c, 0), pltpu.roll(btot, tm - c, 0))
            epart = jnp.exp(bpart)
            kk = kk_ref[...]
            qd = hq_ref[...] * jnp.exp(bcum)
            kd = kk * jnp.exp(-bcum)
            ke = kk * jnp.exp(btot - bcum)
            qd2 = jnp.where(second, qd * epart, qd)
            ke2 = jnp.where(second, ke, ke * epart)
            return qd, kd, ke, qd2, ke2, btot + bpart, vv_ref[...]

        def head(job, hd, qd, kd, ke, qd2, ke2, bpair, vv):
            o_ref, sfin_ref = job[2 + dr], job[4]
            sl = slice(hd * dk, (hd + 1) * dk)
            qd_h = qd[:, sl].astype(BF16)
            v_h = vv[:, hd * dv:(hd + 1) * dv]
            a = jnp.where(tri, _dot_nt(qd_h, kd[:, sl].astype(BF16)),
                          jnp.where(cross, _dot_nt(qd_h, ke[:, sl].astype(BF16)), 0.0))
            o_intra = _dot(a.astype(BF16), v_h.astype(BF16))
            kebd = jnp.where(bd, jnp.concatenate([ke2[:, sl]] * npair, axis=1), 0.0).astype(BF16)
            qbd = jnp.where(bd, jnp.concatenate([qd2[:, sl]] * npair, axis=1), 0.0).astype(BF16)
            ut = _dot(v_h.T.astype(BF16), kebd)
            st = st_scr[dr, hd] if st_scr is not None else initial(dr, hd)
            prev = [None] * npair
            for p in order:
                prev[p] = st
                st = st * jnp.exp(bpair[p * pair:p * pair + 1, sl]) + ut[:, p * dk:(p + 1) * dk]
            if st_scr is not None:
                st_scr[dr, hd] = st
            o_inter = _dot_nt(qbd, jnp.concatenate(prev, axis=1).astype(BF16))
            o_ref[:, hd * dv:(hd + 1) * dv] = o_intra + o_inter

            if last is True:
                sfin_ref[0, dr, hd] = st.T
            else:
                @pl.when(last)
                def _final():
                    sfin_ref[0, dr, hd] = st.T

        prepared = [decayed(job) for job in jobs]
        for hd in range(HG_HEADS):
            for job, arrays in zip(jobs, prepared):
                head(job, hd, *arrays)


def _hgrn(hgx, batch, seq, s0=None):
    n = hgx.shape[0]
    tm = TOK_BLOCK
    nblk = seq // tm
    hw = HG_HEADS * HG_DK

    def spec(lane_blk, rev):
        if rev:
            return pl.BlockSpec((tm, hw), lambda b, i: (b * nblk + nblk - 1 - i, lane_blk))
        return pl.BlockSpec((tm, hw), lambda b, i: (b * nblk + i, lane_blk))

    in_specs = [spec(0, False), spec(1, False), spec(2, False), spec(5, False),
                spec(0, True), spec(3, True), spec(4, True), spec(5, True)]
    ins = [hgx] * 8
    st_shape = (1, 2, HG_HEADS, HG_DK, HG_DV)
    st_spec = pl.BlockSpec(st_shape, lambda b, i: (b, 0, 0, 0, 0))
    if s0 is not None:
        ins.append(s0)
        in_specs.append(st_spec)
    return pl.pallas_call(
        functools.partial(_hgrn_kernel, has_init=s0 is not None),
        out_shape=[jax.ShapeDtypeStruct((n, hw), F32), jax.ShapeDtypeStruct((n, hw), F32),
                   jax.ShapeDtypeStruct((batch,) + st_shape[1:], F32)],
        grid=(batch, nblk),
        in_specs=in_specs,
        out_specs=[spec(0, False), spec(0, True), st_spec],
        scratch_shapes=[pltpu.VMEM((2, HG_HEADS, HG_DV, HG_DK), F32)],
        compiler_params=_params("arbitrary", "arbitrary"),
        name="hgrn",
    )(*ins)


def _layer_norm(x, g, b):
    xc = x - jnp.mean(x, axis=-1, keepdims=True)
    var = jnp.mean(xc * xc, axis=-1, keepdims=True)
    return xc * lax.rsqrt(var + EPS) * g + b


N_POSTMIX_WEIGHTS = 8
INPROJ_KEYS = ("win", "qn", "wuq", "kvn", "wk", "wv")
POSTMIX_KEYS = ("hgn", "womla", "wohg", "wout", "ln1g", "ln1b", "wr")
MIXER_SEQS = 2


def _postmix_kernel(x_ref, mod_ref, *refs, alpha, row0, per_batch, cached):
    n_attn = 5 if cached else 3
    q_ref, k_ref, v_ref = refs[:3]
    kc_ref, vc_ref = refs[3:5] if cached else (None, None)
    of_ref, ob_ref, zg_ref = refs[n_attn:n_attn + 3]
    om_s = refs[-1]
    _attn_body(q_ref, k_ref, v_ref, kc_ref, vc_ref, om_s)
    _postmix_body(x_ref, _mod_row(mod_ref, row0, per_batch), of_ref, ob_ref, zg_ref, om_s,
                  *refs[n_attn + 3:-1], alpha=alpha)


def _postmix_body(x_ref, m, of_ref, ob_ref, zg_ref, om_ref, wg_ref, hgn_ref, womla_ref,
                  wohg_ref, wout_ref, lng_ref, lnb_ref, wr_ref, x1_o, h2_o, aff_o, *, alpha):
    d = x_ref.shape[1]
    tb = aff_o.shape[2]
    g1, sh2, sc2 = m[:, 2 * d:3 * d], m[:, 3 * d:4 * d], m[:, 4 * d:5 * d]
    o = of_ref[...] + ob_ref[...]
    zg = zg_ref[...]
    parts = []
    for hd in range(HG_HEADS):
        sl = slice(hd * HG_DV, (hd + 1) * HG_DV)
        parts.append(_rms(o[:, sl], hgn_ref[...]) * _silu(zg[:, sl]))
    ohg = jnp.concatenate(parts, axis=1).astype(BF16)
    gates = _dot_nt(_modulated(x_ref, m), wg_ref[wg_ref.shape[0] - 2 * d:, :])
    merged = (jax.nn.sigmoid(gates[:, 0:d]) * _dot(om_ref[...], womla_ref[...])
              + jax.nn.sigmoid(gates[:, d:2 * d]) * _dot(ohg, wohg_ref[...]))
    mix = _dot(merged.astype(BF16), wout_ref[...])
    x1 = _layer_norm(alpha * x_ref[...] + g1 * mix, lng_ref[...], lnb_ref[...])
    x1_o[...] = x1
    h2 = (x1 * (1.0 + sc2) + sh2).astype(BF16)
    h2_o[...] = h2
    logits = _dot_nt(wr_ref[...], h2)
    e = jnp.exp(logits - jnp.max(logits, axis=0, keepdims=True))
    aff = e / jnp.sum(e, axis=0, keepdims=True)
    for blk in range(aff_o.shape[0]):
        aff_o[blk] = aff[:, blk * tb:(blk + 1) * tb]


def _postmix(x2d, batch, seq, mod, q, k, v, cache, o_f, o_b, hgx, wts, alpha, n_experts, row0, per_batch):
    n, d = x2d.shape
    tm = TOK_BLOCK
    nblk = seq // tm
    hw = HG_HEADS * HG_DV
    hp, hv = q.shape[1], v.shape[1]
    tok = lambda b, i: (b * nblk + i, 0)
    per_seq = lambda b, i: (b, 0)
    weights = [wts[k] for k in POSTMIX_KEYS]
    attn_ins = [q, k, v]
    attn_specs = [pl.BlockSpec((tm, hp), tok), pl.BlockSpec((seq, hp), per_seq), pl.BlockSpec((seq, hv), per_seq)]
    if cache is not None:
        past = cache[0].shape[0] // batch
        attn_ins += list(cache)
        attn_specs += [pl.BlockSpec((past, hp), per_seq), pl.BlockSpec((past, hv), per_seq)]
    return pl.pallas_call(
        functools.partial(_postmix_kernel, alpha=alpha, row0=row0, per_batch=per_batch,
                          cached=cache is not None),
        out_shape=[jax.ShapeDtypeStruct((n, d), F32), jax.ShapeDtypeStruct((n, d), BF16),
                   jax.ShapeDtypeStruct((n // tm, n_experts, tm), F32)],
        grid=(batch, nblk),
        in_specs=[pl.BlockSpec((tm, d), tok), _const_spec(mod.shape)] + attn_specs
                 + [pl.BlockSpec((tm, hw), tok), pl.BlockSpec((tm, hw), tok),
                    pl.BlockSpec((tm, hw), lambda b, i: (b * nblk + i, 6)), _const_spec(wts["win"].shape)]
                 + [_const_spec(w.shape) for w in weights],
        out_specs=[pl.BlockSpec((tm, d), tok), pl.BlockSpec((tm, d), tok),
                   pl.BlockSpec((1, n_experts, tm), lambda b, i: (b * nblk + i, 0, 0))],
        scratch_shapes=[pltpu.VMEM((tm, hv), BF16)],
        compiler_params=_params("arbitrary", "arbitrary"),
        name="postmix",
    )(x2d, mod, *attn_ins, o_f, o_b, hgx, wts["win"], *weights)


def _mixer_kernel(x_ref, mod_ref, *refs, alpha, row0, seq):
    nw = 1 + N_INPROJ_WEIGHTS
    in_w, refs = refs[:nw], refs[nw:]
    pm_w, refs = refs[:N_POSTMIX_WEIGHTS - 1], refs[N_POSTMIX_WEIGHTS - 1:]
    pm_w = (in_w[1],) + tuple(pm_w)
    x1_o, h2_o, aff_o, ckv_o, kpe_o, sfin_o, q_s, k_s, v_s, hgx_s, om_s, of_s, ob_s = refs
    m = _mod_row(mod_ref, row0, False)
    _inproj_body(x_ref, m, *in_w, q_s, k_s, v_s, ckv_o, kpe_o, hgx_s, rope=False)
    hw = HG_HEADS * HG_DK
    jobs = []
    for s in range(x_ref.shape[0] // seq):
        rows = slice(s * seq, (s + 1) * seq)
        _attn_body(q_s.at[rows], k_s.at[rows], v_s.at[rows], None, None, om_s.at[rows])
        lane = lambda j, rows=rows: hgx_s.at[rows, j * hw:(j + 1) * hw]
        jobs.append(((lane(0), lane(1), lane(2), lane(5)), (lane(0), lane(3), lane(4), lane(5)),
                     of_s.at[rows], ob_s.at[rows], sfin_o.at[s:s + 1]))
    _hgrn_body(jobs, None, None, True, True)
    _postmix_body(x_ref, m, of_s, ob_s, hgx_s.at[:, 6 * hw:7 * hw], om_s, *pm_w, x1_o, h2_o, aff_o,
                  alpha=alpha)


def _mixer(x2d, batch, seq, mod, gamma, wts, alpha, n_experts, row0):
    n, d = x2d.shape
    assert seq == TOK_BLOCK
    ns = MIXER_SEQS if batch % MIXER_SEQS == 0 else 1
    tm = ns * seq
    hp = MLA_HEADS * LANES
    hv = MLA_HEADS * MLA_V
    hw = HG_HEADS * HG_DK
    kvl = wts["kvn"].shape[1]
    weights = [wts[k] for k in INPROJ_KEYS + POSTMIX_KEYS]
    tok = lambda b: (b, 0)
    st_shape = (ns, 2, HG_HEADS, HG_DK, HG_DV)
    return pl.pallas_call(
        functools.partial(_mixer_kernel, alpha=alpha, row0=row0, seq=seq),
        out_shape=[jax.ShapeDtypeStruct((n, d), F32), jax.ShapeDtypeStruct((n, d), BF16),
                   jax.ShapeDtypeStruct((n // seq, n_experts, seq), F32),
                   jax.ShapeDtypeStruct((n, kvl), F32), jax.ShapeDtypeStruct((n, LANES), F32),
                   jax.ShapeDtypeStruct((batch,) + st_shape[1:], F32)],
        grid=(batch // ns,),
        in_specs=[pl.BlockSpec((tm, d), tok), _const_spec(mod.shape), _const_spec(gamma.shape)]
                 + [_const_spec(w.shape) for w in weights],
        out_specs=[pl.BlockSpec((tm, d), tok), pl.BlockSpec((tm, d), tok),
                   pl.BlockSpec((ns, n_experts, seq), lambda b: (b, 0, 0)),
                   pl.BlockSpec((tm, kvl), tok), pl.BlockSpec((tm, LANES), tok),
                   pl.BlockSpec(st_shape, lambda b: (b, 0, 0, 0, 0))],
        scratch_shapes=[pltpu.VMEM((tm, hp), BF16), pltpu.VMEM((tm, hp), BF16), pltpu.VMEM((tm, hv), BF16),
                        pltpu.VMEM((tm, 7 * hw), F32), pltpu.VMEM((tm, hv), BF16),
                        pltpu.VMEM((tm, hw), F32), pltpu.VMEM((tm, hw), F32)],
        compiler_params=_params("arbitrary"),
        name="mixer",
    )(x2d, mod, gamma, *weights)


def _route_kernel(*refs, caps):
    ng = len(caps)
    for aff_ref, rank_o, cnt_o, cap in zip(refs[0:ng], refs[ng:2 * ng], refs[2 * ng:3 * ng], caps):
        _route_group(aff_ref, rank_o, cnt_o, cap)


def _route_group(aff_ref, rank_o, cnt_o, cap):
    nb, ne, tb = aff_ref.shape
    key = aff_ref[...]

    def count(mask):
        return jnp.sum(jnp.sum(jnp.where(mask, 1.0, 0.0), axis=0), axis=1, keepdims=True)

    def bit_step(it, bits):
        cand = bits | jnp.left_shift(jnp.int32(1), 30 - it)
        return jnp.where(count(key >= pltpu.bitcast(cand, F32)[None]) >= cap, cand, bits)

    bits = lax.fori_loop(0, 31, bit_step, jnp.zeros((ne, 1), jnp.int32))
    thr = pltpu.bitcast(bits, F32)
    need = cap - count(key > thr[None])
    before = (lax.broadcasted_iota(jnp.int32, (tb, tb), 0)
              < lax.broadcasted_iota(jnp.int32, (tb, tb), 1))
    before = jnp.where(before, 1.0, 0.0).astype(BF16)
    off_eq = jnp.zeros((ne, 1), F32)
    off_sel = jnp.zeros((ne, 1), F32)
    cnt_o[...] = jnp.zeros_like(cnt_o)
    for blk in range(nb):
        key_b = key[blk]
        eq = key_b == thr
        eq_b = jnp.where(eq, 1.0, 0.0)
        eq_rank = _dot(eq_b.astype(BF16), before) + off_eq
        sel = (key_b > thr) | (eq & (eq_rank < need))
        sel_b = jnp.where(sel, 1.0, 0.0)
        rank = _dot(sel_b.astype(BF16), before) + off_sel
        rank_o[blk] = jnp.where(sel, rank.astype(jnp.int32), UNSELECTED)
        cnt_o[:, blk:blk + 1] = off_sel.astype(jnp.int32)
        off_eq = off_eq + jnp.sum(eq_b, axis=1, keepdims=True)
        off_sel = off_sel + jnp.sum(sel_b, axis=1, keepdims=True)
    cnt_o[:, nb:nb + 1] = off_sel.astype(jnp.int32)


def _route(affs, caps):
    ng = len(affs)
    ne = affs[0].shape[1]
    assert all(a.shape[0] + 1 <= LANES for a in affs)
    outs = pl.pallas_call(
        functools.partial(_route_kernel, caps=tuple(caps)),
        out_shape=[jax.ShapeDtypeStruct(a.shape, jnp.int32) for a in affs]
                  + [jax.ShapeDtypeStruct((ne, LANES), jnp.int32)] * ng,
        in_specs=[pl.BlockSpec(memory_space=pltpu.VMEM)] * ng,
        out_specs=[pl.BlockSpec(memory_space=pltpu.VMEM)] * (2 * ng),
        compiler_params=pltpu.CompilerParams(vmem_limit_bytes=VMEM_LIMIT),
        name="route",
    )(*affs)
    return [(outs[gi], outs[ng + gi][:, :affs[gi].shape[0] + 1]) for gi in range(ng)]


def _window_hits(rk_ref, firsts, slot0, win):
    ne, tb = rk_ref.shape[1], rk_ref.shape[2]
    win_iota = lax.broadcasted_iota(jnp.int32, (win, tb), 0)
    return [(rk_ref[0, e:e + 1, :] + (slot0 - firsts[e])) == win_iota for e in range(ne)]


def _compact_kernel(first_ref, end_ref, rounds_ref, *refs, groups, slots):
    ng = len(groups)
    h2_refs, rk_refs, af_refs = refs[0:ng], refs[ng:2 * ng], refs[2 * ng:3 * ng]
    hbms, refs = refs[3 * ng:3 * ng + 2], refs[3 * ng + 2:]
    stages, tails = refs[0:2], refs[2:4]
    sem, issued = refs[4:]
    b = pl.program_id(0)
    ne = rk_refs[0].shape[1]
    win = WIN_ROWS
    sub = BF16_ROWS

    def copies(slot, dsts):
        return [pltpu.make_async_copy(stage.at[slot, pl.ds(e * win, win), :],
                                      hbm.at[e, pl.ds(pl.multiple_of(dsts[e], sub), win), :], sem.at[c, e])
                for c, (stage, hbm) in enumerate(zip(stages, hbms)) for e in range(ne)]

    def wait_previous():
        @pl.when(issued[0] > 0)
        def _():
            for cp in copies(0, [0] * ne):
                cp.wait()

    @pl.when(b == 0)
    def _init():
        issued[0] = 0
        for stage, tail in zip(stages, tails):
            tail[...] = jnp.zeros_like(tail)
            stage[1] = jnp.zeros(stage.shape[1:], stage.dtype)
        pad = copies(1, [slots] * ne)
        for cp in pad:
            cp.start()
        for cp in pad:
            cp.wait()

    def group_body(h2_ref, rk_ref, af_ref, slot0):
        firsts = [first_ref[b * ne + e] for e in range(ne)]
        bases = [(f // sub) * sub for f in firsts]
        ends = [end_ref[b * ne + e] - bases[e] for e in range(ne)]

        def one_round(r, carry):
            dsts = [bases[e] + r * win for e in range(ne)]
            hits = _window_hits(rk_ref, dsts, slot0, win)
            onehot = jnp.where(jnp.concatenate(hits, axis=0), 1.0, 0.0).astype(BF16)
            gate = jnp.concatenate(
                [jnp.sum(jnp.where(hits[e], af_ref[0, e:e + 1, :], 0.0), axis=1, keepdims=True)
                 for e in range(ne)], axis=0)
            slot = issued[0] % 2
            streams = (_dot(onehot, h2_ref[...]).astype(BF16),
                       jnp.broadcast_to(gate, (ne * win, LANES)))
            for val, stage, tail in zip(streams, stages, tails):
                sub_iota = lax.broadcasted_iota(jnp.int32, (sub, val.shape[1]), 0)
                pieces = []
                for e in range(ne):
                    old = tail[e * sub:(e + 1) * sub, :]
                    shared = jnp.where(r == 0, firsts[e] - bases[e], 0)
                    groups_e = [val[e * win + g * sub:e * win + (g + 1) * sub, :] for g in range(win // sub)]
                    groups_e[0] = jnp.where(sub_iota < shared, old, groups_e[0])
                    pieces += groups_e
                    last = (ends[e] // sub) * sub
                    new = old
                    for g, grp in enumerate(groups_e):
                        new = jnp.where((r == last // win) & (last % win == g * sub), grp, new)
                    tail[e * sub:(e + 1) * sub, :] = new
                stage[slot] = jnp.concatenate(pieces, axis=0)
            wait_previous()
            for cp in copies(slot, [jnp.minimum(dst, slots) for dst in dsts]):
                cp.start()
            issued[0] = issued[0] + 1
            return carry

        lax.fori_loop(0, rounds_ref[b], one_round, 0)

    blk0 = 0
    for gi, g in enumerate(groups):
        @pl.when((b >= blk0) & (b < blk0 + g["nb"]))
        def _(gi=gi, g=g):
            group_body(h2_refs[gi], rk_refs[gi], af_refs[gi], g["slot0"])
        blk0 += g["nb"]

    @pl.when(b == pl.num_programs(0) - 1)
    def _drain():
        wait_previous()


def _compact(groups, first, end, rounds, slots):
    d = groups[0]["h2"].shape[1]
    nbs = [g["rank"].shape[0] for g in groups]
    ne, tb = groups[0]["rank"].shape[1:]
    meta, specs_h2, specs_rk = [], [], []
    blk0 = 0
    for g, nb in zip(groups, nbs):
        meta.append(dict(nb=nb, slot0=g["slot0"]))
        local = lambda b, *_, blk0=blk0, nb=nb: jnp.clip(b - blk0, 0, nb - 1)
        specs_h2.append(pl.BlockSpec((tb, d), lambda b, *_, local=local: (local(b), 0)))
        specs_rk.append(pl.BlockSpec((1, ne, tb), lambda b, *_, local=local: (local(b), 0, 0)))
        blk0 += nb
    streams = [(d, BF16), (LANES, F32)]
    return pl.pallas_call(
        functools.partial(_compact_kernel, groups=meta, slots=slots),
        out_shape=[jax.ShapeDtypeStruct((ne, slots + WIN_ROWS, w), dt) for w, dt in streams],
        grid_spec=pltpu.PrefetchScalarGridSpec(
            num_scalar_prefetch=3,
            grid=(sum(nbs),),
            in_specs=specs_h2 + specs_rk + specs_rk,
            out_specs=[pl.BlockSpec(memory_space=pl.ANY)] * 2,
            scratch_shapes=[pltpu.VMEM((2, ne * WIN_ROWS, w), dt) for w, dt in streams]
                           + [pltpu.VMEM((ne * BF16_ROWS, w), dt) for w, dt in streams]
                           + [pltpu.SemaphoreType.DMA((2, ne)), pltpu.SMEM((1,), jnp.int32)]),
        compiler_params=_params("arbitrary"),
        name="compact",
    )(first, end, rounds, *[g["h2"] for g in groups], *[g["rank"] for g in groups],
      *[g["aff"] for g in groups])


def _ffn_kernel(xe_ref, ge_ref, w1_ref, w3_ref, w2_ref, ye_ref, *scratch):
    f = pl.program_id(1)
    x = xe_ref[0]
    hid = _silu(_dot(x, w1_ref[0].astype(BF16))) * _dot(x, w3_ref[0].astype(BF16))
    y = _dot(hid.astype(BF16), w2_ref[0].astype(BF16))
    if not scratch:
        ye_ref[0] = (y * ge_ref[0, :, 0:1]).astype(ye_ref.dtype)
        return
    acc_scr, = scratch
    last = pl.num_programs(1) - 1

    @pl.when(f == 0)
    def _first():
        acc_scr[...] = y

    @pl.when((f > 0) & (f < last))
    def _middle():
        acc_scr[...] += y

    @pl.when(f == last)
    def _last():
        ye_ref[0] = ((acc_scr[...] + y) * ge_ref[0, :, 0:1]).astype(ye_ref.dtype)


def _ffn(xe, ge, w1, w3, w2, slots, ft):
    ne, d, dff = w1.shape
    nf = dff // ft
    return pl.pallas_call(
        _ffn_kernel,
        out_shape=jax.ShapeDtypeStruct((ne, slots, d), BF16),
        grid=(ne, nf),
        in_specs=[pl.BlockSpec((1, slots, d), lambda e, f: (e, 0, 0)),
                  pl.BlockSpec((1, slots, ge.shape[2]), lambda e, f: (e, 0, 0)),
                  pl.BlockSpec((1, d, ft), lambda e, f: (e, 0, f)),
                  pl.BlockSpec((1, d, ft), lambda e, f: (e, 0, f)),
                  pl.BlockSpec((1, ft, d), lambda e, f: (e, f, 0))],
        out_specs=pl.BlockSpec((1, slots, d), lambda e, f: (e, 0, 0)),
        scratch_shapes=[pltpu.VMEM((slots, d), F32)] if nf > 1 else [],
        compiler_params=_params("arbitrary", "arbitrary"),
        name="ffn",
    )(xe, ge, w1, w3, w2)


def _combine_kernel(first_ref, rounds_ref, *refs, groups, alpha, slots):
    ng = len(groups)
    rk_refs, x1_refs = refs[0:ng], refs[ng:2 * ng]
    mod_ref, lng_ref, lnb_ref, ye_hbm = refs[2 * ng:2 * ng + 4]
    out_refs = refs[2 * ng + 4:3 * ng + 4]
    buf, acc_scr, sem = refs[3 * ng + 4:]
    d = x1_refs[0].shape[1]
    b = pl.program_id(0)
    nblk = pl.num_programs(0)
    ne, tb = rk_refs[0].shape[1], rk_refs[0].shape[2]
    win = WIN_ROWS

    def starts_of(blk, r):
        firsts = [(first_ref[blk * ne + e] // BF16_ROWS) * BF16_ROWS + r * win for e in range(ne)]
        return firsts, [jnp.minimum(f, slots - win) for f in firsts]

    def windows(slot, starts):
        return [pltpu.make_async_copy(ye_hbm.at[e, pl.ds(pl.multiple_of(starts[e], BF16_ROWS), win), :],
                                      buf.at[slot, pl.ds(e * win, win), :], sem.at[slot, e])
                for e in range(ne)]

    def scatter(rk_ref, slot0, slot, firsts, starts, later_round):
        hits = _window_hits(rk_ref, starts, slot0, win)
        if later_round:
            hits = [h & ((rk_ref[0, e:e + 1, :] + slot0) >= firsts[e]) for e, h in enumerate(hits)]
        hit = jnp.where(jnp.concatenate(hits, axis=0), 1.0, 0.0)
        return _dot(hit.T.astype(BF16), buf[slot])

    cur = b % 2

    @pl.when(b == 0)
    def _prime():
        for cp in windows(0, starts_of(0, 0)[1]):
            cp.start()

    @pl.when(b + 1 < nblk)
    def _prefetch():
        for cp in windows(1 - cur, starts_of(b + 1, 0)[1]):
            cp.start()

    firsts0, starts0 = starts_of(b, 0)
    for cp in windows(cur, starts0):
        cp.wait()

    def group_body(rk_ref, x1_ref, out_ref, g, local):
        acc_scr[...] = scatter(rk_ref, g["slot0"], cur, firsts0, starts0, False)

        def extra_round(r, carry):
            firsts, starts = starts_of(b, r)
            for cp in windows(2, starts):
                cp.start()
            for cp in windows(2, starts):
                cp.wait()
            acc_scr[...] += scatter(rk_ref, g["slot0"], 2, firsts, starts, True)
            return carry

        lax.fori_loop(1, rounds_ref[b], extra_round, 0)
        r = g["row0"] + local // g["blocks_per_batch"]
        g2 = mod_ref[pl.ds(r, 1), :][:, 5 * d:6 * d]
        out_ref[...] = _layer_norm(alpha * x1_ref[...] + g2 * acc_scr[...], lng_ref[...], lnb_ref[...])

    blk0 = 0
    for gi, g in enumerate(groups):
        @pl.when((b >= blk0) & (b < blk0 + g["nb"]))
        def _(gi=gi, g=g, blk0=blk0):
            group_body(rk_refs[gi], x1_refs[gi], out_refs[gi], g, b - blk0)
        blk0 += g["nb"]


def _combine(ye, groups, first, rounds, mod, ln_g, ln_b, alpha):
    d = groups[0]["x1"].shape[1]
    ne, tb = groups[0]["rank"].shape[1:]
    slots = ye.shape[1]
    meta, specs_rk, specs_x1 = [], [], []
    blk0 = 0
    for g in groups:
        nb = g["rank"].shape[0]
        meta.append(dict(nb=nb, slot0=g["slot0"], row0=g["row0"], blocks_per_batch=g["blocks_per_batch"]))
        local = lambda b, *_, blk0=blk0, nb=nb: jnp.clip(b - blk0, 0, nb - 1)
        specs_rk.append(pl.BlockSpec((1, ne, tb), lambda b, *_, local=local: (local(b), 0, 0)))
        specs_x1.append(pl.BlockSpec((tb, d), lambda b, *_, local=local: (local(b), 0)))
        blk0 += nb
    const = lambda shape: pl.BlockSpec(shape, lambda b, *_: (0,) * len(shape))
    return pl.pallas_call(
        functools.partial(_combine_kernel, groups=meta, alpha=alpha, slots=slots),
        out_shape=[jax.ShapeDtypeStruct(g["x1"].shape, F32) for g in groups],
        grid_spec=pltpu.PrefetchScalarGridSpec(
            num_scalar_prefetch=2,
            grid=(blk0,),
            in_specs=specs_rk + specs_x1 + [const(mod.shape), const((1, d)), const((1, d)),
                                            pl.BlockSpec(memory_space=pl.ANY)],
            out_specs=specs_x1,
            scratch_shapes=[pltpu.VMEM((3, ne * WIN_ROWS, d), ye.dtype), pltpu.VMEM((tb, d), F32),
                            pltpu.SemaphoreType.DMA((3, ne))]),
        compiler_params=_params("arbitrary"),
        name="combine",
    )(first, rounds, *[g["rank"] for g in groups], *[g["x1"] for g in groups], mod, ln_g, ln_b, ye)


def _prep_weights(w_in, q_norm, w_uq, kv_norm, w_ukv, w_o_mla, hgrn_norm, w_o_hg, w_out, ln1_g, ln1_b,
                  w_router):
    d = w_in.shape[0]
    q_lora, kv_lora = q_norm.shape[0], kv_norm.shape[0]
    hw = HG_HEADS * HG_DK
    hh, hp = MLA_HEADS, MLA_HEADS * LANES
    o_kv, o_pe = q_lora, q_lora + kv_lora
    o_h = o_pe + MLA_ROPE
    o_g = o_h + 5 * hw
    assert w_in.shape[1] == o_g + 2 * d
    qk = MLA_NOPE + MLA_ROPE
    kvw = MLA_NOPE + MLA_V
    b16 = lambda a: a.astype(BF16)
    assert all(o % BF16_ROWS == 0 for o in (o_kv, o_pe, o_h, o_g))
    win = b16(w_in.T)
    wuq = jnp.pad(w_uq.reshape(q_lora, hh, qk), ((0, 0), (0, 0), (0, LANES - qk))).reshape(q_lora, hp)
    ukv = w_ukv.reshape(kv_lora, hh, kvw)
    wk = jnp.pad(ukv[:, :, :MLA_NOPE], ((0, 0), (0, 0), (0, LANES - MLA_NOPE))).reshape(kv_lora, hp)
    wv = ukv[:, :, MLA_NOPE:].reshape(kv_lora, hh * MLA_V)
    return dict(
        win=win, qn=q_norm.reshape(1, -1), wuq=b16(wuq), kvn=kv_norm.reshape(1, -1), wk=b16(wk), wv=b16(wv),
        hgn=hgrn_norm.reshape(1, -1), womla=b16(w_o_mla), wohg=b16(w_o_hg), wout=b16(w_out),
        ln1g=ln1_g.reshape(1, -1), ln1b=ln1_b.reshape(1, -1), wr=b16(w_router.T))


def _rope_tables(seq):
    n_freq = MLA_ROPE // 4
    inv = ROPE_BASE ** (-np.arange(n_freq, dtype=np.float64) / n_freq)
    t = np.arange(seq)
    ang = np.concatenate([(t // GRID_W)[:, None] * inv, (t % GRID_W)[:, None] * inv], axis=-1)
    cos = np.repeat(np.cos(ang), 2, axis=1)
    sin = np.repeat(np.sin(ang), 2, axis=1) * np.tile([-1.0, 1.0], MLA_ROPE // 2)
    ck = np.pad(cos, ((0, 0), (0, LANES - MLA_ROPE)), constant_values=1.0)
    sk = np.pad(sin, ((0, 0), (0, LANES - MLA_ROPE)))
    cq = np.pad(cos, ((0, 0), (MLA_NOPE, LANES - MLA_NOPE - MLA_ROPE)), constant_values=1.0)
    sq = np.pad(sin, ((0, 0), (MLA_NOPE, LANES - MLA_NOPE - MLA_ROPE)))
    return tuple(jnp.asarray(a, F32) for a in (cq, sq, ck, sk))


def _window_sched(cnt, slot0):
    first = slot0 + cnt[:, :-1]
    end = slot0 + cnt[:, 1:]
    flat = lambda a: a.T.reshape(-1).astype(jnp.int32)
    rounds = jnp.max((end - (first // BF16_ROWS) * BF16_ROWS + WIN_ROWS - 1) // WIN_ROWS, axis=0)
    return flat(first), flat(end), jnp.maximum(rounds, 1).astype(jnp.int32)


def kernel(x_prompt, x_sample, c, cache_ckv, cache_kpe, state_hgrn, c_ctx, w_ada, b_ada, w_in, mla_q_norm, mla_w_uq, mla_kv_norm, mla_w_ukv, mla_w_o, hgrn_gamma, hgrn_norm, hgrn_w_o, w_out, ln1_g, ln1_b, moe_w_router, moe_w1, moe_w3, moe_w2, ln2_g, ln2_b):
    depth = w_ada.shape[0]
    assert depth == 1, "single trunk layer"
    bp, tp, d = x_prompt.shape
    bs, tsq, _ = x_sample.shape
    ne = moe_w_router.shape[-1]
    alpha = (2 * depth) ** 0.25
    past = cache_ckv.shape[2]
    assert tp % TOK_BLOCK == 0 and tsq % TOK_BLOCK == 0 and past % TOK_BLOCK == 0 and tsq % GRID_W == 0

    wts = _prep_weights(w_in[0], mla_q_norm[0], mla_w_uq[0], mla_kv_norm[0], mla_w_ukv[0], mla_w_o[0],
                        hgrn_norm[0], hgrn_w_o[0], w_out[0], ln1_g[0], ln1_b[0], moe_w_router[0])
    cond_rows = -(-(1 + bs) // SUBLANES) * SUBLANES
    cond = jnp.concatenate([c_ctx[None], c, jnp.zeros((cond_rows - 1 - bs, d), F32)], axis=0)
    mod = _adaln(cond, w_ada[0], b_ada[0])

    xs = [x_prompt.reshape(bp * tp, d), x_sample.reshape(bs * tsq, d)]
    dims = [(bp, tp), (bs, tsq)]
    rows = [(0, False), (1, True)]
    ropes = [None, _rope_tables(tsq)]
    kpe_c = jnp.pad(cache_kpe[:, 0].reshape(bs * past, MLA_ROPE), ((0, 0), (0, LANES - MLA_ROPE)))
    caches = [None, _kvup(cache_ckv[:, 0].reshape(bs * past, -1), kpe_c, wts)]
    inits = [None, state_hgrn[:, 0]]

    x1s, h2s, affs, extras = [], [], [], []
    for gi in range(2):
        (bt, sq), (row0, per_batch) = dims[gi], rows[gi]
        if sq == TOK_BLOCK and caches[gi] is None and ropes[gi] is None and not per_batch:
            x1, h2, aff, ckv, kpe, s_fin = _mixer(xs[gi], bt, sq, mod, hgrn_gamma, wts, alpha, ne, row0)
        else:
            q, k, v, ckv, kpe, hgx = _inproj(xs[gi], bt, sq, mod, hgrn_gamma, wts, row0, per_batch, ropes[gi])
            o_f, o_b, s_fin = _hgrn(hgx, bt, sq, inits[gi])
            x1, h2, aff = _postmix(xs[gi], bt, sq, mod, q, k, v, caches[gi], o_f, o_b, hgx, wts, alpha, ne,
                                   row0, per_batch)
        x1s.append(x1)
        h2s.append(h2)
        affs.append(aff)
        extras.append((ckv, kpe, s_fin))

    caps = [EC_FACTOR * x.shape[0] // ne for x in xs]
    slots = sum(caps)
    assert all(cp % BF16_ROWS == 0 for cp in caps) and slots >= WIN_ROWS
    groups, scheds = [], []
    slot0 = 0
    routed = _route(affs, caps)
    for gi in range(2):
        rank, cnt = routed[gi]
        groups.append(dict(h2=h2s[gi], rank=rank, aff=affs[gi], slot0=slot0, x1=x1s[gi], row0=rows[gi][0],
                           blocks_per_batch=dims[gi][1] // TOK_BLOCK if rows[gi][1] else 1 << 30))
        scheds.append(_window_sched(cnt, slot0))
        slot0 += caps[gi]
    first, end, rounds = [jnp.concatenate([s[k] for s in scheds]) for k in range(3)]
    xe, ge = _compact(groups, first, end, rounds, slots)
    ye = _ffn(xe, ge, moe_w1[0], moe_w3[0], moe_w2[0], slots, ft=moe_w1.shape[-1])
    outs = _combine(ye, groups, first, rounds, mod, ln2_g[0].reshape(1, -1), ln2_b[0].reshape(1, -1), alpha)

    ckv_p, kpe_p, st_p = extras[0]
    y_prompt = outs[0].reshape(bp, tp, d)
    y_sample = outs[1].reshape(bs, tsq, d)
    new_ckv = ckv_p.reshape(bp, 1, tp, -1)
    new_kpe = kpe_p[:, :MLA_ROPE].reshape(bp, 1, tp, MLA_ROPE)
    new_state = st_p.reshape(bp, 1, 2, HG_HEADS, HG_DK, HG_DV)
    return (y_prompt, y_sample, new_ckv, new_kpe, new_state)
```

```python
import functools
import math

import jax
import jax.numpy as jnp
import numpy as np
from jax import lax
from jax.experimental import pallas as pl
from jax.experimental.pallas import tpu as pltpu

F32 = jnp.float32
BF16 = jnp.bfloat16

MLA_HEADS = 8
MLA_NOPE = 64
MLA_ROPE = 32
MLA_V = 64
HG_HEADS = 4
HG_DK = 128
HG_DV = 128
HG_CHUNK = 32
GRID_W = 64
ROPE_BASE = 10000.0
EC_FACTOR = 2
EPS = 1e-6

LANES = 128
SUBLANES = 8
BF16_ROWS = 16
VMEM_LIMIT = 56 * 1024 * 1024

TOK_BLOCK = 256
WIN_ROWS = 64
UNSELECTED = -(1 << 30)

NT_DIMS = (((1,), (1,)), ((), ()))


def _dot(a, b):
    return jnp.dot(a, b, preferred_element_type=F32)


def _dot_nt(a, b):
    return lax.dot_general(a, b, NT_DIMS, preferred_element_type=F32)


def _silu(x):
    return x * jax.nn.sigmoid(x)


def _params(*sem):
    return pltpu.CompilerParams(dimension_semantics=sem, vmem_limit_bytes=VMEM_LIMIT)


def _const_spec(shape):
    zeros = (0,) * len(shape)
    return pl.BlockSpec(shape, lambda *_: zeros, pipeline_mode=pl.Buffered(1))


def _adaln_kernel(c_ref, w_ref, b_ref, o_ref):
    s = _silu(c_ref[...]).astype(BF16)
    o_ref[...] = _dot(s, w_ref[...].astype(BF16)) + b_ref[...]


def _adaln(cond, w_ada, b_ada):
    rows, d = cond.shape
    n = w_ada.shape[1]
    tn = n // 4
    return pl.pallas_call(
        _adaln_kernel,
        out_shape=jax.ShapeDtypeStruct((rows, n), F32),
        grid=(n // tn,),
        in_specs=[_const_spec((rows, d)),
                  pl.BlockSpec((d, tn), lambda j: (0, j)),
                  pl.BlockSpec((1, tn), lambda j: (0, j))],
        out_specs=pl.BlockSpec((rows, tn), lambda j: (0, j)),
        compiler_params=_params("arbitrary"),
        name="adaln",
    )(cond, w_ada, b_ada.reshape(1, n))


def _rms(x, g):
    return x * lax.rsqrt(jnp.mean(x * x, axis=-1, keepdims=True) + EPS) * g


def _rope(x, c, s):
    w = x.shape[-1]
    lane = lax.broadcasted_iota(jnp.int32, x.shape, 1)
    nxt = pltpu.roll(x, w - 1, 1)
    prv = pltpu.roll(x, 1, 1)
    return x * c + jnp.where(lane % 2 == 0, nxt, prv) * s


N_INPROJ_WEIGHTS = 6


def _mod_row(mod_ref, row0, per_batch):
    r = row0 + pl.program_id(0) if per_batch else row0
    return mod_ref[pl.ds(r, 1), :]


def _modulated(x_ref, m):
    d = x_ref.shape[1]
    return (x_ref[...] * (1.0 + m[:, d:2 * d]) + m[:, 0:d]).astype(BF16)


def _keys(k_nope, kpe):
    shared = pltpu.roll(kpe, MLA_NOPE, 1)
    return (k_nope + jnp.concatenate([shared] * MLA_HEADS, axis=1)).astype(BF16)


def _inproj_kernel(*refs, row0, per_batch, rope):
    x_ref, mod_ref = refs[:2]
    _inproj_body(x_ref, _mod_row(mod_ref, row0, per_batch), *refs[2:], rope=rope)


def _inproj_body(x_ref, m, *refs, rope):
    gam_ref, win_ref, qn_ref, wuq_ref, kvn_ref, wk_ref, wv_ref = refs[:1 + N_INPROJ_WEIGHTS]
    refs = refs[1 + N_INPROJ_WEIGHTS:]
    if rope:
        cq_ref, sq_ref, ck_ref, sk_ref = refs[:4]
        refs = refs[4:]
    q_o, k_o, v_o, ckv_o, kpe_o, hgx_o = refs
    h = _modulated(x_ref, m)
    hw = HG_HEADS * HG_DK
    o_kv = qn_ref.shape[1]
    o_pe = o_kv + kvn_ref.shape[1]
    o_h = o_pe + MLA_ROPE

    cq = _rms(_dot_nt(h, win_ref[0:o_kv, :]), qn_ref[...])
    q = _dot(cq.astype(BF16), wuq_ref[...])
    if rope:
        q = _rope(q, jnp.concatenate([cq_ref[...]] * MLA_HEADS, axis=1),
                  jnp.concatenate([sq_ref[...]] * MLA_HEADS, axis=1))
    q_o[...] = q.astype(BF16)

    ckv = _rms(_dot_nt(h, win_ref[o_kv:o_pe, :]), kvn_ref[...])
    ckv_o[...] = ckv
    kpe = _dot_nt(h, win_ref[o_pe:o_h, :])
    kpe = jnp.concatenate([kpe, jnp.zeros((kpe.shape[0], LANES - MLA_ROPE), F32)], axis=1)
    if rope:
        kpe = _rope(kpe, ck_ref[...], sk_ref[...])
    kpe_o[...] = kpe
    cb = ckv.astype(BF16)
    k_o[...] = _keys(_dot(cb, wk_ref[...]), kpe)
    v_o[...] = _dot(cb, wv_ref[...]).astype(BF16)

    z = _dot_nt(h, win_ref[o_h:o_h + 5 * hw, :])
    hgx_o[:, 0:hw] = _silu(z[:, 0:hw])
    for dr in range(2):
        g0, g1 = gam_ref[dr, 0:1, :], gam_ref[dr, 1:2, :]
        gmax = jnp.maximum(g0, g1)
        e0, e1 = jnp.exp(g0 - gmax), jnp.exp(g1 - gmax)
        lb = e0 / (e0 + e1)
        f = lb + (1.0 - lb) * jax.nn.sigmoid(z[:, (1 + dr) * hw:(2 + dr) * hw])
        hgx_o[:, (1 + 2 * dr) * hw:(2 + 2 * dr) * hw] = jnp.log(f)
        hgx_o[:, (2 + 2 * dr) * hw:(3 + 2 * dr) * hw] = 1.0 - f
    hgx_o[:, 5 * hw:6 * hw] = z[:, 3 * hw:4 * hw]
    hgx_o[:, 6 * hw:7 * hw] = z[:, 4 * hw:5 * hw]


def _inproj(x2d, batch, seq, mod, gamma, wts, row0, per_batch, rope_tabs):
    n, d = x2d.shape
    tm = MIXER_SEQS * TOK_BLOCK if seq % (MIXER_SEQS * TOK_BLOCK) == 0 else TOK_BLOCK
    nblk = seq // tm
    rope = rope_tabs is not None
    hp = MLA_HEADS * LANES
    hw = HG_HEADS * HG_DK
    tok = lambda b, i: (b * nblk + i, 0)
    pos = lambda b, i: (i, 0)
    weights = [wts[k] for k in INPROJ_KEYS]
    ins = [x2d, mod, gamma] + weights
    in_specs = ([pl.BlockSpec((tm, d), tok), _const_spec(mod.shape), _const_spec(gamma.shape)]
                + [_const_spec(w.shape) for w in weights])
    if rope:
        ins += list(rope_tabs)
        in_specs += [pl.BlockSpec((tm, t.shape[1]), pos) for t in rope_tabs]
    widths = [(hp, BF16), (hp, BF16), (MLA_HEADS * MLA_V, BF16), (wts["kvn"].shape[1], F32), (LANES, F32),
              (7 * hw, F32)]
    return pl.pallas_call(
        functools.partial(_inproj_kernel, row0=row0, per_batch=per_batch, rope=rope),
        out_shape=[jax.ShapeDtypeStruct((n, w), dt) for w, dt in widths],
        grid=(batch, nblk),
        in_specs=in_specs,
        out_specs=[pl.BlockSpec((tm, w), tok) for w, _ in widths],
        compiler_params=_params("arbitrary", "arbitrary"),
        name="inproj",
    )(*ins)


def _kvup_kernel(ckv_ref, kpe_ref, wk_ref, wv_ref, k_o, v_o):
    cb = ckv_ref[...].astype(BF16)
    k_o[...] = _keys(_dot(cb, wk_ref[...]), kpe_ref[...])
    v_o[...] = _dot(cb, wv_ref[...]).astype(BF16)


def _kvup(ckv2d, kpe2d, wts):
    n = ckv2d.shape[0]
    tm = TOK_BLOCK
    widths = [MLA_HEADS * LANES, MLA_HEADS * MLA_V]
    row = lambda i: (i, 0)
    ws = [wts["wk"], wts["wv"]]
    return pl.pallas_call(
        _kvup_kernel,
        out_shape=[jax.ShapeDtypeStruct((n, w), BF16) for w in widths],
        grid=(n // tm,),
        in_specs=[pl.BlockSpec((tm, ckv2d.shape[1]), row), pl.BlockSpec((tm, LANES), row)]
                 + [_const_spec(w.shape) for w in ws],
        out_specs=[pl.BlockSpec((tm, w), row) for w in widths],
        compiler_params=_params("arbitrary"),
        name="kvup",
    )(ckv2d, kpe2d, *ws)


ATTN_SCALE = (MLA_NOPE + MLA_ROPE) ** -0.5


def _attn_body(q_ref, k_ref, v_ref, kc_ref, vc_ref, o_ref):
    cached = kc_ref is not None
    scale = ATTN_SCALE * math.log2(math.e)
    per_slab = LANES // MLA_V
    own = lax.broadcasted_iota(jnp.int32, (q_ref.shape[0], LANES), 1) // MLA_V
    for slab in range(MLA_HEADS // per_slab):
        vsl = slice(slab * LANES, (slab + 1) * LANES)
        out = None
        for sub in range(per_slab):
            hd = slab * per_slab + sub
            sl = slice(hd * LANES, (hd + 1) * LANES)
            q = q_ref[:, sl]
            s = _dot_nt(q, k_ref[:, sl])
            mx = jnp.max(s, axis=-1, keepdims=True)
            if cached:
                s2 = _dot_nt(q, kc_ref[:, sl])
                mx = jnp.maximum(mx, jnp.max(s2, axis=-1, keepdims=True))
            e = jnp.exp2((s - mx) * scale)
            den = jnp.sum(e, axis=-1, keepdims=True)
            o = _dot(e.astype(BF16), v_ref[:, vsl])
            if cached:
                e2 = jnp.exp2((s2 - mx) * scale)
                den = den + jnp.sum(e2, axis=-1, keepdims=True)
                o = o + _dot(e2.astype(BF16), vc_ref[:, vsl])
            o = o / den
            out = o if out is None else jnp.where(own == sub, o, out)
        o_ref[:, vsl] = out.astype(o_ref.dtype)


def _chunk_scan(x, reverse):
    tm = x.shape[0]
    rin = lax.broadcasted_iota(jnp.int32, x.shape, 0) % HG_CHUNK
    step = 1
    while step < HG_CHUNK:
        if reverse:
            x = x + jnp.where(rin < HG_CHUNK - step, pltpu.roll(x, tm - step, 0), 0.0)
        else:
            x = x + jnp.where(rin >= step, pltpu.roll(x, step, 0), 0.0)
        step *= 2
    return x


def _hgrn_kernel(*refs, has_init):
    fwd, bwd = refs[0:4], refs[4:8]
    refs = refs[8:]
    s0_ref = None
    if has_init:
        s0_ref = refs[0]
        refs = refs[1:]
    of_ref, ob_ref, sfin_ref, st_scr = refs
    i = pl.program_id(1)
    _hgrn_body([(fwd, bwd, of_ref, ob_ref, sfin_ref)], s0_ref, st_scr, i == 0, i == pl.num_programs(1) - 1)


def _hgrn_body(jobs, s0_ref, st_scr, first, last):
    tm = jobs[0][0][0].shape[0]
    c = HG_CHUNK
    nch = tm // c
    dk, dv = HG_DK, HG_DV
    hw = HG_HEADS * dk

    def initial(dr, hd):
        return s0_ref[0, dr, hd].T if s0_ref is not None else jnp.zeros((dv, dk), F32)

    if st_scr is not None:
        @pl.when(first)
        def _init():
            for dr in range(2):
                for hd in range(HG_HEADS):
                    st_scr[dr, hd] = initial(dr, hd)

    npair = nch // 2
    pair = 2 * c
    row = lax.broadcasted_iota(jnp.int32, (tm, tm), 0)
    col = lax.broadcasted_iota(jnp.int32, (tm, tm), 1)
    same = (row // c) == (col // c)
    same_pair = (row // pair) == (col // pair)
    bd = (lax.broadcasted_iota(jnp.int32, (tm, npair * dk), 0) // pair
          == lax.broadcasted_iota(jnp.int32, (tm, npair * dk), 1) // dk)
    chunk_odd = (lax.broadcasted_iota(jnp.int32, (tm, hw), 0) // c) % 2 == 1

    for dr in range(2):
        tri = same & ((col <= row) if dr == 0 else (col >= row))
        cross = same_pair & (((row // c) > (col // c)) if dr == 0 else ((row // c) < (col // c)))
        second = chunk_odd if dr == 0 else ~chunk_odd
        order = range(npair) if dr == 0 else range(npair - 1, -1, -1)

        def decayed(job):
            hq_ref, lf_ref, kk_ref, vv_ref = job[dr]
            bcum = _chunk_scan(lf_ref[...], reverse=dr == 1)
            closing = c - 1 if dr == 0 else 0
            btot3 = bcum.reshape(nch, c, hw)[:, closing:closing + 1, :]
            btot = jnp.broadcast_to(btot3, (nch, c, hw)).reshape(tm, hw)
            bpart = jnp.where(chunk_odd, pltpu.roll(btot, c, 0), pltpu.roll(btot, tm - c, 0))
            epart = jnp.exp(bpart)
            kk = kk_ref[...]
            qd = hq_ref[...] * jnp.exp(bcum)
            kd = kk * jnp.exp(-bcum)
            ke = kk * jnp.exp(btot - bcum)
            qd2 = jnp.where(second, qd * epart, qd)
            ke2 = jnp.where(second, ke, ke * epart)
            return qd, kd, ke, qd2, ke2, btot + bpart, vv_ref[...]

        def head(job, hd, qd, kd, ke, qd2, ke2, bpair, vv):
            o_ref, sfin_ref = job[2 + dr], job[4]
            sl = slice(hd * dk, (hd + 1) * dk)
            qd_h = qd[:, sl].astype(BF16)
            v_h = vv[:, hd * dv:(hd + 1) * dv]
            a = jnp.where(tri, _dot_nt(qd_h, kd[:, sl].astype(BF16)),
                          jnp.where(cross, _dot_nt(qd_h, ke[:, sl].astype(BF16)), 0.0))
            o_intra = _dot(a.astype(BF16), v_h.astype(BF16))
            kebd = jnp.where(bd, jnp.concatenate([ke2[:, sl]] * npair, axis=1), 0.0).astype(BF16)
            qbd = jnp.where(bd, jnp.concatenate([qd2[:, sl]] * npair, axis=1), 0.0).astype(BF16)
            ut = _dot(v_h.T.astype(BF16), kebd)
            st = st_scr[dr, hd] if st_scr is not None else initial(dr, hd)
            prev = [None] * npair
            for p in order:
                prev[p] = st
                st = st * jnp.exp(bpair[p * pair:p * pair + 1, sl]) + ut[:, p * dk:(p + 1) * dk]
            if st_scr is not None:
                st_scr[dr, hd] = st
            o_inter = _dot_nt(qbd, jnp.concatenate(prev, axis=1).astype(BF16))
            o_ref[:, hd * dv:(hd + 1) * dv] = o_intra + o_inter

            if last is True:
                sfin_ref[0, dr, hd] = st.T
            else:
                @pl.when(last)
                def _final():
                    sfin_ref[0, dr, hd] = st.T

        prepared = [decayed(job) for job in jobs]
        for hd in range(HG_HEADS):
            for job, arrays in zip(jobs, prepared):
                head(job, hd, *arrays)


def _hgrn(hgx, batch, seq, s0=None):
    n = hgx.shape[0]
    tm = TOK_BLOCK
    nblk = seq // tm
    hw = HG_HEADS * HG_DK

    def spec(lane_blk, rev):
        if rev:
            return pl.BlockSpec((tm, hw), lambda b, i: (b * nblk + nblk - 1 - i, lane_blk))
        return pl.BlockSpec((tm, hw), lambda b, i: (b * nblk + i, lane_blk))

    in_specs = [spec(0, False), spec(1, False), spec(2, False), spec(5, False),
                spec(0, True), spec(3, True), spec(4, True), spec(5, True)]
    ins = [hgx] * 8
    st_shape = (1, 2, HG_HEADS, HG_DK, HG_DV)
    st_spec = pl.BlockSpec(st_shape, lambda b, i: (b, 0, 0, 0, 0))
    if s0 is not None:
        ins.append(s0)
        in_specs.append(st_spec)
    return pl.pallas_call(
        functools.partial(_hgrn_kernel, has_init=s0 is not None),
        out_shape=[jax.ShapeDtypeStruct((n, hw), F32), jax.ShapeDtypeStruct((n, hw), F32),
                   jax.ShapeDtypeStruct((batch,) + st_shape[1:], F32)],
        grid=(batch, nblk),
        in_specs=in_specs,
        out_specs=[spec(0, False), spec(0, True), st_spec],
        scratch_shapes=[pltpu.VMEM((2, HG_HEADS, HG_DV, HG_DK), F32)],
        compiler_params=_params("arbitrary", "arbitrary"),
        name="hgrn",
    )(*ins)


def _layer_norm(x, g, b):
    xc = x - jnp.mean(x, axis=-1, keepdims=True)
    var = jnp.mean(xc * xc, axis=-1, keepdims=True)
    return xc * lax.rsqrt(var + EPS) * g + b


N_POSTMIX_WEIGHTS = 8
INPROJ_KEYS = ("win", "qn", "wuq", "kvn", "wk", "wv")
POSTMIX_KEYS = ("hgn", "womla", "wohg", "wout", "ln1g", "ln1b", "wr")
MIXER_SEQS = 2


def _postmix_kernel(x_ref, mod_ref, *refs, alpha, row0, per_batch, cached):
    n_attn = 5 if cached else 3
    q_ref, k_ref, v_ref = refs[:3]
    kc_ref, vc_ref = refs[3:5] if cached else (None, None)
    of_ref, ob_ref, zg_ref = refs[n_attn:n_attn + 3]
    om_s = refs[-1]
    _attn_body(q_ref, k_ref, v_ref, kc_ref, vc_ref, om_s)
    _postmix_body(x_ref, _mod_row(mod_ref, row0, per_batch), of_ref, ob_ref, zg_ref, om_s,
                  *refs[n_attn + 3:-1], alpha=alpha)


def _postmix_body(x_ref, m, of_ref, ob_ref, zg_ref, om_ref, wg_ref, hgn_ref, womla_ref,
                  wohg_ref, wout_ref, lng_ref, lnb_ref, wr_ref, x1_o, h2_o, aff_o, *, alpha):
    d = x_ref.shape[1]
    tb = aff_o.shape[2]
    g1, sh2, sc2 = m[:, 2 * d:3 * d], m[:, 3 * d:4 * d], m[:, 4 * d:5 * d]
    o = of_ref[...] + ob_ref[...]
    zg = zg_ref[...]
    parts = []
    for hd in range(HG_HEADS):
        sl = slice(hd * HG_DV, (hd + 1) * HG_DV)
        parts.append(_rms(o[:, sl], hgn_ref[...]) * _silu(zg[:, sl]))
    ohg = jnp.concatenate(parts, axis=1).astype(BF16)
    gates = _dot_nt(_modulated(x_ref, m), wg_ref[wg_ref.shape[0] - 2 * d:, :])
    merged = (jax.nn.sigmoid(gates[:, 0:d]) * _dot(om_ref[...], womla_ref[...])
              + jax.nn.sigmoid(gates[:, d:2 * d]) * _dot(ohg, wohg_ref[...]))
    mix = _dot(merged.astype(BF16), wout_ref[...])
    x1 = _layer_norm(alpha * x_ref[...] + g1 * mix, lng_ref[...], lnb_ref[...])
    x1_o[...] = x1
    h2 = (x1 * (1.0 + sc2) + sh2).astype(BF16)
    h2_o[...] = h2
    logits = _dot_nt(wr_ref[...], h2)
    e = jnp.exp(logits - jnp.max(logits, axis=0, keepdims=True))
    aff = e / jnp.sum(e, axis=0, keepdims=True)
    for blk in range(aff_o.shape[0]):
        aff_o[blk] = aff[:, blk * tb:(blk + 1) * tb]


def _postmix(x2d, batch, seq, mod, q, k, v, cache, o_f, o_b, hgx, wts, alpha, n_experts, row0, per_batch):
    n, d = x2d.shape
    tm = TOK_BLOCK
    nblk = seq // tm
    hw = HG_HEADS * HG_DV
    hp, hv = q.shape[1], v.shape[1]
    tok = lambda b, i: (b * nblk + i, 0)
    per_seq = lambda b, i: (b, 0)
    weights = [wts[k] for k in POSTMIX_KEYS]
    attn_ins = [q, k, v]
    attn_specs = [pl.BlockSpec((tm, hp), tok), pl.BlockSpec((seq, hp), per_seq), pl.BlockSpec((seq, hv), per_seq)]
    if cache is not None:
        past = cache[0].shape[0] // batch
        attn_ins += list(cache)
        attn_specs += [pl.BlockSpec((past, hp), per_seq), pl.BlockSpec((past, hv), per_seq)]
    return pl.pallas_call(
        functools.partial(_postmix_kernel, alpha=alpha, row0=row0, per_batch=per_batch,
                          cached=cache is not None),
        out_shape=[jax.ShapeDtypeStruct((n, d), F32), jax.ShapeDtypeStruct((n, d), BF16),
                   jax.ShapeDtypeStruct((n // tm, n_experts, tm), F32)],
        grid=(batch, nblk),
        in_specs=[pl.BlockSpec((tm, d), tok), _const_spec(mod.shape)] + attn_specs
                 + [pl.BlockSpec((tm, hw), tok), pl.BlockSpec((tm, hw), tok),
                    pl.BlockSpec((tm, hw), lambda b, i: (b * nblk + i, 6)), _const_spec(wts["win"].shape)]
                 + [_const_spec(w.shape) for w in weights],
        out_specs=[pl.BlockSpec((tm, d), tok), pl.BlockSpec((tm, d), tok),
                   pl.BlockSpec((1, n_experts, tm), lambda b, i: (b * nblk + i, 0, 0))],
        scratch_shapes=[pltpu.VMEM((tm, hv), BF16)],
        compiler_params=_params("arbitrary", "arbitrary"),
        name="postmix",
    )(x2d, mod, *attn_ins, o_f, o_b, hgx, wts["win"], *weights)


def _mixer_kernel(x_ref, mod_ref, *refs, alpha, row0, seq):
    nw = 1 + N_INPROJ_WEIGHTS
    in_w, refs = refs[:nw], refs[nw:]
    pm_w, refs = refs[:N_POSTMIX_WEIGHTS - 1], refs[N_POSTMIX_WEIGHTS - 1:]
    pm_w = (in_w[1],) + tuple(pm_w)
    x1_o, h2_o, aff_o, ckv_o, kpe_o, sfin_o, q_s, k_s, v_s, hgx_s, om_s, of_s, ob_s = refs
    m = _mod_row(mod_ref, row0, False)
    _inproj_body(x_ref, m, *in_w, q_s, k_s, v_s, ckv_o, kpe_o, hgx_s, rope=False)
    hw = HG_HEADS * HG_DK
    jobs = []
    for s in range(x_ref.shape[0] // seq):
        rows = slice(s * seq, (s + 1) * seq)
        _attn_body(q_s.at[rows], k_s.at[rows], v_s.at[rows], None, None, om_s.at[rows])
        lane = lambda j, rows=rows: hgx_s.at[rows, j * hw:(j + 1) * hw]
        jobs.append(((lane(0), lane(1), lane(2), lane(5)), (lane(0), lane(3), lane(4), lane(5)),
                     of_s.at[rows], ob_s.at[rows], sfin_o.at[s:s + 1]))
    _hgrn_body(jobs, None, None, True, True)
    _postmix_body(x_ref, m, of_s, ob_s, hgx_s.at[:, 6 * hw:7 * hw], om_s, *pm_w, x1_o, h2_o, aff_o,
                  alpha=alpha)


def _mixer(x2d, batch, seq, mod, gamma, wts, alpha, n_experts, row0):
    n, d = x2d.shape
    assert seq == TOK_BLOCK
    ns = MIXER_SEQS if batch % MIXER_SEQS == 0 else 1
    tm = ns * seq
    hp = MLA_HEADS * LANES
    hv = MLA_HEADS * MLA_V
    hw = HG_HEADS * HG_DK
    kvl = wts["kvn"].shape[1]
    weights = [wts[k] for k in INPROJ_KEYS + POSTMIX_KEYS]
    tok = lambda b: (b, 0)
    st_shape = (ns, 2, HG_HEADS, HG_DK, HG_DV)
    return pl.pallas_call(
        functools.partial(_mixer_kernel, alpha=alpha, row0=row0, seq=seq),
        out_shape=[jax.ShapeDtypeStruct((n, d), F32), jax.ShapeDtypeStruct((n, d), BF16),
                   jax.ShapeDtypeStruct((n // seq, n_experts, seq), F32),
                   jax.ShapeDtypeStruct((n, kvl), F32), jax.ShapeDtypeStruct((n, LANES), F32),
                   jax.ShapeDtypeStruct((batch,) + st_shape[1:], F32)],
        grid=(batch // ns,),
        in_specs=[pl.BlockSpec((tm, d), tok), _const_spec(mod.shape), _const_spec(gamma.shape)]
                 + [_const_spec(w.shape) for w in weights],
        out_specs=[pl.BlockSpec((tm, d), tok), pl.BlockSpec((tm, d), tok),
                   pl.BlockSpec((ns, n_experts, seq), lambda b: (b, 0, 0)),
                   pl.BlockSpec((tm, kvl), tok), pl.BlockSpec((tm, LANES), tok),
                   pl.BlockSpec(st_shape, lambda b: (b, 0, 0, 0, 0))],
        scratch_shapes=[pltpu.VMEM((tm, hp), BF16), pltpu.VMEM((tm, hp), BF16), pltpu.VMEM((tm, hv), BF16),
                        pltpu.VMEM((tm, 7 * hw), F32), pltpu.VMEM((tm, hv), BF16),
                        pltpu.VMEM((tm, hw), F32), pltpu.VMEM((tm, hw), F32)],
        compiler_params=_params("arbitrary"),
        name="mixer",
    )(x2d, mod, gamma, *weights)


def _route_kernel(*refs, caps):
    ng = len(caps)
    for aff_ref, rank_o, cnt_o, cap in zip(refs[0:ng], refs[ng:2 * ng], refs[2 * ng:3 * ng], caps):
        _route_group(aff_ref, rank_o, cnt_o, cap)


def _route_group(aff_ref, rank_o, cnt_o, cap):
    nb, ne, tb = aff_ref.shape
    key = aff_ref[...]

    def count(mask):
        return jnp.sum(jnp.sum(jnp.where(mask, 1.0, 0.0), axis=0), axis=1, keepdims=True)

    def bit_step(it, bits):
        cand = bits | jnp.left_shift(jnp.int32(1), 30 - it)
        return jnp.where(count(key >= pltpu.bitcast(cand, F32)[None]) >= cap, cand, bits)

    bits = lax.fori_loop(0, 31, bit_step, jnp.zeros((ne, 1), jnp.int32))
    thr = pltpu.bitcast(bits, F32)
    need = cap - count(key > thr[None])
    before = (lax.broadcasted_iota(jnp.int32, (tb, tb), 0)
              < lax.broadcasted_iota(jnp.int32, (tb, tb), 1))
    before = jnp.where(before, 1.0, 0.0).astype(BF16)
    off_eq = jnp.zeros((ne, 1), F32)
    off_sel = jnp.zeros((ne, 1), F32)
    cnt_o[...] = jnp.zeros_like(cnt_o)
    for blk in range(nb):
        key_b = key[blk]
        eq = key_b == thr
        eq_b = jnp.where(eq, 1.0, 0.0)
        eq_rank = _dot(eq_b.astype(BF16), before) + off_eq
        sel = (key_b > thr) | (eq & (eq_rank < need))
        sel_b = jnp.where(sel, 1.0, 0.0)
        rank = _dot(sel_b.astype(BF16), before) + off_sel
        rank_o[blk] = jnp.where(sel, rank.astype(jnp.int32), UNSELECTED)
        cnt_o[:, blk:blk + 1] = off_sel.astype(jnp.int32)
        off_eq = off_eq + jnp.sum(eq_b, axis=1, keepdims=True)
        off_sel = off_sel + jnp.sum(sel_b, axis=1, keepdims=True)
    cnt_o[:, nb:nb + 1] = off_sel.astype(jnp.int32)


def _route(affs, caps):
    ng = len(affs)
    ne = affs[0].shape[1]
    assert all(a.shape[0] + 1 <= LANES for a in affs)
    outs = pl.pallas_call(
        functools.partial(_route_kernel, caps=tuple(caps)),
        out_shape=[jax.ShapeDtypeStruct(a.shape, jnp.int32) for a in affs]
                  + [jax.ShapeDtypeStruct((ne, LANES), jnp.int32)] * ng,
        in_specs=[pl.BlockSpec(memory_space=pltpu.VMEM)] * ng,
        out_specs=[pl.BlockSpec(memory_space=pltpu.VMEM)] * (2 * ng),
        compiler_params=pltpu.CompilerParams(vmem_limit_bytes=VMEM_LIMIT),
        name="route",
    )(*affs)
    return [(outs[gi], outs[ng + gi][:, :affs[gi].shape[0] + 1]) for gi in range(ng)]


def _window_hits(rk_ref, firsts, slot0, win):
    ne, tb = rk_ref.shape[1], rk_ref.shape[2]
    win_iota = lax.broadcasted_iota(jnp.int32, (win, tb), 0)
    return [(rk_ref[0, e:e + 1, :] + (slot0 - firsts[e])) == win_iota for e in range(ne)]


def _compact_kernel(first_ref, end_ref, rounds_ref, *refs, groups, slots):
    ng = len(groups)
    h2_refs, rk_refs, af_refs = refs[0:ng], refs[ng:2 * ng], refs[2 * ng:3 * ng]
    xe_hbm, stage, tail, sem, issued = refs[3 * ng:]
    hbms, stages, tails = [xe_hbm], [stage], [tail]
    b = pl.program_id(0)
    ne = rk_refs[0].shape[1]
    win = WIN_ROWS
    sub = BF16_ROWS

    def copies(slot, dsts):
        return [pltpu.make_async_copy(stage.at[slot, pl.ds(e * win, win), :],
                                      hbm.at[e, pl.ds(pl.multiple_of(dsts[e], sub), win), :], sem.at[e])
                for stage, hbm in zip(stages, hbms) for e in range(ne)]

    def wait_previous():
        @pl.when(issued[0] > 0)
        def _():
            for cp in copies(0, [0] * ne):
                cp.wait()

    @pl.when(b == 0)
    def _init():
        issued[0] = 0
        for stage, tail in zip(stages, tails):
            tail[...] = jnp.zeros_like(tail)
            stage[1] = jnp.zeros(stage.shape[1:], stage.dtype)
        pad = copies(1, [slots] * ne)
        for cp in pad:
            cp.start()
        for cp in pad:
            cp.wait()

    def group_body(h2_ref, rk_ref, af_ref, slot0):
        firsts = [first_ref[b * ne + e] for e in range(ne)]
        bases = [(f // sub) * sub for f in firsts]
        ends = [end_ref[b * ne + e] - bases[e] for e in range(ne)]

        def one_round(r, carry):
            dsts = [bases[e] + r * win for e in range(ne)]
            hits = _window_hits(rk_ref, dsts, slot0, win)
            onehot = jnp.where(jnp.concatenate(hits, axis=0), 1.0, 0.0).astype(BF16)
            gate = jnp.concatenate(
                [jnp.sum(jnp.where(hits[e], af_ref[0, e:e + 1, :], 0.0), axis=1, keepdims=True)
                 for e in range(ne)], axis=0)
            slot = issued[0] % 2
            g1 = gate.astype(BF16).astype(F32)
            g2 = (gate - g1).astype(BF16).astype(F32)
            g3 = (gate - g1 - g2).astype(BF16).astype(F32)
            lane = lax.broadcasted_iota(jnp.int32, (ne * win, LANES), 1)
            pieces3 = jnp.where(lane == 0, g1, jnp.where(lane == 1, g2, jnp.where(lane == 2, g3, 0.0)))
            streams = (jnp.concatenate([_dot(onehot, h2_ref[...]).astype(BF16),
                                        pieces3.astype(BF16)], axis=1),)
            for val, stage, tail in zip(streams, stages, tails):
                sub_iota = lax.broadcasted_iota(jnp.int32, (sub, val.shape[1]), 0)
                pieces = []
                for e in range(ne):
                    old = tail[e * sub:(e + 1) * sub, :]
                    shared = jnp.where(r == 0, firsts[e] - bases[e], 0)
                    groups_e = [val[e * win + g * sub:e * win + (g + 1) * sub, :] for g in range(win // sub)]
                    groups_e[0] = jnp.where(sub_iota < shared, old, groups_e[0])
                    pieces += groups_e
                    last = (ends[e] // sub) * sub
                    new = old
                    for g, grp in enumerate(groups_e):
                        new = jnp.where((r == last // win) & (last % win == g * sub), grp, new)
                    tail[e * sub:(e + 1) * sub, :] = new
                stage[slot] = jnp.concatenate(pieces, axis=0)
            wait_previous()
            for cp in copies(slot, [jnp.minimum(dst, slots) for dst in dsts]):
                cp.start()
            issued[0] = issued[0] + 1
            return carry

        lax.fori_loop(0, rounds_ref[b], one_round, 0)

    blk0 = 0
    for gi, g in enumerate(groups):
        @pl.when((b >= blk0) & (b < blk0 + g["nb"]))
        def _(gi=gi, g=g):
            group_body(h2_refs[gi], rk_refs[gi], af_refs[gi], g["slot0"])
        blk0 += g["nb"]

    @pl.when(b == pl.num_programs(0) - 1)
    def _drain():
        wait_previous()


def _compact(groups, first, end, rounds, slots):
    d = groups[0]["h2"].shape[1]
    nbs = [g["rank"].shape[0] for g in groups]
    ne, tb = groups[0]["rank"].shape[1:]
    meta, specs_h2, specs_rk = [], [], []
    blk0 = 0
    for g, nb in zip(groups, nbs):
        meta.append(dict(nb=nb, slot0=g["slot0"]))
        local = lambda b, *_, blk0=blk0, nb=nb: jnp.clip(b - blk0, 0, nb - 1)
        specs_h2.append(pl.BlockSpec((tb, d), lambda b, *_, local=local: (local(b), 0)))
        specs_rk.append(pl.BlockSpec((1, ne, tb), lambda b, *_, local=local: (local(b), 0, 0)))
        blk0 += nb
    width = d + LANES
    return pl.pallas_call(
        functools.partial(_compact_kernel, groups=meta, slots=slots),
        out_shape=jax.ShapeDtypeStruct((ne, slots + WIN_ROWS, width), BF16),
        grid_spec=pltpu.PrefetchScalarGridSpec(
            num_scalar_prefetch=3,
            grid=(sum(nbs),),
            in_specs=specs_h2 + specs_rk + specs_rk,
            out_specs=pl.BlockSpec(memory_space=pl.ANY),
            scratch_shapes=[pltpu.VMEM((2, ne * WIN_ROWS, width), BF16),
                            pltpu.VMEM((ne * BF16_ROWS, width), BF16),
                            pltpu.SemaphoreType.DMA((ne,)), pltpu.SMEM((1,), jnp.int32)]),
        compiler_params=_params("arbitrary"),
        name="compact",
    )(first, end, rounds, *[g["h2"] for g in groups], *[g["rank"] for g in groups],
      *[g["aff"] for g in groups])


def _ffn_kernel(xe_ref, w1_ref, w3_ref, w2_ref, ye_ref, *scratch):
    f = pl.program_id(1)
    d = w1_ref.shape[1]
    x = xe_ref[0, :, 0:d]
    pieces = xe_ref[0, :, d:d + LANES].astype(F32)
    gate = pieces[:, 0:1] + pieces[:, 1:2] + pieces[:, 2:3]
    hid = _silu(_dot(x, w1_ref[0].astype(BF16))) * _dot(x, w3_ref[0].astype(BF16))
    y = _dot(hid.astype(BF16), w2_ref[0].astype(BF16))
    if not scratch:
        ye_ref[0] = (y * gate).astype(ye_ref.dtype)
        return
    acc_scr, = scratch
    last = pl.num_programs(1) - 1

    @pl.when(f == 0)
    def _first():
        acc_scr[...] = y

    @pl.when((f > 0) & (f < last))
    def _middle():
        acc_scr[...] += y

    @pl.when(f == last)
    def _last():
        ye_ref[0] = ((acc_scr[...] + y) * gate).astype(ye_ref.dtype)


def _ffn(xe, w1, w3, w2, slots, ft):
    ne, d, dff = w1.shape
    nf = dff // ft
    return pl.pallas_call(
        _ffn_kernel,
        out_shape=jax.ShapeDtypeStruct((ne, slots, d), BF16),
        grid=(ne, nf),
        in_specs=[pl.BlockSpec((1, slots, xe.shape[2]), lambda e, f: (e, 0, 0)),
                  pl.BlockSpec((1, d, ft), lambda e, f: (e, 0, f)),
                  pl.BlockSpec((1, d, ft), lambda e, f: (e, 0, f)),
                  pl.BlockSpec((1, ft, d), lambda e, f: (e, f, 0))],
        out_specs=pl.BlockSpec((1, slots, d), lambda e, f: (e, 0, 0)),
        scratch_shapes=[pltpu.VMEM((slots, d), F32)] if nf > 1 else [],
        compiler_params=_params("arbitrary", "arbitrary"),
        name="ffn",
    )(xe, w1, w3, w2)


def _combine_kernel(first_ref, rounds_ref, *refs, groups, alpha, slots):
    ng = len(groups)
    rk_refs, x1_refs = refs[0:ng], refs[ng:2 * ng]
    mod_ref, lng_ref, lnb_ref, ye_hbm = refs[2 * ng:2 * ng + 4]
    out_refs = refs[2 * ng + 4:3 * ng + 4]
    buf, acc_scr, sem = refs[3 * ng + 4:]
    d = x1_refs[0].shape[1]
    b = pl.program_id(0)
    nblk = pl.num_programs(0)
    ne, tb = rk_refs[0].shape[1], rk_refs[0].shape[2]
    win = WIN_ROWS

    def starts_of(blk, r):
        firsts = [(first_ref[blk * ne + e] // BF16_ROWS) * BF16_ROWS + r * win for e in range(ne)]
        return firsts, [jnp.minimum(f, slots - win) for f in firsts]

    def windows(slot, starts):
        return [pltpu.make_async_copy(ye_hbm.at[e, pl.ds(pl.multiple_of(starts[e], BF16_ROWS), win), :],
                                      buf.at[slot, pl.ds(e * win, win), :], sem.at[slot, e])
                for e in range(ne)]

    def scatter(rk_ref, slot0, slot, firsts, starts, later_round):
        hits = _window_hits(rk_ref, starts, slot0, win)
        if later_round:
            hits = [h & ((rk_ref[0, e:e + 1, :] + slot0) >= firsts[e]) for e, h in enumerate(hits)]
        hit = jnp.where(jnp.concatenate(hits, axis=0), 1.0, 0.0)
        return _dot(hit.T.astype(BF16), buf[slot])

    cur = b % 2

    @pl.when(b == 0)
    def _prime():
        for cp in windows(0, starts_of(0, 0)[1]):
            cp.start()

    @pl.when(b + 1 < nblk)
    def _prefetch():
        for cp in windows(1 - cur, starts_of(b + 1, 0)[1]):
            cp.start()

    firsts0, starts0 = starts_of(b, 0)
    for cp in windows(cur, starts0):
        cp.wait()

    def group_body(rk_ref, x1_ref, out_ref, g, local):
        acc_scr[...] = scatter(rk_ref, g["slot0"], cur, firsts0, starts0, False)

        def extra_round(r, carry):
            firsts, starts = starts_of(b, r)
            for cp in windows(2, starts):
                cp.start()
            for cp in windows(2, starts):
                cp.wait()
            acc_scr[...] += scatter(rk_ref, g["slot0"], 2, firsts, starts, True)
            return carry

        lax.fori_loop(1, rounds_ref[b], extra_round, 0)
        r = g["row0"] + local // g["blocks_per_batch"]
        g2 = mod_ref[pl.ds(r, 1), :][:, 5 * d:6 * d]
        out_ref[...] = _layer_norm(alpha * x1_ref[...] + g2 * acc_scr[...], lng_ref[...], lnb_ref[...])

    blk0 = 0
    for gi, g in enumerate(groups):
        @pl.when((b >= blk0) & (b < blk0 + g["nb"]))
        def _(gi=gi, g=g, blk0=blk0):
            group_body(rk_refs[gi], x1_refs[gi], out_refs[gi], g, b - blk0)
        blk0 += g["nb"]


def _combine(ye, groups, first, rounds, mod, ln_g, ln_b, alpha):
    d = groups[0]["x1"].shape[1]
    ne, tb = groups[0]["rank"].shape[1:]
    slots = ye.shape[1]
    meta, specs_rk, specs_x1 = [], [], []
    blk0 = 0
    for g in groups:
        nb = g["rank"].shape[0]
        meta.append(dict(nb=nb, slot0=g["slot0"], row0=g["row0"], blocks_per_batch=g["blocks_per_batch"]))
        local = lambda b, *_, blk0=blk0, nb=nb: jnp.clip(b - blk0, 0, nb - 1)
        specs_rk.append(pl.BlockSpec((1, ne, tb), lambda b, *_, local=local: (local(b), 0, 0)))
        specs_x1.append(pl.BlockSpec((tb, d), lambda b, *_, local=local: (local(b), 0)))
        blk0 += nb
    const = lambda shape: pl.BlockSpec(shape, lambda b, *_: (0,) * len(shape))
    return pl.pallas_call(
        functools.partial(_combine_kernel, groups=meta, alpha=alpha, slots=slots),
        out_shape=[jax.ShapeDtypeStruct(g["x1"].shape, F32) for g in groups],
        grid_spec=pltpu.PrefetchScalarGridSpec(
            num_scalar_prefetch=2,
            grid=(blk0,),
            in_specs=specs_rk + specs_x1 + [const(mod.shape), const((1, d)), const((1, d)),
                                            pl.BlockSpec(memory_space=pl.ANY)],
            out_specs=specs_x1,
            scratch_shapes=[pltpu.VMEM((3, ne * WIN_ROWS, d), ye.dtype), pltpu.VMEM((tb, d), F32),
                            pltpu.SemaphoreType.DMA((3, ne))]),
        compiler_params=_params("arbitrary"),
        name="combine",
    )(first, rounds, *[g["rank"] for g in groups], *[g["x1"] for g in groups], mod, ln_g, ln_b, ye)


def _prep_weights(w_in, q_norm, w_uq, kv_norm, w_ukv, w_o_mla, hgrn_norm, w_o_hg, w_out, ln1_g, ln1_b,
                  w_router):
    d = w_in.shape[0]
    q_lora, kv_lora = q_norm.shape[0], kv_norm.shape[0]
    hw = HG_HEADS * HG_DK
    hh, hp = MLA_HEADS, MLA_HEADS * LANES
    o_kv, o_pe = q_lora, q_lora + kv_lora
    o_h = o_pe + MLA_ROPE
    o_g = o_h + 5 * hw
    assert w_in.shape[1] == o_g + 2 * d
    qk = MLA_NOPE + MLA_ROPE
    kvw = MLA_NOPE + MLA_V
    b16 = lambda a: a.astype(BF16)
    assert all(o % BF16_ROWS == 0 for o in (o_kv, o_pe, o_h, o_g))
    win = b16(w_in.T)
    wuq = jnp.pad(w_uq.reshape(q_lora, hh, qk), ((0, 0), (0, 0), (0, LANES - qk))).reshape(q_lora, hp)
    ukv = w_ukv.reshape(kv_lora, hh, kvw)
    wk = jnp.pad(ukv[:, :, :MLA_NOPE], ((0, 0), (0, 0), (0, LANES - MLA_NOPE))).reshape(kv_lora, hp)
    wv = ukv[:, :, MLA_NOPE:].reshape(kv_lora, hh * MLA_V)
    return dict(
        win=win, qn=q_norm.reshape(1, -1), wuq=b16(wuq), kvn=kv_norm.reshape(1, -1), wk=b16(wk), wv=b16(wv),
        hgn=hgrn_norm.reshape(1, -1), womla=b16(w_o_mla), wohg=b16(w_o_hg), wout=b16(w_out),
        ln1g=ln1_g.reshape(1, -1), ln1b=ln1_b.reshape(1, -1), wr=b16(w_router.T))


def _rope_tables(seq):
    n_freq = MLA_ROPE // 4
    inv = ROPE_BASE ** (-np.arange(n_freq, dtype=np.float64) / n_freq)
    t = np.arange(seq)
    ang = np.concatenate([(t // GRID_W)[:, None] * inv, (t % GRID_W)[:, None] * inv], axis=-1)
    cos = np.repeat(np.cos(ang), 2, axis=1)
    sin = np.repeat(np.sin(ang), 2, axis=1) * np.tile([-1.0, 1.0], MLA_ROPE // 2)
    ck = np.pad(cos, ((0, 0), (0, LANES - MLA_ROPE)), constant_values=1.0)
    sk = np.pad(sin, ((0, 0), (0, LANES - MLA_ROPE)))
    cq = np.pad(cos, ((0, 0), (MLA_NOPE, LANES - MLA_NOPE - MLA_ROPE)), constant_values=1.0)
    sq = np.pad(sin, ((0, 0), (MLA_NOPE, LANES - MLA_NOPE - MLA_ROPE)))
    return tuple(jnp.asarray(a, F32) for a in (cq, sq, ck, sk))


def _window_sched(cnt, slot0):
    first = slot0 + cnt[:, :-1]
    end = slot0 + cnt[:, 1:]
    flat = lambda a: a.T.reshape(-1).astype(jnp.int32)
    rounds = jnp.max((end - (first // BF16_ROWS) * BF16_ROWS + WIN_ROWS - 1) // WIN_ROWS, axis=0)
    return flat(first), flat(end), jnp.maximum(rounds, 1).astype(jnp.int32)


def kernel(x_prompt, x_sample, c, cache_ckv, cache_kpe, state_hgrn, c_ctx, w_ada, b_ada, w_in, mla_q_norm, mla_w_uq, mla_kv_norm, mla_w_ukv, mla_w_o, hgrn_gamma, hgrn_norm, hgrn_w_o, w_out, ln1_g, ln1_b, moe_w_router, moe_w1, moe_w3, moe_w2, ln2_g, ln2_b):
    depth = w_ada.shape[0]
    assert depth == 1, "single trunk layer"
    bp, tp, d = x_prompt.shape
    bs, tsq, _ = x_sample.shape
    ne = moe_w_router.shape[-1]
    alpha = (2 * depth) ** 0.25
    past = cache_ckv.shape[2]
    assert tp % TOK_BLOCK == 0 and tsq % TOK_BLOCK == 0 and past % TOK_BLOCK == 0 and tsq % GRID_W == 0

    wts = _prep_weights(w_in[0], mla_q_norm[0], mla_w_uq[0], mla_kv_norm[0], mla_w_ukv[0], mla_w_o[0],
                        hgrn_norm[0], hgrn_w_o[0], w_out[0], ln1_g[0], ln1_b[0], moe_w_router[0])
    cond_rows = -(-(1 + bs) // SUBLANES) * SUBLANES
    cond = jnp.concatenate([c_ctx[None], c, jnp.zeros((cond_rows - 1 - bs, d), F32)], axis=0)
    mod = _adaln(cond, w_ada[0], b_ada[0])

    xs = [x_prompt.reshape(bp * tp, d), x_sample.reshape(bs * tsq, d)]
    dims = [(bp, tp), (bs, tsq)]
    rows = [(0, False), (1, True)]
    ropes = [None, _rope_tables(tsq)]
    kpe_c = jnp.pad(cache_kpe[:, 0].reshape(bs * past, MLA_ROPE), ((0, 0), (0, LANES - MLA_ROPE)))
    caches = [None, _kvup(cache_ckv[:, 0].reshape(bs * past, -1), kpe_c, wts)]
    inits = [None, state_hgrn[:, 0]]

    x1s, h2s, affs, extras = [], [], [], []
    for gi in range(2):
        (bt, sq), (row0, per_batch) = dims[gi], rows[gi]
        if sq == TOK_BLOCK and caches[gi] is None and ropes[gi] is None and not per_batch:
            x1, h2, aff, ckv, kpe, s_fin = _mixer(xs[gi], bt, sq, mod, hgrn_gamma, wts, alpha, ne, row0)
        else:
            q, k, v, ckv, kpe, hgx = _inproj(xs[gi], bt, sq, mod, hgrn_gamma, wts, row0, per_batch, ropes[gi])
            o_f, o_b, s_fin = _hgrn(hgx, bt, sq, inits[gi])
            x1, h2, aff = _postmix(xs[gi], bt, sq, mod, q, k, v, caches[gi], o_f, o_b, hgx, wts, alpha, ne,
                                   row0, per_batch)
        x1s.append(x1)
        h2s.append(h2)
        affs.append(aff)
        extras.append((ckv, kpe, s_fin))

    caps = [EC_FACTOR * x.shape[0] // ne for x in xs]
    slots = sum(caps)
    assert all(cp % BF16_ROWS == 0 for cp in caps) and slots >= WIN_ROWS
    groups, scheds = [], []
    slot0 = 0
    routed = _route(affs, caps)
    for gi in range(2):
        rank, cnt = routed[gi]
        groups.append(dict(h2=h2s[gi], rank=rank, aff=affs[gi], slot0=slot0, x1=x1s[gi], row0=rows[gi][0],
                           blocks_per_batch=dims[gi][1] // TOK_BLOCK if rows[gi][1] else 1 << 30))
        scheds.append(_window_sched(cnt, slot0))
        slot0 += caps[gi]
    first, end, rounds = [jnp.concatenate([s[k] for s in scheds]) for k in range(3)]
    xe = _compact(groups, first, end, rounds, slots)
    ye = _ffn(xe, moe_w1[0], moe_w3[0], moe_w2[0], slots, ft=moe_w1.shape[-1])
    outs = _combine(ye, groups, first, rounds, mod, ln2_g[0].reshape(1, -1), ln2_b[0].reshape(1, -1), alpha)

    ckv_p, kpe_p, st_p = extras[0]
    y_prompt = outs[0].reshape(bp, tp, d)
    y_sample = outs[1].reshape(bs, tsq, d)
    new_ckv = ckv_p.reshape(bp, 1, tp, -1)
    new_kpe = kpe_p[:, :MLA_ROPE].reshape(bp, 1, tp, MLA_ROPE)
    new_state = st_p.reshape(bp, 1, 2, HG_HEADS, HG_DK, HG_DV)
    return (y_prompt, y_sample, new_ckv, new_kpe, new_state)
```

```python
import functools
import math

import jax
import jax.numpy as jnp
import numpy as np
from jax import lax
from jax.experimental import pallas as pl
from jax.experimental.pallas import tpu as pltpu

F32 = jnp.float32
BF16 = jnp.bfloat16

MLA_HEADS = 8
MLA_NOPE = 64
MLA_ROPE = 32
MLA_V = 64
HG_HEADS = 4
HG_DK = 128
HG_DV = 128
HG_CHUNK = 32
GRID_W = 64
ROPE_BASE = 10000.0
EC_FACTOR = 2
EPS = 1e-6

LANES = 128
SUBLANES = 8
BF16_ROWS = 16
VMEM_LIMIT = 56 * 1024 * 1024

TOK_BLOCK = 256
WIN_ROWS = 64
UNSELECTED = -(1 << 30)

NT_DIMS = (((1,), (1,)), ((), ()))


def _dot(a, b):
    return jnp.dot(a, b, preferred_element_type=F32)


def _dot_nt(a, b):
    return lax.dot_general(a, b, NT_DIMS, preferred_element_type=F32)


def _silu(x):
    return x * jax.nn.sigmoid(x)


def _params(*sem):
    return pltpu.CompilerParams(dimension_semantics=sem, vmem_limit_bytes=VMEM_LIMIT)


def _const_spec(shape):
    zeros = (0,) * len(shape)
    return pl.BlockSpec(shape, lambda *_: zeros, pipeline_mode=pl.Buffered(1))


def _adaln_kernel(c_ref, w_ref, b_ref, o_ref):
    s = _silu(c_ref[...]).astype(BF16)
    o_ref[...] = _dot(s, w_ref[...].astype(BF16)) + b_ref[...]


def _adaln(cond, w_ada, b_ada):
    rows, d = cond.shape
    n = w_ada.shape[1]
    tn = n // 4
    return pl.pallas_call(
        _adaln_kernel,
        out_shape=jax.ShapeDtypeStruct((rows, n), F32),
        grid=(n // tn,),
        in_specs=[_const_spec((rows, d)),
                  pl.BlockSpec((d, tn), lambda j: (0, j)),
                  pl.BlockSpec((1, tn), lambda j: (0, j))],
        out_specs=pl.BlockSpec((rows, tn), lambda j: (0, j)),
        compiler_params=_params("arbitrary"),
        name="adaln",
    )(cond, w_ada, b_ada.reshape(1, n))


def _rms(x, g):
    return x * lax.rsqrt(jnp.mean(x * x, axis=-1, keepdims=True) + EPS) * g


def _rope(x, c, s):
    w = x.shape[-1]
    lane = lax.broadcasted_iota(jnp.int32, x.shape, 1)
    nxt = pltpu.roll(x, w - 1, 1)
    prv = pltpu.roll(x, 1, 1)
    return x * c + jnp.where(lane % 2 == 0, nxt, prv) * s


N_INPROJ_WEIGHTS = 6


def _mod_row(mod_ref, row0, per_batch):
    r = row0 + pl.program_id(0) if per_batch else row0
    return mod_ref[pl.ds(r, 1), :]


def _modulated(x_ref, m):
    d = x_ref.shape[1]
    return (x_ref[...] * (1.0 + m[:, d:2 * d]) + m[:, 0:d]).astype(BF16)


def _keys(k_nope, kpe):
    shared = pltpu.roll(kpe, MLA_NOPE, 1)
    return (k_nope + jnp.concatenate([shared] * MLA_HEADS, axis=1)).astype(BF16)


def _inproj_kernel(*refs, row0, per_batch, rope):
    x_ref, mod_ref = refs[:2]
    _inproj_body(x_ref, _mod_row(mod_ref, row0, per_batch), *refs[2:], rope=rope)


def _inproj_body(x_ref, m, *refs, rope):
    gam_ref, win_ref, qn_ref, wuq_ref, kvn_ref, wk_ref, wv_ref = refs[:1 + N_INPROJ_WEIGHTS]
    refs = refs[1 + N_INPROJ_WEIGHTS:]
    if rope:
        cq_ref, sq_ref, ck_ref, sk_ref = refs[:4]
        refs = refs[4:]
    q_o, k_o, v_o, ckv_o, kpe_o, hgx_o = refs
    h = _modulated(x_ref, m)
    hw = HG_HEADS * HG_DK
    o_kv = qn_ref.shape[1]
    o_pe = o_kv + kvn_ref.shape[1]
    o_h = o_pe + MLA_ROPE

    cq = _rms(_dot_nt(h, win_ref[0:o_kv, :]), qn_ref[...])
    q = _dot(cq.astype(BF16), wuq_ref[...])
    if rope:
        q = _rope(q, jnp.concatenate([cq_ref[...]] * MLA_HEADS, axis=1),
                  jnp.concatenate([sq_ref[...]] * MLA_HEADS, axis=1))
    q_o[...] = q.astype(BF16)

    ckv = _rms(_dot_nt(h, win_ref[o_kv:o_pe, :]), kvn_ref[...])
    ckv_o[...] = ckv
    kpe = _dot_nt(h, win_ref[o_pe:o_h, :])
    kpe = jnp.concatenate([kpe, jnp.zeros((kpe.shape[0], LANES - MLA_ROPE), F32)], axis=1)
    if rope:
        kpe = _rope(kpe, ck_ref[...], sk_ref[...])
    kpe_o[...] = kpe
    cb = ckv.astype(BF16)
    k_o[...] = _keys(_dot(cb, wk_ref[...]), kpe)
    v_o[...] = _dot(cb, wv_ref[...]).astype(BF16)

    z = _dot_nt(h, win_ref[o_h:o_h + 5 * hw, :])
    hgx_o[:, 0:hw] = _silu(z[:, 0:hw])
    for dr in range(2):
        g0, g1 = gam_ref[dr, 0:1, :], gam_ref[dr, 1:2, :]
        gmax = jnp.maximum(g0, g1)
        e0, e1 = jnp.exp(g0 - gmax), jnp.exp(g1 - gmax)
        lb = e0 / (e0 + e1)
        f = lb + (1.0 - lb) * jax.nn.sigmoid(z[:, (1 + dr) * hw:(2 + dr) * hw])
        hgx_o[:, (1 + 2 * dr) * hw:(2 + 2 * dr) * hw] = jnp.log(f)
        hgx_o[:, (2 + 2 * dr) * hw:(3 + 2 * dr) * hw] = 1.0 - f
    hgx_o[:, 5 * hw:6 * hw] = z[:, 3 * hw:4 * hw]
    hgx_o[:, 6 * hw:7 * hw] = z[:, 4 * hw:5 * hw]


def _inproj(x2d, batch, seq, mod, gamma, wts, row0, per_batch, rope_tabs):
    n, d = x2d.shape
    tm = MIXER_SEQS * TOK_BLOCK if seq % (MIXER_SEQS * TOK_BLOCK) == 0 else TOK_BLOCK
    nblk = seq // tm
    rope = rope_tabs is not None
    hp = MLA_HEADS * LANES
    hw = HG_HEADS * HG_DK
    tok = lambda b, i: (b * nblk + i, 0)
    pos = lambda b, i: (i, 0)
    weights = [wts[k] for k in INPROJ_KEYS]
    ins = [x2d, mod, gamma] + weights
    in_specs = ([pl.BlockSpec((tm, d), tok), _const_spec(mod.shape), _const_spec(gamma.shape)]
                + [_const_spec(w.shape) for w in weights])
    if rope:
        ins += list(rope_tabs)
        in_specs += [pl.BlockSpec((tm, t.shape[1]), pos) for t in rope_tabs]
    widths = [(hp, BF16), (hp, BF16), (MLA_HEADS * MLA_V, BF16), (wts["kvn"].shape[1], F32), (LANES, F32),
              (7 * hw, F32)]
    return pl.pallas_call(
        functools.partial(_inproj_kernel, row0=row0, per_batch=per_batch, rope=rope),
        out_shape=[jax.ShapeDtypeStruct((n, w), dt) for w, dt in widths],
        grid=(batch, nblk),
        in_specs=in_specs,
        out_specs=[pl.BlockSpec((tm, w), tok) for w, _ in widths],
        compiler_params=_params("arbitrary", "arbitrary"),
        name="inproj",
    )(*ins)


def _kvup_kernel(ckv_ref, kpe_ref, wk_ref, wv_ref, k_o, v_o):
    cb = ckv_ref[...].astype(BF16)
    k_o[...] = _keys(_dot(cb, wk_ref[...]), kpe_ref[...])
    v_o[...] = _dot(cb, wv_ref[...]).astype(BF16)


def _kvup(ckv2d, kpe2d, wts):
    n = ckv2d.shape[0]
    tm = TOK_BLOCK
    widths = [MLA_HEADS * LANES, MLA_HEADS * MLA_V]
    row = lambda i: (i, 0)
    ws = [wts["wk"], wts["wv"]]
    return pl.pallas_call(
        _kvup_kernel,
        out_shape=[jax.ShapeDtypeStruct((n, w), BF16) for w in widths],
        grid=(n // tm,),
        in_specs=[pl.BlockSpec((tm, ckv2d.shape[1]), row), pl.BlockSpec((tm, LANES), row)]
                 + [_const_spec(w.shape) for w in ws],
        out_specs=[pl.BlockSpec((tm, w), row) for w in widths],
        compiler_params=_params("arbitrary"),
        name="kvup",
    )(ckv2d, kpe2d, *ws)


ATTN_SCALE = (MLA_NOPE + MLA_ROPE) ** -0.5


def _attn_body(q_ref, k_ref, v_ref, kc_ref, vc_ref, o_ref):
    cached = kc_ref is not None
    scale = ATTN_SCALE * math.log2(math.e)
    per_slab = LANES // MLA_V
    own = lax.broadcasted_iota(jnp.int32, (q_ref.shape[0], LANES), 1) // MLA_V
    for slab in range(MLA_HEADS // per_slab):
        vsl = slice(slab * LANES, (slab + 1) * LANES)
        out = None
        for sub in range(per_slab):
            hd = slab * per_slab + sub
            sl = slice(hd * LANES, (hd + 1) * LANES)
            q = q_ref[:, sl]
            s = _dot_nt(q, k_ref[:, sl])
            mx = jnp.max(s, axis=-1, keepdims=True)
            if cached:
                s2 = _dot_nt(q, kc_ref[:, sl])
                mx = jnp.maximum(mx, jnp.max(s2, axis=-1, keepdims=True))
            e = jnp.exp2((s - mx) * scale)
            den = jnp.sum(e, axis=-1, keepdims=True)
            o = _dot(e.astype(BF16), v_ref[:, vsl])
            if cached:
                e2 = jnp.exp2((s2 - mx) * scale)
                den = den + jnp.sum(e2, axis=-1, keepdims=True)
                o = o + _dot(e2.astype(BF16), vc_ref[:, vsl])
            o = o / den
            out = o if out is None else jnp.where(own == sub, o, out)
        o_ref[:, vsl] = out.astype(o_ref.dtype)


def _chunk_scan(x, reverse):
    tm = x.shape[0]
    rin = lax.broadcasted_iota(jnp.int32, x.shape, 0) % HG_CHUNK
    step = 1
    while step < HG_CHUNK:
        if reverse:
            x = x + jnp.where(rin < HG_CHUNK - step, pltpu.roll(x, tm - step, 0), 0.0)
        else:
            x = x + jnp.where(rin >= step, pltpu.roll(x, step, 0), 0.0)
        step *= 2
    return x


def _hgrn_kernel(*refs, has_init):
    fwd, bwd = refs[0:4], refs[4:8]
    refs = refs[8:]
    s0_ref = None
    if has_init:
        s0_ref = refs[0]
        refs = refs[1:]
    of_ref, ob_ref, sfin_ref, st_scr = refs
    i = pl.program_id(1)
    _hgrn_body([(fwd, bwd, of_ref, ob_ref, sfin_ref)], s0_ref, st_scr, i == 0, i == pl.num_programs(1) - 1)


def _hgrn_body(jobs, s0_ref, st_scr, first, last):
    tm = jobs[0][0][0].shape[0]
    c = HG_CHUNK
    nch = tm // c
    dk, dv = HG_DK, HG_DV
    hw = HG_HEADS * dk

    def initial(dr, hd):
        return s0_ref[0, dr, hd].T if s0_ref is not None else jnp.zeros((dv, dk), F32)

    if st_scr is not None:
        @pl.when(first)
        def _init():
            for dr in range(2):
                for hd in range(HG_HEADS):
                    st_scr[dr, hd] = initial(dr, hd)

    npair = nch // 2
    pair = 2 * c
    row = lax.broadcasted_iota(jnp.int32, (tm, tm), 0)
    col = lax.broadcasted_iota(jnp.int32, (tm, tm), 1)
    same = (row // c) == (col // c)
    same_pair = (row // pair) == (col // pair)
    bd = (lax.broadcasted_iota(jnp.int32, (tm, npair * dk), 0) // pair
          == lax.broadcasted_iota(jnp.int32, (tm, npair * dk), 1) // dk)
    chunk_odd = (lax.broadcasted_iota(jnp.int32, (tm, hw), 0) // c) % 2 == 1

    for dr in range(2):
        tri = same & ((col <= row) if dr == 0 else (col >= row))
        cross = same_pair & (((row // c) > (col // c)) if dr == 0 else ((row // c) < (col // c)))
        second = chunk_odd if dr == 0 else ~chunk_odd
        order = range(npair) if dr == 0 else range(npair - 1, -1, -1)

        def decayed(job):
            hq_ref, lf_ref, kk_ref, vv_ref = job[dr]
            bcum = _chunk_scan(lf_ref[...], reverse=dr == 1)
            closing = c - 1 if dr == 0 else 0
            btot3 = bcum.reshape(nch, c, hw)[:, closing:closing + 1, :]
            btot = jnp.broadcast_to(btot3, (nch, c, hw)).reshape(tm, hw)
            bpart = jnp.where(chunk_odd, pltpu.roll(btot, c, 0), pltpu.roll(btot, tm - c, 0))
            epart = jnp.exp(bpart)
            kk = kk_ref[...]
            qd = hq_ref[...] * jnp.exp(bcum)
            kd = kk * jnp.exp(-bcum)
            ke = kk * jnp.exp(btot - bcum)
            qd2 = jnp.where(second, qd * epart, qd)
            ke2 = jnp.where(second, ke, ke * epart)
            return qd, kd, ke, qd2, ke2, btot + bpart, vv_ref[...]

        def head(job, hd, qd, kd, ke, qd2, ke2, bpair, vv):
            o_ref, sfin_ref = job[2 + dr], job[4]
            sl = slice(hd * dk, (hd + 1) * dk)
            qd_h = qd[:, sl].astype(BF16)
            v_h = vv[:, hd * dv:(hd + 1) * dv]
            a = jnp.where(tri, _dot_nt(qd_h, kd[:, sl].astype(BF16)),
                          jnp.where(cross, _dot_nt(qd_h, ke[:, sl].astype(BF16)), 0.0))
            o_intra = _dot(a.astype(BF16), v_h.astype(BF16))
            kebd = jnp.where(bd, jnp.concatenate([ke2[:, sl]] * npair, axis=1), 0.0).astype(BF16)
            qbd = jnp.where(bd, jnp.concatenate([qd2[:, sl]] * npair, axis=1), 0.0).astype(BF16)
            ut = _dot(v_h.T.astype(BF16), kebd)
            st = st_scr[dr, hd] if st_scr is not None else initial(dr, hd)
            prev = [None] * npair
            for p in order:
                prev[p] = st
                st = st * jnp.exp(bpair[p * pair:p * pair + 1, sl]) + ut[:, p * dk:(p + 1) * dk]
            if st_scr is not None:
                st_scr[dr, hd] = st
            o_inter = _dot_nt(qbd, jnp.concatenate(prev, axis=1).astype(BF16))
            o_ref[:, hd * dv:(hd + 1) * dv] = o_intra + o_inter

            if last is True:
                sfin_ref[0, dr, hd] = st.T
            else:
                @pl.when(last)
                def _final():
                    sfin_ref[0, dr, hd] = st.T

        prepared = [decayed(job) for job in jobs]
        for hd in range(HG_HEADS):
            for job, arrays in zip(jobs, prepared):
                head(job, hd, *arrays)


def _hgrn(hgx, batch, seq, s0=None):
    n = hgx.shape[0]
    tm = TOK_BLOCK
    nblk = seq // tm
    hw = HG_HEADS * HG_DK

    def spec(lane_blk, rev):
        if rev:
            return pl.BlockSpec((tm, hw), lambda b, i: (b * nblk + nblk - 1 - i, lane_blk))
        return pl.BlockSpec((tm, hw), lambda b, i: (b * nblk + i, lane_blk))

    in_specs = [spec(0, False), spec(1, False), spec(2, False), spec(5, False),
                spec(0, True), spec(3, True), spec(4, True), spec(5, True)]
    ins = [hgx] * 8
    st_shape = (1, 2, HG_HEADS, HG_DK, HG_DV)
    st_spec = pl.BlockSpec(st_shape, lambda b, i: (b, 0, 0, 0, 0))
    if s0 is not None:
        ins.append(s0)
        in_specs.append(st_spec)
    return pl.pallas_call(
        functools.partial(_hgrn_kernel, has_init=s0 is not None),
        out_shape=[jax.ShapeDtypeStruct((n, hw), F32), jax.ShapeDtypeStruct((n, hw), F32),
                   jax.ShapeDtypeStruct((batch,) + st_shape[1:], F32)],
        grid=(batch, nblk),
        in_specs=in_specs,
        out_specs=[spec(0, False), spec(0, True), st_spec],
        scratch_shapes=[pltpu.VMEM((2, HG_HEADS, HG_DV, HG_DK), F32)],
        compiler_params=_params("arbitrary", "arbitrary"),
        name="hgrn",
    )(*ins)


def _layer_norm(x, g, b):
    xc = x - jnp.mean(x, axis=-1, keepdims=True)
    var = jnp.mean(xc * xc, axis=-1, keepdims=True)
    return xc * lax.rsqrt(var + EPS) * g + b


N_POSTMIX_WEIGHTS = 8
INPROJ_KEYS = ("win", "qn", "wuq", "kvn", "wk", "wv")
POSTMIX_KEYS = ("hgn", "womla", "wohg", "wout", "ln1g", "ln1b", "wr")
MIXER_SEQS = 2


def _postmix_kernel(x_ref, mod_ref, *refs, alpha, row0, per_batch, cached):
    n_attn = 5 if cached else 3
    q_ref, k_ref, v_ref = refs[:3]
    kc_ref, vc_ref = refs[3:5] if cached else (None, None)
    of_ref, ob_ref, zg_ref = refs[n_attn:n_attn + 3]
    om_s = refs[-1]
    _attn_body(q_ref, k_ref, v_ref, kc_ref, vc_ref, om_s)
    _postmix_body(x_ref, _mod_row(mod_ref, row0, per_batch), of_ref, ob_ref, zg_ref, om_s,
                  *refs[n_attn + 3:-1], alpha=alpha)


def _postmix_body(x_ref, m, of_ref, ob_ref, zg_ref, om_ref, wg_ref, hgn_ref, womla_ref,
                  wohg_ref, wout_ref, lng_ref, lnb_ref, wr_ref, x1_o, h2_o, aff_o, *, alpha):
    d = x_ref.shape[1]
    tb = aff_o.shape[2]
    g1, sh2, sc2 = m[:, 2 * d:3 * d], m[:, 3 * d:4 * d], m[:, 4 * d:5 * d]
    o = of_ref[...] + ob_ref[...]
    zg = zg_ref[...]
    parts = []
    for hd in range(HG_HEADS):
        sl = slice(hd * HG_DV, (hd + 1) * HG_DV)
        parts.append(_rms(o[:, sl], hgn_ref[...]) * _silu(zg[:, sl]))
    ohg = jnp.concatenate(parts, axis=1).astype(BF16)
    gates = _dot_nt(_modulated(x_ref, m), wg_ref[wg_ref.shape[0] - 2 * d:, :])
    merged = (jax.nn.sigmoid(gates[:, 0:d]) * _dot(om_ref[...], womla_ref[...])
              + jax.nn.sigmoid(gates[:, d:2 * d]) * _dot(ohg, wohg_ref[...]))
    mix = _dot(merged.astype(BF16), wout_ref[...])
    x1 = _layer_norm(alpha * x_ref[...] + g1 * mix, lng_ref[...], lnb_ref[...])
    x1_o[...] = x1
    h2 = (x1 * (1.0 + sc2) + sh2).astype(BF16)
    h2_o[...] = h2
    logits = _dot_nt(wr_ref[...], h2)
    e = jnp.exp(logits - jnp.max(logits, axis=0, keepdims=True))
    aff = e / jnp.sum(e, axis=0, keepdims=True)
    for blk in range(aff_o.shape[0]):
        aff_o[blk] = aff[:, blk * tb:(blk + 1) * tb]


def _postmix(x2d, batch, seq, mod, q, k, v, cache, o_f, o_b, hgx, wts, alpha, n_experts, row0, per_batch):
    n, d = x2d.shape
    tm = TOK_BLOCK
    nblk = seq // tm
    hw = HG_HEADS * HG_DV
    hp, hv = q.shape[1], v.shape[1]
    tok = lambda b, i: (b * nblk + i, 0)
    per_seq = lambda b, i: (b, 0)
    weights = [wts[k] for k in POSTMIX_KEYS]
    attn_ins = [q, k, v]
    attn_specs = [pl.BlockSpec((tm, hp), tok), pl.BlockSpec((seq, hp), per_seq), pl.BlockSpec((seq, hv), per_seq)]
    if cache is not None:
        past = cache[0].shape[0] // batch
        attn_ins += list(cache)
        attn_specs += [pl.BlockSpec((past, hp), per_seq), pl.BlockSpec((past, hv), per_seq)]
    return pl.pallas_call(
        functools.partial(_postmix_kernel, alpha=alpha, row0=row0, per_batch=per_batch,
                          cached=cache is not None),
        out_shape=[jax.ShapeDtypeStruct((n, d), F32), jax.ShapeDtypeStruct((n, d), BF16),
                   jax.ShapeDtypeStruct((n // tm, n_experts, tm), F32)],
        grid=(batch, nblk),
        in_specs=[pl.BlockSpec((tm, d), tok), _const_spec(mod.shape)] + attn_specs
                 + [pl.BlockSpec((tm, hw), tok), pl.BlockSpec((tm, hw), tok),
                    pl.BlockSpec((tm, hw), lambda b, i: (b * nblk + i, 6)), _const_spec(wts["win"].shape)]
                 + [_const_spec(w.shape) for w in weights],
        out_specs=[pl.BlockSpec((tm, d), tok), pl.BlockSpec((tm, d), tok),
                   pl.BlockSpec((1, n_experts, tm), lambda b, i: (b * nblk + i, 0, 0))],
        scratch_shapes=[pltpu.VMEM((tm, hv), BF16)],
        compiler_params=_params("arbitrary", "arbitrary"),
        name="postmix",
    )(x2d, mod, *attn_ins, o_f, o_b, hgx, wts["win"], *weights)


def _mixer_kernel(x_ref, mod_ref, *refs, alpha, row0, seq):
    nw = 1 + N_INPROJ_WEIGHTS
    in_w, refs = refs[:nw], refs[nw:]
    pm_w, refs = refs[:N_POSTMIX_WEIGHTS - 1], refs[N_POSTMIX_WEIGHTS - 1:]
    pm_w = (in_w[1],) + tuple(pm_w)
    x1_o, h2_o, aff_o, ckv_o, kpe_o, sfin_o, q_s, k_s, v_s, hgx_s, om_s, of_s, ob_s = refs
    m = _mod_row(mod_ref, row0, False)
    _inproj_body(x_ref, m, *in_w, q_s, k_s, v_s, ckv_o, kpe_o, hgx_s, rope=False)
    hw = HG_HEADS * HG_DK
    jobs = []
    for s in range(x_ref.shape[0] // seq):
        rows = slice(s * seq, (s + 1) * seq)
        _attn_body(q_s.at[rows], k_s.at[rows], v_s.at[rows], None, None, om_s.at[rows])
        lane = lambda j, rows=rows: hgx_s.at[rows, j * hw:(j + 1) * hw]
        jobs.append(((lane(0), lane(1), lane(2), lane(5)), (lane(0), lane(3), lane(4), lane(5)),
                     of_s.at[rows], ob_s.at[rows], sfin_o.at[s:s + 1]))
    _hgrn_body(jobs, None, None, True, True)
    _postmix_body(x_ref, m, of_s, ob_s, hgx_s.at[:, 6 * hw:7 * hw], om_s, *pm_w, x1_o, h2_o, aff_o,
                  alpha=alpha)


def _mixer(x2d, batch, seq, mod, gamma, wts, alpha, n_experts, row0):
    n, d = x2d.shape
    assert seq == TOK_BLOCK
    ns = MIXER_SEQS if batch % MIXER_SEQS == 0 else 1
    tm = ns * seq
    hp = MLA_HEADS * LANES
    hv = MLA_HEADS * MLA_V
    hw = HG_HEADS * HG_DK
    kvl = wts["kvn"].shape[1]
    weights = [wts[k] for k in INPROJ_KEYS + POSTMIX_KEYS]
    tok = lambda b: (b, 0)
    st_shape = (ns, 2, HG_HEADS, HG_DK, HG_DV)
    return pl.pallas_call(
        functools.partial(_mixer_kernel, alpha=alpha, row0=row0, seq=seq),
        out_shape=[jax.ShapeDtypeStruct((n, d), F32), jax.ShapeDtypeStruct((n, d), BF16),
                   jax.ShapeDtypeStruct((n // seq, n_experts, seq), F32),
                   jax.ShapeDtypeStruct((n, kvl), F32), jax.ShapeDtypeStruct((n, LANES), F32),
                   jax.ShapeDtypeStruct((batch,) + st_shape[1:], F32)],
        grid=(batch // ns,),
        in_specs=[pl.BlockSpec((tm, d), tok), _const_spec(mod.shape), _const_spec(gamma.shape)]
                 + [_const_spec(w.shape) for w in weights],
        out_specs=[pl.BlockSpec((tm, d), tok), pl.BlockSpec((tm, d), tok),
                   pl.BlockSpec((ns, n_experts, seq), lambda b: (b, 0, 0)),
                   pl.BlockSpec((tm, kvl), tok), pl.BlockSpec((tm, LANES), tok),
                   pl.BlockSpec(st_shape, lambda b: (b, 0, 0, 0, 0))],
        scratch_shapes=[pltpu.VMEM((tm, hp), BF16), pltpu.VMEM((tm, hp), BF16), pltpu.VMEM((tm, hv), BF16),
                        pltpu.VMEM((tm, 7 * hw), F32), pltpu.VMEM((tm, hv), BF16),
                        pltpu.VMEM((tm, hw), F32), pltpu.VMEM((tm, hw), F32)],
        compiler_params=_params("arbitrary"),
        name="mixer",
    )(x2d, mod, gamma, *weights)


def _route_kernel(*refs, caps):
    ng = len(caps)
    for aff_ref, rank_o, cnt_o, cap in zip(refs[0:ng], refs[ng:2 * ng], refs[2 * ng:3 * ng], caps):
        _route_group(aff_ref, rank_o, cnt_o, cap)


def _route_group(aff_ref, rank_o, cnt_o, cap):
    nb, ne, tb = aff_ref.shape
    key = aff_ref[...]

    def count(mask):
        return jnp.sum(jnp.sum(jnp.where(mask, 1.0, 0.0), axis=0), axis=1, keepdims=True)

    def bit_step(it, bits):
        cand = bits | jnp.left_shift(jnp.int32(1), 30 - it)
        return jnp.where(count(key >= pltpu.bitcast(cand, F32)[None]) >= cap, cand, bits)

    bits = lax.fori_loop(0, 31, bit_step, jnp.zeros((ne, 1), jnp.int32))
    thr = pltpu.bitcast(bits, F32)
    need = cap - count(key > thr[None])
    before = (lax.broadcasted_iota(jnp.int32, (tb, tb), 0)
              < lax.broadcasted_iota(jnp.int32, (tb, tb), 1))
    before = jnp.where(before, 1.0, 0.0).astype(BF16)
    off_eq = jnp.zeros((ne, 1), F32)
    off_sel = jnp.zeros((ne, 1), F32)
    cnt_o[...] = jnp.zeros_like(cnt_o)
    for blk in range(nb):
        key_b = key[blk]
        eq = key_b == thr
        eq_b = jnp.where(eq, 1.0, 0.0)
        eq_rank = _dot(eq_b.astype(BF16), before) + off_eq
        sel = (key_b > thr) | (eq & (eq_rank < need))
        sel_b = jnp.where(sel, 1.0, 0.0)
        rank = _dot(sel_b.astype(BF16), before) + off_sel
        rank_o[blk] = jnp.where(sel, rank.astype(jnp.int32), UNSELECTED)
        cnt_o[:, blk:blk + 1] = off_sel.astype(jnp.int32)
        off_eq = off_eq + jnp.sum(eq_b, axis=1, keepdims=True)
        off_sel = off_sel + jnp.sum(sel_b, axis=1, keepdims=True)
    cnt_o[:, nb:nb + 1] = off_sel.astype(jnp.int32)


def _route(affs, caps):
    ng = len(affs)
    ne = affs[0].shape[1]
    assert all(a.shape[0] + 1 <= LANES for a in affs)
    outs = pl.pallas_call(
        functools.partial(_route_kernel, caps=tuple(caps)),
        out_shape=[jax.ShapeDtypeStruct(a.shape, jnp.int32) for a in affs]
                  + [jax.ShapeDtypeStruct((ne, LANES), jnp.int32)] * ng,
        in_specs=[pl.BlockSpec(memory_space=pltpu.VMEM)] * ng,
        out_specs=[pl.BlockSpec(memory_space=pltpu.VMEM)] * (2 * ng),
        compiler_params=pltpu.CompilerParams(vmem_limit_bytes=VMEM_LIMIT),
        name="route",
    )(*affs)
    return [(outs[gi], outs[ng + gi][:, :affs[gi].shape[0] + 1]) for gi in range(ng)]


def _window_hits(rk_ref, firsts, slot0, win):
    ne, tb = rk_ref.shape[1], rk_ref.shape[2]
    win_iota = lax.broadcasted_iota(jnp.int32, (win, tb), 0)
    return [(rk_ref[0, e:e + 1, :] + (slot0 - firsts[e])) == win_iota for e in range(ne)]


def _compact_kernel(first_ref, end_ref, rounds_ref, *refs, groups, slots):
    ng = len(groups)
    h2_refs, rk_refs, af_refs = refs[0:ng], refs[ng:2 * ng], refs[2 * ng:3 * ng]
    xe_hbm, stage, tail, sem, issued = refs[3 * ng:]
    hbms, stages, tails = [xe_hbm], [stage], [tail]
    b = pl.program_id(0)
    ne = rk_refs[0].shape[1]
    win = WIN_ROWS
    sub = BF16_ROWS

    def copies(slot, dsts):
        return [pltpu.make_async_copy(stage.at[slot, pl.ds(e * win, win), :],
                                      hbm.at[e, pl.ds(pl.multiple_of(dsts[e], sub), win), :], sem.at[e])
                for stage, hbm in zip(stages, hbms) for e in range(ne)]

    def wait_previous():
        @pl.when(issued[0] > 0)
        def _():
            for cp in copies(0, [0] * ne):
                cp.wait()

    @pl.when(b == 0)
    def _init():
        issued[0] = 0
        for stage, tail in zip(stages, tails):
            tail[...] = jnp.zeros_like(tail)
            stage[1] = jnp.zeros(stage.shape[1:], stage.dtype)
        pad = copies(1, [slots] * ne)
        for cp in pad:
            cp.start()
        for cp in pad:
            cp.wait()

    def group_body(h2_ref, rk_ref, af_ref, slot0):
        firsts = [first_ref[b * ne + e] for e in range(ne)]
        bases = [(f // sub) * sub for f in firsts]
        ends = [end_ref[b * ne + e] - bases[e] for e in range(ne)]

        def one_round(r, carry):
            dsts = [bases[e] + r * win for e in range(ne)]
            hits = _window_hits(rk_ref, dsts, slot0, win)
            onehot = jnp.where(jnp.concatenate(hits, axis=0), 1.0, 0.0).astype(BF16)
            gate = jnp.concatenate(
                [jnp.sum(jnp.where(hits[e], af_ref[0, e:e + 1, :], 0.0), axis=1, keepdims=True)
                 for e in range(ne)], axis=0)
            slot = issued[0] % 2
            gate = jnp.broadcast_to(gate, (ne * win, LANES))
            g1 = gate.astype(BF16).astype(F32)
            g2 = (gate - g1).astype(BF16).astype(F32)
            g3 = (gate - g1 - g2).astype(BF16).astype(F32)
            lane = lax.broadcasted_iota(jnp.int32, (ne * win, LANES), 1)
            pieces3 = jnp.where(lane == 0, g1, jnp.where(lane == 1, g2, jnp.where(lane == 2, g3, 0.0)))
            streams = (jnp.concatenate([_dot(onehot, h2_ref[...]).astype(BF16),
                                        pieces3.astype(BF16)], axis=1),)
            for val, stage, tail in zip(streams, stages, tails):
                sub_iota = lax.broadcasted_iota(jnp.int32, (sub, val.shape[1]), 0)
                pieces = []
                for e in range(ne):
                    old = tail[e * sub:(e + 1) * sub, :]
                    shared = jnp.where(r == 0, firsts[e] - bases[e], 0)
                    groups_e = [val[e * win + g * sub:e * win + (g + 1) * sub, :] for g in range(win // sub)]
                    groups_e[0] = jnp.where(sub_iota < shared, old, groups_e[0])
                    pieces += groups_e
                    last = (ends[e] // sub) * sub
                    new = old
                    for g, grp in enumerate(groups_e):
                        new = jnp.where((r == last // win) & (last % win == g * sub), grp, new)
                    tail[e * sub:(e + 1) * sub, :] = new
                stage[slot] = jnp.concatenate(pieces, axis=0)
            wait_previous()
            for cp in copies(slot, [jnp.minimum(dst, slots) for dst in dsts]):
                cp.start()
            issued[0] = issued[0] + 1
            return carry

        lax.fori_loop(0, rounds_ref[b], one_round, 0)

    blk0 = 0
    for gi, g in enumerate(groups):
        @pl.when((b >= blk0) & (b < blk0 + g["nb"]))
        def _(gi=gi, g=g):
            group_body(h2_refs[gi], rk_refs[gi], af_refs[gi], g["slot0"])
        blk0 += g["nb"]

    @pl.when(b == pl.num_programs(0) - 1)
    def _drain():
        wait_previous()


def _compact(groups, first, end, rounds, slots):
    d = groups[0]["h2"].shape[1]
    nbs = [g["rank"].shape[0] for g in groups]
    ne, tb = groups[0]["rank"].shape[1:]
    meta, specs_h2, specs_rk = [], [], []
    blk0 = 0
    for g, nb in zip(groups, nbs):
        meta.append(dict(nb=nb, slot0=g["slot0"]))
        local = lambda b, *_, blk0=blk0, nb=nb: jnp.clip(b - blk0, 0, nb - 1)
        specs_h2.append(pl.BlockSpec((tb, d), lambda b, *_, local=local: (local(b), 0)))
        specs_rk.append(pl.BlockSpec((1, ne, tb), lambda b, *_, local=local: (local(b), 0, 0)))
        blk0 += nb
    width = d + LANES
    return pl.pallas_call(
        functools.partial(_compact_kernel, groups=meta, slots=slots),
        out_shape=jax.ShapeDtypeStruct((ne, slots + WIN_ROWS, width), BF16),
        grid_spec=pltpu.PrefetchScalarGridSpec(
            num_scalar_prefetch=3,
            grid=(sum(nbs),),
            in_specs=specs_h2 + specs_rk + specs_rk,
            out_specs=pl.BlockSpec(memory_space=pl.ANY),
            scratch_shapes=[pltpu.VMEM((2, ne * WIN_ROWS, width), BF16),
                            pltpu.VMEM((ne * BF16_ROWS, width), BF16),
                            pltpu.SemaphoreType.DMA((ne,)), pltpu.SMEM((1,), jnp.int32)]),
        compiler_params=_params("arbitrary"),
        name="compact",
    )(first, end, rounds, *[g["h2"] for g in groups], *[g["rank"] for g in groups],
      *[g["aff"] for g in groups])


def _ffn_kernel(xe_ref, w1_ref, w3_ref, w2_ref, ye_ref, *scratch):
    f = pl.program_id(1)
    d = w1_ref.shape[1]
    x = xe_ref[0, :, 0:d]
    pieces = xe_ref[0, :, d:d + LANES].astype(F32)
    gate = pieces[:, 0:1] + pieces[:, 1:2] + pieces[:, 2:3]
    hid = _silu(_dot(x, w1_ref[0].astype(BF16))) * _dot(x, w3_ref[0].astype(BF16))
    y = _dot(hid.astype(BF16), w2_ref[0].astype(BF16))
    if not scratch:
        ye_ref[0] = (y * gate).astype(ye_ref.dtype)
        return
    acc_scr, = scratch
    last = pl.num_programs(1) - 1

    @pl.when(f == 0)
    def _first():
        acc_scr[...] = y

    @pl.when((f > 0) & (f < last))
    def _middle():
        acc_scr[...] += y

    @pl.when(f == last)
    def _last():
        ye_ref[0] = ((acc_scr[...] + y) * gate).astype(ye_ref.dtype)


def _ffn(xe, w1, w3, w2, slots, ft):
    ne, d, dff = w1.shape
    nf = dff // ft
    return pl.pallas_call(
        _ffn_kernel,
        out_shape=jax.ShapeDtypeStruct((ne, slots, d), BF16),
        grid=(ne, nf),
        in_specs=[pl.BlockSpec((1, slots, xe.shape[2]), lambda e, f: (e, 0, 0)),
                  pl.BlockSpec((1, d, ft), lambda e, f: (e, 0, f)),
                  pl.BlockSpec((1, d, ft), lambda e, f: (e, 0, f)),
                  pl.BlockSpec((1, ft, d), lambda e, f: (e, f, 0))],
        out_specs=pl.BlockSpec((1, slots, d), lambda e, f: (e, 0, 0)),
        scratch_shapes=[pltpu.VMEM((slots, d), F32)] if nf > 1 else [],
        compiler_params=_params("arbitrary", "arbitrary"),
        name="ffn",
    )(xe, w1, w3, w2)


def _combine_kernel(first_ref, rounds_ref, *refs, groups, alpha, slots):
    ng = len(groups)
    rk_refs, x1_refs = refs[0:ng], refs[ng:2 * ng]
    mod_ref, lng_ref, lnb_ref, ye_hbm = refs[2 * ng:2 * ng + 4]
    out_refs = refs[2 * ng + 4:3 * ng + 4]
    buf, acc_scr, sem = refs[3 * ng + 4:]
    d = x1_refs[0].shape[1]
    b = pl.program_id(0)
    nblk = pl.num_programs(0)
    ne, tb = rk_refs[0].shape[1], rk_refs[0].shape[2]
    win = WIN_ROWS

    def starts_of(blk, r):
        firsts = [(first_ref[blk * ne + e] // BF16_ROWS) * BF16_ROWS + r * win for e in range(ne)]
        return firsts, [jnp.minimum(f, slots - win) for f in firsts]

    def windows(slot, starts):
        return [pltpu.make_async_copy(ye_hbm.at[e, pl.ds(pl.multiple_of(starts[e], BF16_ROWS), win), :],
                                      buf.at[slot, pl.ds(e * win, win), :], sem.at[slot, e])
                for e in range(ne)]

    def scatter(rk_ref, slot0, slot, firsts, starts, later_round):
        hits = _window_hits(rk_ref, starts, slot0, win)
        if later_round:
            hits = [h & ((rk_ref[0, e:e + 1, :] + slot0) >= firsts[e]) for e, h in enumerate(hits)]
        hit = jnp.where(jnp.concatenate(hits, axis=0), 1.0, 0.0)
        return _dot(hit.T.astype(BF16), buf[slot])

    cur = b % 2

    @pl.when(b == 0)
    def _prime():
        for cp in windows(0, starts_of(0, 0)[1]):
            cp.start()

    @pl.when(b + 1 < nblk)
    def _prefetch():
        for cp in windows(1 - cur, starts_of(b + 1, 0)[1]):
            cp.start()

    firsts0, starts0 = starts_of(b, 0)
    for cp in windows(cur, starts0):
        cp.wait()

    def group_body(rk_ref, x1_ref, out_ref, g, local):
        acc_scr[...] = scatter(rk_ref, g["slot0"], cur, firsts0, starts0, False)

        def extra_round(r, carry):
            firsts, starts = starts_of(b, r)
            for cp in windows(2, starts):
                cp.start()
            for cp in windows(2, starts):
                cp.wait()
            acc_scr[...] += scatter(rk_ref, g["slot0"], 2, firsts, starts, True)
            return carry

        lax.fori_loop(1, rounds_ref[b], extra_round, 0)
        r = g["row0"] + local // g["blocks_per_batch"]
        g2 = mod_ref[pl.ds(r, 1), :][:, 5 * d:6 * d]
        out_ref[...] = _layer_norm(alpha * x1_ref[...] + g2 * acc_scr[...], lng_ref[...], lnb_ref[...])

    blk0 = 0
    for gi, g in enumerate(groups):
        @pl.when((b >= blk0) & (b < blk0 + g["nb"]))
        def _(gi=gi, g=g, blk0=blk0):
            group_body(rk_refs[gi], x1_refs[gi], out_refs[gi], g, b - blk0)
        blk0 += g["nb"]


def _combine(ye, groups, first, rounds, mod, ln_g, ln_b, alpha):
    d = groups[0]["x1"].shape[1]
    ne, tb = groups[0]["rank"].shape[1:]
    slots = ye.shape[1]
    meta, specs_rk, specs_x1 = [], [], []
    blk0 = 0
    for g in groups:
        nb = g["rank"].shape[0]
        meta.append(dict(nb=nb, slot0=g["slot0"], row0=g["row0"], blocks_per_batch=g["blocks_per_batch"]))
        local = lambda b, *_, blk0=blk0, nb=nb: jnp.clip(b - blk0, 0, nb - 1)
        specs_rk.append(pl.BlockSpec((1, ne, tb), lambda b, *_, local=local: (local(b), 0, 0)))
        specs_x1.append(pl.BlockSpec((tb, d), lambda b, *_, local=local: (local(b), 0)))
        blk0 += nb
    const = lambda shape: pl.BlockSpec(shape, lambda b, *_: (0,) * len(shape))
    return pl.pallas_call(
        functools.partial(_combine_kernel, groups=meta, alpha=alpha, slots=slots),
        out_shape=[jax.ShapeDtypeStruct(g["x1"].shape, F32) for g in groups],
        grid_spec=pltpu.PrefetchScalarGridSpec(
            num_scalar_prefetch=2,
            grid=(blk0,),
            in_specs=specs_rk + specs_x1 + [const(mod.shape), const((1, d)), const((1, d)),
                                            pl.BlockSpec(memory_space=pl.ANY)],
            out_specs=specs_x1,
            scratch_shapes=[pltpu.VMEM((3, ne * WIN_ROWS, d), ye.dtype), pltpu.VMEM((tb, d), F32),
                            pltpu.SemaphoreType.DMA((3, ne))]),
        compiler_params=_params("arbitrary"),
        name="combine",
    )(first, rounds, *[g["rank"] for g in groups], *[g["x1"] for g in groups], mod, ln_g, ln_b, ye)


def _prep_weights(w_in, q_norm, w_uq, kv_norm, w_ukv, w_o_mla, hgrn_norm, w_o_hg, w_out, ln1_g, ln1_b,
                  w_router):
    d = w_in.shape[0]
    q_lora, kv_lora = q_norm.shape[0], kv_norm.shape[0]
    hw = HG_HEADS * HG_DK
    hh, hp = MLA_HEADS, MLA_HEADS * LANES
    o_kv, o_pe = q_lora, q_lora + kv_lora
    o_h = o_pe + MLA_ROPE
    o_g = o_h + 5 * hw
    assert w_in.shape[1] == o_g + 2 * d
    qk = MLA_NOPE + MLA_ROPE
    kvw = MLA_NOPE + MLA_V
    b16 = lambda a: a.astype(BF16)
    assert all(o % BF16_ROWS == 0 for o in (o_kv, o_pe, o_h, o_g))
    win = b16(w_in.T)
    wuq = jnp.pad(w_uq.reshape(q_lora, hh, qk), ((0, 0), (0, 0), (0, LANES - qk))).reshape(q_lora, hp)
    ukv = w_ukv.reshape(kv_lora, hh, kvw)
    wk = jnp.pad(ukv[:, :, :MLA_NOPE], ((0, 0), (0, 0), (0, LANES - MLA_NOPE))).reshape(kv_lora, hp)
    wv = ukv[:, :, MLA_NOPE:].reshape(kv_lora, hh * MLA_V)
    return dict(
        win=win, qn=q_norm.reshape(1, -1), wuq=b16(wuq), kvn=kv_norm.reshape(1, -1), wk=b16(wk), wv=b16(wv),
        hgn=hgrn_norm.reshape(1, -1), womla=b16(w_o_mla), wohg=b16(w_o_hg), wout=b16(w_out),
        ln1g=ln1_g.reshape(1, -1), ln1b=ln1_b.reshape(1, -1), wr=b16(w_router.T))


def _rope_tables(seq):
    n_freq = MLA_ROPE // 4
    inv = ROPE_BASE ** (-np.arange(n_freq, dtype=np.float64) / n_freq)
    t = np.arange(seq)
    ang = np.concatenate([(t // GRID_W)[:, None] * inv, (t % GRID_W)[:, None] * inv], axis=-1)
    cos = np.repeat(np.cos(ang), 2, axis=1)
    sin = np.repeat(np.sin(ang), 2, axis=1) * np.tile([-1.0, 1.0], MLA_ROPE // 2)
    ck = np.pad(cos, ((0, 0), (0, LANES - MLA_ROPE)), constant_values=1.0)
    sk = np.pad(sin, ((0, 0), (0, LANES - MLA_ROPE)))
    cq = np.pad(cos, ((0, 0), (MLA_NOPE, LANES - MLA_NOPE - MLA_ROPE)), constant_values=1.0)
    sq = np.pad(sin, ((0, 0), (MLA_NOPE, LANES - MLA_NOPE - MLA_ROPE)))
    return tuple(jnp.asarray(a, F32) for a in (cq, sq, ck, sk))


def _window_sched(cnt, slot0):
    first = slot0 + cnt[:, :-1]
    end = slot0 + cnt[:, 1:]
    flat = lambda a: a.T.reshape(-1).astype(jnp.int32)
    rounds = jnp.max((end - (first // BF16_ROWS) * BF16_ROWS + WIN_ROWS - 1) // WIN_ROWS, axis=0)
    return flat(first), flat(end), jnp.maximum(rounds, 1).astype(jnp.int32)


def kernel(x_prompt, x_sample, c, cache_ckv, cache_kpe, state_hgrn, c_ctx, w_ada, b_ada, w_in, mla_q_norm, mla_w_uq, mla_kv_norm, mla_w_ukv, mla_w_o, hgrn_gamma, hgrn_norm, hgrn_w_o, w_out, ln1_g, ln1_b, moe_w_router, moe_w1, moe_w3, moe_w2, ln2_g, ln2_b):
    depth = w_ada.shape[0]
    assert depth == 1, "single trunk layer"
    bp, tp, d = x_prompt.shape
    bs, tsq, _ = x_sample.shape
    ne = moe_w_router.shape[-1]
    alpha = (2 * depth) ** 0.25
    past = cache_ckv.shape[2]
    assert tp % TOK_BLOCK == 0 and tsq % TOK_BLOCK == 0 and past % TOK_BLOCK == 0 and tsq % GRID_W == 0

    wts = _prep_weights(w_in[0], mla_q_norm[0], mla_w_uq[0], mla_kv_norm[0], mla_w_ukv[0], mla_w_o[0],
                        hgrn_norm[0], hgrn_w_o[0], w_out[0], ln1_g[0], ln1_b[0], moe_w_router[0])
    cond_rows = -(-(1 + bs) // SUBLANES) * SUBLANES
    cond = jnp.concatenate([c_ctx[None], c, jnp.zeros((cond_rows - 1 - bs, d), F32)], axis=0)
    mod = _adaln(cond, w_ada[0], b_ada[0])

    xs = [x_prompt.reshape(bp * tp, d), x_sample.reshape(bs * tsq, d)]
    dims = [(bp, tp), (bs, tsq)]
    rows = [(0, False), (1, True)]
    ropes = [None, _rope_tables(tsq)]
    kpe_c = jnp.pad(cache_kpe[:, 0].reshape(bs * past, MLA_ROPE), ((0, 0), (0, LANES - MLA_ROPE)))
    caches = [None, _kvup(cache_ckv[:, 0].reshape(bs * past, -1), kpe_c, wts)]
    inits = [None, state_hgrn[:, 0]]

    x1s, h2s, affs, extras = [], [], [], []
    for gi in range(2):
        (bt, sq), (row0, per_batch) = dims[gi], rows[gi]
        if sq == TOK_BLOCK and caches[gi] is None and ropes[gi] is None and not per_batch:
            x1, h2, aff, ckv, kpe, s_fin = _mixer(xs[gi], bt, sq, mod, hgrn_gamma, wts, alpha, ne, row0)
        else:
            q, k, v, ckv, kpe, hgx = _inproj(xs[gi], bt, sq, mod, hgrn_gamma, wts, row0, per_batch, ropes[gi])
            o_f, o_b, s_fin = _hgrn(hgx, bt, sq, inits[gi])
            x1, h2, aff = _postmix(xs[gi], bt, sq, mod, q, k, v, caches[gi], o_f, o_b, hgx, wts, alpha, ne,
                                   row0, per_batch)
        x1s.append(x1)
        h2s.append(h2)
        affs.append(aff)
        extras.append((ckv, kpe, s_fin))

    caps = [EC_FACTOR * x.shape[0] // ne for x in xs]
    slots = sum(caps)
    assert all(cp % BF16_ROWS == 0 for cp in caps) and slots >= WIN_ROWS
    groups, scheds = [], []
    slot0 = 0
    routed = _route(affs, caps)
    for gi in range(2):
        rank, cnt = routed[gi]
        groups.append(dict(h2=h2s[gi], rank=rank, aff=affs[gi], slot0=slot0, x1=x1s[gi], row0=rows[gi][0],
                           blocks_per_batch=dims[gi][1] // TOK_BLOCK if rows[gi][1] else 1 << 30))
        scheds.append(_window_sched(cnt, slot0))
        slot0 += caps[gi]
    first, end, rounds = [jnp.concatenate([s[k] for s in scheds]) for k in range(3)]
    xe = _compact(groups, first, end, rounds, slots)
    ye = _ffn(xe, moe_w1[0], moe_w3[0], moe_w2[0], slots, ft=moe_w1.shape[-1])
    outs = _combine(ye, groups, first, rounds, mod, ln2_g[0].reshape(1, -1), ln2_b[0].reshape(1, -1), alpha)

    ckv_p, kpe_p, st_p = extras[0]
    y_prompt = outs[0].reshape(bp, tp, d)
    y_sample = outs[1].reshape(bs, tsq, d)
    new_ckv = ckv_p.reshape(bp, 1, tp, -1)
    new_kpe = kpe_p[:, :MLA_ROPE].reshape(bp, 1, tp, MLA_ROPE)
    new_state = st_p.reshape(bp, 1, 2, HG_HEADS, HG_DK, HG_DV)
    return (y_prompt, y_sample, new_ckv, new_kpe, new_state)
```

```python
import functools
import math

import jax
import jax.numpy as jnp
import numpy as np
from jax import lax
from jax.experimental import pallas as pl
from jax.experimental.pallas import tpu as pltpu

F32 = jnp.float32
BF16 = jnp.bfloat16

MLA_HEADS = 8
MLA_NOPE = 64
MLA_ROPE = 32
MLA_V = 64
HG_HEADS = 4
HG_DK = 128
HG_DV = 128
HG_CHUNK = 32
GRID_W = 64
ROPE_BASE = 10000.0
EC_FACTOR = 2
EPS = 1e-6

LANES = 128
SUBLANES = 8
BF16_ROWS = 16
VMEM_LIMIT = 56 * 1024 * 1024

TOK_BLOCK = 256
WIN_ROWS = 64
UNSELECTED = -(1 << 30)

NT_DIMS = (((1,), (1,)), ((), ()))


def _dot(a, b):
    return jnp.dot(a, b, preferred_element_type=F32)


def _dot_nt(a, b):
    return lax.dot_general(a, b, NT_DIMS, preferred_element_type=F32)


def _silu(x):
    return x * jax.nn.sigmoid(x)


def _params(*sem):
    return pltpu.CompilerParams(dimension_semantics=sem, vmem_limit_bytes=VMEM_LIMIT)


def _const_spec(shape):
    zeros = (0,) * len(shape)
    return pl.BlockSpec(shape, lambda *_: zeros, pipeline_mode=pl.Buffered(1))


def _adaln_kernel(c_ref, w_ref, b_ref, o_ref):
    s = _silu(c_ref[...]).astype(BF16)
    o_ref[...] = _dot(s, w_ref[...].astype(BF16)) + b_ref[...]


def _adaln(cond, w_ada, b_ada):
    rows, d = cond.shape
    n = w_ada.shape[1]
    tn = n // 4
    return pl.pallas_call(
        _adaln_kernel,
        out_shape=jax.ShapeDtypeStruct((rows, n), F32),
        grid=(n // tn,),
        in_specs=[_const_spec((rows, d)),
                  pl.BlockSpec((d, tn), lambda j: (0, j)),
                  pl.BlockSpec((1, tn), lambda j: (0, j))],
        out_specs=pl.BlockSpec((rows, tn), lambda j: (0, j)),
        compiler_params=_params("arbitrary"),
        name="adaln",
    )(cond, w_ada, b_ada.reshape(1, n))


def _rms(x, g):
    return x * lax.rsqrt(jnp.mean(x * x, axis=-1, keepdims=True) + EPS) * g


def _rope(x, c, s):
    w = x.shape[-1]
    lane = lax.broadcasted_iota(jnp.int32, x.shape, 1)
    nxt = pltpu.roll(x, w - 1, 1)
    prv = pltpu.roll(x, 1, 1)
    return x * c + jnp.where(lane % 2 == 0, nxt, prv) * s


N_INPROJ_WEIGHTS = 6


def _mod_row(mod_ref, row0, per_batch):
    r = row0 + pl.program_id(0) if per_batch else row0
    return mod_ref[pl.ds(r, 1), :]


def _modulated(x_ref, m):
    d = x_ref.shape[1]
    return (x_ref[...] * (1.0 + m[:, d:2 * d]) + m[:, 0:d]).astype(BF16)


def _keys(k_nope, kpe):
    shared = pltpu.roll(kpe, MLA_NOPE, 1)
    return (k_nope + jnp.concatenate([shared] * MLA_HEADS, axis=1)).astype(BF16)


def _inproj_kernel(*refs, row0, per_batch, rope):
    x_ref, mod_ref = refs[:2]
    _inproj_body(x_ref, _mod_row(mod_ref, row0, per_batch), *refs[2:], rope=rope)


def _inproj_body(x_ref, m, *refs, rope):
    gam_ref, win_ref, qn_ref, wuq_ref, kvn_ref, wk_ref, wv_ref = refs[:1 + N_INPROJ_WEIGHTS]
    refs = refs[1 + N_INPROJ_WEIGHTS:]
    if rope:
        cq_ref, sq_ref, ck_ref, sk_ref = refs[:4]
        refs = refs[4:]
    q_o, k_o, v_o, ckv_o, kpe_o, hgx_o = refs
    h = _modulated(x_ref, m)
    hw = HG_HEADS * HG_DK
    o_kv = qn_ref.shape[1]
    o_pe = o_kv + kvn_ref.shape[1]
    o_h = o_pe + MLA_ROPE

    cq = _rms(_dot_nt(h, win_ref[0:o_kv, :]), qn_ref[...])
    q = _dot(cq.astype(BF16), wuq_ref[...])
    if rope:
        q = _rope(q, jnp.concatenate([cq_ref[...]] * MLA_HEADS, axis=1),
                  jnp.concatenate([sq_ref[...]] * MLA_HEADS, axis=1))
    q_o[...] = q.astype(BF16)

    ckv = _rms(_dot_nt(h, win_ref[o_kv:o_pe, :]), kvn_ref[...])
    ckv_o[...] = ckv
    kpe = _dot_nt(h, win_ref[o_pe:o_h, :])
    kpe = jnp.concatenate([kpe, jnp.zeros((kpe.shape[0], LANES - MLA_ROPE), F32)], axis=1)
    if rope:
        kpe = _rope(kpe, ck_ref[...], sk_ref[...])
    kpe_o[...] = kpe
    cb = ckv.astype(BF16)
    k_o[...] = _keys(_dot(cb, wk_ref[...]), kpe)
    v_o[...] = _dot(cb, wv_ref[...]).astype(BF16)

    z = _dot_nt(h, win_ref[o_h:o_h + 5 * hw, :])
    hgx_o[:, 0:hw] = _silu(z[:, 0:hw])
    for dr in range(2):
        g0, g1 = gam_ref[dr, 0:1, :], gam_ref[dr, 1:2, :]
        gmax = jnp.maximum(g0, g1)
        e0, e1 = jnp.exp(g0 - gmax), jnp.exp(g1 - gmax)
        lb = e0 / (e0 + e1)
        f = lb + (1.0 - lb) * jax.nn.sigmoid(z[:, (1 + dr) * hw:(2 + dr) * hw])
        hgx_o[:, (1 + 2 * dr) * hw:(2 + 2 * dr) * hw] = jnp.log(f)
        hgx_o[:, (2 + 2 * dr) * hw:(3 + 2 * dr) * hw] = 1.0 - f
    hgx_o[:, 5 * hw:6 * hw] = z[:, 3 * hw:4 * hw]
    hgx_o[:, 6 * hw:7 * hw] = z[:, 4 * hw:5 * hw]


def _inproj(x2d, batch, seq, mod, gamma, wts, row0, per_batch, rope_tabs):
    n, d = x2d.shape
    tm = MIXER_SEQS * TOK_BLOCK if seq % (MIXER_SEQS * TOK_BLOCK) == 0 else TOK_BLOCK
    nblk = seq // tm
    rope = rope_tabs is not None
    hp = MLA_HEADS * LANES
    hw = HG_HEADS * HG_DK
    tok = lambda b, i: (b * nblk + i, 0)
    pos = lambda b, i: (i, 0)
    weights = [wts[k] for k in INPROJ_KEYS]
    ins = [x2d, mod, gamma] + weights
    in_specs = ([pl.BlockSpec((tm, d), tok), _const_spec(mod.shape), _const_spec(gamma.shape)]
                + [_const_spec(w.shape) for w in weights])
    if rope:
        ins += list(rope_tabs)
        in_specs += [pl.BlockSpec((tm, t.shape[1]), pos) for t in rope_tabs]
    widths = [(hp, BF16), (hp, BF16), (MLA_HEADS * MLA_V, BF16), (wts["kvn"].shape[1], F32), (LANES, F32),
              (7 * hw, F32)]
    return pl.pallas_call(
        functools.partial(_inproj_kernel, row0=row0, per_batch=per_batch, rope=rope),
        out_shape=[jax.ShapeDtypeStruct((n, w), dt) for w, dt in widths],
        grid=(batch, nblk),
        in_specs=in_specs,
        out_specs=[pl.BlockSpec((tm, w), tok) for w, _ in widths],
        compiler_params=_params("arbitrary", "arbitrary"),
        name="inproj",
    )(*ins)


def _kvup_kernel(ckv_ref, kpe_ref, wk_ref, wv_ref, k_o, v_o):
    cb = ckv_ref[...].astype(BF16)
    k_o[...] = _keys(_dot(cb, wk_ref[...]), kpe_ref[...])
    v_o[...] = _dot(cb, wv_ref[...]).astype(BF16)


def _kvup(ckv2d, kpe2d, wts):
    n = ckv2d.shape[0]
    tm = TOK_BLOCK
    widths = [MLA_HEADS * LANES, MLA_HEADS * MLA_V]
    row = lambda i: (i, 0)
    ws = [wts["wk"], wts["wv"]]
    return pl.pallas_call(
        _kvup_kernel,
        out_shape=[jax.ShapeDtypeStruct((n, w), BF16) for w in widths],
        grid=(n // tm,),
        in_specs=[pl.BlockSpec((tm, ckv2d.shape[1]), row), pl.BlockSpec((tm, LANES), row)]
                 + [_const_spec(w.shape) for w in ws],
        out_specs=[pl.BlockSpec((tm, w), row) for w in widths],
        compiler_params=_params("arbitrary"),
        name="kvup",
    )(ckv2d, kpe2d, *ws)


ATTN_SCALE = (MLA_NOPE + MLA_ROPE) ** -0.5


def _attn_body(q_ref, k_ref, v_ref, kc_ref, vc_ref, o_ref):
    cached = kc_ref is not None
    scale = ATTN_SCALE * math.log2(math.e)
    per_slab = LANES // MLA_V
    own = lax.broadcasted_iota(jnp.int32, (q_ref.shape[0], LANES), 1) // MLA_V
    for slab in range(MLA_HEADS // per_slab):
        vsl = slice(slab * LANES, (slab + 1) * LANES)
        out = None
        for sub in range(per_slab):
            hd = slab * per_slab + sub
            sl = slice(hd * LANES, (hd + 1) * LANES)
            q = q_ref[:, sl]
            s = _dot_nt(q, k_ref[:, sl])
            mx = jnp.max(s, axis=-1, keepdims=True)
            if cached:
                s2 = _dot_nt(q, kc_ref[:, sl])
                mx = jnp.maximum(mx, jnp.max(s2, axis=-1, keepdims=True))
            e = jnp.exp2((s - mx) * scale)
            den = jnp.sum(e, axis=-1, keepdims=True)
            o = _dot(e.astype(BF16), v_ref[:, vsl])
            if cached:
                e2 = jnp.exp2((s2 - mx) * scale)
                den = den + jnp.sum(e2, axis=-1, keepdims=True)
                o = o + _dot(e2.astype(BF16), vc_ref[:, vsl])
            o = o / den
            out = o if out is None else jnp.where(own == sub, o, out)
        o_ref[:, vsl] = out.astype(o_ref.dtype)


def _chunk_scan(x, reverse):
    tm = x.shape[0]
    rin = lax.broadcasted_iota(jnp.int32, x.shape, 0) % HG_CHUNK
    step = 1
    while step < HG_CHUNK:
        if reverse:
            x = x + jnp.where(rin < HG_CHUNK - step, pltpu.roll(x, tm - step, 0), 0.0)
        else:
            x = x + jnp.where(rin >= step, pltpu.roll(x, step, 0), 0.0)
        step *= 2
    return x


def _hgrn_kernel(*refs, has_init):
    fwd, bwd = refs[0:4], refs[4:8]
    refs = refs[8:]
    s0_ref = None
    if has_init:
        s0_ref = refs[0]
        refs = refs[1:]
    of_ref, ob_ref, sfin_ref, st_scr = refs
    i = pl.program_id(1)
    _hgrn_body([(fwd, bwd, of_ref, ob_ref, sfin_ref)], s0_ref, st_scr, i == 0, i == pl.num_programs(1) - 1)


def _hgrn_body(jobs, s0_ref, st_scr, first, last):
    tm = jobs[0][0][0].shape[0]
    c = HG_CHUNK
    nch = tm // c
    dk, dv = HG_DK, HG_DV
    hw = HG_HEADS * dk

    def initial(dr, hd):
        return s0_ref[0, dr, hd].T if s0_ref is not None else jnp.zeros((dv, dk), F32)

    if st_scr is not None:
        @pl.when(first)
        def _init():
            for dr in range(2):
                for hd in range(HG_HEADS):
                    st_scr[dr, hd] = initial(dr, hd)

    npair = nch // 2
    pair = 2 * c
    row = lax.broadcasted_iota(jnp.int32, (tm, tm), 0)
    col = lax.broadcasted_iota(jnp.int32, (tm, tm), 1)
    same = (row // c) == (col // c)
    same_pair = (row // pair) == (col // pair)
    bd = (lax.broadcasted_iota(jnp.int32, (tm, npair * dk), 0) // pair
          == lax.broadcasted_iota(jnp.int32, (tm, npair * dk), 1) // dk)
    chunk_odd = (lax.broadcasted_iota(jnp.int32, (tm, hw), 0) // c) % 2 == 1

    for dr in range(2):
        tri = same & ((col <= row) if dr == 0 else (col >= row))
        cross = same_pair & (((row // c) > (col // c)) if dr == 0 else ((row // c) < (col // c)))
        second = chunk_odd if dr == 0 else ~chunk_odd
        order = range(npair) if dr == 0 else range(npair - 1, -1, -1)

        def decayed(job):
            hq_ref, lf_ref, kk_ref, vv_ref = job[dr]
            bcum = _chunk_scan(lf_ref[...], reverse=dr == 1)
            closing = c - 1 if dr == 0 else 0
            btot3 = bcum.reshape(nch, c, hw)[:, closing:closing + 1, :]
            btot = jnp.broadcast_to(btot3, (nch, c, hw)).reshape(tm, hw)
            bpart = jnp.where(chunk_odd, pltpu.roll(btot, c, 0), pltpu.roll(btot, tm - c, 0))
            epart = jnp.exp(bpart)
            kk = kk_ref[...]
            qd = hq_ref[...] * jnp.exp(bcum)
            kd = kk * jnp.exp(-bcum)
            ke = kk * jnp.exp(btot - bcum)
            qd2 = jnp.where(second, qd * epart, qd)
            ke2 = jnp.where(second, ke, ke * epart)
            return qd, kd, ke, qd2, ke2, btot + bpart, vv_ref[...]

        def head(job, hd, qd, kd, ke, qd2, ke2, bpair, vv):
            o_ref, sfin_ref = job[2 + dr], job[4]
            sl = slice(hd * dk, (hd + 1) * dk)
            qd_h = qd[:, sl].astype(BF16)
            v_h = vv[:, hd * dv:(hd + 1) * dv]
            a = jnp.where(tri, _dot_nt(qd_h, kd[:, sl].astype(BF16)),
                          jnp.where(cross, _dot_nt(qd_h, ke[:, sl].astype(BF16)), 0.0))
            o_intra = _dot(a.astype(BF16), v_h.astype(BF16))
            kebd = jnp.where(bd, jnp.concatenate([ke2[:, sl]] * npair, axis=1), 0.0).astype(BF16)
            qbd = jnp.where(bd, jnp.concatenate([qd2[:, sl]] * npair, axis=1), 0.0).astype(BF16)
            ut = _dot(v_h.T.astype(BF16), kebd)
            st = st_scr[dr, hd] if st_scr is not None else initial(dr, hd)
            prev = [None] * npair
            for p in order:
                prev[p] = st
                st = st * jnp.exp(bpair[p * pair:p * pair + 1, sl]) + ut[:, p * dk:(p + 1) * dk]
            if st_scr is not None:
                st_scr[dr, hd] = st
            o_inter = _dot_nt(qbd, jnp.concatenate(prev, axis=1).astype(BF16))
            o_ref[:, hd * dv:(hd + 1) * dv] = o_intra + o_inter

            if last is True:
                sfin_ref[0, dr, hd] = st.T
            else:
                @pl.when(last)
                def _final():
                    sfin_ref[0, dr, hd] = st.T

        prepared = [decayed(job) for job in jobs]
        for hd in range(HG_HEADS):
            for job, arrays in zip(jobs, prepared):
                head(job, hd, *arrays)


def _hgrn(hgx, batch, seq, s0=None):
    n = hgx.shape[0]
    tm = TOK_BLOCK
    nblk = seq // tm
    hw = HG_HEADS * HG_DK

    def spec(lane_blk, rev):
        if rev:
            return pl.BlockSpec((tm, hw), lambda b, i: (b * nblk + nblk - 1 - i, lane_blk))
        return pl.BlockSpec((tm, hw), lambda b, i: (b * nblk + i, lane_blk))

    in_specs = [spec(0, False), spec(1, False), spec(2, False), spec(5, False),
                spec(0, True), spec(3, True), spec(4, True), spec(5, True)]
    ins = [hgx] * 8
    st_shape = (1, 2, HG_HEADS, HG_DK, HG_DV)
    st_spec = pl.BlockSpec(st_shape, lambda b, i: (b, 0, 0, 0, 0))
    if s0 is not None:
        ins.append(s0)
        in_specs.append(st_spec)
    return pl.pallas_call(
        functools.partial(_hgrn_kernel, has_init=s0 is not None),
        out_shape=[jax.ShapeDtypeStruct((n, hw), F32), jax.ShapeDtypeStruct((n, hw), F32),
                   jax.ShapeDtypeStruct((batch,) + st_shape[1:], F32)],
        grid=(batch, nblk),
        in_specs=in_specs,
        out_specs=[spec(0, False), spec(0, True), st_spec],
        scratch_shapes=[pltpu.VMEM((2, HG_HEADS, HG_DV, HG_DK), F32)],
        compiler_params=_params("arbitrary", "arbitrary"),
        name="hgrn",
    )(*ins)


def _layer_norm(x, g, b):
    xc = x - jnp.mean(x, axis=-1, keepdims=True)
    var = jnp.mean(xc * xc, axis=-1, keepdims=True)
    return xc * lax.rsqrt(var + EPS) * g + b


N_POSTMIX_WEIGHTS = 8
INPROJ_KEYS = ("win", "qn", "wuq", "kvn", "wk", "wv")
POSTMIX_KEYS = ("hgn", "womla", "wohg", "wout", "ln1g", "ln1b", "wr")
MIXER_SEQS = 2


def _postmix_kernel(x_ref, mod_ref, *refs, alpha, row0, per_batch, cached):
    n_attn = 5 if cached else 3
    q_ref, k_ref, v_ref = refs[:3]
    kc_ref, vc_ref = refs[3:5] if cached else (None, None)
    of_ref, ob_ref, zg_ref = refs[n_attn:n_attn + 3]
    om_s = refs[-1]
    _attn_body(q_ref, k_ref, v_ref, kc_ref, vc_ref, om_s)
    _postmix_body(x_ref, _mod_row(mod_ref, row0, per_batch), of_ref, ob_ref, zg_ref, om_s,
                  *refs[n_attn + 3:-1], alpha=alpha)


def _postmix_body(x_ref, m, of_ref, ob_ref, zg_ref, om_ref, wg_ref, hgn_ref, womla_ref,
                  wohg_ref, wout_ref, lng_ref, lnb_ref, wr_ref, x1_o, h2_o, aff_o, *, alpha):
    d = x_ref.shape[1]
    tb = aff_o.shape[2]
    g1, sh2, sc2 = m[:, 2 * d:3 * d], m[:, 3 * d:4 * d], m[:, 4 * d:5 * d]
    o = of_ref[...] + ob_ref[...]
    zg = zg_ref[...]
    parts = []
    for hd in range(HG_HEADS):
        sl = slice(hd * HG_DV, (hd + 1) * HG_DV)
        parts.append(_rms(o[:, sl], hgn_ref[...]) * _silu(zg[:, sl]))
    ohg = jnp.concatenate(parts, axis=1).astype(BF16)
    gates = _dot_nt(_modulated(x_ref, m), wg_ref[wg_ref.shape[0] - 2 * d:, :])
    merged = (jax.nn.sigmoid(gates[:, 0:d]) * _dot(om_ref[...], womla_ref[...])
              + jax.nn.sigmoid(gates[:, d:2 * d]) * _dot(ohg, wohg_ref[...]))
    mix = _dot(merged.astype(BF16), wout_ref[...])
    x1 = _layer_norm(alpha * x_ref[...] + g1 * mix, lng_ref[...], lnb_ref[...])
    x1_o[...] = x1
    h2 = (x1 * (1.0 + sc2) + sh2).astype(BF16)
    h2_o[...] = h2
    logits = _dot_nt(wr_ref[...], h2)
    e = jnp.exp(logits - jnp.max(logits, axis=0, keepdims=True))
    aff = e / jnp.sum(e, axis=0, keepdims=True)
    for blk in range(aff_o.shape[0]):
        aff_o[blk] = aff[:, blk * tb:(blk + 1) * tb]


def _postmix(x2d, batch, seq, mod, q, k, v, cache, o_f, o_b, hgx, wts, alpha, n_experts, row0, per_batch):
    n, d = x2d.shape
    tm = TOK_BLOCK
    nblk = seq // tm
    hw = HG_HEADS * HG_DV
    hp, hv = q.shape[1], v.shape[1]
    tok = lambda b, i: (b * nblk + i, 0)
    per_seq = lambda b, i: (b, 0)
    weights = [wts[k] for k in POSTMIX_KEYS]
    attn_ins = [q, k, v]
    attn_specs = [pl.BlockSpec((tm, hp), tok), pl.BlockSpec((seq, hp), per_seq), pl.BlockSpec((seq, hv), per_seq)]
    if cache is not None:
        past = cache[0].shape[0] // batch
        attn_ins += list(cache)
        attn_specs += [pl.BlockSpec((past, hp), per_seq), pl.BlockSpec((past, hv), per_seq)]
    return pl.pallas_call(
        functools.partial(_postmix_kernel, alpha=alpha, row0=row0, per_batch=per_batch,
                          cached=cache is not None),
        out_shape=[jax.ShapeDtypeStruct((n, d), F32), jax.ShapeDtypeStruct((n, d), BF16),
                   jax.ShapeDtypeStruct((n // tm, n_experts, tm), F32)],
        grid=(batch, nblk),
        in_specs=[pl.BlockSpec((tm, d), tok), _const_spec(mod.shape)] + attn_specs
                 + [pl.BlockSpec((tm, hw), tok), pl.BlockSpec((tm, hw), tok),
                    pl.BlockSpec((tm, hw), lambda b, i: (b * nblk + i, 6)), _const_spec(wts["win"].shape)]
                 + [_const_spec(w.shape) for w in weights],
        out_specs=[pl.BlockSpec((tm, d), tok), pl.BlockSpec((tm, d), tok),
                   pl.BlockSpec((1, n_experts, tm), lambda b, i: (b * nblk + i, 0, 0))],
        scratch_shapes=[pltpu.VMEM((tm, hv), BF16)],
        compiler_params=_params("arbitrary", "arbitrary"),
        name="postmix",
    )(x2d, mod, *attn_ins, o_f, o_b, hgx, wts["win"], *weights)


def _mixer_kernel(x_ref, mod_ref, *refs, alpha, row0, seq):
    nw = 1 + N_INPROJ_WEIGHTS
    in_w, refs = refs[:nw], refs[nw:]
    pm_w, refs = refs[:N_POSTMIX_WEIGHTS - 1], refs[N_POSTMIX_WEIGHTS - 1:]
    pm_w = (in_w[1],) + tuple(pm_w)
    x1_o, h2_o, aff_o, ckv_o, kpe_o, sfin_o, q_s, k_s, v_s, hgx_s, om_s, of_s, ob_s = refs
    m = _mod_row(mod_ref, row0, False)
    _inproj_body(x_ref, m, *in_w, q_s, k_s, v_s, ckv_o, kpe_o, hgx_s, rope=False)
    hw = HG_HEADS * HG_DK
    jobs = []
    for s in range(x_ref.shape[0] // seq):
        rows = slice(s * seq, (s + 1) * seq)
        _attn_body(q_s.at[rows], k_s.at[rows], v_s.at[rows], None, None, om_s.at[rows])
        lane = lambda j, rows=rows: hgx_s.at[rows, j * hw:(j + 1) * hw]
        jobs.append(((lane(0), lane(1), lane(2), lane(5)), (lane(0), lane(3), lane(4), lane(5)),
                     of_s.at[rows], ob_s.at[rows], sfin_o.at[s:s + 1]))
    _hgrn_body(jobs, None, None, True, True)
    _postmix_body(x_ref, m, of_s, ob_s, hgx_s.at[:, 6 * hw:7 * hw], om_s, *pm_w, x1_o, h2_o, aff_o,
                  alpha=alpha)


def _mixer(x2d, batch, seq, mod, gamma, wts, alpha, n_experts, row0):
    n, d = x2d.shape
    assert seq == TOK_BLOCK
    ns = MIXER_SEQS if batch % MIXER_SEQS == 0 else 1
    tm = ns * seq
    hp = MLA_HEADS * LANES
    hv = MLA_HEADS * MLA_V
    hw = HG_HEADS * HG_DK
    kvl = wts["kvn"].shape[1]
    weights = [wts[k] for k in INPROJ_KEYS + POSTMIX_KEYS]
    tok = lambda b: (b, 0)
    st_shape = (ns, 2, HG_HEADS, HG_DK, HG_DV)
    return pl.pallas_call(
        functools.partial(_mixer_kernel, alpha=alpha, row0=row0, seq=seq),
        out_shape=[jax.ShapeDtypeStruct((n, d), F32), jax.ShapeDtypeStruct((n, d), BF16),
                   jax.ShapeDtypeStruct((n // seq, n_experts, seq), F32),
                   jax.ShapeDtypeStruct((n, kvl), F32), jax.ShapeDtypeStruct((n, LANES), F32),
                   jax.ShapeDtypeStruct((batch,) + st_shape[1:], F32)],
        grid=(batch // ns,),
        in_specs=[pl.BlockSpec((tm, d), tok), _const_spec(mod.shape), _const_spec(gamma.shape)]
                 + [_const_spec(w.shape) for w in weights],
        out_specs=[pl.BlockSpec((tm, d), tok), pl.BlockSpec((tm, d), tok),
                   pl.BlockSpec((ns, n_experts, seq), lambda b: (b, 0, 0)),
                   pl.BlockSpec((tm, kvl), tok), pl.BlockSpec((tm, LANES), tok),
                   pl.BlockSpec(st_shape, lambda b: (b, 0, 0, 0, 0))],
        scratch_shapes=[pltpu.VMEM((tm, hp), BF16), pltpu.VMEM((tm, hp), BF16), pltpu.VMEM((tm, hv), BF16),
                        pltpu.VMEM((tm, 7 * hw), F32), pltpu.VMEM((tm, hv), BF16),
                        pltpu.VMEM((tm, hw), F32), pltpu.VMEM((tm, hw), F32)],
        compiler_params=_params("arbitrary"),
        name="mixer",
    )(x2d, mod, gamma, *weights)


def _route_kernel(*refs, caps):
    ng = len(caps)
    for aff_ref, rank_o, cnt_o, cap in zip(refs[0:ng], refs[ng:2 * ng], refs[2 * ng:3 * ng], caps):
        _route_group(aff_ref, rank_o, cnt_o, cap)


def _route_group(aff_ref, rank_o, cnt_o, cap):
    nb, ne, tb = aff_ref.shape
    key = aff_ref[...]

    def count(mask):
        return jnp.sum(jnp.sum(jnp.where(mask, 1.0, 0.0), axis=0), axis=1, keepdims=True)

    def bit_step(it, bits):
        cand = bits | jnp.left_shift(jnp.int32(1), 30 - it)
        return jnp.where(count(key >= pltpu.bitcast(cand, F32)[None]) >= cap, cand, bits)

    bits = lax.fori_loop(0, 31, bit_step, jnp.zeros((ne, 1), jnp.int32))
    thr = pltpu.bitcast(bits, F32)
    need = cap - count(key > thr[None])
    before = (lax.broadcasted_iota(jnp.int32, (tb, tb), 0)
              < lax.broadcasted_iota(jnp.int32, (tb, tb), 1))
    before = jnp.where(before, 1.0, 0.0).astype(BF16)
    off_eq = jnp.zeros((ne, 1), F32)
    off_sel = jnp.zeros((ne, 1), F32)
    cnt_o[...] = jnp.zeros_like(cnt_o)
    for blk in range(nb):
        key_b = key[blk]
        eq = key_b == thr
        eq_b = jnp.where(eq, 1.0, 0.0)
        eq_rank = _dot(eq_b.astype(BF16), before) + off_eq
        sel = (key_b > thr) | (eq & (eq_rank < need))
        sel_b = jnp.where(sel, 1.0, 0.0)
        rank = _dot(sel_b.astype(BF16), before) + off_sel
        rank_o[blk] = jnp.where(sel, rank.astype(jnp.int32), UNSELECTED)
        cnt_o[:, blk:blk + 1] = off_sel.astype(jnp.int32)
        off_eq = off_eq + jnp.sum(eq_b, axis=1, keepdims=True)
        off_sel = off_sel + jnp.sum(sel_b, axis=1, keepdims=True)
    cnt_o[:, nb:nb + 1] = off_sel.astype(jnp.int32)


def _route(affs, caps):
    ng = len(affs)
    ne = affs[0].shape[1]
    assert all(a.shape[0] + 1 <= LANES for a in affs)
    outs = pl.pallas_call(
        functools.partial(_route_kernel, caps=tuple(caps)),
        out_shape=[jax.ShapeDtypeStruct(a.shape, jnp.int32) for a in affs]
                  + [jax.ShapeDtypeStruct((ne, LANES), jnp.int32)] * ng,
        in_specs=[pl.BlockSpec(memory_space=pltpu.VMEM)] * ng,
        out_specs=[pl.BlockSpec(memory_space=pltpu.VMEM)] * (2 * ng),
        compiler_params=pltpu.CompilerParams(vmem_limit_bytes=VMEM_LIMIT),
        name="route",
    )(*affs)
    return [(outs[gi], outs[ng + gi][:, :affs[gi].shape[0] + 1]) for gi in range(ng)]


def _window_hits(rk_ref, firsts, slot0, win):
    ne, tb = rk_ref.shape[1], rk_ref.shape[2]
    win_iota = lax.broadcasted_iota(jnp.int32, (win, tb), 0)
    return [(rk_ref[0, e:e + 1, :] + (slot0 - firsts[e])) == win_iota for e in range(ne)]


def _compact_kernel(first_ref, end_ref, rounds_ref, *refs, groups, slots):
    ng = len(groups)
    h2_refs, rk_refs, af_refs = refs[0:ng], refs[ng:2 * ng], refs[2 * ng:3 * ng]
    xe_hbm, stage, tail, sem, issued = refs[3 * ng:]
    hbms, stages, tails = [xe_hbm], [stage], [tail]
    b = pl.program_id(0)
    ne = rk_refs[0].shape[1]
    win = WIN_ROWS
    sub = BF16_ROWS

    def copies(slot, dsts):
        return [pltpu.make_async_copy(stage.at[slot, pl.ds(e * win, win), :],
                                      hbm.at[e, pl.ds(pl.multiple_of(dsts[e], sub), win), :], sem.at[e])
                for stage, hbm in zip(stages, hbms) for e in range(ne)]

    def wait_previous():
        @pl.when(issued[0] > 0)
        def _():
            for cp in copies(0, [0] * ne):
                cp.wait()

    @pl.when(b == 0)
    def _init():
        issued[0] = 0
        for stage, tail in zip(stages, tails):
            tail[...] = jnp.zeros_like(tail)
            stage[1] = jnp.zeros(stage.shape[1:], stage.dtype)
        pad = copies(1, [slots] * ne)
        for cp in pad:
            cp.start()
        for cp in pad:
            cp.wait()

    def group_body(h2_ref, rk_ref, af_ref, slot0):
        firsts = [first_ref[b * ne + e] for e in range(ne)]
        bases = [(f // sub) * sub for f in firsts]
        ends = [end_ref[b * ne + e] - bases[e] for e in range(ne)]

        def one_round(r, carry):
            dsts = [bases[e] + r * win for e in range(ne)]
            hits = _window_hits(rk_ref, dsts, slot0, win)
            onehot = jnp.where(jnp.concatenate(hits, axis=0), 1.0, 0.0).astype(BF16)
            gate = jnp.concatenate(
                [jnp.sum(jnp.where(hits[e], af_ref[0, e:e + 1, :], 0.0), axis=1, keepdims=True)
                 for e in range(ne)], axis=0)
            slot = issued[0] % 2
            gate = jnp.broadcast_to(gate, (ne * win, LANES))
            g1 = gate.astype(BF16).astype(F32)
            g2 = (gate - g1).astype(BF16).astype(F32)
            g3 = (gate - g1 - g2).astype(BF16).astype(F32)
            lane = lax.broadcasted_iota(jnp.int32, (ne * win, LANES), 1)
            pieces3 = jnp.where(lane == 0, g1, jnp.where(lane == 1, g2, jnp.where(lane == 2, g3, 0.0)))
            streams = (jnp.concatenate([_dot(onehot, h2_ref[...]).astype(BF16),
                                        pieces3.astype(BF16)], axis=1),)
            for val, stage, tail in zip(streams, stages, tails):
                sub_iota = lax.broadcasted_iota(jnp.int32, (sub, val.shape[1]), 0)
                pieces = []
                for e in range(ne):
                    old = tail[e * sub:(e + 1) * sub, :]
                    shared = jnp.where(r == 0, firsts[e] - bases[e], 0)
                    groups_e = [val[e * win + g * sub:e * win + (g + 1) * sub, :] for g in range(win // sub)]
                    groups_e[0] = jnp.where(sub_iota < shared, old, groups_e[0])
                    pieces += groups_e
                    last = (ends[e] // sub) * sub
                    new = old
                    for g, grp in enumerate(groups_e):
                        new = jnp.where((r == last // win) & (last % win == g * sub), grp, new)
                    tail[e * sub:(e + 1) * sub, :] = new
                stage[slot] = jnp.concatenate(pieces, axis=0)
            wait_previous()
            for e, cp in enumerate(copies(slot, [jnp.minimum(dst, slots) for dst in dsts])):
                cp.start(priority=e % 2)
            issued[0] = issued[0] + 1
            return carry

        lax.fori_loop(0, rounds_ref[b], one_round, 0)

    blk0 = 0
    for gi, g in enumerate(groups):
        @pl.when((b >= blk0) & (b < blk0 + g["nb"]))
        def _(gi=gi, g=g):
            group_body(h2_refs[gi], rk_refs[gi], af_refs[gi], g["slot0"])
        blk0 += g["nb"]

    @pl.when(b == pl.num_programs(0) - 1)
    def _drain():
        wait_previous()


def _compact(groups, first, end, rounds, slots):
    d = groups[0]["h2"].shape[1]
    nbs = [g["rank"].shape[0] for g in groups]
    ne, tb = groups[0]["rank"].shape[1:]
    meta, specs_h2, specs_rk = [], [], []
    blk0 = 0
    for g, nb in zip(groups, nbs):
        meta.append(dict(nb=nb, slot0=g["slot0"]))
        local = lambda b, *_, blk0=blk0, nb=nb: jnp.clip(b - blk0, 0, nb - 1)
        specs_h2.append(pl.BlockSpec((tb, d), lambda b, *_, local=local: (local(b), 0)))
        specs_rk.append(pl.BlockSpec((1, ne, tb), lambda b, *_, local=local: (local(b), 0, 0)))
        blk0 += nb
    width = d + LANES
    return pl.pallas_call(
        functools.partial(_compact_kernel, groups=meta, slots=slots),
        out_shape=jax.ShapeDtypeStruct((ne, slots + WIN_ROWS, width), BF16),
        grid_spec=pltpu.PrefetchScalarGridSpec(
            num_scalar_prefetch=3,
            grid=(sum(nbs),),
            in_specs=specs_h2 + specs_rk + specs_rk,
            out_specs=pl.BlockSpec(memory_space=pl.ANY),
            scratch_shapes=[pltpu.VMEM((2, ne * WIN_ROWS, width), BF16),
                            pltpu.VMEM((ne * BF16_ROWS, width), BF16),
                            pltpu.SemaphoreType.DMA((ne,)), pltpu.SMEM((1,), jnp.int32)]),
        compiler_params=_params("arbitrary"),
        name="compact",
    )(first, end, rounds, *[g["h2"] for g in groups], *[g["rank"] for g in groups],
      *[g["aff"] for g in groups])


def _ffn_kernel(xe_ref, w1_ref, w3_ref, w2_ref, ye_ref, *scratch):
    f = pl.program_id(1)
    d = w1_ref.shape[1]
    x = xe_ref[0, :, 0:d]
    pieces = xe_ref[0, :, d:d + LANES].astype(F32)
    gate = pieces[:, 0:1] + pieces[:, 1:2] + pieces[:, 2:3]
    hid = _silu(_dot(x, w1_ref[0].astype(BF16))) * _dot(x, w3_ref[0].astype(BF16))
    y = _dot(hid.astype(BF16), w2_ref[0].astype(BF16))
    if not scratch:
        ye_ref[0] = (y * gate).astype(ye_ref.dtype)
        return
    acc_scr, = scratch
    last = pl.num_programs(1) - 1

    @pl.when(f == 0)
    def _first():
        acc_scr[...] = y

    @pl.when((f > 0) & (f < last))
    def _middle():
        acc_scr[...] += y

    @pl.when(f == last)
    def _last():
        ye_ref[0] = ((acc_scr[...] + y) * gate).astype(ye_ref.dtype)


def _ffn(xe, w1, w3, w2, slots, ft):
    ne, d, dff = w1.shape
    nf = dff // ft
    return pl.pallas_call(
        _ffn_kernel,
        out_shape=jax.ShapeDtypeStruct((ne, slots, d), BF16),
        grid=(ne, nf),
        in_specs=[pl.BlockSpec((1, slots, xe.shape[2]), lambda e, f: (e, 0, 0)),
                  pl.BlockSpec((1, d, ft), lambda e, f: (e, 0, f)),
                  pl.BlockSpec((1, d, ft), lambda e, f: (e, 0, f)),
                  pl.BlockSpec((1, ft, d), lambda e, f: (e, f, 0))],
        out_specs=pl.BlockSpec((1, slots, d), lambda e, f: (e, 0, 0)),
        scratch_shapes=[pltpu.VMEM((slots, d), F32)] if nf > 1 else [],
        compiler_params=_params("arbitrary", "arbitrary"),
        name="ffn",
    )(xe, w1, w3, w2)


def _combine_kernel(first_ref, rounds_ref, *refs, groups, alpha, slots):
    ng = len(groups)
    rk_refs, x1_refs = refs[0:ng], refs[ng:2 * ng]
    mod_ref, lng_ref, lnb_ref, ye_hbm = refs[2 * ng:2 * ng + 4]
    out_refs = refs[2 * ng + 4:3 * ng + 4]
    buf, acc_scr, sem = refs[3 * ng + 4:]
    d = x1_refs[0].shape[1]
    b = pl.program_id(0)
    nblk = pl.num_programs(0)
    ne, tb = rk_refs[0].shape[1], rk_refs[0].shape[2]
    win = WIN_ROWS

    def starts_of(blk, r):
        firsts = [(first_ref[blk * ne + e] // BF16_ROWS) * BF16_ROWS + r * win for e in range(ne)]
        return firsts, [jnp.minimum(f, slots - win) for f in firsts]

    def windows(slot, starts):
        return [pltpu.make_async_copy(ye_hbm.at[e, pl.ds(pl.multiple_of(starts[e], BF16_ROWS), win), :],
                                      buf.at[slot, pl.ds(e * win, win), :], sem.at[slot, e])
                for e in range(ne)]

    def scatter(rk_ref, slot0, slot, firsts, starts, later_round):
        hits = _window_hits(rk_ref, starts, slot0, win)
        if later_round:
            hits = [h & ((rk_ref[0, e:e + 1, :] + slot0) >= firsts[e]) for e, h in enumerate(hits)]
        hit = jnp.where(jnp.concatenate(hits, axis=0), 1.0, 0.0)
        return _dot(hit.T.astype(BF16), buf[slot])

    cur = b % 2

    @pl.when(b == 0)
    def _prime():
        for e, cp in enumerate(windows(0, starts_of(0, 0)[1])):
            cp.start(priority=e % 2)

    @pl.when(b + 1 < nblk)
    def _prefetch():
        for e, cp in enumerate(windows(1 - cur, starts_of(b + 1, 0)[1])):
            cp.start(priority=e % 2)

    firsts0, starts0 = starts_of(b, 0)
    for cp in windows(cur, starts0):
        cp.wait()

    def group_body(rk_ref, x1_ref, out_ref, g, local):
        acc_scr[...] = scatter(rk_ref, g["slot0"], cur, firsts0, starts0, False)

        def extra_round(r, carry):
            firsts, starts = starts_of(b, r)
            for cp in windows(2, starts):
                cp.start()
            for cp in windows(2, starts):
                cp.wait()
            acc_scr[...] += scatter(rk_ref, g["slot0"], 2, firsts, starts, True)
            return carry

        lax.fori_loop(1, rounds_ref[b], extra_round, 0)
        r = g["row0"] + local // g["blocks_per_batch"]
        g2 = mod_ref[pl.ds(r, 1), :][:, 5 * d:6 * d]
        out_ref[...] = _layer_norm(alpha * x1_ref[...] + g2 * acc_scr[...], lng_ref[...], lnb_ref[...])

    blk0 = 0
    for gi, g in enumerate(groups):
        @pl.when((b >= blk0) & (b < blk0 + g["nb"]))
        def _(gi=gi, g=g, blk0=blk0):
            group_body(rk_refs[gi], x1_refs[gi], out_refs[gi], g, b - blk0)
        blk0 += g["nb"]


def _combine(ye, groups, first, rounds, mod, ln_g, ln_b, alpha):
    d = groups[0]["x1"].shape[1]
    ne, tb = groups[0]["rank"].shape[1:]
    slots = ye.shape[1]
    meta, specs_rk, specs_x1 = [], [], []
    blk0 = 0
    for g in groups:
        nb = g["rank"].shape[0]
        meta.append(dict(nb=nb, slot0=g["slot0"], row0=g["row0"], blocks_per_batch=g["blocks_per_batch"]))
        local = lambda b, *_, blk0=blk0, nb=nb: jnp.clip(b - blk0, 0, nb - 1)
        specs_rk.append(pl.BlockSpec((1, ne, tb), lambda b, *_, local=local: (local(b), 0, 0)))
        specs_x1.append(pl.BlockSpec((tb, d), lambda b, *_, local=local: (local(b), 0)))
        blk0 += nb
    const = lambda shape: pl.BlockSpec(shape, lambda b, *_: (0,) * len(shape))
    return pl.pallas_call(
        functools.partial(_combine_kernel, groups=meta, alpha=alpha, slots=slots),
        out_shape=[jax.ShapeDtypeStruct(g["x1"].shape, F32) for g in groups],
        grid_spec=pltpu.PrefetchScalarGridSpec(
            num_scalar_prefetch=2,
            grid=(blk0,),
            in_specs=specs_rk + specs_x1 + [const(mod.shape), const((1, d)), const((1, d)),
                                            pl.BlockSpec(memory_space=pl.ANY)],
            out_specs=specs_x1,
            scratch_shapes=[pltpu.VMEM((3, ne * WIN_ROWS, d), ye.dtype), pltpu.VMEM((tb, d), F32),
                            pltpu.SemaphoreType.DMA((3, ne))]),
        compiler_params=_params("arbitrary"),
        name="combine",
    )(first, rounds, *[g["rank"] for g in groups], *[g["x1"] for g in groups], mod, ln_g, ln_b, ye)


def _prep_weights(w_in, q_norm, w_uq, kv_norm, w_ukv, w_o_mla, hgrn_norm, w_o_hg, w_out, ln1_g, ln1_b,
                  w_router):
    d = w_in.shape[0]
    q_lora, kv_lora = q_norm.shape[0], kv_norm.shape[0]
    hw = HG_HEADS * HG_DK
    hh, hp = MLA_HEADS, MLA_HEADS * LANES
    o_kv, o_pe = q_lora, q_lora + kv_lora
    o_h = o_pe + MLA_ROPE
    o_g = o_h + 5 * hw
    assert w_in.shape[1] == o_g + 2 * d
    qk = MLA_NOPE + MLA_ROPE
    kvw = MLA_NOPE + MLA_V
    b16 = lambda a: a.astype(BF16)
    assert all(o % BF16_ROWS == 0 for o in (o_kv, o_pe, o_h, o_g))
    win = b16(w_in.T)
    wuq = jnp.pad(w_uq.reshape(q_lora, hh, qk), ((0, 0), (0, 0), (0, LANES - qk))).reshape(q_lora, hp)
    ukv = w_ukv.reshape(kv_lora, hh, kvw)
    wk = jnp.pad(ukv[:, :, :MLA_NOPE], ((0, 0), (0, 0), (0, LANES - MLA_NOPE))).reshape(kv_lora, hp)
    wv = ukv[:, :, MLA_NOPE:].reshape(kv_lora, hh * MLA_V)
    return dict(
        win=win, qn=q_norm.reshape(1, -1), wuq=b16(wuq), kvn=kv_norm.reshape(1, -1), wk=b16(wk), wv=b16(wv),
        hgn=hgrn_norm.reshape(1, -1), womla=b16(w_o_mla), wohg=b16(w_o_hg), wout=b16(w_out),
        ln1g=ln1_g.reshape(1, -1), ln1b=ln1_b.reshape(1, -1), wr=b16(w_router.T))


def _rope_tables(seq):
    n_freq = MLA_ROPE // 4
    inv = ROPE_BASE ** (-np.arange(n_freq, dtype=np.float64) / n_freq)
    t = np.arange(seq)
    ang = np.concatenate([(t // GRID_W)[:, None] * inv, (t % GRID_W)[:, None] * inv], axis=-1)
    cos = np.repeat(np.cos(ang), 2, axis=1)
    sin = np.repeat(np.sin(ang), 2, axis=1) * np.tile([-1.0, 1.0], MLA_ROPE // 2)
    ck = np.pad(cos, ((0, 0), (0, LANES - MLA_ROPE)), constant_values=1.0)
    sk = np.pad(sin, ((0, 0), (0, LANES - MLA_ROPE)))
    cq = np.pad(cos, ((0, 0), (MLA_NOPE, LANES - MLA_NOPE - MLA_ROPE)), constant_values=1.0)
    sq = np.pad(sin, ((0, 0), (MLA_NOPE, LANES - MLA_NOPE - MLA_ROPE)))
    return tuple(jnp.asarray(a, F32) for a in (cq, sq, ck, sk))


def _window_sched(cnt, slot0):
    first = slot0 + cnt[:, :-1]
    end = slot0 + cnt[:, 1:]
    flat = lambda a: a.T.reshape(-1).astype(jnp.int32)
    rounds = jnp.max((end - (first // BF16_ROWS) * BF16_ROWS + WIN_ROWS - 1) // WIN_ROWS, axis=0)
    return flat(first), flat(end), jnp.maximum(rounds, 1).astype(jnp.int32)


def kernel(x_prompt, x_sample, c, cache_ckv, cache_kpe, state_hgrn, c_ctx, w_ada, b_ada, w_in, mla_q_norm, mla_w_uq, mla_kv_norm, mla_w_ukv, mla_w_o, hgrn_gamma, hgrn_norm, hgrn_w_o, w_out, ln1_g, ln1_b, moe_w_router, moe_w1, moe_w3, moe_w2, ln2_g, ln2_b):
    depth = w_ada.shape[0]
    assert depth == 1, "single trunk layer"
    bp, tp, d = x_prompt.shape
    bs, tsq, _ = x_sample.shape
    ne = moe_w_router.shape[-1]
    alpha = (2 * depth) ** 0.25
    past = cache_ckv.shape[2]
    assert tp % TOK_BLOCK == 0 and tsq % TOK_BLOCK == 0 and past % TOK_BLOCK == 0 and tsq % GRID_W == 0

    wts = _prep_weights(w_in[0], mla_q_norm[0], mla_w_uq[0], mla_kv_norm[0], mla_w_ukv[0], mla_w_o[0],
                        hgrn_norm[0], hgrn_w_o[0], w_out[0], ln1_g[0], ln1_b[0], moe_w_router[0])
    cond_rows = -(-(1 + bs) // SUBLANES) * SUBLANES
    cond = jnp.concatenate([c_ctx[None], c, jnp.zeros((cond_rows - 1 - bs, d), F32)], axis=0)
    mod = _adaln(cond, w_ada[0], b_ada[0])

    xs = [x_prompt.reshape(bp * tp, d), x_sample.reshape(bs * tsq, d)]
    dims = [(bp, tp), (bs, tsq)]
    rows = [(0, False), (1, True)]
    ropes = [None, _rope_tables(tsq)]
    kpe_c = jnp.pad(cache_kpe[:, 0].reshape(bs * past, MLA_ROPE), ((0, 0), (0, LANES - MLA_ROPE)))
    caches = [None, _kvup(cache_ckv[:, 0].reshape(bs * past, -1), kpe_c, wts)]
    inits = [None, state_hgrn[:, 0]]

    x1s, h2s, affs, extras = [], [], [], []
    for gi in range(2):
        (bt, sq), (row0, per_batch) = dims[gi], rows[gi]
        if sq == TOK_BLOCK and caches[gi] is None and ropes[gi] is None and not per_batch:
            x1, h2, aff, ckv, kpe, s_fin = _mixer(xs[gi], bt, sq, mod, hgrn_gamma, wts, alpha, ne, row0)
        else:
            q, k, v, ckv, kpe, hgx = _inproj(xs[gi], bt, sq, mod, hgrn_gamma, wts, row0, per_batch, ropes[gi])
            o_f, o_b, s_fin = _hgrn(hgx, bt, sq, inits[gi])
            x1, h2, aff = _postmix(xs[gi], bt, sq, mod, q, k, v, caches[gi], o_f, o_b, hgx, wts, alpha, ne,
                                   row0, per_batch)
        x1s.append(x1)
        h2s.append(h2)
        affs.append(aff)
        extras.append((ckv, kpe, s_fin))

    caps = [EC_FACTOR * x.shape[0] // ne for x in xs]
    slots = sum(caps)
    assert all(cp % BF16_ROWS == 0 for cp in caps) and slots >= WIN_ROWS
    groups, scheds = [], []
    slot0 = 0
    routed = _route(affs, caps)
    for gi in range(2):
        rank, cnt = routed[gi]
        groups.append(dict(h2=h2s[gi], rank=rank, aff=affs[gi], slot0=slot0, x1=x1s[gi], row0=rows[gi][0],
                           blocks_per_batch=dims[gi][1] // TOK_BLOCK if rows[gi][1] else 1 << 30))
        scheds.append(_window_sched(cnt, slot0))
        slot0 += caps[gi]
    first, end, rounds = [jnp.concatenate([s[k] for s in scheds]) for k in range(3)]
    xe = _compact(groups, first, end, rounds, slots)
    ye = _ffn(xe, moe_w1[0], moe_w3[0], moe_w2[0], slots, ft=moe_w1.shape[-1])
    outs = _combine(ye, groups, first, rounds, mod, ln2_g[0].reshape(1, -1), ln2_b[0].reshape(1, -1), alpha)

    ckv_p, kpe_p, st_p = extras[0]
    y_prompt = outs[0].reshape(bp, tp, d)
    y_sample = outs[1].reshape(bs, tsq, d)
    new_ckv = ckv_p.reshape(bp, 1, tp, -1)
    new_kpe = kpe_p[:, :MLA_ROPE].reshape(bp, 1, tp, MLA_ROPE)
    new_state = st_p.reshape(bp, 1, 2, HG_HEADS, HG_DK, HG_DV)
    return (y_prompt, y_sample, new_ckv, new_kpe, new_state)
```

```python
import functools
import math

import jax
import jax.numpy as jnp
import numpy as np
from jax import lax
from jax.experimental import pallas as pl
from jax.experimental.pallas import tpu as pltpu

F32 = jnp.float32
BF16 = jnp.bfloat16

MLA_HEADS = 8
MLA_NOPE = 64
MLA_ROPE = 32
MLA_V = 64
HG_HEADS = 4
HG_DK = 128
HG_DV = 128
HG_CHUNK = 32
GRID_W = 64
ROPE_BASE = 10000.0
EC_FACTOR = 2
EPS = 1e-6

LANES = 128
SUBLANES = 8
BF16_ROWS = 16
VMEM_LIMIT = 56 * 1024 * 1024

TOK_BLOCK = 256
WIN_ROWS = 64
UNSELECTED = -(1 << 30)

NT_DIMS = (((1,), (1,)), ((), ()))


def _dot(a, b):
    return jnp.dot(a, b, preferred_element_type=F32)


def _dot_nt(a, b):
    return lax.dot_general(a, b, NT_DIMS, preferred_element_type=F32)


def _silu(x):
    return x * jax.nn.sigmoid(x)


def _params(*sem):
    return pltpu.CompilerParams(dimension_semantics=sem, vmem_limit_bytes=VMEM_LIMIT)


def _const_spec(shape):
    zeros = (0,) * len(shape)
    return pl.BlockSpec(shape, lambda *_: zeros, pipeline_mode=pl.Buffered(1))


def _adaln_kernel(c_ref, w_ref, b_ref, o_ref):
    s = _silu(c_ref[...]).astype(BF16)
    o_ref[...] = _dot(s, w_ref[...].astype(BF16)) + b_ref[...]


def _adaln(cond, w_ada, b_ada):
    rows, d = cond.shape
    n = w_ada.shape[1]
    tn = n // 8
    return pl.pallas_call(
        _adaln_kernel,
        out_shape=jax.ShapeDtypeStruct((rows, n), F32),
        grid=(n // tn,),
        in_specs=[_const_spec((rows, d)),
                  pl.BlockSpec((d, tn), lambda j: (0, j)),
                  pl.BlockSpec((1, tn), lambda j: (0, j))],
        out_specs=pl.BlockSpec((rows, tn), lambda j: (0, j)),
        compiler_params=_params("arbitrary"),
        name="adaln",
    )(cond, w_ada, b_ada.reshape(1, n))


def _rms(x, g):
    return x * lax.rsqrt(jnp.mean(x * x, axis=-1, keepdims=True) + EPS) * g


def _rope(x, c, s):
    w = x.shape[-1]
    lane = lax.broadcasted_iota(jnp.int32, x.shape, 1)
    nxt = pltpu.roll(x, w - 1, 1)
    prv = pltpu.roll(x, 1, 1)
    return x * c + jnp.where(lane % 2 == 0, nxt, prv) * s


N_INPROJ_WEIGHTS = 6


def _mod_row(mod_ref, row0, per_batch):
    r = row0 + pl.program_id(0) if per_batch else row0
    return mod_ref[pl.ds(r, 1), :]


def _modulated(x_ref, m):
    d = x_ref.shape[1]
    return (x_ref[...] * (1.0 + m[:, d:2 * d]) + m[:, 0:d]).astype(BF16)


def _keys(k_nope, kpe):
    shared = pltpu.roll(kpe, MLA_NOPE, 1)
    return (k_nope + jnp.concatenate([shared] * MLA_HEADS, axis=1)).astype(BF16)


def _inproj_kernel(*refs, row0, per_batch, rope):
    x_ref, mod_ref = refs[:2]
    _inproj_body(x_ref, _mod_row(mod_ref, row0, per_batch), *refs[2:], rope=rope)


def _inproj_body(x_ref, m, *refs, rope):
    gam_ref, win_ref, qn_ref, wuq_ref, kvn_ref, wk_ref, wv_ref = refs[:1 + N_INPROJ_WEIGHTS]
    refs = refs[1 + N_INPROJ_WEIGHTS:]
    if rope:
        cq_ref, sq_ref, ck_ref, sk_ref = refs[:4]
        refs = refs[4:]
    q_o, k_o, v_o, ckv_o, kpe_o, hgx_o = refs
    h = _modulated(x_ref, m)
    hw = HG_HEADS * HG_DK
    o_kv = qn_ref.shape[1]
    o_pe = o_kv + kvn_ref.shape[1]
    o_h = o_pe + MLA_ROPE

    cq = _rms(_dot_nt(h, win_ref[0:o_kv, :]), qn_ref[...])
    q = _dot(cq.astype(BF16), wuq_ref[...])
    if rope:
        q = _rope(q, jnp.concatenate([cq_ref[...]] * MLA_HEADS, axis=1),
                  jnp.concatenate([sq_ref[...]] * MLA_HEADS, axis=1))
    q_o[...] = q.astype(BF16)

    ckv = _rms(_dot_nt(h, win_ref[o_kv:o_pe, :]), kvn_ref[...])
    ckv_o[...] = ckv
    kpe = _dot_nt(h, win_ref[o_pe:o_h, :])
    kpe = jnp.concatenate([kpe, jnp.zeros((kpe.shape[0], LANES - MLA_ROPE), F32)], axis=1)
    if rope:
        kpe = _rope(kpe, ck_ref[...], sk_ref[...])
    kpe_o[...] = kpe
    cb = ckv.astype(BF16)
    k_o[...] = _keys(_dot(cb, wk_ref[...]), kpe)
    v_o[...] = _dot(cb, wv_ref[...]).astype(BF16)

    z = _dot_nt(h, win_ref[o_h:o_h + 5 * hw, :])
    hgx_o[:, 0:hw] = _silu(z[:, 0:hw])
    for dr in range(2):
        g0, g1 = gam_ref[dr, 0:1, :], gam_ref[dr, 1:2, :]
        gmax = jnp.maximum(g0, g1)
        e0, e1 = jnp.exp(g0 - gmax), jnp.exp(g1 - gmax)
        lb = e0 / (e0 + e1)
        f = lb + (1.0 - lb) * jax.nn.sigmoid(z[:, (1 + dr) * hw:(2 + dr) * hw])
        hgx_o[:, (1 + 2 * dr) * hw:(2 + 2 * dr) * hw] = jnp.log(f)
        hgx_o[:, (2 + 2 * dr) * hw:(3 + 2 * dr) * hw] = 1.0 - f
    hgx_o[:, 5 * hw:6 * hw] = z[:, 3 * hw:4 * hw]
    hgx_o[:, 6 * hw:7 * hw] = z[:, 4 * hw:5 * hw]


def _inproj(x2d, batch, seq, mod, gamma, wts, row0, per_batch, rope_tabs):
    n, d = x2d.shape
    tm = MIXER_SEQS * TOK_BLOCK if seq % (MIXER_SEQS * TOK_BLOCK) == 0 else TOK_BLOCK
    nblk = seq // tm
    rope = rope_tabs is not None
    hp = MLA_HEADS * LANES
    hw = HG_HEADS * HG_DK
    tok = lambda b, i: (b * nblk + i, 0)
    pos = lambda b, i: (i, 0)
    weights = [wts[k] for k in INPROJ_KEYS]
    ins = [x2d, mod, gamma] + weights
    in_specs = ([pl.BlockSpec((tm, d), tok), _const_spec(mod.shape), _const_spec(gamma.shape)]
                + [_const_spec(w.shape) for w in weights])
    if rope:
        ins += list(rope_tabs)
        in_specs += [pl.BlockSpec((tm, t.shape[1]), pos) for t in rope_tabs]
    widths = [(hp, BF16), (hp, BF16), (MLA_HEADS * MLA_V, BF16), (wts["kvn"].shape[1], F32), (LANES, F32),
              (7 * hw, F32)]
    return pl.pallas_call(
        functools.partial(_inproj_kernel, row0=row0, per_batch=per_batch, rope=rope),
        out_shape=[jax.ShapeDtypeStruct((n, w), dt) for w, dt in widths],
        grid=(batch, nblk),
        in_specs=in_specs,
        out_specs=[pl.BlockSpec((tm, w), tok) for w, _ in widths],
        compiler_params=_params("arbitrary", "arbitrary"),
        name="inproj",
    )(*ins)


def _kvup_kernel(ckv_ref, kpe_ref, wk_ref, wv_ref, k_o, v_o):
    cb = ckv_ref[...].astype(BF16)
    k_o[...] = _keys(_dot(cb, wk_ref[...]), kpe_ref[...])
    v_o[...] = _dot(cb, wv_ref[...]).astype(BF16)


def _kvup(ckv2d, kpe2d, wts):
    n = ckv2d.shape[0]
    tm = TOK_BLOCK
    widths = [MLA_HEADS * LANES, MLA_HEADS * MLA_V]
    row = lambda i: (i, 0)
    ws = [wts["wk"], wts["wv"]]
    return pl.pallas_call(
        _kvup_kernel,
        out_shape=[jax.ShapeDtypeStruct((n, w), BF16) for w in widths],
        grid=(n // tm,),
        in_specs=[pl.BlockSpec((tm, ckv2d.shape[1]), row), pl.BlockSpec((tm, LANES), row)]
                 + [_const_spec(w.shape) for w in ws],
        out_specs=[pl.BlockSpec((tm, w), row) for w in widths],
        compiler_params=_params("arbitrary"),
        name="kvup",
    )(ckv2d, kpe2d, *ws)


ATTN_SCALE = (MLA_NOPE + MLA_ROPE) ** -0.5


def _attn_body(q_ref, k_ref, v_ref, kc_ref, vc_ref, o_ref):
    cached = kc_ref is not None
    scale = ATTN_SCALE * math.log2(math.e)
    per_slab = LANES // MLA_V
    own = lax.broadcasted_iota(jnp.int32, (q_ref.shape[0], LANES), 1) // MLA_V
    for slab in range(MLA_HEADS // per_slab):
        vsl = slice(slab * LANES, (slab + 1) * LANES)
        out = None
        for sub in range(per_slab):
            hd = slab * per_slab + sub
            sl = slice(hd * LANES, (hd + 1) * LANES)
            q = q_ref[:, sl]
            s = _dot_nt(q, k_ref[:, sl])
            mx = jnp.max(s, axis=-1, keepdims=True)
            if cached:
                s2 = _dot_nt(q, kc_ref[:, sl])
                mx = jnp.maximum(mx, jnp.max(s2, axis=-1, keepdims=True))
            e = jnp.exp2((s - mx) * scale)
            den = jnp.sum(e, axis=-1, keepdims=True)
            o = _dot(e.astype(BF16), v_ref[:, vsl])
            if cached:
                e2 = jnp.exp2((s2 - mx) * scale)
                den = den + jnp.sum(e2, axis=-1, keepdims=True)
                o = o + _dot(e2.astype(BF16), vc_ref[:, vsl])
            o = o / den
            out = o if out is None else jnp.where(own == sub, o, out)
        o_ref[:, vsl] = out.astype(o_ref.dtype)


def _chunk_scan(x, reverse):
    tm = x.shape[0]
    rin = lax.broadcasted_iota(jnp.int32, x.shape, 0) % HG_CHUNK
    step = 1
    while step < HG_CHUNK:
        if reverse:
            x = x + jnp.where(rin < HG_CHUNK - step, pltpu.roll(x, tm - step, 0), 0.0)
        else:
            x = x + jnp.where(rin >= step, pltpu.roll(x, step, 0), 0.0)
        step *= 2
    return x


def _hgrn_kernel(*refs, has_init):
    fwd, bwd = refs[0:4], refs[4:8]
    refs = refs[8:]
    s0_ref = None
    if has_init:
        s0_ref = refs[0]
        refs = refs[1:]
    of_ref, ob_ref, sfin_ref, st_scr = refs
    i = pl.program_id(1)
    _hgrn_body([(fwd, bwd, of_ref, ob_ref, sfin_ref)], s0_ref, st_scr, i == 0, i == pl.num_programs(1) - 1)


def _hgrn_body(jobs, s0_ref, st_scr, first, last):
    tm = jobs[0][0][0].shape[0]
    c = HG_CHUNK
    nch = tm // c
    dk, dv = HG_DK, HG_DV
    hw = HG_HEADS * dk

    def initial(dr, hd):
        return s0_ref[0, dr, hd].T if s0_ref is not None else jnp.zeros((dv, dk), F32)

    if st_scr is not None:
        @pl.when(first)
        def _init():
            for dr in range(2):
                for hd in range(HG_HEADS):
                    st_scr[dr, hd] = initial(dr, hd)

    npair = nch // 2
    pair = 2 * c
    row = lax.broadcasted_iota(jnp.int32, (tm, tm), 0)
    col = lax.broadcasted_iota(jnp.int32, (tm, tm), 1)
    same = (row // c) == (col // c)
    same_pair = (row // pair) == (col // pair)
    bd = (lax.broadcasted_iota(jnp.int32, (tm, npair * dk), 0) // pair
          == lax.broadcasted_iota(jnp.int32, (tm, npair * dk), 1) // dk)
    chunk_odd = (lax.broadcasted_iota(jnp.int32, (tm, hw), 0) // c) % 2 == 1

    for dr in range(2):
        tri = same & ((col <= row) if dr == 0 else (col >= row))
        cross = same_pair & (((row // c) > (col // c)) if dr == 0 else ((row // c) < (col // c)))
        second = chunk_odd if dr == 0 else ~chunk_odd
        order = range(npair) if dr == 0 else range(npair - 1, -1, -1)

        def decayed(job):
            hq_ref, lf_ref, kk_ref, vv_ref = job[dr]
            bcum = _chunk_scan(lf_ref[...], reverse=dr == 1)
            closing = c - 1 if dr == 0 else 0
            btot3 = bcum.reshape(nch, c, hw)[:, closing:closing + 1, :]
            btot = jnp.broadcast_to(btot3, (nch, c, hw)).reshape(tm, hw)
            bpart = jnp.where(chunk_odd, pltpu.roll(btot, c, 0), pltpu.roll(btot, tm - c, 0))
            epart = jnp.exp(bpart)
            kk = kk_ref[...]
            qd = hq_ref[...] * jnp.exp(bcum)
            kd = kk * jnp.exp(-bcum)
            ke = kk * jnp.exp(btot - bcum)
            qd2 = jnp.where(second, qd * epart, qd)
            ke2 = jnp.where(second, ke, ke * epart)
            return qd, kd, ke, qd2, ke2, btot + bpart, vv_ref[...]

        def head(job, hd, qd, kd, ke, qd2, ke2, bpair, vv):
            o_ref, sfin_ref = job[2 + dr], job[4]
            sl = slice(hd * dk, (hd + 1) * dk)
            qd_h = qd[:, sl].astype(BF16)
            v_h = vv[:, hd * dv:(hd + 1) * dv]
            a = jnp.where(tri, _dot_nt(qd_h, kd[:, sl].astype(BF16)),
                          jnp.where(cross, _dot_nt(qd_h, ke[:, sl].astype(BF16)), 0.0))
            o_intra = _dot(a.astype(BF16), v_h.astype(BF16))
            kebd = jnp.where(bd, jnp.concatenate([ke2[:, sl]] * npair, axis=1), 0.0).astype(BF16)
            qbd = jnp.where(bd, jnp.concatenate([qd2[:, sl]] * npair, axis=1), 0.0).astype(BF16)
            ut = _dot(v_h.T.astype(BF16), kebd)
            st = st_scr[dr, hd] if st_scr is not None else initial(dr, hd)
            prev = [None] * npair
            for p in order:
                prev[p] = st
                st = st * jnp.exp(bpair[p * pair:p * pair + 1, sl]) + ut[:, p * dk:(p + 1) * dk]
            if st_scr is not None:
                st_scr[dr, hd] = st
            o_inter = _dot_nt(qbd, jnp.concatenate(prev, axis=1).astype(BF16))
            o_ref[:, hd * dv:(hd + 1) * dv] = o_intra + o_inter

            if last is True:
                sfin_ref[0, dr, hd] = st.T
            else:
                @pl.when(last)
                def _final():
                    sfin_ref[0, dr, hd] = st.T

        prepared = [decayed(job) for job in jobs]
        for hd in range(HG_HEADS):
            for job, arrays in zip(jobs, prepared):
                head(job, hd, *arrays)


def _hgrn(hgx, batch, seq, s0=None):
    n = hgx.shape[0]
    tm = TOK_BLOCK
    nblk = seq // tm
    hw = HG_HEADS * HG_DK

    def spec(lane_blk, rev):
        if rev:
            return pl.BlockSpec((tm, hw), lambda b, i: (b * nblk + nblk - 1 - i, lane_blk))
        return pl.BlockSpec((tm, hw), lambda b, i: (b * nblk + i, lane_blk))

    in_specs = [spec(0, False), spec(1, False), spec(2, False), spec(5, False),
                spec(0, True), spec(3, True), spec(4, True), spec(5, True)]
    ins = [hgx] * 8
    st_shape = (1, 2, HG_HEADS, HG_DK, HG_DV)
    st_spec = pl.BlockSpec(st_shape, lambda b, i: (b, 0, 0, 0, 0))
    if s0 is not None:
        ins.append(s0)
        in_specs.append(st_spec)
    return pl.pallas_call(
        functools.partial(_hgrn_kernel, has_init=s0 is not None),
        out_shape=[jax.ShapeDtypeStruct((n, hw), F32), jax.ShapeDtypeStruct((n, hw), F32),
                   jax.ShapeDtypeStruct((batch,) + st_shape[1:], F32)],
        grid=(batch, nblk),
        in_specs=in_specs,
        out_specs=[spec(0, False), spec(0, True), st_spec],
        scratch_shapes=[pltpu.VMEM((2, HG_HEADS, HG_DV, HG_DK), F32)],
        compiler_params=_params("arbitrary", "arbitrary"),
        name="hgrn",
    )(*ins)


def _layer_norm(x, g, b):
    xc = x - jnp.mean(x, axis=-1, keepdims=True)
    var = jnp.mean(xc * xc, axis=-1, keepdims=True)
    return xc * lax.rsqrt(var + EPS) * g + b


N_POSTMIX_WEIGHTS = 8
INPROJ_KEYS = ("win", "qn", "wuq", "kvn", "wk", "wv")
POSTMIX_KEYS = ("hgn", "womla", "wohg", "wout", "ln1g", "ln1b", "wr")
MIXER_SEQS = 2


def _postmix_kernel(x_ref, mod_ref, *refs, alpha, row0, per_batch, cached):
    n_attn = 5 if cached else 3
    q_ref, k_ref, v_ref = refs[:3]
    kc_ref, vc_ref = refs[3:5] if cached else (None, None)
    of_ref, ob_ref, zg_ref = refs[n_attn:n_attn + 3]
    om_s = refs[-1]
    _attn_body(q_ref, k_ref, v_ref, kc_ref, vc_ref, om_s)
    _postmix_body(x_ref, _mod_row(mod_ref, row0, per_batch), of_ref, ob_ref, zg_ref, om_s,
                  *refs[n_attn + 3:-1], alpha=alpha)


def _postmix_body(x_ref, m, of_ref, ob_ref, zg_ref, om_ref, wg_ref, hgn_ref, womla_ref,
                  wohg_ref, wout_ref, lng_ref, lnb_ref, wr_ref, x1_o, h2_o, aff_o, *, alpha):
    d = x_ref.shape[1]
    tb = aff_o.shape[2]
    g1, sh2, sc2 = m[:, 2 * d:3 * d], m[:, 3 * d:4 * d], m[:, 4 * d:5 * d]
    o = of_ref[...] + ob_ref[...]
    zg = zg_ref[...]
    parts = []
    for hd in range(HG_HEADS):
        sl = slice(hd * HG_DV, (hd + 1) * HG_DV)
        parts.append(_rms(o[:, sl], hgn_ref[...]) * _silu(zg[:, sl]))
    ohg = jnp.concatenate(parts, axis=1).astype(BF16)
    gates = _dot_nt(_modulated(x_ref, m), wg_ref[wg_ref.shape[0] - 2 * d:, :])
    merged = (jax.nn.sigmoid(gates[:, 0:d]) * _dot(om_ref[...], womla_ref[...])
              + jax.nn.sigmoid(gates[:, d:2 * d]) * _dot(ohg, wohg_ref[...]))
    mix = _dot(merged.astype(BF16), wout_ref[...])
    x1 = _layer_norm(alpha * x_ref[...] + g1 * mix, lng_ref[...], lnb_ref[...])
    x1_o[...] = x1
    h2 = (x1 * (1.0 + sc2) + sh2).astype(BF16)
    h2_o[...] = h2
    logits = _dot_nt(wr_ref[...], h2)
    e = jnp.exp(logits - jnp.max(logits, axis=0, keepdims=True))
    aff = e / jnp.sum(e, axis=0, keepdims=True)
    for blk in range(aff_o.shape[0]):
        aff_o[blk] = aff[:, blk * tb:(blk + 1) * tb]


def _postmix(x2d, batch, seq, mod, q, k, v, cache, o_f, o_b, hgx, wts, alpha, n_experts, row0, per_batch):
    n, d = x2d.shape
    tm = TOK_BLOCK
    nblk = seq // tm
    hw = HG_HEADS * HG_DV
    hp, hv = q.shape[1], v.shape[1]
    tok = lambda b, i: (b * nblk + i, 0)
    per_seq = lambda b, i: (b, 0)
    weights = [wts[k] for k in POSTMIX_KEYS]
    attn_ins = [q, k, v]
    attn_specs = [pl.BlockSpec((tm, hp), tok), pl.BlockSpec((seq, hp), per_seq), pl.BlockSpec((seq, hv), per_seq)]
    if cache is not None:
        past = cache[0].shape[0] // batch
        attn_ins += list(cache)
        attn_specs += [pl.BlockSpec((past, hp), per_seq), pl.BlockSpec((past, hv), per_seq)]
    return pl.pallas_call(
        functools.partial(_postmix_kernel, alpha=alpha, row0=row0, per_batch=per_batch,
                          cached=cache is not None),
        out_shape=[jax.ShapeDtypeStruct((n, d), F32), jax.ShapeDtypeStruct((n, d), BF16),
                   jax.ShapeDtypeStruct((n // tm, n_experts, tm), F32)],
        grid=(batch, nblk),
        in_specs=[pl.BlockSpec((tm, d), tok), _const_spec(mod.shape)] + attn_specs
                 + [pl.BlockSpec((tm, hw), tok), pl.BlockSpec((tm, hw), tok),
                    pl.BlockSpec((tm, hw), lambda b, i: (b * nblk + i, 6)), _const_spec(wts["win"].shape)]
                 + [_const_spec(w.shape) for w in weights],
        out_specs=[pl.BlockSpec((tm, d), tok), pl.BlockSpec((tm, d), tok),
                   pl.BlockSpec((1, n_experts, tm), lambda b, i: (b * nblk + i, 0, 0))],
        scratch_shapes=[pltpu.VMEM((tm, hv), BF16)],
        compiler_params=_params("arbitrary", "arbitrary"),
        name="postmix",
    )(x2d, mod, *attn_ins, o_f, o_b, hgx, wts["win"], *weights)


def _mixer_kernel(x_ref, mod_ref, *refs, alpha, row0, seq):
    nw = 1 + N_INPROJ_WEIGHTS
    in_w, refs = refs[:nw], refs[nw:]
    pm_w, refs = refs[:N_POSTMIX_WEIGHTS - 1], refs[N_POSTMIX_WEIGHTS - 1:]
    pm_w = (in_w[1],) + tuple(pm_w)
    x1_o, h2_o, aff_o, ckv_o, kpe_o, sfin_o, q_s, k_s, v_s, hgx_s, om_s, of_s, ob_s = refs
    m = _mod_row(mod_ref, row0, False)
    _inproj_body(x_ref, m, *in_w, q_s, k_s, v_s, ckv_o, kpe_o, hgx_s, rope=False)
    hw = HG_HEADS * HG_DK
    jobs = []
    for s in range(x_ref.shape[0] // seq):
        rows = slice(s * seq, (s + 1) * seq)
        _attn_body(q_s.at[rows], k_s.at[rows], v_s.at[rows], None, None, om_s.at[rows])
        lane = lambda j, rows=rows: hgx_s.at[rows, j * hw:(j + 1) * hw]
        jobs.append(((lane(0), lane(1), lane(2), lane(5)), (lane(0), lane(3), lane(4), lane(5)),
                     of_s.at[rows], ob_s.at[rows], sfin_o.at[s:s + 1]))
    _hgrn_body(jobs, None, None, True, True)
    _postmix_body(x_ref, m, of_s, ob_s, hgx_s.at[:, 6 * hw:7 * hw], om_s, *pm_w, x1_o, h2_o, aff_o,
                  alpha=alpha)


def _mixer(x2d, batch, seq, mod, gamma, wts, alpha, n_experts, row0):
    n, d = x2d.shape
    assert seq == TOK_BLOCK
    ns = MIXER_SEQS if batch % MIXER_SEQS == 0 else 1
    tm = ns * seq
    hp = MLA_HEADS * LANES
    hv = MLA_HEADS * MLA_V
    hw = HG_HEADS * HG_DK
    kvl = wts["kvn"].shape[1]
    weights = [wts[k] for k in INPROJ_KEYS + POSTMIX_KEYS]
    tok = lambda b: (b, 0)
    st_shape = (ns, 2, HG_HEADS, HG_DK, HG_DV)
    return pl.pallas_call(
        functools.partial(_mixer_kernel, alpha=alpha, row0=row0, seq=seq),
        out_shape=[jax.ShapeDtypeStruct((n, d), F32), jax.ShapeDtypeStruct((n, d), BF16),
                   jax.ShapeDtypeStruct((n // seq, n_experts, seq), F32),
                   jax.ShapeDtypeStruct((n, kvl), F32), jax.ShapeDtypeStruct((n, LANES), F32),
                   jax.ShapeDtypeStruct((batch,) + st_shape[1:], F32)],
        grid=(batch // ns,),
        in_specs=[pl.BlockSpec((tm, d), tok), _const_spec(mod.shape), _const_spec(gamma.shape)]
                 + [_const_spec(w.shape) for w in weights],
        out_specs=[pl.BlockSpec((tm, d), tok), pl.BlockSpec((tm, d), tok),
                   pl.BlockSpec((ns, n_experts, seq), lambda b: (b, 0, 0)),
                   pl.BlockSpec((tm, kvl), tok), pl.BlockSpec((tm, LANES), tok),
                   pl.BlockSpec(st_shape, lambda b: (b, 0, 0, 0, 0))],
        scratch_shapes=[pltpu.VMEM((tm, hp), BF16), pltpu.VMEM((tm, hp), BF16), pltpu.VMEM((tm, hv), BF16),
                        pltpu.VMEM((tm, 7 * hw), F32), pltpu.VMEM((tm, hv), BF16),
                        pltpu.VMEM((tm, hw), F32), pltpu.VMEM((tm, hw), F32)],
        compiler_params=_params("arbitrary"),
        name="mixer",
    )(x2d, mod, gamma, *weights)


def _route_kernel(*refs, caps):
    ng = len(caps)
    for aff_ref, rank_o, cnt_o, cap in zip(refs[0:ng], refs[ng:2 * ng], refs[2 * ng:3 * ng], caps):
        _route_group(aff_ref, rank_o, cnt_o, cap)


def _route_group(aff_ref, rank_o, cnt_o, cap):
    nb, ne, tb = aff_ref.shape
    key = aff_ref[...]

    def count(mask):
        return jnp.sum(jnp.sum(jnp.where(mask, 1.0, 0.0), axis=0), axis=1, keepdims=True)

    def bit_step(it, bits):
        cand = bits | jnp.left_shift(jnp.int32(1), 30 - it)
        return jnp.where(count(key >= pltpu.bitcast(cand, F32)[None]) >= cap, cand, bits)

    bits = lax.fori_loop(0, 31, bit_step, jnp.zeros((ne, 1), jnp.int32))
    thr = pltpu.bitcast(bits, F32)
    need = cap - count(key > thr[None])
    before = (lax.broadcasted_iota(jnp.int32, (tb, tb), 0)
              < lax.broadcasted_iota(jnp.int32, (tb, tb), 1))
    before = jnp.where(before, 1.0, 0.0).astype(BF16)
    off_eq = jnp.zeros((ne, 1), F32)
    off_sel = jnp.zeros((ne, 1), F32)
    cnt_o[...] = jnp.zeros_like(cnt_o)
    for blk in range(nb):
        key_b = key[blk]
        eq = key_b == thr
        eq_b = jnp.where(eq, 1.0, 0.0)
        eq_rank = _dot(eq_b.astype(BF16), before) + off_eq
        sel = (key_b > thr) | (eq & (eq_rank < need))
        sel_b = jnp.where(sel, 1.0, 0.0)
        rank = _dot(sel_b.astype(BF16), before) + off_sel
        rank_o[blk] = jnp.where(sel, rank.astype(jnp.int32), UNSELECTED)
        cnt_o[:, blk:blk + 1] = off_sel.astype(jnp.int32)
        off_eq = off_eq + jnp.sum(eq_b, axis=1, keepdims=True)
        off_sel = off_sel + jnp.sum(sel_b, axis=1, keepdims=True)
    cnt_o[:, nb:nb + 1] = off_sel.astype(jnp.int32)


def _route(affs, caps):
    ng = len(affs)
    ne = affs[0].shape[1]
    assert all(a.shape[0] + 1 <= LANES for a in affs)
    outs = pl.pallas_call(
        functools.partial(_route_kernel, caps=tuple(caps)),
        out_shape=[jax.ShapeDtypeStruct(a.shape, jnp.int32) for a in affs]
                  + [jax.ShapeDtypeStruct((ne, LANES), jnp.int32)] * ng,
        in_specs=[pl.BlockSpec(memory_space=pltpu.VMEM)] * ng,
        out_specs=[pl.BlockSpec(memory_space=pltpu.VMEM)] * (2 * ng),
        compiler_params=pltpu.CompilerParams(vmem_limit_bytes=VMEM_LIMIT),
        name="route",
    )(*affs)
    return [(outs[gi], outs[ng + gi][:, :affs[gi].shape[0] + 1]) for gi in range(ng)]


def _window_hits(rk_ref, firsts, slot0, win):
    ne, tb = rk_ref.shape[1], rk_ref.shape[2]
    win_iota = lax.broadcasted_iota(jnp.int32, (win, tb), 0)
    return [(rk_ref[0, e:e + 1, :] + (slot0 - firsts[e])) == win_iota for e in range(ne)]


def _compact_kernel(first_ref, end_ref, rounds_ref, *refs, groups, slots):
    ng = len(groups)
    h2_refs, rk_refs, af_refs = refs[0:ng], refs[ng:2 * ng], refs[2 * ng:3 * ng]
    xe_hbm, stage, tail, sem, issued = refs[3 * ng:]
    hbms, stages, tails = [xe_hbm], [stage], [tail]
    b = pl.program_id(0)
    ne = rk_refs[0].shape[1]
    win = WIN_ROWS
    sub = BF16_ROWS

    def copies(slot, dsts):
        return [pltpu.make_async_copy(stage.at[slot, pl.ds(e * win, win), :],
                                      hbm.at[e, pl.ds(pl.multiple_of(dsts[e], sub), win), :], sem.at[e])
                for stage, hbm in zip(stages, hbms) for e in range(ne)]

    def wait_previous():
        @pl.when(issued[0] > 0)
        def _():
            for cp in copies(0, [0] * ne):
                cp.wait()

    @pl.when(b == 0)
    def _init():
        issued[0] = 0
        for stage, tail in zip(stages, tails):
            tail[...] = jnp.zeros_like(tail)
            stage[1] = jnp.zeros(stage.shape[1:], stage.dtype)
        pad = copies(1, [slots] * ne)
        for cp in pad:
            cp.start()
        for cp in pad:
            cp.wait()

    def group_body(h2_ref, rk_ref, af_ref, slot0):
        firsts = [first_ref[b * ne + e] for e in range(ne)]
        bases = [(f // sub) * sub for f in firsts]
        ends = [end_ref[b * ne + e] - bases[e] for e in range(ne)]

        def one_round(r, carry):
            dsts = [bases[e] + r * win for e in range(ne)]
            hits = _window_hits(rk_ref, dsts, slot0, win)
            onehot = jnp.where(jnp.concatenate(hits, axis=0), 1.0, 0.0).astype(BF16)
            gate = jnp.concatenate(
                [jnp.sum(jnp.where(hits[e], af_ref[0, e:e + 1, :], 0.0), axis=1, keepdims=True)
                 for e in range(ne)], axis=0)
            slot = issued[0] % 2
            gate = jnp.broadcast_to(gate, (ne * win, LANES))
            g1 = gate.astype(BF16).astype(F32)
            g2 = (gate - g1).astype(BF16).astype(F32)
            g3 = (gate - g1 - g2).astype(BF16).astype(F32)
            lane = lax.broadcasted_iota(jnp.int32, (ne * win, LANES), 1)
            pieces3 = jnp.where(lane == 0, g1, jnp.where(lane == 1, g2, jnp.where(lane == 2, g3, 0.0)))
            streams = (jnp.concatenate([_dot(onehot, h2_ref[...]).astype(BF16),
                                        pieces3.astype(BF16)], axis=1),)
            for val, stage, tail in zip(streams, stages, tails):
                sub_iota = lax.broadcasted_iota(jnp.int32, (sub, val.shape[1]), 0)
                pieces = []
                for e in range(ne):
                    old = tail[e * sub:(e + 1) * sub, :]
                    shared = jnp.where(r == 0, firsts[e] - bases[e], 0)
                    groups_e = [val[e * win + g * sub:e * win + (g + 1) * sub, :] for g in range(win // sub)]
                    groups_e[0] = jnp.where(sub_iota < shared, old, groups_e[0])
                    pieces += groups_e
                    last = (ends[e] // sub) * sub
                    new = old
                    for g, grp in enumerate(groups_e):
                        new = jnp.where((r == last // win) & (last % win == g * sub), grp, new)
                    tail[e * sub:(e + 1) * sub, :] = new
                stage[slot] = jnp.concatenate(pieces, axis=0)
            wait_previous()
            for e, cp in enumerate(copies(slot, [jnp.minimum(dst, slots) for dst in dsts])):
                cp.start(priority=e % 2)
            issued[0] = issued[0] + 1
            return carry

        lax.fori_loop(0, rounds_ref[b], one_round, 0)

    blk0 = 0
    for gi, g in enumerate(groups):
        @pl.when((b >= blk0) & (b < blk0 + g["nb"]))
        def _(gi=gi, g=g):
            group_body(h2_refs[gi], rk_refs[gi], af_refs[gi], g["slot0"])
        blk0 += g["nb"]

    @pl.when(b == pl.num_programs(0) - 1)
    def _drain():
        wait_previous()


def _compact(groups, first, end, rounds, slots):
    d = groups[0]["h2"].shape[1]
    nbs = [g["rank"].shape[0] for g in groups]
    ne, tb = groups[0]["rank"].shape[1:]
    meta, specs_h2, specs_rk = [], [], []
    blk0 = 0
    for g, nb in zip(groups, nbs):
        meta.append(dict(nb=nb, slot0=g["slot0"]))
        local = lambda b, *_, blk0=blk0, nb=nb: jnp.clip(b - blk0, 0, nb - 1)
        specs_h2.append(pl.BlockSpec((tb, d), lambda b, *_, local=local: (local(b), 0)))
        specs_rk.append(pl.BlockSpec((1, ne, tb), lambda b, *_, local=local: (local(b), 0, 0)))
        blk0 += nb
    width = d + LANES
    return pl.pallas_call(
        functools.partial(_compact_kernel, groups=meta, slots=slots),
        out_shape=jax.ShapeDtypeStruct((ne, slots + WIN_ROWS, width), BF16),
        grid_spec=pltpu.PrefetchScalarGridSpec(
            num_scalar_prefetch=3,
            grid=(sum(nbs),),
            in_specs=specs_h2 + specs_rk + specs_rk,
            out_specs=pl.BlockSpec(memory_space=pl.ANY),
            scratch_shapes=[pltpu.VMEM((2, ne * WIN_ROWS, width), BF16),
                            pltpu.VMEM((ne * BF16_ROWS, width), BF16),
                            pltpu.SemaphoreType.DMA((ne,)), pltpu.SMEM((1,), jnp.int32)]),
        compiler_params=_params("arbitrary"),
        name="compact",
    )(first, end, rounds, *[g["h2"] for g in groups], *[g["rank"] for g in groups],
      *[g["aff"] for g in groups])


def _ffn_kernel(xe_ref, w1_ref, w3_ref, w2_ref, ye_ref, *scratch):
    f = pl.program_id(1)
    d = w1_ref.shape[1]
    x = xe_ref[0, :, 0:d]
    pieces = xe_ref[0, :, d:d + LANES].astype(F32)
    gate = pieces[:, 0:1] + pieces[:, 1:2] + pieces[:, 2:3]
    hid = _silu(_dot(x, w1_ref[0].astype(BF16))) * _dot(x, w3_ref[0].astype(BF16))
    y = _dot(hid.astype(BF16), w2_ref[0].astype(BF16))
    if not scratch:
        ye_ref[0] = (y * gate).astype(ye_ref.dtype)
        return
    acc_scr, = scratch
    last = pl.num_programs(1) - 1

    @pl.when(f == 0)
    def _first():
        acc_scr[...] = y

    @pl.when((f > 0) & (f < last))
    def _middle():
        acc_scr[...] += y

    @pl.when(f == last)
    def _last():
        ye_ref[0] = ((acc_scr[...] + y) * gate).astype(ye_ref.dtype)


def _ffn(xe, w1, w3, w2, slots, ft):
    ne, d, dff = w1.shape
    nf = dff // ft
    return pl.pallas_call(
        _ffn_kernel,
        out_shape=jax.ShapeDtypeStruct((ne, slots, d), BF16),
        grid=(ne, nf),
        in_specs=[pl.BlockSpec((1, slots, xe.shape[2]), lambda e, f: (e, 0, 0)),
                  pl.BlockSpec((1, d, ft), lambda e, f: (e, 0, f)),
                  pl.BlockSpec((1, d, ft), lambda e, f: (e, 0, f)),
                  pl.BlockSpec((1, ft, d), lambda e, f: (e, f, 0))],
        out_specs=pl.BlockSpec((1, slots, d), lambda e, f: (e, 0, 0)),
        scratch_shapes=[pltpu.VMEM((slots, d), F32)] if nf > 1 else [],
        compiler_params=_params("arbitrary", "arbitrary"),
        name="ffn",
    )(xe, w1, w3, w2)


def _combine_kernel(first_ref, rounds_ref, *refs, groups, alpha, slots):
    ng = len(groups)
    rk_refs, x1_refs = refs[0:ng], refs[ng:2 * ng]
    mod_ref, lng_ref, lnb_ref, ye_hbm = refs[2 * ng:2 * ng + 4]
    out_refs = refs[2 * ng + 4:3 * ng + 4]
    buf, acc_scr, sem = refs[3 * ng + 4:]
    d = x1_refs[0].shape[1]
    b = pl.program_id(0)
    nblk = pl.num_programs(0)
    ne, tb = rk_refs[0].shape[1], rk_refs[0].shape[2]
    win = WIN_ROWS

    def starts_of(blk, r):
        firsts = [(first_ref[blk * ne + e] // BF16_ROWS) * BF16_ROWS + r * win for e in range(ne)]
        return firsts, [jnp.minimum(f, slots - win) for f in firsts]

    def windows(slot, starts):
        return [pltpu.make_async_copy(ye_hbm.at[e, pl.ds(pl.multiple_of(starts[e], BF16_ROWS), win), :],
                                      buf.at[slot, pl.ds(e * win, win), :], sem.at[slot, e])
                for e in range(ne)]

    def scatter(rk_ref, slot0, slot, firsts, starts, later_round):
        hits = _window_hits(rk_ref, starts, slot0, win)
        if later_round:
            hits = [h & ((rk_ref[0, e:e + 1, :] + slot0) >= firsts[e]) for e, h in enumerate(hits)]
        hit = jnp.where(jnp.concatenate(hits, axis=0), 1.0, 0.0)
        return _dot(hit.T.astype(BF16), buf[slot])

    cur = b % 2

    @pl.when(b == 0)
    def _prime():
        for e, cp in enumerate(windows(0, starts_of(0, 0)[1])):
            cp.start(priority=e % 2)

    @pl.when(b + 1 < nblk)
    def _prefetch():
        for e, cp in enumerate(windows(1 - cur, starts_of(b + 1, 0)[1])):
            cp.start(priority=e % 2)

    firsts0, starts0 = starts_of(b, 0)
    for cp in windows(cur, starts0):
        cp.wait()

    def group_body(rk_ref, x1_ref, out_ref, g, local):
        acc_scr[...] = scatter(rk_ref, g["slot0"], cur, firsts0, starts0, False)

        def extra_round(r, carry):
            firsts, starts = starts_of(b, r)
            for cp in windows(2, starts):
                cp.start()
            for cp in windows(2, starts):
                cp.wait()
            acc_scr[...] += scatter(rk_ref, g["slot0"], 2, firsts, starts, True)
            return carry

        lax.fori_loop(1, rounds_ref[b], extra_round, 0)
        r = g["row0"] + local // g["blocks_per_batch"]
        g2 = mod_ref[pl.ds(r, 1), :][:, 5 * d:6 * d]
        out_ref[...] = _layer_norm(alpha * x1_ref[...] + g2 * acc_scr[...], lng_ref[...], lnb_ref[...])

    blk0 = 0
    for gi, g in enumerate(groups):
        @pl.when((b >= blk0) & (b < blk0 + g["nb"]))
        def _(gi=gi, g=g, blk0=blk0):
            group_body(rk_refs[gi], x1_refs[gi], out_refs[gi], g, b - blk0)
        blk0 += g["nb"]


def _combine(ye, groups, first, rounds, mod, ln_g, ln_b, alpha):
    d = groups[0]["x1"].shape[1]
    ne, tb = groups[0]["rank"].shape[1:]
    slots = ye.shape[1]
    meta, specs_rk, specs_x1 = [], [], []
    blk0 = 0
    for g in groups:
        nb = g["rank"].shape[0]
        meta.append(dict(nb=nb, slot0=g["slot0"], row0=g["row0"], blocks_per_batch=g["blocks_per_batch"]))
        local = lambda b, *_, blk0=blk0, nb=nb: jnp.clip(b - blk0, 0, nb - 1)
        specs_rk.append(pl.BlockSpec((1, ne, tb), lambda b, *_, local=local: (local(b), 0, 0)))
        specs_x1.append(pl.BlockSpec((tb, d), lambda b, *_, local=local: (local(b), 0)))
        blk0 += nb
    const = lambda shape: pl.BlockSpec(shape, lambda b, *_: (0,) * len(shape))
    return pl.pallas_call(
        functools.partial(_combine_kernel, groups=meta, alpha=alpha, slots=slots),
        out_shape=[jax.ShapeDtypeStruct(g["x1"].shape, F32) for g in groups],
        grid_spec=pltpu.PrefetchScalarGridSpec(
            num_scalar_prefetch=2,
            grid=(blk0,),
            in_specs=specs_rk + specs_x1 + [const(mod.shape), const((1, d)), const((1, d)),
                                            pl.BlockSpec(memory_space=pl.ANY)],
            out_specs=specs_x1,
            scratch_shapes=[pltpu.VMEM((3, ne * WIN_ROWS, d), ye.dtype), pltpu.VMEM((tb, d), F32),
                            pltpu.SemaphoreType.DMA((3, ne))]),
        compiler_params=_params("arbitrary"),
        name="combine",
    )(first, rounds, *[g["rank"] for g in groups], *[g["x1"] for g in groups], mod, ln_g, ln_b, ye)


def _prep_weights(w_in, q_norm, w_uq, kv_norm, w_ukv, w_o_mla, hgrn_norm, w_o_hg, w_out, ln1_g, ln1_b,
                  w_router):
    d = w_in.shape[0]
    q_lora, kv_lora = q_norm.shape[0], kv_norm.shape[0]
    hw = HG_HEADS * HG_DK
    hh, hp = MLA_HEADS, MLA_HEADS * LANES
    o_kv, o_pe = q_lora, q_lora + kv_lora
    o_h = o_pe + MLA_ROPE
    o_g = o_h + 5 * hw
    assert w_in.shape[1] == o_g + 2 * d
    qk = MLA_NOPE + MLA_ROPE
    kvw = MLA_NOPE + MLA_V
    b16 = lambda a: a.astype(BF16)
    assert all(o % BF16_ROWS == 0 for o in (o_kv, o_pe, o_h, o_g))
    win = b16(w_in.T)
    wuq = jnp.pad(w_uq.reshape(q_lora, hh, qk), ((0, 0), (0, 0), (0, LANES - qk))).reshape(q_lora, hp)
    ukv = w_ukv.reshape(kv_lora, hh, kvw)
    wk = jnp.pad(ukv[:, :, :MLA_NOPE], ((0, 0), (0, 0), (0, LANES - MLA_NOPE))).reshape(kv_lora, hp)
    wv = ukv[:, :, MLA_NOPE:].reshape(kv_lora, hh * MLA_V)
    return dict(
        win=win, qn=q_norm.reshape(1, -1), wuq=b16(wuq), kvn=kv_norm.reshape(1, -1), wk=b16(wk), wv=b16(wv),
        hgn=hgrn_norm.reshape(1, -1), womla=b16(w_o_mla), wohg=b16(w_o_hg), wout=b16(w_out),
        ln1g=ln1_g.reshape(1, -1), ln1b=ln1_b.reshape(1, -1), wr=b16(w_router.T))


def _rope_tables(seq):
    n_freq = MLA_ROPE // 4
    inv = ROPE_BASE ** (-np.arange(n_freq, dtype=np.float64) / n_freq)
    t = np.arange(seq)
    ang = np.concatenate([(t // GRID_W)[:, None] * inv, (t % GRID_W)[:, None] * inv], axis=-1)
    cos = np.repeat(np.cos(ang), 2, axis=1)
    sin = np.repeat(np.sin(ang), 2, axis=1) * np.tile([-1.0, 1.0], MLA_ROPE // 2)
    ck = np.pad(cos, ((0, 0), (0, LANES - MLA_ROPE)), constant_values=1.0)
    sk = np.pad(sin, ((0, 0), (0, LANES - MLA_ROPE)))
    cq = np.pad(cos, ((0, 0), (MLA_NOPE, LANES - MLA_NOPE - MLA_ROPE)), constant_values=1.0)
    sq = np.pad(sin, ((0, 0), (MLA_NOPE, LANES - MLA_NOPE - MLA_ROPE)))
    return tuple(jnp.asarray(a, F32) for a in (cq, sq, ck, sk))


def _window_sched(cnt, slot0):
    first = slot0 + cnt[:, :-1]
    end = slot0 + cnt[:, 1:]
    flat = lambda a: a.T.reshape(-1).astype(jnp.int32)
    rounds = jnp.max((end - (first // BF16_ROWS) * BF16_ROWS + WIN_ROWS - 1) // WIN_ROWS, axis=0)
    return flat(first), flat(end), jnp.maximum(rounds, 1).astype(jnp.int32)


def kernel(x_prompt, x_sample, c, cache_ckv, cache_kpe, state_hgrn, c_ctx, w_ada, b_ada, w_in, mla_q_norm, mla_w_uq, mla_kv_norm, mla_w_ukv, mla_w_o, hgrn_gamma, hgrn_norm, hgrn_w_o, w_out, ln1_g, ln1_b, moe_w_router, moe_w1, moe_w3, moe_w2, ln2_g, ln2_b):
    depth = w_ada.shape[0]
    assert depth == 1, "single trunk layer"
    bp, tp, d = x_prompt.shape
    bs, tsq, _ = x_sample.shape
    ne = moe_w_router.shape[-1]
    alpha = (2 * depth) ** 0.25
    past = cache_ckv.shape[2]
    assert tp % TOK_BLOCK == 0 and tsq % TOK_BLOCK == 0 and past % TOK_BLOCK == 0 and tsq % GRID_W == 0

    wts = _prep_weights(w_in[0], mla_q_norm[0], mla_w_uq[0], mla_kv_norm[0], mla_w_ukv[0], mla_w_o[0],
                        hgrn_norm[0], hgrn_w_o[0], w_out[0], ln1_g[0], ln1_b[0], moe_w_router[0])
    cond_rows = -(-(1 + bs) // SUBLANES) * SUBLANES
    cond = jnp.concatenate([c_ctx[None], c, jnp.zeros((cond_rows - 1 - bs, d), F32)], axis=0)
    mod = _adaln(cond, w_ada[0], b_ada[0])

    xs = [x_prompt.reshape(bp * tp, d), x_sample.reshape(bs * tsq, d)]
    dims = [(bp, tp), (bs, tsq)]
    rows = [(0, False), (1, True)]
    ropes = [None, _rope_tables(tsq)]
    kpe_c = jnp.pad(cache_kpe[:, 0].reshape(bs * past, MLA_ROPE), ((0, 0), (0, LANES - MLA_ROPE)))
    caches = [None, _kvup(cache_ckv[:, 0].reshape(bs * past, -1), kpe_c, wts)]
    inits = [None, state_hgrn[:, 0]]

    x1s, h2s, affs, extras = [], [], [], []
    for gi in range(2):
        (bt, sq), (row0, per_batch) = dims[gi], rows[gi]
        if sq == TOK_BLOCK and caches[gi] is None and ropes[gi] is None and not per_batch:
            x1, h2, aff, ckv, kpe, s_fin = _mixer(xs[gi], bt, sq, mod, hgrn_gamma, wts, alpha, ne, row0)
        else:
            q, k, v, ckv, kpe, hgx = _inproj(xs[gi], bt, sq, mod, hgrn_gamma, wts, row0, per_batch, ropes[gi])
            o_f, o_b, s_fin = _hgrn(hgx, bt, sq, inits[gi])
            x1, h2, aff = _postmix(xs[gi], bt, sq, mod, q, k, v, caches[gi], o_f, o_b, hgx, wts, alpha, ne,
                                   row0, per_batch)
        x1s.append(x1)
        h2s.append(h2)
        affs.append(aff)
        extras.append((ckv, kpe, s_fin))

    caps = [EC_FACTOR * x.shape[0] // ne for x in xs]
    slots = sum(caps)
    assert all(cp % BF16_ROWS == 0 for cp in caps) and slots >= WIN_ROWS
    groups, scheds = [], []
    slot0 = 0
    routed = _route(affs, caps)
    for gi in range(2):
        rank, cnt = routed[gi]
        groups.append(dict(h2=h2s[gi], rank=rank, aff=affs[gi], slot0=slot0, x1=x1s[gi], row0=rows[gi][0],
                           blocks_per_batch=dims[gi][1] // TOK_BLOCK if rows[gi][1] else 1 << 30))
        scheds.append(_window_sched(cnt, slot0))
        slot0 += caps[gi]
    first, end, rounds = [jnp.concatenate([s[k] for s in scheds]) for k in range(3)]
    xe = _compact(groups, first, end, rounds, slots)
    ye = _ffn(xe, moe_w1[0], moe_w3[0], moe_w2[0], slots, ft=moe_w1.shape[-1])
    outs = _combine(ye, groups, first, rounds, mod, ln2_g[0].reshape(1, -1), ln2_b[0].reshape(1, -1), alpha)

    ckv_p, kpe_p, st_p = extras[0]
    y_prompt = outs[0].reshape(bp, tp, d)
    y_sample = outs[1].reshape(bs, tsq, d)
    new_ckv = ckv_p.reshape(bp, 1, tp, -1)
    new_kpe = kpe_p[:, :MLA_ROPE].reshape(bp, 1, tp, MLA_ROPE)
    new_state = st_p.reshape(bp, 1, 2, HG_HEADS, HG_DK, HG_DV)
    return (y_prompt, y_sample, new_ckv, new_kpe, new_state)
```

```python
import functools
import math

import jax
import jax.numpy as jnp
import numpy as np
from jax import lax
from jax.experimental import pallas as pl
from jax.experimental.pallas import tpu as pltpu

F32 = jnp.float32
BF16 = jnp.bfloat16

MLA_HEADS = 8
MLA_NOPE = 64
MLA_ROPE = 32
MLA_V = 64
HG_HEADS = 4
HG_DK = 128
HG_DV = 128
HG_CHUNK = 32
GRID_W = 64
ROPE_BASE = 10000.0
EC_FACTOR = 2
EPS = 1e-6

LANES = 128
SUBLANES = 8
BF16_ROWS = 16
VMEM_LIMIT = 56 * 1024 * 1024

TOK_BLOCK = 256
WIN_ROWS = 64
UNSELECTED = -(1 << 30)

NT_DIMS = (((1,), (1,)), ((), ()))


def _dot(a, b):
    return jnp.dot(a, b, preferred_element_type=F32)


def _dot_nt(a, b):
    return lax.dot_general(a, b, NT_DIMS, preferred_element_type=F32)


def _silu(x):
    return x * jax.nn.sigmoid(x)


def _params(*sem):
    return pltpu.CompilerParams(dimension_semantics=sem, vmem_limit_bytes=VMEM_LIMIT)


def _const_spec(shape):
    zeros = (0,) * len(shape)
    return pl.BlockSpec(shape, lambda *_: zeros, pipeline_mode=pl.Buffered(1))


def _adaln_kernel(c_ref, w_ref, b_ref, o_ref):
    s = _silu(c_ref[...]).astype(BF16)
    o_ref[...] = _dot(s, w_ref[...].astype(BF16)) + b_ref[...]


def _adaln(cond, w_ada, b_ada):
    rows, d = cond.shape
    n = w_ada.shape[1]
    tn = n // 4
    return pl.pallas_call(
        _adaln_kernel,
        out_shape=jax.ShapeDtypeStruct((rows, n), F32),
        grid=(n // tn,),
        in_specs=[_const_spec((rows, d)),
                  pl.BlockSpec((d, tn), lambda j: (0, j)),
                  pl.BlockSpec((1, tn), lambda j: (0, j))],
        out_specs=pl.BlockSpec((rows, tn), lambda j: (0, j)),
        compiler_params=_params("arbitrary"),
        name="adaln",
    )(cond, w_ada, b_ada.reshape(1, n))


def _rms(x, g):
    return x * lax.rsqrt(jnp.mean(x * x, axis=-1, keepdims=True) + EPS) * g


def _rope(x, c, s):
    w = x.shape[-1]
    lane = lax.broadcasted_iota(jnp.int32, x.shape, 1)
    nxt = pltpu.roll(x, w - 1, 1)
    prv = pltpu.roll(x, 1, 1)
    return x * c + jnp.where(lane % 2 == 0, nxt, prv) * s


N_INPROJ_WEIGHTS = 6


def _mod_row(mod_ref, row0, per_batch):
    r = row0 + pl.program_id(0) if per_batch else row0
    return mod_ref[pl.ds(r, 1), :]


def _modulated(x_ref, m):
    d = x_ref.shape[1]
    return (x_ref[...] * (1.0 + m[:, d:2 * d]) + m[:, 0:d]).astype(BF16)


def _keys(k_nope, kpe):
    shared = pltpu.roll(kpe, MLA_NOPE, 1)
    return (k_nope + jnp.concatenate([shared] * MLA_HEADS, axis=1)).astype(BF16)


def _inproj_kernel(*refs, row0, per_batch, rope):
    x_ref, mod_ref = refs[:2]
    _inproj_body(x_ref, _mod_row(mod_ref, row0, per_batch), *refs[2:], rope=rope)


def _inproj_body(x_ref, m, *refs, rope):
    gam_ref, win_ref, qn_ref, wuq_ref, kvn_ref, wk_ref, wv_ref = refs[:1 + N_INPROJ_WEIGHTS]
    refs = refs[1 + N_INPROJ_WEIGHTS:]
    if rope:
        cq_ref, sq_ref, ck_ref, sk_ref = refs[:4]
        refs = refs[4:]
    q_o, k_o, v_o, ckv_o, kpe_o, hgx_o = refs
    h = _modulated(x_ref, m)
    hw = HG_HEADS * HG_DK
    o_kv = qn_ref.shape[1]
    o_pe = o_kv + kvn_ref.shape[1]
    o_h = o_pe + MLA_ROPE

    cq = _rms(_dot_nt(h, win_ref[0:o_kv, :]), qn_ref[...])
    q = _dot(cq.astype(BF16), wuq_ref[...])
    if rope:
        q = _rope(q, jnp.concatenate([cq_ref[...]] * MLA_HEADS, axis=1),
                  jnp.concatenate([sq_ref[...]] * MLA_HEADS, axis=1))
    q_o[...] = q.astype(BF16)

    ckv = _rms(_dot_nt(h, win_ref[o_kv:o_pe, :]), kvn_ref[...])
    ckv_o[...] = ckv
    kpe = _dot_nt(h, win_ref[o_pe:o_h, :])
    kpe = jnp.concatenate([kpe, jnp.zeros((kpe.shape[0], LANES - MLA_ROPE), F32)], axis=1)
    if rope:
        kpe = _rope(kpe, ck_ref[...], sk_ref[...])
    kpe_o[...] = kpe
    cb = ckv.astype(BF16)
    k_o[...] = _keys(_dot(cb, wk_ref[...]), kpe)
    v_o[...] = _dot(cb, wv_ref[...]).astype(BF16)

    z = _dot_nt(h, win_ref[o_h:o_h + 5 * hw, :])
    hgx_o[:, 0:hw] = _silu(z[:, 0:hw])
    for dr in range(2):
        g0, g1 = gam_ref[dr, 0:1, :], gam_ref[dr, 1:2, :]
        gmax = jnp.maximum(g0, g1)
        e0, e1 = jnp.exp(g0 - gmax), jnp.exp(g1 - gmax)
        lb = e0 / (e0 + e1)
        f = lb + (1.0 - lb) * jax.nn.sigmoid(z[:, (1 + dr) * hw:(2 + dr) * hw])
        hgx_o[:, (1 + 2 * dr) * hw:(2 + 2 * dr) * hw] = jnp.log(f)
        hgx_o[:, (2 + 2 * dr) * hw:(3 + 2 * dr) * hw] = 1.0 - f
    hgx_o[:, 5 * hw:6 * hw] = z[:, 3 * hw:4 * hw]
    hgx_o[:, 6 * hw:7 * hw] = z[:, 4 * hw:5 * hw]


def _inproj(x2d, batch, seq, mod, gamma, wts, row0, per_batch, rope_tabs):
    n, d = x2d.shape
    tm = MIXER_SEQS * TOK_BLOCK if seq % (MIXER_SEQS * TOK_BLOCK) == 0 else TOK_BLOCK
    nblk = seq // tm
    rope = rope_tabs is not None
    hp = MLA_HEADS * LANES
    hw = HG_HEADS * HG_DK
    tok = lambda b, i: (b * nblk + i, 0)
    pos = lambda b, i: (i, 0)
    weights = [wts[k] for k in INPROJ_KEYS]
    ins = [x2d, mod, gamma] + weights
    in_specs = ([pl.BlockSpec((tm, d), tok), _const_spec(mod.shape), _const_spec(gamma.shape)]
                + [_const_spec(w.shape) for w in weights])
    if rope:
        ins += list(rope_tabs)
        in_specs += [pl.BlockSpec((tm, t.shape[1]), pos) for t in rope_tabs]
    widths = [(hp, BF16), (hp, BF16), (MLA_HEADS * MLA_V, BF16), (wts["kvn"].shape[1], F32), (LANES, F32),
              (7 * hw, F32)]
    return pl.pallas_call(
        functools.partial(_inproj_kernel, row0=row0, per_batch=per_batch, rope=rope),
        out_shape=[jax.ShapeDtypeStruct((n, w), dt) for w, dt in widths],
        grid=(batch, nblk),
        in_specs=in_specs,
        out_specs=[pl.BlockSpec((tm, w), tok) for w, _ in widths],
        compiler_params=_params("arbitrary", "arbitrary"),
        name="inproj",
    )(*ins)


def _kvup_kernel(ckv_ref, kpe_ref, wk_ref, wv_ref, k_o, v_o):
    cb = ckv_ref[...].astype(BF16)
    k_o[...] = _keys(_dot(cb, wk_ref[...]), kpe_ref[...])
    v_o[...] = _dot(cb, wv_ref[...]).astype(BF16)


def _kvup(ckv2d, kpe2d, wts):
    n = ckv2d.shape[0]
    tm = TOK_BLOCK
    widths = [MLA_HEADS * LANES, MLA_HEADS * MLA_V]
    row = lambda i: (i, 0)
    ws = [wts["wk"], wts["wv"]]
    return pl.pallas_call(
        _kvup_kernel,
        out_shape=[jax.ShapeDtypeStruct((n, w), BF16) for w in widths],
        grid=(n // tm,),
        in_specs=[pl.BlockSpec((tm, ckv2d.shape[1]), row), pl.BlockSpec((tm, LANES), row)]
                 + [_const_spec(w.shape) for w in ws],
        out_specs=[pl.BlockSpec((tm, w), row) for w in widths],
        compiler_params=_params("arbitrary"),
        name="kvup",
    )(ckv2d, kpe2d, *ws)


ATTN_SCALE = (MLA_NOPE + MLA_ROPE) ** -0.5


def _attn_body(q_ref, k_ref, v_ref, kc_ref, vc_ref, o_ref):
    cached = kc_ref is not None
    scale = ATTN_SCALE * math.log2(math.e)
    per_slab = LANES // MLA_V
    own = lax.broadcasted_iota(jnp.int32, (q_ref.shape[0], LANES), 1) // MLA_V
    for slab in range(MLA_HEADS // per_slab):
        vsl = slice(slab * LANES, (slab + 1) * LANES)
        out = None
        for sub in range(per_slab):
            hd = slab * per_slab + sub
            sl = slice(hd * LANES, (hd + 1) * LANES)
            q = q_ref[:, sl]
            s = _dot_nt(q, k_ref[:, sl])
            mx = jnp.max(s, axis=-1, keepdims=True)
            if cached:
                s2 = _dot_nt(q, kc_ref[:, sl])
                mx = jnp.maximum(mx, jnp.max(s2, axis=-1, keepdims=True))
            e = jnp.exp2((s - mx) * scale)
            den = jnp.sum(e, axis=-1, keepdims=True)
            o = _dot(e.astype(BF16), v_ref[:, vsl])
            if cached:
                e2 = jnp.exp2((s2 - mx) * scale)
                den = den + jnp.sum(e2, axis=-1, keepdims=True)
                o = o + _dot(e2.astype(BF16), vc_ref[:, vsl])
            o = o / den
            out = o if out is None else jnp.where(own == sub, o, out)
        o_ref[:, vsl] = out.astype(o_ref.dtype)


def _chunk_scan(x, reverse):
    tm = x.shape[0]
    rin = lax.broadcasted_iota(jnp.int32, x.shape, 0) % HG_CHUNK
    step = 1
    while step < HG_CHUNK:
        if reverse:
            x = x + jnp.where(rin < HG_CHUNK - step, pltpu.roll(x, tm - step, 0), 0.0)
        else:
            x = x + jnp.where(rin >= step, pltpu.roll(x, step, 0), 0.0)
        step *= 2
    return x


def _hgrn_kernel(*refs, has_init):
    fwd, bwd = refs[0:4], refs[4:8]
    refs = refs[8:]
    s0_ref = None
    if has_init:
        s0_ref = refs[0]
        refs = refs[1:]
    of_ref, ob_ref, sfin_ref, st_scr = refs
    i = pl.program_id(1)
    _hgrn_body([(fwd, bwd, of_ref, ob_ref, sfin_ref)], s0_ref, st_scr, i == 0, i == pl.num_programs(1) - 1)


def _hgrn_body(jobs, s0_ref, st_scr, first, last):
    tm = jobs[0][0][0].shape[0]
    c = HG_CHUNK
    nch = tm // c
    dk, dv = HG_DK, HG_DV
    hw = HG_HEADS * dk

    def initial(dr, hd):
        return s0_ref[0, dr, hd].T if s0_ref is not None else jnp.zeros((dv, dk), F32)

    if st_scr is not None:
        @pl.when(first)
        def _init():
            for dr in range(2):
                for hd in range(HG_HEADS):
                    st_scr[dr, hd] = initial(dr, hd)

    npair = nch // 2
    pair = 2 * c
    row = lax.broadcasted_iota(jnp.int32, (tm, tm), 0)
    col = lax.broadcasted_iota(jnp.int32, (tm, tm), 1)
    same = (row // c) == (col // c)
    same_pair = (row // pair) == (col // pair)
    bd = (lax.broadcasted_iota(jnp.int32, (tm, npair * dk), 0) // pair
          == lax.broadcasted_iota(jnp.int32, (tm, npair * dk), 1) // dk)
    chunk_odd = (lax.broadcasted_iota(jnp.int32, (tm, hw), 0) // c) % 2 == 1

    for dr in range(2):
        tri = same & ((col <= row) if dr == 0 else (col >= row))
        cross = same_pair & (((row // c) > (col // c)) if dr == 0 else ((row // c) < (col // c)))
        second = chunk_odd if dr == 0 else ~chunk_odd
        order = range(npair) if dr == 0 else range(npair - 1, -1, -1)

        def decayed(job):
            hq_ref, lf_ref, kk_ref, vv_ref = job[dr]
            bcum = _chunk_scan(lf_ref[...], reverse=dr == 1)
            closing = c - 1 if dr == 0 else 0
            btot3 = bcum.reshape(nch, c, hw)[:, closing:closing + 1, :]
            btot = jnp.broadcast_to(btot3, (nch, c, hw)).reshape(tm, hw)
            bpart = jnp.where(chunk_odd, pltpu.roll(btot, c, 0), pltpu.roll(btot, tm - c, 0))
            epart = jnp.exp(bpart)
            kk = kk_ref[...]
            qd = hq_ref[...] * jnp.exp(bcum)
            kd = kk * jnp.exp(-bcum)
            ke = kk * jnp.exp(btot - bcum)
            qd2 = jnp.where(second, qd * epart, qd)
            ke2 = jnp.where(second, ke, ke * epart)
            return qd, kd, ke, qd2, ke2, btot + bpart, vv_ref[...]

        def head(job, hd, qd, kd, ke, qd2, ke2, bpair, vv):
            o_ref, sfin_ref = job[2 + dr], job[4]
            sl = slice(hd * dk, (hd + 1) * dk)
            qd_h = qd[:, sl].astype(BF16)
            v_h = vv[:, hd * dv:(hd + 1) * dv]
            a = jnp.where(tri, _dot_nt(qd_h, kd[:, sl].astype(BF16)),
                          jnp.where(cross, _dot_nt(qd_h, ke[:, sl].astype(BF16)), 0.0))
            o_intra = _dot(a.astype(BF16), v_h.astype(BF16))
            kebd = jnp.where(bd, jnp.concatenate([ke2[:, sl]] * npair, axis=1), 0.0).astype(BF16)
            qbd = jnp.where(bd, jnp.concatenate([qd2[:, sl]] * npair, axis=1), 0.0).astype(BF16)
            ut = _dot(v_h.T.astype(BF16), kebd)
            st = st_scr[dr, hd] if st_scr is not None else initial(dr, hd)
            prev = [None] * npair
            for p in order:
                prev[p] = st
                st = st * jnp.exp(bpair[p * pair:p * pair + 1, sl]) + ut[:, p * dk:(p + 1) * dk]
            if st_scr is not None:
                st_scr[dr, hd] = st
            o_inter = _dot_nt(qbd, jnp.concatenate(prev, axis=1).astype(BF16))
            o_ref[:, hd * dv:(hd + 1) * dv] = o_intra + o_inter

            if last is True:
                sfin_ref[0, dr, hd] = st.T
            else:
                @pl.when(last)
                def _final():
                    sfin_ref[0, dr, hd] = st.T

        prepared = [decayed(job) for job in jobs]
        for hd in range(HG_HEADS):
            for job, arrays in zip(jobs, prepared):
                head(job, hd, *arrays)


def _hgrn(hgx, batch, seq, s0=None):
    n = hgx.shape[0]
    tm = TOK_BLOCK
    nblk = seq // tm
    hw = HG_HEADS * HG_DK

    def spec(lane_blk, rev):
        if rev:
            return pl.BlockSpec((tm, hw), lambda b, i: (b * nblk + nblk - 1 - i, lane_blk))
        return pl.BlockSpec((tm, hw), lambda b, i: (b * nblk + i, lane_blk))

    in_specs = [spec(0, False), spec(1, False), spec(2, False), spec(5, False),
                spec(0, True), spec(3, True), spec(4, True), spec(5, True)]
    ins = [hgx] * 8
    st_shape = (1, 2, HG_HEADS, HG_DK, HG_DV)
    st_spec = pl.BlockSpec(st_shape, lambda b, i: (b, 0, 0, 0, 0))
    if s0 is not None:
        ins.append(s0)
        in_specs.append(st_spec)
    return pl.pallas_call(
        functools.partial(_hgrn_kernel, has_init=s0 is not None),
        out_shape=[jax.ShapeDtypeStruct((n, hw), F32), jax.ShapeDtypeStruct((n, hw), F32),
                   jax.ShapeDtypeStruct((batch,) + st_shape[1:], F32)],
        grid=(batch, nblk),
        in_specs=in_specs,
        out_specs=[spec(0, False), spec(0, True), st_spec],
        scratch_shapes=[pltpu.VMEM((2, HG_HEADS, HG_DV, HG_DK), F32)],
        compiler_params=_params("arbitrary", "arbitrary"),
        name="hgrn",
    )(*ins)


def _layer_norm(x, g, b):
    xc = x - jnp.mean(x, axis=-1, keepdims=True)
    var = jnp.mean(xc * xc, axis=-1, keepdims=True)
    return xc * lax.rsqrt(var + EPS) * g + b


N_POSTMIX_WEIGHTS = 8
INPROJ_KEYS = ("win", "qn", "wuq", "kvn", "wk", "wv")
POSTMIX_KEYS = ("hgn", "womla", "wohg", "wout", "ln1g", "ln1b", "wr")
MIXER_SEQS = 2


def _postmix_kernel(x_ref, mod_ref, *refs, alpha, row0, per_batch, cached):
    n_attn = 5 if cached else 3
    q_ref, k_ref, v_ref = refs[:3]
    kc_ref, vc_ref = refs[3:5] if cached else (None, None)
    of_ref, ob_ref, zg_ref = refs[n_attn:n_attn + 3]
    om_s = refs[-1]
    _attn_body(q_ref, k_ref, v_ref, kc_ref, vc_ref, om_s)
    _postmix_body(x_ref, _mod_row(mod_ref, row0, per_batch), of_ref, ob_ref, zg_ref, om_s,
                  *refs[n_attn + 3:-1], alpha=alpha)


def _postmix_body(x_ref, m, of_ref, ob_ref, zg_ref, om_ref, wg_ref, hgn_ref, womla_ref,
                  wohg_ref, wout_ref, lng_ref, lnb_ref, wr_ref, x1_o, h2_o, aff_o, *, alpha):
    d = x_ref.shape[1]
    tb = aff_o.shape[2]
    g1, sh2, sc2 = m[:, 2 * d:3 * d], m[:, 3 * d:4 * d], m[:, 4 * d:5 * d]
    o = of_ref[...] + ob_ref[...]
    zg = zg_ref[...]
    parts = []
    for hd in range(HG_HEADS):
        sl = slice(hd * HG_DV, (hd + 1) * HG_DV)
        parts.append(_rms(o[:, sl], hgn_ref[...]) * _silu(zg[:, sl]))
    ohg = jnp.concatenate(parts, axis=1).astype(BF16)
    gates = _dot_nt(_modulated(x_ref, m), wg_ref[wg_ref.shape[0] - 2 * d:, :])
    merged = (jax.nn.sigmoid(gates[:, 0:d]) * _dot(om_ref[...], womla_ref[...])
              + jax.nn.sigmoid(gates[:, d:2 * d]) * _dot(ohg, wohg_ref[...]))
    mix = _dot(merged.astype(BF16), wout_ref[...])
    x1 = _layer_norm(alpha * x_ref[...] + g1 * mix, lng_ref[...], lnb_ref[...])
    x1_o[...] = x1
    h2 = (x1 * (1.0 + sc2) + sh2).astype(BF16)
    h2_o[...] = h2
    logits = _dot_nt(wr_ref[...], h2)
    e = jnp.exp(logits - jnp.max(logits, axis=0, keepdims=True))
    aff = e / jnp.sum(e, axis=0, keepdims=True)
    for blk in range(aff_o.shape[0]):
        aff_o[blk] = aff[:, blk * tb:(blk + 1) * tb]


def _postmix(x2d, batch, seq, mod, q, k, v, cache, o_f, o_b, hgx, wts, alpha, n_experts, row0, per_batch):
    n, d = x2d.shape
    tm = TOK_BLOCK
    nblk = seq // tm
    hw = HG_HEADS * HG_DV
    hp, hv = q.shape[1], v.shape[1]
    tok = lambda b, i: (b * nblk + i, 0)
    per_seq = lambda b, i: (b, 0)
    weights = [wts[k] for k in POSTMIX_KEYS]
    attn_ins = [q, k, v]
    attn_specs = [pl.BlockSpec((tm, hp), tok), pl.BlockSpec((seq, hp), per_seq), pl.BlockSpec((seq, hv), per_seq)]
    if cache is not None:
        past = cache[0].shape[0] // batch
        attn_ins += list(cache)
        attn_specs += [pl.BlockSpec((past, hp), per_seq), pl.BlockSpec((past, hv), per_seq)]
    return pl.pallas_call(
        functools.partial(_postmix_kernel, alpha=alpha, row0=row0, per_batch=per_batch,
                          cached=cache is not None),
        out_shape=[jax.ShapeDtypeStruct((n, d), F32), jax.ShapeDtypeStruct((n, d), BF16),
                   jax.ShapeDtypeStruct((n // tm, n_experts, tm), F32)],
        grid=(batch, nblk),
        in_specs=[pl.BlockSpec((tm, d), tok), _const_spec(mod.shape)] + attn_specs
                 + [pl.BlockSpec((tm, hw), tok), pl.BlockSpec((tm, hw), tok),
                    pl.BlockSpec((tm, hw), lambda b, i: (b * nblk + i, 6)), _const_spec(wts["win"].shape)]
                 + [_const_spec(w.shape) for w in weights],
        out_specs=[pl.BlockSpec((tm, d), tok), pl.BlockSpec((tm, d), tok),
                   pl.BlockSpec((1, n_experts, tm), lambda b, i: (b * nblk + i, 0, 0))],
        scratch_shapes=[pltpu.VMEM((tm, hv), BF16)],
        compiler_params=_params("arbitrary", "arbitrary"),
        name="postmix",
    )(x2d, mod, *attn_ins, o_f, o_b, hgx, wts["win"], *weights)


def _mixer_kernel(x_ref, mod_ref, *refs, alpha, row0, seq):
    nw = 1 + N_INPROJ_WEIGHTS
    in_w, refs = refs[:nw], refs[nw:]
    pm_w, refs = refs[:N_POSTMIX_WEIGHTS - 1], refs[N_POSTMIX_WEIGHTS - 1:]
    pm_w = (in_w[1],) + tuple(pm_w)
    x1_o, h2_o, aff_o, ckv_o, kpe_o, sfin_o, q_s, k_s, v_s, hgx_s, om_s, of_s, ob_s = refs
    m = _mod_row(mod_ref, row0, False)
    _inproj_body(x_ref, m, *in_w, q_s, k_s, v_s, ckv_o, kpe_o, hgx_s, rope=False)
    hw = HG_HEADS * HG_DK
    jobs = []
    for s in range(x_ref.shape[0] // seq):
        rows = slice(s * seq, (s + 1) * seq)
        _attn_body(q_s.at[rows], k_s.at[rows], v_s.at[rows], None, None, om_s.at[rows])
        lane = lambda j, rows=rows: hgx_s.at[rows, j * hw:(j + 1) * hw]
        jobs.append(((lane(0), lane(1), lane(2), lane(5)), (lane(0), lane(3), lane(4), lane(5)),
                     of_s.at[rows], ob_s.at[rows], sfin_o.at[s:s + 1]))
    _hgrn_body(jobs, None, None, True, True)
    _postmix_body(x_ref, m, of_s, ob_s, hgx_s.at[:, 6 * hw:7 * hw], om_s, *pm_w, x1_o, h2_o, aff_o,
                  alpha=alpha)


def _mixer(x2d, batch, seq, mod, gamma, wts, alpha, n_experts, row0):
    n, d = x2d.shape
    assert seq == TOK_BLOCK
    ns = MIXER_SEQS if batch % MIXER_SEQS == 0 else 1
    tm = ns * seq
    hp = MLA_HEADS * LANES
    hv = MLA_HEADS * MLA_V
    hw = HG_HEADS * HG_DK
    kvl = wts["kvn"].shape[1]
    weights = [wts[k] for k in INPROJ_KEYS + POSTMIX_KEYS]
    tok = lambda b: (b, 0)
    st_shape = (ns, 2, HG_HEADS, HG_DK, HG_DV)
    return pl.pallas_call(
        functools.partial(_mixer_kernel, alpha=alpha, row0=row0, seq=seq),
        out_shape=[jax.ShapeDtypeStruct((n, d), F32), jax.ShapeDtypeStruct((n, d), BF16),
                   jax.ShapeDtypeStruct((n // seq, n_experts, seq), F32),
                   jax.ShapeDtypeStruct((n, kvl), F32), jax.ShapeDtypeStruct((n, LANES), F32),
                   jax.ShapeDtypeStruct((batch,) + st_shape[1:], F32)],
        grid=(batch // ns,),
        in_specs=[pl.BlockSpec((tm, d), tok), _const_spec(mod.shape), _const_spec(gamma.shape)]
                 + [_const_spec(w.shape) for w in weights],
        out_specs=[pl.BlockSpec((tm, d), tok), pl.BlockSpec((tm, d), tok),
                   pl.BlockSpec((ns, n_experts, seq), lambda b: (b, 0, 0)),
                   pl.BlockSpec((tm, kvl), tok), pl.BlockSpec((tm, LANES), tok),
                   pl.BlockSpec(st_shape, lambda b: (b, 0, 0, 0, 0))],
        scratch_shapes=[pltpu.VMEM((tm, hp), BF16), pltpu.VMEM((tm, hp), BF16), pltpu.VMEM((tm, hv), BF16),
                        pltpu.VMEM((tm, 7 * hw), F32), pltpu.VMEM((tm, hv), BF16),
                        pltpu.VMEM((tm, hw), F32), pltpu.VMEM((tm, hw), F32)],
        compiler_params=_params("arbitrary"),
        name="mixer",
    )(x2d, mod, gamma, *weights)


def _route_kernel(*refs, caps):
    ng = len(caps)
    for aff_ref, rank_o, cnt_o, cap in zip(refs[0:ng], refs[ng:2 * ng], refs[2 * ng:3 * ng], caps):
        _route_group(aff_ref, rank_o, cnt_o, cap)


def _route_group(aff_ref, rank_o, cnt_o, cap):
    nb, ne, tb = aff_ref.shape
    key = aff_ref[...]

    def count(mask):
        return jnp.sum(jnp.sum(jnp.where(mask, 1.0, 0.0), axis=0), axis=1, keepdims=True)

    def bit_step(it, bits):
        cand = bits | jnp.left_shift(jnp.int32(1), 30 - it)
        return jnp.where(count(key >= pltpu.bitcast(cand, F32)[None]) >= cap, cand, bits)

    bits = lax.fori_loop(0, 31, bit_step, jnp.zeros((ne, 1), jnp.int32))
    thr = pltpu.bitcast(bits, F32)
    need = cap - count(key > thr[None])
    before = (lax.broadcasted_iota(jnp.int32, (tb, tb), 0)
              < lax.broadcasted_iota(jnp.int32, (tb, tb), 1))
    before = jnp.where(before, 1.0, 0.0).astype(BF16)
    off_eq = jnp.zeros((ne, 1), F32)
    off_sel = jnp.zeros((ne, 1), F32)
    cnt_o[...] = jnp.zeros_like(cnt_o)
    for blk in range(nb):
        key_b = key[blk]
        eq = key_b == thr
        eq_b = jnp.where(eq, 1.0, 0.0)
        eq_rank = _dot(eq_b.astype(BF16), before) + off_eq
        sel = (key_b > thr) | (eq & (eq_rank < need))
        sel_b = jnp.where(sel, 1.0, 0.0)
        rank = _dot(sel_b.astype(BF16), before) + off_sel
        rank_o[blk] = jnp.where(sel, rank.astype(jnp.int32), UNSELECTED)
        cnt_o[:, blk:blk + 1] = off_sel.astype(jnp.int32)
        off_eq = off_eq + jnp.sum(eq_b, axis=1, keepdims=True)
        off_sel = off_sel + jnp.sum(sel_b, axis=1, keepdims=True)
    cnt_o[:, nb:nb + 1] = off_sel.astype(jnp.int32)


def _route(affs, caps):
    ng = len(affs)
    ne = affs[0].shape[1]
    assert all(a.shape[0] + 1 <= LANES for a in affs)
    outs = pl.pallas_call(
        functools.partial(_route_kernel, caps=tuple(caps)),
        out_shape=[jax.ShapeDtypeStruct(a.shape, jnp.int32) for a in affs]
                  + [jax.ShapeDtypeStruct((ne, LANES), jnp.int32)] * ng,
        in_specs=[pl.BlockSpec(memory_space=pltpu.VMEM)] * ng,
        out_specs=[pl.BlockSpec(memory_space=pltpu.VMEM)] * (2 * ng),
        compiler_params=pltpu.CompilerParams(vmem_limit_bytes=VMEM_LIMIT),
        name="route",
    )(*affs)
    return [(outs[gi], outs[ng + gi][:, :affs[gi].shape[0] + 1]) for gi in range(ng)]


def _window_hits(rk_ref, firsts, slot0, win):
    ne, tb = rk_ref.shape[1], rk_ref.shape[2]
    win_iota = lax.broadcasted_iota(jnp.int32, (win, tb), 0)
    return [(rk_ref[0, e:e + 1, :] + (slot0 - firsts[e])) == win_iota for e in range(ne)]


def _compact_kernel(first_ref, end_ref, rounds_ref, *refs, groups, slots):
    ng = len(groups)
    h2_refs, rk_refs, af_refs = refs[0:ng], refs[ng:2 * ng], refs[2 * ng:3 * ng]
    xe_hbm, stage, tail, sem, issued = refs[3 * ng:]
    hbms, stages, tails = [xe_hbm], [stage], [tail]
    b = pl.program_id(0)
    ne = rk_refs[0].shape[1]
    win = WIN_ROWS
    sub = BF16_ROWS

    def copies(slot, dsts):
        return [pltpu.make_async_copy(stage.at[slot, pl.ds(e * win, win), :],
                                      hbm.at[e, pl.ds(pl.multiple_of(dsts[e], sub), win), :], sem.at[e])
                for stage, hbm in zip(stages, hbms) for e in range(ne)]

    def wait_previous():
        @pl.when(issued[0] > 0)
        def _():
            for cp in copies(0, [0] * ne):
                cp.wait()

    @pl.when(b == 0)
    def _init():
        issued[0] = 0
        for stage, tail in zip(stages, tails):
            tail[...] = jnp.zeros_like(tail)
            stage[1] = jnp.zeros(stage.shape[1:], stage.dtype)
        pad = copies(1, [slots] * ne)
        for cp in pad:
            cp.start()
        for cp in pad:
            cp.wait()

    def group_body(h2_ref, rk_ref, af_ref, slot0):
        firsts = [first_ref[b * ne + e] for e in range(ne)]
        bases = [(f // sub) * sub for f in firsts]
        ends = [end_ref[b * ne + e] - bases[e] for e in range(ne)]

        def one_round(r, carry):
            dsts = [bases[e] + r * win for e in range(ne)]
            hits = _window_hits(rk_ref, dsts, slot0, win)
            onehot = jnp.where(jnp.concatenate(hits, axis=0), 1.0, 0.0).astype(BF16)
            gate = jnp.concatenate(
                [jnp.sum(jnp.where(hits[e], af_ref[0, e:e + 1, :], 0.0), axis=1, keepdims=True)
                 for e in range(ne)], axis=0)
            slot = issued[0] % 2
            gate = jnp.broadcast_to(gate, (ne * win, LANES))
            g1 = gate.astype(BF16).astype(F32)
            g2 = (gate - g1).astype(BF16).astype(F32)
            g3 = (gate - g1 - g2).astype(BF16).astype(F32)
            lane = lax.broadcasted_iota(jnp.int32, (ne * win, LANES), 1)
            pieces3 = jnp.where(lane == 0, g1, jnp.where(lane == 1, g2, jnp.where(lane == 2, g3, 0.0)))
            streams = (jnp.concatenate([_dot(onehot, h2_ref[...]).astype(BF16),
                                        pieces3.astype(BF16)], axis=1),)
            for val, stage, tail in zip(streams, stages, tails):
                sub_iota = lax.broadcasted_iota(jnp.int32, (sub, val.shape[1]), 0)
                pieces = []
                for e in range(ne):
                    old = tail[e * sub:(e + 1) * sub, :]
                    shared = jnp.where(r == 0, firsts[e] - bases[e], 0)
                    groups_e = [val[e * win + g * sub:e * win + (g + 1) * sub, :] for g in range(win // sub)]
                    groups_e[0] = jnp.where(sub_iota < shared, old, groups_e[0])
                    pieces += groups_e
                    last = (ends[e] // sub) * sub
                    new = old
                    for g, grp in enumerate(groups_e):
                        new = jnp.where((r == last // win) & (last % win == g * sub), grp, new)
                    tail[e * sub:(e + 1) * sub, :] = new
                stage[slot] = jnp.concatenate(pieces, axis=0)
            wait_previous()
            for e, cp in enumerate(copies(slot, [jnp.minimum(dst, slots) for dst in dsts])):
                cp.start(priority=e % 2)
            issued[0] = issued[0] + 1
            return carry

        lax.fori_loop(0, rounds_ref[b], one_round, 0)

    blk0 = 0
    for gi, g in enumerate(groups):
        @pl.when((b >= blk0) & (b < blk0 + g["nb"]))
        def _(gi=gi, g=g):
            group_body(h2_refs[gi], rk_refs[gi], af_refs[gi], g["slot0"])
        blk0 += g["nb"]

    @pl.when(b == pl.num_programs(0) - 1)
    def _drain():
        wait_previous()


def _compact(groups, first, end, rounds, slots):
    d = groups[0]["h2"].shape[1]
    nbs = [g["rank"].shape[0] for g in groups]
    ne, tb = groups[0]["rank"].shape[1:]
    meta, specs_h2, specs_rk = [], [], []
    blk0 = 0
    for g, nb in zip(groups, nbs):
        meta.append(dict(nb=nb, slot0=g["slot0"]))
        local = lambda b, *_, blk0=blk0, nb=nb: jnp.clip(b - blk0, 0, nb - 1)
        specs_h2.append(pl.BlockSpec((tb, d), lambda b, *_, local=local: (local(b), 0)))
        specs_rk.append(pl.BlockSpec((1, ne, tb), lambda b, *_, local=local: (local(b), 0, 0)))
        blk0 += nb
    width = d + LANES
    return pl.pallas_call(
        functools.partial(_compact_kernel, groups=meta, slots=slots),
        out_shape=jax.ShapeDtypeStruct((ne, slots + WIN_ROWS, width), BF16),
        grid_spec=pltpu.PrefetchScalarGridSpec(
            num_scalar_prefetch=3,
            grid=(sum(nbs),),
            in_specs=specs_h2 + specs_rk + specs_rk,
            out_specs=pl.BlockSpec(memory_space=pl.ANY),
            scratch_shapes=[pltpu.VMEM((2, ne * WIN_ROWS, width), BF16),
                            pltpu.VMEM((ne * BF16_ROWS, width), BF16),
                            pltpu.SemaphoreType.DMA((ne,)), pltpu.SMEM((1,), jnp.int32)]),
        compiler_params=_params("arbitrary"),
        name="compact",
    )(first, end, rounds, *[g["h2"] for g in groups], *[g["rank"] for g in groups],
      *[g["aff"] for g in groups])


def _ffn_kernel(xe_ref, w1_ref, w3_ref, w2_ref, ye_ref, *scratch):
    f = pl.program_id(1)
    d = w1_ref.shape[1]
    x = xe_ref[0, :, 0:d]
    pieces = xe_ref[0, :, d:d + LANES].astype(F32)
    gate = pieces[:, 0:1] + pieces[:, 1:2] + pieces[:, 2:3]
    hid = _silu(_dot(x, w1_ref[0].astype(BF16))) * _dot(x, w3_ref[0].astype(BF16))
    y = _dot(hid.astype(BF16), w2_ref[0].astype(BF16))
    if not scratch:
        ye_ref[0] = (y * gate).astype(ye_ref.dtype)
        return
    acc_scr, = scratch
    last = pl.num_programs(1) - 1

    @pl.when(f == 0)
    def _first():
        acc_scr[...] = y

    @pl.when((f > 0) & (f < last))
    def _middle():
        acc_scr[...] += y

    @pl.when(f == last)
    def _last():
        ye_ref[0] = ((acc_scr[...] + y) * gate).astype(ye_ref.dtype)


def _ffn(xe, w1, w3, w2, slots, ft):
    ne, d, dff = w1.shape
    nf = dff // ft
    return pl.pallas_call(
        _ffn_kernel,
        out_shape=jax.ShapeDtypeStruct((ne, slots, d), BF16),
        grid=(ne, nf),
        in_specs=[pl.BlockSpec((1, slots, xe.shape[2]), lambda e, f: (e, 0, 0)),
                  pl.BlockSpec((1, d, ft), lambda e, f: (e, 0, f)),
                  pl.BlockSpec((1, d, ft), lambda e, f: (e, 0, f)),
                  pl.BlockSpec((1, ft, d), lambda e, f: (e, f, 0))],
        out_specs=pl.BlockSpec((1, slots, d), lambda e, f: (e, 0, 0)),
        scratch_shapes=[pltpu.VMEM((slots, d), F32)] if nf > 1 else [],
        compiler_params=_params("arbitrary", "arbitrary"),
        name="ffn",
    )(xe, w1, w3, w2)


def _combine_kernel(first_ref, rounds_ref, *refs, groups, alpha, slots):
    ng = len(groups)
    rk_refs, x1_refs = refs[0:ng], refs[ng:2 * ng]
    mod_ref, lng_ref, lnb_ref, ye_hbm = refs[2 * ng:2 * ng + 4]
    out_refs = refs[2 * ng + 4:3 * ng + 4]
    buf, acc_scr, sem = refs[3 * ng + 4:]
    d = x1_refs[0].shape[1]
    b = pl.program_id(0)
    nblk = pl.num_programs(0)
    ne, tb = rk_refs[0].shape[1], rk_refs[0].shape[2]
    win = WIN_ROWS

    def starts_of(blk, r):
        firsts = [(first_ref[blk * ne + e] // BF16_ROWS) * BF16_ROWS + r * win for e in range(ne)]
        return firsts, [jnp.minimum(f, slots - win) for f in firsts]

    def windows(slot, starts):
        return [pltpu.make_async_copy(ye_hbm.at[e, pl.ds(pl.multiple_of(starts[e], BF16_ROWS), win), :],
                                      buf.at[slot, pl.ds(e * win, win), :], sem.at[slot, e])
                for e in range(ne)]

    def scatter(rk_ref, slot0, slot, firsts, starts, later_round):
        hits = _window_hits(rk_ref, starts, slot0, win)
        if later_round:
            hits = [h & ((rk_ref[0, e:e + 1, :] + slot0) >= firsts[e]) for e, h in enumerate(hits)]
        hit = jnp.where(jnp.concatenate(hits, axis=0), 1.0, 0.0)
        return _dot(hit.T.astype(BF16), buf[slot])

    cur = b % 2

    @pl.when(b == 0)
    def _prime():
        for cp in windows(0, starts_of(0, 0)[1]):
            cp.start(priority=1)

    @pl.when(b + 1 < nblk)
    def _prefetch():
        for cp in windows(1 - cur, starts_of(b + 1, 0)[1]):
            cp.start(priority=1)

    firsts0, starts0 = starts_of(b, 0)
    for cp in windows(cur, starts0):
        cp.wait()

    def group_body(rk_ref, x1_ref, out_ref, g, local):
        acc_scr[...] = scatter(rk_ref, g["slot0"], cur, firsts0, starts0, False)

        def extra_round(r, carry):
            firsts, starts = starts_of(b, r)
            for cp in windows(2, starts):
                cp.start()
            for cp in windows(2, starts):
                cp.wait()
            acc_scr[...] += scatter(rk_ref, g["slot0"], 2, firsts, starts, True)
            return carry

        lax.fori_loop(1, rounds_ref[b], extra_round, 0)
        r = g["row0"] + local // g["blocks_per_batch"]
        g2 = mod_ref[pl.ds(r, 1), :][:, 5 * d:6 * d]
        out_ref[...] = _layer_norm(alpha * x1_ref[...] + g2 * acc_scr[...], lng_ref[...], lnb_ref[...])

    blk0 = 0
    for gi, g in enumerate(groups):
        @pl.when((b >= blk0) & (b < blk0 + g["nb"]))
        def _(gi=gi, g=g, blk0=blk0):
            group_body(rk_refs[gi], x1_refs[gi], out_refs[gi], g, b - blk0)
        blk0 += g["nb"]


def _combine(ye, groups, first, rounds, mod, ln_g, ln_b, alpha):
    d = groups[0]["x1"].shape[1]
    ne, tb = groups[0]["rank"].shape[1:]
    slots = ye.shape[1]
    meta, specs_rk, specs_x1 = [], [], []
    blk0 = 0
    for g in groups:
        nb = g["rank"].shape[0]
        meta.append(dict(nb=nb, slot0=g["slot0"], row0=g["row0"], blocks_per_batch=g["blocks_per_batch"]))
        local = lambda b, *_, blk0=blk0, nb=nb: jnp.clip(b - blk0, 0, nb - 1)
        specs_rk.append(pl.BlockSpec((1, ne, tb), lambda b, *_, local=local: (local(b), 0, 0)))
        specs_x1.append(pl.BlockSpec((tb, d), lambda b, *_, local=local: (local(b), 0)))
        blk0 += nb
    const = lambda shape: pl.BlockSpec(shape, lambda b, *_: (0,) * len(shape))
    return pl.pallas_call(
        functools.partial(_combine_kernel, groups=meta, alpha=alpha, slots=slots),
        out_shape=[jax.ShapeDtypeStruct(g["x1"].shape, F32) for g in groups],
        grid_spec=pltpu.PrefetchScalarGridSpec(
            num_scalar_prefetch=2,
            grid=(blk0,),
            in_specs=specs_rk + specs_x1 + [const(mod.shape), const((1, d)), const((1, d)),
                                            pl.BlockSpec(memory_space=pl.ANY)],
            out_specs=specs_x1,
            scratch_shapes=[pltpu.VMEM((3, ne * WIN_ROWS, d), ye.dtype), pltpu.VMEM((tb, d), F32),
                            pltpu.SemaphoreType.DMA((3, ne))]),
        compiler_params=_params("arbitrary"),
        name="combine",
    )(first, rounds, *[g["rank"] for g in groups], *[g["x1"] for g in groups], mod, ln_g, ln_b, ye)


def _prep_weights(w_in, q_norm, w_uq, kv_norm, w_ukv, w_o_mla, hgrn_norm, w_o_hg, w_out, ln1_g, ln1_b,
                  w_router):
    d = w_in.shape[0]
    q_lora, kv_lora = q_norm.shape[0], kv_norm.shape[0]
    hw = HG_HEADS * HG_DK
    hh, hp = MLA_HEADS, MLA_HEADS * LANES
    o_kv, o_pe = q_lora, q_lora + kv_lora
    o_h = o_pe + MLA_ROPE
    o_g = o_h + 5 * hw
    assert w_in.shape[1] == o_g + 2 * d
    qk = MLA_NOPE + MLA_ROPE
    kvw = MLA_NOPE + MLA_V
    b16 = lambda a: a.astype(BF16)
    assert all(o % BF16_ROWS == 0 for o in (o_kv, o_pe, o_h, o_g))
    win = b16(w_in.T)
    wuq = jnp.pad(w_uq.reshape(q_lora, hh, qk), ((0, 0), (0, 0), (0, LANES - qk))).reshape(q_lora, hp)
    ukv = w_ukv.reshape(kv_lora, hh, kvw)
    wk = jnp.pad(ukv[:, :, :MLA_NOPE], ((0, 0), (0, 0), (0, LANES - MLA_NOPE))).reshape(kv_lora, hp)
    wv = ukv[:, :, MLA_NOPE:].reshape(kv_lora, hh * MLA_V)
    return dict(
        win=win, qn=q_norm.reshape(1, -1), wuq=b16(wuq), kvn=kv_norm.reshape(1, -1), wk=b16(wk), wv=b16(wv),
        hgn=hgrn_norm.reshape(1, -1), womla=b16(w_o_mla), wohg=b16(w_o_hg), wout=b16(w_out),
        ln1g=ln1_g.reshape(1, -1), ln1b=ln1_b.reshape(1, -1), wr=b16(w_router.T))


def _rope_tables(seq):
    n_freq = MLA_ROPE // 4
    inv = ROPE_BASE ** (-np.arange(n_freq, dtype=np.float64) / n_freq)
    t = np.arange(seq)
    ang = np.concatenate([(t // GRID_W)[:, None] * inv, (t % GRID_W)[:, None] * inv], axis=-1)
    cos = np.repeat(np.cos(ang), 2, axis=1)
    sin = np.repeat(np.sin(ang), 2, axis=1) * np.tile([-1.0, 1.0], MLA_ROPE // 2)
    ck = np.pad(cos, ((0, 0), (0, LANES - MLA_ROPE)), constant_values=1.0)
    sk = np.pad(sin, ((0, 0), (0, LANES - MLA_ROPE)))
    cq = np.pad(cos, ((0, 0), (MLA_NOPE, LANES - MLA_NOPE - MLA_ROPE)), constant_values=1.0)
    sq = np.pad(sin, ((0, 0), (MLA_NOPE, LANES - MLA_NOPE - MLA_ROPE)))
    return tuple(jnp.asarray(a, F32) for a in (cq, sq, ck, sk))


def _window_sched(cnt, slot0):
    first = slot0 + cnt[:, :-1]
    end = slot0 + cnt[:, 1:]
    flat = lambda a: a.T.reshape(-1).astype(jnp.int32)
    rounds = jnp.max((end - (first // BF16_ROWS) * BF16_ROWS + WIN_ROWS - 1) // WIN_ROWS, axis=0)
    return flat(first), flat(end), jnp.maximum(rounds, 1).astype(jnp.int32)


def kernel(x_prompt, x_sample, c, cache_ckv, cache_kpe, state_hgrn, c_ctx, w_ada, b_ada, w_in, mla_q_norm, mla_w_uq, mla_kv_norm, mla_w_ukv, mla_w_o, hgrn_gamma, hgrn_norm, hgrn_w_o, w_out, ln1_g, ln1_b, moe_w_router, moe_w1, moe_w3, moe_w2, ln2_g, ln2_b):
    depth = w_ada.shape[0]
    assert depth == 1, "single trunk layer"
    bp, tp, d = x_prompt.shape
    bs, tsq, _ = x_sample.shape
    ne = moe_w_router.shape[-1]
    alpha = (2 * depth) ** 0.25
    past = cache_ckv.shape[2]
    assert tp % TOK_BLOCK == 0 and tsq % TOK_BLOCK == 0 and past % TOK_BLOCK == 0 and tsq % GRID_W == 0

    wts = _prep_weights(w_in[0], mla_q_norm[0], mla_w_uq[0], mla_kv_norm[0], mla_w_ukv[0], mla_w_o[0],
                        hgrn_norm[0], hgrn_w_o[0], w_out[0], ln1_g[0], ln1_b[0], moe_w_router[0])
    cond_rows = -(-(1 + bs) // SUBLANES) * SUBLANES
    cond = jnp.concatenate([c_ctx[None], c, jnp.zeros((cond_rows - 1 - bs, d), F32)], axis=0)
    mod = _adaln(cond, w_ada[0], b_ada[0])

    xs = [x_prompt.reshape(bp * tp, d), x_sample.reshape(bs * tsq, d)]
    dims = [(bp, tp), (bs, tsq)]
    rows = [(0, False), (1, True)]
    ropes = [None, _rope_tables(tsq)]
    kpe_c = jnp.pad(cache_kpe[:, 0].reshape(bs * past, MLA_ROPE), ((0, 0), (0, LANES - MLA_ROPE)))
    caches = [None, _kvup(cache_ckv[:, 0].reshape(bs * past, -1), kpe_c, wts)]
    inits = [None, state_hgrn[:, 0]]

    x1s, h2s, affs, extras = [], [], [], []
    for gi in range(2):
        (bt, sq), (row0, per_batch) = dims[gi], rows[gi]
        if sq == TOK_BLOCK and caches[gi] is None and ropes[gi] is None and not per_batch:
            x1, h2, aff, ckv, kpe, s_fin = _mixer(xs[gi], bt, sq, mod, hgrn_gamma, wts, alpha, ne, row0)
        else:
            q, k, v, ckv, kpe, hgx = _inproj(xs[gi], bt, sq, mod, hgrn_gamma, wts, row0, per_batch, ropes[gi])
            o_f, o_b, s_fin = _hgrn(hgx, bt, sq, inits[gi])
            x1, h2, aff = _postmix(xs[gi], bt, sq, mod, q, k, v, caches[gi], o_f, o_b, hgx, wts, alpha, ne,
                                   row0, per_batch)
        x1s.append(x1)
        h2s.append(h2)
        affs.append(aff)
        extras.append((ckv, kpe, s_fin))

    caps = [EC_FACTOR * x.shape[0] // ne for x in xs]
    slots = sum(caps)
    assert all(cp % BF16_ROWS == 0 for cp in caps) and slots >= WIN_ROWS
    groups, scheds = [], []
    slot0 = 0
    routed = _route(affs, caps)
    for gi in range(2):
        rank, cnt = routed[gi]
        groups.append(dict(h2=h2s[gi], rank=rank, aff=affs[gi], slot0=slot0, x1=x1s[gi], row0=rows[gi][0],
                           blocks_per_batch=dims[gi][1] // TOK_BLOCK if rows[gi][1] else 1 << 30))
        scheds.append(_window_sched(cnt, slot0))
        slot0 += caps[gi]
    first, end, rounds = [jnp.concatenate([s[k] for s in scheds]) for k in range(3)]
    xe = _compact(groups, first, end, rounds, slots)
    ye = _ffn(xe, moe_w1[0], moe_w3[0], moe_w2[0], slots, ft=moe_w1.shape[-1])
    outs = _combine(ye, groups, first, rounds, mod, ln2_g[0].reshape(1, -1), ln2_b[0].reshape(1, -1), alpha)

    ckv_p, kpe_p, st_p = extras[0]
    y_prompt = outs[0].reshape(bp, tp, d)
    y_sample = outs[1].reshape(bs, tsq, d)
    new_ckv = ckv_p.reshape(bp, 1, tp, -1)
    new_kpe = kpe_p[:, :MLA_ROPE].reshape(bp, 1, tp, MLA_ROPE)
    new_state = st_p.reshape(bp, 1, 2, HG_HEADS, HG_DK, HG_DV)
    return (y_prompt, y_sample, new_ckv, new_kpe, new_state)
```
